```python
import math
import jax
import jax.numpy as jnp
from jax import lax
import numpy as np

D_MODEL = 1024
BATCH = 8
SEQ = 2048
DEPTH = 4

HEAD_DIM = 64
NSA_HEADS = 8
NSA_KV_HEADS = 2
NSA_GROUP = NSA_HEADS // NSA_KV_HEADS
CMP_BLOCK = 32
CMP_STRIDE = 16
CMP_HIDDEN = 128
SEL_BLOCK = 64
N_SELECT = 16
N_LOCAL_SEL = 2
WINDOW = 512
NSA_Q_BLOCK = 64
FOX_HEADS = 8
Q_BLOCK = 128
REL_BUCKETS = 32
REL_MAX_DIST = 128
D_FF_DENSE = 2816
N_EXPERTS = 8
TOP_K = 2
D_FF_EXPERT = 3584
DN_ALPHA = (2 * DEPTH) ** 0.25
DN_BETA = (8 * DEPTH) ** -0.25
LN_EPS = 1e-5
FORCED_SCORE = 1e4
NEG_INF = -1e30

NSA_Q_W = NSA_HEADS * HEAD_DIM
NSA_KV_W = NSA_KV_HEADS * HEAD_DIM
FOX_W = FOX_HEADS * HEAD_DIM
IN_SPLITS = (NSA_Q_W, 6 * NSA_KV_W, 3 * NSA_HEADS, 3 * FOX_W, FOX_HEADS, 2 * D_MODEL)
D_IN_PROJ = NSA_Q_W + 6 * NSA_KV_W + 3 * NSA_HEADS + 3 * FOX_W + FOX_HEADS + 2 * D_MODEL
N_DENSE_LAYERS = (DEPTH + 1) // 2
N_MOE_LAYERS = DEPTH // 2

kernel_name = 'hybrid_nsa_fox_moe_deepnorm'


def layer_norm(x, g, b):
    xf = x.astype(jnp.float32)
    mu = xf.mean(-1, keepdims=True)
    var = jnp.square(xf - mu).mean(-1, keepdims=True)
    return ((xf - mu) * lax.rsqrt(var + LN_EPS) * g + b).astype(x.dtype)


def masked_softmax(s, mask, axis):
    s = jnp.where(mask, s.astype(jnp.float32), NEG_INF)
    s = s - s.max(axis=axis, keepdims=True)
    e = jnp.where(mask, jnp.exp(s), 0.0)
    return e / jnp.maximum(e.sum(axis=axis, keepdims=True), 1e-30)


def t5_bucket(dist):
    n = jnp.maximum(dist, 0)
    max_exact = REL_BUCKETS // 2
    large = max_exact + (jnp.log(jnp.maximum(n, 1).astype(jnp.float32) / max_exact)
                         / math.log(REL_MAX_DIST / max_exact) * (REL_BUCKETS - max_exact)).astype(jnp.int32)
    large = jnp.minimum(large, REL_BUCKETS - 1)
    return jnp.where(n < max_exact, n, large)


def compress_blocks(kv, pe, w1, w2):
    B, S, Hkv, dk = kv.shape
    r = CMP_BLOCK // CMP_STRIDE
    n_half = S // CMP_STRIDE
    n_cmp = n_half - r + 1
    h = kv.reshape(B, n_half, CMP_STRIDE, Hkv, dk)
    blocks = jnp.concatenate([h[:, j:j + n_cmp] for j in range(r)], axis=2)
    blocks = blocks + pe[None, None, :, None, :]
    blocks = blocks.transpose(0, 1, 3, 2, 4).reshape(B, n_cmp, Hkv, CMP_BLOCK * dk)
    return jax.nn.gelu(blocks @ w1) @ w2


def cmp_to_sel_overlap(n_cmp, n_sel):
    c0 = np.arange(n_cmp)[:, None] * CMP_STRIDE
    s0 = np.arange(n_sel)[None, :] * SEL_BLOCK
    ov = np.maximum(np.minimum(c0 + CMP_BLOCK, s0 + SEL_BLOCK) - np.maximum(c0, s0), 0)
    return jnp.asarray(ov / CMP_BLOCK, dtype=jnp.float32)


def nsa_attention(q, k_c, v_c, k_s, v_s, k_w, v_w, gates, rel_bias):
    B, S, Hkv, G, dk = q.shape
    n_cmp = k_c.shape[1]
    n_sel = S // SEL_BLOCK
    top_n = min(N_SELECT, n_sel)
    n_qb = S // NSA_Q_BLOCK
    scale = dk ** -0.5
    tab = rel_bias.astype(jnp.float32).reshape(REL_BUCKETS, Hkv, G)
    tab_h = tab.transpose(1, 0, 2)
    cmp_end = jnp.arange(n_cmp) * CMP_STRIDE + CMP_BLOCK - 1
    overlap = cmp_to_sel_overlap(n_cmp, n_sel)
    blk = jnp.arange(n_sel)
    k_sb = k_s.reshape(B, n_sel, SEL_BLOCK, Hkv, dk).transpose(0, 3, 1, 2, 4)
    v_sb = v_s.reshape(B, n_sel, SEL_BLOCK, Hkv, dk).transpose(0, 3, 1, 2, 4)
    b_idx = jnp.arange(B)[:, None, None, None]
    h_idx = jnp.arange(Hkv)[None, :, None, None]
    pad = ((0, 0), (WINDOW, 0), (0, 0), (0, 0))
    k_wp = jnp.pad(k_w, pad)
    v_wp = jnp.pad(v_w, pad)
    qi = jnp.arange(NSA_Q_BLOCK)
    kj = jnp.arange(NSA_Q_BLOCK + WINDOW)
    rel_w = qi[:, None] + WINDOW - kj[None, :]
    band = (rel_w >= 0) & (rel_w < WINDOW)
    bias_w = tab[t5_bucket(rel_w)].transpose(2, 3, 0, 1)

    def block(args):
        c, qb, gb = args
        t = c * NSA_Q_BLOCK + qi
        s_c = jnp.einsum('bqhgd,bchd->bhgqc', qb, k_c) * scale
        vis_c = cmp_end[None, :] <= t[:, None]
        bias_c = tab[t5_bucket(t[:, None] - cmp_end[None, :])].transpose(2, 3, 0, 1)
        p_c = masked_softmax(s_c + bias_c, vis_c, -1)
        o_c = jnp.einsum('bhgqc,bchd->bqhgd', p_c.astype(v_c.dtype), v_c)
        imp = jnp.einsum('bhgqc,cj->bhqj', p_c, overlap)
        cur = (t // SEL_BLOCK)[:, None]
        forced = (blk[None, :] == 0) | ((blk[None, :] <= cur) & (blk[None, :] > cur - N_LOCAL_SEL))
        imp = jnp.where(forced, FORCED_SCORE, jnp.where(blk[None, :] <= cur, imp, -1.0))
        _, idx = lax.top_k(imp, top_n)
        k_g = k_sb[b_idx, h_idx, idx]
        v_g = v_sb[b_idx, h_idx, idx]
        pos = idx[..., None] * SEL_BLOCK + jnp.arange(SEL_BLOCK)
        rel_s = t[:, None, None] - pos
        vis_s = (rel_s >= 0)[:, :, None]
        bias_s = tab_h[h_idx[..., None], t5_bucket(rel_s)].transpose(0, 1, 5, 2, 3, 4)
        s_s = jnp.einsum('bqhgd,bhqnld->bhgqnl', qb, k_g) * scale + bias_s
        p_s = masked_softmax(s_s, vis_s, (-2, -1))
        o_s = jnp.einsum('bhgqnl,bhqnld->bqhgd', p_s.astype(v_g.dtype), v_g)
        k_wb = lax.dynamic_slice_in_dim(k_wp, c * NSA_Q_BLOCK, NSA_Q_BLOCK + WINDOW, axis=1)
        v_wb = lax.dynamic_slice_in_dim(v_wp, c * NSA_Q_BLOCK, NSA_Q_BLOCK + WINDOW, axis=1)
        s_w = jnp.einsum('bqhgd,bkhd->bhgqk', qb, k_wb) * scale + bias_w
        vis_w = band & ((c * NSA_Q_BLOCK - WINDOW + kj) >= 0)[None, :]
        p_w = masked_softmax(s_w, vis_w, -1)
        o_w = jnp.einsum('bhgqk,bkhd->bqhgd', p_w.astype(v_wb.dtype), v_wb)
        return gb[..., 0:1] * o_c + gb[..., 1:2] * o_s + gb[..., 2:3] * o_w

    q_blocks = q.reshape(B, n_qb, NSA_Q_BLOCK, Hkv, G, dk).transpose(1, 0, 2, 3, 4, 5)
    g_blocks = gates.reshape(B, n_qb, NSA_Q_BLOCK, Hkv, G, 3).transpose(1, 0, 2, 3, 4, 5)
    out = lax.map(block, (jnp.arange(n_qb), q_blocks, g_blocks))
    return out.transpose(1, 0, 2, 3, 4, 5).reshape(B, S, Hkv * G * dk)


def forgetting_attention(q, k, v, log_f):
    B, S, H, dk = q.shape
    n_qb = S // Q_BLOCK
    scale = dk ** -0.5
    F = jnp.cumsum(log_f, axis=1)
    F_k = F.transpose(0, 2, 1)
    key_pos = jnp.arange(S)

    def block(args):
        c, qb, Fq = args
        t = c * Q_BLOCK + jnp.arange(Q_BLOCK)
        s = jnp.einsum('bqhd,bkhd->bhqk', qb, k).astype(jnp.float32) * scale
        s = s + (Fq.transpose(0, 2, 1)[..., None] - F_k[:, :, None, :])
        p = masked_softmax(s, key_pos[None, :] <= t[:, None], -1)
        return jnp.einsum('bhqk,bkhd->bqhd', p.astype(v.dtype), v)

    q_blocks = q.reshape(B, n_qb, Q_BLOCK, H, dk).transpose(1, 0, 2, 3, 4)
    f_blocks = F.reshape(B, n_qb, Q_BLOCK, H).transpose(1, 0, 2, 3)
    out = lax.map(block, (jnp.arange(n_qb), q_blocks, f_blocks))
    return out.transpose(1, 0, 2, 3, 4).reshape(B, S, H * dk)


def hybrid_mixer(x, w_in, cmp_pe, cmp_w1, cmp_w2, f_bias, w_nsa_br, w_fox_br, w_o, rel_bias):
    B, S, _ = x.shape
    proj = x @ w_in
    offsets = np.cumsum(IN_SPLITS)[:-1].tolist()
    nsa_q, nsa_kv, nsa_g, fox_qkv, fox_f, merge_g = jnp.split(proj, offsets, axis=-1)
    q = nsa_q.reshape(B, S, NSA_KV_HEADS, NSA_GROUP, HEAD_DIM)
    k_c, v_c, k_s, v_s, k_w, v_w = [a.reshape(B, S, NSA_KV_HEADS, HEAD_DIM)
                                    for a in jnp.split(nsa_kv, 6, axis=-1)]
    k_c = compress_blocks(k_c, cmp_pe[0], cmp_w1[0], cmp_w2[0])
    v_c = compress_blocks(v_c, cmp_pe[1], cmp_w1[1], cmp_w2[1])
    gates = jax.nn.sigmoid(nsa_g).reshape(B, S, NSA_KV_HEADS, NSA_GROUP, 3)
    y_nsa = nsa_attention(q, k_c, v_c, k_s, v_s, k_w, v_w, gates, rel_bias)
    fq, fk, fv = [a.reshape(B, S, FOX_HEADS, HEAD_DIM) for a in jnp.split(fox_qkv, 3, axis=-1)]
    log_f = jax.nn.log_sigmoid(fox_f.astype(jnp.float32) + f_bias.astype(jnp.float32))
    y_fox = forgetting_attention(fq, fk, fv, log_f)
    g_nsa, g_fox = jnp.split(jax.nn.sigmoid(merge_g), 2, axis=-1)
    merged = g_nsa * (y_nsa @ w_nsa_br) + g_fox * (y_fox @ w_fox_br)
    return merged @ w_o


def swiglu(x, w_gate, w_up, w_down):
    return (jax.nn.silu(x @ w_gate) * (x @ w_up)) @ w_down


def moe_swiglu(x, router, w_gate, w_up, w_down):
    logits = (x @ router).astype(jnp.float32)
    top_val, top_idx = lax.top_k(logits, TOP_K)
    top_w = jax.nn.softmax(top_val, axis=-1)
    gate = jnp.sum(jax.nn.one_hot(top_idx, N_EXPERTS, dtype=jnp.float32) * top_w[..., None], axis=-2)
    gate = gate.astype(x.dtype)
    out = jnp.zeros_like(x)
    for e in range(N_EXPERTS):
        out = out + gate[..., e:e + 1] * swiglu(x, w_gate[e], w_up[e], w_down[e])
    return out


def setup_inputs(seed: int = 0) -> dict:
    key = jax.random.key(seed)
    ks = jax.random.split(key, 21)
    f32 = jnp.float32

    def nrm(k, shape, scale):
        return jax.random.normal(k, shape, f32) * scale

    return {
        'x': nrm(ks[0], (BATCH, SEQ, D_MODEL), 1.0),
        'w_in': nrm(ks[1], (DEPTH, D_MODEL, D_IN_PROJ), D_MODEL ** -0.5),
        'nsa_cmp_pe': nrm(ks[2], (DEPTH, 2, CMP_BLOCK, HEAD_DIM), 0.1),
        'nsa_cmp_w1': nrm(ks[3], (DEPTH, 2, CMP_BLOCK * HEAD_DIM, CMP_HIDDEN), (CMP_BLOCK * HEAD_DIM) ** -0.5),
        'nsa_cmp_w2': nrm(ks[4], (DEPTH, 2, CMP_HIDDEN, HEAD_DIM), CMP_HIDDEN ** -0.5),
        'fox_f_bias': jax.random.uniform(ks[5], (DEPTH, FOX_HEADS), f32, 1.0, 4.0),
        'w_nsa_branch': nrm(ks[6], (DEPTH, NSA_Q_W, D_MODEL), NSA_Q_W ** -0.5),
        'w_fox_branch': nrm(ks[7], (DEPTH, FOX_W, D_MODEL), FOX_W ** -0.5),
        'w_out': nrm(ks[8], (DEPTH, D_MODEL, D_MODEL), DN_BETA * D_MODEL ** -0.5),
        'rel_bias': nrm(ks[9], (REL_BUCKETS, NSA_HEADS), 0.5),
        'ln1_g': 1.0 + nrm(ks[10], (DEPTH, D_MODEL), 0.02),
        'ln1_b': nrm(ks[11], (DEPTH, D_MODEL), 0.02),
        'ln2_g': 1.0 + nrm(ks[12], (DEPTH, D_MODEL), 0.02),
        'ln2_b': nrm(ks[13], (DEPTH, D_MODEL), 0.02),
        'dense_w_gate': nrm(ks[14], (N_DENSE_LAYERS, D_MODEL, D_FF_DENSE), D_MODEL ** -0.5),
        'dense_w_up': nrm(ks[15], (N_DENSE_LAYERS, D_MODEL, D_FF_DENSE), D_MODEL ** -0.5),
        'dense_w_down': nrm(ks[16], (N_DENSE_LAYERS, D_FF_DENSE, D_MODEL), DN_BETA * D_FF_DENSE ** -0.5),
        'moe_router': nrm(ks[17], (N_MOE_LAYERS, D_MODEL, N_EXPERTS), D_MODEL ** -0.5),
        'moe_w_gate': nrm(ks[18], (N_MOE_LAYERS, N_EXPERTS, D_MODEL, D_FF_EXPERT), D_MODEL ** -0.5),
        'moe_w_up': nrm(ks[19], (N_MOE_LAYERS, N_EXPERTS, D_MODEL, D_FF_EXPERT), D_MODEL ** -0.5),
        'moe_w_down': nrm(ks[20], (N_MOE_LAYERS, N_EXPERTS, D_FF_EXPERT, D_MODEL), DN_BETA * D_FF_EXPERT ** -0.5),
    }


def reference(x, w_in, nsa_cmp_pe, nsa_cmp_w1, nsa_cmp_w2, fox_f_bias, w_nsa_branch, w_fox_branch,
              w_out, rel_bias, ln1_g, ln1_b, ln2_g, ln2_b, dense_w_gate, dense_w_up, dense_w_down,
              moe_router, moe_w_gate, moe_w_up, moe_w_down):
    for layer in range(DEPTH):
        h = hybrid_mixer(x, w_in[layer], nsa_cmp_pe[layer], nsa_cmp_w1[layer], nsa_cmp_w2[layer],
                         fox_f_bias[layer], w_nsa_branch[layer], w_fox_branch[layer], w_out[layer], rel_bias)
        x = layer_norm(DN_ALPHA * x + h, ln1_g[layer], ln1_b[layer])
        j = layer // 2
        if layer % 2 == 0:
            f = swiglu(x, dense_w_gate[j], dense_w_up[j], dense_w_down[j])
        else:
            f = moe_swiglu(x, moe_router[j], moe_w_gate[j], moe_w_up[j], moe_w_down[j])
        x = layer_norm(DN_ALPHA * x + f, ln2_g[layer], ln2_b[layer])
    return x
```

```python
import functools
import math

import numpy as np
import jax
import jax.numpy as jnp
from jax import lax
from jax.experimental import pallas as pl
from jax.experimental.pallas import tpu as pltpu

f32 = jnp.float32
bf16 = jnp.bfloat16
i32 = jnp.int32

D_MODEL = 1024
HEAD_DIM = 64
LANES = 128
NSA_HEADS = 8
NSA_KV_HEADS = 2
NSA_GROUP = NSA_HEADS // NSA_KV_HEADS
FOX_HEADS = 8
CMP_BLOCK = 32
CMP_STRIDE = 16
CMP_HIDDEN = 128
SEL_BLOCK = 64
N_SELECT = 16
N_LOCAL_SEL = 2
WINDOW = 512
REL_BUCKETS = 32
REL_MAX_DIST = 128
N_EXPERTS = 8
DEPTH = 4
DN_ALPHA = (2 * DEPTH) ** 0.25
LN_EPS = 1e-5
FORCED_SCORE = 1e4
NEG_INF = -1e30

NSA_Q_W = NSA_HEADS * HEAD_DIM
NSA_KV_W = NSA_KV_HEADS * HEAD_DIM
FOX_W = FOX_HEADS * HEAD_DIM
QKV_W = NSA_Q_W + 4 * NSA_KV_W + 3 * FOX_W
AUX_W = 2 * NSA_KV_W + LANES
GATE_LANES = 3 * NSA_HEADS

T_NSA = 128
T_FOX = 256
TM_PROJ = 512
TM_MERGE = 256
TM_FFN = 512
TF_FFN = 256
TR_MOE = 256
TF_MOE = 512
VMEM_LIMIT = 56 * 1024 * 1024


def _cparams(sem):
    return pltpu.CompilerParams(dimension_semantics=sem, vmem_limit_bytes=VMEM_LIMIT)


def _dot(a, b):
    return jnp.dot(a, b, preferred_element_type=f32)


def _dot_nt(a, b):
    return lax.dot_general(a, b, (((1,), (1,)), ((), ())), preferred_element_type=f32)


def _sigmoid(x):
    return 1.0 / (1.0 + jnp.exp(-x))


def _layer_norm(z, g, b):
    mu = jnp.mean(z, axis=-1, keepdims=True)
    zc = z - mu
    var = jnp.mean(zc * zc, axis=-1, keepdims=True)
    return zc * lax.rsqrt(var + LN_EPS) * g + b


def _split3(x):
    hi = x.astype(bf16)
    r1 = x - hi.astype(f32)
    mid = r1.astype(bf16)
    lo = (r1 - mid.astype(f32)).astype(bf16)
    return hi, mid, lo


def _proj_kernel(x_ref, w_ref, o_ref):
    o_ref[...] = _dot(x_ref[...], w_ref[...]).astype(o_ref.dtype)


def _proj(xb, w, out_dtype):
    m, k = xb.shape
    n = w.shape[1]
    return pl.pallas_call(
        _proj_kernel,
        grid=(m // TM_PROJ,),
        in_specs=[pl.BlockSpec((TM_PROJ, k), lambda i: (i, 0)),
                  pl.BlockSpec((k, n), lambda i: (0, 0))],
        out_specs=pl.BlockSpec((TM_PROJ, n), lambda i: (i, 0)),
        out_shape=jax.ShapeDtypeStruct((m, n), out_dtype),
        compiler_params=_cparams(("parallel",)),
        name="proj",
    )(xb, w)


def _gateprep_kernel(a_ref, fb_ref, tri_ref, g_ref, ft_ref, *, seq):
    tb = LANES
    lane = lax.broadcasted_iota(i32, (tb, LANES), 1)
    tri = tri_ref[...]
    carry = jnp.zeros((1, LANES), f32)
    for blk in range(seq // tb):
        v = a_ref[0, blk * tb:(blk + 1) * tb, :]
        sig = _sigmoid(v)
        z = v + fb_ref[...]
        logf = jnp.minimum(z, 0.0) - jnp.log1p(jnp.exp(-jnp.abs(z)))
        hi, mid, lo = _split3(logf)
        c = _dot(tri, hi) + _dot(tri, mid) + _dot(tri, lo) + carry
        carry = c[tb - 1:tb, :]
        g_ref[0, blk * tb:(blk + 1) * tb, :] = jnp.where(lane < GATE_LANES, sig, c)
        ct = c.T
        ft_ref[0, :, blk * tb:(blk + 1) * tb] = ct[GATE_LANES:GATE_LANES + FOX_HEADS, :]


def _gateprep(aux, fb_row, tri):
    b, s, _ = aux.shape
    return pl.pallas_call(
        functools.partial(_gateprep_kernel, seq=s),
        grid=(b,),
        in_specs=[pl.BlockSpec((1, s, LANES), lambda i: (i, 0, 2)),
                  pl.BlockSpec((1, LANES), lambda i: (0, 0)),
                  pl.BlockSpec((LANES, LANES), lambda i: (0, 0))],
        out_specs=[pl.BlockSpec((1, s, LANES), lambda i: (i, 0, 0)),
                   pl.BlockSpec((1, FOX_HEADS, s), lambda i: (i, 0, 0))],
        out_shape=[jax.ShapeDtypeStruct((b, s, LANES), f32),
                   jax.ShapeDtypeStruct((b, FOX_HEADS, s), f32)],
        compiler_params=_cparams(("parallel",)),
        name="gateprep",
    )(aux, fb_row, tri)


def _gelu_tanh(x):
    c = math.sqrt(2.0 / math.pi)
    return x * (0.5 * (1.0 + jnp.tanh(c * (x + 0.044715 * (x * x * x)))))


def _compress_kernel(h_ref, pet_ref, peb_ref, w1_ref, w2_ref, o_ref, *, nhalf):
    half_w = CMP_STRIDE * HEAD_DIM
    out = jnp.zeros((nhalf, LANES), f32)
    for h in range(NSA_KV_HEADS):
        hh = h_ref[0, 0, h]
        top = _dot((hh + pet_ref[0]).astype(bf16), w1_ref[0, 0:half_w, :])
        bot = _dot((hh + peb_ref[0]).astype(bf16), w1_ref[0, half_w:2 * half_w, :])
        pre = top + pltpu.roll(bot, nhalf - 1, 0)
        out = out + _dot(_gelu_tanh(pre).astype(bf16), w2_ref[0, h])
    o_ref[0, 0] = out.astype(o_ref.dtype)


def _compress(hk, pe_top, pe_bot, w1, w2p):
    b, _, _, nhalf, hw = hk.shape
    return pl.pallas_call(
        functools.partial(_compress_kernel, nhalf=nhalf),
        grid=(b, 2),
        in_specs=[pl.BlockSpec((1, 1, NSA_KV_HEADS, nhalf, hw), lambda i, j: (i, j, 0, 0, 0)),
                  pl.BlockSpec((1, 1, hw), lambda i, j: (j, 0, 0)),
                  pl.BlockSpec((1, 1, hw), lambda i, j: (j, 0, 0)),
                  pl.BlockSpec((1, 2 * hw, CMP_HIDDEN), lambda i, j: (j, 0, 0)),
                  pl.BlockSpec((1, NSA_KV_HEADS, CMP_HIDDEN, LANES), lambda i, j: (j, 0, 0, 0))],
        out_specs=pl.BlockSpec((1, 1, nhalf, LANES), lambda i, j: (i, j, 0, 0)),
        out_shape=jax.ShapeDtypeStruct((b, 2, nhalf, LANES), bf16),
        compiler_params=_cparams(("parallel", "parallel")),
        name="compress",
    )(hk, pe_top, pe_bot, w1, w2p)


def _online_softmax_step(s, v, m_ref, l_ref, acc_ref):
    m_old = m_ref[...]
    m_new = jnp.maximum(m_old, jnp.max(s, axis=-1, keepdims=True))
    alpha = jnp.exp(m_old - m_new)
    p = jnp.exp(s - m_new)
    l_ref[...] = alpha * l_ref[...] + jnp.sum(p, axis=-1, keepdims=True)
    acc_ref[...] = alpha * acc_ref[...] + _dot(p.astype(bf16), v)
    m_ref[...] = m_new


def _nsa_kernel(q_ref, ks_ref, vs_ref, kw_ref, vw_ref, cmp_ref, gate_ref, bc_ref, tb_ref, ov_ref,
                et_ref, o_ref, qaug_ref, kaug_ref, m_ref, l_ref, acc_ref, os_ref, *, seq, ncmp):
    t = T_NSA
    rows = NSA_HEADS * t
    nsel = seq // SEL_BLOCK
    topn = min(N_SELECT, nsel)
    i = pl.program_id(1)
    t0 = i * t
    lane = lax.broadcasted_iota(i32, (t, LANES), 1)
    lo_half = lane < HEAD_DIM

    @pl.when(i == 0)
    def _():
        kaug_ref[:, 0:LANES] = ks_ref[0]
        kaug_ref[:, LANES:2 * LANES] = et_ref[...]

    for g in range(NSA_GROUP):
        qg = q_ref[0, :, g * LANES:(g + 1) * LANES].astype(f32)
        qaug_ref[(2 * g) * t:(2 * g + 1) * t, 0:LANES] = jnp.where(lo_half, qg, 0.0).astype(bf16)
        qaug_ref[(2 * g + 1) * t:(2 * g + 2) * t, 0:LANES] = jnp.where(lo_half, 0.0, qg).astype(bf16)
    qs = qaug_ref[:, 0:LANES]

    s = _dot_nt(qs, cmp_ref[0, 0]) + bc_ref[...]
    row_t = t0 + (lax.broadcasted_iota(i32, (rows, ncmp), 0) & (t - 1))
    cmp_end = lax.broadcasted_iota(i32, (rows, ncmp), 1) * CMP_STRIDE + (CMP_BLOCK - 1)
    vis = cmp_end <= row_t
    s = jnp.where(vis, s, NEG_INF)
    s = s - jnp.max(s, axis=-1, keepdims=True)
    e = jnp.where(vis, jnp.exp(s), 0.0)
    p_c = e * (1.0 / jnp.maximum(jnp.sum(e, axis=-1, keepdims=True), 1e-30))
    o_c = _dot(p_c.astype(bf16), cmp_ref[0, 1])

    t_col = t0 + lax.broadcasted_iota(i32, (t, LANES), 0)
    cur = t_col >> 6
    forced = (lane == 0) | ((lane <= cur) & (lane > cur - N_LOCAL_SEL))
    for h in range(NSA_KV_HEADS):
        psum = p_c[h * t:(h + 1) * t]
        for g in range(1, NSA_GROUP):
            psum = psum + p_c[(2 * g + h) * t:(2 * g + h + 1) * t]
        hi = psum.astype(bf16)
        lo = (psum - hi.astype(f32)).astype(bf16)
        imp = _dot(hi, ov_ref[...]) + _dot(lo, ov_ref[...])
        val = jnp.where(forced, FORCED_SCORE, jnp.where(lane <= cur, imp, -1.0))
        val = jnp.where(lane < nsel, val, -2.0)
        cnt = jnp.zeros((t, LANES), f32)
        for j in range(nsel):
            vj = val[:, j:j + 1]
            beats = (vj > val) | ((vj == val) & (lane > j))
            cnt = cnt + jnp.where(beats, 1.0, 0.0)
        mneg = jnp.where(cnt < topn, 0.0, NEG_INF).astype(bf16)
        for g in range(NSA_GROUP):
            qaug_ref[(2 * g + h) * t:(2 * g + h + 1) * t, LANES:2 * LANES] = mneg

    def init_state():
        m_ref[...] = jnp.full((rows, 1), NEG_INF, f32)
        l_ref[...] = jnp.zeros((rows, 1), f32)
        acc_ref[...] = jnp.zeros((rows, LANES), f32)

    def finish():
        return acc_ref[...] * (1.0 / jnp.maximum(l_ref[...], 1e-30))

    init_state()

    def sel_body(kt, carry):
        k = kaug_ref[pl.ds(kt * t, t), :]
        s = _dot_nt(qaug_ref[...], k) + tb_ref[jnp.minimum(i - kt, 2)]
        _online_softmax_step(s, vs_ref[0, pl.ds(kt * t, t), :], m_ref, l_ref, acc_ref)
        return carry

    lax.fori_loop(0, i + 1, sel_body, 0)
    os_ref[...] = finish()

    init_state()
    nwin = WINDOW // t

    def win_body(kt, carry):
        d = i - kt
        kind = jnp.where(d < 3, d, jnp.where(d == nwin, 3, 2))
        k = kw_ref[0, pl.ds(kt * t, t), :]
        s = _dot_nt(qs, k) + tb_ref[kind]
        _online_softmax_step(s, vw_ref[0, pl.ds(kt * t, t), :], m_ref, l_ref, acc_ref)
        return carry

    lax.fori_loop(jnp.maximum(i - nwin, 0), i + 1, win_body, 0)
    o_w = finish()
    o_s = os_ref[...]

    gates = gate_ref[0]
    for g in range(NSA_GROUP):
        outs = []
        for h in range(NSA_KV_HEADS):
            r = 2 * g + h
            sl = slice(r * t, (r + 1) * t)
            outs.append(gates[:, r:r + 1] * o_c[sl]
                        + gates[:, NSA_HEADS + r:NSA_HEADS + r + 1] * o_s[sl]
                        + gates[:, 2 * NSA_HEADS + r:2 * NSA_HEADS + r + 1] * o_w[sl])
        o_ref[0, :, g * LANES:(g + 1) * LANES] = jnp.where(lo_half, outs[0], outs[1]).astype(o_ref.dtype)


def _nsa(qkv, cmpkv, gates, bias_c, tbias, ov, et):
    b, s, _ = qkv.shape
    ncmp = cmpkv.shape[2]
    t = T_NSA
    rows = NSA_HEADS * t
    kv_spec = lambda col: pl.BlockSpec((1, s, LANES), lambda bi, i, col=col: (bi, 0, col))
    return pl.pallas_call(
        functools.partial(_nsa_kernel, seq=s, ncmp=ncmp),
        grid=(b, s // t),
        in_specs=[pl.BlockSpec((1, t, NSA_Q_W), lambda bi, i: (bi, i, 0)),
                  kv_spec(4), kv_spec(5), kv_spec(6), kv_spec(7),
                  pl.BlockSpec((1, 2, ncmp, LANES), lambda bi, i: (bi, 0, 0, 0)),
                  pl.BlockSpec((1, t, LANES), lambda bi, i: (bi, i, 0)),
                  pl.BlockSpec((rows, ncmp), lambda bi, i: (i, 0)),
                  pl.BlockSpec((4, rows, LANES), lambda bi, i: (0, 0, 0)),
                  pl.BlockSpec((ncmp, LANES), lambda bi, i: (0, 0)),
                  pl.BlockSpec((s, LANES), lambda bi, i: (0, 0))],
        out_specs=pl.BlockSpec((1, t, NSA_Q_W), lambda bi, i: (bi, i, 0)),
        out_shape=jax.ShapeDtypeStruct((b, s, NSA_Q_W), bf16),
        scratch_shapes=[pltpu.VMEM((rows, 2 * LANES), bf16),
                        pltpu.VMEM((s, 2 * LANES), bf16),
                        pltpu.VMEM((rows, 1), f32),
                        pltpu.VMEM((rows, 1), f32),
                        pltpu.VMEM((rows, LANES), f32),
                        pltpu.VMEM((rows, LANES), f32)],
        compiler_params=_cparams(("parallel", "arbitrary")),
        name="nsa",
    )(qkv, qkv, qkv, qkv, qkv, cmpkv, gates, bias_c, tbias, ov, et)


def _fox_kernel(q_ref, k_ref, v_ref, fq0_ref, fq1_ref, ft_ref, o_ref, m_ref, l_ref, acc_ref):
    t = T_FOX
    pair = pl.program_id(1)
    i = pl.program_id(2)
    lane = lax.broadcasted_iota(i32, (t, LANES), 1)
    lo_half = lane < HEAD_DIM
    q = q_ref[0].astype(f32)
    qm = (jnp.where(lo_half, q, 0.0).astype(bf16), jnp.where(lo_half, 0.0, q).astype(bf16))
    fq = (fq0_ref[0, 0], fq1_ref[0, 0])
    causal = lax.broadcasted_iota(i32, (t, t), 1) <= lax.broadcasted_iota(i32, (t, t), 0)

    for h in range(2):
        m_ref[h] = jnp.full((t, 1), NEG_INF, f32)
        l_ref[h] = jnp.zeros((t, 1), f32)
        acc_ref[h] = jnp.zeros((t, LANES), f32)

    def tile(kt, masked):
        k = k_ref[0, pl.ds(kt * t, t), :]
        v = v_ref[0, pl.ds(kt * t, t), :]
        for h in range(2):
            fk = ft_ref[0, pl.ds(2 * pair + h, 1), pl.ds(kt * t, t)]
            s = _dot_nt(qm[h], k) + (fq[h] - fk)
            if masked:
                s = jnp.where(causal, s, NEG_INF)
            _online_softmax_step(s, v, m_ref.at[h], l_ref.at[h], acc_ref.at[h])

    def body(kt, carry):
        tile(kt, False)
        return carry

    lax.fori_loop(0, i, body, 0)
    tile(i, True)
    o0 = acc_ref[0] * (1.0 / jnp.maximum(l_ref[0], 1e-30))
    o1 = acc_ref[1] * (1.0 / jnp.maximum(l_ref[1], 1e-30))
    o_ref[0] = jnp.where(lo_half, o0, o1).astype(o_ref.dtype)


def _fox(qkv, ft):
    b, s, _ = qkv.shape
    t = T_FOX
    npair = FOX_HEADS // 2
    fqe = ft[..., None]
    base = (NSA_Q_W + 4 * NSA_KV_W) // LANES
    return pl.pallas_call(
        _fox_kernel,
        grid=(b, npair, s // t),
        in_specs=[pl.BlockSpec((1, t, LANES), lambda bi, p, i: (bi, i, base + p)),
                  pl.BlockSpec((1, s, LANES), lambda bi, p, i: (bi, 0, base + npair + p)),
                  pl.BlockSpec((1, s, LANES), lambda bi, p, i: (bi, 0, base + 2 * npair + p)),
                  pl.BlockSpec((1, 1, t, 1), lambda bi, p, i: (bi, 2 * p, i, 0)),
                  pl.BlockSpec((1, 1, t, 1), lambda bi, p, i: (bi, 2 * p + 1, i, 0)),
                  pl.BlockSpec((1, FOX_HEADS, s), lambda bi, p, i: (bi, 0, 0))],
        out_specs=pl.BlockSpec((1, t, LANES), lambda bi, p, i: (bi, i, p)),
        out_shape=jax.ShapeDtypeStruct((b, s, FOX_W), bf16),
        scratch_shapes=[pltpu.VMEM((2, t, 1), f32),
                        pltpu.VMEM((2, t, 1), f32),
                        pltpu.VMEM((2, t, LANES), f32)],
        compiler_params=_cparams(("parallel", "parallel", "parallel")),
        name="fox",
    )(qkv, qkv, qkv, fqe, fqe, ft)


def _merge_kernel(yn_ref, yf_ref, xb_ref, x_ref, wn_ref, wf_ref, wmg_ref, wo_ref, g_ref, b_ref,
                  xo_ref, xbo_ref):
    mg = _dot(xb_ref[...], wmg_ref[...])
    merged = (_sigmoid(mg[:, 0:D_MODEL]) * _dot(yn_ref[...], wn_ref[...])
              + _sigmoid(mg[:, D_MODEL:2 * D_MODEL]) * _dot(yf_ref[...], wf_ref[...]))
    hmix = _dot(merged.astype(bf16), wo_ref[...])
    xn = _layer_norm(DN_ALPHA * x_ref[...] + hmix, g_ref[...], b_ref[...])
    xo_ref[...] = xn
    xbo_ref[...] = xn.astype(bf16)


def _merge(yn, yf, xb, x, wn, wf, wmg, wo, g, bb):
    m = x.shape[0]
    tm = TM_MERGE
    row = lambda w: pl.BlockSpec((tm, w), lambda i: (i, 0))
    full = lambda a: pl.BlockSpec(a.shape, lambda i: (0, 0))
    return pl.pallas_call(
        _merge_kernel,
        grid=(m // tm,),
        in_specs=[row(NSA_Q_W), row(FOX_W), row(D_MODEL), row(D_MODEL),
                  full(wn), full(wf), full(wmg), full(wo), full(g), full(bb)],
        out_specs=[row(D_MODEL), row(D_MODEL)],
        out_shape=[jax.ShapeDtypeStruct((m, D_MODEL), f32), jax.ShapeDtypeStruct((m, D_MODEL), bf16)],
        compiler_params=_cparams(("parallel",)),
        name="merge",
    )(yn, yf, xb, x, wn, wf, wmg, wo, g, bb)


def _ffn_kernel(xb_ref, x_ref, wg_ref, wu_ref, wd_ref, g_ref, b_ref, xo_ref, xbo_ref, acc_ref, *, nf):
    f = pl.program_id(1)

    @pl.when(f == 0)
    def _():
        acc_ref[...] = jnp.zeros_like(acc_ref)

    xb = xb_ref[...]
    gate = _dot(xb, wg_ref[...])
    up = _dot(xb, wu_ref[...])
    act = (gate * _sigmoid(gate) * up).astype(bf16)
    acc_ref[...] += _dot(act, wd_ref[...])

    @pl.when(f == nf - 1)
    def _():
        xn = _layer_norm(DN_ALPHA * x_ref[...] + acc_ref[...], g_ref[...], b_ref[...])
        xo_ref[...] = xn
        xbo_ref[...] = xn.astype(bf16)


def _ffn(xb, x, wg, wu, wd, g, bb):
    m = x.shape[0]
    dff = wg.shape[1]
    tm, tf = TM_FFN, TF_FFN
    nf = dff // tf
    row = pl.BlockSpec((tm, D_MODEL), lambda i, f: (i, 0))
    vec = pl.BlockSpec((1, D_MODEL), lambda i, f: (0, 0))
    return pl.pallas_call(
        functools.partial(_ffn_kernel, nf=nf),
        grid=(m // tm, nf),
        in_specs=[row, row,
                  pl.BlockSpec((D_MODEL, tf), lambda i, f: (0, f)),
                  pl.BlockSpec((D_MODEL, tf), lambda i, f: (0, f)),
                  pl.BlockSpec((tf, D_MODEL), lambda i, f: (f, 0)),
                  vec, vec],
        out_specs=[row, row],
        out_shape=[jax.ShapeDtypeStruct((m, D_MODEL), f32), jax.ShapeDtypeStruct((m, D_MODEL), bf16)],
        scratch_shapes=[pltpu.VMEM((tm, D_MODEL), f32)],
        compiler_params=_cparams(("parallel", "arbitrary")),
        name="ffn",
    )(xb, x, wg, wu, wd, g, bb)


def _router_kernel(x_ref, r_ref, tri_ref, gate_ref, rank_ref, cnt_ref, *, seq):
    tb = tri_ref.shape[0]
    x = x_ref[0]
    xh = x.astype(bf16)
    xl = (x - xh.astype(f32)).astype(bf16)
    r = r_ref[...]
    rh = r.astype(bf16)
    rl = (r - rh.astype(f32)).astype(bf16)
    logits = _dot(xh, rh) + _dot(xh, rl) + _dot(xl, rh)
    lane = lax.broadcasted_iota(i32, (seq, LANES), 1).astype(f32)
    low = -3.0e38
    lg = jnp.where(lane < N_EXPERTS, logits, low)
    m1 = jnp.max(lg, axis=-1, keepdims=True)
    i1 = jnp.min(jnp.where(lg == m1, lane, float(LANES)), axis=-1, keepdims=True)
    lg2 = jnp.where(lane == i1, low, lg)
    m2 = jnp.max(lg2, axis=-1, keepdims=True)
    i2 = jnp.min(jnp.where(lg2 == m2, lane, float(LANES)), axis=-1, keepdims=True)
    e2 = jnp.exp(m2 - m1)
    den = 1.0 + e2
    gate_ref[0] = jnp.where(lane == i1, 1.0 / den, jnp.where(lane == i2, e2 / den, 0.0))
    sel = (lane == i1) | (lane == i2)
    selb = jnp.where(sel, 1.0, 0.0).astype(bf16)
    carry = jnp.zeros((1, LANES), f32)
    for blk in range(seq // tb):
        sl = slice(blk * tb, (blk + 1) * tb)
        c = _dot(tri_ref[...], selb[sl]) + carry
        carry = c[tb - 1:tb, :]
        rank_ref[0, sl, :] = jnp.where(sel[sl], c - 1.0, -1.0)
    cnt_ref[0] = carry.astype(i32)


def _router(x3, router_pad, tri):
    b, s, _ = x3.shape
    return pl.pallas_call(
        functools.partial(_router_kernel, seq=s),
        grid=(b,),
        in_specs=[pl.BlockSpec((1, s, D_MODEL), lambda i: (i, 0, 0)),
                  pl.BlockSpec((D_MODEL, LANES), lambda i: (0, 0)),
                  pl.BlockSpec(tri.shape, lambda i: (0, 0))],
        out_specs=[pl.BlockSpec((1, s, LANES), lambda i: (i, 0, 0)),
                   pl.BlockSpec((1, s, LANES), lambda i: (i, 0, 0)),
                   pl.BlockSpec((1, 1, LANES), lambda i: (i, 0, 0))],
        out_shape=[jax.ShapeDtypeStruct((b, s, LANES), f32),
                   jax.ShapeDtypeStruct((b, s, LANES), f32),
                   jax.ShapeDtypeStruct((b, 1, LANES), i32)],
        compiler_params=_cparams(("parallel",)),
        name="router",
    )(x3, router_pad, tri)


def _moe_kernel(cnt_ref, xb_ref, rankt_ref, gatet_ref, ranke_ref, wg_ref, wu_ref, wd_ref, y_ref,
                xg_ref, acc_ref, *, seq, nf):
    tr = TR_MOE
    c = pl.program_id(0)
    e = pl.program_id(1)
    f = pl.program_id(2)
    nsub = (cnt_ref[c * N_EXPERTS + e] + tr - 1) // tr

    def onehot_rows(s):
        rk = rankt_ref[0, pl.ds(e, 1), :]
        want = (s * tr + lax.broadcasted_iota(i32, (tr, seq), 0)).astype(f32)
        return rk == want

    @pl.when((e == 0) & (f == 0))
    def _():
        y_ref[...] = jnp.zeros_like(y_ref)

    @pl.when(f == 0)
    def _():
        def gather(s, carry):
            p = jnp.where(onehot_rows(s), 1.0, 0.0).astype(bf16)
            xg_ref[pl.ds(s * tr, tr), :] = _dot(p, xb_ref[0]).astype(bf16)
            acc_ref[pl.ds(s * tr, tr), :] = jnp.zeros((tr, D_MODEL), f32)
            return carry
        lax.fori_loop(0, nsub, gather, 0)

    def hidden(s, carry):
        xs = xg_ref[pl.ds(s * tr, tr), :]
        gate = _dot(xs, wg_ref[0])
        up = _dot(xs, wu_ref[0])
        act = (gate * _sigmoid(gate) * up).astype(bf16)
        acc_ref[pl.ds(s * tr, tr), :] += _dot(act, wd_ref[0])
        return carry

    lax.fori_loop(0, nsub, hidden, 0)

    @pl.when(f == nf - 1)
    def _():
        def combine(s, carry):
            grow = gatet_ref[0, pl.ds(e, 1), :]
            w = jnp.sum(jnp.where(onehot_rows(s), grow, 0.0), axis=-1, keepdims=True)
            z = (acc_ref[pl.ds(s * tr, tr), :] * w).astype(bf16)
            want = (s * tr + lax.broadcasted_iota(i32, (512, tr), 1)).astype(f32)
            for jb in range(seq // 512):
                rc = ranke_ref[0, 0, jb * 512:(jb + 1) * 512, :]
                pt = jnp.where(rc == want, 1.0, 0.0).astype(bf16)
                y_ref[0, jb * 512:(jb + 1) * 512, :] += _dot(pt, z)
            return carry
        lax.fori_loop(0, nsub, combine, 0)


def _moe(counts, xb3, rankt, gatet, ranke, wg, wu, wd):
    b, s, _ = xb3.shape
    dff = wg.shape[2]
    nf = dff // TF_MOE
    grid_spec = pltpu.PrefetchScalarGridSpec(
        num_scalar_prefetch=1,
        grid=(b, N_EXPERTS, nf),
        in_specs=[pl.BlockSpec((1, s, D_MODEL), lambda c, e, f, cnt: (c, 0, 0)),
                  pl.BlockSpec((1, N_EXPERTS, s), lambda c, e, f, cnt: (c, 0, 0)),
                  pl.BlockSpec((1, N_EXPERTS, s), lambda c, e, f, cnt: (c, 0, 0)),
                  pl.BlockSpec((1, 1, s, 1), lambda c, e, f, cnt: (c, e, 0, 0)),
                  pl.BlockSpec((1, D_MODEL, TF_MOE), lambda c, e, f, cnt: (e, 0, f)),
                  pl.BlockSpec((1, D_MODEL, TF_MOE), lambda c, e, f, cnt: (e, 0, f)),
                  pl.BlockSpec((1, TF_MOE, D_MODEL), lambda c, e, f, cnt: (e, f, 0))],
        out_specs=pl.BlockSpec((1, s, D_MODEL), lambda c, e, f, cnt: (c, 0, 0)),
        scratch_shapes=[pltpu.VMEM((s, D_MODEL), bf16), pltpu.VMEM((s, D_MODEL), f32)],
    )
    return pl.pallas_call(
        functools.partial(_moe_kernel, seq=s, nf=nf),
        grid_spec=grid_spec,
        out_shape=jax.ShapeDtypeStruct((b, s, D_MODEL), f32),
        compiler_params=_cparams(("parallel", "arbitrary", "arbitrary")),
        name="moe",
    )(counts, xb3, rankt, gatet, ranke, wg, wu, wd)


def _resln_kernel(x_ref, y_ref, g_ref, b_ref, xo_ref, xbo_ref):
    xn = _layer_norm(DN_ALPHA * x_ref[...] + y_ref[...], g_ref[...], b_ref[...])
    xo_ref[...] = xn
    xbo_ref[...] = xn.astype(bf16)


def _resln(x, y, g, bb):
    m = x.shape[0]
    tm = TM_FFN
    row = pl.BlockSpec((tm, D_MODEL), lambda i: (i, 0))
    vec = pl.BlockSpec((1, D_MODEL), lambda i: (0, 0))
    return pl.pallas_call(
        _resln_kernel,
        grid=(m // tm,),
        in_specs=[row, row, vec, vec],
        out_specs=[row, row],
        out_shape=[jax.ShapeDtypeStruct((m, D_MODEL), f32), jax.ShapeDtypeStruct((m, D_MODEL), bf16)],
        compiler_params=_cparams(("parallel",)),
        name="resln",
    )(x, y, g, bb)


def _t5_bucket(dist):
    n = jnp.maximum(dist, 0)
    max_exact = REL_BUCKETS // 2
    large = max_exact + (jnp.log(jnp.maximum(n, 1).astype(f32) / max_exact)
                         / math.log(REL_MAX_DIST / max_exact) * (REL_BUCKETS - max_exact)).astype(i32)
    large = jnp.minimum(large, REL_BUCKETS - 1)
    return jnp.where(n < max_exact, n, large)


def _head_perm():
    return np.array([(r % 2) * NSA_GROUP + r // 2 for r in range(NSA_HEADS)])


def _bias_tables(rel_bias, seq):
    t = T_NSA
    tab = rel_bias.astype(f32)[:, _head_perm()]
    ii = jnp.arange(t)[:, None]
    jj = jnp.arange(t)[None, :]
    d0 = ii - jj
    kinds = [
        jnp.where((d0 >= 0)[..., None], tab[_t5_bucket(d0)], NEG_INF),
        tab[_t5_bucket(d0 + t)],
        tab[_t5_bucket(d0 + 2 * t)],
        jnp.where((d0 < 0)[..., None], tab[_t5_bucket(d0 + WINDOW)], NEG_INF),
    ]
    tbias = jnp.stack(kinds).transpose(0, 3, 1, 2).reshape(4, NSA_HEADS * t, t)
    ncmp = seq // CMP_STRIDE
    tpos = jnp.arange(seq)[:, None]
    cend = jnp.arange(ncmp)[None, :] * CMP_STRIDE + CMP_BLOCK - 1
    bc = tab[_t5_bucket(tpos - cend)]
    bc = bc.reshape(seq // t, t, ncmp, NSA_HEADS).transpose(0, 3, 1, 2).reshape(seq * NSA_HEADS, ncmp)
    return tbias, bc


def _selection_constants(seq):
    ncmp = seq // CMP_STRIDE
    nsel = seq // SEL_BLOCK
    c0 = np.arange(ncmp)[:, None] * CMP_STRIDE
    s0 = np.arange(LANES)[None, :] * SEL_BLOCK
    ov = np.maximum(np.minimum(c0 + CMP_BLOCK, s0 + SEL_BLOCK) - np.maximum(c0, s0), 0) / CMP_BLOCK
    ov[ncmp - 1, :] = 0.0
    ov[:, nsel:] = 0.0
    et = (np.arange(seq)[:, None] // SEL_BLOCK == np.arange(LANES)[None, :]).astype(np.float32)
    return jnp.asarray(ov, bf16), jnp.asarray(et, bf16)


def _layer_weights(w_in, layer_pe, w1, w2, f_bias, w_nsa_br, w_fox_br):
    offs = np.cumsum((NSA_Q_W, 6 * NSA_KV_W, 3 * NSA_HEADS, 3 * FOX_W, FOX_HEADS, 2 * D_MODEL))
    q0, kv0, g0, fx0, ff0, mg0 = 0, offs[0], offs[1], offs[2], offs[3], offs[4]
    scale = HEAD_DIM ** -0.5
    perm = _head_perm()
    qcols = (perm[:, None] * HEAD_DIM + np.arange(HEAD_DIM)[None, :]).reshape(-1)
    w_q = w_in[:, q0 + qcols] * scale
    w_kv = w_in[:, kv0:g0]
    w_fq = w_in[:, fx0:fx0 + FOX_W] * scale
    w_fkv = w_in[:, fx0 + FOX_W:ff0]
    w_qkv = jnp.concatenate([w_q, w_kv[:, 2 * NSA_KV_W:], w_fq, w_fkv], axis=1).astype(bf16)
    gcols = np.array([perm[r] * 3 + br for br in range(3) for r in range(NSA_HEADS)])
    pad = jnp.zeros((D_MODEL, LANES - GATE_LANES - FOX_HEADS), w_in.dtype)
    w_aux = jnp.concatenate([w_kv[:, :2 * NSA_KV_W], w_in[:, g0 + gcols], w_in[:, ff0:mg0], pad],
                            axis=1).astype(bf16)
    w_mg = w_in[:, mg0:].astype(bf16)
    fb_row = jnp.zeros((1, LANES), f32).at[0, GATE_LANES:GATE_LANES + FOX_HEADS].set(f_bias.astype(f32))
    half = CMP_STRIDE * HEAD_DIM
    pe_flat = layer_pe.astype(f32).reshape(2, 1, CMP_BLOCK * HEAD_DIM)
    pe_top, pe_bot = pe_flat[:, :, :half], pe_flat[:, :, half:]
    z = jnp.zeros_like(w2)
    w2p = jnp.stack([jnp.concatenate([w2, z], axis=-1), jnp.concatenate([z, w2], axis=-1)], axis=1)
    wn = w_nsa_br[qcols].astype(bf16)
    return (w_qkv, w_aux, w_mg, fb_row, pe_top, pe_bot, w1.astype(bf16), w2p.astype(bf16), wn,
            w_fox_br.astype(bf16))


def kernel(x, w_in, nsa_cmp_pe, nsa_cmp_w1, nsa_cmp_w2, fox_f_bias, w_nsa_branch, w_fox_branch, w_out,
           rel_bias, ln1_g, ln1_b, ln2_g, ln2_b, dense_w_gate, dense_w_up, dense_w_down, moe_router,
           moe_w_gate, moe_w_up, moe_w_down):
    b, s, d = x.shape
    m = b * s
    nhalf = s // CMP_STRIDE
    tbias, bias_c = _bias_tables(rel_bias, s)
    ov, et = _selection_constants(s)
    tri128 = jnp.asarray(np.tril(np.ones((LANES, LANES), np.float32)), bf16)
    tri256 = jnp.asarray(np.tril(np.ones((256, 256), np.float32)), bf16)

    xf = x.reshape(m, d).astype(f32)
    xb = xf.astype(bf16)
    for layer in range(DEPTH):
        (w_qkv, w_aux, w_mg, fb_row, pe_top, pe_bot, w1, w2p, wn, wf) = _layer_weights(
            w_in[layer], nsa_cmp_pe[layer], nsa_cmp_w1[layer], nsa_cmp_w2[layer], fox_f_bias[layer],
            w_nsa_branch[layer], w_fox_branch[layer])
        qkv = _proj(xb, w_qkv, bf16).reshape(b, s, QKV_W)
        aux = _proj(xb, w_aux, f32).reshape(b, s, AUX_W)
        gates, ft = _gateprep(aux, fb_row, tri128)
        hk = aux[:, :, :2 * NSA_KV_W].reshape(b, nhalf, CMP_STRIDE, 2, NSA_KV_HEADS, HEAD_DIM)
        hk = hk.transpose(0, 3, 4, 1, 2, 5).reshape(b, 2, NSA_KV_HEADS, nhalf, CMP_STRIDE * HEAD_DIM)
        cmpkv = _compress(hk, pe_top, pe_bot, w1, w2p)
        y_nsa = _nsa(qkv, cmpkv, gates, bias_c, tbias, ov, et).reshape(m, NSA_Q_W)
        y_fox = _fox(qkv, ft).reshape(m, FOX_W)
        xf, xb = _merge(y_nsa, y_fox, xb, xf, wn, wf, w_mg, w_out[layer].astype(bf16),
                        ln1_g[layer].reshape(1, d), ln1_b[layer].reshape(1, d))
        j = layer // 2
        g2, b2 = ln2_g[layer].reshape(1, d), ln2_b[layer].reshape(1, d)
        if layer % 2 == 0:
            xf, xb = _ffn(xb, xf, dense_w_gate[j].astype(bf16), dense_w_up[j].astype(bf16),
                          dense_w_down[j].astype(bf16), g2, b2)
        else:
            router_pad = jnp.zeros((d, LANES), f32).at[:, :N_EXPERTS].set(moe_router[j].astype(f32))
            gate, rank, cnt = _router(xf.reshape(b, s, d), router_pad, tri256)
            rankt = rank[:, :, :N_EXPERTS].transpose(0, 2, 1)
            gatet = gate[:, :, :N_EXPERTS].transpose(0, 2, 1)
            counts = cnt[:, 0, :N_EXPERTS].reshape(-1)
            y = _moe(counts, xb.reshape(b, s, d), rankt, gatet, rankt[..., None],
                     moe_w_gate[j].astype(bf16), moe_w_up[j].astype(bf16), moe_w_down[j].astype(bf16))
            xf, xb = _resln(xf, y.reshape(m, d), g2, b2)
    return xf.reshape(b, s, d).astype(x.dtype)
```

```python
import functools
import math

import numpy as np
import jax
import jax.numpy as jnp
from jax import lax
from jax.experimental import pallas as pl
from jax.experimental.pallas import tpu as pltpu

f32 = jnp.float32
bf16 = jnp.bfloat16
i32 = jnp.int32

D_MODEL = 1024
HEAD_DIM = 64
LANES = 128
NSA_HEADS = 8
NSA_KV_HEADS = 2
NSA_GROUP = NSA_HEADS // NSA_KV_HEADS
FOX_HEADS = 8
FOX_PAIRS = FOX_HEADS // 2
CMP_BLOCK = 32
CMP_STRIDE = 16
CMP_HIDDEN = 128
SEL_BLOCK = 64
N_SELECT = 16
N_LOCAL_SEL = 2
WINDOW = 512
REL_BUCKETS = 32
REL_MAX_DIST = 128
N_EXPERTS = 8
DEPTH = 4
DN_ALPHA = (2 * DEPTH) ** 0.25
LN_EPS = 1e-5
FORCED_SCORE = 1e4
NEG_INF = -1e30

NSA_Q_W = NSA_HEADS * HEAD_DIM
FOX_W = FOX_HEADS * HEAD_DIM
QKV_TILES_KV = NSA_Q_W // LANES
QKV_TILES_FOX = QKV_TILES_KV + 4 * NSA_KV_HEADS
QKV_W = (QKV_TILES_FOX + 3 * FOX_PAIRS) * LANES
AUX_W = 4 * LANES
FGATE_LANE = 24
XCOLS = 6

T_ATT = 256
TM_PROJ = 512
TM_MERGE = 256
TM_FFN = 512
TF_FFN = 256
TR_MOE = 256
TF_MOE = 512
VMEM_LIMIT = 56 * 1024 * 1024


def _cparams(sem):
    return pltpu.CompilerParams(dimension_semantics=sem, vmem_limit_bytes=VMEM_LIMIT)


def _dot(a, b):
    return jnp.dot(a, b, preferred_element_type=f32)


def _dot_nt(a, b):
    return lax.dot_general(a, b, (((1,), (1,)), ((), ())), preferred_element_type=f32)


def _sigmoid(x):
    return 1.0 / (1.0 + jnp.exp(-x))


def _layer_norm(z, g, b):
    mu = jnp.mean(z, axis=-1, keepdims=True)
    zc = z - mu
    var = jnp.mean(zc * zc, axis=-1, keepdims=True)
    return zc * lax.rsqrt(var + LN_EPS) * g + b


def _split3(x):
    hi = x.astype(bf16)
    r1 = x - hi.astype(f32)
    mid = r1.astype(bf16)
    lo = (r1 - mid.astype(f32)).astype(bf16)
    return hi, mid, lo


def _proj_kernel(x_ref, w_ref, o_ref):
    o_ref[...] = _dot(x_ref[...], w_ref[...]).astype(o_ref.dtype)


def _proj(xb, w, out_dtype):
    m, k = xb.shape
    n = w.shape[1]
    return pl.pallas_call(
        _proj_kernel,
        grid=(m // TM_PROJ,),
        in_specs=[pl.BlockSpec((TM_PROJ, k), lambda i: (i, 0)),
                  pl.BlockSpec((k, n), lambda i: (0, 0))],
        out_specs=pl.BlockSpec((TM_PROJ, n), lambda i: (i, 0)),
        out_shape=jax.ShapeDtypeStruct((m, n), out_dtype),
        compiler_params=_cparams(("parallel",)),
        name="proj",
    )(xb, w)


def _gateprep_kernel(a_ref, fb_ref, tri_ref, pq_ref, pk_ref, oq_ref, ok_ref, g_ref, qx_ref, kx_ref, *, seq):
    tb = LANES
    tri = tri_ref[...]
    carry = jnp.zeros((1, LANES), f32)
    for blk in range(seq // tb):
        sl = slice(blk * tb, (blk + 1) * tb)
        va = a_ref[0, sl, 0:LANES]
        g_ref[0, 0, sl, :] = _sigmoid(va)
        g_ref[0, 1, sl, :] = _sigmoid(a_ref[0, sl, LANES:2 * LANES])
        z = va + fb_ref[...]
        logf = jnp.minimum(z, 0.0) - jnp.log1p(jnp.exp(-jnp.abs(z)))
        hi, mid, lo = _split3(logf)
        c = _dot(tri, hi) + _dot(tri, mid) + _dot(tri, lo) + carry
        carry = c[tb - 1:tb, :]
        chi, cmid, clo = _split3(c)
        qx = _dot(chi, pq_ref[0]) + _dot(cmid, pq_ref[1]) + _dot(clo, pq_ref[2]) + oq_ref[...]
        kx = _dot(chi, pk_ref[0]) + _dot(cmid, pk_ref[1]) + _dot(clo, pk_ref[2]) + ok_ref[...]
        qx_ref[0, sl, :] = qx.astype(bf16)
        kx_ref[0, sl, :] = kx.astype(bf16)


def _gateprep(aux, fb_row, tri, pq, pk, oq, ok):
    b, s, _ = aux.shape
    xw = FOX_PAIRS * LANES
    const2 = lambda a: pl.BlockSpec(a.shape, lambda i: (0, 0))
    const3 = lambda a: pl.BlockSpec(a.shape, lambda i: (0, 0, 0))
    return pl.pallas_call(
        functools.partial(_gateprep_kernel, seq=s),
        grid=(b,),
        in_specs=[pl.BlockSpec((1, s, 2 * LANES), lambda i: (i, 0, 1)),
                  const2(fb_row), const2(tri), const3(pq), const3(pk), const2(oq), const2(ok)],
        out_specs=[pl.BlockSpec((1, 2, s, LANES), lambda i: (i, 0, 0, 0)),
                   pl.BlockSpec((1, s, xw), lambda i: (i, 0, 0)),
                   pl.BlockSpec((1, s, xw), lambda i: (i, 0, 0))],
        out_shape=[jax.ShapeDtypeStruct((b, 2, s, LANES), f32),
                   jax.ShapeDtypeStruct((b, s, xw), bf16),
                   jax.ShapeDtypeStruct((b, s, xw), bf16)],
        compiler_params=_cparams(("parallel",)),
        name="gateprep",
    )(aux, fb_row, tri, pq, pk, oq, ok)


def _gelu_tanh(x):
    c = math.sqrt(2.0 / math.pi)
    return x * (0.5 * (1.0 + jnp.tanh(c * (x + 0.044715 * (x * x * x)))))


def _compress_kernel(h_ref, pet_ref, peb_ref, w1_ref, w2_ref, o_ref, *, nhalf):
    half_w = CMP_STRIDE * HEAD_DIM
    for h in range(NSA_KV_HEADS):
        hh = h_ref[0, 0, h]
        top = _dot((hh + pet_ref[0]).astype(bf16), w1_ref[0, 0:half_w, :])
        bot = _dot((hh + peb_ref[0]).astype(bf16), w1_ref[0, half_w:2 * half_w, :])
        pre = top + pltpu.roll(bot, nhalf - 1, 0)
        o_ref[0, 0, h] = _dot(_gelu_tanh(pre).astype(bf16), w2_ref[0]).astype(o_ref.dtype)


def _compress(hk, pe_top, pe_bot, w1, w2d):
    b, _, _, nhalf, hw = hk.shape
    return pl.pallas_call(
        functools.partial(_compress_kernel, nhalf=nhalf),
        grid=(b, 2),
        in_specs=[pl.BlockSpec((1, 1, NSA_KV_HEADS, nhalf, hw), lambda i, j: (i, j, 0, 0, 0)),
                  pl.BlockSpec((1, 1, hw), lambda i, j: (j, 0, 0)),
                  pl.BlockSpec((1, 1, hw), lambda i, j: (j, 0, 0)),
                  pl.BlockSpec((1, 2 * hw, CMP_HIDDEN), lambda i, j: (j, 0, 0)),
                  pl.BlockSpec((1, CMP_HIDDEN, LANES), lambda i, j: (j, 0, 0))],
        out_specs=pl.BlockSpec((1, 1, NSA_KV_HEADS, nhalf, LANES), lambda i, j: (i, j, 0, 0, 0)),
        out_shape=jax.ShapeDtypeStruct((b, 2, NSA_KV_HEADS, nhalf, LANES), bf16),
        compiler_params=_cparams(("parallel", "parallel")),
        name="compress",
    )(hk, pe_top, pe_bot, w1, w2d)


def _flash_init(m_ref, l_ref, acc_ref):
    m_ref[...] = jnp.full(m_ref.shape, NEG_INF, f32)
    l_ref[...] = jnp.zeros(l_ref.shape, f32)
    acc_ref[...] = jnp.zeros(acc_ref.shape, f32)


def _flash_step(s, v, m_ref, l_ref, acc_ref):
    nk = s.shape[1] // LANES
    cols = [s[:, c * LANES:(c + 1) * LANES] for c in range(nk)]
    mx = cols[0]
    for c in cols[1:]:
        mx = jnp.maximum(mx, c)
    m_old = m_ref[...]
    m_new = jnp.maximum(m_old, jnp.broadcast_to(jnp.max(mx, axis=-1, keepdims=True), m_old.shape))
    alpha = jnp.exp(m_old - m_new)
    ps = [jnp.exp(c - m_new) for c in cols]
    psum = ps[0]
    for p in ps[1:]:
        psum = psum + p
    l_ref[...] = alpha * l_ref[...] + psum
    p = jnp.concatenate([x.astype(bf16) for x in ps], axis=1)
    acc_ref[...] = alpha * acc_ref[...] + _dot(p, v)
    m_ref[...] = m_new


def _flash_finish(l_ref, acc_ref):
    l = jnp.sum(l_ref[...], axis=-1, keepdims=True)
    return acc_ref[...] * (1.0 / jnp.maximum(l, 1e-30))


def _nsa_kernel(q_ref, ks_ref, vs_ref, kw_ref, vw_ref, kc_ref, vc_ref, gate_ref, bc_ref, tb_ref, ov_ref,
                et_ref, o_ref, qaug_ref, kaug_ref, m_ref, l_ref, acc_ref, os_ref, *, seq, ncmp):
    t = T_ATT
    rows = NSA_GROUP * t
    nsel = seq // SEL_BLOCK
    topn = min(N_SELECT, nsel)
    i = pl.program_id(2)
    t0 = i * t
    lane = lax.broadcasted_iota(i32, (t, LANES), 1)
    lo_half = lane < HEAD_DIM

    @pl.when(i == 0)
    def _():
        kaug_ref[:, 0:LANES] = ks_ref[0]
        kaug_ref[:, LANES:2 * LANES] = et_ref[...]

    for g in range(NSA_GROUP):
        qg = q_ref[0, :, (g // 2) * LANES:(g // 2 + 1) * LANES].astype(f32)
        keep = lo_half if g % 2 == 0 else jnp.logical_not(lo_half)
        qaug_ref[g * t:(g + 1) * t, 0:LANES] = jnp.where(keep, qg, 0.0).astype(bf16)
    qs = qaug_ref[:, 0:LANES]

    s = _dot_nt(qs, kc_ref[0, 0, 0]) + bc_ref[0]
    row_t = t0 + (lax.broadcasted_iota(i32, (rows, ncmp), 0) & (t - 1))
    cmp_end = lax.broadcasted_iota(i32, (rows, ncmp), 1) * CMP_STRIDE + (CMP_BLOCK - 1)
    vis = cmp_end <= row_t
    s = jnp.where(vis, s, NEG_INF)
    s = s - jnp.max(s, axis=-1, keepdims=True)
    e = jnp.where(vis, jnp.exp(s), 0.0)
    p_c = e * (1.0 / jnp.maximum(jnp.sum(e, axis=-1, keepdims=True), 1e-30))
    o_c = _dot(p_c.astype(bf16), vc_ref[0, 0, 0])

    t_col = t0 + lax.broadcasted_iota(i32, (t, LANES), 0)
    cur = t_col >> 6
    forced = (lane == 0) | ((lane <= cur) & (lane > cur - N_LOCAL_SEL))
    psum = p_c[0:t]
    for g in range(1, NSA_GROUP):
        psum = psum + p_c[g * t:(g + 1) * t]
    hi = psum.astype(bf16)
    lo = (psum - hi.astype(f32)).astype(bf16)
    imp = _dot(hi, ov_ref[...]) + _dot(lo, ov_ref[...])
    val = jnp.where(forced, FORCED_SCORE, jnp.where(lane <= cur, imp, -1.0))
    val = jnp.where(lane < nsel, val, -2.0)
    cnt = jnp.zeros((t, LANES), f32)
    for j in range(nsel):
        vj = val[:, j:j + 1]
        beats = (vj > val) | ((vj == val) & (lane > j))
        cnt = cnt + jnp.where(beats, 1.0, 0.0)
    mneg = jnp.where(cnt < topn, 0.0, NEG_INF).astype(bf16)
    for g in range(NSA_GROUP):
        qaug_ref[g * t:(g + 1) * t, LANES:2 * LANES] = mneg

    _flash_init(m_ref, l_ref, acc_ref)

    def sel_body(kt, carry):
        k = kaug_ref[pl.ds(kt * t, t), :]
        s = _dot_nt(qaug_ref[...], k) + tb_ref[0, jnp.minimum(i - kt, 2)]
        _flash_step(s, vs_ref[0, pl.ds(kt * t, t), :], m_ref, l_ref, acc_ref)
        return carry

    lax.fori_loop(0, i + 1, sel_body, 0)
    os_ref[...] = _flash_finish(l_ref, acc_ref)

    _flash_init(m_ref, l_ref, acc_ref)
    nwin = WINDOW // t

    def win_body(kt, carry):
        d = i - kt
        kind = jnp.where(d == nwin, 3, d)
        k = kw_ref[0, pl.ds(kt * t, t), :]
        s = _dot_nt(qs, k) + tb_ref[0, kind]
        _flash_step(s, vw_ref[0, pl.ds(kt * t, t), :], m_ref, l_ref, acc_ref)
        return carry

    lax.fori_loop(jnp.maximum(i - nwin, 0), i + 1, win_body, 0)
    o_w = _flash_finish(l_ref, acc_ref)
    o_s = os_ref[...]

    gates = gate_ref[0, 0]
    outs = []
    for g in range(NSA_GROUP):
        sl = slice(g * t, (g + 1) * t)
        outs.append(gates[:, g:g + 1] * o_c[sl]
                    + gates[:, NSA_GROUP + g:NSA_GROUP + g + 1] * o_s[sl]
                    + gates[:, 2 * NSA_GROUP + g:2 * NSA_GROUP + g + 1] * o_w[sl])
    for j in range(NSA_GROUP // 2):
        o_ref[0, :, j * LANES:(j + 1) * LANES] = jnp.where(lo_half, outs[2 * j], outs[2 * j + 1]).astype(o_ref.dtype)


def _nsa(qkv, cmpkv, gates, bias_c, tbias, ov, et):
    b, s, _ = qkv.shape
    ncmp = cmpkv.shape[3]
    t = T_ATT
    rows = NSA_GROUP * t
    qw = NSA_GROUP * HEAD_DIM
    kv_spec = lambda col: pl.BlockSpec((1, s, LANES), lambda bi, h, i, col=col: (bi, 0, col + h))
    cmp_spec = lambda kv: pl.BlockSpec((1, 1, 1, ncmp, LANES), lambda bi, h, i, kv=kv: (bi, kv, h, 0, 0))
    base = QKV_TILES_KV
    return pl.pallas_call(
        functools.partial(_nsa_kernel, seq=s, ncmp=ncmp),
        grid=(b, NSA_KV_HEADS, s // t),
        in_specs=[pl.BlockSpec((1, t, qw), lambda bi, h, i: (bi, i, h)),
                  kv_spec(base), kv_spec(base + 2), kv_spec(base + 4), kv_spec(base + 6),
                  cmp_spec(0), cmp_spec(1),
                  pl.BlockSpec((1, 1, t, LANES), lambda bi, h, i: (bi, h, i, 0)),
                  pl.BlockSpec((1, rows, ncmp), lambda bi, h, i: (h, i, 0)),
                  pl.BlockSpec((1, 4, rows, t), lambda bi, h, i: (h, 0, 0, 0)),
                  pl.BlockSpec((ncmp, LANES), lambda bi, h, i: (0, 0)),
                  pl.BlockSpec((s, LANES), lambda bi, h, i: (0, 0))],
        out_specs=pl.BlockSpec((1, t, qw), lambda bi, h, i: (bi, i, h)),
        out_shape=jax.ShapeDtypeStruct((b, s, NSA_Q_W), bf16),
        scratch_shapes=[pltpu.VMEM((rows, 2 * LANES), bf16),
                        pltpu.VMEM((s, 2 * LANES), bf16),
                        pltpu.VMEM((rows, LANES), f32),
                        pltpu.VMEM((rows, LANES), f32),
                        pltpu.VMEM((rows, LANES), f32),
                        pltpu.VMEM((rows, LANES), f32)],
        compiler_params=_cparams(("parallel", "parallel", "arbitrary")),
        name="nsa",
    )(qkv, qkv, qkv, qkv, qkv, cmpkv, cmpkv, gates, bias_c, tbias, ov, et)


def _fox_kernel(q_ref, k_ref, v_ref, qx_ref, kx_ref, o_ref, qaug_ref, kaug_ref, m_ref, l_ref, acc_ref):
    t = T_ATT
    i = pl.program_id(2)
    lane = lax.broadcasted_iota(i32, (t, LANES), 1)
    lo_half = lane < HEAD_DIM

    @pl.when(i == 0)
    def _():
        kaug_ref[:, 0:LANES] = k_ref[0]
        kaug_ref[:, LANES:2 * LANES] = kx_ref[0]

    q = q_ref[0].astype(f32)
    qx = qx_ref[0].astype(f32)
    qaug_ref[0:t, 0:LANES] = jnp.where(lo_half, q, 0.0).astype(bf16)
    qaug_ref[t:2 * t, 0:LANES] = jnp.where(lo_half, 0.0, q).astype(bf16)
    qaug_ref[0:t, LANES:2 * LANES] = jnp.where(lane < XCOLS, qx, 0.0).astype(bf16)
    qaug_ref[t:2 * t, LANES:2 * LANES] = jnp.where((lane >= XCOLS) & (lane < 2 * XCOLS), qx, 0.0).astype(bf16)
    _flash_init(m_ref, l_ref, acc_ref)

    def body(kt, carry):
        s = _dot_nt(qaug_ref[...], kaug_ref[pl.ds(kt * t, t), :])
        _flash_step(s, v_ref[0, pl.ds(kt * t, t), :], m_ref, l_ref, acc_ref)
        return carry

    lax.fori_loop(0, i, body, 0)
    s = _dot_nt(qaug_ref[...], kaug_ref[pl.ds(i * t, t), :])
    row = lax.broadcasted_iota(i32, (2 * t, t), 0) & (t - 1)
    s = jnp.where(lax.broadcasted_iota(i32, (2 * t, t), 1) <= row, s, NEG_INF)
    _flash_step(s, v_ref[0, pl.ds(i * t, t), :], m_ref, l_ref, acc_ref)
    o = _flash_finish(l_ref, acc_ref)
    o_ref[0] = jnp.where(lo_half, o[0:t], o[t:2 * t]).astype(o_ref.dtype)


def _fox(qkv, qx, kx):
    b, s, _ = qkv.shape
    t = T_ATT
    base = QKV_TILES_FOX
    return pl.pallas_call(
        _fox_kernel,
        grid=(b, FOX_PAIRS, s // t),
        in_specs=[pl.BlockSpec((1, t, LANES), lambda bi, p, i: (bi, i, base + p)),
                  pl.BlockSpec((1, s, LANES), lambda bi, p, i: (bi, 0, base + FOX_PAIRS + p)),
                  pl.BlockSpec((1, s, LANES), lambda bi, p, i: (bi, 0, base + 2 * FOX_PAIRS + p)),
                  pl.BlockSpec((1, t, LANES), lambda bi, p, i: (bi, i, p)),
                  pl.BlockSpec((1, s, LANES), lambda bi, p, i: (bi, 0, p))],
        out_specs=pl.BlockSpec((1, t, LANES), lambda bi, p, i: (bi, i, p)),
        out_shape=jax.ShapeDtypeStruct((b, s, FOX_W), bf16),
        scratch_shapes=[pltpu.VMEM((2 * t, 2 * LANES), bf16),
                        pltpu.VMEM((s, 2 * LANES), bf16),
                        pltpu.VMEM((2 * t, LANES), f32),
                        pltpu.VMEM((2 * t, LANES), f32),
                        pltpu.VMEM((2 * t, LANES), f32)],
        compiler_params=_cparams(("parallel", "parallel", "arbitrary")),
        name="fox",
    )(qkv, qkv, qkv, qx, kx)


def _merge_kernel(yn_ref, yf_ref, xb_ref, x_ref, wn_ref, wf_ref, wmg_ref, wo_ref, g_ref, b_ref,
                  xo_ref, xbo_ref):
    mg = _dot(xb_ref[...], wmg_ref[...])
    merged = (_sigmoid(mg[:, 0:D_MODEL]) * _dot(yn_ref[...], wn_ref[...])
              + _sigmoid(mg[:, D_MODEL:2 * D_MODEL]) * _dot(yf_ref[...], wf_ref[...]))
    hmix = _dot(merged.astype(bf16), wo_ref[...])
    xn = _layer_norm(DN_ALPHA * x_ref[...] + hmix, g_ref[...], b_ref[...])
    xo_ref[...] = xn
    xbo_ref[...] = xn.astype(bf16)


def _merge(yn, yf, xb, x, wn, wf, wmg, wo, g, bb):
    m = x.shape[0]
    tm = TM_MERGE
    row = lambda w: pl.BlockSpec((tm, w), lambda i: (i, 0))
    full = lambda a: pl.BlockSpec(a.shape, lambda i: (0, 0))
    return pl.pallas_call(
        _merge_kernel,
        grid=(m // tm,),
        in_specs=[row(NSA_Q_W), row(FOX_W), row(D_MODEL), row(D_MODEL),
                  full(wn), full(wf), full(wmg), full(wo), full(g), full(bb)],
        out_specs=[row(D_MODEL), row(D_MODEL)],
        out_shape=[jax.ShapeDtypeStruct((m, D_MODEL), f32), jax.ShapeDtypeStruct((m, D_MODEL), bf16)],
        compiler_params=_cparams(("parallel",)),
        name="merge",
    )(yn, yf, xb, x, wn, wf, wmg, wo, g, bb)


def _ffn_kernel(xb_ref, x_ref, wg_ref, wu_ref, wd_ref, g_ref, b_ref, xo_ref, xbo_ref, acc_ref, *, nf):
    f = pl.program_id(1)

    @pl.when(f == 0)
    def _():
        acc_ref[...] = jnp.zeros_like(acc_ref)

    xb = xb_ref[...]
    gate = _dot(xb, wg_ref[...])
    up = _dot(xb, wu_ref[...])
    act = (gate * _sigmoid(gate) * up).astype(bf16)
    acc_ref[...] += _dot(act, wd_ref[...])

    @pl.when(f == nf - 1)
    def _():
        xn = _layer_norm(DN_ALPHA * x_ref[...] + acc_ref[...], g_ref[...], b_ref[...])
        xo_ref[...] = xn
        xbo_ref[...] = xn.astype(bf16)


def _ffn(xb, x, wg, wu, wd, g, bb):
    m = x.shape[0]
    dff = wg.shape[1]
    tm, tf = TM_FFN, TF_FFN
    nf = dff // tf
    row = pl.BlockSpec((tm, D_MODEL), lambda i, f: (i, 0))
    vec = pl.BlockSpec((1, D_MODEL), lambda i, f: (0, 0))
    return pl.pallas_call(
        functools.partial(_ffn_kernel, nf=nf),
        grid=(m // tm, nf),
        in_specs=[row, row,
                  pl.BlockSpec((D_MODEL, tf), lambda i, f: (0, f)),
                  pl.BlockSpec((D_MODEL, tf), lambda i, f: (0, f)),
                  pl.BlockSpec((tf, D_MODEL), lambda i, f: (f, 0)),
                  vec, vec],
        out_specs=[row, row],
        out_shape=[jax.ShapeDtypeStruct((m, D_MODEL), f32), jax.ShapeDtypeStruct((m, D_MODEL), bf16)],
        scratch_shapes=[pltpu.VMEM((tm, D_MODEL), f32)],
        compiler_params=_cparams(("parallel", "arbitrary")),
        name="ffn",
    )(xb, x, wg, wu, wd, g, bb)


def _router_kernel(x_ref, r_ref, tri_ref, gate_ref, rank_ref, cnt_ref, *, seq):
    tb = tri_ref.shape[0]
    x = x_ref[0]
    xh = x.astype(bf16)
    xl = (x - xh.astype(f32)).astype(bf16)
    r = r_ref[...]
    rh = r.astype(bf16)
    rl = (r - rh.astype(f32)).astype(bf16)
    logits = _dot(xh, rh) + _dot(xh, rl) + _dot(xl, rh)
    lane = lax.broadcasted_iota(i32, (seq, LANES), 1).astype(f32)
    low = -3.0e38
    lg = jnp.where(lane < N_EXPERTS, logits, low)
    m1 = jnp.max(lg, axis=-1, keepdims=True)
    i1 = jnp.min(jnp.where(lg == m1, lane, float(LANES)), axis=-1, keepdims=True)
    lg2 = jnp.where(lane == i1, low, lg)
    m2 = jnp.max(lg2, axis=-1, keepdims=True)
    i2 = jnp.min(jnp.where(lg2 == m2, lane, float(LANES)), axis=-1, keepdims=True)
    e2 = jnp.exp(m2 - m1)
    den = 1.0 + e2
    gate_ref[0] = jnp.where(lane == i1, 1.0 / den, jnp.where(lane == i2, e2 / den, 0.0))
    sel = (lane == i1) | (lane == i2)
    selb = jnp.where(sel, 1.0, 0.0).astype(bf16)
    carry = jnp.zeros((1, LANES), f32)
    for blk in range(seq // tb):
        sl = slice(blk * tb, (blk + 1) * tb)
        c = _dot(tri_ref[...], selb[sl]) + carry
        carry = c[tb - 1:tb, :]
        rank_ref[0, sl, :] = jnp.where(sel[sl], c - 1.0, -1.0)
    cnt_ref[0] = carry.astype(i32)


def _router(x3, router_pad, tri):
    b, s, _ = x3.shape
    return pl.pallas_call(
        functools.partial(_router_kernel, seq=s),
        grid=(b,),
        in_specs=[pl.BlockSpec((1, s, D_MODEL), lambda i: (i, 0, 0)),
                  pl.BlockSpec((D_MODEL, LANES), lambda i: (0, 0)),
                  pl.BlockSpec(tri.shape, lambda i: (0, 0))],
        out_specs=[pl.BlockSpec((1, s, LANES), lambda i: (i, 0, 0)),
                   pl.BlockSpec((1, s, LANES), lambda i: (i, 0, 0)),
                   pl.BlockSpec((1, 1, LANES), lambda i: (i, 0, 0))],
        out_shape=[jax.ShapeDtypeStruct((b, s, LANES), f32),
                   jax.ShapeDtypeStruct((b, s, LANES), f32),
                   jax.ShapeDtypeStruct((b, 1, LANES), i32)],
        compiler_params=_cparams(("parallel",)),
        name="router",
    )(x3, router_pad, tri)


def _moe_kernel(cnt_ref, xb_ref, rankt_ref, gatet_ref, ranke_ref, wg_ref, wu_ref, wd_ref, y_ref,
                xg_ref, acc_ref, *, seq, nf):
    tr = TR_MOE
    c = pl.program_id(0)
    e = pl.program_id(1)
    f = pl.program_id(2)
    nsub = (cnt_ref[c * N_EXPERTS + e] + tr - 1) // tr

    def onehot_rows(s):
        rk = rankt_ref[0, pl.ds(e, 1), :]
        want = (s * tr + lax.broadcasted_iota(i32, (tr, seq), 0)).astype(f32)
        return rk == want

    @pl.when((e == 0) & (f == 0))
    def _():
        y_ref[...] = jnp.zeros_like(y_ref)

    @pl.when(f == 0)
    def _():
        def gather(s, carry):
            p = jnp.where(onehot_rows(s), 1.0, 0.0).astype(bf16)
            xg_ref[pl.ds(s * tr, tr), :] = _dot(p, xb_ref[0]).astype(bf16)
            acc_ref[pl.ds(s * tr, tr), :] = jnp.zeros((tr, D_MODEL), f32)
            return carry
        lax.fori_loop(0, nsub, gather, 0)

    def hidden(s, carry):
        xs = xg_ref[pl.ds(s * tr, tr), :]
        gate = _dot(xs, wg_ref[0])
        up = _dot(xs, wu_ref[0])
        act = (gate * _sigmoid(gate) * up).astype(bf16)
        acc_ref[pl.ds(s * tr, tr), :] += _dot(act, wd_ref[0])
        return carry

    lax.fori_loop(0, nsub, hidden, 0)

    @pl.when(f == nf - 1)
    def _():
        def combine(s, carry):
            grow = gatet_ref[0, pl.ds(e, 1), :]
            w = jnp.sum(jnp.where(onehot_rows(s), grow, 0.0), axis=-1, keepdims=True)
            z = (acc_ref[pl.ds(s * tr, tr), :] * w).astype(bf16)
            want = (s * tr + lax.broadcasted_iota(i32, (512, tr), 1)).astype(f32)
            for jb in range(seq // 512):
                rc = ranke_ref[0, 0, jb * 512:(jb + 1) * 512, :]
                pt = jnp.where(rc == want, 1.0, 0.0).astype(bf16)
                y_ref[0, jb * 512:(jb + 1) * 512, :] += _dot(pt, z)
            return carry
        lax.fori_loop(0, nsub, combine, 0)


def _moe(counts, xb3, rankt, gatet, ranke, wg, wu, wd):
    b, s, _ = xb3.shape
    dff = wg.shape[2]
    nf = dff // TF_MOE
    grid_spec = pltpu.PrefetchScalarGridSpec(
        num_scalar_prefetch=1,
        grid=(b, N_EXPERTS, nf),
        in_specs=[pl.BlockSpec((1, s, D_MODEL), lambda c, e, f, cnt: (c, 0, 0)),
                  pl.BlockSpec((1, N_EXPERTS, s), lambda c, e, f, cnt: (c, 0, 0)),
                  pl.BlockSpec((1, N_EXPERTS, s), lambda c, e, f, cnt: (c, 0, 0)),
                  pl.BlockSpec((1, 1, s, 1), lambda c, e, f, cnt: (c, e, 0, 0)),
                  pl.BlockSpec((1, D_MODEL, TF_MOE), lambda c, e, f, cnt: (e, 0, f)),
                  pl.BlockSpec((1, D_MODEL, TF_MOE), lambda c, e, f, cnt: (e, 0, f)),
                  pl.BlockSpec((1, TF_MOE, D_MODEL), lambda c, e, f, cnt: (e, f, 0))],
        out_specs=pl.BlockSpec((1, s, D_MODEL), lambda c, e, f, cnt: (c, 0, 0)),
        scratch_shapes=[pltpu.VMEM((s, D_MODEL), bf16), pltpu.VMEM((s, D_MODEL), f32)],
    )
    return pl.pallas_call(
        functools.partial(_moe_kernel, seq=s, nf=nf),
        grid_spec=grid_spec,
        out_shape=jax.ShapeDtypeStruct((b, s, D_MODEL), f32),
        compiler_params=_cparams(("parallel", "arbitrary", "arbitrary")),
        name="moe",
    )(counts, xb3, rankt, gatet, ranke, wg, wu, wd)


def _resln_kernel(x_ref, y_ref, g_ref, b_ref, xo_ref, xbo_ref):
    xn = _layer_norm(DN_ALPHA * x_ref[...] + y_ref[...], g_ref[...], b_ref[...])
    xo_ref[...] = xn
    xbo_ref[...] = xn.astype(bf16)


def _resln(x, y, g, bb):
    m = x.shape[0]
    tm = TM_FFN
    row = pl.BlockSpec((tm, D_MODEL), lambda i: (i, 0))
    vec = pl.BlockSpec((1, D_MODEL), lambda i: (0, 0))
    return pl.pallas_call(
        _resln_kernel,
        grid=(m // tm,),
        in_specs=[row, row, vec, vec],
        out_specs=[row, row],
        out_shape=[jax.ShapeDtypeStruct((m, D_MODEL), f32), jax.ShapeDtypeStruct((m, D_MODEL), bf16)],
        compiler_params=_cparams(("parallel",)),
        name="resln",
    )(x, y, g, bb)


def _t5_bucket(dist):
    n = jnp.maximum(dist, 0)
    max_exact = REL_BUCKETS // 2
    large = max_exact + (jnp.log(jnp.maximum(n, 1).astype(f32) / max_exact)
                         / math.log(REL_MAX_DIST / max_exact) * (REL_BUCKETS - max_exact)).astype(i32)
    large = jnp.minimum(large, REL_BUCKETS - 1)
    return jnp.where(n < max_exact, n, large)


def _windows(vec, starts, size):
    return jax.vmap(lambda st: lax.dynamic_slice_in_dim(vec, st, size, axis=-1))(starts)


def _bias_tables(rel_bias, seq):
    t = T_ATT
    rows = NSA_GROUP * t
    ncmp = seq // CMP_STRIDE
    ndist = seq + WINDOW + t
    dist = jnp.arange(ndist)
    onehot = (_t5_bucket(dist)[None, :] == jnp.arange(REL_BUCKETS)[:, None]).astype(f32)
    bd = jnp.einsum("kh,kd->hd", rel_bias.astype(f32), onehot, precision=lax.Precision.HIGHEST)
    rev = bd[:, ::-1]
    r_idx = jnp.arange(t)

    def toeplitz(off):
        starts = (ndist - 1) - (off + r_idx)
        return _windows(rev, starts, t).transpose(1, 0, 2)

    d0 = r_idx[:, None] - r_idx[None, :]
    front = jnp.broadcast_to(bd[:, :1], (NSA_HEADS, t))
    diag = jnp.where(d0 >= 0, _windows(jnp.concatenate([front, bd], axis=1)[:, ::-1],
                                       (ndist + t - 1) - (t + r_idx), t).transpose(1, 0, 2), NEG_INF)
    kinds = jnp.stack([diag, toeplitz(t), toeplitz(2 * t),
                       jnp.where(d0 < 0, toeplitz(WINDOW), NEG_INF)])
    tbias = kinds.reshape(4, NSA_KV_HEADS, rows, t).transpose(1, 0, 2, 3)
    pad = CMP_STRIDE * (ncmp - 1) + CMP_BLOCK - 1
    bdp = jnp.concatenate([jnp.broadcast_to(bd[:, :1], (NSA_HEADS, pad)), bd], axis=1)
    starts = pad - (jnp.arange(ncmp) * CMP_STRIDE + CMP_BLOCK - 1)
    bc = _windows(bdp, starts, seq)
    bc = bc.reshape(ncmp, NSA_KV_HEADS, NSA_GROUP, seq // t, t).transpose(1, 3, 2, 4, 0)
    return tbias, bc.reshape(NSA_KV_HEADS, seq * NSA_GROUP, ncmp)


def _selection_constants(seq):
    ncmp = seq // CMP_STRIDE
    nsel = seq // SEL_BLOCK
    c0 = np.arange(ncmp)[:, None] * CMP_STRIDE
    s0 = np.arange(LANES)[None, :] * SEL_BLOCK
    ov = np.maximum(np.minimum(c0 + CMP_BLOCK, s0 + SEL_BLOCK) - np.maximum(c0, s0), 0) / CMP_BLOCK
    ov[ncmp - 1, :] = 0.0
    ov[:, nsel:] = 0.0
    et = (np.arange(seq)[:, None] // SEL_BLOCK == np.arange(LANES)[None, :]).astype(np.float32)
    return jnp.asarray(ov, bf16), jnp.asarray(et, bf16)


def _fox_placement():
    xw = FOX_PAIRS * LANES
    pq = np.zeros((3, LANES, xw), np.float32)
    pk = np.zeros((3, LANES, xw), np.float32)
    oq = np.zeros((1, xw), np.float32)
    ok = np.zeros((1, xw), np.float32)
    for p in range(FOX_PAIRS):
        for hh in range(2):
            src = FGATE_LANE + 2 * p + hh
            base = p * LANES + hh * XCOLS
            for part in range(3):
                pk[part, src, base + part] = -1.0
                pq[part, src, base + 3 + part] = 1.0
                oq[0, base + part] = 1.0
                ok[0, base + 3 + part] = 1.0
    return jnp.asarray(pq, bf16), jnp.asarray(pk, bf16), jnp.asarray(oq), jnp.asarray(ok)


def _layer_weights(w_in, layer_pe, w1, w2, f_bias):
    offs = np.cumsum((NSA_Q_W, 6 * 2 * HEAD_DIM, 3 * NSA_HEADS, 3 * FOX_W, FOX_HEADS, 2 * D_MODEL))
    kv0, g0, fx0, ff0, mg0 = offs[0], offs[1], offs[2], offs[3], offs[4]
    scale = HEAD_DIM ** -0.5
    kvw = NSA_KV_HEADS * HEAD_DIM
    w_kv = w_in[:, kv0:g0]
    dup = np.concatenate([np.tile(np.arange(h * HEAD_DIM, (h + 1) * HEAD_DIM), 2) for h in range(NSA_KV_HEADS)])
    w_kvdup = jnp.concatenate([w_kv[:, a * kvw:(a + 1) * kvw][:, dup] for a in range(2, 6)], axis=1)
    w_qkv = jnp.concatenate([w_in[:, :NSA_Q_W] * scale, w_kvdup, w_in[:, fx0:fx0 + FOX_W] * scale,
                             w_in[:, fx0 + FOX_W:ff0]], axis=1).astype(bf16)
    zeros = lambda n: jnp.zeros((D_MODEL, n), w_in.dtype)
    gate_cols = lambda h: w_in[:, g0 + np.array([(h * NSA_GROUP + g) * 3 + br
                                                 for br in range(3) for g in range(NSA_GROUP)])]
    ng = 3 * NSA_GROUP
    w_aux = jnp.concatenate([w_kv[:, :2 * kvw],
                             gate_cols(0), zeros(FGATE_LANE - ng), w_in[:, ff0:mg0],
                             zeros(LANES - FGATE_LANE - FOX_HEADS),
                             gate_cols(1), zeros(LANES - ng)], axis=1).astype(bf16)
    fb_row = jnp.zeros((1, LANES), f32).at[0, FGATE_LANE:FGATE_LANE + FOX_HEADS].set(f_bias.astype(f32))
    half = CMP_STRIDE * HEAD_DIM
    pe_flat = layer_pe.astype(f32).reshape(2, 1, CMP_BLOCK * HEAD_DIM)
    pe_top, pe_bot = pe_flat[:, :, :half], pe_flat[:, :, half:]
    w2d = jnp.concatenate([w2, w2], axis=-1).astype(bf16)
    return w_qkv, w_aux, w_in[:, mg0:].astype(bf16), fb_row, pe_top, pe_bot, w1.astype(bf16), w2d


def kernel(x, w_in, nsa_cmp_pe, nsa_cmp_w1, nsa_cmp_w2, fox_f_bias, w_nsa_branch, w_fox_branch, w_out,
           rel_bias, ln1_g, ln1_b, ln2_g, ln2_b, dense_w_gate, dense_w_up, dense_w_down, moe_router,
           moe_w_gate, moe_w_up, moe_w_down):
    b, s, d = x.shape
    m = b * s
    nhalf = s // CMP_STRIDE
    tbias, bias_c = _bias_tables(rel_bias, s)
    ov, et = _selection_constants(s)
    pq, pk, oq, ok = _fox_placement()
    tri128 = jnp.asarray(np.tril(np.ones((LANES, LANES), np.float32)), bf16)
    tri256 = jnp.asarray(np.tril(np.ones((256, 256), np.float32)), bf16)

    xf = x.reshape(m, d).astype(f32)
    xb = xf.astype(bf16)
    for layer in range(DEPTH):
        w_qkv, w_aux, w_mg, fb_row, pe_top, pe_bot, w1, w2d = _layer_weights(
            w_in[layer], nsa_cmp_pe[layer], nsa_cmp_w1[layer], nsa_cmp_w2[layer], fox_f_bias[layer])
        qkv = _proj(xb, w_qkv, bf16).reshape(b, s, QKV_W)
        aux = _proj(xb, w_aux, f32).reshape(b, s, AUX_W)
        gates, qx, kx = _gateprep(aux, fb_row, tri128, pq, pk, oq, ok)
        hk = aux[:, :, :2 * LANES].reshape(b, nhalf, CMP_STRIDE, 2, NSA_KV_HEADS, HEAD_DIM)
        hk = hk.transpose(0, 3, 4, 1, 2, 5).reshape(b, 2, NSA_KV_HEADS, nhalf, CMP_STRIDE * HEAD_DIM)
        cmpkv = _compress(hk, pe_top, pe_bot, w1, w2d)
        y_nsa = _nsa(qkv, cmpkv, gates, bias_c, tbias, ov, et).reshape(m, NSA_Q_W)
        y_fox = _fox(qkv, qx, kx).reshape(m, FOX_W)
        xf, xb = _merge(y_nsa, y_fox, xb, xf, w_nsa_branch[layer].astype(bf16),
                        w_fox_branch[layer].astype(bf16), w_mg, w_out[layer].astype(bf16),
                        ln1_g[layer].reshape(1, d), ln1_b[layer].reshape(1, d))
        j = layer // 2
        g2, b2 = ln2_g[layer].reshape(1, d), ln2_b[layer].reshape(1, d)
        if layer % 2 == 0:
            xf, xb = _ffn(xb, xf, dense_w_gate[j].astype(bf16), dense_w_up[j].astype(bf16),
                          dense_w_down[j].astype(bf16), g2, b2)
        else:
            router_pad = jnp.zeros((d, LANES), f32).at[:, :N_EXPERTS].set(moe_router[j].astype(f32))
            gate, rank, cnt = _router(xf.reshape(b, s, d), router_pad, tri256)
            rankt = rank[:, :, :N_EXPERTS].transpose(0, 2, 1)
            gatet = gate[:, :, :N_EXPERTS].transpose(0, 2, 1)
            counts = cnt[:, 0, :N_EXPERTS].reshape(-1)
            y = _moe(counts, xb.reshape(b, s, d), rankt, gatet, rankt[..., None],
                     moe_w_gate[j].astype(bf16), moe_w_up[j].astype(bf16), moe_w_down[j].astype(bf16))
            xf, xb = _resln(xf, y.reshape(m, d), g2, b2)
    return xf.reshape(b, s, d).astype(x.dtype)
```

```python
import functools
import math

import numpy as np
import jax
import jax.numpy as jnp
from jax import lax
from jax.experimental import pallas as pl
from jax.experimental.pallas import tpu as pltpu

f32 = jnp.float32
bf16 = jnp.bfloat16
i32 = jnp.int32

D_MODEL = 1024
HEAD_DIM = 64
LANES = 128
NSA_HEADS = 8
NSA_KV_HEADS = 2
NSA_GROUP = NSA_HEADS // NSA_KV_HEADS
FOX_HEADS = 8
FOX_PAIRS = FOX_HEADS // 2
CMP_BLOCK = 32
CMP_STRIDE = 16
CMP_HIDDEN = 128
SEL_BLOCK = 64
N_SELECT = 16
N_LOCAL_SEL = 2
WINDOW = 512
REL_BUCKETS = 32
REL_MAX_DIST = 128
N_EXPERTS = 8
DEPTH = 4
DN_ALPHA = (2 * DEPTH) ** 0.25
LN_EPS = 1e-5
FORCED_SCORE = 1e4
NEG_INF = -1e30
LOG2E = math.log2(math.e)

NSA_Q_W = NSA_HEADS * HEAD_DIM
FOX_W = FOX_HEADS * HEAD_DIM
QKV_TILES_KV = NSA_Q_W // LANES
QKV_TILES_FOX = QKV_TILES_KV + 4 * NSA_KV_HEADS
QKV_W = (QKV_TILES_FOX + 3 * FOX_PAIRS) * LANES
AUX_W = 4 * LANES
FGATE_LANE = 24
XCOLS = 6

T_ATT = 256
TM_PROJ = 512
TM_MERGE = 256
TM_FFN = 512
TF_FFN = 256
TR_MOE = 256
TF_MOE = 512
VMEM_LIMIT = 56 * 1024 * 1024


def _cparams(sem):
    return pltpu.CompilerParams(dimension_semantics=sem, vmem_limit_bytes=VMEM_LIMIT)


def _dot(a, b):
    return jnp.dot(a, b, preferred_element_type=f32)


def _dot_nt(a, b):
    return lax.dot_general(a, b, (((1,), (1,)), ((), ())), preferred_element_type=f32)


def _sigmoid(x):
    return 1.0 / (1.0 + jnp.exp(-x))


def _layer_norm(z, g, b):
    mu = jnp.mean(z, axis=-1, keepdims=True)
    zc = z - mu
    var = jnp.mean(zc * zc, axis=-1, keepdims=True)
    return zc * lax.rsqrt(var + LN_EPS) * g + b


def _split3(x):
    hi = x.astype(bf16)
    r1 = x - hi.astype(f32)
    mid = r1.astype(bf16)
    lo = (r1 - mid.astype(f32)).astype(bf16)
    return hi, mid, lo


def _proj_kernel(x_ref, w_ref, o_ref):
    o_ref[...] = _dot(x_ref[...], w_ref[...]).astype(o_ref.dtype)


def _proj(xb, w, out_dtype):
    m, k = xb.shape
    n = w.shape[1]
    return pl.pallas_call(
        _proj_kernel,
        grid=(m // TM_PROJ,),
        in_specs=[pl.BlockSpec((TM_PROJ, k), lambda i: (i, 0)),
                  pl.BlockSpec((k, n), lambda i: (0, 0))],
        out_specs=pl.BlockSpec((TM_PROJ, n), lambda i: (i, 0)),
        out_shape=jax.ShapeDtypeStruct((m, n), out_dtype),
        compiler_params=_cparams(("parallel",)),
        name="proj",
    )(xb, w)


def _gateprep_kernel(a_ref, fb_ref, tri_ref, pq_ref, pk_ref, oq_ref, ok_ref, g_ref, qx_ref, kx_ref, *, seq):
    tb = LANES
    tri = tri_ref[...]
    carry = jnp.zeros((1, LANES), f32)
    for blk in range(seq // tb):
        sl = slice(blk * tb, (blk + 1) * tb)
        va = a_ref[0, sl, 0:LANES]
        g_ref[0, 0, sl, :] = _sigmoid(va)
        g_ref[0, 1, sl, :] = _sigmoid(a_ref[0, sl, LANES:2 * LANES])
        z = va + fb_ref[...]
        logf = jnp.minimum(z, 0.0) - jnp.log1p(jnp.exp(-jnp.abs(z)))
        hi, mid, lo = _split3(logf)
        c = _dot(tri, hi) + _dot(tri, mid) + _dot(tri, lo) + carry
        carry = c[tb - 1:tb, :]
        chi, cmid, clo = _split3(c * LOG2E)
        qx = _dot(chi, pq_ref[0]) + _dot(cmid, pq_ref[1]) + _dot(clo, pq_ref[2]) + oq_ref[...]
        kx = _dot(chi, pk_ref[0]) + _dot(cmid, pk_ref[1]) + _dot(clo, pk_ref[2]) + ok_ref[...]
        qx_ref[0, sl, :] = qx.astype(bf16)
        kx_ref[0, sl, :] = kx.astype(bf16)


def _gateprep(aux, fb_row, tri, pq, pk, oq, ok):
    b, s, _ = aux.shape
    xw = FOX_PAIRS * LANES
    const2 = lambda a: pl.BlockSpec(a.shape, lambda i: (0, 0))
    const3 = lambda a: pl.BlockSpec(a.shape, lambda i: (0, 0, 0))
    return pl.pallas_call(
        functools.partial(_gateprep_kernel, seq=s),
        grid=(b,),
        in_specs=[pl.BlockSpec((1, s, 2 * LANES), lambda i: (i, 0, 1)),
                  const2(fb_row), const2(tri), const3(pq), const3(pk), const2(oq), const2(ok)],
        out_specs=[pl.BlockSpec((1, 2, s, LANES), lambda i: (i, 0, 0, 0)),
                   pl.BlockSpec((1, s, xw), lambda i: (i, 0, 0)),
                   pl.BlockSpec((1, s, xw), lambda i: (i, 0, 0))],
        out_shape=[jax.ShapeDtypeStruct((b, 2, s, LANES), f32),
                   jax.ShapeDtypeStruct((b, s, xw), bf16),
                   jax.ShapeDtypeStruct((b, s, xw), bf16)],
        compiler_params=_cparams(("parallel",)),
        name="gateprep",
    )(aux, fb_row, tri, pq, pk, oq, ok)


def _gelu_tanh(x):
    c = math.sqrt(2.0 / math.pi)
    return x * (0.5 * (1.0 + jnp.tanh(c * (x + 0.044715 * (x * x * x)))))


def _compress_kernel(h_ref, pet_ref, peb_ref, w1_ref, w2_ref, o_ref, *, nhalf):
    half_w = CMP_STRIDE * HEAD_DIM
    for h in range(NSA_KV_HEADS):
        hh = h_ref[0, 0, h]
        top = _dot((hh + pet_ref[0]).astype(bf16), w1_ref[0, 0:half_w, :])
        bot = _dot((hh + peb_ref[0]).astype(bf16), w1_ref[0, half_w:2 * half_w, :])
        pre = top + pltpu.roll(bot, nhalf - 1, 0)
        o_ref[0, 0, h] = _dot(_gelu_tanh(pre).astype(bf16), w2_ref[0]).astype(o_ref.dtype)


def _compress(hk, pe_top, pe_bot, w1, w2d):
    b, _, _, nhalf, hw = hk.shape
    return pl.pallas_call(
        functools.partial(_compress_kernel, nhalf=nhalf),
        grid=(b, 2),
        in_specs=[pl.BlockSpec((1, 1, NSA_KV_HEADS, nhalf, hw), lambda i, j: (i, j, 0, 0, 0)),
                  pl.BlockSpec((1, 1, hw), lambda i, j: (j, 0, 0)),
                  pl.BlockSpec((1, 1, hw), lambda i, j: (j, 0, 0)),
                  pl.BlockSpec((1, 2 * hw, CMP_HIDDEN), lambda i, j: (j, 0, 0)),
                  pl.BlockSpec((1, CMP_HIDDEN, LANES), lambda i, j: (j, 0, 0))],
        out_specs=pl.BlockSpec((1, 1, NSA_KV_HEADS, nhalf, LANES), lambda i, j: (i, j, 0, 0, 0)),
        out_shape=jax.ShapeDtypeStruct((b, 2, NSA_KV_HEADS, nhalf, LANES), bf16),
        compiler_params=_cparams(("parallel", "parallel")),
        name="compress",
    )(hk, pe_top, pe_bot, w1, w2d)


def _flash_init(m_ref, l_ref, acc_ref):
    m_ref[...] = jnp.full(m_ref.shape, NEG_INF, f32)
    l_ref[...] = jnp.zeros(l_ref.shape, f32)
    acc_ref[...] = jnp.zeros(acc_ref.shape, f32)


def _flash_step(s, v, m_ref, l_ref, acc_ref):
    nk = s.shape[1] // LANES
    cols = [s[:, c * LANES:(c + 1) * LANES] for c in range(nk)]
    mx = cols[0]
    for c in cols[1:]:
        mx = jnp.maximum(mx, c)
    m_old = m_ref[...]
    m_new = jnp.maximum(m_old, jnp.broadcast_to(jnp.max(mx, axis=-1, keepdims=True), m_old.shape))
    alpha = jnp.exp2(m_old - m_new)
    ps = [jnp.exp2(c - m_new) for c in cols]
    psum = ps[0]
    for p in ps[1:]:
        psum = psum + p
    l_ref[...] = alpha * l_ref[...] + psum
    p = jnp.concatenate([x.astype(bf16) for x in ps], axis=1)
    acc_ref[...] = alpha * acc_ref[...] + _dot(p, v)
    m_ref[...] = m_new


def _flash_loop(first, last, logits_fn, v_fn, s_ref, m_ref, l_ref, acc_ref, last_fix=lambda s: s):
    sa, sb = s_ref.at[0], s_ref.at[1]
    n = last - first + 1
    pairs = (n - 1) // 2

    def step(buf, kt, fix=lambda s: s):
        _flash_step(fix(buf[...]), v_fn(kt), m_ref, l_ref, acc_ref)

    sa[...] = logits_fn(first)

    def body(j, carry):
        kt = first + 2 * j
        sb[...] = logits_fn(kt + 1)
        step(sa, kt)
        sa[...] = logits_fn(kt + 2)
        step(sb, kt + 1)
        return carry

    lax.fori_loop(0, pairs, body, 0)
    two_left = n - 2 * pairs == 2

    @pl.when(two_left)
    def _():
        sb[...] = logits_fn(last)
        step(sa, last - 1)
        step(sb, last, last_fix)

    @pl.when(jnp.logical_not(two_left))
    def _():
        step(sa, last, last_fix)


def _flash_finish(l_ref, acc_ref):
    l = jnp.sum(l_ref[...], axis=-1, keepdims=True)
    return acc_ref[...] * (1.0 / jnp.maximum(l, 1e-30))


def _nsa_kernel(q_ref, ks_ref, vs_ref, kw_ref, vw_ref, kc_ref, vc_ref, gate_ref, bc_ref, tb_ref, ov_ref,
                et_ref, o_ref, qaug_ref, kaug_ref, m_ref, l_ref, acc_ref, os_ref, s_ref, *, seq, ncmp):
    t = T_ATT
    rows = NSA_GROUP * t
    nsel = seq // SEL_BLOCK
    topn = min(N_SELECT, nsel)
    i = pl.program_id(2)
    t0 = i * t
    lane = lax.broadcasted_iota(i32, (t, LANES), 1)
    lo_half = lane < HEAD_DIM

    @pl.when(i == 0)
    def _():
        kaug_ref[:, 0:LANES] = ks_ref[0]
        kaug_ref[:, LANES:2 * LANES] = et_ref[...]

    for g in range(NSA_GROUP):
        qg = q_ref[0, :, (g // 2) * LANES:(g // 2 + 1) * LANES].astype(f32)
        keep = lo_half if g % 2 == 0 else jnp.logical_not(lo_half)
        qaug_ref[g * t:(g + 1) * t, 0:LANES] = jnp.where(keep, qg, 0.0).astype(bf16)
    qs = qaug_ref[:, 0:LANES]

    s = _dot_nt(qs, kc_ref[0, 0, 0]) + bc_ref[0]
    row_t = t0 + (lax.broadcasted_iota(i32, (rows, ncmp), 0) & (t - 1))
    cmp_end = lax.broadcasted_iota(i32, (rows, ncmp), 1) * CMP_STRIDE + (CMP_BLOCK - 1)
    vis = cmp_end <= row_t
    s = jnp.where(vis, s, NEG_INF)
    s = s - jnp.max(s, axis=-1, keepdims=True)
    e = jnp.where(vis, jnp.exp2(s), 0.0)
    p_c = e * (1.0 / jnp.maximum(jnp.sum(e, axis=-1, keepdims=True), 1e-30))
    o_c = _dot(p_c.astype(bf16), vc_ref[0, 0, 0])

    psum = p_c[0:t]
    for g in range(1, NSA_GROUP):
        psum = psum + p_c[g * t:(g + 1) * t]
    hi = psum.astype(bf16)
    lo = (psum - hi.astype(f32)).astype(bf16)
    imp = (_dot_nt(ov_ref[...], hi) + _dot_nt(ov_ref[...], lo))[0:nsel]
    blk = lax.broadcasted_iota(i32, (nsel, t), 0)
    cur = (t0 + lax.broadcasted_iota(i32, (nsel, t), 1)) >> 6
    forced = (blk == 0) | ((blk <= cur) & (blk > cur - N_LOCAL_SEL))
    val = jnp.where(forced, FORCED_SCORE, jnp.where(blk <= cur, imp, -1.0))
    cnt = jnp.zeros((nsel, t), f32)
    for j in range(nsel):
        vj = val[j:j + 1, :]
        beats = (vj > val) | ((vj == val) & (blk > j))
        cnt = cnt + jnp.where(beats, 1.0, 0.0)
    mneg = jnp.where(cnt < topn, 0.0, NEG_INF)
    mneg = jnp.concatenate([mneg, jnp.zeros((LANES - nsel, t), f32)], axis=0).T.astype(bf16)
    for g in range(NSA_GROUP):
        qaug_ref[g * t:(g + 1) * t, LANES:2 * LANES] = mneg

    _flash_init(m_ref, l_ref, acc_ref)

    def sel_logits(kt):
        k = kaug_ref[pl.ds(kt * t, t), :]
        return _dot_nt(qaug_ref[...], k) + tb_ref[0, jnp.minimum(i - kt, 2)]

    _flash_loop(0, i, sel_logits, lambda kt: vs_ref[0, pl.ds(kt * t, t), :], s_ref, m_ref, l_ref, acc_ref)
    os_ref[...] = _flash_finish(l_ref, acc_ref)

    _flash_init(m_ref, l_ref, acc_ref)
    nwin = WINDOW // t

    def win_logits(kt):
        d = i - kt
        kind = jnp.where(d == nwin, 3, d)
        return _dot_nt(qs, kw_ref[0, pl.ds(kt * t, t), :]) + tb_ref[0, kind]

    _flash_loop(jnp.maximum(i - nwin, 0), i, win_logits, lambda kt: vw_ref[0, pl.ds(kt * t, t), :],
                s_ref, m_ref, l_ref, acc_ref)
    o_w = _flash_finish(l_ref, acc_ref)
    o_s = os_ref[...]

    gates = gate_ref[0, 0]
    outs = []
    for g in range(NSA_GROUP):
        sl = slice(g * t, (g + 1) * t)
        outs.append(gates[:, g:g + 1] * o_c[sl]
                    + gates[:, NSA_GROUP + g:NSA_GROUP + g + 1] * o_s[sl]
                    + gates[:, 2 * NSA_GROUP + g:2 * NSA_GROUP + g + 1] * o_w[sl])
    for j in range(NSA_GROUP // 2):
        o_ref[0, :, j * LANES:(j + 1) * LANES] = jnp.where(lo_half, outs[2 * j], outs[2 * j + 1]).astype(o_ref.dtype)


def _nsa(qkv, cmpkv, gates, bias_c, tbias, ov, et):
    b, s, _ = qkv.shape
    ncmp = cmpkv.shape[3]
    t = T_ATT
    rows = NSA_GROUP * t
    qw = NSA_GROUP * HEAD_DIM
    kv_spec = lambda col: pl.BlockSpec((1, s, LANES), lambda bi, h, i, col=col: (bi, 0, col + h))
    cmp_spec = lambda kv: pl.BlockSpec((1, 1, 1, ncmp, LANES), lambda bi, h, i, kv=kv: (bi, kv, h, 0, 0))
    base = QKV_TILES_KV
    return pl.pallas_call(
        functools.partial(_nsa_kernel, seq=s, ncmp=ncmp),
        grid=(b, NSA_KV_HEADS, s // t),
        in_specs=[pl.BlockSpec((1, t, qw), lambda bi, h, i: (bi, i, h)),
                  kv_spec(base), kv_spec(base + 2), kv_spec(base + 4), kv_spec(base + 6),
                  cmp_spec(0), cmp_spec(1),
                  pl.BlockSpec((1, 1, t, LANES), lambda bi, h, i: (bi, h, i, 0)),
                  pl.BlockSpec((1, rows, ncmp), lambda bi, h, i: (h, i, 0)),
                  pl.BlockSpec((1, 4, rows, t), lambda bi, h, i: (h, 0, 0, 0)),
                  pl.BlockSpec((LANES, ncmp), lambda bi, h, i: (0, 0)),
                  pl.BlockSpec((s, LANES), lambda bi, h, i: (0, 0))],
        out_specs=pl.BlockSpec((1, t, qw), lambda bi, h, i: (bi, i, h)),
        out_shape=jax.ShapeDtypeStruct((b, s, NSA_Q_W), bf16),
        scratch_shapes=[pltpu.VMEM((rows, 2 * LANES), bf16),
                        pltpu.VMEM((s, 2 * LANES), bf16),
                        pltpu.VMEM((rows, LANES), f32),
                        pltpu.VMEM((rows, LANES), f32),
                        pltpu.VMEM((rows, LANES), f32),
                        pltpu.VMEM((rows, LANES), f32),
                        pltpu.VMEM((2, rows, t), f32)],
        compiler_params=_cparams(("parallel", "parallel", "arbitrary")),
        name="nsa",
    )(qkv, qkv, qkv, qkv, qkv, cmpkv, cmpkv, gates, bias_c, tbias, ov, et)


def _fox_kernel(q_ref, k_ref, v_ref, qx_ref, kx_ref, o_ref, qaug_ref, kaug_ref, m_ref, l_ref, acc_ref,
                s_ref):
    t = T_ATT
    i = pl.program_id(2)
    lane = lax.broadcasted_iota(i32, (t, LANES), 1)
    lo_half = lane < HEAD_DIM

    @pl.when(i == 0)
    def _():
        kaug_ref[:, 0:LANES] = k_ref[0]
        kaug_ref[:, LANES:2 * LANES] = kx_ref[0]

    q = q_ref[0].astype(f32)
    qx = qx_ref[0].astype(f32)
    qaug_ref[0:t, 0:LANES] = jnp.where(lo_half, q, 0.0).astype(bf16)
    qaug_ref[t:2 * t, 0:LANES] = jnp.where(lo_half, 0.0, q).astype(bf16)
    qaug_ref[0:t, LANES:2 * LANES] = jnp.where(lane < XCOLS, qx, 0.0).astype(bf16)
    qaug_ref[t:2 * t, LANES:2 * LANES] = jnp.where((lane >= XCOLS) & (lane < 2 * XCOLS), qx, 0.0).astype(bf16)
    _flash_init(m_ref, l_ref, acc_ref)

    def logits(kt):
        return _dot_nt(qaug_ref[...], kaug_ref[pl.ds(kt * t, t), :])

    def causal(s):
        row = lax.broadcasted_iota(i32, (2 * t, t), 0) & (t - 1)
        return jnp.where(lax.broadcasted_iota(i32, (2 * t, t), 1) <= row, s, NEG_INF)

    _flash_loop(0, i, logits, lambda kt: v_ref[0, pl.ds(kt * t, t), :], s_ref, m_ref, l_ref, acc_ref,
                last_fix=causal)
    o = _flash_finish(l_ref, acc_ref)
    o_ref[0] = jnp.where(lo_half, o[0:t], o[t:2 * t]).astype(o_ref.dtype)


def _fox(qkv, qx, kx):
    b, s, _ = qkv.shape
    t = T_ATT
    base = QKV_TILES_FOX
    return pl.pallas_call(
        _fox_kernel,
        grid=(b, FOX_PAIRS, s // t),
        in_specs=[pl.BlockSpec((1, t, LANES), lambda bi, p, i: (bi, i, base + p)),
                  pl.BlockSpec((1, s, LANES), lambda bi, p, i: (bi, 0, base + FOX_PAIRS + p)),
                  pl.BlockSpec((1, s, LANES), lambda bi, p, i: (bi, 0, base + 2 * FOX_PAIRS + p)),
                  pl.BlockSpec((1, t, LANES), lambda bi, p, i: (bi, i, p)),
                  pl.BlockSpec((1, s, LANES), lambda bi, p, i: (bi, 0, p))],
        out_specs=pl.BlockSpec((1, t, LANES), lambda bi, p, i: (bi, i, p)),
        out_shape=jax.ShapeDtypeStruct((b, s, FOX_W), bf16),
        scratch_shapes=[pltpu.VMEM((2 * t, 2 * LANES), bf16),
                        pltpu.VMEM((s, 2 * LANES), bf16),
                        pltpu.VMEM((2 * t, LANES), f32),
                        pltpu.VMEM((2 * t, LANES), f32),
                        pltpu.VMEM((2 * t, LANES), f32),
                        pltpu.VMEM((2, 2 * t, t), f32)],
        compiler_params=_cparams(("parallel", "parallel", "arbitrary")),
        name="fox",
    )(qkv, qkv, qkv, qx, kx)


def _merge_kernel(yn_ref, yf_ref, xb_ref, x_ref, wn_ref, wf_ref, wmg_ref, wo_ref, g_ref, b_ref,
                  xo_ref, xbo_ref):
    mg = _dot(xb_ref[...], wmg_ref[...])
    merged = (_sigmoid(mg[:, 0:D_MODEL]) * _dot(yn_ref[...], wn_ref[...])
              + _sigmoid(mg[:, D_MODEL:2 * D_MODEL]) * _dot(yf_ref[...], wf_ref[...]))
    hmix = _dot(merged.astype(bf16), wo_ref[...])
    xn = _layer_norm(DN_ALPHA * x_ref[...] + hmix, g_ref[...], b_ref[...])
    xo_ref[...] = xn
    xbo_ref[...] = xn.astype(bf16)


def _merge(yn, yf, xb, x, wn, wf, wmg, wo, g, bb):
    m = x.shape[0]
    tm = TM_MERGE
    row = lambda w: pl.BlockSpec((tm, w), lambda i: (i, 0))
    full = lambda a: pl.BlockSpec(a.shape, lambda i: (0, 0))
    return pl.pallas_call(
        _merge_kernel,
        grid=(m // tm,),
        in_specs=[row(NSA_Q_W), row(FOX_W), row(D_MODEL), row(D_MODEL),
                  full(wn), full(wf), full(wmg), full(wo), full(g), full(bb)],
        out_specs=[row(D_MODEL), row(D_MODEL)],
        out_shape=[jax.ShapeDtypeStruct((m, D_MODEL), f32), jax.ShapeDtypeStruct((m, D_MODEL), bf16)],
        compiler_params=_cparams(("parallel",)),
        name="merge",
    )(yn, yf, xb, x, wn, wf, wmg, wo, g, bb)


def _ffn_kernel(xb_ref, x_ref, wg_ref, wu_ref, wd_ref, g_ref, b_ref, xo_ref, xbo_ref, acc_ref, *, nf):
    f = pl.program_id(1)

    @pl.when(f == 0)
    def _():
        acc_ref[...] = jnp.zeros_like(acc_ref)

    xb = xb_ref[...]
    gate = _dot(xb, wg_ref[...])
    up = _dot(xb, wu_ref[...])
    act = (gate * _sigmoid(gate) * up).astype(bf16)
    acc_ref[...] += _dot(act, wd_ref[...])

    @pl.when(f == nf - 1)
    def _():
        xn = _layer_norm(DN_ALPHA * x_ref[...] + acc_ref[...], g_ref[...], b_ref[...])
        xo_ref[...] = xn
        xbo_ref[...] = xn.astype(bf16)


def _ffn(xb, x, wg, wu, wd, g, bb):
    m = x.shape[0]
    dff = wg.shape[1]
    tm, tf = TM_FFN, TF_FFN
    nf = dff // tf
    row = pl.BlockSpec((tm, D_MODEL), lambda i, f: (i, 0))
    vec = pl.BlockSpec((1, D_MODEL), lambda i, f: (0, 0))
    return pl.pallas_call(
        functools.partial(_ffn_kernel, nf=nf),
        grid=(m // tm, nf),
        in_specs=[row, row,
                  pl.BlockSpec((D_MODEL, tf), lambda i, f: (0, f)),
                  pl.BlockSpec((D_MODEL, tf), lambda i, f: (0, f)),
                  pl.BlockSpec((tf, D_MODEL), lambda i, f: (f, 0)),
                  vec, vec],
        out_specs=[row, row],
        out_shape=[jax.ShapeDtypeStruct((m, D_MODEL), f32), jax.ShapeDtypeStruct((m, D_MODEL), bf16)],
        scratch_shapes=[pltpu.VMEM((tm, D_MODEL), f32)],
        compiler_params=_cparams(("parallel", "arbitrary")),
        name="ffn",
    )(xb, x, wg, wu, wd, g, bb)


def _router_kernel(x_ref, r_ref, tri_ref, gate_ref, rank_ref, cnt_ref, *, seq):
    tb = tri_ref.shape[0]
    x = x_ref[0]
    xh = x.astype(bf16)
    xl = (x - xh.astype(f32)).astype(bf16)
    r = r_ref[...]
    rh = r.astype(bf16)
    rl = (r - rh.astype(f32)).astype(bf16)
    logits = _dot(xh, rh) + _dot(xh, rl) + _dot(xl, rh)
    lane = lax.broadcasted_iota(i32, (seq, LANES), 1).astype(f32)
    low = -3.0e38
    lg = jnp.where(lane < N_EXPERTS, logits, low)
    m1 = jnp.max(lg, axis=-1, keepdims=True)
    i1 = jnp.min(jnp.where(lg == m1, lane, float(LANES)), axis=-1, keepdims=True)
    lg2 = jnp.where(lane == i1, low, lg)
    m2 = jnp.max(lg2, axis=-1, keepdims=True)
    i2 = jnp.min(jnp.where(lg2 == m2, lane, float(LANES)), axis=-1, keepdims=True)
    e2 = jnp.exp(m2 - m1)
    den = 1.0 + e2
    gate_ref[0] = jnp.where(lane == i1, 1.0 / den, jnp.where(lane == i2, e2 / den, 0.0))
    sel = (lane == i1) | (lane == i2)
    selb = jnp.where(sel, 1.0, 0.0).astype(bf16)
    carry = jnp.zeros((1, LANES), f32)
    for blk in range(seq // tb):
        sl = slice(blk * tb, (blk + 1) * tb)
        c = _dot(tri_ref[...], selb[sl]) + carry
        carry = c[tb - 1:tb, :]
        rank_ref[0, sl, :] = jnp.where(sel[sl], c - 1.0, -1.0)
    cnt_ref[0] = carry.astype(i32)


def _router(x3, router_pad, tri):
    b, s, _ = x3.shape
    return pl.pallas_call(
        functools.partial(_router_kernel, seq=s),
        grid=(b,),
        in_specs=[pl.BlockSpec((1, s, D_MODEL), lambda i: (i, 0, 0)),
                  pl.BlockSpec((D_MODEL, LANES), lambda i: (0, 0)),
                  pl.BlockSpec(tri.shape, lambda i: (0, 0))],
        out_specs=[pl.BlockSpec((1, s, LANES), lambda i: (i, 0, 0)),
                   pl.BlockSpec((1, s, LANES), lambda i: (i, 0, 0)),
                   pl.BlockSpec((1, 1, LANES), lambda i: (i, 0, 0))],
        out_shape=[jax.ShapeDtypeStruct((b, s, LANES), f32),
                   jax.ShapeDtypeStruct((b, s, LANES), f32),
                   jax.ShapeDtypeStruct((b, 1, LANES), i32)],
        compiler_params=_cparams(("parallel",)),
        name="router",
    )(x3, router_pad, tri)


def _moe_kernel(cnt_ref, xb_ref, rankt_ref, gatet_ref, ranke_ref, wg_ref, wu_ref, wd_ref, y_ref,
                xg_ref, acc_ref, *, seq, nf):
    tr = TR_MOE
    c = pl.program_id(0)
    e = pl.program_id(1)
    f = pl.program_id(2)
    nsub = (cnt_ref[c * N_EXPERTS + e] + tr - 1) // tr

    def onehot_rows(s):
        rk = rankt_ref[0, pl.ds(e, 1), :]
        want = (s * tr + lax.broadcasted_iota(i32, (tr, seq), 0)).astype(f32)
        return rk == want

    @pl.when((e == 0) & (f == 0))
    def _():
        y_ref[...] = jnp.zeros_like(y_ref)

    @pl.when(f == 0)
    def _():
        def gather(s, carry):
            p = jnp.where(onehot_rows(s), 1.0, 0.0).astype(bf16)
            xg_ref[pl.ds(s * tr, tr), :] = _dot(p, xb_ref[0]).astype(bf16)
            acc_ref[pl.ds(s * tr, tr), :] = jnp.zeros((tr, D_MODEL), f32)
            return carry
        lax.fori_loop(0, nsub, gather, 0)

    def hidden(s, carry):
        xs = xg_ref[pl.ds(s * tr, tr), :]
        gate = _dot(xs, wg_ref[0])
        up = _dot(xs, wu_ref[0])
        act = (gate * _sigmoid(gate) * up).astype(bf16)
        acc_ref[pl.ds(s * tr, tr), :] += _dot(act, wd_ref[0])
        return carry

    lax.fori_loop(0, nsub, hidden, 0)

    @pl.when(f == nf - 1)
    def _():
        def combine(s, carry):
            grow = gatet_ref[0, pl.ds(e, 1), :]
            w = jnp.sum(jnp.where(onehot_rows(s), grow, 0.0), axis=-1, keepdims=True)
            z = (acc_ref[pl.ds(s * tr, tr), :] * w).astype(bf16)
            want = (s * tr + lax.broadcasted_iota(i32, (512, tr), 1)).astype(f32)
            for jb in range(seq // 512):
                rc = ranke_ref[0, 0, jb * 512:(jb + 1) * 512, :]
                pt = jnp.where(rc == want, 1.0, 0.0).astype(bf16)
                y_ref[0, jb * 512:(jb + 1) * 512, :] += _dot(pt, z)
            return carry
        lax.fori_loop(0, nsub, combine, 0)


def _moe(counts, xb3, rankt, gatet, ranke, wg, wu, wd):
    b, s, _ = xb3.shape
    dff = wg.shape[2]
    nf = dff // TF_MOE
    grid_spec = pltpu.PrefetchScalarGridSpec(
        num_scalar_prefetch=1,
        grid=(b, N_EXPERTS, nf),
        in_specs=[pl.BlockSpec((1, s, D_MODEL), lambda c, e, f, cnt: (c, 0, 0)),
                  pl.BlockSpec((1, N_EXPERTS, s), lambda c, e, f, cnt: (c, 0, 0)),
                  pl.BlockSpec((1, N_EXPERTS, s), lambda c, e, f, cnt: (c, 0, 0)),
                  pl.BlockSpec((1, 1, s, 1), lambda c, e, f, cnt: (c, e, 0, 0)),
                  pl.BlockSpec((1, D_MODEL, TF_MOE), lambda c, e, f, cnt: (e, 0, f)),
                  pl.BlockSpec((1, D_MODEL, TF_MOE), lambda c, e, f, cnt: (e, 0, f)),
                  pl.BlockSpec((1, TF_MOE, D_MODEL), lambda c, e, f, cnt: (e, f, 0))],
        out_specs=pl.BlockSpec((1, s, D_MODEL), lambda c, e, f, cnt: (c, 0, 0)),
        scratch_shapes=[pltpu.VMEM((s, D_MODEL), bf16), pltpu.VMEM((s, D_MODEL), f32)],
    )
    return pl.pallas_call(
        functools.partial(_moe_kernel, seq=s, nf=nf),
        grid_spec=grid_spec,
        out_shape=jax.ShapeDtypeStruct((b, s, D_MODEL), f32),
        compiler_params=_cparams(("parallel", "arbitrary", "arbitrary")),
        name="moe",
    )(counts, xb3, rankt, gatet, ranke, wg, wu, wd)


def _resln_kernel(x_ref, y_ref, g_ref, b_ref, xo_ref, xbo_ref):
    xn = _layer_norm(DN_ALPHA * x_ref[...] + y_ref[...], g_ref[...], b_ref[...])
    xo_ref[...] = xn
    xbo_ref[...] = xn.astype(bf16)


def _resln(x, y, g, bb):
    m = x.shape[0]
    tm = TM_FFN
    row = pl.BlockSpec((tm, D_MODEL), lambda i: (i, 0))
    vec = pl.BlockSpec((1, D_MODEL), lambda i: (0, 0))
    return pl.pallas_call(
        _resln_kernel,
        grid=(m // tm,),
        in_specs=[row, row, vec, vec],
        out_specs=[row, row],
        out_shape=[jax.ShapeDtypeStruct((m, D_MODEL), f32), jax.ShapeDtypeStruct((m, D_MODEL), bf16)],
        compiler_params=_cparams(("parallel",)),
        name="resln",
    )(x, y, g, bb)


def _t5_bucket(dist):
    n = jnp.maximum(dist, 0)
    max_exact = REL_BUCKETS // 2
    large = max_exact + (jnp.log(jnp.maximum(n, 1).astype(f32) / max_exact)
                         / math.log(REL_MAX_DIST / max_exact) * (REL_BUCKETS - max_exact)).astype(i32)
    large = jnp.minimum(large, REL_BUCKETS - 1)
    return jnp.where(n < max_exact, n, large)


def _bias_of_dist(rel_bias, dist):
    onehot = (_t5_bucket(dist)[None] == jnp.arange(REL_BUCKETS).reshape((-1,) + (1,) * dist.ndim)).astype(f32)
    return LOG2E * jnp.einsum("kh,k...->h...", rel_bias.astype(f32), onehot, precision=lax.Precision.HIGHEST)


def _bias_tables(rel_bias, seq):
    t = T_ATT
    rows = NSA_GROUP * t
    ncmp = seq // CMP_STRIDE
    d0 = jnp.arange(t)[:, None] - jnp.arange(t)[None, :]
    offs = jnp.array([0, t, 2 * t, WINDOW]).reshape(4, 1, 1)
    kinds = _bias_of_dist(rel_bias, offs + d0[None])
    mask = jnp.stack([d0 >= 0, d0 == d0, d0 == d0, d0 < 0])
    kinds = jnp.where(mask[None], kinds, NEG_INF)
    tbias = kinds.reshape(NSA_KV_HEADS, NSA_GROUP, 4, t, t).transpose(0, 2, 1, 3, 4)
    tbias = tbias.reshape(NSA_KV_HEADS, 4, rows, t)
    cend = jnp.arange(ncmp) * CMP_STRIDE + CMP_BLOCK - 1
    bc = _bias_of_dist(rel_bias, jnp.arange(seq)[:, None] - cend[None, :])
    bc = bc.reshape(NSA_KV_HEADS, NSA_GROUP, seq // t, t, ncmp).transpose(0, 2, 1, 3, 4)
    return tbias, bc.reshape(NSA_KV_HEADS, seq * NSA_GROUP, ncmp)


def _selection_constants(seq):
    ncmp = seq // CMP_STRIDE
    nsel = seq // SEL_BLOCK
    c0 = np.arange(ncmp)[:, None] * CMP_STRIDE
    s0 = np.arange(LANES)[None, :] * SEL_BLOCK
    ov = np.maximum(np.minimum(c0 + CMP_BLOCK, s0 + SEL_BLOCK) - np.maximum(c0, s0), 0) / CMP_BLOCK
    ov[ncmp - 1, :] = 0.0
    ov[:, nsel:] = 0.0
    et = (np.arange(seq)[:, None] // SEL_BLOCK == np.arange(LANES)[None, :]).astype(np.float32)
    return jnp.asarray(ov.T, bf16), jnp.asarray(et, bf16)


def _fox_placement():
    xw = FOX_PAIRS * LANES
    pq = np.zeros((3, LANES, xw), np.float32)
    pk = np.zeros((3, LANES, xw), np.float32)
    oq = np.zeros((1, xw), np.float32)
    ok = np.zeros((1, xw), np.float32)
    for p in range(FOX_PAIRS):
        for hh in range(2):
            src = FGATE_LANE + 2 * p + hh
            base = p * LANES + hh * XCOLS
            for part in range(3):
                pk[part, src, base + part] = -1.0
                pq[part, src, base + 3 + part] = 1.0
                oq[0, base + part] = 1.0
                ok[0, base + 3 + part] = 1.0
    return jnp.asarray(pq, bf16), jnp.asarray(pk, bf16), jnp.asarray(oq), jnp.asarray(ok)


def _layer_weights(w_in, layer_pe, w1, w2, f_bias):
    offs = np.cumsum((NSA_Q_W, 6 * 2 * HEAD_DIM, 3 * NSA_HEADS, 3 * FOX_W, FOX_HEADS, 2 * D_MODEL))
    kv0, g0, fx0, ff0, mg0 = offs[0], offs[1], offs[2], offs[3], offs[4]
    scale = HEAD_DIM ** -0.5 * LOG2E
    kvw = NSA_KV_HEADS * HEAD_DIM
    w_kv = w_in[:, kv0:g0]
    dup = np.concatenate([np.tile(np.arange(h * HEAD_DIM, (h + 1) * HEAD_DIM), 2) for h in range(NSA_KV_HEADS)])
    w_kvdup = jnp.concatenate([w_kv[:, a * kvw:(a + 1) * kvw][:, dup] for a in range(2, 6)], axis=1)
    w_qkv = jnp.concatenate([w_in[:, :NSA_Q_W] * scale, w_kvdup, w_in[:, fx0:fx0 + FOX_W] * scale,
                             w_in[:, fx0 + FOX_W:ff0]], axis=1).astype(bf16)
    zeros = lambda n: jnp.zeros((D_MODEL, n), w_in.dtype)
    gate_cols = lambda h: w_in[:, g0 + np.array([(h * NSA_GROUP + g) * 3 + br
                                                 for br in range(3) for g in range(NSA_GROUP)])]
    ng = 3 * NSA_GROUP
    w_aux = jnp.concatenate([w_kv[:, :2 * kvw],
                             gate_cols(0), zeros(FGATE_LANE - ng), w_in[:, ff0:mg0],
                             zeros(LANES - FGATE_LANE - FOX_HEADS),
                             gate_cols(1), zeros(LANES - ng)], axis=1).astype(bf16)
    fb_row = jnp.zeros((1, LANES), f32).at[0, FGATE_LANE:FGATE_LANE + FOX_HEADS].set(f_bias.astype(f32))
    half = CMP_STRIDE * HEAD_DIM
    pe_flat = layer_pe.astype(f32).reshape(2, 1, CMP_BLOCK * HEAD_DIM)
    pe_top, pe_bot = pe_flat[:, :, :half], pe_flat[:, :, half:]
    w2d = jnp.concatenate([w2, w2], axis=-1).astype(bf16)
    return w_qkv, w_aux, w_in[:, mg0:].astype(bf16), fb_row, pe_top, pe_bot, w1.astype(bf16), w2d


def kernel(x, w_in, nsa_cmp_pe, nsa_cmp_w1, nsa_cmp_w2, fox_f_bias, w_nsa_branch, w_fox_branch, w_out,
           rel_bias, ln1_g, ln1_b, ln2_g, ln2_b, dense_w_gate, dense_w_up, dense_w_down, moe_router,
           moe_w_gate, moe_w_up, moe_w_down):
    b, s, d = x.shape
    m = b * s
    nhalf = s // CMP_STRIDE
    tbias, bias_c = _bias_tables(rel_bias, s)
    ov, et = _selection_constants(s)
    pq, pk, oq, ok = _fox_placement()
    tri128 = jnp.asarray(np.tril(np.ones((LANES, LANES), np.float32)), bf16)
    tri256 = jnp.asarray(np.tril(np.ones((256, 256), np.float32)), bf16)

    xf = x.reshape(m, d).astype(f32)
    xb = xf.astype(bf16)
    for layer in range(DEPTH):
        w_qkv, w_aux, w_mg, fb_row, pe_top, pe_bot, w1, w2d = _layer_weights(
            w_in[layer], nsa_cmp_pe[layer], nsa_cmp_w1[layer], nsa_cmp_w2[layer], fox_f_bias[layer])
        qkv = _proj(xb, w_qkv, bf16).reshape(b, s, QKV_W)
        aux = _proj(xb, w_aux, f32).reshape(b, s, AUX_W)
        gates, qx, kx = _gateprep(aux, fb_row, tri128, pq, pk, oq, ok)
        hk = aux[:, :, :2 * LANES].reshape(b, nhalf, CMP_STRIDE, 2, NSA_KV_HEADS, HEAD_DIM)
        hk = hk.transpose(0, 3, 4, 1, 2, 5).reshape(b, 2, NSA_KV_HEADS, nhalf, CMP_STRIDE * HEAD_DIM)
        cmpkv = _compress(hk, pe_top, pe_bot, w1, w2d)
        y_nsa = _nsa(qkv, cmpkv, gates, bias_c, tbias, ov, et).reshape(m, NSA_Q_W)
        y_fox = _fox(qkv, qx, kx).reshape(m, FOX_W)
        xf, xb = _merge(y_nsa, y_fox, xb, xf, w_nsa_branch[layer].astype(bf16),
                        w_fox_branch[layer].astype(bf16), w_mg, w_out[layer].astype(bf16),
                        ln1_g[layer].reshape(1, d), ln1_b[layer].reshape(1, d))
        j = layer // 2
        g2, b2 = ln2_g[layer].reshape(1, d), ln2_b[layer].reshape(1, d)
        if layer % 2 == 0:
            xf, xb = _ffn(xb, xf, dense_w_gate[j].astype(bf16), dense_w_up[j].astype(bf16),
                          dense_w_down[j].astype(bf16), g2, b2)
        else:
            router_pad = jnp.zeros((d, LANES), f32).at[:, :N_EXPERTS].set(moe_router[j].astype(f32))
            gate, rank, cnt = _router(xf.reshape(b, s, d), router_pad, tri256)
            rankt = rank[:, :, :N_EXPERTS].transpose(0, 2, 1)
            gatet = gate[:, :, :N_EXPERTS].transpose(0, 2, 1)
            counts = cnt[:, 0, :N_EXPERTS].reshape(-1)
            y = _moe(counts, xb.reshape(b, s, d), rankt, gatet, rankt[..., None],
                     moe_w_gate[j].astype(bf16), moe_w_up[j].astype(bf16), moe_w_down[j].astype(bf16))
            xf, xb = _resln(xf, y.reshape(m, d), g2, b2)
    return xf.reshape(b, s, d).astype(x.dtype)
```

```python
import functools
import math

import numpy as np
import jax
import jax.numpy as jnp
from jax import lax
from jax.experimental import pallas as pl
from jax.experimental.pallas import tpu as pltpu

f32 = jnp.float32
bf16 = jnp.bfloat16
i32 = jnp.int32

D_MODEL = 1024
HEAD_DIM = 64
LANES = 128
NSA_HEADS = 8
NSA_KV_HEADS = 2
NSA_GROUP = NSA_HEADS // NSA_KV_HEADS
FOX_HEADS = 8
FOX_PAIRS = FOX_HEADS // 2
CMP_BLOCK = 32
CMP_STRIDE = 16
CMP_HIDDEN = 128
SEL_BLOCK = 64
N_SELECT = 16
N_LOCAL_SEL = 2
WINDOW = 512
REL_BUCKETS = 32
REL_MAX_DIST = 128
N_EXPERTS = 8
DEPTH = 4
DN_ALPHA = (2 * DEPTH) ** 0.25
LN_EPS = 1e-5
FORCED_SCORE = 1e4
NEG_INF = -1e30
LOG2E = math.log2(math.e)

NSA_Q_W = NSA_HEADS * HEAD_DIM
FOX_W = FOX_HEADS * HEAD_DIM
QKV_TILES_KV = NSA_Q_W // LANES
QKV_TILES_FOX = QKV_TILES_KV + 4 * NSA_KV_HEADS
QKV_W = (QKV_TILES_FOX + 3 * FOX_PAIRS) * LANES
AUX_W = 4 * LANES
FGATE_LANE = 24
XCOLS = 6

T_ATT = 256
TM_PROJ = 512
TM_MERGE = 256
TM_FFN = 512
TF_FFN = 1408
TR_MOE = 256
TF_MOE = 512
VMEM_LIMIT = 56 * 1024 * 1024


def _cparams(sem):
    return pltpu.CompilerParams(dimension_semantics=sem, vmem_limit_bytes=VMEM_LIMIT)


def _dot(a, b):
    return jnp.dot(a, b, preferred_element_type=f32)


def _dot_nt(a, b):
    return lax.dot_general(a, b, (((1,), (1,)), ((), ())), preferred_element_type=f32)


def _sigmoid(x):
    return 1.0 / (1.0 + jnp.exp(-x))


def _layer_norm(z, g, b):
    mu = jnp.mean(z, axis=-1, keepdims=True)
    zc = z - mu
    var = jnp.mean(zc * zc, axis=-1, keepdims=True)
    return zc * lax.rsqrt(var + LN_EPS) * g + b


def _split3(x):
    hi = x.astype(bf16)
    r1 = x - hi.astype(f32)
    mid = r1.astype(bf16)
    lo = (r1 - mid.astype(f32)).astype(bf16)
    return hi, mid, lo


def _proj_kernel(x_ref, w_ref, o_ref):
    o_ref[...] = _dot(x_ref[...], w_ref[...]).astype(o_ref.dtype)


def _proj(xb, w, out_dtype):
    m, k = xb.shape
    n = w.shape[1]
    return pl.pallas_call(
        _proj_kernel,
        grid=(m // TM_PROJ,),
        in_specs=[pl.BlockSpec((TM_PROJ, k), lambda i: (i, 0)),
                  pl.BlockSpec((k, n), lambda i: (0, 0))],
        out_specs=pl.BlockSpec((TM_PROJ, n), lambda i: (i, 0)),
        out_shape=jax.ShapeDtypeStruct((m, n), out_dtype),
        compiler_params=_cparams(("parallel",)),
        name="proj",
    )(xb, w)


def _gateprep_kernel(a_ref, fb_ref, tri_ref, pq_ref, pk_ref, oq_ref, ok_ref, g_ref, qx_ref, kx_ref, *, seq):
    tb = LANES
    tri = tri_ref[...]
    carry = jnp.zeros((1, LANES), f32)
    for blk in range(seq // tb):
        sl = slice(blk * tb, (blk + 1) * tb)
        va = a_ref[0, sl, 0:LANES]
        g_ref[0, 0, sl, :] = _sigmoid(va)
        g_ref[0, 1, sl, :] = _sigmoid(a_ref[0, sl, LANES:2 * LANES])
        z = va + fb_ref[...]
        logf = jnp.minimum(z, 0.0) - jnp.log1p(jnp.exp(-jnp.abs(z)))
        hi, mid, lo = _split3(logf)
        c = _dot(tri, hi) + _dot(tri, mid) + _dot(tri, lo) + carry
        carry = c[tb - 1:tb, :]
        chi, cmid, clo = _split3(c * LOG2E)
        qx = _dot(chi, pq_ref[0]) + _dot(cmid, pq_ref[1]) + _dot(clo, pq_ref[2]) + oq_ref[...]
        kx = _dot(chi, pk_ref[0]) + _dot(cmid, pk_ref[1]) + _dot(clo, pk_ref[2]) + ok_ref[...]
        qx_ref[0, sl, :] = qx.astype(bf16)
        kx_ref[0, sl, :] = kx.astype(bf16)


def _gateprep(aux, fb_row, tri, pq, pk, oq, ok):
    b, s, _ = aux.shape
    xw = FOX_PAIRS * LANES
    const2 = lambda a: pl.BlockSpec(a.shape, lambda i: (0, 0))
    const3 = lambda a: pl.BlockSpec(a.shape, lambda i: (0, 0, 0))
    return pl.pallas_call(
        functools.partial(_gateprep_kernel, seq=s),
        grid=(b,),
        in_specs=[pl.BlockSpec((1, s, 2 * LANES), lambda i: (i, 0, 1)),
                  const2(fb_row), const2(tri), const3(pq), const3(pk), const2(oq), const2(ok)],
        out_specs=[pl.BlockSpec((1, 2, s, LANES), lambda i: (i, 0, 0, 0)),
                   pl.BlockSpec((1, s, xw), lambda i: (i, 0, 0)),
                   pl.BlockSpec((1, s, xw), lambda i: (i, 0, 0))],
        out_shape=[jax.ShapeDtypeStruct((b, 2, s, LANES), f32),
                   jax.ShapeDtypeStruct((b, s, xw), bf16),
                   jax.ShapeDtypeStruct((b, s, xw), bf16)],
        compiler_params=_cparams(("parallel",)),
        name="gateprep",
    )(aux, fb_row, tri, pq, pk, oq, ok)


def _gelu_tanh(x):
    c = math.sqrt(2.0 / math.pi)
    return x * (0.5 * (1.0 + jnp.tanh(c * (x + 0.044715 * (x * x * x)))))


def _compress_kernel(a_ref, pe_ref, w1t_ref, w1b_ref, w2_ref, o_ref, *, nhalf):
    top = jnp.zeros((nhalf, NSA_KV_HEADS * CMP_HIDDEN), f32)
    bot = jnp.zeros((nhalf, NSA_KV_HEADS * CMP_HIDDEN), f32)
    for l in range(CMP_STRIDE):
        rows = a_ref[0, pl.ds(l, nhalf, stride=CMP_STRIDE), :]
        top = top + _dot((rows + pe_ref[0, l:l + 1, :]).astype(bf16), w1t_ref[0, l])
        bot = bot + _dot((rows + pe_ref[0, CMP_STRIDE + l:CMP_STRIDE + l + 1, :]).astype(bf16), w1b_ref[0, l])
    pre = top + pltpu.roll(bot, nhalf - 1, 0)
    act = _gelu_tanh(pre).astype(bf16)
    for h in range(NSA_KV_HEADS):
        o_ref[0, 0, h] = _dot(act[:, h * CMP_HIDDEN:(h + 1) * CMP_HIDDEN], w2_ref[0]).astype(o_ref.dtype)


def _compress(aux, pe2, w1t, w1b, w2d):
    b, s, _ = aux.shape
    nhalf = s // CMP_STRIDE
    w1_spec = pl.BlockSpec((1,) + w1t.shape[1:], lambda i, j: (j, 0, 0, 0))
    return pl.pallas_call(
        functools.partial(_compress_kernel, nhalf=nhalf),
        grid=(b, 2),
        in_specs=[pl.BlockSpec((1, s, LANES), lambda i, j: (i, 0, j)),
                  pl.BlockSpec((1, CMP_BLOCK, LANES), lambda i, j: (j, 0, 0)),
                  w1_spec, w1_spec,
                  pl.BlockSpec((1, CMP_HIDDEN, LANES), lambda i, j: (j, 0, 0))],
        out_specs=pl.BlockSpec((1, 1, NSA_KV_HEADS, nhalf, LANES), lambda i, j: (i, j, 0, 0, 0)),
        out_shape=jax.ShapeDtypeStruct((b, 2, NSA_KV_HEADS, nhalf, LANES), bf16),
        compiler_params=_cparams(("parallel", "parallel")),
        name="compress",
    )(aux, pe2, w1t, w1b, w2d)


def _flash_init(m_ref, l_ref, acc_ref):
    m_ref[...] = jnp.full(m_ref.shape, NEG_INF, f32)
    l_ref[...] = jnp.zeros(l_ref.shape, f32)
    acc_ref[...] = jnp.zeros(acc_ref.shape, f32)


def _flash_step(s, v, m_ref, l_ref, acc_ref):
    nk = s.shape[1] // LANES
    cols = [s[:, c * LANES:(c + 1) * LANES] for c in range(nk)]
    mx = cols[0]
    for c in cols[1:]:
        mx = jnp.maximum(mx, c)
    m_old = m_ref[...]
    m_new = jnp.maximum(m_old, jnp.broadcast_to(jnp.max(mx, axis=-1, keepdims=True), m_old.shape))
    alpha = jnp.exp2(m_old - m_new)
    ps = [jnp.exp2(c - m_new) for c in cols]
    psum = ps[0]
    for p in ps[1:]:
        psum = psum + p
    l_ref[...] = alpha * l_ref[...] + psum
    p = jnp.concatenate([x.astype(bf16) for x in ps], axis=1)
    acc_ref[...] = alpha * acc_ref[...] + _dot(p, v)
    m_ref[...] = m_new


def _flash_loop(first, last, logits_fn, v_fn, s_ref, m_ref, l_ref, acc_ref, last_fix=lambda s: s):
    sa, sb = s_ref.at[0], s_ref.at[1]
    n = last - first + 1
    pairs = (n - 1) // 2

    def step(buf, kt, fix=lambda s: s):
        _flash_step(fix(buf[...]), v_fn(kt), m_ref, l_ref, acc_ref)

    sa[...] = logits_fn(first)

    def body(j, carry):
        kt = first + 2 * j
        sb[...] = logits_fn(kt + 1)
        step(sa, kt)
        sa[...] = logits_fn(kt + 2)
        step(sb, kt + 1)
        return carry

    lax.fori_loop(0, pairs, body, 0)
    two_left = n - 2 * pairs == 2

    @pl.when(two_left)
    def _():
        sb[...] = logits_fn(last)
        step(sa, last - 1)
        step(sb, last, last_fix)

    @pl.when(jnp.logical_not(two_left))
    def _():
        step(sa, last, last_fix)


def _flash_finish(l_ref, acc_ref):
    l = jnp.sum(l_ref[...], axis=-1, keepdims=True)
    return acc_ref[...] * (1.0 / jnp.maximum(l, 1e-30))


def _nsa_kernel(q_ref, ks_ref, vs_ref, kw_ref, vw_ref, kc_ref, vc_ref, gate_ref, bc_ref, tb_ref, ov_ref,
                et_ref, o_ref, qaug_ref, kaug_ref, m_ref, l_ref, acc_ref, os_ref, s_ref, *, seq, ncmp):
    t = T_ATT
    rows = NSA_GROUP * t
    nsel = seq // SEL_BLOCK
    topn = min(N_SELECT, nsel)
    i = pl.program_id(2)
    t0 = i * t
    lane = lax.broadcasted_iota(i32, (t, LANES), 1)
    lo_half = lane < HEAD_DIM

    @pl.when(i == 0)
    def _():
        kaug_ref[:, 0:LANES] = ks_ref[0]
        kaug_ref[:, LANES:2 * LANES] = et_ref[...]

    for g in range(NSA_GROUP):
        qg = q_ref[0, :, (g // 2) * LANES:(g // 2 + 1) * LANES].astype(f32)
        keep = lo_half if g % 2 == 0 else jnp.logical_not(lo_half)
        qaug_ref[g * t:(g + 1) * t, 0:LANES] = jnp.where(keep, qg, 0.0).astype(bf16)
    qs = qaug_ref[:, 0:LANES]

    s = _dot_nt(qs, kc_ref[0, 0, 0]) + bc_ref[0]
    row_t = t0 + (lax.broadcasted_iota(i32, (rows, ncmp), 0) & (t - 1))
    cmp_end = lax.broadcasted_iota(i32, (rows, ncmp), 1) * CMP_STRIDE + (CMP_BLOCK - 1)
    vis = cmp_end <= row_t
    s = jnp.where(vis, s, NEG_INF)
    s = s - jnp.max(s, axis=-1, keepdims=True)
    e = jnp.where(vis, jnp.exp2(s), 0.0)
    p_c = e * (1.0 / jnp.maximum(jnp.sum(e, axis=-1, keepdims=True), 1e-30))
    o_c = _dot(p_c.astype(bf16), vc_ref[0, 0, 0])

    psum = p_c[0:t]
    for g in range(1, NSA_GROUP):
        psum = psum + p_c[g * t:(g + 1) * t]
    hi = psum.astype(bf16)
    lo = (psum - hi.astype(f32)).astype(bf16)
    imp = (_dot_nt(ov_ref[...], hi) + _dot_nt(ov_ref[...], lo))[0:nsel]
    blk = lax.broadcasted_iota(i32, (nsel, t), 0)
    cur = (t0 + lax.broadcasted_iota(i32, (nsel, t), 1)) >> 6
    forced = (blk == 0) | ((blk <= cur) & (blk > cur - N_LOCAL_SEL))
    val = jnp.where(forced, FORCED_SCORE, jnp.where(blk <= cur, imp, -1.0))
    cnt = jnp.zeros((nsel, t), f32)
    for j in range(nsel):
        vj = val[j:j + 1, :]
        beats = (vj > val) | ((vj == val) & (blk > j))
        cnt = cnt + jnp.where(beats, 1.0, 0.0)
    mneg = jnp.where(cnt < topn, 0.0, NEG_INF)
    mneg = jnp.concatenate([mneg, jnp.zeros((LANES - nsel, t), f32)], axis=0).T.astype(bf16)
    for g in range(NSA_GROUP):
        qaug_ref[g * t:(g + 1) * t, LANES:2 * LANES] = mneg

    _flash_init(m_ref, l_ref, acc_ref)

    def sel_logits(kt):
        k = kaug_ref[pl.ds(kt * t, t), :]
        return _dot_nt(qaug_ref[...], k) + tb_ref[0, jnp.minimum(i - kt, 2)]

    _flash_loop(0, i, sel_logits, lambda kt: vs_ref[0, pl.ds(kt * t, t), :], s_ref, m_ref, l_ref, acc_ref)
    os_ref[...] = _flash_finish(l_ref, acc_ref)

    _flash_init(m_ref, l_ref, acc_ref)
    nwin = WINDOW // t

    def win_logits(kt):
        d = i - kt
        kind = jnp.where(d == nwin, 3, d)
        return _dot_nt(qs, kw_ref[0, pl.ds(kt * t, t), :]) + tb_ref[0, kind]

    _flash_loop(jnp.maximum(i - nwin, 0), i, win_logits, lambda kt: vw_ref[0, pl.ds(kt * t, t), :],
                s_ref, m_ref, l_ref, acc_ref)
    o_w = _flash_finish(l_ref, acc_ref)
    o_s = os_ref[...]

    gates = gate_ref[0, 0]
    outs = []
    for g in range(NSA_GROUP):
        sl = slice(g * t, (g + 1) * t)
        outs.append(gates[:, g:g + 1] * o_c[sl]
                    + gates[:, NSA_GROUP + g:NSA_GROUP + g + 1] * o_s[sl]
                    + gates[:, 2 * NSA_GROUP + g:2 * NSA_GROUP + g + 1] * o_w[sl])
    for j in range(NSA_GROUP // 2):
        o_ref[0, :, j * LANES:(j + 1) * LANES] = jnp.where(lo_half, outs[2 * j], outs[2 * j + 1]).astype(o_ref.dtype)


def _nsa(qkv, cmpkv, gates, bias_c, tbias, ov, et):
    b, s, _ = qkv.shape
    ncmp = cmpkv.shape[3]
    t = T_ATT
    rows = NSA_GROUP * t
    qw = NSA_GROUP * HEAD_DIM
    kv_spec = lambda col: pl.BlockSpec((1, s, LANES), lambda bi, h, i, col=col: (bi, 0, col + h))
    cmp_spec = lambda kv: pl.BlockSpec((1, 1, 1, ncmp, LANES), lambda bi, h, i, kv=kv: (bi, kv, h, 0, 0))
    base = QKV_TILES_KV
    return pl.pallas_call(
        functools.partial(_nsa_kernel, seq=s, ncmp=ncmp),
        grid=(b, NSA_KV_HEADS, s // t),
        in_specs=[pl.BlockSpec((1, t, qw), lambda bi, h, i: (bi, i, h)),
                  kv_spec(base), kv_spec(base + 2), kv_spec(base + 4), kv_spec(base + 6),
                  cmp_spec(0), cmp_spec(1),
                  pl.BlockSpec((1, 1, t, LANES), lambda bi, h, i: (bi, h, i, 0)),
                  pl.BlockSpec((1, rows, ncmp), lambda bi, h, i: (h, i, 0)),
                  pl.BlockSpec((1, 4, rows, t), lambda bi, h, i: (h, 0, 0, 0)),
                  pl.BlockSpec((LANES, ncmp), lambda bi, h, i: (0, 0)),
                  pl.BlockSpec((s, LANES), lambda bi, h, i: (0, 0))],
        out_specs=pl.BlockSpec((1, t, qw), lambda bi, h, i: (bi, i, h)),
        out_shape=jax.ShapeDtypeStruct((b, s, NSA_Q_W), bf16),
        scratch_shapes=[pltpu.VMEM((rows, 2 * LANES), bf16),
                        pltpu.VMEM((s, 2 * LANES), bf16),
                        pltpu.VMEM((rows, LANES), f32),
                        pltpu.VMEM((rows, LANES), f32),
                        pltpu.VMEM((rows, LANES), f32),
                        pltpu.VMEM((rows, LANES), f32),
                        pltpu.VMEM((2, rows, t), f32)],
        compiler_params=_cparams(("parallel", "parallel", "arbitrary")),
        name="nsa",
    )(qkv, qkv, qkv, qkv, qkv, cmpkv, cmpkv, gates, bias_c, tbias, ov, et)


def _fox_kernel(q_ref, k_ref, v_ref, qx_ref, kx_ref, o_ref, qaug_ref, kaug_ref, m_ref, l_ref, acc_ref,
                s_ref):
    t = T_ATT
    i = pl.program_id(2)
    lane = lax.broadcasted_iota(i32, (t, LANES), 1)
    lo_half = lane < HEAD_DIM

    @pl.when(i == 0)
    def _():
        kaug_ref[:, 0:LANES] = k_ref[0]
        kaug_ref[:, LANES:2 * LANES] = kx_ref[0]

    q = q_ref[0].astype(f32)
    qx = qx_ref[0].astype(f32)
    qaug_ref[0:t, 0:LANES] = jnp.where(lo_half, q, 0.0).astype(bf16)
    qaug_ref[t:2 * t, 0:LANES] = jnp.where(lo_half, 0.0, q).astype(bf16)
    qaug_ref[0:t, LANES:2 * LANES] = jnp.where(lane < XCOLS, qx, 0.0).astype(bf16)
    qaug_ref[t:2 * t, LANES:2 * LANES] = jnp.where((lane >= XCOLS) & (lane < 2 * XCOLS), qx, 0.0).astype(bf16)
    _flash_init(m_ref, l_ref, acc_ref)

    def logits(kt):
        return _dot_nt(qaug_ref[...], kaug_ref[pl.ds(kt * t, t), :])

    def causal(s):
        row = lax.broadcasted_iota(i32, (2 * t, t), 0) & (t - 1)
        return jnp.where(lax.broadcasted_iota(i32, (2 * t, t), 1) <= row, s, NEG_INF)

    _flash_loop(0, i, logits, lambda kt: v_ref[0, pl.ds(kt * t, t), :], s_ref, m_ref, l_ref, acc_ref,
                last_fix=causal)
    o = _flash_finish(l_ref, acc_ref)
    o_ref[0] = jnp.where(lo_half, o[0:t], o[t:2 * t]).astype(o_ref.dtype)


def _fox(qkv, qx, kx):
    b, s, _ = qkv.shape
    t = T_ATT
    base = QKV_TILES_FOX
    return pl.pallas_call(
        _fox_kernel,
        grid=(b, FOX_PAIRS, s // t),
        in_specs=[pl.BlockSpec((1, t, LANES), lambda bi, p, i: (bi, i, base + p)),
                  pl.BlockSpec((1, s, LANES), lambda bi, p, i: (bi, 0, base + FOX_PAIRS + p)),
                  pl.BlockSpec((1, s, LANES), lambda bi, p, i: (bi, 0, base + 2 * FOX_PAIRS + p)),
                  pl.BlockSpec((1, t, LANES), lambda bi, p, i: (bi, i, p)),
                  pl.BlockSpec((1, s, LANES), lambda bi, p, i: (bi, 0, p))],
        out_specs=pl.BlockSpec((1, t, LANES), lambda bi, p, i: (bi, i, p)),
        out_shape=jax.ShapeDtypeStruct((b, s, FOX_W), bf16),
        scratch_shapes=[pltpu.VMEM((2 * t, 2 * LANES), bf16),
                        pltpu.VMEM((s, 2 * LANES), bf16),
                        pltpu.VMEM((2 * t, LANES), f32),
                        pltpu.VMEM((2 * t, LANES), f32),
                        pltpu.VMEM((2 * t, LANES), f32),
                        pltpu.VMEM((2, 2 * t, t), f32)],
        compiler_params=_cparams(("parallel", "parallel", "arbitrary")),
        name="fox",
    )(qkv, qkv, qkv, qx, kx)


def _merge_kernel(yn_ref, yf_ref, xb_ref, x_ref, wn_ref, wf_ref, wmg_ref, wo_ref, g_ref, b_ref,
                  xo_ref, xbo_ref):
    mg = _dot(xb_ref[...], wmg_ref[...])
    merged = (_sigmoid(mg[:, 0:D_MODEL]) * _dot(yn_ref[...], wn_ref[...])
              + _sigmoid(mg[:, D_MODEL:2 * D_MODEL]) * _dot(yf_ref[...], wf_ref[...]))
    hmix = _dot(merged.astype(bf16), wo_ref[...])
    xn = _layer_norm(DN_ALPHA * x_ref[...] + hmix, g_ref[...], b_ref[...])
    xo_ref[...] = xn
    xbo_ref[...] = xn.astype(bf16)


def _merge(yn, yf, xb, x, wn, wf, wmg, wo, g, bb):
    m = x.shape[0]
    tm = TM_MERGE
    row = lambda w: pl.BlockSpec((tm, w), lambda i: (i, 0))
    full = lambda a: pl.BlockSpec(a.shape, lambda i: (0, 0))
    return pl.pallas_call(
        _merge_kernel,
        grid=(m // tm,),
        in_specs=[row(NSA_Q_W), row(FOX_W), row(D_MODEL), row(D_MODEL),
                  full(wn), full(wf), full(wmg), full(wo), full(g), full(bb)],
        out_specs=[row(D_MODEL), row(D_MODEL)],
        out_shape=[jax.ShapeDtypeStruct((m, D_MODEL), f32), jax.ShapeDtypeStruct((m, D_MODEL), bf16)],
        compiler_params=_cparams(("parallel",)),
        name="merge",
    )(yn, yf, xb, x, wn, wf, wmg, wo, g, bb)


def _ffn_kernel(xb_ref, x_ref, wg_ref, wu_ref, wd_ref, g_ref, b_ref, xo_ref, xbo_ref, acc_ref, *, nf):
    f = pl.program_id(1)

    @pl.when(f == 0)
    def _():
        acc_ref[...] = jnp.zeros_like(acc_ref)

    xb = xb_ref[...]
    gate = _dot(xb, wg_ref[...])
    up = _dot(xb, wu_ref[...])
    act = (gate * _sigmoid(gate) * up).astype(bf16)
    acc_ref[...] += _dot(act, wd_ref[...])

    @pl.when(f == nf - 1)
    def _():
        xn = _layer_norm(DN_ALPHA * x_ref[...] + acc_ref[...], g_ref[...], b_ref[...])
        xo_ref[...] = xn
        xbo_ref[...] = xn.astype(bf16)


def _ffn(xb, x, wg, wu, wd, g, bb):
    m = x.shape[0]
    dff = wg.shape[1]
    tm, tf = TM_FFN, TF_FFN
    nf = dff // tf
    row = pl.BlockSpec((tm, D_MODEL), lambda i, f: (i, 0))
    vec = pl.BlockSpec((1, D_MODEL), lambda i, f: (0, 0))
    return pl.pallas_call(
        functools.partial(_ffn_kernel, nf=nf),
        grid=(m // tm, nf),
        in_specs=[row, row,
                  pl.BlockSpec((D_MODEL, tf), lambda i, f: (0, f)),
                  pl.BlockSpec((D_MODEL, tf), lambda i, f: (0, f)),
                  pl.BlockSpec((tf, D_MODEL), lambda i, f: (f, 0)),
                  vec, vec],
        out_specs=[row, row],
        out_shape=[jax.ShapeDtypeStruct((m, D_MODEL), f32), jax.ShapeDtypeStruct((m, D_MODEL), bf16)],
        scratch_shapes=[pltpu.VMEM((tm, D_MODEL), f32)],
        compiler_params=_cparams(("parallel", "arbitrary")),
        name="ffn",
    )(xb, x, wg, wu, wd, g, bb)


def _router_kernel(x_ref, r_ref, tri_ref, gate_ref, rank_ref, cnt_ref, *, seq):
    tb = tri_ref.shape[0]
    x = x_ref[0]
    xh = x.astype(bf16)
    xl = (x - xh.astype(f32)).astype(bf16)
    r = r_ref[...]
    rh = r.astype(bf16)
    rl = (r - rh.astype(f32)).astype(bf16)
    logits = _dot(xh, rh) + _dot(xh, rl) + _dot(xl, rh)
    lane = lax.broadcasted_iota(i32, (seq, LANES), 1).astype(f32)
    low = -3.0e38
    lg = jnp.where(lane < N_EXPERTS, logits, low)
    m1 = jnp.max(lg, axis=-1, keepdims=True)
    i1 = jnp.min(jnp.where(lg == m1, lane, float(LANES)), axis=-1, keepdims=True)
    lg2 = jnp.where(lane == i1, low, lg)
    m2 = jnp.max(lg2, axis=-1, keepdims=True)
    i2 = jnp.min(jnp.where(lg2 == m2, lane, float(LANES)), axis=-1, keepdims=True)
    e2 = jnp.exp(m2 - m1)
    den = 1.0 + e2
    gate_ref[0] = jnp.where(lane == i1, 1.0 / den, jnp.where(lane == i2, e2 / den, 0.0))
    sel = (lane == i1) | (lane == i2)
    selb = jnp.where(sel, 1.0, 0.0).astype(bf16)
    carry = jnp.zeros((1, LANES), f32)
    for blk in range(seq // tb):
        sl = slice(blk * tb, (blk + 1) * tb)
        c = _dot(tri_ref[...], selb[sl]) + carry
        carry = c[tb - 1:tb, :]
        rank_ref[0, sl, :] = jnp.where(sel[sl], c - 1.0, -1.0)
    cnt_ref[0] = carry.astype(i32)


def _router(x3, router_pad, tri):
    b, s, _ = x3.shape
    return pl.pallas_call(
        functools.partial(_router_kernel, seq=s),
        grid=(b,),
        in_specs=[pl.BlockSpec((1, s, D_MODEL), lambda i: (i, 0, 0)),
                  pl.BlockSpec((D_MODEL, LANES), lambda i: (0, 0)),
                  pl.BlockSpec(tri.shape, lambda i: (0, 0))],
        out_specs=[pl.BlockSpec((1, s, LANES), lambda i: (i, 0, 0)),
                   pl.BlockSpec((1, s, LANES), lambda i: (i, 0, 0)),
                   pl.BlockSpec((1, 1, LANES), lambda i: (i, 0, 0))],
        out_shape=[jax.ShapeDtypeStruct((b, s, LANES), f32),
                   jax.ShapeDtypeStruct((b, s, LANES), f32),
                   jax.ShapeDtypeStruct((b, 1, LANES), i32)],
        compiler_params=_cparams(("parallel",)),
        name="router",
    )(x3, router_pad, tri)


def _moe_kernel(cnt_ref, xb_ref, rankt_ref, gatet_ref, ranke_ref, wg_ref, wu_ref, wd_ref, y_ref,
                xg_ref, acc_ref, *, seq, nf):
    tr = TR_MOE
    c = pl.program_id(0)
    e = pl.program_id(1)
    f = pl.program_id(2)
    nsub = (cnt_ref[c * N_EXPERTS + e] + tr - 1) // tr

    def onehot_rows(s):
        rk = rankt_ref[0, pl.ds(e, 1), :]
        want = (s * tr + lax.broadcasted_iota(i32, (tr, seq), 0)).astype(f32)
        return rk == want

    def zero(j, carry):
        y_ref[0, pl.ds(j * tr, tr), :] = jnp.zeros((tr, D_MODEL), f32)
        return carry

    lax.fori_loop(0, jnp.where((e == 0) & (f == 0), seq // tr, 0), zero, 0)

    @pl.when(f == 0)
    def _():
        def gather(s, carry):
            p = jnp.where(onehot_rows(s), 1.0, 0.0).astype(bf16)
            xg_ref[pl.ds(s * tr, tr), :] = _dot(p, xb_ref[0]).astype(bf16)
            acc_ref[pl.ds(s * tr, tr), :] = jnp.zeros((tr, D_MODEL), f32)
            return carry
        lax.fori_loop(0, nsub, gather, 0)

    def hidden(s, carry):
        xs = xg_ref[pl.ds(s * tr, tr), :]
        gate = _dot(xs, wg_ref[0])
        up = _dot(xs, wu_ref[0])
        act = (gate * _sigmoid(gate) * up).astype(bf16)
        acc_ref[pl.ds(s * tr, tr), :] += _dot(act, wd_ref[0])
        return carry

    lax.fori_loop(0, nsub, hidden, 0)

    @pl.when(f == nf - 1)
    def _():
        def combine(s, carry):
            grow = gatet_ref[0, pl.ds(e, 1), :]
            w = jnp.sum(jnp.where(onehot_rows(s), grow, 0.0), axis=-1, keepdims=True)
            z = (acc_ref[pl.ds(s * tr, tr), :] * w).astype(bf16)
            want = (s * tr + lax.broadcasted_iota(i32, (512, tr), 1)).astype(f32)
            for jb in range(seq // 512):
                rc = ranke_ref[0, 0, jb * 512:(jb + 1) * 512, :]
                pt = jnp.where(rc == want, 1.0, 0.0).astype(bf16)
                y_ref[0, jb * 512:(jb + 1) * 512, :] += _dot(pt, z)
            return carry
        lax.fori_loop(0, nsub, combine, 0)


def _moe(xb3, gate, rank, cnt, wg, wu, wd):
    b, s, _ = xb3.shape
    dff = wg.shape[2]
    nf = dff // TF_MOE
    rankt = rank[:, :, :N_EXPERTS].transpose(0, 2, 1)
    gatet = gate[:, :, :N_EXPERTS].transpose(0, 2, 1)
    counts = cnt[:, 0, :N_EXPERTS].reshape(-1)
    grid_spec = pltpu.PrefetchScalarGridSpec(
        num_scalar_prefetch=1,
        grid=(b, N_EXPERTS, nf),
        in_specs=[pl.BlockSpec((1, s, D_MODEL), lambda c, e, f, cnt: (c, 0, 0)),
                  pl.BlockSpec((1, N_EXPERTS, s), lambda c, e, f, cnt: (c, 0, 0)),
                  pl.BlockSpec((1, N_EXPERTS, s), lambda c, e, f, cnt: (c, 0, 0)),
                  pl.BlockSpec((1, 1, s, 1), lambda c, e, f, cnt: (c, e, 0, 0)),
                  pl.BlockSpec((1, D_MODEL, TF_MOE), lambda c, e, f, cnt: (e, 0, f)),
                  pl.BlockSpec((1, D_MODEL, TF_MOE), lambda c, e, f, cnt: (e, 0, f)),
                  pl.BlockSpec((1, TF_MOE, D_MODEL), lambda c, e, f, cnt: (e, f, 0))],
        out_specs=pl.BlockSpec((1, s, D_MODEL), lambda c, e, f, cnt: (c, 0, 0)),
        scratch_shapes=[pltpu.VMEM((s, D_MODEL), bf16), pltpu.VMEM((s, D_MODEL), f32)],
    )
    return pl.pallas_call(
        functools.partial(_moe_kernel, seq=s, nf=nf),
        grid_spec=grid_spec,
        out_shape=jax.ShapeDtypeStruct((b, s, D_MODEL), f32),
        compiler_params=_cparams(("parallel", "arbitrary", "arbitrary")),
        name="moe",
    )(counts, xb3, rankt, gatet, rankt[..., None], wg, wu, wd)


def _resln_kernel(x_ref, y_ref, g_ref, b_ref, xo_ref, xbo_ref):
    xn = _layer_norm(DN_ALPHA * x_ref[...] + y_ref[...], g_ref[...], b_ref[...])
    xo_ref[...] = xn
    xbo_ref[...] = xn.astype(bf16)


def _resln(x, y, g, bb):
    m = x.shape[0]
    tm = TM_FFN
    row = pl.BlockSpec((tm, D_MODEL), lambda i: (i, 0))
    vec = pl.BlockSpec((1, D_MODEL), lambda i: (0, 0))
    return pl.pallas_call(
        _resln_kernel,
        grid=(m // tm,),
        in_specs=[row, row, vec, vec],
        out_specs=[row, row],
        out_shape=[jax.ShapeDtypeStruct((m, D_MODEL), f32), jax.ShapeDtypeStruct((m, D_MODEL), bf16)],
        compiler_params=_cparams(("parallel",)),
        name="resln",
    )(x, y, g, bb)


def _t5_bucket(dist):
    n = jnp.maximum(dist, 0)
    max_exact = REL_BUCKETS // 2
    large = max_exact + (jnp.log(jnp.maximum(n, 1).astype(f32) / max_exact)
                         / math.log(REL_MAX_DIST / max_exact) * (REL_BUCKETS - max_exact)).astype(i32)
    large = jnp.minimum(large, REL_BUCKETS - 1)
    return jnp.where(n < max_exact, n, large)


def _bias_of_dist(rel_bias, dist):
    onehot = (_t5_bucket(dist)[None] == jnp.arange(REL_BUCKETS).reshape((-1,) + (1,) * dist.ndim)).astype(f32)
    return LOG2E * jnp.einsum("kh,k...->h...", rel_bias.astype(f32), onehot, precision=lax.Precision.HIGHEST)


def _bias_tables(rel_bias, seq):
    t = T_ATT
    rows = NSA_GROUP * t
    ncmp = seq // CMP_STRIDE
    d0 = jnp.arange(t)[:, None] - jnp.arange(t)[None, :]
    offs = jnp.array([0, t, 2 * t, WINDOW]).reshape(4, 1, 1)
    kinds = _bias_of_dist(rel_bias, offs + d0[None])
    mask = jnp.stack([d0 >= 0, d0 == d0, d0 == d0, d0 < 0])
    kinds = jnp.where(mask[None], kinds, NEG_INF)
    tbias = kinds.reshape(NSA_KV_HEADS, NSA_GROUP, 4, t, t).transpose(0, 2, 1, 3, 4)
    tbias = tbias.reshape(NSA_KV_HEADS, 4, rows, t)
    cend = jnp.arange(ncmp) * CMP_STRIDE + CMP_BLOCK - 1
    bc = _bias_of_dist(rel_bias, jnp.arange(seq)[:, None] - cend[None, :])
    bc = bc.reshape(NSA_KV_HEADS, NSA_GROUP, seq // t, t, ncmp).transpose(0, 2, 1, 3, 4)
    return tbias, bc.reshape(NSA_KV_HEADS, seq * NSA_GROUP, ncmp)


def _selection_constants(seq):
    ncmp = seq // CMP_STRIDE
    nsel = seq // SEL_BLOCK
    c0 = np.arange(ncmp)[:, None] * CMP_STRIDE
    s0 = np.arange(LANES)[None, :] * SEL_BLOCK
    ov = np.maximum(np.minimum(c0 + CMP_BLOCK, s0 + SEL_BLOCK) - np.maximum(c0, s0), 0) / CMP_BLOCK
    ov[ncmp - 1, :] = 0.0
    ov[:, nsel:] = 0.0
    et = (np.arange(seq)[:, None] // SEL_BLOCK == np.arange(LANES)[None, :]).astype(np.float32)
    return jnp.asarray(ov.T, bf16), jnp.asarray(et, bf16)


def _fox_placement():
    xw = FOX_PAIRS * LANES
    pq = np.zeros((3, LANES, xw), np.float32)
    pk = np.zeros((3, LANES, xw), np.float32)
    oq = np.zeros((1, xw), np.float32)
    ok = np.zeros((1, xw), np.float32)
    for p in range(FOX_PAIRS):
        for hh in range(2):
            src = FGATE_LANE + 2 * p + hh
            base = p * LANES + hh * XCOLS
            for part in range(3):
                pk[part, src, base + part] = -1.0
                pq[part, src, base + 3 + part] = 1.0
                oq[0, base + part] = 1.0
                ok[0, base + 3 + part] = 1.0
    return jnp.asarray(pq, bf16), jnp.asarray(pk, bf16), jnp.asarray(oq), jnp.asarray(ok)


def _layer_weights(w_in, layer_pe, w1, w2, f_bias):
    offs = np.cumsum((NSA_Q_W, 6 * 2 * HEAD_DIM, 3 * NSA_HEADS, 3 * FOX_W, FOX_HEADS, 2 * D_MODEL))
    kv0, g0, fx0, ff0, mg0 = offs[0], offs[1], offs[2], offs[3], offs[4]
    scale = HEAD_DIM ** -0.5 * LOG2E
    kvw = NSA_KV_HEADS * HEAD_DIM
    w_kv = w_in[:, kv0:g0]
    w_kvdup = jnp.repeat(w_kv[:, 2 * kvw:].reshape(D_MODEL, 4 * NSA_KV_HEADS, 1, HEAD_DIM), 2, axis=2)
    w_kvdup = w_kvdup.reshape(D_MODEL, 8 * kvw)
    w_qkv = jnp.concatenate([w_in[:, :NSA_Q_W] * scale, w_kvdup, w_in[:, fx0:fx0 + FOX_W] * scale,
                             w_in[:, fx0 + FOX_W:ff0]], axis=1).astype(bf16)
    zeros = lambda n: jnp.zeros((D_MODEL, n), w_in.dtype)
    ng = 3 * NSA_GROUP
    w_g = w_in[:, g0:fx0].reshape(D_MODEL, NSA_KV_HEADS, NSA_GROUP, 3).transpose(0, 1, 3, 2)
    gate_cols = lambda h: w_g[:, h].reshape(D_MODEL, ng)
    w_aux = jnp.concatenate([w_kv[:, :2 * kvw],
                             gate_cols(0), zeros(FGATE_LANE - ng), w_in[:, ff0:mg0],
                             zeros(LANES - FGATE_LANE - FOX_HEADS),
                             gate_cols(1), zeros(LANES - ng)], axis=1).astype(bf16)
    fb_row = jnp.zeros((1, LANES), f32).at[0, FGATE_LANE:FGATE_LANE + FOX_HEADS].set(f_bias.astype(f32))
    pe2 = jnp.tile(layer_pe.astype(f32), (1, 1, NSA_KV_HEADS))
    w1r = w1.reshape(2, CMP_BLOCK, HEAD_DIM, CMP_HIDDEN).astype(bf16)
    zero = jnp.zeros_like(w1r)
    w1bd = jnp.concatenate([jnp.concatenate([w1r, zero], axis=-1),
                            jnp.concatenate([zero, w1r], axis=-1)], axis=-2)
    w2d = jnp.concatenate([w2, w2], axis=-1).astype(bf16)
    return (w_qkv, w_aux, w_in[:, mg0:].astype(bf16), fb_row, pe2, w1bd[:, :CMP_STRIDE], w1bd[:, CMP_STRIDE:],
            w2d)


def kernel(x, w_in, nsa_cmp_pe, nsa_cmp_w1, nsa_cmp_w2, fox_f_bias, w_nsa_branch, w_fox_branch, w_out,
           rel_bias, ln1_g, ln1_b, ln2_g, ln2_b, dense_w_gate, dense_w_up, dense_w_down, moe_router,
           moe_w_gate, moe_w_up, moe_w_down):
    b, s, d = x.shape
    m = b * s
    tbias, bias_c = _bias_tables(rel_bias, s)
    ov, et = _selection_constants(s)
    pq, pk, oq, ok = _fox_placement()
    tri128 = jnp.asarray(np.tril(np.ones((LANES, LANES), np.float32)), bf16)
    tri256 = jnp.asarray(np.tril(np.ones((256, 256), np.float32)), bf16)

    xf = x.reshape(m, d).astype(f32)
    xb = xf.astype(bf16)
    for layer in range(DEPTH):
        w_qkv, w_aux, w_mg, fb_row, pe2, w1t, w1b, w2d = _layer_weights(
            w_in[layer], nsa_cmp_pe[layer], nsa_cmp_w1[layer], nsa_cmp_w2[layer], fox_f_bias[layer])
        qkv = _proj(xb, w_qkv, bf16).reshape(b, s, QKV_W)
        aux = _proj(xb, w_aux, f32).reshape(b, s, AUX_W)
        gates, qx, kx = _gateprep(aux, fb_row, tri128, pq, pk, oq, ok)
        cmpkv = _compress(aux, pe2, w1t, w1b, w2d)
        y_nsa = _nsa(qkv, cmpkv, gates, bias_c, tbias, ov, et).reshape(m, NSA_Q_W)
        y_fox = _fox(qkv, qx, kx).reshape(m, FOX_W)
        xf, xb = _merge(y_nsa, y_fox, xb, xf, w_nsa_branch[layer].astype(bf16),
                        w_fox_branch[layer].astype(bf16), w_mg, w_out[layer].astype(bf16),
                        ln1_g[layer].reshape(1, d), ln1_b[layer].reshape(1, d))
        j = layer // 2
        g2, b2 = ln2_g[layer].reshape(1, d), ln2_b[layer].reshape(1, d)
        if layer % 2 == 0:
            xf, xb = _ffn(xb, xf, dense_w_gate[j].astype(bf16), dense_w_up[j].astype(bf16),
                          dense_w_down[j].astype(bf16), g2, b2)
        else:
            router_pad = jnp.zeros((d, LANES), f32).at[:, :N_EXPERTS].set(moe_router[j].astype(f32))
            gate, rank, cnt = _router(xf.reshape(b, s, d), router_pad, tri256)
            y = _moe(xb.reshape(b, s, d), gate, rank, cnt,
                     moe_w_gate[j].astype(bf16), moe_w_up[j].astype(bf16), moe_w_down[j].astype(bf16))
            xf, xb = _resln(xf, y.reshape(m, d), g2, b2)
    return xf.reshape(b, s, d).astype(x.dtype)
```

```python
import functools
import math

import numpy as np
import jax
import jax.numpy as jnp
from jax import lax
from jax.experimental import pallas as pl
from jax.experimental.pallas import tpu as pltpu

f32 = jnp.float32
bf16 = jnp.bfloat16
i32 = jnp.int32

D_MODEL = 1024
HEAD_DIM = 64
LANES = 128
NSA_HEADS = 8
NSA_KV_HEADS = 2
NSA_GROUP = NSA_HEADS // NSA_KV_HEADS
FOX_HEADS = 8
FOX_PAIRS = FOX_HEADS // 2
CMP_BLOCK = 32
CMP_STRIDE = 16
CMP_HIDDEN = 128
SEL_BLOCK = 64
N_SELECT = 16
N_LOCAL_SEL = 2
WINDOW = 512
REL_BUCKETS = 32
REL_MAX_DIST = 128
N_EXPERTS = 8
DEPTH = 4
DN_ALPHA = (2 * DEPTH) ** 0.25
LN_EPS = 1e-5
FORCED_SCORE = 1e4
NEG_INF = -1e30
LOG2E = math.log2(math.e)

NSA_Q_W = NSA_HEADS * HEAD_DIM
FOX_W = FOX_HEADS * HEAD_DIM
QKV_TILES_KV = NSA_Q_W // LANES
QKV_TILES_FOX = QKV_TILES_KV + 4 * NSA_KV_HEADS
QKV_W = (QKV_TILES_FOX + 3 * FOX_PAIRS) * LANES
AUX_W = 4 * LANES
FGATE_LANE = 24
XCOLS = 6

T_ATT = 256
TM_PROJ = 512
TM_MERGE = 256
TM_FFN = 512
TF_FFN = 1408
TR_MOE = 256
TF_MOE = 512
VMEM_LIMIT = 56 * 1024 * 1024


def _cparams(sem):
    return pltpu.CompilerParams(dimension_semantics=sem, vmem_limit_bytes=VMEM_LIMIT)


def _dot(a, b):
    return jnp.dot(a, b, preferred_element_type=f32)


def _dot_nt(a, b):
    return lax.dot_general(a, b, (((1,), (1,)), ((), ())), preferred_element_type=f32)


def _sigmoid(x):
    return 1.0 / (1.0 + jnp.exp(-x))


def _layer_norm(z, g, b):
    mu = jnp.mean(z, axis=-1, keepdims=True)
    zc = z - mu
    var = jnp.mean(zc * zc, axis=-1, keepdims=True)
    return zc * lax.rsqrt(var + LN_EPS) * g + b


def _split3(x):
    hi = x.astype(bf16)
    r1 = x - hi.astype(f32)
    mid = r1.astype(bf16)
    lo = (r1 - mid.astype(f32)).astype(bf16)
    return hi, mid, lo


def _proj_kernel(x_ref, w_ref, o_ref):
    o_ref[...] = _dot(x_ref[...], w_ref[...]).astype(o_ref.dtype)


def _proj(xb, w, out_dtype):
    m, k = xb.shape
    n = w.shape[1]
    return pl.pallas_call(
        _proj_kernel,
        grid=(m // TM_PROJ,),
        in_specs=[pl.BlockSpec((TM_PROJ, k), lambda i: (i, 0)),
                  pl.BlockSpec((k, n), lambda i: (0, 0))],
        out_specs=pl.BlockSpec((TM_PROJ, n), lambda i: (i, 0)),
        out_shape=jax.ShapeDtypeStruct((m, n), out_dtype),
        compiler_params=_cparams(("parallel",)),
        name="proj",
    )(xb, w)


def _proj_tiles_kernel(x_ref, w_ref, o_ref):
    res = _dot(x_ref[...], w_ref[...])
    for j in range(o_ref.shape[1]):
        o_ref[0, j] = res[:, j * LANES:(j + 1) * LANES].astype(o_ref.dtype)


def _proj_tiles(xb, w, batch, out_dtype):
    m, k = xb.shape
    n = w.shape[1]
    nb = m // batch // TM_PROJ
    return pl.pallas_call(
        _proj_tiles_kernel,
        grid=(m // TM_PROJ,),
        in_specs=[pl.BlockSpec((TM_PROJ, k), lambda i: (i, 0)),
                  pl.BlockSpec((k, n), lambda i: (0, 0))],
        out_specs=pl.BlockSpec((1, n // LANES, TM_PROJ, LANES), lambda i: (i // nb, 0, i % nb, 0)),
        out_shape=jax.ShapeDtypeStruct((batch, n // LANES, m // batch, LANES), out_dtype),
        compiler_params=_cparams(("parallel",)),
        name="proj_tiles",
    )(xb, w)


def _gateprep_kernel(a_ref, fb_ref, tri_ref, pq_ref, pk_ref, oq_ref, ok_ref, g_ref, qx_ref, kx_ref, *, seq):
    tb = LANES
    tri = tri_ref[...]
    carry = jnp.zeros((1, LANES), f32)
    for blk in range(seq // tb):
        sl = slice(blk * tb, (blk + 1) * tb)
        va = a_ref[0, sl, 0:LANES]
        g_ref[0, 0, sl, :] = _sigmoid(va)
        g_ref[0, 1, sl, :] = _sigmoid(a_ref[0, sl, LANES:2 * LANES])
        z = va + fb_ref[...]
        logf = jnp.minimum(z, 0.0) - jnp.log1p(jnp.exp(-jnp.abs(z)))
        hi, mid, lo = _split3(logf)
        c = _dot(tri, hi) + _dot(tri, mid) + _dot(tri, lo) + carry
        carry = c[tb - 1:tb, :]
        chi, cmid, clo = _split3(c * LOG2E)
        qx = _dot(chi, pq_ref[0]) + _dot(cmid, pq_ref[1]) + _dot(clo, pq_ref[2]) + oq_ref[...]
        kx = _dot(chi, pk_ref[0]) + _dot(cmid, pk_ref[1]) + _dot(clo, pk_ref[2]) + ok_ref[...]
        for p in range(FOX_PAIRS):
            qx_ref[0, p, sl, :] = qx[:, p * LANES:(p + 1) * LANES].astype(bf16)
            kx_ref[0, p, sl, :] = kx[:, p * LANES:(p + 1) * LANES].astype(bf16)


def _gateprep(aux, fb_row, tri, pq, pk, oq, ok):
    b, s, _ = aux.shape
    const2 = lambda a: pl.BlockSpec(a.shape, lambda i: (0, 0))
    const3 = lambda a: pl.BlockSpec(a.shape, lambda i: (0, 0, 0))
    return pl.pallas_call(
        functools.partial(_gateprep_kernel, seq=s),
        grid=(b,),
        in_specs=[pl.BlockSpec((1, s, 2 * LANES), lambda i: (i, 0, 1)),
                  const2(fb_row), const2(tri), const3(pq), const3(pk), const2(oq), const2(ok)],
        out_specs=[pl.BlockSpec((1, 2, s, LANES), lambda i: (i, 0, 0, 0)),
                   pl.BlockSpec((1, FOX_PAIRS, s, LANES), lambda i: (i, 0, 0, 0)),
                   pl.BlockSpec((1, FOX_PAIRS, s, LANES), lambda i: (i, 0, 0, 0))],
        out_shape=[jax.ShapeDtypeStruct((b, 2, s, LANES), f32),
                   jax.ShapeDtypeStruct((b, FOX_PAIRS, s, LANES), bf16),
                   jax.ShapeDtypeStruct((b, FOX_PAIRS, s, LANES), bf16)],
        compiler_params=_cparams(("parallel",)),
        name="gateprep",
    )(aux, fb_row, tri, pq, pk, oq, ok)


def _gelu_tanh(x):
    c = math.sqrt(2.0 / math.pi)
    return x * (0.5 * (1.0 + jnp.tanh(c * (x + 0.044715 * (x * x * x)))))


def _compress_kernel(a_ref, pe_ref, w1t_ref, w1b_ref, w2_ref, o_ref, *, nhalf):
    top = jnp.zeros((nhalf, NSA_KV_HEADS * CMP_HIDDEN), f32)
    bot = jnp.zeros((nhalf, NSA_KV_HEADS * CMP_HIDDEN), f32)
    for l in range(CMP_STRIDE):
        rows = a_ref[0, pl.ds(l, nhalf, stride=CMP_STRIDE), :]
        top = top + _dot((rows + pe_ref[0, l:l + 1, :]).astype(bf16), w1t_ref[0, l])
        bot = bot + _dot((rows + pe_ref[0, CMP_STRIDE + l:CMP_STRIDE + l + 1, :]).astype(bf16), w1b_ref[0, l])
    pre = top + pltpu.roll(bot, nhalf - 1, 0)
    act = _gelu_tanh(pre).astype(bf16)
    for h in range(NSA_KV_HEADS):
        o_ref[0, 0, h] = _dot(act[:, h * CMP_HIDDEN:(h + 1) * CMP_HIDDEN], w2_ref[0]).astype(o_ref.dtype)


def _compress(aux, pe2, w1t, w1b, w2d):
    b, s, _ = aux.shape
    nhalf = s // CMP_STRIDE
    w1_spec = pl.BlockSpec((1,) + w1t.shape[1:], lambda i, j: (j, 0, 0, 0))
    return pl.pallas_call(
        functools.partial(_compress_kernel, nhalf=nhalf),
        grid=(b, 2),
        in_specs=[pl.BlockSpec((1, s, LANES), lambda i, j: (i, 0, j)),
                  pl.BlockSpec((1, CMP_BLOCK, LANES), lambda i, j: (j, 0, 0)),
                  w1_spec, w1_spec,
                  pl.BlockSpec((1, CMP_HIDDEN, LANES), lambda i, j: (j, 0, 0))],
        out_specs=pl.BlockSpec((1, 1, NSA_KV_HEADS, nhalf, LANES), lambda i, j: (i, j, 0, 0, 0)),
        out_shape=jax.ShapeDtypeStruct((b, 2, NSA_KV_HEADS, nhalf, LANES), bf16),
        compiler_params=_cparams(("parallel", "parallel")),
        name="compress",
    )(aux, pe2, w1t, w1b, w2d)


def _flash_init(m_ref, l_ref, acc_ref):
    m_ref[...] = jnp.full(m_ref.shape, NEG_INF, f32)
    l_ref[...] = jnp.zeros(l_ref.shape, f32)
    acc_ref[...] = jnp.zeros(acc_ref.shape, f32)


def _flash_step(s, v, m_ref, l_ref, acc_ref):
    nk = s.shape[1] // LANES
    cols = [s[:, c * LANES:(c + 1) * LANES] for c in range(nk)]
    mx = cols[0]
    for c in cols[1:]:
        mx = jnp.maximum(mx, c)
    m_old = m_ref[...]
    m_new = jnp.maximum(m_old, jnp.broadcast_to(jnp.max(mx, axis=-1, keepdims=True), m_old.shape))
    alpha = jnp.exp2(m_old - m_new)
    ps = [jnp.exp2(c - m_new) for c in cols]
    psum = ps[0]
    for p in ps[1:]:
        psum = psum + p
    l_ref[...] = alpha * l_ref[...] + psum
    p = jnp.concatenate([x.astype(bf16) for x in ps], axis=1)
    acc_ref[...] = alpha * acc_ref[...] + _dot(p, v)
    m_ref[...] = m_new


def _flash_loop(first, last, logits_fn, v_fn, s_ref, m_ref, l_ref, acc_ref, last_fix=lambda s: s):
    sa, sb = s_ref.at[0], s_ref.at[1]
    n = last - first + 1
    pairs = (n - 1) // 2

    def step(buf, kt, fix=lambda s: s):
        _flash_step(fix(buf[...]), v_fn(kt), m_ref, l_ref, acc_ref)

    sa[...] = logits_fn(first)

    def body(j, carry):
        kt = first + 2 * j
        sb[...] = logits_fn(kt + 1)
        step(sa, kt)
        sa[...] = logits_fn(kt + 2)
        step(sb, kt + 1)
        return carry

    lax.fori_loop(0, pairs, body, 0)
    two_left = n - 2 * pairs == 2

    @pl.when(two_left)
    def _():
        sb[...] = logits_fn(last)
        step(sa, last - 1)
        step(sb, last, last_fix)

    @pl.when(jnp.logical_not(two_left))
    def _():
        step(sa, last, last_fix)


def _flash_finish(l_ref, acc_ref):
    l = jnp.sum(l_ref[...], axis=-1, keepdims=True)
    return acc_ref[...] * (1.0 / jnp.maximum(l, 1e-30))


def _nsa_kernel(q_ref, ks_ref, vs_ref, kw_ref, vw_ref, kc_ref, vc_ref, gate_ref, bc_ref, tb_ref, ov_ref,
                et_ref, o_ref, qaug_ref, kaug_ref, m_ref, l_ref, acc_ref, os_ref, s_ref, *, seq, ncmp):
    t = T_ATT
    rows = NSA_GROUP * t
    nsel = seq // SEL_BLOCK
    topn = min(N_SELECT, nsel)
    i = pl.program_id(2)
    t0 = i * t
    lane = lax.broadcasted_iota(i32, (t, LANES), 1)
    lo_half = lane < HEAD_DIM

    @pl.when(i == 0)
    def _():
        kaug_ref[:, 0:LANES] = ks_ref[0, 0]
        kaug_ref[:, LANES:2 * LANES] = et_ref[...]

    for g in range(NSA_GROUP):
        qg = q_ref[0, g // 2].astype(f32)
        keep = lo_half if g % 2 == 0 else jnp.logical_not(lo_half)
        qaug_ref[g * t:(g + 1) * t, 0:LANES] = jnp.where(keep, qg, 0.0).astype(bf16)
    qs = qaug_ref[:, 0:LANES]

    s = _dot_nt(kc_ref[0, 0, 0], qs) + bc_ref[0]
    row_t = t0 + (lax.broadcasted_iota(i32, (ncmp, rows), 1) & (t - 1))
    cmp_end = lax.broadcasted_iota(i32, (ncmp, rows), 0) * CMP_STRIDE + (CMP_BLOCK - 1)
    vis = cmp_end <= row_t
    s = jnp.where(vis, s, NEG_INF)
    s = s - jnp.max(s, axis=0, keepdims=True)
    e = jnp.where(vis, jnp.exp2(s), 0.0)
    p_c = e * (1.0 / jnp.maximum(jnp.sum(e, axis=0, keepdims=True), 1e-30))
    o_c = _dot(p_c.T.astype(bf16), vc_ref[0, 0, 0])

    psum = p_c[:, 0:t]
    for g in range(1, NSA_GROUP):
        psum = psum + p_c[:, g * t:(g + 1) * t]
    hi = psum.astype(bf16)
    lo = (psum - hi.astype(f32)).astype(bf16)
    imp = (_dot(ov_ref[...], hi) + _dot(ov_ref[...], lo))[0:nsel]
    blk = lax.broadcasted_iota(i32, (nsel, t), 0)
    cur = (t0 + lax.broadcasted_iota(i32, (nsel, t), 1)) >> 6
    forced = (blk == 0) | ((blk <= cur) & (blk > cur - N_LOCAL_SEL))
    val = jnp.where(forced, FORCED_SCORE, jnp.where(blk <= cur, imp, -1.0))
    cnt = jnp.zeros((nsel, t), f32)
    for j in range(nsel):
        vj = val[j:j + 1, :]
        beats = (vj > val) | ((vj == val) & (blk > j))
        cnt = cnt + jnp.where(beats, 1.0, 0.0)
    mneg = jnp.where(cnt < topn, 0.0, NEG_INF)
    mneg = jnp.concatenate([mneg, jnp.zeros((LANES - nsel, t), f32)], axis=0).T.astype(bf16)
    for g in range(NSA_GROUP):
        qaug_ref[g * t:(g + 1) * t, LANES:2 * LANES] = mneg

    _flash_init(m_ref, l_ref, acc_ref)

    def sel_logits(kt):
        k = kaug_ref[pl.ds(kt * t, t), :]
        return _dot_nt(qaug_ref[...], k) + tb_ref[0, jnp.minimum(i - kt, 2)]

    _flash_loop(0, i, sel_logits, lambda kt: vs_ref[0, 0, pl.ds(kt * t, t), :], s_ref, m_ref, l_ref, acc_ref)
    os_ref[...] = _flash_finish(l_ref, acc_ref)

    _flash_init(m_ref, l_ref, acc_ref)
    nwin = WINDOW // t

    def win_logits(kt):
        d = i - kt
        kind = jnp.where(d == nwin, 3, d)
        return _dot_nt(qs, kw_ref[0, 0, pl.ds(kt * t, t), :]) + tb_ref[0, kind]

    _flash_loop(jnp.maximum(i - nwin, 0), i, win_logits, lambda kt: vw_ref[0, 0, pl.ds(kt * t, t), :],
                s_ref, m_ref, l_ref, acc_ref)
    o_w = _flash_finish(l_ref, acc_ref)
    o_s = os_ref[...]

    gates = gate_ref[0, 0]
    outs = []
    for g in range(NSA_GROUP):
        sl = slice(g * t, (g + 1) * t)
        outs.append(gates[:, g:g + 1] * o_c[sl]
                    + gates[:, NSA_GROUP + g:NSA_GROUP + g + 1] * o_s[sl]
                    + gates[:, 2 * NSA_GROUP + g:2 * NSA_GROUP + g + 1] * o_w[sl])
    for j in range(NSA_GROUP // 2):
        o_ref[0, :, j * LANES:(j + 1) * LANES] = jnp.where(lo_half, outs[2 * j], outs[2 * j + 1]).astype(o_ref.dtype)


def _nsa(qkv, cmpkv, gates, bias_c, tbias, ov, et):
    b, _, s, _ = qkv.shape
    ncmp = cmpkv.shape[3]
    t = T_ATT
    rows = NSA_GROUP * t
    qw = NSA_GROUP * HEAD_DIM
    qtiles = qw // LANES
    kv_spec = lambda col: pl.BlockSpec((1, 1, s, LANES), lambda h, bi, i, col=col: (bi, col + h, 0, 0))
    cmp_spec = lambda kv: pl.BlockSpec((1, 1, 1, ncmp, LANES), lambda h, bi, i, kv=kv: (bi, kv, h, 0, 0))
    base = QKV_TILES_KV
    return pl.pallas_call(
        functools.partial(_nsa_kernel, seq=s, ncmp=ncmp),
        grid=(NSA_KV_HEADS, b, s // t),
        in_specs=[pl.BlockSpec((1, qtiles, t, LANES), lambda h, bi, i: (bi, h, i, 0)),
                  kv_spec(base), kv_spec(base + 2), kv_spec(base + 4), kv_spec(base + 6),
                  cmp_spec(0), cmp_spec(1),
                  pl.BlockSpec((1, 1, t, LANES), lambda h, bi, i: (bi, h, i, 0)),
                  pl.BlockSpec((1, ncmp, rows), lambda h, bi, i: (h, i, 0)),
                  pl.BlockSpec((1, 4, rows, t), lambda h, bi, i: (h, 0, 0, 0)),
                  pl.BlockSpec((LANES, ncmp), lambda h, bi, i: (0, 0)),
                  pl.BlockSpec((s, LANES), lambda h, bi, i: (0, 0))],
        out_specs=pl.BlockSpec((1, t, qw), lambda h, bi, i: (bi, i, h)),
        out_shape=jax.ShapeDtypeStruct((b, s, NSA_Q_W), bf16),
        scratch_shapes=[pltpu.VMEM((rows, 2 * LANES), bf16),
                        pltpu.VMEM((s, 2 * LANES), bf16),
                        pltpu.VMEM((rows, LANES), f32),
                        pltpu.VMEM((rows, LANES), f32),
                        pltpu.VMEM((rows, LANES), f32),
                        pltpu.VMEM((rows, LANES), f32),
                        pltpu.VMEM((2, rows, t), f32)],
        compiler_params=_cparams(("parallel", "parallel", "arbitrary")),
        name="nsa",
    )(qkv, qkv, qkv, qkv, qkv, cmpkv, cmpkv, gates, bias_c, tbias, ov, et)


def _fox_kernel(q_ref, k_ref, v_ref, qx_ref, kx_ref, o_ref, qaug_ref, kaug_ref, m_ref, l_ref, acc_ref,
                s_ref):
    t = T_ATT
    i = pl.program_id(2)
    lane = lax.broadcasted_iota(i32, (t, LANES), 1)
    lo_half = lane < HEAD_DIM

    @pl.when(i == 0)
    def _():
        kaug_ref[:, 0:LANES] = k_ref[0, 0]
        kaug_ref[:, LANES:2 * LANES] = kx_ref[0, 0]

    q = q_ref[0, 0].astype(f32)
    qx = qx_ref[0, 0].astype(f32)
    qaug_ref[0:t, 0:LANES] = jnp.where(lo_half, q, 0.0).astype(bf16)
    qaug_ref[t:2 * t, 0:LANES] = jnp.where(lo_half, 0.0, q).astype(bf16)
    qaug_ref[0:t, LANES:2 * LANES] = jnp.where(lane < XCOLS, qx, 0.0).astype(bf16)
    qaug_ref[t:2 * t, LANES:2 * LANES] = jnp.where((lane >= XCOLS) & (lane < 2 * XCOLS), qx, 0.0).astype(bf16)
    _flash_init(m_ref, l_ref, acc_ref)

    def logits(kt):
        return _dot_nt(qaug_ref[...], kaug_ref[pl.ds(kt * t, t), :])

    def causal(s):
        row = lax.broadcasted_iota(i32, (2 * t, t), 0) & (t - 1)
        return jnp.where(lax.broadcasted_iota(i32, (2 * t, t), 1) <= row, s, NEG_INF)

    _flash_loop(0, i, logits, lambda kt: v_ref[0, 0, pl.ds(kt * t, t), :], s_ref, m_ref, l_ref, acc_ref,
                last_fix=causal)
    o = _flash_finish(l_ref, acc_ref)
    o_ref[0] = jnp.where(lo_half, o[0:t], o[t:2 * t]).astype(o_ref.dtype)


def _fox(qkv, qx, kx):
    b, _, s, _ = qkv.shape
    t = T_ATT
    base = QKV_TILES_FOX
    return pl.pallas_call(
        _fox_kernel,
        grid=(b, FOX_PAIRS, s // t),
        in_specs=[pl.BlockSpec((1, 1, t, LANES), lambda bi, p, i: (bi, base + p, i, 0)),
                  pl.BlockSpec((1, 1, s, LANES), lambda bi, p, i: (bi, base + FOX_PAIRS + p, 0, 0)),
                  pl.BlockSpec((1, 1, s, LANES), lambda bi, p, i: (bi, base + 2 * FOX_PAIRS + p, 0, 0)),
                  pl.BlockSpec((1, 1, t, LANES), lambda bi, p, i: (bi, p, i, 0)),
                  pl.BlockSpec((1, 1, s, LANES), lambda bi, p, i: (bi, p, 0, 0))],
        out_specs=pl.BlockSpec((1, t, LANES), lambda bi, p, i: (bi, i, p)),
        out_shape=jax.ShapeDtypeStruct((b, s, FOX_W), bf16),
        scratch_shapes=[pltpu.VMEM((2 * t, 2 * LANES), bf16),
                        pltpu.VMEM((s, 2 * LANES), bf16),
                        pltpu.VMEM((2 * t, LANES), f32),
                        pltpu.VMEM((2 * t, LANES), f32),
                        pltpu.VMEM((2 * t, LANES), f32),
                        pltpu.VMEM((2, 2 * t, t), f32)],
        compiler_params=_cparams(("parallel", "parallel", "arbitrary")),
        name="fox",
    )(qkv, qkv, qkv, qx, kx)


def _merge_kernel(yn_ref, yf_ref, xb_ref, x_ref, wn_ref, wf_ref, wmg_ref, wo_ref, g_ref, b_ref,
                  xo_ref, xbo_ref):
    mg = _dot(xb_ref[...], wmg_ref[...])
    merged = (_sigmoid(mg[:, 0:D_MODEL]) * _dot(yn_ref[...], wn_ref[...])
              + _sigmoid(mg[:, D_MODEL:2 * D_MODEL]) * _dot(yf_ref[...], wf_ref[...]))
    hmix = _dot(merged.astype(bf16), wo_ref[...])
    xn = _layer_norm(DN_ALPHA * x_ref[...] + hmix, g_ref[...], b_ref[...])
    xo_ref[...] = xn
    xbo_ref[...] = xn.astype(bf16)


def _merge(yn, yf, xb, x, wn, wf, wmg, wo, g, bb):
    m = x.shape[0]
    tm = TM_MERGE
    row = lambda w: pl.BlockSpec((tm, w), lambda i: (i, 0))
    full = lambda a: pl.BlockSpec(a.shape, lambda i: (0, 0))
    return pl.pallas_call(
        _merge_kernel,
        grid=(m // tm,),
        in_specs=[row(NSA_Q_W), row(FOX_W), row(D_MODEL), row(D_MODEL),
                  full(wn), full(wf), full(wmg), full(wo), full(g), full(bb)],
        out_specs=[row(D_MODEL), row(D_MODEL)],
        out_shape=[jax.ShapeDtypeStruct((m, D_MODEL), f32), jax.ShapeDtypeStruct((m, D_MODEL), bf16)],
        compiler_params=_cparams(("parallel",)),
        name="merge",
    )(yn, yf, xb, x, wn, wf, wmg, wo, g, bb)


def _ffn_kernel(xb_ref, x_ref, wg_ref, wu_ref, wd_ref, g_ref, b_ref, xo_ref, xbo_ref, acc_ref, *, nf):
    f = pl.program_id(1)

    @pl.when(f == 0)
    def _():
        acc_ref[...] = jnp.zeros_like(acc_ref)

    xb = xb_ref[...]
    gate = _dot(xb, wg_ref[...])
    up = _dot(xb, wu_ref[...])
    act = (gate * _sigmoid(gate) * up).astype(bf16)
    acc_ref[...] += _dot(act, wd_ref[...])

    @pl.when(f == nf - 1)
    def _():
        xn = _layer_norm(DN_ALPHA * x_ref[...] + acc_ref[...], g_ref[...], b_ref[...])
        xo_ref[...] = xn
        xbo_ref[...] = xn.astype(bf16)


def _ffn(xb, x, wg, wu, wd, g, bb):
    m = x.shape[0]
    dff = wg.shape[1]
    tm, tf = TM_FFN, TF_FFN
    nf = dff // tf
    row = pl.BlockSpec((tm, D_MODEL), lambda i, f: (i, 0))
    vec = pl.BlockSpec((1, D_MODEL), lambda i, f: (0, 0))
    return pl.pallas_call(
        functools.partial(_ffn_kernel, nf=nf),
        grid=(m // tm, nf),
        in_specs=[row, row,
                  pl.BlockSpec((D_MODEL, tf), lambda i, f: (0, f)),
                  pl.BlockSpec((D_MODEL, tf), lambda i, f: (0, f)),
                  pl.BlockSpec((tf, D_MODEL), lambda i, f: (f, 0)),
                  vec, vec],
        out_specs=[row, row],
        out_shape=[jax.ShapeDtypeStruct((m, D_MODEL), f32), jax.ShapeDtypeStruct((m, D_MODEL), bf16)],
        scratch_shapes=[pltpu.VMEM((tm, D_MODEL), f32)],
        compiler_params=_cparams(("parallel", "arbitrary")),
        name="ffn",
    )(xb, x, wg, wu, wd, g, bb)


def _router_kernel(x_ref, r_ref, tri_ref, gate_ref, rank_ref, cnt_ref, *, seq):
    tb = tri_ref.shape[0]
    x = x_ref[0]
    xh = x.astype(bf16)
    xl = (x - xh.astype(f32)).astype(bf16)
    r = r_ref[...]
    rh = r.astype(bf16)
    rl = (r - rh.astype(f32)).astype(bf16)
    logits = _dot(xh, rh) + _dot(xh, rl) + _dot(xl, rh)
    lane = lax.broadcasted_iota(i32, (seq, LANES), 1).astype(f32)
    low = -3.0e38
    lg = jnp.where(lane < N_EXPERTS, logits, low)
    m1 = jnp.max(lg, axis=-1, keepdims=True)
    i1 = jnp.min(jnp.where(lg == m1, lane, float(LANES)), axis=-1, keepdims=True)
    lg2 = jnp.where(lane == i1, low, lg)
    m2 = jnp.max(lg2, axis=-1, keepdims=True)
    i2 = jnp.min(jnp.where(lg2 == m2, lane, float(LANES)), axis=-1, keepdims=True)
    e2 = jnp.exp(m2 - m1)
    den = 1.0 + e2
    gate_ref[0] = jnp.where(lane == i1, 1.0 / den, jnp.where(lane == i2, e2 / den, 0.0))
    sel = (lane == i1) | (lane == i2)
    selb = jnp.where(sel, 1.0, 0.0).astype(bf16)
    carry = jnp.zeros((1, LANES), f32)
    for blk in range(seq // tb):
        sl = slice(blk * tb, (blk + 1) * tb)
        c = _dot(tri_ref[...], selb[sl]) + carry
        carry = c[tb - 1:tb, :]
        rank_ref[0, sl, :] = jnp.where(sel[sl], c - 1.0, -1.0)
    cnt_ref[0] = carry.astype(i32)


def _router(x3, router_pad, tri):
    b, s, _ = x3.shape
    return pl.pallas_call(
        functools.partial(_router_kernel, seq=s),
        grid=(b,),
        in_specs=[pl.BlockSpec((1, s, D_MODEL), lambda i: (i, 0, 0)),
                  pl.BlockSpec((D_MODEL, LANES), lambda i: (0, 0)),
                  pl.BlockSpec(tri.shape, lambda i: (0, 0))],
        out_specs=[pl.BlockSpec((1, s, LANES), lambda i: (i, 0, 0)),
                   pl.BlockSpec((1, s, LANES), lambda i: (i, 0, 0)),
                   pl.BlockSpec((1, 1, LANES), lambda i: (i, 0, 0))],
        out_shape=[jax.ShapeDtypeStruct((b, s, LANES), f32),
                   jax.ShapeDtypeStruct((b, s, LANES), f32),
                   jax.ShapeDtypeStruct((b, 1, LANES), i32)],
        compiler_params=_cparams(("parallel",)),
        name="router",
    )(x3, router_pad, tri)


def _moe_kernel(cnt_ref, xb_ref, rankt_ref, gatet_ref, ranke_ref, wg_ref, wu_ref, wd_ref, y_ref,
                xg_ref, acc_ref, *, seq, nf):
    tr = TR_MOE
    c = pl.program_id(0)
    e = pl.program_id(1)
    f = pl.program_id(2)
    nsub = (cnt_ref[c * N_EXPERTS + e] + tr - 1) // tr

    def onehot_rows(s):
        rk = rankt_ref[0, pl.ds(e, 1), :]
        want = (s * tr + lax.broadcasted_iota(i32, (tr, seq), 0)).astype(f32)
        return rk == want

    def zero(j, carry):
        y_ref[0, pl.ds(j * tr, tr), :] = jnp.zeros((tr, D_MODEL), f32)
        return carry

    lax.fori_loop(0, jnp.where((e == 0) & (f == 0), seq // tr, 0), zero, 0)

    @pl.when(f == 0)
    def _():
        def gather(s, carry):
            p = jnp.where(onehot_rows(s), 1.0, 0.0).astype(bf16)
            xg_ref[pl.ds(s * tr, tr), :] = _dot(p, xb_ref[0]).astype(bf16)
            acc_ref[pl.ds(s * tr, tr), :] = jnp.zeros((tr, D_MODEL), f32)
            return carry
        lax.fori_loop(0, nsub, gather, 0)

    def hidden(s, carry):
        xs = xg_ref[pl.ds(s * tr, tr), :]
        gate = _dot(xs, wg_ref[0, 0])
        up = _dot(xs, wu_ref[0, 0])
        act = (gate * _sigmoid(gate) * up).astype(bf16)
        acc_ref[pl.ds(s * tr, tr), :] += _dot(act, wd_ref[0])
        return carry

    lax.fori_loop(0, nsub, hidden, 0)

    @pl.when(f == nf - 1)
    def _():
        def combine(s, carry):
            grow = gatet_ref[0, pl.ds(e, 1), :]
            w = jnp.sum(jnp.where(onehot_rows(s), grow, 0.0), axis=-1, keepdims=True)
            z = (acc_ref[pl.ds(s * tr, tr), :] * w).astype(bf16)
            want = (s * tr + lax.broadcasted_iota(i32, (512, tr), 1)).astype(f32)
            for jb in range(seq // 512):
                rc = ranke_ref[0, 0, jb * 512:(jb + 1) * 512, :]
                pt = jnp.where(rc == want, 1.0, 0.0).astype(bf16)
                y_ref[0, jb * 512:(jb + 1) * 512, :] += _dot(pt, z)
            return carry
        lax.fori_loop(0, nsub, combine, 0)


def _moe(xb3, gate, rank, cnt, wg, wu, wd):
    b, s, _ = xb3.shape
    dff = wg.shape[2]
    nf = dff // TF_MOE
    wg = wg.reshape(N_EXPERTS, D_MODEL, nf, TF_MOE).transpose(0, 2, 1, 3)
    wu = wu.reshape(N_EXPERTS, D_MODEL, nf, TF_MOE).transpose(0, 2, 1, 3)
    rankt = rank[:, :, :N_EXPERTS].transpose(0, 2, 1)
    gatet = gate[:, :, :N_EXPERTS].transpose(0, 2, 1)
    counts = cnt[:, 0, :N_EXPERTS].reshape(-1)
    grid_spec = pltpu.PrefetchScalarGridSpec(
        num_scalar_prefetch=1,
        grid=(b, N_EXPERTS, nf),
        in_specs=[pl.BlockSpec((1, s, D_MODEL), lambda c, e, f, cnt: (c, 0, 0)),
                  pl.BlockSpec((1, N_EXPERTS, s), lambda c, e, f, cnt: (c, 0, 0)),
                  pl.BlockSpec((1, N_EXPERTS, s), lambda c, e, f, cnt: (c, 0, 0)),
                  pl.BlockSpec((1, 1, s, 1), lambda c, e, f, cnt: (c, e, 0, 0)),
                  pl.BlockSpec((1, 1, D_MODEL, TF_MOE), lambda c, e, f, cnt: (e, f, 0, 0)),
                  pl.BlockSpec((1, 1, D_MODEL, TF_MOE), lambda c, e, f, cnt: (e, f, 0, 0)),
                  pl.BlockSpec((1, TF_MOE, D_MODEL), lambda c, e, f, cnt: (e, f, 0))],
        out_specs=pl.BlockSpec((1, s, D_MODEL), lambda c, e, f, cnt: (c, 0, 0)),
        scratch_shapes=[pltpu.VMEM((s, D_MODEL), bf16), pltpu.VMEM((s, D_MODEL), f32)],
    )
    return pl.pallas_call(
        functools.partial(_moe_kernel, seq=s, nf=nf),
        grid_spec=grid_spec,
        out_shape=jax.ShapeDtypeStruct((b, s, D_MODEL), f32),
        compiler_params=_cparams(("parallel", "arbitrary", "arbitrary")),
        name="moe",
    )(counts, xb3, rankt, gatet, rankt[..., None], wg, wu, wd)


def _resln_kernel(x_ref, y_ref, g_ref, b_ref, xo_ref, xbo_ref):
    xn = _layer_norm(DN_ALPHA * x_ref[...] + y_ref[...], g_ref[...], b_ref[...])
    xo_ref[...] = xn
    xbo_ref[...] = xn.astype(bf16)


def _resln(x, y, g, bb):
    m = x.shape[0]
    tm = TM_FFN
    row = pl.BlockSpec((tm, D_MODEL), lambda i: (i, 0))
    vec = pl.BlockSpec((1, D_MODEL), lambda i: (0, 0))
    return pl.pallas_call(
        _resln_kernel,
        grid=(m // tm,),
        in_specs=[row, row, vec, vec],
        out_specs=[row, row],
        out_shape=[jax.ShapeDtypeStruct((m, D_MODEL), f32), jax.ShapeDtypeStruct((m, D_MODEL), bf16)],
        compiler_params=_cparams(("parallel",)),
        name="resln",
    )(x, y, g, bb)


def _t5_bucket(dist):
    n = jnp.maximum(dist, 0)
    max_exact = REL_BUCKETS // 2
    large = max_exact + (jnp.log(jnp.maximum(n, 1).astype(f32) / max_exact)
                         / math.log(REL_MAX_DIST / max_exact) * (REL_BUCKETS - max_exact)).astype(i32)
    large = jnp.minimum(large, REL_BUCKETS - 1)
    return jnp.where(n < max_exact, n, large)


def _bias_of_dist(rel_bias, dist):
    onehot = (_t5_bucket(dist)[None] == jnp.arange(REL_BUCKETS).reshape((-1,) + (1,) * dist.ndim)).astype(f32)
    return LOG2E * jnp.einsum("kh,k...->h...", rel_bias.astype(f32), onehot, precision=lax.Precision.HIGHEST)


def _bias_tables(rel_bias, seq):
    t = T_ATT
    rows = NSA_GROUP * t
    ncmp = seq // CMP_STRIDE
    d0 = jnp.arange(t)[:, None] - jnp.arange(t)[None, :]
    offs = jnp.array([0, t, 2 * t, WINDOW]).reshape(4, 1, 1)
    kinds = _bias_of_dist(rel_bias, offs + d0[None])
    mask = jnp.stack([d0 >= 0, d0 == d0, d0 == d0, d0 < 0])
    kinds = jnp.where(mask[None], kinds, NEG_INF)
    tbias = kinds.reshape(NSA_KV_HEADS, NSA_GROUP, 4, t, t).transpose(0, 2, 1, 3, 4)
    tbias = tbias.reshape(NSA_KV_HEADS, 4, rows, t)
    cend = jnp.arange(ncmp) * CMP_STRIDE + CMP_BLOCK - 1
    bc = _bias_of_dist(rel_bias, jnp.arange(seq)[:, None] - cend[None, :])
    bc = bc.reshape(NSA_KV_HEADS, NSA_GROUP, seq // t, t, ncmp).transpose(0, 2, 4, 1, 3)
    return tbias, bc.reshape(NSA_KV_HEADS, (seq // t) * ncmp, rows)


def _selection_constants(seq):
    ncmp = seq // CMP_STRIDE
    nsel = seq // SEL_BLOCK
    c0 = np.arange(ncmp)[:, None] * CMP_STRIDE
    s0 = np.arange(LANES)[None, :] * SEL_BLOCK
    ov = np.maximum(np.minimum(c0 + CMP_BLOCK, s0 + SEL_BLOCK) - np.maximum(c0, s0), 0) / CMP_BLOCK
    ov[ncmp - 1, :] = 0.0
    ov[:, nsel:] = 0.0
    et = (np.arange(seq)[:, None] // SEL_BLOCK == np.arange(LANES)[None, :]).astype(np.float32)
    return jnp.asarray(ov.T, bf16), jnp.asarray(et, bf16)


def _fox_placement():
    xw = FOX_PAIRS * LANES
    pq = np.zeros((3, LANES, xw), np.float32)
    pk = np.zeros((3, LANES, xw), np.float32)
    oq = np.zeros((1, xw), np.float32)
    ok = np.zeros((1, xw), np.float32)
    for p in range(FOX_PAIRS):
        for hh in range(2):
            src = FGATE_LANE + 2 * p + hh
            base = p * LANES + hh * XCOLS
            for part in range(3):
                pk[part, src, base + part] = -1.0
                pq[part, src, base + 3 + part] = 1.0
                oq[0, base + part] = 1.0
                ok[0, base + 3 + part] = 1.0
    return jnp.asarray(pq, bf16), jnp.asarray(pk, bf16), jnp.asarray(oq), jnp.asarray(ok)


def _layer_weights(w_in, layer_pe, w1, w2, f_bias):
    offs = np.cumsum((NSA_Q_W, 6 * 2 * HEAD_DIM, 3 * NSA_HEADS, 3 * FOX_W, FOX_HEADS, 2 * D_MODEL))
    kv0, g0, fx0, ff0, mg0 = offs[0], offs[1], offs[2], offs[3], offs[4]
    scale = HEAD_DIM ** -0.5 * LOG2E
    kvw = NSA_KV_HEADS * HEAD_DIM
    w_kv = w_in[:, kv0:g0]
    w_kvdup = jnp.repeat(w_kv[:, 2 * kvw:].reshape(D_MODEL, 4 * NSA_KV_HEADS, 1, HEAD_DIM), 2, axis=2)
    w_kvdup = w_kvdup.reshape(D_MODEL, 8 * kvw)
    w_qkv = jnp.concatenate([w_in[:, :NSA_Q_W] * scale, w_kvdup, w_in[:, fx0:fx0 + FOX_W] * scale,
                             w_in[:, fx0 + FOX_W:ff0]], axis=1).astype(bf16)
    zeros = lambda n: jnp.zeros((D_MODEL, n), w_in.dtype)
    ng = 3 * NSA_GROUP
    w_g = w_in[:, g0:fx0].reshape(D_MODEL, NSA_KV_HEADS, NSA_GROUP, 3).transpose(0, 1, 3, 2)
    gate_cols = lambda h: w_g[:, h].reshape(D_MODEL, ng)
    w_aux = jnp.concatenate([w_kv[:, :2 * kvw],
                             gate_cols(0), zeros(FGATE_LANE - ng), w_in[:, ff0:mg0],
                             zeros(LANES - FGATE_LANE - FOX_HEADS),
                             gate_cols(1), zeros(LANES - ng)], axis=1).astype(bf16)
    fb_row = jnp.zeros((1, LANES), f32).at[0, FGATE_LANE:FGATE_LANE + FOX_HEADS].set(f_bias.astype(f32))
    pe2 = jnp.tile(layer_pe.astype(f32), (1, 1, NSA_KV_HEADS))
    w1r = w1.reshape(2, CMP_BLOCK, HEAD_DIM, CMP_HIDDEN).astype(bf16)
    zero = jnp.zeros_like(w1r)
    w1bd = jnp.concatenate([jnp.concatenate([w1r, zero], axis=-1),
                            jnp.concatenate([zero, w1r], axis=-1)], axis=-2)
    w2d = jnp.concatenate([w2, w2], axis=-1).astype(bf16)
    return (w_qkv, w_aux, w_in[:, mg0:].astype(bf16), fb_row, pe2, w1bd[:, :CMP_STRIDE], w1bd[:, CMP_STRIDE:],
            w2d)


def kernel(x, w_in, nsa_cmp_pe, nsa_cmp_w1, nsa_cmp_w2, fox_f_bias, w_nsa_branch, w_fox_branch, w_out,
           rel_bias, ln1_g, ln1_b, ln2_g, ln2_b, dense_w_gate, dense_w_up, dense_w_down, moe_router,
           moe_w_gate, moe_w_up, moe_w_down):
    b, s, d = x.shape
    m = b * s
    tbias, bias_c = _bias_tables(rel_bias, s)
    ov, et = _selection_constants(s)
    pq, pk, oq, ok = _fox_placement()
    tri128 = jnp.asarray(np.tril(np.ones((LANES, LANES), np.float32)), bf16)
    tri256 = jnp.asarray(np.tril(np.ones((256, 256), np.float32)), bf16)

    xf = x.reshape(m, d).astype(f32)
    xb = xf.astype(bf16)
    for layer in range(DEPTH):
        w_qkv, w_aux, w_mg, fb_row, pe2, w1t, w1b, w2d = _layer_weights(
            w_in[layer], nsa_cmp_pe[layer], nsa_cmp_w1[layer], nsa_cmp_w2[layer], fox_f_bias[layer])
        qkv = _proj_tiles(xb, w_qkv, b, bf16)
        aux = _proj(xb, w_aux, f32).reshape(b, s, AUX_W)
        gates, qx, kx = _gateprep(aux, fb_row, tri128, pq, pk, oq, ok)
        cmpkv = _compress(aux, pe2, w1t, w1b, w2d)
        y_nsa = _nsa(qkv, cmpkv, gates, bias_c, tbias, ov, et).reshape(m, NSA_Q_W)
        y_fox = _fox(qkv, qx, kx).reshape(m, FOX_W)
        xf, xb = _merge(y_nsa, y_fox, xb, xf, w_nsa_branch[layer].astype(bf16),
                        w_fox_branch[layer].astype(bf16), w_mg, w_out[layer].astype(bf16),
                        ln1_g[layer].reshape(1, d), ln1_b[layer].reshape(1, d))
        j = layer // 2
        g2, b2 = ln2_g[layer].reshape(1, d), ln2_b[layer].reshape(1, d)
        if layer % 2 == 0:
            xf, xb = _ffn(xb, xf, dense_w_gate[j].astype(bf16), dense_w_up[j].astype(bf16),
                          dense_w_down[j].astype(bf16), g2, b2)
        else:
            router_pad = jnp.zeros((d, LANES), f32).at[:, :N_EXPERTS].set(moe_router[j].astype(f32))
            gate, rank, cnt = _router(xf.reshape(b, s, d), router_pad, tri256)
            y = _moe(xb.reshape(b, s, d), gate, rank, cnt,
                     moe_w_gate[j].astype(bf16), moe_w_up[j].astype(bf16), moe_w_down[j].astype(bf16))
            xf, xb = _resln(xf, y.reshape(m, d), g2, b2)
    return xf.reshape(b, s, d).astype(x.dtype)
```

```python
import functools
import math

import numpy as np
import jax
import jax.numpy as jnp
from jax import lax
from jax.experimental import pallas as pl
from jax.experimental.pallas import tpu as pltpu

f32 = jnp.float32
bf16 = jnp.bfloat16
i32 = jnp.int32

D_MODEL = 1024
HEAD_DIM = 64
LANES = 128
NSA_HEADS = 8
NSA_KV_HEADS = 2
NSA_GROUP = NSA_HEADS // NSA_KV_HEADS
FOX_HEADS = 8
FOX_PAIRS = FOX_HEADS // 2
CMP_BLOCK = 32
CMP_STRIDE = 16
CMP_HIDDEN = 128
SEL_BLOCK = 64
N_SELECT = 16
N_LOCAL_SEL = 2
WINDOW = 512
REL_BUCKETS = 32
REL_MAX_DIST = 128
N_EXPERTS = 8
DEPTH = 4
DN_ALPHA = (2 * DEPTH) ** 0.25
LN_EPS = 1e-5
FORCED_SCORE = 1e4
NEG_INF = -1e30
LOG2E = math.log2(math.e)

NSA_Q_W = NSA_HEADS * HEAD_DIM
FOX_W = FOX_HEADS * HEAD_DIM
QKV_TILES_KV = NSA_Q_W // LANES
QKV_TILES_FOX = QKV_TILES_KV + 4 * NSA_KV_HEADS
QKV_W = (QKV_TILES_FOX + 3 * FOX_PAIRS) * LANES
AUX_W = 4 * LANES
FGATE_LANE = 24
XCOLS = 6

T_ATT = 256
TM_PROJ = 512
TM_MERGE = 256
TM_FFN = 512
TF_FFN = 1408
TR_MOE = 256
TF_MOE = 896
VMEM_LIMIT = 56 * 1024 * 1024


def _cparams(sem):
    return pltpu.CompilerParams(dimension_semantics=sem, vmem_limit_bytes=VMEM_LIMIT)


def _dot(a, b):
    return jnp.dot(a, b, preferred_element_type=f32)


def _dot_nt(a, b):
    return lax.dot_general(a, b, (((1,), (1,)), ((), ())), preferred_element_type=f32)


def _sigmoid(x):
    return 1.0 / (1.0 + jnp.exp(-x))


def _layer_norm(z, g, b):
    mu = jnp.mean(z, axis=-1, keepdims=True)
    zc = z - mu
    var = jnp.mean(zc * zc, axis=-1, keepdims=True)
    return zc * lax.rsqrt(var + LN_EPS) * g + b


def _split3(x):
    hi = x.astype(bf16)
    r1 = x - hi.astype(f32)
    mid = r1.astype(bf16)
    lo = (r1 - mid.astype(f32)).astype(bf16)
    return hi, mid, lo


def _proj_kernel(x_ref, w_ref, o_ref):
    o_ref[...] = _dot(x_ref[...], w_ref[...]).astype(o_ref.dtype)


def _proj(xb, w, out_dtype):
    m, k = xb.shape
    n = w.shape[1]
    return pl.pallas_call(
        _proj_kernel,
        grid=(m // TM_PROJ,),
        in_specs=[pl.BlockSpec((TM_PROJ, k), lambda i: (i, 0)),
                  pl.BlockSpec((k, n), lambda i: (0, 0))],
        out_specs=pl.BlockSpec((TM_PROJ, n), lambda i: (i, 0)),
        out_shape=jax.ShapeDtypeStruct((m, n), out_dtype),
        compiler_params=_cparams(("parallel",)),
        name="proj",
    )(xb, w)


def _proj_tiles_kernel(x_ref, w_ref, o_ref):
    res = _dot(x_ref[...], w_ref[...])
    for j in range(o_ref.shape[1]):
        o_ref[0, j] = res[:, j * LANES:(j + 1) * LANES].astype(o_ref.dtype)


def _proj_tiles(xb, w, batch, out_dtype):
    m, k = xb.shape
    n = w.shape[1]
    nb = m // batch // TM_PROJ
    return pl.pallas_call(
        _proj_tiles_kernel,
        grid=(m // TM_PROJ,),
        in_specs=[pl.BlockSpec((TM_PROJ, k), lambda i: (i, 0)),
                  pl.BlockSpec((k, n), lambda i: (0, 0))],
        out_specs=pl.BlockSpec((1, n // LANES, TM_PROJ, LANES), lambda i: (i // nb, 0, i % nb, 0)),
        out_shape=jax.ShapeDtypeStruct((batch, n // LANES, m // batch, LANES), out_dtype),
        compiler_params=_cparams(("parallel",)),
        name="proj_tiles",
    )(xb, w)


def _gateprep_kernel(a_ref, fb_ref, tri_ref, pq_ref, pk_ref, oq_ref, ok_ref, g_ref, qx_ref, kx_ref, *, seq):
    tb = LANES
    tri = tri_ref[...]
    carry = jnp.zeros((1, LANES), f32)
    for blk in range(seq // tb):
        sl = slice(blk * tb, (blk + 1) * tb)
        va = a_ref[0, sl, 0:LANES]
        g_ref[0, 0, sl, :] = _sigmoid(va)
        g_ref[0, 1, sl, :] = _sigmoid(a_ref[0, sl, LANES:2 * LANES])
        z = va + fb_ref[...]
        logf = jnp.minimum(z, 0.0) - jnp.log1p(jnp.exp(-jnp.abs(z)))
        hi, mid, lo = _split3(logf)
        c = _dot(tri, hi) + _dot(tri, mid) + _dot(tri, lo) + carry
        carry = c[tb - 1:tb, :]
        chi, cmid, clo = _split3(c * LOG2E)
        qx = _dot(chi, pq_ref[0]) + _dot(cmid, pq_ref[1]) + _dot(clo, pq_ref[2]) + oq_ref[...]
        kx = _dot(chi, pk_ref[0]) + _dot(cmid, pk_ref[1]) + _dot(clo, pk_ref[2]) + ok_ref[...]
        for p in range(FOX_PAIRS):
            qx_ref[0, p, sl, :] = qx[:, p * LANES:(p + 1) * LANES].astype(bf16)
            kx_ref[0, p, sl, :] = kx[:, p * LANES:(p + 1) * LANES].astype(bf16)


def _gateprep(aux, fb_row, tri, pq, pk, oq, ok):
    b, s, _ = aux.shape
    const2 = lambda a: pl.BlockSpec(a.shape, lambda i: (0, 0))
    const3 = lambda a: pl.BlockSpec(a.shape, lambda i: (0, 0, 0))
    return pl.pallas_call(
        functools.partial(_gateprep_kernel, seq=s),
        grid=(b,),
        in_specs=[pl.BlockSpec((1, s, 2 * LANES), lambda i: (i, 0, 1)),
                  const2(fb_row), const2(tri), const3(pq), const3(pk), const2(oq), const2(ok)],
        out_specs=[pl.BlockSpec((1, 2, s, LANES), lambda i: (i, 0, 0, 0)),
                   pl.BlockSpec((1, FOX_PAIRS, s, LANES), lambda i: (i, 0, 0, 0)),
                   pl.BlockSpec((1, FOX_PAIRS, s, LANES), lambda i: (i, 0, 0, 0))],
        out_shape=[jax.ShapeDtypeStruct((b, 2, s, LANES), f32),
                   jax.ShapeDtypeStruct((b, FOX_PAIRS, s, LANES), bf16),
                   jax.ShapeDtypeStruct((b, FOX_PAIRS, s, LANES), bf16)],
        compiler_params=_cparams(("parallel",)),
        name="gateprep",
    )(aux, fb_row, tri, pq, pk, oq, ok)


def _gelu_tanh(x):
    c = math.sqrt(2.0 / math.pi)
    return x * (0.5 * (1.0 + jnp.tanh(c * (x + 0.044715 * (x * x * x)))))


def _compress_kernel(a_ref, pe_ref, w1t_ref, w1b_ref, w2_ref, o_ref, *, nhalf):
    top = jnp.zeros((nhalf, NSA_KV_HEADS * CMP_HIDDEN), f32)
    bot = jnp.zeros((nhalf, NSA_KV_HEADS * CMP_HIDDEN), f32)
    for l in range(CMP_STRIDE):
        rows = a_ref[0, pl.ds(l, nhalf, stride=CMP_STRIDE), :]
        top = top + _dot((rows + pe_ref[0, l:l + 1, :]).astype(bf16), w1t_ref[0, l])
        bot = bot + _dot((rows + pe_ref[0, CMP_STRIDE + l:CMP_STRIDE + l + 1, :]).astype(bf16), w1b_ref[0, l])
    pre = top + pltpu.roll(bot, nhalf - 1, 0)
    act = _gelu_tanh(pre).astype(bf16)
    for h in range(NSA_KV_HEADS):
        o_ref[0, 0, h] = _dot(act[:, h * CMP_HIDDEN:(h + 1) * CMP_HIDDEN], w2_ref[0]).astype(o_ref.dtype)


def _compress(aux, pe2, w1t, w1b, w2d):
    b, s, _ = aux.shape
    nhalf = s // CMP_STRIDE
    w1_spec = pl.BlockSpec((1,) + w1t.shape[1:], lambda i, j: (j, 0, 0, 0))
    return pl.pallas_call(
        functools.partial(_compress_kernel, nhalf=nhalf),
        grid=(b, 2),
        in_specs=[pl.BlockSpec((1, s, LANES), lambda i, j: (i, 0, j)),
                  pl.BlockSpec((1, CMP_BLOCK, LANES), lambda i, j: (j, 0, 0)),
                  w1_spec, w1_spec,
                  pl.BlockSpec((1, CMP_HIDDEN, LANES), lambda i, j: (j, 0, 0))],
        out_specs=pl.BlockSpec((1, 1, NSA_KV_HEADS, nhalf, LANES), lambda i, j: (i, j, 0, 0, 0)),
        out_shape=jax.ShapeDtypeStruct((b, 2, NSA_KV_HEADS, nhalf, LANES), bf16),
        compiler_params=_cparams(("parallel", "parallel")),
        name="compress",
    )(aux, pe2, w1t, w1b, w2d)


def _flash_init(m_ref, l_ref, acc_ref):
    m_ref[...] = jnp.full(m_ref.shape, NEG_INF, f32)
    l_ref[...] = jnp.zeros(l_ref.shape, f32)
    acc_ref[...] = jnp.zeros(acc_ref.shape, f32)


def _flash_step(s, v, m_ref, l_ref, acc_ref):
    nk = s.shape[1] // LANES
    cols = [s[:, c * LANES:(c + 1) * LANES] for c in range(nk)]
    mx = cols[0]
    for c in cols[1:]:
        mx = jnp.maximum(mx, c)
    m_old = m_ref[...]
    m_new = jnp.maximum(m_old, jnp.broadcast_to(jnp.max(mx, axis=-1, keepdims=True), m_old.shape))
    alpha = jnp.exp2(m_old - m_new)
    ps = [jnp.exp2(c - m_new) for c in cols]
    psum = ps[0]
    for p in ps[1:]:
        psum = psum + p
    l_ref[...] = alpha * l_ref[...] + psum
    p = jnp.concatenate([x.astype(bf16) for x in ps], axis=1)
    acc_ref[...] = alpha * acc_ref[...] + _dot(p, v)
    m_ref[...] = m_new


def _flash_loop(first, last, logits_fn, v_fn, s_ref, m_ref, l_ref, acc_ref, last_fix=lambda s: s):
    sa, sb = s_ref.at[0], s_ref.at[1]
    n = last - first + 1
    pairs = (n - 1) // 2

    def step(buf, kt, fix=lambda s: s):
        _flash_step(fix(buf[...]), v_fn(kt), m_ref, l_ref, acc_ref)

    sa[...] = logits_fn(first)

    def body(j, carry):
        kt = first + 2 * j
        sb[...] = logits_fn(kt + 1)
        step(sa, kt)
        sa[...] = logits_fn(kt + 2)
        step(sb, kt + 1)
        return carry

    lax.fori_loop(0, pairs, body, 0)
    two_left = n - 2 * pairs == 2

    @pl.when(two_left)
    def _():
        sb[...] = logits_fn(last)
        step(sa, last - 1)
        step(sb, last, last_fix)

    @pl.when(jnp.logical_not(two_left))
    def _():
        step(sa, last, last_fix)


def _flash_finish(l_ref, acc_ref):
    l = jnp.sum(l_ref[...], axis=-1, keepdims=True)
    return acc_ref[...] * (1.0 / jnp.maximum(l, 1e-30))


def _nsa_kernel(q_ref, ks_ref, vs_ref, kw_ref, vw_ref, kc_ref, vc_ref, gate_ref, bc_ref, tb_ref, ov_ref,
                et_ref, o_ref, qaug_ref, kaug_ref, m_ref, l_ref, acc_ref, os_ref, s_ref, *, seq, ncmp):
    t = T_ATT
    rows = NSA_GROUP * t
    nsel = seq // SEL_BLOCK
    topn = min(N_SELECT, nsel)
    i = pl.program_id(2)
    t0 = i * t
    lane = lax.broadcasted_iota(i32, (t, LANES), 1)
    lo_half = lane < HEAD_DIM

    @pl.when(i == 0)
    def _():
        kaug_ref[:, 0:LANES] = ks_ref[0, 0]
        kaug_ref[:, LANES:2 * LANES] = et_ref[...]

    for g in range(NSA_GROUP):
        qg = q_ref[0, g // 2].astype(f32)
        keep = lo_half if g % 2 == 0 else jnp.logical_not(lo_half)
        qaug_ref[g * t:(g + 1) * t, 0:LANES] = jnp.where(keep, qg, 0.0).astype(bf16)
    qs = qaug_ref[:, 0:LANES]

    s = _dot_nt(kc_ref[0, 0, 0], qs) + bc_ref[0]
    row_t = t0 + (lax.broadcasted_iota(i32, (ncmp, rows), 1) & (t - 1))
    cmp_end = lax.broadcasted_iota(i32, (ncmp, rows), 0) * CMP_STRIDE + (CMP_BLOCK - 1)
    vis = cmp_end <= row_t
    s = jnp.where(vis, s, NEG_INF)
    s = s - jnp.max(s, axis=0, keepdims=True)
    e = jnp.where(vis, jnp.exp2(s), 0.0)
    p_c = e * (1.0 / jnp.maximum(jnp.sum(e, axis=0, keepdims=True), 1e-30))
    o_c = _dot(p_c.T.astype(bf16), vc_ref[0, 0, 0])

    psum = p_c[:, 0:t]
    for g in range(1, NSA_GROUP):
        psum = psum + p_c[:, g * t:(g + 1) * t]
    hi = psum.astype(bf16)
    lo = (psum - hi.astype(f32)).astype(bf16)
    imp = (_dot(ov_ref[...], hi) + _dot(ov_ref[...], lo))[0:nsel]
    blk = lax.broadcasted_iota(i32, (nsel, t), 0)
    cur = (t0 + lax.broadcasted_iota(i32, (nsel, t), 1)) >> 6
    forced = (blk == 0) | ((blk <= cur) & (blk > cur - N_LOCAL_SEL))
    val = jnp.where(forced, FORCED_SCORE, jnp.where(blk <= cur, imp, -1.0))
    cnt = jnp.zeros((nsel, t), f32)
    for j in range(nsel):
        vj = val[j:j + 1, :]
        beats = (vj > val) | ((vj == val) & (blk > j))
        cnt = cnt + jnp.where(beats, 1.0, 0.0)
    mneg = jnp.where(cnt < topn, 0.0, NEG_INF)
    mneg = jnp.concatenate([mneg, jnp.zeros((LANES - nsel, t), f32)], axis=0).T.astype(bf16)
    for g in range(NSA_GROUP):
        qaug_ref[g * t:(g + 1) * t, LANES:2 * LANES] = mneg

    _flash_init(m_ref, l_ref, acc_ref)

    def sel_logits(kt):
        k = kaug_ref[pl.ds(kt * t, t), :]
        return _dot_nt(qaug_ref[...], k) + tb_ref[0, jnp.minimum(i - kt, 2)]

    _flash_loop(0, i, sel_logits, lambda kt: vs_ref[0, 0, pl.ds(kt * t, t), :], s_ref, m_ref, l_ref, acc_ref)
    os_ref[...] = _flash_finish(l_ref, acc_ref)

    _flash_init(m_ref, l_ref, acc_ref)
    nwin = WINDOW // t

    def win_logits(kt):
        d = i - kt
        kind = jnp.where(d == nwin, 3, d)
        return _dot_nt(qs, kw_ref[0, 0, pl.ds(kt * t, t), :]) + tb_ref[0, kind]

    _flash_loop(jnp.maximum(i - nwin, 0), i, win_logits, lambda kt: vw_ref[0, 0, pl.ds(kt * t, t), :],
                s_ref, m_ref, l_ref, acc_ref)
    o_w = _flash_finish(l_ref, acc_ref)
    o_s = os_ref[...]

    gates = gate_ref[0, 0]
    outs = []
    for g in range(NSA_GROUP):
        sl = slice(g * t, (g + 1) * t)
        outs.append(gates[:, g:g + 1] * o_c[sl]
                    + gates[:, NSA_GROUP + g:NSA_GROUP + g + 1] * o_s[sl]
                    + gates[:, 2 * NSA_GROUP + g:2 * NSA_GROUP + g + 1] * o_w[sl])
    for j in range(NSA_GROUP // 2):
        o_ref[0, :, j * LANES:(j + 1) * LANES] = jnp.where(lo_half, outs[2 * j], outs[2 * j + 1]).astype(o_ref.dtype)


def _nsa(qkv, cmpkv, gates, bias_c, tbias, ov, et):
    b, _, s, _ = qkv.shape
    ncmp = cmpkv.shape[3]
    t = T_ATT
    rows = NSA_GROUP * t
    qw = NSA_GROUP * HEAD_DIM
    qtiles = qw // LANES
    kv_spec = lambda col: pl.BlockSpec((1, 1, s, LANES), lambda h, bi, i, col=col: (bi, col + h, 0, 0))
    cmp_spec = lambda kv: pl.BlockSpec((1, 1, 1, ncmp, LANES), lambda h, bi, i, kv=kv: (bi, kv, h, 0, 0))
    base = QKV_TILES_KV
    return pl.pallas_call(
        functools.partial(_nsa_kernel, seq=s, ncmp=ncmp),
        grid=(NSA_KV_HEADS, b, s // t),
        in_specs=[pl.BlockSpec((1, qtiles, t, LANES), lambda h, bi, i: (bi, h, i, 0)),
                  kv_spec(base), kv_spec(base + 2), kv_spec(base + 4), kv_spec(base + 6),
                  cmp_spec(0), cmp_spec(1),
                  pl.BlockSpec((1, 1, t, LANES), lambda h, bi, i: (bi, h, i, 0)),
                  pl.BlockSpec((1, ncmp, rows), lambda h, bi, i: (h, i, 0)),
                  pl.BlockSpec((1, 4, rows, t), lambda h, bi, i: (h, 0, 0, 0)),
                  pl.BlockSpec((LANES, ncmp), lambda h, bi, i: (0, 0)),
                  pl.BlockSpec((s, LANES), lambda h, bi, i: (0, 0))],
        out_specs=pl.BlockSpec((1, t, qw), lambda h, bi, i: (bi, i, h)),
        out_shape=jax.ShapeDtypeStruct((b, s, NSA_Q_W), bf16),
        scratch_shapes=[pltpu.VMEM((rows, 2 * LANES), bf16),
                        pltpu.VMEM((s, 2 * LANES), bf16),
                        pltpu.VMEM((rows, LANES), f32),
                        pltpu.VMEM((rows, LANES), f32),
                        pltpu.VMEM((rows, LANES), f32),
                        pltpu.VMEM((rows, LANES), f32),
                        pltpu.VMEM((2, rows, t), f32)],
        compiler_params=_cparams(("parallel", "parallel", "arbitrary")),
        name="nsa",
    )(qkv, qkv, qkv, qkv, qkv, cmpkv, cmpkv, gates, bias_c, tbias, ov, et)


def _fox_kernel(q_ref, k_ref, v_ref, qx_ref, kx_ref, o_ref, qaug_ref, kaug_ref, m_ref, l_ref, acc_ref,
                s_ref):
    t = T_ATT
    i = pl.program_id(2)
    lane = lax.broadcasted_iota(i32, (t, LANES), 1)
    lo_half = lane < HEAD_DIM

    @pl.when(i == 0)
    def _():
        kaug_ref[:, 0:LANES] = k_ref[0, 0]
        kaug_ref[:, LANES:2 * LANES] = kx_ref[0, 0]

    q = q_ref[0, 0].astype(f32)
    qx = qx_ref[0, 0].astype(f32)
    qaug_ref[0:t, 0:LANES] = jnp.where(lo_half, q, 0.0).astype(bf16)
    qaug_ref[t:2 * t, 0:LANES] = jnp.where(lo_half, 0.0, q).astype(bf16)
    qaug_ref[0:t, LANES:2 * LANES] = jnp.where(lane < XCOLS, qx, 0.0).astype(bf16)
    qaug_ref[t:2 * t, LANES:2 * LANES] = jnp.where((lane >= XCOLS) & (lane < 2 * XCOLS), qx, 0.0).astype(bf16)
    _flash_init(m_ref, l_ref, acc_ref)

    def logits(kt):
        return _dot_nt(qaug_ref[...], kaug_ref[pl.ds(kt * t, t), :])

    def causal(s):
        row = lax.broadcasted_iota(i32, (2 * t, t), 0) & (t - 1)
        return jnp.where(lax.broadcasted_iota(i32, (2 * t, t), 1) <= row, s, NEG_INF)

    _flash_loop(0, i, logits, lambda kt: v_ref[0, 0, pl.ds(kt * t, t), :], s_ref, m_ref, l_ref, acc_ref,
                last_fix=causal)
    o = _flash_finish(l_ref, acc_ref)
    o_ref[0] = jnp.where(lo_half, o[0:t], o[t:2 * t]).astype(o_ref.dtype)


def _fox(qkv, qx, kx):
    b, _, s, _ = qkv.shape
    t = T_ATT
    base = QKV_TILES_FOX
    return pl.pallas_call(
        _fox_kernel,
        grid=(b, FOX_PAIRS, s // t),
        in_specs=[pl.BlockSpec((1, 1, t, LANES), lambda bi, p, i: (bi, base + p, i, 0)),
                  pl.BlockSpec((1, 1, s, LANES), lambda bi, p, i: (bi, base + FOX_PAIRS + p, 0, 0)),
                  pl.BlockSpec((1, 1, s, LANES), lambda bi, p, i: (bi, base + 2 * FOX_PAIRS + p, 0, 0)),
                  pl.BlockSpec((1, 1, t, LANES), lambda bi, p, i: (bi, p, i, 0)),
                  pl.BlockSpec((1, 1, s, LANES), lambda bi, p, i: (bi, p, 0, 0))],
        out_specs=pl.BlockSpec((1, t, LANES), lambda bi, p, i: (bi, i, p)),
        out_shape=jax.ShapeDtypeStruct((b, s, FOX_W), bf16),
        scratch_shapes=[pltpu.VMEM((2 * t, 2 * LANES), bf16),
                        pltpu.VMEM((s, 2 * LANES), bf16),
                        pltpu.VMEM((2 * t, LANES), f32),
                        pltpu.VMEM((2 * t, LANES), f32),
                        pltpu.VMEM((2 * t, LANES), f32),
                        pltpu.VMEM((2, 2 * t, t), f32)],
        compiler_params=_cparams(("parallel", "parallel", "arbitrary")),
        name="fox",
    )(qkv, qkv, qkv, qx, kx)


def _merge_kernel(yn_ref, yf_ref, xb_ref, x_ref, wn_ref, wf_ref, wmg_ref, wo_ref, g_ref, b_ref,
                  xo_ref, xbo_ref):
    mg = _dot(xb_ref[...], wmg_ref[...])
    merged = (_sigmoid(mg[:, 0:D_MODEL]) * _dot(yn_ref[...], wn_ref[...])
              + _sigmoid(mg[:, D_MODEL:2 * D_MODEL]) * _dot(yf_ref[...], wf_ref[...]))
    hmix = _dot(merged.astype(bf16), wo_ref[...])
    xn = _layer_norm(DN_ALPHA * x_ref[...] + hmix, g_ref[...], b_ref[...])
    xo_ref[...] = xn
    xbo_ref[...] = xn.astype(bf16)


def _merge(yn, yf, xb, x, wn, wf, wmg, wo, g, bb):
    m = x.shape[0]
    tm = TM_MERGE
    row = lambda w: pl.BlockSpec((tm, w), lambda i: (i, 0))
    full = lambda a: pl.BlockSpec(a.shape, lambda i: (0, 0))
    return pl.pallas_call(
        _merge_kernel,
        grid=(m // tm,),
        in_specs=[row(NSA_Q_W), row(FOX_W), row(D_MODEL), row(D_MODEL),
                  full(wn), full(wf), full(wmg), full(wo), full(g), full(bb)],
        out_specs=[row(D_MODEL), row(D_MODEL)],
        out_shape=[jax.ShapeDtypeStruct((m, D_MODEL), f32), jax.ShapeDtypeStruct((m, D_MODEL), bf16)],
        compiler_params=_cparams(("parallel",)),
        name="merge",
    )(yn, yf, xb, x, wn, wf, wmg, wo, g, bb)


def _ffn_kernel(xb_ref, x_ref, wg_ref, wu_ref, wd_ref, g_ref, b_ref, xo_ref, xbo_ref, acc_ref, *, nf):
    f = pl.program_id(1)

    @pl.when(f == 0)
    def _():
        acc_ref[...] = jnp.zeros_like(acc_ref)

    xb = xb_ref[...]
    gate = _dot(xb, wg_ref[...])
    up = _dot(xb, wu_ref[...])
    act = (gate * _sigmoid(gate) * up).astype(bf16)
    acc_ref[...] += _dot(act, wd_ref[...])

    @pl.when(f == nf - 1)
    def _():
        xn = _layer_norm(DN_ALPHA * x_ref[...] + acc_ref[...], g_ref[...], b_ref[...])
        xo_ref[...] = xn
        xbo_ref[...] = xn.astype(bf16)


def _ffn(xb, x, wg, wu, wd, g, bb):
    m = x.shape[0]
    dff = wg.shape[1]
    tm, tf = TM_FFN, TF_FFN
    nf = dff // tf
    row = pl.BlockSpec((tm, D_MODEL), lambda i, f: (i, 0))
    vec = pl.BlockSpec((1, D_MODEL), lambda i, f: (0, 0))
    return pl.pallas_call(
        functools.partial(_ffn_kernel, nf=nf),
        grid=(m // tm, nf),
        in_specs=[row, row,
                  pl.BlockSpec((D_MODEL, tf), lambda i, f: (0, f)),
                  pl.BlockSpec((D_MODEL, tf), lambda i, f: (0, f)),
                  pl.BlockSpec((tf, D_MODEL), lambda i, f: (f, 0)),
                  vec, vec],
        out_specs=[row, row],
        out_shape=[jax.ShapeDtypeStruct((m, D_MODEL), f32), jax.ShapeDtypeStruct((m, D_MODEL), bf16)],
        scratch_shapes=[pltpu.VMEM((tm, D_MODEL), f32)],
        compiler_params=_cparams(("parallel", "arbitrary")),
        name="ffn",
    )(xb, x, wg, wu, wd, g, bb)


def _router_kernel(x_ref, r_ref, tri_ref, gate_ref, rank_ref, cnt_ref, *, seq):
    tb = tri_ref.shape[0]
    x = x_ref[0]
    xh = x.astype(bf16)
    xl = (x - xh.astype(f32)).astype(bf16)
    r = r_ref[...]
    rh = r.astype(bf16)
    rl = (r - rh.astype(f32)).astype(bf16)
    logits = _dot(xh, rh) + _dot(xh, rl) + _dot(xl, rh)
    lane = lax.broadcasted_iota(i32, (seq, LANES), 1).astype(f32)
    low = -3.0e38
    lg = jnp.where(lane < N_EXPERTS, logits, low)
    m1 = jnp.max(lg, axis=-1, keepdims=True)
    i1 = jnp.min(jnp.where(lg == m1, lane, float(LANES)), axis=-1, keepdims=True)
    lg2 = jnp.where(lane == i1, low, lg)
    m2 = jnp.max(lg2, axis=-1, keepdims=True)
    i2 = jnp.min(jnp.where(lg2 == m2, lane, float(LANES)), axis=-1, keepdims=True)
    e2 = jnp.exp(m2 - m1)
    den = 1.0 + e2
    gate_ref[0] = jnp.where(lane == i1, 1.0 / den, jnp.where(lane == i2, e2 / den, 0.0))
    sel = (lane == i1) | (lane == i2)
    selb = jnp.where(sel, 1.0, 0.0).astype(bf16)
    carry = jnp.zeros((1, LANES), f32)
    for blk in range(seq // tb):
        sl = slice(blk * tb, (blk + 1) * tb)
        c = _dot(tri_ref[...], selb[sl]) + carry
        carry = c[tb - 1:tb, :]
        rank_ref[0, sl, :] = jnp.where(sel[sl], c - 1.0, -1.0)
    cnt_ref[0] = carry.astype(i32)


def _router(x3, router_pad, tri):
    b, s, _ = x3.shape
    return pl.pallas_call(
        functools.partial(_router_kernel, seq=s),
        grid=(b,),
        in_specs=[pl.BlockSpec((1, s, D_MODEL), lambda i: (i, 0, 0)),
                  pl.BlockSpec((D_MODEL, LANES), lambda i: (0, 0)),
                  pl.BlockSpec(tri.shape, lambda i: (0, 0))],
        out_specs=[pl.BlockSpec((1, s, LANES), lambda i: (i, 0, 0)),
                   pl.BlockSpec((1, s, LANES), lambda i: (i, 0, 0)),
                   pl.BlockSpec((1, 1, LANES), lambda i: (i, 0, 0))],
        out_shape=[jax.ShapeDtypeStruct((b, s, LANES), f32),
                   jax.ShapeDtypeStruct((b, s, LANES), f32),
                   jax.ShapeDtypeStruct((b, 1, LANES), i32)],
        compiler_params=_cparams(("parallel",)),
        name="router",
    )(x3, router_pad, tri)


def _moe_kernel(cnt_ref, xb_ref, rankt_ref, gatet_ref, ranke_ref, wg_ref, wu_ref, wd_ref, y_ref,
                xg_ref, acc_ref, *, seq, nf):
    tr = TR_MOE
    half = tr // 2
    c = pl.program_id(0)
    e = pl.program_id(1)
    f = pl.program_id(2)
    cnt = cnt_ref[c * N_EXPERTS + e]
    rem = cnt % tr
    nfull = cnt // tr + jnp.where(rem > half, 1, 0)
    has_tail = (rem > 0) & (rem <= half)
    tail0 = pl.multiple_of(nfull * tr, half)

    def onehot_rows(r0, nr):
        rk = rankt_ref[0, pl.ds(e, 1), :]
        want = (r0 + lax.broadcasted_iota(i32, (nr, seq), 0)).astype(f32)
        return rk == want

    def gather(r0, nr):
        p = jnp.where(onehot_rows(r0, nr), 1.0, 0.0).astype(bf16)
        xg_ref[pl.ds(r0, nr), :] = _dot(p, xb_ref[0]).astype(bf16)
        acc_ref[pl.ds(r0, nr), :] = jnp.zeros((nr, D_MODEL), f32)

    def hidden(r0, nr):
        xs = xg_ref[pl.ds(r0, nr), :]
        gate = _dot(xs, wg_ref[0])
        up = _dot(xs, wu_ref[0])
        act = (gate * _sigmoid(gate) * up).astype(bf16)
        acc_ref[pl.ds(r0, nr), :] += _dot(act, wd_ref[0])

    def combine(r0, nr):
        grow = gatet_ref[0, pl.ds(e, 1), :]
        w = jnp.sum(jnp.where(onehot_rows(r0, nr), grow, 0.0), axis=-1, keepdims=True)
        z = (acc_ref[pl.ds(r0, nr), :] * w).astype(bf16)
        want = (r0 + lax.broadcasted_iota(i32, (512, nr), 1)).astype(f32)
        for jb in range(seq // 512):
            rc = ranke_ref[0, 0, jb * 512:(jb + 1) * 512, :]
            pt = jnp.where(rc == want, 1.0, 0.0).astype(bf16)
            y_ref[0, jb * 512:(jb + 1) * 512, :] += _dot(pt, z)

    def tiles(fn):
        def body(s, carry):
            fn(pl.multiple_of(s * tr, tr), tr)
            return carry
        lax.fori_loop(0, nfull, body, 0)

        @pl.when(has_tail)
        def _():
            fn(tail0, half)

    def zero(j, carry):
        y_ref[0, pl.ds(j * tr, tr), :] = jnp.zeros((tr, D_MODEL), f32)
        return carry

    lax.fori_loop(0, jnp.where((e == 0) & (f == 0), seq // tr, 0), zero, 0)

    @pl.when(f == 0)
    def _():
        tiles(gather)

    tiles(hidden)

    @pl.when(f == nf - 1)
    def _():
        tiles(combine)


def _moe(xb3, gate, rank, cnt, wg, wu, wd):
    b, s, _ = xb3.shape
    dff = wg.shape[2]
    nf = dff // TF_MOE
    rankt = rank[:, :, :N_EXPERTS].transpose(0, 2, 1)
    gatet = gate[:, :, :N_EXPERTS].transpose(0, 2, 1)
    counts = cnt[:, 0, :N_EXPERTS].reshape(-1)
    grid_spec = pltpu.PrefetchScalarGridSpec(
        num_scalar_prefetch=1,
        grid=(b, N_EXPERTS, nf),
        in_specs=[pl.BlockSpec((1, s, D_MODEL), lambda c, e, f, cnt: (c, 0, 0)),
                  pl.BlockSpec((1, N_EXPERTS, s), lambda c, e, f, cnt: (c, 0, 0)),
                  pl.BlockSpec((1, N_EXPERTS, s), lambda c, e, f, cnt: (c, 0, 0)),
                  pl.BlockSpec((1, 1, s, 1), lambda c, e, f, cnt: (c, e, 0, 0)),
                  pl.BlockSpec((1, D_MODEL, TF_MOE), lambda c, e, f, cnt: (e, 0, f)),
                  pl.BlockSpec((1, D_MODEL, TF_MOE), lambda c, e, f, cnt: (e, 0, f)),
                  pl.BlockSpec((1, TF_MOE, D_MODEL), lambda c, e, f, cnt: (e, f, 0))],
        out_specs=pl.BlockSpec((1, s, D_MODEL), lambda c, e, f, cnt: (c, 0, 0)),
        scratch_shapes=[pltpu.VMEM((s, D_MODEL), bf16), pltpu.VMEM((s, D_MODEL), f32)],
    )
    return pl.pallas_call(
        functools.partial(_moe_kernel, seq=s, nf=nf),
        grid_spec=grid_spec,
        out_shape=jax.ShapeDtypeStruct((b, s, D_MODEL), f32),
        compiler_params=_cparams(("parallel", "arbitrary", "arbitrary")),
        name="moe",
    )(counts, xb3, rankt, gatet, rankt[..., None], wg, wu, wd)


def _resln_kernel(x_ref, y_ref, g_ref, b_ref, xo_ref, xbo_ref):
    xn = _layer_norm(DN_ALPHA * x_ref[...] + y_ref[...], g_ref[...], b_ref[...])
    xo_ref[...] = xn
    xbo_ref[...] = xn.astype(bf16)


def _resln(x, y, g, bb):
    m = x.shape[0]
    tm = TM_FFN
    row = pl.BlockSpec((tm, D_MODEL), lambda i: (i, 0))
    vec = pl.BlockSpec((1, D_MODEL), lambda i: (0, 0))
    return pl.pallas_call(
        _resln_kernel,
        grid=(m // tm,),
        in_specs=[row, row, vec, vec],
        out_specs=[row, row],
        out_shape=[jax.ShapeDtypeStruct((m, D_MODEL), f32), jax.ShapeDtypeStruct((m, D_MODEL), bf16)],
        compiler_params=_cparams(("parallel",)),
        name="resln",
    )(x, y, g, bb)


def _t5_bucket(dist):
    n = jnp.maximum(dist, 0)
    max_exact = REL_BUCKETS // 2
    large = max_exact + (jnp.log(jnp.maximum(n, 1).astype(f32) / max_exact)
                         / math.log(REL_MAX_DIST / max_exact) * (REL_BUCKETS - max_exact)).astype(i32)
    large = jnp.minimum(large, REL_BUCKETS - 1)
    return jnp.where(n < max_exact, n, large)


def _bias_of_dist(rel_bias, dist):
    onehot = (_t5_bucket(dist)[None] == jnp.arange(REL_BUCKETS).reshape((-1,) + (1,) * dist.ndim)).astype(f32)
    return LOG2E * jnp.einsum("kh,k...->h...", rel_bias.astype(f32), onehot, precision=lax.Precision.HIGHEST)


def _bias_tables(rel_bias, seq):
    t = T_ATT
    rows = NSA_GROUP * t
    ncmp = seq // CMP_STRIDE
    d0 = jnp.arange(t)[:, None] - jnp.arange(t)[None, :]
    offs = jnp.array([0, t, 2 * t, WINDOW]).reshape(4, 1, 1)
    kinds = _bias_of_dist(rel_bias, offs + d0[None])
    mask = jnp.stack([d0 >= 0, d0 == d0, d0 == d0, d0 < 0])
    kinds = jnp.where(mask[None], kinds, NEG_INF)
    tbias = kinds.reshape(NSA_KV_HEADS, NSA_GROUP, 4, t, t).transpose(0, 2, 1, 3, 4)
    tbias = tbias.reshape(NSA_KV_HEADS, 4, rows, t)
    cend = jnp.arange(ncmp) * CMP_STRIDE + CMP_BLOCK - 1
    bc = _bias_of_dist(rel_bias, jnp.arange(seq)[:, None] - cend[None, :])
    bc = bc.reshape(NSA_KV_HEADS, NSA_GROUP, seq // t, t, ncmp).transpose(0, 2, 4, 1, 3)
    return tbias, bc.reshape(NSA_KV_HEADS, (seq // t) * ncmp, rows)


def _selection_constants(seq):
    ncmp = seq // CMP_STRIDE
    nsel = seq // SEL_BLOCK
    c0 = np.arange(ncmp)[:, None] * CMP_STRIDE
    s0 = np.arange(LANES)[None, :] * SEL_BLOCK
    ov = np.maximum(np.minimum(c0 + CMP_BLOCK, s0 + SEL_BLOCK) - np.maximum(c0, s0), 0) / CMP_BLOCK
    ov[ncmp - 1, :] = 0.0
    ov[:, nsel:] = 0.0
    et = (np.arange(seq)[:, None] // SEL_BLOCK == np.arange(LANES)[None, :]).astype(np.float32)
    return jnp.asarray(ov.T, bf16), jnp.asarray(et, bf16)


def _fox_placement():
    xw = FOX_PAIRS * LANES
    pq = np.zeros((3, LANES, xw), np.float32)
    pk = np.zeros((3, LANES, xw), np.float32)
    oq = np.zeros((1, xw), np.float32)
    ok = np.zeros((1, xw), np.float32)
    for p in range(FOX_PAIRS):
        for hh in range(2):
            src = FGATE_LANE + 2 * p + hh
            base = p * LANES + hh * XCOLS
            for part in range(3):
                pk[part, src, base + part] = -1.0
                pq[part, src, base + 3 + part] = 1.0
                oq[0, base + part] = 1.0
                ok[0, base + 3 + part] = 1.0
    return jnp.asarray(pq, bf16), jnp.asarray(pk, bf16), jnp.asarray(oq), jnp.asarray(ok)


def _layer_weights(w_in, layer_pe, w1, w2, f_bias):
    offs = np.cumsum((NSA_Q_W, 6 * 2 * HEAD_DIM, 3 * NSA_HEADS, 3 * FOX_W, FOX_HEADS, 2 * D_MODEL))
    kv0, g0, fx0, ff0, mg0 = offs[0], offs[1], offs[2], offs[3], offs[4]
    scale = HEAD_DIM ** -0.5 * LOG2E
    kvw = NSA_KV_HEADS * HEAD_DIM
    w_kv = w_in[:, kv0:g0]
    w_kvdup = jnp.repeat(w_kv[:, 2 * kvw:].reshape(D_MODEL, 4 * NSA_KV_HEADS, 1, HEAD_DIM), 2, axis=2)
    w_kvdup = w_kvdup.reshape(D_MODEL, 8 * kvw)
    w_qkv = jnp.concatenate([w_in[:, :NSA_Q_W] * scale, w_kvdup, w_in[:, fx0:fx0 + FOX_W] * scale,
                             w_in[:, fx0 + FOX_W:ff0]], axis=1).astype(bf16)
    zeros = lambda n: jnp.zeros((D_MODEL, n), w_in.dtype)
    ng = 3 * NSA_GROUP
    w_g = w_in[:, g0:fx0].reshape(D_MODEL, NSA_KV_HEADS, NSA_GROUP, 3).transpose(0, 1, 3, 2)
    gate_cols = lambda h: w_g[:, h].reshape(D_MODEL, ng)
    w_aux = jnp.concatenate([w_kv[:, :2 * kvw],
                             gate_cols(0), zeros(FGATE_LANE - ng), w_in[:, ff0:mg0],
                             zeros(LANES - FGATE_LANE - FOX_HEADS),
                             gate_cols(1), zeros(LANES - ng)], axis=1).astype(bf16)
    fb_row = jnp.zeros((1, LANES), f32).at[0, FGATE_LANE:FGATE_LANE + FOX_HEADS].set(f_bias.astype(f32))
    pe2 = jnp.tile(layer_pe.astype(f32), (1, 1, NSA_KV_HEADS))
    w1r = w1.reshape(2, CMP_BLOCK, HEAD_DIM, CMP_HIDDEN).astype(bf16)
    zero = jnp.zeros_like(w1r)
    w1bd = jnp.concatenate([jnp.concatenate([w1r, zero], axis=-1),
                            jnp.concatenate([zero, w1r], axis=-1)], axis=-2)
    w2d = jnp.concatenate([w2, w2], axis=-1).astype(bf16)
    return (w_qkv, w_aux, w_in[:, mg0:].astype(bf16), fb_row, pe2, w1bd[:, :CMP_STRIDE], w1bd[:, CMP_STRIDE:],
            w2d)


def kernel(x, w_in, nsa_cmp_pe, nsa_cmp_w1, nsa_cmp_w2, fox_f_bias, w_nsa_branch, w_fox_branch, w_out,
           rel_bias, ln1_g, ln1_b, ln2_g, ln2_b, dense_w_gate, dense_w_up, dense_w_down, moe_router,
           moe_w_gate, moe_w_up, moe_w_down):
    b, s, d = x.shape
    m = b * s
    tbias, bias_c = _bias_tables(rel_bias, s)
    ov, et = _selection_constants(s)
    pq, pk, oq, ok = _fox_placement()
    tri128 = jnp.asarray(np.tril(np.ones((LANES, LANES), np.float32)), bf16)
    tri256 = jnp.asarray(np.tril(np.ones((256, 256), np.float32)), bf16)

    xf = x.reshape(m, d).astype(f32)
    xb = xf.astype(bf16)
    for layer in range(DEPTH):
        w_qkv, w_aux, w_mg, fb_row, pe2, w1t, w1b, w2d = _layer_weights(
            w_in[layer], nsa_cmp_pe[layer], nsa_cmp_w1[layer], nsa_cmp_w2[layer], fox_f_bias[layer])
        qkv = _proj_tiles(xb, w_qkv, b, bf16)
        aux = _proj(xb, w_aux, f32).reshape(b, s, AUX_W)
        gates, qx, kx = _gateprep(aux, fb_row, tri128, pq, pk, oq, ok)
        cmpkv = _compress(aux, pe2, w1t, w1b, w2d)
        y_nsa = _nsa(qkv, cmpkv, gates, bias_c, tbias, ov, et).reshape(m, NSA_Q_W)
        y_fox = _fox(qkv, qx, kx).reshape(m, FOX_W)
        xf, xb = _merge(y_nsa, y_fox, xb, xf, w_nsa_branch[layer].astype(bf16),
                        w_fox_branch[layer].astype(bf16), w_mg, w_out[layer].astype(bf16),
                        ln1_g[layer].reshape(1, d), ln1_b[layer].reshape(1, d))
        j = layer // 2
        g2, b2 = ln2_g[layer].reshape(1, d), ln2_b[layer].reshape(1, d)
        if layer % 2 == 0:
            xf, xb = _ffn(xb, xf, dense_w_gate[j].astype(bf16), dense_w_up[j].astype(bf16),
                          dense_w_down[j].astype(bf16), g2, b2)
        else:
            router_pad = jnp.zeros((d, LANES), f32).at[:, :N_EXPERTS].set(moe_router[j].astype(f32))
            gate, rank, cnt = _router(xf.reshape(b, s, d), router_pad, tri256)
            y = _moe(xb.reshape(b, s, d), gate, rank, cnt,
                     moe_w_gate[j].astype(bf16), moe_w_up[j].astype(bf16), moe_w_down[j].astype(bf16))
            xf, xb = _resln(xf, y.reshape(m, d), g2, b2)
    return xf.reshape(b, s, d).astype(x.dtype)
```

```python
import functools
import math

import numpy as np
import jax
import jax.numpy as jnp
from jax import lax
from jax.experimental import pallas as pl
from jax.experimental.pallas import tpu as pltpu

f32 = jnp.float32
bf16 = jnp.bfloat16
i32 = jnp.int32

D_MODEL = 1024
HEAD_DIM = 64
LANES = 128
NSA_HEADS = 8
NSA_KV_HEADS = 2
NSA_GROUP = NSA_HEADS // NSA_KV_HEADS
NSA_ROW_ORDER = (0, 2, 1, 3)
FOX_HEADS = 8
FOX_PAIRS = FOX_HEADS // 2
CMP_BLOCK = 32
CMP_STRIDE = 16
CMP_HIDDEN = 128
SEL_BLOCK = 64
N_SELECT = 16
N_LOCAL_SEL = 2
WINDOW = 512
REL_BUCKETS = 32
REL_MAX_DIST = 128
N_EXPERTS = 8
DEPTH = 4
DN_ALPHA = (2 * DEPTH) ** 0.25
LN_EPS = 1e-5
FORCED_SCORE = 1e4
NEG_INF = -1e30
LOG2E = math.log2(math.e)

NSA_Q_W = NSA_HEADS * HEAD_DIM
FOX_W = FOX_HEADS * HEAD_DIM
QKV_TILES_KV = NSA_Q_W // LANES
QKV_TILES_FOX = QKV_TILES_KV + 4 * NSA_KV_HEADS
QKV_W = (QKV_TILES_FOX + 3 * FOX_PAIRS) * LANES
AUX_W = 4 * LANES
FGATE_LANE = 24
XCOLS = 6

T_ATT = 256
TM_PROJ = 512
TM_MERGE = 256
TM_FFN = 512
TF_FFN = 1408
TR_MOE = 256
SUB_MOE = 512
PIECE_MOE = 192
ROW_ALIGN = 16
TF_MOE = 896
VMEM_LIMIT = 56 * 1024 * 1024


def _cparams(sem):
    return pltpu.CompilerParams(dimension_semantics=sem, vmem_limit_bytes=VMEM_LIMIT)


def _dot(a, b):
    return jnp.dot(a, b, preferred_element_type=f32)


def _dot_nt(a, b):
    return lax.dot_general(a, b, (((1,), (1,)), ((), ())), preferred_element_type=f32)


def _sigmoid(x):
    return 1.0 / (1.0 + jnp.exp(-x))


def _layer_norm(z, g, b):
    mu = jnp.mean(z, axis=-1, keepdims=True)
    zc = z - mu
    var = jnp.mean(zc * zc, axis=-1, keepdims=True)
    return zc * lax.rsqrt(var + LN_EPS) * g + b


def _split3(x):
    hi = x.astype(bf16)
    r1 = x - hi.astype(f32)
    mid = r1.astype(bf16)
    lo = (r1 - mid.astype(f32)).astype(bf16)
    return hi, mid, lo


def _proj_kernel(x_ref, w_ref, o_ref):
    o_ref[...] = _dot(x_ref[...], w_ref[...]).astype(o_ref.dtype)


def _proj(xb, w, out_dtype):
    m, k = xb.shape
    n = w.shape[1]
    return pl.pallas_call(
        _proj_kernel,
        grid=(m // TM_PROJ,),
        in_specs=[pl.BlockSpec((TM_PROJ, k), lambda i: (i, 0)),
                  pl.BlockSpec((k, n), lambda i: (0, 0))],
        out_specs=pl.BlockSpec((TM_PROJ, n), lambda i: (i, 0)),
        out_shape=jax.ShapeDtypeStruct((m, n), out_dtype),
        compiler_params=_cparams(("parallel",)),
        name="proj",
    )(xb, w)


def _proj_tiles_kernel(x_ref, w_ref, o_ref):
    res = _dot(x_ref[...], w_ref[...])
    for j in range(o_ref.shape[1]):
        o_ref[0, j] = res[:, j * LANES:(j + 1) * LANES].astype(o_ref.dtype)


def _proj_tiles(xb, w, batch, out_dtype):
    m, k = xb.shape
    n = w.shape[1]
    nb = m // batch // TM_PROJ
    return pl.pallas_call(
        _proj_tiles_kernel,
        grid=(m // TM_PROJ,),
        in_specs=[pl.BlockSpec((TM_PROJ, k), lambda i: (i, 0)),
                  pl.BlockSpec((k, n), lambda i: (0, 0))],
        out_specs=pl.BlockSpec((1, n // LANES, TM_PROJ, LANES), lambda i: (i // nb, 0, i % nb, 0)),
        out_shape=jax.ShapeDtypeStruct((batch, n // LANES, m // batch, LANES), out_dtype),
        compiler_params=_cparams(("parallel",)),
        name="proj_tiles",
    )(xb, w)


def _gateprep_kernel(a_ref, fb_ref, tri_ref, pq_ref, pk_ref, oq_ref, ok_ref, g_ref, qx_ref, kx_ref, *, seq):
    tb = LANES
    tri = tri_ref[...]
    carry = jnp.zeros((1, LANES), f32)
    for blk in range(seq // tb):
        sl = slice(blk * tb, (blk + 1) * tb)
        va = a_ref[0, sl, 0:LANES]
        g_ref[0, 0, sl, :] = _sigmoid(va)
        g_ref[0, 1, sl, :] = _sigmoid(a_ref[0, sl, LANES:2 * LANES])
        z = va + fb_ref[...]
        logf = jnp.minimum(z, 0.0) - jnp.log1p(jnp.exp(-jnp.abs(z)))
        hi, mid, lo = _split3(logf)
        c = _dot(tri, hi) + _dot(tri, mid) + _dot(tri, lo) + carry
        carry = c[tb - 1:tb, :]
        chi, cmid, clo = _split3(c * LOG2E)
        qx = _dot(chi, pq_ref[0]) + _dot(cmid, pq_ref[1]) + _dot(clo, pq_ref[2]) + oq_ref[...]
        kx = _dot(chi, pk_ref[0]) + _dot(cmid, pk_ref[1]) + _dot(clo, pk_ref[2]) + ok_ref[...]
        for p in range(FOX_PAIRS):
            qx_ref[0, p, sl, :] = qx[:, p * LANES:(p + 1) * LANES].astype(bf16)
            kx_ref[0, p, sl, :] = kx[:, p * LANES:(p + 1) * LANES].astype(bf16)


def _gateprep(aux, fb_row, tri, pq, pk, oq, ok):
    b, s, _ = aux.shape
    const2 = lambda a: pl.BlockSpec(a.shape, lambda i: (0, 0))
    const3 = lambda a: pl.BlockSpec(a.shape, lambda i: (0, 0, 0))
    return pl.pallas_call(
        functools.partial(_gateprep_kernel, seq=s),
        grid=(b,),
        in_specs=[pl.BlockSpec((1, s, 2 * LANES), lambda i: (i, 0, 1)),
                  const2(fb_row), const2(tri), const3(pq), const3(pk), const2(oq), const2(ok)],
        out_specs=[pl.BlockSpec((1, 2, s, LANES), lambda i: (i, 0, 0, 0)),
                   pl.BlockSpec((1, FOX_PAIRS, s, LANES), lambda i: (i, 0, 0, 0)),
                   pl.BlockSpec((1, FOX_PAIRS, s, LANES), lambda i: (i, 0, 0, 0))],
        out_shape=[jax.ShapeDtypeStruct((b, 2, s, LANES), f32),
                   jax.ShapeDtypeStruct((b, FOX_PAIRS, s, LANES), bf16),
                   jax.ShapeDtypeStruct((b, FOX_PAIRS, s, LANES), bf16)],
        compiler_params=_cparams(("parallel",)),
        name="gateprep",
    )(aux, fb_row, tri, pq, pk, oq, ok)


def _gelu_tanh(x):
    c = math.sqrt(2.0 / math.pi)
    return x * (0.5 * (1.0 + jnp.tanh(c * (x + 0.044715 * (x * x * x)))))


def _compress_kernel(a_ref, pe_ref, w1t_ref, w1b_ref, w2_ref, o_ref, *, nhalf):
    top = jnp.zeros((nhalf, NSA_KV_HEADS * CMP_HIDDEN), f32)
    bot = jnp.zeros((nhalf, NSA_KV_HEADS * CMP_HIDDEN), f32)
    for l in range(CMP_STRIDE):
        rows = a_ref[0, pl.ds(l, nhalf, stride=CMP_STRIDE), :]
        top = top + _dot((rows + pe_ref[0, l:l + 1, :]).astype(bf16), w1t_ref[0, l])
        bot = bot + _dot((rows + pe_ref[0, CMP_STRIDE + l:CMP_STRIDE + l + 1, :]).astype(bf16), w1b_ref[0, l])
    pre = top + pltpu.roll(bot, nhalf - 1, 0)
    act = _gelu_tanh(pre).astype(bf16)
    for h in range(NSA_KV_HEADS):
        o_ref[0, 0, h] = _dot(act[:, h * CMP_HIDDEN:(h + 1) * CMP_HIDDEN], w2_ref[0]).astype(o_ref.dtype)


def _compress(aux, pe2, w1t, w1b, w2d):
    b, s, _ = aux.shape
    nhalf = s // CMP_STRIDE
    w1_spec = pl.BlockSpec((1,) + w1t.shape[1:], lambda i, j: (j, 0, 0, 0))
    return pl.pallas_call(
        functools.partial(_compress_kernel, nhalf=nhalf),
        grid=(b, 2),
        in_specs=[pl.BlockSpec((1, s, LANES), lambda i, j: (i, 0, j)),
                  pl.BlockSpec((1, CMP_BLOCK, LANES), lambda i, j: (j, 0, 0)),
                  w1_spec, w1_spec,
                  pl.BlockSpec((1, CMP_HIDDEN, LANES), lambda i, j: (j, 0, 0))],
        out_specs=pl.BlockSpec((1, 1, NSA_KV_HEADS, nhalf, LANES), lambda i, j: (i, j, 0, 0, 0)),
        out_shape=jax.ShapeDtypeStruct((b, 2, NSA_KV_HEADS, nhalf, LANES), bf16),
        compiler_params=_cparams(("parallel", "parallel")),
        name="compress",
    )(aux, pe2, w1t, w1b, w2d)


def _flash_init(m_ref, acc_ref):
    m_ref[...] = jnp.full(m_ref.shape, NEG_INF, f32)
    acc_ref[...] = jnp.zeros(acc_ref.shape, f32)


def _ones_values(v):
    lo_half = lax.broadcasted_iota(i32, v.shape, 1) < HEAD_DIM
    vf = v.astype(f32)
    return jnp.where(lo_half, vf, 1.0).astype(bf16), jnp.where(lo_half, 1.0, vf).astype(bf16)


def _flash_step(s, v_lo, v_hi, m_ref, acc_ref):
    nk = s.shape[1] // LANES
    half = s.shape[0] // 2
    cols = [s[:, c * LANES:(c + 1) * LANES] for c in range(nk)]
    mx = cols[0]
    for c in cols[1:]:
        mx = jnp.maximum(mx, c)
    m_old = m_ref[...]
    m_new = jnp.maximum(m_old, jnp.broadcast_to(jnp.max(mx, axis=-1, keepdims=True), m_old.shape))
    alpha = jnp.exp2(m_old - m_new)
    p = jnp.concatenate([jnp.exp2(c - m_new).astype(bf16) for c in cols], axis=1)
    pv = jnp.concatenate([_dot(p[0:half], v_lo), _dot(p[half:], v_hi)], axis=0)
    acc_ref[...] = alpha * acc_ref[...] + pv
    m_ref[...] = m_new


def _flash_loop(first, last, logits_fn, v_fn, s_ref, m_ref, acc_ref, last_fix=lambda s: s):
    sa, sb = s_ref.at[0], s_ref.at[1]
    n = last - first + 1
    pairs = (n - 1) // 2

    def step(buf, kt, fix=lambda s: s):
        _flash_step(fix(buf[...]), *v_fn(kt), m_ref, acc_ref)

    sa[...] = logits_fn(first)

    def body(j, carry):
        kt = first + 2 * j
        sb[...] = logits_fn(kt + 1)
        step(sa, kt)
        sa[...] = logits_fn(kt + 2)
        step(sb, kt + 1)
        return carry

    lax.fori_loop(0, pairs, body, 0)
    two_left = n - 2 * pairs == 2

    @pl.when(two_left)
    def _():
        sb[...] = logits_fn(last)
        step(sa, last - 1)
        step(sb, last, last_fix)

    @pl.when(jnp.logical_not(two_left))
    def _():
        step(sa, last, last_fix)


def _flash_finish(acc_ref):
    acc = acc_ref[...]
    return acc * (1.0 / jnp.maximum(pltpu.roll(acc, HEAD_DIM, 1), 1e-30))


def _nsa_kernel(q_ref, ks_ref, vs_ref, kw_ref, vw_ref, kc_ref, vc_ref, gate_ref, bc_ref, tb_ref, ov_ref,
                et_ref, o_ref, qaug_ref, kaug_ref, m_ref, acc_ref, os_ref, s_ref, vsel_ref, vwin_ref, *,
                seq, ncmp):
    t = T_ATT
    rows = NSA_GROUP * t
    nsel = seq // SEL_BLOCK
    topn = min(N_SELECT, nsel)
    i = pl.program_id(2)
    t0 = i * t
    lane = lax.broadcasted_iota(i32, (t, LANES), 1)
    lo_half = lane < HEAD_DIM

    @pl.when(i == 0)
    def _():
        kaug_ref[:, 0:LANES] = ks_ref[0, 0]
        kaug_ref[:, LANES:2 * LANES] = et_ref[...]
        vsel_ref[0], vsel_ref[1] = _ones_values(vs_ref[0, 0])
        vwin_ref[0], vwin_ref[1] = _ones_values(vw_ref[0, 0])

    for rb, g in enumerate(NSA_ROW_ORDER):
        qg = q_ref[0, g // 2].astype(f32)
        keep = lo_half if g % 2 == 0 else jnp.logical_not(lo_half)
        qaug_ref[rb * t:(rb + 1) * t, 0:LANES] = jnp.where(keep, qg, 0.0).astype(bf16)
    qs = qaug_ref[:, 0:LANES]

    s = _dot_nt(kc_ref[0, 0, 0], qs) + bc_ref[0]
    row_t = t0 + (lax.broadcasted_iota(i32, (ncmp, rows), 1) & (t - 1))
    cmp_end = lax.broadcasted_iota(i32, (ncmp, rows), 0) * CMP_STRIDE + (CMP_BLOCK - 1)
    vis = cmp_end <= row_t
    s = jnp.where(vis, s, NEG_INF)
    s = s - jnp.max(s, axis=0, keepdims=True)
    e = jnp.where(vis, jnp.exp2(s), 0.0)
    p_c = e * (1.0 / jnp.maximum(jnp.sum(e, axis=0, keepdims=True), 1e-30))
    o_c = _dot(p_c.T.astype(bf16), vc_ref[0, 0, 0])

    psum = p_c[:, 0:t]
    for g in range(1, NSA_GROUP):
        psum = psum + p_c[:, g * t:(g + 1) * t]
    hi = psum.astype(bf16)
    lo = (psum - hi.astype(f32)).astype(bf16)
    imp = (_dot(ov_ref[...], hi) + _dot(ov_ref[...], lo))[0:nsel]
    blk = lax.broadcasted_iota(i32, (nsel, t), 0)
    cur = (t0 + lax.broadcasted_iota(i32, (nsel, t), 1)) >> 6
    forced = (blk == 0) | ((blk <= cur) & (blk > cur - N_LOCAL_SEL))
    val = jnp.where(forced, FORCED_SCORE, jnp.where(blk <= cur, imp, -1.0))
    cnt = jnp.zeros((nsel, t), f32)
    for j in range(nsel):
        vj = val[j:j + 1, :]
        beats = (vj > val) | ((vj == val) & (blk > j))
        cnt = cnt + jnp.where(beats, 1.0, 0.0)
    mneg = jnp.where(cnt < topn, 0.0, NEG_INF)
    mneg = jnp.concatenate([mneg, jnp.zeros((LANES - nsel, t), f32)], axis=0).T.astype(bf16)
    for g in range(NSA_GROUP):
        qaug_ref[g * t:(g + 1) * t, LANES:2 * LANES] = mneg

    _flash_init(m_ref, acc_ref)

    def sel_logits(kt):
        k = kaug_ref[pl.ds(kt * t, t), :]
        return _dot_nt(qaug_ref[...], k) + tb_ref[0, jnp.minimum(i - kt, 2)]

    def values(v_ref):
        return lambda kt: (v_ref[0, pl.ds(kt * t, t), :], v_ref[1, pl.ds(kt * t, t), :])

    _flash_loop(0, i, sel_logits, values(vsel_ref), s_ref, m_ref, acc_ref)
    os_ref[...] = _flash_finish(acc_ref)

    _flash_init(m_ref, acc_ref)
    nwin = WINDOW // t

    def win_logits(kt):
        d = i - kt
        kind = jnp.where(d == nwin, 3, d)
        return _dot_nt(qs, kw_ref[0, 0, pl.ds(kt * t, t), :]) + tb_ref[0, kind]

    _flash_loop(jnp.maximum(i - nwin, 0), i, win_logits, values(vwin_ref), s_ref, m_ref, acc_ref)
    o_w = _flash_finish(acc_ref)
    o_s = os_ref[...]

    gates = gate_ref[0, 0]
    outs = []
    for g in range(NSA_GROUP):
        rb = NSA_ROW_ORDER.index(g)
        sl = slice(rb * t, (rb + 1) * t)
        outs.append(gates[:, g:g + 1] * o_c[sl]
                    + gates[:, NSA_GROUP + g:NSA_GROUP + g + 1] * o_s[sl]
                    + gates[:, 2 * NSA_GROUP + g:2 * NSA_GROUP + g + 1] * o_w[sl])
    for j in range(NSA_GROUP // 2):
        o_ref[0, :, j * LANES:(j + 1) * LANES] = jnp.where(lo_half, outs[2 * j], outs[2 * j + 1]).astype(o_ref.dtype)


def _nsa(qkv, cmpkv, gates, bias_c, tbias, ov, et):
    b, _, s, _ = qkv.shape
    ncmp = cmpkv.shape[3]
    t = T_ATT
    rows = NSA_GROUP * t
    qw = NSA_GROUP * HEAD_DIM
    qtiles = qw // LANES
    kv_spec = lambda col: pl.BlockSpec((1, 1, s, LANES), lambda h, bi, i, col=col: (bi, col + h, 0, 0))
    cmp_spec = lambda kv: pl.BlockSpec((1, 1, 1, ncmp, LANES), lambda h, bi, i, kv=kv: (bi, kv, h, 0, 0))
    base = QKV_TILES_KV
    return pl.pallas_call(
        functools.partial(_nsa_kernel, seq=s, ncmp=ncmp),
        grid=(NSA_KV_HEADS, b, s // t),
        in_specs=[pl.BlockSpec((1, qtiles, t, LANES), lambda h, bi, i: (bi, h, i, 0)),
                  kv_spec(base), kv_spec(base + 2), kv_spec(base + 4), kv_spec(base + 6),
                  cmp_spec(0), cmp_spec(1),
                  pl.BlockSpec((1, 1, t, LANES), lambda h, bi, i: (bi, h, i, 0)),
                  pl.BlockSpec((1, ncmp, rows), lambda h, bi, i: (h, i, 0)),
                  pl.BlockSpec((1, 4, rows, t), lambda h, bi, i: (h, 0, 0, 0)),
                  pl.BlockSpec((LANES, ncmp), lambda h, bi, i: (0, 0)),
                  pl.BlockSpec((s, LANES), lambda h, bi, i: (0, 0))],
        out_specs=pl.BlockSpec((1, t, qw), lambda h, bi, i: (bi, i, h)),
        out_shape=jax.ShapeDtypeStruct((b, s, NSA_Q_W), bf16),
        scratch_shapes=[pltpu.VMEM((rows, 2 * LANES), bf16),
                        pltpu.VMEM((s, 2 * LANES), bf16),
                        pltpu.VMEM((rows, LANES), f32),
                        pltpu.VMEM((rows, LANES), f32),
                        pltpu.VMEM((rows, LANES), f32),
                        pltpu.VMEM((2, rows, t), f32),
                        pltpu.VMEM((2, s, LANES), bf16),
                        pltpu.VMEM((2, s, LANES), bf16)],
        compiler_params=_cparams(("parallel", "parallel", "arbitrary")),
        name="nsa",
    )(qkv, qkv, qkv, qkv, qkv, cmpkv, cmpkv, gates, bias_c, tbias, ov, et)


def _fox_kernel(q_ref, k_ref, v_ref, qx_ref, kx_ref, o_ref, qaug_ref, kaug_ref, m_ref, acc_ref, s_ref,
                vaug_ref):
    t = T_ATT
    i = pl.program_id(2)
    lane = lax.broadcasted_iota(i32, (t, LANES), 1)
    lo_half = lane < HEAD_DIM

    @pl.when(i == 0)
    def _():
        kaug_ref[:, 0:LANES] = k_ref[0, 0]
        kaug_ref[:, LANES:2 * LANES] = kx_ref[0, 0]
        vaug_ref[0], vaug_ref[1] = _ones_values(v_ref[0, 0])

    q = q_ref[0, 0].astype(f32)
    qx = qx_ref[0, 0].astype(f32)
    qaug_ref[0:t, 0:LANES] = jnp.where(lo_half, q, 0.0).astype(bf16)
    qaug_ref[t:2 * t, 0:LANES] = jnp.where(lo_half, 0.0, q).astype(bf16)
    qaug_ref[0:t, LANES:2 * LANES] = jnp.where(lane < XCOLS, qx, 0.0).astype(bf16)
    qaug_ref[t:2 * t, LANES:2 * LANES] = jnp.where((lane >= XCOLS) & (lane < 2 * XCOLS), qx, 0.0).astype(bf16)
    _flash_init(m_ref, acc_ref)

    def logits(kt):
        return _dot_nt(qaug_ref[...], kaug_ref[pl.ds(kt * t, t), :])

    def values(kt):
        return vaug_ref[0, pl.ds(kt * t, t), :], vaug_ref[1, pl.ds(kt * t, t), :]

    def causal(s):
        row = lax.broadcasted_iota(i32, (2 * t, t), 0) & (t - 1)
        return jnp.where(lax.broadcasted_iota(i32, (2 * t, t), 1) <= row, s, NEG_INF)

    _flash_loop(0, i, logits, values, s_ref, m_ref, acc_ref, last_fix=causal)
    o = _flash_finish(acc_ref)
    o_ref[0] = jnp.where(lo_half, o[0:t], o[t:2 * t]).astype(o_ref.dtype)


def _fox(qkv, qx, kx):
    b, _, s, _ = qkv.shape
    t = T_ATT
    base = QKV_TILES_FOX
    return pl.pallas_call(
        _fox_kernel,
        grid=(b, FOX_PAIRS, s // t),
        in_specs=[pl.BlockSpec((1, 1, t, LANES), lambda bi, p, i: (bi, base + p, i, 0)),
                  pl.BlockSpec((1, 1, s, LANES), lambda bi, p, i: (bi, base + FOX_PAIRS + p, 0, 0)),
                  pl.BlockSpec((1, 1, s, LANES), lambda bi, p, i: (bi, base + 2 * FOX_PAIRS + p, 0, 0)),
                  pl.BlockSpec((1, 1, t, LANES), lambda bi, p, i: (bi, p, i, 0)),
                  pl.BlockSpec((1, 1, s, LANES), lambda bi, p, i: (bi, p, 0, 0))],
        out_specs=pl.BlockSpec((1, t, LANES), lambda bi, p, i: (bi, i, p)),
        out_shape=jax.ShapeDtypeStruct((b, s, FOX_W), bf16),
        scratch_shapes=[pltpu.VMEM((2 * t, 2 * LANES), bf16),
                        pltpu.VMEM((s, 2 * LANES), bf16),
                        pltpu.VMEM((2 * t, LANES), f32),
                        pltpu.VMEM((2 * t, LANES), f32),
                        pltpu.VMEM((2, 2 * t, t), f32),
                        pltpu.VMEM((2, s, LANES), bf16)],
        compiler_params=_cparams(("parallel", "parallel", "arbitrary")),
        name="fox",
    )(qkv, qkv, qkv, qx, kx)


def _merge_kernel(yn_ref, yf_ref, xb_ref, x_ref, wn_ref, wf_ref, wmg_ref, wo_ref, g_ref, b_ref,
                  xo_ref, xbo_ref):
    mg = _dot(xb_ref[...], wmg_ref[...])
    merged = (_sigmoid(mg[:, 0:D_MODEL]) * _dot(yn_ref[...], wn_ref[...])
              + _sigmoid(mg[:, D_MODEL:2 * D_MODEL]) * _dot(yf_ref[...], wf_ref[...]))
    hmix = _dot(merged.astype(bf16), wo_ref[...])
    xn = _layer_norm(DN_ALPHA * x_ref[...] + hmix, g_ref[...], b_ref[...])
    xo_ref[...] = xn
    xbo_ref[...] = xn.astype(bf16)


def _merge(yn, yf, xb, x, wn, wf, wmg, wo, g, bb):
    m = x.shape[0]
    tm = TM_MERGE
    row = lambda w: pl.BlockSpec((tm, w), lambda i: (i, 0))
    full = lambda a: pl.BlockSpec(a.shape, lambda i: (0, 0))
    return pl.pallas_call(
        _merge_kernel,
        grid=(m // tm,),
        in_specs=[row(NSA_Q_W), row(FOX_W), row(D_MODEL), row(D_MODEL),
                  full(wn), full(wf), full(wmg), full(wo), full(g), full(bb)],
        out_specs=[row(D_MODEL), row(D_MODEL)],
        out_shape=[jax.ShapeDtypeStruct((m, D_MODEL), f32), jax.ShapeDtypeStruct((m, D_MODEL), bf16)],
        compiler_params=_cparams(("parallel",)),
        name="merge",
    )(yn, yf, xb, x, wn, wf, wmg, wo, g, bb)


def _ffn_kernel(xb_ref, x_ref, wg_ref, wu_ref, wd_ref, g_ref, b_ref, xo_ref, xbo_ref, acc_ref, *, nf):
    f = pl.program_id(1)

    @pl.when(f == 0)
    def _():
        acc_ref[...] = jnp.zeros_like(acc_ref)

    xb = xb_ref[...]
    gate = _dot(xb, wg_ref[...])
    up = _dot(xb, wu_ref[...])
    act = (gate * _sigmoid(gate) * up).astype(bf16)
    acc_ref[...] += _dot(act, wd_ref[...])

    @pl.when(f == nf - 1)
    def _():
        xn = _layer_norm(DN_ALPHA * x_ref[...] + acc_ref[...], g_ref[...], b_ref[...])
        xo_ref[...] = xn
        xbo_ref[...] = xn.astype(bf16)


def _ffn(xb, x, wg, wu, wd, g, bb):
    m = x.shape[0]
    dff = wg.shape[1]
    tm, tf = TM_FFN, TF_FFN
    nf = dff // tf
    row = pl.BlockSpec((tm, D_MODEL), lambda i, f: (i, 0))
    vec = pl.BlockSpec((1, D_MODEL), lambda i, f: (0, 0))
    return pl.pallas_call(
        functools.partial(_ffn_kernel, nf=nf),
        grid=(m // tm, nf),
        in_specs=[row, row,
                  pl.BlockSpec((D_MODEL, tf), lambda i, f: (0, f)),
                  pl.BlockSpec((D_MODEL, tf), lambda i, f: (0, f)),
                  pl.BlockSpec((tf, D_MODEL), lambda i, f: (f, 0)),
                  vec, vec],
        out_specs=[row, row],
        out_shape=[jax.ShapeDtypeStruct((m, D_MODEL), f32), jax.ShapeDtypeStruct((m, D_MODEL), bf16)],
        scratch_shapes=[pltpu.VMEM((tm, D_MODEL), f32)],
        compiler_params=_cparams(("parallel", "arbitrary")),
        name="ffn",
    )(xb, x, wg, wu, wd, g, bb)


def _router_kernel(x_ref, r_ref, tri_ref, gate_ref, rank_ref, cnt_ref, *, seq):
    tb = tri_ref.shape[0]
    x = x_ref[0]
    xh = x.astype(bf16)
    xl = (x - xh.astype(f32)).astype(bf16)
    r = r_ref[...]
    rh = r.astype(bf16)
    rl = (r - rh.astype(f32)).astype(bf16)
    logits = _dot(xh, rh) + _dot(xh, rl) + _dot(xl, rh)
    lane = lax.broadcasted_iota(i32, (seq, LANES), 1).astype(f32)
    low = -3.0e38
    lg = jnp.where(lane < N_EXPERTS, logits, low)
    m1 = jnp.max(lg, axis=-1, keepdims=True)
    i1 = jnp.min(jnp.where(lg == m1, lane, float(LANES)), axis=-1, keepdims=True)
    lg2 = jnp.where(lane == i1, low, lg)
    m2 = jnp.max(lg2, axis=-1, keepdims=True)
    i2 = jnp.min(jnp.where(lg2 == m2, lane, float(LANES)), axis=-1, keepdims=True)
    e2 = jnp.exp(m2 - m1)
    den = 1.0 + e2
    gate_ref[0] = jnp.where(lane == i1, 1.0 / den, jnp.where(lane == i2, e2 / den, 0.0))
    sel = (lane == i1) | (lane == i2)
    selb = jnp.where(sel, 1.0, 0.0).astype(bf16)
    carry = jnp.zeros((1, LANES), f32)
    for blk in range(seq // tb):
        sl = slice(blk * tb, (blk + 1) * tb)
        c = _dot(tri_ref[...], selb[sl]) + carry
        carry = c[tb - 1:tb, :]
        rank_ref[0, sl, :] = jnp.where(sel[sl], c - 1.0, -1.0)
    cnt_ref[0] = carry.astype(i32)


def _router(x3, router_pad, tri):
    b, s, _ = x3.shape
    return pl.pallas_call(
        functools.partial(_router_kernel, seq=s),
        grid=(b,),
        in_specs=[pl.BlockSpec((1, s, D_MODEL), lambda i: (i, 0, 0)),
                  pl.BlockSpec((D_MODEL, LANES), lambda i: (0, 0)),
                  pl.BlockSpec(tri.shape, lambda i: (0, 0))],
        out_specs=[pl.BlockSpec((1, s, LANES), lambda i: (i, 0, 0)),
                   pl.BlockSpec((1, s, LANES), lambda i: (i, 0, 0)),
                   pl.BlockSpec((1, 1, LANES), lambda i: (i, 0, 0))],
        out_shape=[jax.ShapeDtypeStruct((b, s, LANES), f32),
                   jax.ShapeDtypeStruct((b, s, LANES), f32),
                   jax.ShapeDtypeStruct((b, 1, LANES), i32)],
        compiler_params=_cparams(("parallel",)),
        name="router",
    )(x3, router_pad, tri)


def _moe_kernel(tot_ref, off_ref, npc_ref, xb_ref, rankt_ref, gatet_ref, ranke_ref, wg_ref, wu_ref, wd_ref,
                y_ref, xg_ref, acc_ref, *, seq, nf):
    tr = TR_MOE
    half = tr // 2
    nsubc = seq // SUB_MOE
    c = pl.program_id(0)
    e = pl.program_id(1)
    f = pl.program_id(2)
    ce = c * N_EXPERTS + e
    tot = pl.multiple_of(tot_ref[ce], ROW_ALIGN)
    rem = tot % tr
    nfull = tot // tr + jnp.where(rem > half, 1, 0)
    has_tail = (rem > 0) & (rem <= half)
    tail0 = pl.multiple_of(nfull * tr, half)

    def pieces(fn):
        for q in range(nsubc):
            base = off_ref[ce * nsubc + q]

            def body(p, carry, q=q, base=base):
                fn(q, pl.multiple_of(base + p * PIECE_MOE, ROW_ALIGN))
                return carry
            lax.fori_loop(0, npc_ref[ce * nsubc + q], body, 0)

    def onehot_rows(q, r0):
        rk = rankt_ref[0, pl.ds(e, 1), q * SUB_MOE:(q + 1) * SUB_MOE]
        want = (r0 + lax.broadcasted_iota(i32, (PIECE_MOE, SUB_MOE), 0)).astype(f32)
        return rk == want

    def gather(q, r0):
        p = jnp.where(onehot_rows(q, r0), 1.0, 0.0).astype(bf16)
        xg_ref[pl.ds(r0, PIECE_MOE), :] = _dot(p, xb_ref[0, q * SUB_MOE:(q + 1) * SUB_MOE, :]).astype(bf16)

    def clear(r0, nr):
        acc_ref[pl.ds(r0, nr), :] = jnp.zeros((nr, D_MODEL), f32)

    def hidden(r0, nr):
        xs = xg_ref[pl.ds(r0, nr), :]
        gate = _dot(xs, wg_ref[0])
        up = _dot(xs, wu_ref[0])
        act = (gate * _sigmoid(gate) * up).astype(bf16)
        acc_ref[pl.ds(r0, nr), :] += _dot(act, wd_ref[0])

    def combine(q, r0):
        grow = gatet_ref[0, pl.ds(e, 1), q * SUB_MOE:(q + 1) * SUB_MOE]
        w = jnp.sum(jnp.where(onehot_rows(q, r0), grow, 0.0), axis=-1, keepdims=True)
        z = (acc_ref[pl.ds(r0, PIECE_MOE), :] * w).astype(bf16)
        want = (r0 + lax.broadcasted_iota(i32, (SUB_MOE, PIECE_MOE), 1)).astype(f32)
        rc = ranke_ref[0, 0, q * SUB_MOE:(q + 1) * SUB_MOE, :]
        pt = jnp.where(rc == want, 1.0, 0.0).astype(bf16)
        y_ref[0, q * SUB_MOE:(q + 1) * SUB_MOE, :] += _dot(pt, z)

    def tiles(fn):
        def body(s, carry):
            fn(pl.multiple_of(s * tr, tr), tr)
            return carry
        lax.fori_loop(0, nfull, body, 0)

        @pl.when(has_tail)
        def _():
            fn(tail0, half)

    def zero(j, carry):
        y_ref[0, pl.ds(j * tr, tr), :] = jnp.zeros((tr, D_MODEL), f32)
        return carry

    lax.fori_loop(0, jnp.where((e == 0) & (f == 0), seq // tr, 0), zero, 0)

    @pl.when(f == 0)
    def _():
        pieces(gather)
        xg_ref[pl.ds(tot, tr), :] = jnp.zeros((tr, D_MODEL), bf16)
        tiles(clear)
        clear(tot, tr)

    tiles(hidden)

    @pl.when(f == nf - 1)
    def _():
        pieces(combine)


def _moe(xb3, gate, rank, wg, wu, wd):
    b, s, _ = xb3.shape
    dff = wg.shape[2]
    nf = dff // TF_MOE
    nsubc = s // SUB_MOE
    rk = rank[:, :, :N_EXPERTS]
    sel = rk >= 0.0
    cnt = jnp.sum(sel.reshape(b, nsubc, SUB_MOE, N_EXPERTS), axis=2).astype(i32)
    cnt_al = (cnt + ROW_ALIGN - 1) // ROW_ALIGN * ROW_ALIGN
    off = jnp.cumsum(cnt_al, axis=1) - cnt_al
    first = jnp.cumsum(cnt, axis=1) - cnt
    shift = jnp.repeat((off - first).astype(f32), SUB_MOE, axis=1)
    post = jnp.where(sel, rk + shift, -1.0).transpose(0, 2, 1)
    gatet = gate[:, :, :N_EXPERTS].transpose(0, 2, 1)
    tot = (off[:, -1] + cnt_al[:, -1]).reshape(-1)
    off_flat = off.transpose(0, 2, 1).reshape(-1)
    npc = ((cnt + PIECE_MOE - 1) // PIECE_MOE).transpose(0, 2, 1).reshape(-1)
    buf_rows = s + nsubc * ROW_ALIGN + PIECE_MOE + TR_MOE
    grid_spec = pltpu.PrefetchScalarGridSpec(
        num_scalar_prefetch=3,
        grid=(b, N_EXPERTS, nf),
        in_specs=[pl.BlockSpec((1, s, D_MODEL), lambda c, e, f, *_: (c, 0, 0)),
                  pl.BlockSpec((1, N_EXPERTS, s), lambda c, e, f, *_: (c, 0, 0)),
                  pl.BlockSpec((1, N_EXPERTS, s), lambda c, e, f, *_: (c, 0, 0)),
                  pl.BlockSpec((1, 1, s, 1), lambda c, e, f, *_: (c, e, 0, 0)),
                  pl.BlockSpec((1, D_MODEL, TF_MOE), lambda c, e, f, *_: (e, 0, f)),
                  pl.BlockSpec((1, D_MODEL, TF_MOE), lambda c, e, f, *_: (e, 0, f)),
                  pl.BlockSpec((1, TF_MOE, D_MODEL), lambda c, e, f, *_: (e, f, 0))],
        out_specs=pl.BlockSpec((1, s, D_MODEL), lambda c, e, f, *_: (c, 0, 0)),
        scratch_shapes=[pltpu.VMEM((buf_rows, D_MODEL), bf16), pltpu.VMEM((buf_rows, D_MODEL), f32)],
    )
    return pl.pallas_call(
        functools.partial(_moe_kernel, seq=s, nf=nf),
        grid_spec=grid_spec,
        out_shape=jax.ShapeDtypeStruct((b, s, D_MODEL), f32),
        compiler_params=_cparams(("parallel", "arbitrary", "arbitrary")),
        name="moe",
    )(tot, off_flat, npc, xb3, post, gatet, post[..., None], wg, wu, wd)


def _resln_kernel(x_ref, y_ref, g_ref, b_ref, xo_ref, xbo_ref):
    xn = _layer_norm(DN_ALPHA * x_ref[...] + y_ref[...], g_ref[...], b_ref[...])
    xo_ref[...] = xn
    xbo_ref[...] = xn.astype(bf16)


def _resln(x, y, g, bb):
    m = x.shape[0]
    tm = TM_FFN
    row = pl.BlockSpec((tm, D_MODEL), lambda i: (i, 0))
    vec = pl.BlockSpec((1, D_MODEL), lambda i: (0, 0))
    return pl.pallas_call(
        _resln_kernel,
        grid=(m // tm,),
        in_specs=[row, row, vec, vec],
        out_specs=[row, row],
        out_shape=[jax.ShapeDtypeStruct((m, D_MODEL), f32), jax.ShapeDtypeStruct((m, D_MODEL), bf16)],
        compiler_params=_cparams(("parallel",)),
        name="resln",
    )(x, y, g, bb)


def _t5_bucket(dist):
    n = jnp.maximum(dist, 0)
    max_exact = REL_BUCKETS // 2
    large = max_exact + (jnp.log(jnp.maximum(n, 1).astype(f32) / max_exact)
                         / math.log(REL_MAX_DIST / max_exact) * (REL_BUCKETS - max_exact)).astype(i32)
    large = jnp.minimum(large, REL_BUCKETS - 1)
    return jnp.where(n < max_exact, n, large)


def _bias_of_dist(rel_bias, dist):
    onehot = (_t5_bucket(dist)[None] == jnp.arange(REL_BUCKETS).reshape((-1,) + (1,) * dist.ndim)).astype(f32)
    return LOG2E * jnp.einsum("kh,k...->h...", rel_bias.astype(f32), onehot, precision=lax.Precision.HIGHEST)


def _bias_tables(rel_bias, seq):
    t = T_ATT
    rows = NSA_GROUP * t
    ncmp = seq // CMP_STRIDE
    d0 = jnp.arange(t)[:, None] - jnp.arange(t)[None, :]
    offs = jnp.array([0, t, 2 * t, WINDOW]).reshape(4, 1, 1)
    kinds = _bias_of_dist(rel_bias, offs + d0[None])
    mask = jnp.stack([d0 >= 0, d0 == d0, d0 == d0, d0 < 0])
    kinds = jnp.where(mask[None], kinds, NEG_INF)
    order = np.array(NSA_ROW_ORDER)
    tbias = kinds.reshape(NSA_KV_HEADS, NSA_GROUP, 4, t, t)[:, order].transpose(0, 2, 1, 3, 4)
    tbias = tbias.reshape(NSA_KV_HEADS, 4, rows, t)
    cend = jnp.arange(ncmp) * CMP_STRIDE + CMP_BLOCK - 1
    bc = _bias_of_dist(rel_bias, jnp.arange(seq)[:, None] - cend[None, :])
    bc = bc.reshape(NSA_KV_HEADS, NSA_GROUP, seq // t, t, ncmp)[:, order].transpose(0, 2, 4, 1, 3)
    return tbias, bc.reshape(NSA_KV_HEADS, (seq // t) * ncmp, rows)


def _selection_constants(seq):
    ncmp = seq // CMP_STRIDE
    nsel = seq // SEL_BLOCK
    c0 = np.arange(ncmp)[:, None] * CMP_STRIDE
    s0 = np.arange(LANES)[None, :] * SEL_BLOCK
    ov = np.maximum(np.minimum(c0 + CMP_BLOCK, s0 + SEL_BLOCK) - np.maximum(c0, s0), 0) / CMP_BLOCK
    ov[ncmp - 1, :] = 0.0
    ov[:, nsel:] = 0.0
    et = (np.arange(seq)[:, None] // SEL_BLOCK == np.arange(LANES)[None, :]).astype(np.float32)
    return jnp.asarray(ov.T, bf16), jnp.asarray(et, bf16)


def _fox_placement():
    xw = FOX_PAIRS * LANES
    pq = np.zeros((3, LANES, xw), np.float32)
    pk = np.zeros((3, LANES, xw), np.float32)
    oq = np.zeros((1, xw), np.float32)
    ok = np.zeros((1, xw), np.float32)
    for p in range(FOX_PAIRS):
        for hh in range(2):
            src = FGATE_LANE + 2 * p + hh
            base = p * LANES + hh * XCOLS
            for part in range(3):
                pk[part, src, base + part] = -1.0
                pq[part, src, base + 3 + part] = 1.0
                oq[0, base + part] = 1.0
                ok[0, base + 3 + part] = 1.0
    return jnp.asarray(pq, bf16), jnp.asarray(pk, bf16), jnp.asarray(oq), jnp.asarray(ok)


def _layer_weights(w_in, layer_pe, w1, w2, f_bias):
    offs = np.cumsum((NSA_Q_W, 6 * 2 * HEAD_DIM, 3 * NSA_HEADS, 3 * FOX_W, FOX_HEADS, 2 * D_MODEL))
    kv0, g0, fx0, ff0, mg0 = offs[0], offs[1], offs[2], offs[3], offs[4]
    scale = HEAD_DIM ** -0.5 * LOG2E
    kvw = NSA_KV_HEADS * HEAD_DIM
    w_kv = w_in[:, kv0:g0]
    w_kvdup = jnp.repeat(w_kv[:, 2 * kvw:].reshape(D_MODEL, 4 * NSA_KV_HEADS, 1, HEAD_DIM), 2, axis=2)
    w_kvdup = w_kvdup.reshape(D_MODEL, 8 * kvw)
    w_qkv = jnp.concatenate([w_in[:, :NSA_Q_W] * scale, w_kvdup, w_in[:, fx0:fx0 + FOX_W] * scale,
                             w_in[:, fx0 + FOX_W:ff0]], axis=1).astype(bf16)
    zeros = lambda n: jnp.zeros((D_MODEL, n), w_in.dtype)
    ng = 3 * NSA_GROUP
    w_g = w_in[:, g0:fx0].reshape(D_MODEL, NSA_KV_HEADS, NSA_GROUP, 3).transpose(0, 1, 3, 2)
    gate_cols = lambda h: w_g[:, h].reshape(D_MODEL, ng)
    w_aux = jnp.concatenate([w_kv[:, :2 * kvw],
                             gate_cols(0), zeros(FGATE_LANE - ng), w_in[:, ff0:mg0],
                             zeros(LANES - FGATE_LANE - FOX_HEADS),
                             gate_cols(1), zeros(LANES - ng)], axis=1).astype(bf16)
    fb_row = jnp.zeros((1, LANES), f32).at[0, FGATE_LANE:FGATE_LANE + FOX_HEADS].set(f_bias.astype(f32))
    pe2 = jnp.tile(layer_pe.astype(f32), (1, 1, NSA_KV_HEADS))
    w1r = w1.reshape(2, CMP_BLOCK, HEAD_DIM, CMP_HIDDEN).astype(bf16)
    zero = jnp.zeros_like(w1r)
    w1bd = jnp.concatenate([jnp.concatenate([w1r, zero], axis=-1),
                            jnp.concatenate([zero, w1r], axis=-1)], axis=-2)
    w2d = jnp.concatenate([w2, w2], axis=-1).astype(bf16)
    return (w_qkv, w_aux, w_in[:, mg0:].astype(bf16), fb_row, pe2, w1bd[:, :CMP_STRIDE], w1bd[:, CMP_STRIDE:],
            w2d)


def kernel(x, w_in, nsa_cmp_pe, nsa_cmp_w1, nsa_cmp_w2, fox_f_bias, w_nsa_branch, w_fox_branch, w_out,
           rel_bias, ln1_g, ln1_b, ln2_g, ln2_b, dense_w_gate, dense_w_up, dense_w_down, moe_router,
           moe_w_gate, moe_w_up, moe_w_down):
    b, s, d = x.shape
    m = b * s
    tbias, bias_c = _bias_tables(rel_bias, s)
    ov, et = _selection_constants(s)
    pq, pk, oq, ok = _fox_placement()
    tri128 = jnp.asarray(np.tril(np.ones((LANES, LANES), np.float32)), bf16)
    tri256 = jnp.asarray(np.tril(np.ones((256, 256), np.float32)), bf16)

    xf = x.reshape(m, d).astype(f32)
    xb = xf.astype(bf16)
    for layer in range(DEPTH):
        w_qkv, w_aux, w_mg, fb_row, pe2, w1t, w1b, w2d = _layer_weights(
            w_in[layer], nsa_cmp_pe[layer], nsa_cmp_w1[layer], nsa_cmp_w2[layer], fox_f_bias[layer])
        qkv = _proj_tiles(xb, w_qkv, b, bf16)
        aux = _proj(xb, w_aux, f32).reshape(b, s, AUX_W)
        gates, qx, kx = _gateprep(aux, fb_row, tri128, pq, pk, oq, ok)
        cmpkv = _compress(aux, pe2, w1t, w1b, w2d)
        y_nsa = _nsa(qkv, cmpkv, gates, bias_c, tbias, ov, et).reshape(m, NSA_Q_W)
        y_fox = _fox(qkv, qx, kx).reshape(m, FOX_W)
        xf, xb = _merge(y_nsa, y_fox, xb, xf, w_nsa_branch[layer].astype(bf16),
                        w_fox_branch[layer].astype(bf16), w_mg, w_out[layer].astype(bf16),
                        ln1_g[layer].reshape(1, d), ln1_b[layer].reshape(1, d))
        j = layer // 2
        g2, b2 = ln2_g[layer].reshape(1, d), ln2_b[layer].reshape(1, d)
        if layer % 2 == 0:
            xf, xb = _ffn(xb, xf, dense_w_gate[j].astype(bf16), dense_w_up[j].astype(bf16),
                          dense_w_down[j].astype(bf16), g2, b2)
        else:
            router_pad = jnp.zeros((d, LANES), f32).at[:, :N_EXPERTS].set(moe_router[j].astype(f32))
            gate, rank, cnt = _router(xf.reshape(b, s, d), router_pad, tri256)
            y = _moe(xb.reshape(b, s, d), gate, rank,
                     moe_w_gate[j].astype(bf16), moe_w_up[j].astype(bf16), moe_w_down[j].astype(bf16))
            xf, xb = _resln(xf, y.reshape(m, d), g2, b2)
    return xf.reshape(b, s, d).astype(x.dtype)
```

```python
import functools
import math

import numpy as np
import jax
import jax.numpy as jnp
from jax import lax
from jax.experimental import pallas as pl
from jax.experimental.pallas import tpu as pltpu

f32 = jnp.float32
bf16 = jnp.bfloat16
i32 = jnp.int32

D_MODEL = 1024
HEAD_DIM = 64
LANES = 128
NSA_HEADS = 8
NSA_KV_HEADS = 2
NSA_GROUP = NSA_HEADS // NSA_KV_HEADS
NSA_ROW_ORDER = (0, 2, 1, 3)
FOX_HEADS = 8
FOX_PAIRS = FOX_HEADS // 2
CMP_BLOCK = 32
CMP_STRIDE = 16
CMP_HIDDEN = 128
SEL_BLOCK = 64
N_SELECT = 16
N_LOCAL_SEL = 2
WINDOW = 512
REL_BUCKETS = 32
REL_MAX_DIST = 128
N_EXPERTS = 8
DEPTH = 4
DN_ALPHA = (2 * DEPTH) ** 0.25
LN_EPS = 1e-5
FORCED_SCORE = 1e4
NEG_INF = -1e30
LOG2E = math.log2(math.e)

NSA_Q_W = NSA_HEADS * HEAD_DIM
FOX_W = FOX_HEADS * HEAD_DIM
QKV_TILES_KV = NSA_Q_W // LANES
QKV_TILES_FOX = QKV_TILES_KV + 4 * NSA_KV_HEADS
QKV_W = (QKV_TILES_FOX + 3 * FOX_PAIRS) * LANES
AUX_W = 4 * LANES
FGATE_LANE = 24
XCOLS = 6

T_ATT = 256
TM_PROJ = 512
TM_MERGE = 256
TM_FFN = 512
TF_FFN = 1408
TR_MOE = 256
SUB_MOE = 512
PIECE_MOE = 192
ROW_ALIGN = 16
TF_MOE = 896
VMEM_LIMIT = 56 * 1024 * 1024


def _cparams(sem):
    return pltpu.CompilerParams(dimension_semantics=sem, vmem_limit_bytes=VMEM_LIMIT)


def _dot(a, b):
    return jnp.dot(a, b, preferred_element_type=f32)


def _dot_nt(a, b):
    return lax.dot_general(a, b, (((1,), (1,)), ((), ())), preferred_element_type=f32)


def _sigmoid(x):
    return 1.0 / (1.0 + jnp.exp(-x))


def _layer_norm(z, g, b):
    mu = jnp.mean(z, axis=-1, keepdims=True)
    zc = z - mu
    var = jnp.mean(zc * zc, axis=-1, keepdims=True)
    return zc * lax.rsqrt(var + LN_EPS) * g + b


def _split3(x):
    hi = x.astype(bf16)
    r1 = x - hi.astype(f32)
    mid = r1.astype(bf16)
    lo = (r1 - mid.astype(f32)).astype(bf16)
    return hi, mid, lo


def _proj_kernel(x_ref, w_ref, wa_ref, o_ref, oa_ref):
    x = x_ref[...]
    res = _dot(x, w_ref[...])
    for j in range(o_ref.shape[1]):
        o_ref[0, j] = res[:, j * LANES:(j + 1) * LANES].astype(o_ref.dtype)
    oa_ref[...] = _dot(x, wa_ref[...])


def _proj(xb, w, wa, batch):
    m, k = xb.shape
    n, na = w.shape[1], wa.shape[1]
    nb = m // batch // TM_PROJ
    return pl.pallas_call(
        _proj_kernel,
        grid=(m // TM_PROJ,),
        in_specs=[pl.BlockSpec((TM_PROJ, k), lambda i: (i, 0)),
                  pl.BlockSpec((k, n), lambda i: (0, 0)),
                  pl.BlockSpec((k, na), lambda i: (0, 0))],
        out_specs=[pl.BlockSpec((1, n // LANES, TM_PROJ, LANES), lambda i: (i // nb, 0, i % nb, 0)),
                   pl.BlockSpec((TM_PROJ, na), lambda i: (i, 0))],
        out_shape=[jax.ShapeDtypeStruct((batch, n // LANES, m // batch, LANES), bf16),
                   jax.ShapeDtypeStruct((m, na), f32)],
        compiler_params=_cparams(("parallel",)),
        name="proj",
    )(xb, w, wa)


def _gateprep_kernel(a_ref, fb_ref, tri_ref, pq_ref, pk_ref, oq_ref, ok_ref, g_ref, qx_ref, kx_ref, *, seq):
    tb = LANES
    tri = tri_ref[...]
    carry = jnp.zeros((1, LANES), f32)
    for blk in range(seq // tb):
        sl = slice(blk * tb, (blk + 1) * tb)
        va = a_ref[0, sl, 0:LANES]
        g_ref[0, 0, sl, :] = _sigmoid(va)
        g_ref[0, 1, sl, :] = _sigmoid(a_ref[0, sl, LANES:2 * LANES])
        z = va + fb_ref[...]
        logf = jnp.minimum(z, 0.0) - jnp.log1p(jnp.exp(-jnp.abs(z)))
        hi, mid, lo = _split3(logf)
        c = _dot(tri, hi) + _dot(tri, mid) + _dot(tri, lo) + carry
        carry = c[tb - 1:tb, :]
        chi, cmid, clo = _split3(c * LOG2E)
        qx = _dot(chi, pq_ref[0]) + _dot(cmid, pq_ref[1]) + _dot(clo, pq_ref[2]) + oq_ref[...]
        kx = _dot(chi, pk_ref[0]) + _dot(cmid, pk_ref[1]) + _dot(clo, pk_ref[2]) + ok_ref[...]
        for p in range(FOX_PAIRS):
            qx_ref[0, p, sl, :] = qx[:, p * LANES:(p + 1) * LANES].astype(bf16)
            kx_ref[0, p, sl, :] = kx[:, p * LANES:(p + 1) * LANES].astype(bf16)


def _gateprep(aux, fb_row, tri, pq, pk, oq, ok):
    b, s, _ = aux.shape
    const2 = lambda a: pl.BlockSpec(a.shape, lambda i: (0, 0))
    const3 = lambda a: pl.BlockSpec(a.shape, lambda i: (0, 0, 0))
    return pl.pallas_call(
        functools.partial(_gateprep_kernel, seq=s),
        grid=(b,),
        in_specs=[pl.BlockSpec((1, s, 2 * LANES), lambda i: (i, 0, 1)),
                  const2(fb_row), const2(tri), const3(pq), const3(pk), const2(oq), const2(ok)],
        out_specs=[pl.BlockSpec((1, 2, s, LANES), lambda i: (i, 0, 0, 0)),
                   pl.BlockSpec((1, FOX_PAIRS, s, LANES), lambda i: (i, 0, 0, 0)),
                   pl.BlockSpec((1, FOX_PAIRS, s, LANES), lambda i: (i, 0, 0, 0))],
        out_shape=[jax.ShapeDtypeStruct((b, 2, s, LANES), f32),
                   jax.ShapeDtypeStruct((b, FOX_PAIRS, s, LANES), bf16),
                   jax.ShapeDtypeStruct((b, FOX_PAIRS, s, LANES), bf16)],
        compiler_params=_cparams(("parallel",)),
        name="gateprep",
    )(aux, fb_row, tri, pq, pk, oq, ok)


def _gelu_tanh(x):
    c = math.sqrt(2.0 / math.pi)
    return x * (0.5 * (1.0 + jnp.tanh(c * (x + 0.044715 * (x * x * x)))))


def _compress_kernel(a_ref, pe_ref, w1t_ref, w1b_ref, w2_ref, o_ref, *, nhalf):
    top = jnp.zeros((nhalf, NSA_KV_HEADS * CMP_HIDDEN), f32)
    bot = jnp.zeros((nhalf, NSA_KV_HEADS * CMP_HIDDEN), f32)
    for l in range(CMP_STRIDE):
        rows = a_ref[0, pl.ds(l, nhalf, stride=CMP_STRIDE), :]
        top = top + _dot((rows + pe_ref[0, l:l + 1, :]).astype(bf16), w1t_ref[0, l])
        bot = bot + _dot((rows + pe_ref[0, CMP_STRIDE + l:CMP_STRIDE + l + 1, :]).astype(bf16), w1b_ref[0, l])
    pre = top + pltpu.roll(bot, nhalf - 1, 0)
    act = _gelu_tanh(pre).astype(bf16)
    for h in range(NSA_KV_HEADS):
        o_ref[0, 0, h] = _dot(act[:, h * CMP_HIDDEN:(h + 1) * CMP_HIDDEN], w2_ref[0]).astype(o_ref.dtype)


def _compress(aux, pe2, w1t, w1b, w2d):
    b, s, _ = aux.shape
    nhalf = s // CMP_STRIDE
    w1_spec = pl.BlockSpec((1,) + w1t.shape[1:], lambda i, j: (j, 0, 0, 0))
    return pl.pallas_call(
        functools.partial(_compress_kernel, nhalf=nhalf),
        grid=(b, 2),
        in_specs=[pl.BlockSpec((1, s, LANES), lambda i, j: (i, 0, j)),
                  pl.BlockSpec((1, CMP_BLOCK, LANES), lambda i, j: (j, 0, 0)),
                  w1_spec, w1_spec,
                  pl.BlockSpec((1, CMP_HIDDEN, LANES), lambda i, j: (j, 0, 0))],
        out_specs=pl.BlockSpec((1, 1, NSA_KV_HEADS, nhalf, LANES), lambda i, j: (i, j, 0, 0, 0)),
        out_shape=jax.ShapeDtypeStruct((b, 2, NSA_KV_HEADS, nhalf, LANES), bf16),
        compiler_params=_cparams(("parallel", "parallel")),
        name="compress",
    )(aux, pe2, w1t, w1b, w2d)


def _flash_init(m_ref, acc_ref):
    m_ref[...] = jnp.full(m_ref.shape, NEG_INF, f32)
    acc_ref[...] = jnp.zeros(acc_ref.shape, f32)


def _ones_values(v):
    lo_half = lax.broadcasted_iota(i32, v.shape, 1) < HEAD_DIM
    vf = v.astype(f32)
    return jnp.where(lo_half, vf, 1.0).astype(bf16), jnp.where(lo_half, 1.0, vf).astype(bf16)


def _flash_step(s, v_lo, v_hi, m_ref, acc_ref):
    nk = s.shape[1] // LANES
    half = s.shape[0] // 2
    cols = [s[:, c * LANES:(c + 1) * LANES] for c in range(nk)]
    mx = cols[0]
    for c in cols[1:]:
        mx = jnp.maximum(mx, c)
    m_old = m_ref[...]
    m_new = jnp.maximum(m_old, jnp.broadcast_to(jnp.max(mx, axis=-1, keepdims=True), m_old.shape))
    alpha = jnp.exp2(m_old - m_new)
    p = jnp.concatenate([jnp.exp2(c - m_new).astype(bf16) for c in cols], axis=1)
    pv = jnp.concatenate([_dot(p[0:half], v_lo), _dot(p[half:], v_hi)], axis=0)
    acc_ref[...] = alpha * acc_ref[...] + pv
    m_ref[...] = m_new


def _flash_loop(first, last, logits_fn, v_fn, s_ref, m_ref, acc_ref, last_fix=lambda s: s):
    sa, sb = s_ref.at[0], s_ref.at[1]
    n = last - first + 1
    pairs = (n - 1) // 2

    def step(buf, kt, fix=lambda s: s):
        _flash_step(fix(buf[...]), *v_fn(kt), m_ref, acc_ref)

    sa[...] = logits_fn(first)

    def body(j, carry):
        kt = first + 2 * j
        sb[...] = logits_fn(kt + 1)
        step(sa, kt)
        sa[...] = logits_fn(kt + 2)
        step(sb, kt + 1)
        return carry

    lax.fori_loop(0, pairs, body, 0)
    two_left = n - 2 * pairs == 2

    @pl.when(two_left)
    def _():
        sb[...] = logits_fn(last)
        step(sa, last - 1)
        step(sb, last, last_fix)

    @pl.when(jnp.logical_not(two_left))
    def _():
        step(sa, last, last_fix)


def _flash_finish(acc_ref):
    acc = acc_ref[...]
    return acc * (1.0 / jnp.maximum(pltpu.roll(acc, HEAD_DIM, 1), 1e-30))


def _nsa_kernel(q_ref, ks_ref, vs_ref, kw_ref, vw_ref, kc_ref, vc_ref, gate_ref, bc_ref, tb_ref, ov_ref,
                et_ref, o_ref, qaug_ref, kaug_ref, m_ref, acc_ref, os_ref, s_ref, vsel_ref, vwin_ref, *,
                seq, ncmp):
    t = T_ATT
    rows = NSA_GROUP * t
    nsel = seq // SEL_BLOCK
    topn = min(N_SELECT, nsel)
    i = pl.program_id(2)
    t0 = i * t
    lane = lax.broadcasted_iota(i32, (t, LANES), 1)
    lo_half = lane < HEAD_DIM

    @pl.when(i == 0)
    def _():
        kaug_ref[:, 0:LANES] = ks_ref[0, 0]
        kaug_ref[:, LANES:2 * LANES] = et_ref[...]
        vsel_ref[0], vsel_ref[1] = _ones_values(vs_ref[0, 0])
        vwin_ref[0], vwin_ref[1] = _ones_values(vw_ref[0, 0])

    for rb, g in enumerate(NSA_ROW_ORDER):
        qg = q_ref[0, g // 2].astype(f32)
        keep = lo_half if g % 2 == 0 else jnp.logical_not(lo_half)
        qaug_ref[rb * t:(rb + 1) * t, 0:LANES] = jnp.where(keep, qg, 0.0).astype(bf16)
    qs = qaug_ref[:, 0:LANES]

    s = _dot_nt(kc_ref[0, 0, 0], qs) + bc_ref[0]
    row_t = t0 + (lax.broadcasted_iota(i32, (ncmp, rows), 1) & (t - 1))
    cmp_end = lax.broadcasted_iota(i32, (ncmp, rows), 0) * CMP_STRIDE + (CMP_BLOCK - 1)
    vis = cmp_end <= row_t
    s = jnp.where(vis, s, NEG_INF)
    s = s - jnp.max(s, axis=0, keepdims=True)
    e = jnp.where(vis, jnp.exp2(s), 0.0)
    p_c = e * (1.0 / jnp.maximum(jnp.sum(e, axis=0, keepdims=True), 1e-30))
    o_c = _dot(p_c.T.astype(bf16), vc_ref[0, 0, 0])

    psum = p_c[:, 0:t]
    for g in range(1, NSA_GROUP):
        psum = psum + p_c[:, g * t:(g + 1) * t]
    hi = psum.astype(bf16)
    lo = (psum - hi.astype(f32)).astype(bf16)
    imp = (_dot(ov_ref[...], hi) + _dot(ov_ref[...], lo))[0:nsel]
    blk = lax.broadcasted_iota(i32, (nsel, t), 0)
    cur = (t0 + lax.broadcasted_iota(i32, (nsel, t), 1)) >> 6
    forced = (blk == 0) | ((blk <= cur) & (blk > cur - N_LOCAL_SEL))
    val = jnp.where(forced, FORCED_SCORE, jnp.where(blk <= cur, imp, -1.0))
    cnt = jnp.zeros((nsel, t), f32)
    for j in range(nsel):
        vj = val[j:j + 1, :]
        beats = (vj > val) | ((vj == val) & (blk > j))
        cnt = cnt + jnp.where(beats, 1.0, 0.0)
    mneg = jnp.where(cnt < topn, 0.0, NEG_INF)
    mneg = jnp.concatenate([mneg, jnp.zeros((LANES - nsel, t), f32)], axis=0).T.astype(bf16)
    for g in range(NSA_GROUP):
        qaug_ref[g * t:(g + 1) * t, LANES:2 * LANES] = mneg

    _flash_init(m_ref, acc_ref)

    def sel_logits(kt):
        k = kaug_ref[pl.ds(kt * t, t), :]
        return _dot_nt(qaug_ref[...], k) + tb_ref[0, jnp.minimum(i - kt, 2)]

    def values(v_ref):
        return lambda kt: (v_ref[0, pl.ds(kt * t, t), :], v_ref[1, pl.ds(kt * t, t), :])

    _flash_loop(0, i, sel_logits, values(vsel_ref), s_ref, m_ref, acc_ref)
    os_ref[...] = _flash_finish(acc_ref)

    _flash_init(m_ref, acc_ref)
    nwin = WINDOW // t

    def win_logits(kt):
        d = i - kt
        kind = jnp.where(d == nwin, 3, d)
        return _dot_nt(qs, kw_ref[0, 0, pl.ds(kt * t, t), :]) + tb_ref[0, kind]

    _flash_loop(jnp.maximum(i - nwin, 0), i, win_logits, values(vwin_ref), s_ref, m_ref, acc_ref)
    o_w = _flash_finish(acc_ref)
    o_s = os_ref[...]

    gates = gate_ref[0, 0]
    outs = []
    for g in range(NSA_GROUP):
        rb = NSA_ROW_ORDER.index(g)
        sl = slice(rb * t, (rb + 1) * t)
        outs.append(gates[:, g:g + 1] * o_c[sl]
                    + gates[:, NSA_GROUP + g:NSA_GROUP + g + 1] * o_s[sl]
                    + gates[:, 2 * NSA_GROUP + g:2 * NSA_GROUP + g + 1] * o_w[sl])
    for j in range(NSA_GROUP // 2):
        o_ref[0, :, j * LANES:(j + 1) * LANES] = jnp.where(lo_half, outs[2 * j], outs[2 * j + 1]).astype(o_ref.dtype)


def _nsa(qkv, cmpkv, gates, bias_c, tbias, ov, et):
    b, _, s, _ = qkv.shape
    ncmp = cmpkv.shape[3]
    t = T_ATT
    rows = NSA_GROUP * t
    qw = NSA_GROUP * HEAD_DIM
    qtiles = qw // LANES
    kv_spec = lambda col: pl.BlockSpec((1, 1, s, LANES), lambda h, bi, i, col=col: (bi, col + h, 0, 0))
    cmp_spec = lambda kv: pl.BlockSpec((1, 1, 1, ncmp, LANES), lambda h, bi, i, kv=kv: (bi, kv, h, 0, 0))
    base = QKV_TILES_KV
    return pl.pallas_call(
        functools.partial(_nsa_kernel, seq=s, ncmp=ncmp),
        grid=(NSA_KV_HEADS, b, s // t),
        in_specs=[pl.BlockSpec((1, qtiles, t, LANES), lambda h, bi, i: (bi, h, i, 0)),
                  kv_spec(base), kv_spec(base + 2), kv_spec(base + 4), kv_spec(base + 6),
                  cmp_spec(0), cmp_spec(1),
                  pl.BlockSpec((1, 1, t, LANES), lambda h, bi, i: (bi, h, i, 0)),
                  pl.BlockSpec((1, ncmp, rows), lambda h, bi, i: (h, i, 0)),
                  pl.BlockSpec((1, 4, rows, t), lambda h, bi, i: (h, 0, 0, 0)),
                  pl.BlockSpec((LANES, ncmp), lambda h, bi, i: (0, 0)),
                  pl.BlockSpec((s, LANES), lambda h, bi, i: (0, 0))],
        out_specs=pl.BlockSpec((1, t, qw), lambda h, bi, i: (bi, i, h)),
        out_shape=jax.ShapeDtypeStruct((b, s, NSA_Q_W), bf16),
        scratch_shapes=[pltpu.VMEM((rows, 2 * LANES), bf16),
                        pltpu.VMEM((s, 2 * LANES), bf16),
                        pltpu.VMEM((rows, LANES), f32),
                        pltpu.VMEM((rows, LANES), f32),
                        pltpu.VMEM((rows, LANES), f32),
                        pltpu.VMEM((2, rows, t), f32),
                        pltpu.VMEM((2, s, LANES), bf16),
                        pltpu.VMEM((2, s, LANES), bf16)],
        compiler_params=_cparams(("parallel", "parallel", "arbitrary")),
        name="nsa",
    )(qkv, qkv, qkv, qkv, qkv, cmpkv, cmpkv, gates, bias_c, tbias, ov, et)


def _fox_kernel(q_ref, k_ref, v_ref, qx_ref, kx_ref, o_ref, qaug_ref, kaug_ref, m_ref, acc_ref, s_ref,
                vaug_ref):
    t = T_ATT
    i = pl.program_id(2)
    lane = lax.broadcasted_iota(i32, (t, LANES), 1)
    lo_half = lane < HEAD_DIM

    @pl.when(i == 0)
    def _():
        kaug_ref[:, 0:LANES] = k_ref[0, 0]
        kaug_ref[:, LANES:2 * LANES] = kx_ref[0, 0]
        vaug_ref[0], vaug_ref[1] = _ones_values(v_ref[0, 0])

    q = q_ref[0, 0].astype(f32)
    qx = qx_ref[0, 0].astype(f32)
    qaug_ref[0:t, 0:LANES] = jnp.where(lo_half, q, 0.0).astype(bf16)
    qaug_ref[t:2 * t, 0:LANES] = jnp.where(lo_half, 0.0, q).astype(bf16)
    qaug_ref[0:t, LANES:2 * LANES] = jnp.where(lane < XCOLS, qx, 0.0).astype(bf16)
    qaug_ref[t:2 * t, LANES:2 * LANES] = jnp.where((lane >= XCOLS) & (lane < 2 * XCOLS), qx, 0.0).astype(bf16)
    _flash_init(m_ref, acc_ref)

    def logits(kt):
        return _dot_nt(qaug_ref[...], kaug_ref[pl.ds(kt * t, t), :])

    def values(kt):
        return vaug_ref[0, pl.ds(kt * t, t), :], vaug_ref[1, pl.ds(kt * t, t), :]

    def causal(s):
        row = lax.broadcasted_iota(i32, (2 * t, t), 0) & (t - 1)
        return jnp.where(lax.broadcasted_iota(i32, (2 * t, t), 1) <= row, s, NEG_INF)

    _flash_loop(0, i, logits, values, s_ref, m_ref, acc_ref, last_fix=causal)
    o = _flash_finish(acc_ref)
    o_ref[0] = jnp.where(lo_half, o[0:t], o[t:2 * t]).astype(o_ref.dtype)


def _fox(qkv, qx, kx):
    b, _, s, _ = qkv.shape
    t = T_ATT
    base = QKV_TILES_FOX
    return pl.pallas_call(
        _fox_kernel,
        grid=(b, FOX_PAIRS, s // t),
        in_specs=[pl.BlockSpec((1, 1, t, LANES), lambda bi, p, i: (bi, base + p, i, 0)),
                  pl.BlockSpec((1, 1, s, LANES), lambda bi, p, i: (bi, base + FOX_PAIRS + p, 0, 0)),
                  pl.BlockSpec((1, 1, s, LANES), lambda bi, p, i: (bi, base + 2 * FOX_PAIRS + p, 0, 0)),
                  pl.BlockSpec((1, 1, t, LANES), lambda bi, p, i: (bi, p, i, 0)),
                  pl.BlockSpec((1, 1, s, LANES), lambda bi, p, i: (bi, p, 0, 0))],
        out_specs=pl.BlockSpec((1, t, LANES), lambda bi, p, i: (bi, i, p)),
        out_shape=jax.ShapeDtypeStruct((b, s, FOX_W), bf16),
        scratch_shapes=[pltpu.VMEM((2 * t, 2 * LANES), bf16),
                        pltpu.VMEM((s, 2 * LANES), bf16),
                        pltpu.VMEM((2 * t, LANES), f32),
                        pltpu.VMEM((2 * t, LANES), f32),
                        pltpu.VMEM((2, 2 * t, t), f32),
                        pltpu.VMEM((2, s, LANES), bf16)],
        compiler_params=_cparams(("parallel", "parallel", "arbitrary")),
        name="fox",
    )(qkv, qkv, qkv, qx, kx)


def _merge_kernel(yn_ref, yf_ref, xb_ref, x_ref, wn_ref, wf_ref, wmg_ref, wo_ref, g_ref, b_ref,
                  xo_ref, xbo_ref):
    mg = _dot(xb_ref[...], wmg_ref[...])
    merged = (_sigmoid(mg[:, 0:D_MODEL]) * _dot(yn_ref[...], wn_ref[...])
              + _sigmoid(mg[:, D_MODEL:2 * D_MODEL]) * _dot(yf_ref[...], wf_ref[...]))
    hmix = _dot(merged.astype(bf16), wo_ref[...])
    xn = _layer_norm(DN_ALPHA * x_ref[...] + hmix, g_ref[...], b_ref[...])
    xo_ref[...] = xn
    xbo_ref[...] = xn.astype(bf16)


def _merge(yn, yf, xb, x, wn, wf, wmg, wo, g, bb):
    m = x.shape[0]
    tm = TM_MERGE
    row = lambda w: pl.BlockSpec((tm, w), lambda i: (i, 0))
    full = lambda a: pl.BlockSpec(a.shape, lambda i: (0, 0))
    return pl.pallas_call(
        _merge_kernel,
        grid=(m // tm,),
        in_specs=[row(NSA_Q_W), row(FOX_W), row(D_MODEL), row(D_MODEL),
                  full(wn), full(wf), full(wmg), full(wo), full(g), full(bb)],
        out_specs=[row(D_MODEL), row(D_MODEL)],
        out_shape=[jax.ShapeDtypeStruct((m, D_MODEL), f32), jax.ShapeDtypeStruct((m, D_MODEL), bf16)],
        compiler_params=_cparams(("parallel",)),
        name="merge",
    )(yn, yf, xb, x, wn, wf, wmg, wo, g, bb)


def _ffn_kernel(xb_ref, x_ref, wg_ref, wu_ref, wd_ref, g_ref, b_ref, xo_ref, xbo_ref, acc_ref, *, nf):
    f = pl.program_id(1)

    @pl.when(f == 0)
    def _():
        acc_ref[...] = jnp.zeros_like(acc_ref)

    xb = xb_ref[...]
    gate = _dot(xb, wg_ref[...])
    up = _dot(xb, wu_ref[...])
    act = (gate * _sigmoid(gate) * up).astype(bf16)
    acc_ref[...] += _dot(act, wd_ref[...])

    @pl.when(f == nf - 1)
    def _():
        xn = _layer_norm(DN_ALPHA * x_ref[...] + acc_ref[...], g_ref[...], b_ref[...])
        xo_ref[...] = xn
        xbo_ref[...] = xn.astype(bf16)


def _ffn(xb, x, wg, wu, wd, g, bb):
    m = x.shape[0]
    dff = wg.shape[1]
    tm, tf = TM_FFN, TF_FFN
    nf = dff // tf
    row = pl.BlockSpec((tm, D_MODEL), lambda i, f: (i, 0))
    vec = pl.BlockSpec((1, D_MODEL), lambda i, f: (0, 0))
    return pl.pallas_call(
        functools.partial(_ffn_kernel, nf=nf),
        grid=(m // tm, nf),
        in_specs=[row, row,
                  pl.BlockSpec((D_MODEL, tf), lambda i, f: (0, f)),
                  pl.BlockSpec((D_MODEL, tf), lambda i, f: (0, f)),
                  pl.BlockSpec((tf, D_MODEL), lambda i, f: (f, 0)),
                  vec, vec],
        out_specs=[row, row],
        out_shape=[jax.ShapeDtypeStruct((m, D_MODEL), f32), jax.ShapeDtypeStruct((m, D_MODEL), bf16)],
        scratch_shapes=[pltpu.VMEM((tm, D_MODEL), f32)],
        compiler_params=_cparams(("parallel", "arbitrary")),
        name="ffn",
    )(xb, x, wg, wu, wd, g, bb)


def _router_kernel(x_ref, r_ref, tri_ref, gate_ref, rank_ref, *, seq):
    tb = tri_ref.shape[0]
    x = x_ref[0]
    xh = x.astype(bf16)
    xl = (x - xh.astype(f32)).astype(bf16)
    r = r_ref[...]
    rh = r.astype(bf16)
    rl = (r - rh.astype(f32)).astype(bf16)
    logits = _dot(xh, rh) + _dot(xh, rl) + _dot(xl, rh)
    lane = lax.broadcasted_iota(i32, (seq, LANES), 1).astype(f32)
    low = -3.0e38
    lg = jnp.where(lane < N_EXPERTS, logits, low)
    m1 = jnp.max(lg, axis=-1, keepdims=True)
    i1 = jnp.min(jnp.where(lg == m1, lane, float(LANES)), axis=-1, keepdims=True)
    lg2 = jnp.where(lane == i1, low, lg)
    m2 = jnp.max(lg2, axis=-1, keepdims=True)
    i2 = jnp.min(jnp.where(lg2 == m2, lane, float(LANES)), axis=-1, keepdims=True)
    e2 = jnp.exp(m2 - m1)
    den = 1.0 + e2
    gate_ref[0] = jnp.where(lane == i1, 1.0 / den, jnp.where(lane == i2, e2 / den, 0.0))
    sel = (lane == i1) | (lane == i2)
    selb = jnp.where(sel, 1.0, 0.0).astype(bf16)
    carry = jnp.zeros((1, LANES), f32)
    for blk in range(seq // tb):
        sl = slice(blk * tb, (blk + 1) * tb)
        c = _dot(tri_ref[...], selb[sl]) + carry
        carry = c[tb - 1:tb, :]
        rank_ref[0, sl, :] = jnp.where(sel[sl], c - 1.0, -1.0)


def _router(x3, router_pad, tri):
    b, s, _ = x3.shape
    return pl.pallas_call(
        functools.partial(_router_kernel, seq=s),
        grid=(b,),
        in_specs=[pl.BlockSpec((1, s, D_MODEL), lambda i: (i, 0, 0)),
                  pl.BlockSpec((D_MODEL, LANES), lambda i: (0, 0)),
                  pl.BlockSpec(tri.shape, lambda i: (0, 0))],
        out_specs=[pl.BlockSpec((1, s, LANES), lambda i: (i, 0, 0)),
                   pl.BlockSpec((1, s, LANES), lambda i: (i, 0, 0))],
        out_shape=[jax.ShapeDtypeStruct((b, s, LANES), f32),
                   jax.ShapeDtypeStruct((b, s, LANES), f32)],
        compiler_params=_cparams(("parallel",)),
        name="router",
    )(x3, router_pad, tri)


def _moe_kernel(tot_ref, tot_al_ref, off_ref, npc_ref, xb_ref, rankt_ref, gatet_ref, ranke_ref, wg_ref, wu_ref, wd_ref,
                y_ref, xg_ref, acc_ref, *, seq, nf):
    tr = TR_MOE
    half = tr // 2
    nsubc = seq // SUB_MOE
    c = pl.program_id(0)
    e = pl.program_id(1)
    f = pl.program_id(2)
    ce = c * N_EXPERTS + e
    tot = tot_ref[ce]
    rem = tot % tr
    nfull = tot // tr + jnp.where(rem > half, 1, 0)
    has_tail = (rem > 0) & (rem <= half)
    tail0 = pl.multiple_of(nfull * tr, half)

    def pieces(fn):
        for q in range(nsubc):
            base = off_ref[ce * nsubc + q]

            def body(p, carry, q=q, base=base):
                fn(q, pl.multiple_of(base + p * PIECE_MOE, ROW_ALIGN))
                return carry
            lax.fori_loop(0, npc_ref[ce * nsubc + q], body, 0)

    def onehot_rows(q, r0):
        rk = rankt_ref[0, pl.ds(e, 1), q * SUB_MOE:(q + 1) * SUB_MOE]
        want = (r0 + lax.broadcasted_iota(i32, (PIECE_MOE, SUB_MOE), 0)).astype(f32)
        return rk == want

    def gather(q, r0):
        p = jnp.where(onehot_rows(q, r0), 1.0, 0.0).astype(bf16)
        xg_ref[pl.ds(r0, PIECE_MOE), :] += _dot(p, xb_ref[0, q * SUB_MOE:(q + 1) * SUB_MOE, :]).astype(bf16)

    def clear(r0, nr):
        xg_ref[pl.ds(r0, nr), :] = jnp.zeros((nr, D_MODEL), bf16)
        acc_ref[pl.ds(r0, nr), :] = jnp.zeros((nr, D_MODEL), f32)

    def hidden(r0, nr):
        xs = xg_ref[pl.ds(r0, nr), :]
        gate = _dot(xs, wg_ref[0])
        up = _dot(xs, wu_ref[0])
        act = (gate * _sigmoid(gate) * up).astype(bf16)
        acc_ref[pl.ds(r0, nr), :] += _dot(act, wd_ref[0])

    def combine(q, r0):
        grow = gatet_ref[0, pl.ds(e, 1), q * SUB_MOE:(q + 1) * SUB_MOE]
        w = jnp.sum(jnp.where(onehot_rows(q, r0), grow, 0.0), axis=-1, keepdims=True)
        z = (acc_ref[pl.ds(r0, PIECE_MOE), :] * w).astype(bf16)
        want = (r0 + lax.broadcasted_iota(i32, (SUB_MOE, PIECE_MOE), 1)).astype(f32)
        rc = ranke_ref[0, 0, q * SUB_MOE:(q + 1) * SUB_MOE, :]
        pt = jnp.where(rc == want, 1.0, 0.0).astype(bf16)
        y_ref[0, q * SUB_MOE:(q + 1) * SUB_MOE, :] += _dot(pt, z)

    def tiles(fn):
        def body(s, carry):
            fn(pl.multiple_of(s * tr, tr), tr)
            return carry
        lax.fori_loop(0, nfull, body, 0)

        @pl.when(has_tail)
        def _():
            fn(tail0, half)

    def zero(j, carry):
        y_ref[0, pl.ds(j * tr, tr), :] = jnp.zeros((tr, D_MODEL), f32)
        return carry

    lax.fori_loop(0, jnp.where((e == 0) & (f == 0), seq // tr, 0), zero, 0)

    @pl.when(f == 0)
    def _():
        tiles(clear)
        clear(pl.multiple_of(tot_al_ref[ce], ROW_ALIGN), tr)
        pieces(gather)

    tiles(hidden)

    @pl.when(f == nf - 1)
    def _():
        pieces(combine)


def _moe(xb3, gate, rank, wg, wu, wd):
    b, s, _ = xb3.shape
    dff = wg.shape[2]
    nf = dff // TF_MOE
    nsubc = s // SUB_MOE
    rk = rank[:, :, :N_EXPERTS]
    cnt = jnp.sum((rk >= 0.0).reshape(b, nsubc, SUB_MOE, N_EXPERTS), axis=2).astype(i32)
    first = jnp.cumsum(cnt, axis=1) - cnt
    off = first // ROW_ALIGN * ROW_ALIGN
    npc = jnp.where(cnt > 0, (first - off + cnt + PIECE_MOE - 1) // PIECE_MOE, 0)
    post = rk.transpose(0, 2, 1)
    gatet = gate[:, :, :N_EXPERTS].transpose(0, 2, 1)
    tot = jnp.sum(cnt, axis=1).reshape(-1)
    tot_al = (tot + ROW_ALIGN - 1) // ROW_ALIGN * ROW_ALIGN
    off_flat = off.transpose(0, 2, 1).reshape(-1)
    npc = npc.transpose(0, 2, 1).reshape(-1)
    buf_rows = s + ROW_ALIGN + PIECE_MOE + TR_MOE
    grid_spec = pltpu.PrefetchScalarGridSpec(
        num_scalar_prefetch=4,
        grid=(b, N_EXPERTS, nf),
        in_specs=[pl.BlockSpec((1, s, D_MODEL), lambda c, e, f, *_: (c, 0, 0)),
                  pl.BlockSpec((1, N_EXPERTS, s), lambda c, e, f, *_: (c, 0, 0)),
                  pl.BlockSpec((1, N_EXPERTS, s), lambda c, e, f, *_: (c, 0, 0)),
                  pl.BlockSpec((1, 1, s, 1), lambda c, e, f, *_: (c, e, 0, 0)),
                  pl.BlockSpec((1, D_MODEL, TF_MOE), lambda c, e, f, *_: (e, 0, f)),
                  pl.BlockSpec((1, D_MODEL, TF_MOE), lambda c, e, f, *_: (e, 0, f)),
                  pl.BlockSpec((1, TF_MOE, D_MODEL), lambda c, e, f, *_: (e, f, 0))],
        out_specs=pl.BlockSpec((1, s, D_MODEL), lambda c, e, f, *_: (c, 0, 0)),
        scratch_shapes=[pltpu.VMEM((buf_rows, D_MODEL), bf16), pltpu.VMEM((buf_rows, D_MODEL), f32)],
    )
    return pl.pallas_call(
        functools.partial(_moe_kernel, seq=s, nf=nf),
        grid_spec=grid_spec,
        out_shape=jax.ShapeDtypeStruct((b, s, D_MODEL), f32),
        compiler_params=_cparams(("parallel", "arbitrary", "arbitrary")),
        name="moe",
    )(tot, tot_al, off_flat, npc, xb3, post, gatet, post[..., None], wg, wu, wd)


def _resln_kernel(x_ref, y_ref, g_ref, b_ref, xo_ref, xbo_ref):
    xn = _layer_norm(DN_ALPHA * x_ref[...] + y_ref[...], g_ref[...], b_ref[...])
    xo_ref[...] = xn
    xbo_ref[...] = xn.astype(bf16)


def _resln(x, y, g, bb):
    m = x.shape[0]
    tm = TM_FFN
    row = pl.BlockSpec((tm, D_MODEL), lambda i: (i, 0))
    vec = pl.BlockSpec((1, D_MODEL), lambda i: (0, 0))
    return pl.pallas_call(
        _resln_kernel,
        grid=(m // tm,),
        in_specs=[row, row, vec, vec],
        out_specs=[row, row],
        out_shape=[jax.ShapeDtypeStruct((m, D_MODEL), f32), jax.ShapeDtypeStruct((m, D_MODEL), bf16)],
        compiler_params=_cparams(("parallel",)),
        name="resln",
    )(x, y, g, bb)


def _t5_bucket(dist):
    n = jnp.maximum(dist, 0)
    max_exact = REL_BUCKETS // 2
    large = max_exact + (jnp.log(jnp.maximum(n, 1).astype(f32) / max_exact)
                         / math.log(REL_MAX_DIST / max_exact) * (REL_BUCKETS - max_exact)).astype(i32)
    large = jnp.minimum(large, REL_BUCKETS - 1)
    return jnp.where(n < max_exact, n, large)


def _bias_of_dist(rel_bias, dist):
    onehot = (_t5_bucket(dist)[None] == jnp.arange(REL_BUCKETS).reshape((-1,) + (1,) * dist.ndim)).astype(f32)
    return LOG2E * jnp.einsum("kh,k...->h...", rel_bias.astype(f32), onehot, precision=lax.Precision.HIGHEST)


def _bias_tables(rel_bias, seq):
    t = T_ATT
    rows = NSA_GROUP * t
    ncmp = seq // CMP_STRIDE
    d0 = jnp.arange(t)[:, None] - jnp.arange(t)[None, :]
    offs = jnp.array([0, t, 2 * t, WINDOW]).reshape(4, 1, 1)
    kinds = _bias_of_dist(rel_bias, offs + d0[None])
    mask = jnp.stack([d0 >= 0, d0 == d0, d0 == d0, d0 < 0])
    kinds = jnp.where(mask[None], kinds, NEG_INF)
    order = np.array(NSA_ROW_ORDER)
    tbias = kinds.reshape(NSA_KV_HEADS, NSA_GROUP, 4, t, t)[:, order].transpose(0, 2, 1, 3, 4)
    tbias = tbias.reshape(NSA_KV_HEADS, 4, rows, t)
    cend = jnp.arange(ncmp) * CMP_STRIDE + CMP_BLOCK - 1
    bc = _bias_of_dist(rel_bias, jnp.arange(seq)[:, None] - cend[None, :])
    bc = bc.reshape(NSA_KV_HEADS, NSA_GROUP, seq // t, t, ncmp)[:, order].transpose(0, 2, 4, 1, 3)
    return tbias, bc.reshape(NSA_KV_HEADS, (seq // t) * ncmp, rows)


def _selection_constants(seq):
    ncmp = seq // CMP_STRIDE
    nsel = seq // SEL_BLOCK
    c0 = np.arange(ncmp)[:, None] * CMP_STRIDE
    s0 = np.arange(LANES)[None, :] * SEL_BLOCK
    ov = np.maximum(np.minimum(c0 + CMP_BLOCK, s0 + SEL_BLOCK) - np.maximum(c0, s0), 0) / CMP_BLOCK
    ov[ncmp - 1, :] = 0.0
    ov[:, nsel:] = 0.0
    et = (np.arange(seq)[:, None] // SEL_BLOCK == np.arange(LANES)[None, :]).astype(np.float32)
    return jnp.asarray(ov.T, bf16), jnp.asarray(et, bf16)


def _fox_placement():
    xw = FOX_PAIRS * LANES
    pq = np.zeros((3, LANES, xw), np.float32)
    pk = np.zeros((3, LANES, xw), np.float32)
    oq = np.zeros((1, xw), np.float32)
    ok = np.zeros((1, xw), np.float32)
    for p in range(FOX_PAIRS):
        for hh in range(2):
            src = FGATE_LANE + 2 * p + hh
            base = p * LANES + hh * XCOLS
            for part in range(3):
                pk[part, src, base + part] = -1.0
                pq[part, src, base + 3 + part] = 1.0
                oq[0, base + part] = 1.0
                ok[0, base + 3 + part] = 1.0
    return jnp.asarray(pq, bf16), jnp.asarray(pk, bf16), jnp.asarray(oq), jnp.asarray(ok)


def _layer_weights(w_in, layer_pe, w1, w2, f_bias):
    offs = np.cumsum((NSA_Q_W, 6 * 2 * HEAD_DIM, 3 * NSA_HEADS, 3 * FOX_W, FOX_HEADS, 2 * D_MODEL))
    kv0, g0, fx0, ff0, mg0 = offs[0], offs[1], offs[2], offs[3], offs[4]
    scale = HEAD_DIM ** -0.5 * LOG2E
    kvw = NSA_KV_HEADS * HEAD_DIM
    w_kv = w_in[:, kv0:g0]
    w_kvdup = jnp.repeat(w_kv[:, 2 * kvw:].reshape(D_MODEL, 4 * NSA_KV_HEADS, 1, HEAD_DIM), 2, axis=2)
    w_kvdup = w_kvdup.reshape(D_MODEL, 8 * kvw)
    w_qkv = jnp.concatenate([w_in[:, :NSA_Q_W] * scale, w_kvdup, w_in[:, fx0:fx0 + FOX_W] * scale,
                             w_in[:, fx0 + FOX_W:ff0]], axis=1).astype(bf16)
    zeros = lambda n: jnp.zeros((D_MODEL, n), w_in.dtype)
    ng = 3 * NSA_GROUP
    w_g = w_in[:, g0:fx0].reshape(D_MODEL, NSA_KV_HEADS, NSA_GROUP, 3).transpose(0, 1, 3, 2)
    gate_cols = lambda h: w_g[:, h].reshape(D_MODEL, ng)
    w_aux = jnp.concatenate([w_kv[:, :2 * kvw],
                             gate_cols(0), zeros(FGATE_LANE - ng), w_in[:, ff0:mg0],
                             zeros(LANES - FGATE_LANE - FOX_HEADS),
                             gate_cols(1), zeros(LANES - ng)], axis=1).astype(bf16)
    fb_row = jnp.zeros((1, LANES), f32).at[0, FGATE_LANE:FGATE_LANE + FOX_HEADS].set(f_bias.astype(f32))
    pe2 = jnp.tile(layer_pe.astype(f32), (1, 1, NSA_KV_HEADS))
    w1r = w1.reshape(2, CMP_BLOCK, HEAD_DIM, CMP_HIDDEN).astype(bf16)
    zero = jnp.zeros_like(w1r)
    w1bd = jnp.concatenate([jnp.concatenate([w1r, zero], axis=-1),
                            jnp.concatenate([zero, w1r], axis=-1)], axis=-2)
    w2d = jnp.concatenate([w2, w2], axis=-1).astype(bf16)
    return (w_qkv, w_aux, w_in[:, mg0:].astype(bf16), fb_row, pe2, w1bd[:, :CMP_STRIDE], w1bd[:, CMP_STRIDE:],
            w2d)


def kernel(x, w_in, nsa_cmp_pe, nsa_cmp_w1, nsa_cmp_w2, fox_f_bias, w_nsa_branch, w_fox_branch, w_out,
           rel_bias, ln1_g, ln1_b, ln2_g, ln2_b, dense_w_gate, dense_w_up, dense_w_down, moe_router,
           moe_w_gate, moe_w_up, moe_w_down):
    b, s, d = x.shape
    m = b * s
    tbias, bias_c = _bias_tables(rel_bias, s)
    ov, et = _selection_constants(s)
    pq, pk, oq, ok = _fox_placement()
    tri128 = jnp.asarray(np.tril(np.ones((LANES, LANES), np.float32)), bf16)
    tri256 = jnp.asarray(np.tril(np.ones((256, 256), np.float32)), bf16)

    xf = x.reshape(m, d).astype(f32)
    xb = xf.astype(bf16)
    for layer in range(DEPTH):
        w_qkv, w_aux, w_mg, fb_row, pe2, w1t, w1b, w2d = _layer_weights(
            w_in[layer], nsa_cmp_pe[layer], nsa_cmp_w1[layer], nsa_cmp_w2[layer], fox_f_bias[layer])
        qkv, aux = _proj(xb, w_qkv, w_aux, b)
        aux = aux.reshape(b, s, AUX_W)
        gates, qx, kx = _gateprep(aux, fb_row, tri128, pq, pk, oq, ok)
        cmpkv = _compress(aux, pe2, w1t, w1b, w2d)
        y_nsa = _nsa(qkv, cmpkv, gates, bias_c, tbias, ov, et).reshape(m, NSA_Q_W)
        y_fox = _fox(qkv, qx, kx).reshape(m, FOX_W)
        xf, xb = _merge(y_nsa, y_fox, xb, xf, w_nsa_branch[layer].astype(bf16),
                        w_fox_branch[layer].astype(bf16), w_mg, w_out[layer].astype(bf16),
                        ln1_g[layer].reshape(1, d), ln1_b[layer].reshape(1, d))
        j = layer // 2
        g2, b2 = ln2_g[layer].reshape(1, d), ln2_b[layer].reshape(1, d)
        if layer % 2 == 0:
            xf, xb = _ffn(xb, xf, dense_w_gate[j].astype(bf16), dense_w_up[j].astype(bf16),
                          dense_w_down[j].astype(bf16), g2, b2)
        else:
            router_pad = jnp.zeros((d, LANES), f32).at[:, :N_EXPERTS].set(moe_router[j].astype(f32))
            gate, rank = _router(xf.reshape(b, s, d), router_pad, tri256)
            y = _moe(xb.reshape(b, s, d), gate, rank,
                     moe_w_gate[j].astype(bf16), moe_w_up[j].astype(bf16), moe_w_down[j].astype(bf16))
            xf, xb = _resln(xf, y.reshape(m, d), g2, b2)
    return xf.reshape(b, s, d).astype(x.dtype)
```

```python
import functools
import math

import numpy as np
import jax
import jax.numpy as jnp
from jax import lax
from jax.experimental import pallas as pl
from jax.experimental.pallas import tpu as pltpu

f32 = jnp.float32
bf16 = jnp.bfloat16
i32 = jnp.int32

D_MODEL = 1024
HEAD_DIM = 64
LANES = 128
NSA_HEADS = 8
NSA_KV_HEADS = 2
NSA_GROUP = NSA_HEADS // NSA_KV_HEADS
NSA_ROW_ORDER = (0, 2, 1, 3)
FOX_HEADS = 8
FOX_PAIRS = FOX_HEADS // 2
CMP_BLOCK = 32
CMP_STRIDE = 16
CMP_HIDDEN = 128
SEL_BLOCK = 64
N_SELECT = 16
N_LOCAL_SEL = 2
WINDOW = 512
REL_BUCKETS = 32
REL_MAX_DIST = 128
N_EXPERTS = 8
DEPTH = 4
DN_ALPHA = (2 * DEPTH) ** 0.25
LN_EPS = 1e-5
FORCED_SCORE = 1e4
NEG_INF = -1e30
LOG2E = math.log2(math.e)

NSA_Q_W = NSA_HEADS * HEAD_DIM
FOX_W = FOX_HEADS * HEAD_DIM
QKV_TILES_KV = NSA_Q_W // LANES
QKV_TILES_FOX = QKV_TILES_KV + 4 * NSA_KV_HEADS
QKV_W = (QKV_TILES_FOX + 3 * FOX_PAIRS) * LANES
AUX_W = 4 * LANES
FGATE_LANE = 24
XCOLS = 6

T_ATT = 256
TM_PROJ = 512
TM_MERGE = 256
TM_FFN = 512
TF_FFN = 1408
TR_MOE = 256
SUB_MOE = 512
PIECE_MOE = 192
ROW_ALIGN = 16
TF_MOE = 896
VMEM_LIMIT = 56 * 1024 * 1024


def _cparams(sem):
    return pltpu.CompilerParams(dimension_semantics=sem, vmem_limit_bytes=VMEM_LIMIT)


def _dot(a, b):
    return jnp.dot(a, b, preferred_element_type=f32)


def _dot_nt(a, b):
    return lax.dot_general(a, b, (((1,), (1,)), ((), ())), preferred_element_type=f32)


def _sigmoid(x):
    return 1.0 / (1.0 + jnp.exp(-x))


def _layer_norm(z, g, b):
    mu = jnp.mean(z, axis=-1, keepdims=True)
    zc = z - mu
    var = jnp.mean(zc * zc, axis=-1, keepdims=True)
    return zc * lax.rsqrt(var + LN_EPS) * g + b


def _split3(x):
    hi = x.astype(bf16)
    r1 = x - hi.astype(f32)
    mid = r1.astype(bf16)
    lo = (r1 - mid.astype(f32)).astype(bf16)
    return hi, mid, lo


def _proj_kernel(x_ref, w_ref, wa_ref, o_ref, oa_ref):
    x = x_ref[...]
    res = _dot(x, w_ref[...])
    for j in range(o_ref.shape[1]):
        o_ref[0, j] = res[:, j * LANES:(j + 1) * LANES].astype(o_ref.dtype)
    oa_ref[...] = _dot(x, wa_ref[...])


def _proj(xb, w, wa, batch):
    m, k = xb.shape
    n, na = w.shape[1], wa.shape[1]
    nb = m // batch // TM_PROJ
    return pl.pallas_call(
        _proj_kernel,
        grid=(m // TM_PROJ,),
        in_specs=[pl.BlockSpec((TM_PROJ, k), lambda i: (i, 0)),
                  pl.BlockSpec((k, n), lambda i: (0, 0)),
                  pl.BlockSpec((k, na), lambda i: (0, 0))],
        out_specs=[pl.BlockSpec((1, n // LANES, TM_PROJ, LANES), lambda i: (i // nb, 0, i % nb, 0)),
                   pl.BlockSpec((TM_PROJ, na), lambda i: (i, 0))],
        out_shape=[jax.ShapeDtypeStruct((batch, n // LANES, m // batch, LANES), bf16),
                   jax.ShapeDtypeStruct((m, na), f32)],
        compiler_params=_cparams(("parallel",)),
        name="proj",
    )(xb, w, wa)


def _gateprep_kernel(a_ref, fb_ref, tri_ref, pq_ref, pk_ref, oq_ref, ok_ref, g_ref, qx_ref, kx_ref, *, seq):
    tb = LANES
    tri = tri_ref[...]
    carry = jnp.zeros((1, LANES), f32)
    for blk in range(seq // tb):
        sl = slice(blk * tb, (blk + 1) * tb)
        va = a_ref[0, sl, 0:LANES]
        g_ref[0, 0, sl, :] = _sigmoid(va)
        g_ref[0, 1, sl, :] = _sigmoid(a_ref[0, sl, LANES:2 * LANES])
        z = va + fb_ref[...]
        logf = jnp.minimum(z, 0.0) - jnp.log1p(jnp.exp(-jnp.abs(z)))
        hi, mid, lo = _split3(logf)
        c = _dot(tri, hi) + _dot(tri, mid) + _dot(tri, lo) + carry
        carry = c[tb - 1:tb, :]
        chi, cmid, clo = _split3(c * LOG2E)
        qx = _dot(chi, pq_ref[0]) + _dot(cmid, pq_ref[1]) + _dot(clo, pq_ref[2]) + oq_ref[...]
        kx = _dot(chi, pk_ref[0]) + _dot(cmid, pk_ref[1]) + _dot(clo, pk_ref[2]) + ok_ref[...]
        for p in range(FOX_PAIRS):
            qx_ref[0, p, sl, :] = qx[:, p * LANES:(p + 1) * LANES].astype(bf16)
            kx_ref[0, p, sl, :] = kx[:, p * LANES:(p + 1) * LANES].astype(bf16)


def _gateprep(aux, fb_row, tri, pq, pk, oq, ok):
    b, s, _ = aux.shape
    const2 = lambda a: pl.BlockSpec(a.shape, lambda i: (0, 0))
    const3 = lambda a: pl.BlockSpec(a.shape, lambda i: (0, 0, 0))
    return pl.pallas_call(
        functools.partial(_gateprep_kernel, seq=s),
        grid=(b,),
        in_specs=[pl.BlockSpec((1, s, 2 * LANES), lambda i: (i, 0, 1)),
                  const2(fb_row), const2(tri), const3(pq), const3(pk), const2(oq), const2(ok)],
        out_specs=[pl.BlockSpec((1, 2, s, LANES), lambda i: (i, 0, 0, 0)),
                   pl.BlockSpec((1, FOX_PAIRS, s, LANES), lambda i: (i, 0, 0, 0)),
                   pl.BlockSpec((1, FOX_PAIRS, s, LANES), lambda i: (i, 0, 0, 0))],
        out_shape=[jax.ShapeDtypeStruct((b, 2, s, LANES), f32),
                   jax.ShapeDtypeStruct((b, FOX_PAIRS, s, LANES), bf16),
                   jax.ShapeDtypeStruct((b, FOX_PAIRS, s, LANES), bf16)],
        compiler_params=_cparams(("parallel",)),
        name="gateprep",
    )(aux, fb_row, tri, pq, pk, oq, ok)


def _gelu_tanh(x):
    c = math.sqrt(2.0 / math.pi)
    return x * (0.5 * (1.0 + jnp.tanh(c * (x + 0.044715 * (x * x * x)))))


def _compress_kernel(a_ref, pe_ref, w1t_ref, w1b_ref, w2_ref, o_ref, *, nhalf):
    top = jnp.zeros((nhalf, NSA_KV_HEADS * CMP_HIDDEN), f32)
    bot = jnp.zeros((nhalf, NSA_KV_HEADS * CMP_HIDDEN), f32)
    for l in range(CMP_STRIDE):
        rows = a_ref[0, pl.ds(l, nhalf, stride=CMP_STRIDE), :]
        top = top + _dot((rows + pe_ref[0, l:l + 1, :]).astype(bf16), w1t_ref[0, l])
        bot = bot + _dot((rows + pe_ref[0, CMP_STRIDE + l:CMP_STRIDE + l + 1, :]).astype(bf16), w1b_ref[0, l])
    pre = top + pltpu.roll(bot, nhalf - 1, 0)
    act = _gelu_tanh(pre).astype(bf16)
    for h in range(NSA_KV_HEADS):
        o_ref[0, 0, h] = _dot(act[:, h * CMP_HIDDEN:(h + 1) * CMP_HIDDEN], w2_ref[0]).astype(o_ref.dtype)


def _compress(aux, pe2, w1t, w1b, w2d):
    b, s, _ = aux.shape
    nhalf = s // CMP_STRIDE
    w1_spec = pl.BlockSpec((1,) + w1t.shape[1:], lambda i, j: (j, 0, 0, 0))
    return pl.pallas_call(
        functools.partial(_compress_kernel, nhalf=nhalf),
        grid=(b, 2),
        in_specs=[pl.BlockSpec((1, s, LANES), lambda i, j: (i, 0, j)),
                  pl.BlockSpec((1, CMP_BLOCK, LANES), lambda i, j: (j, 0, 0)),
                  w1_spec, w1_spec,
                  pl.BlockSpec((1, CMP_HIDDEN, LANES), lambda i, j: (j, 0, 0))],
        out_specs=pl.BlockSpec((1, 1, NSA_KV_HEADS, nhalf, LANES), lambda i, j: (i, j, 0, 0, 0)),
        out_shape=jax.ShapeDtypeStruct((b, 2, NSA_KV_HEADS, nhalf, LANES), bf16),
        compiler_params=_cparams(("parallel", "parallel")),
        name="compress",
    )(aux, pe2, w1t, w1b, w2d)


def _flash_init(m_ref, acc_ref):
    m_ref[...] = jnp.full(m_ref.shape, NEG_INF, f32)
    acc_ref[...] = jnp.zeros(acc_ref.shape, f32)


def _ones_values(v):
    lo_half = lax.broadcasted_iota(i32, v.shape, 1) < HEAD_DIM
    vf = v.astype(f32)
    return jnp.where(lo_half, vf, 1.0).astype(bf16), jnp.where(lo_half, 1.0, vf).astype(bf16)


def _flash_step(s, v_lo, v_hi, m_ref, acc_ref):
    nk = s.shape[1] // LANES
    half = s.shape[0] // 2
    cols = [s[:, c * LANES:(c + 1) * LANES] for c in range(nk)]
    mx = cols[0]
    for c in cols[1:]:
        mx = jnp.maximum(mx, c)
    m_old = m_ref[...]
    m_new = jnp.maximum(m_old, jnp.broadcast_to(jnp.max(mx, axis=-1, keepdims=True), m_old.shape))
    alpha = jnp.exp2(m_old - m_new)
    p = jnp.concatenate([jnp.exp2(c - m_new).astype(bf16) for c in cols], axis=1)
    pv = jnp.concatenate([_dot(p[0:half], v_lo), _dot(p[half:], v_hi)], axis=0)
    acc_ref[...] = alpha * acc_ref[...] + pv
    m_ref[...] = m_new


def _flash_loop(first, last, logits_fn, v_fn, s_ref, m_ref, acc_ref, last_fix=lambda s: s):
    sa, sb = s_ref.at[0], s_ref.at[1]
    n = last - first + 1
    pairs = (n - 1) // 2

    def step(buf, kt, fix=lambda s: s):
        _flash_step(fix(buf[...]), *v_fn(kt), m_ref, acc_ref)

    sa[...] = logits_fn(first)

    def body(j, carry):
        kt = first + 2 * j
        sb[...] = logits_fn(kt + 1)
        step(sa, kt)
        sa[...] = logits_fn(kt + 2)
        step(sb, kt + 1)
        return carry

    lax.fori_loop(0, pairs, body, 0)
    two_left = n - 2 * pairs == 2

    @pl.when(two_left)
    def _():
        sb[...] = logits_fn(last)
        step(sa, last - 1)
        step(sb, last, last_fix)

    @pl.when(jnp.logical_not(two_left))
    def _():
        step(sa, last, last_fix)


def _flash_finish(acc_ref):
    acc = acc_ref[...]
    return acc * (1.0 / jnp.maximum(pltpu.roll(acc, HEAD_DIM, 1), 1e-30))


def _nsa_kernel(q_ref, ks_ref, vs_ref, kw_ref, vw_ref, kc_ref, vc_ref, gate_ref, bc_ref, tb_ref, ov_ref,
                et_ref, o_ref, qaug_ref, kaug_ref, m_ref, acc_ref, os_ref, s_ref, vsel_ref, vwin_ref, *,
                seq, ncmp):
    kaug_ref[:, 0:LANES] = ks_ref[0, 0]
    kaug_ref[:, LANES:2 * LANES] = et_ref[...]
    vsel_ref[0], vsel_ref[1] = _ones_values(vs_ref[0, 0])
    vwin_ref[0], vwin_ref[1] = _ones_values(vw_ref[0, 0])

    def query_tile(i, carry):
        _nsa_tile(i, q_ref, kw_ref, kc_ref, vc_ref, gate_ref, bc_ref, tb_ref, ov_ref, o_ref, qaug_ref,
                  kaug_ref, m_ref, acc_ref, os_ref, s_ref, vsel_ref, vwin_ref, seq=seq, ncmp=ncmp)
        return carry

    lax.fori_loop(0, seq // T_ATT, query_tile, 0)


def _nsa_tile(i, q_ref, kw_ref, kc_ref, vc_ref, gate_ref, bc_ref, tb_ref, ov_ref, o_ref, qaug_ref,
              kaug_ref, m_ref, acc_ref, os_ref, s_ref, vsel_ref, vwin_ref, *, seq, ncmp):
    t = T_ATT
    rows = NSA_GROUP * t
    nsel = seq // SEL_BLOCK
    topn = min(N_SELECT, nsel)
    t0 = i * t
    tile_rows = pl.ds(pl.multiple_of(t0, t), t)
    lane = lax.broadcasted_iota(i32, (t, LANES), 1)
    lo_half = lane < HEAD_DIM

    for rb, g in enumerate(NSA_ROW_ORDER):
        qg = q_ref[0, g // 2, tile_rows, :].astype(f32)
        keep = lo_half if g % 2 == 0 else jnp.logical_not(lo_half)
        qaug_ref[rb * t:(rb + 1) * t, 0:LANES] = jnp.where(keep, qg, 0.0).astype(bf16)
    qs = qaug_ref[:, 0:LANES]

    s = _dot_nt(kc_ref[0, 0, 0], qs) + bc_ref[0, pl.ds(pl.multiple_of(i * ncmp, ncmp), ncmp), :]
    row_t = t0 + (lax.broadcasted_iota(i32, (ncmp, rows), 1) & (t - 1))
    cmp_end = lax.broadcasted_iota(i32, (ncmp, rows), 0) * CMP_STRIDE + (CMP_BLOCK - 1)
    vis = cmp_end <= row_t
    s = jnp.where(vis, s, NEG_INF)
    s = s - jnp.max(s, axis=0, keepdims=True)
    e = jnp.where(vis, jnp.exp2(s), 0.0)
    p_c = e * (1.0 / jnp.maximum(jnp.sum(e, axis=0, keepdims=True), 1e-30))
    o_c = _dot(p_c.T.astype(bf16), vc_ref[0, 0, 0])

    psum = p_c[:, 0:t]
    for g in range(1, NSA_GROUP):
        psum = psum + p_c[:, g * t:(g + 1) * t]
    hi = psum.astype(bf16)
    lo = (psum - hi.astype(f32)).astype(bf16)
    imp = (_dot(ov_ref[...], hi) + _dot(ov_ref[...], lo))[0:nsel]
    blk = lax.broadcasted_iota(i32, (nsel, t), 0)
    cur = (t0 + lax.broadcasted_iota(i32, (nsel, t), 1)) >> 6
    forced = (blk == 0) | ((blk <= cur) & (blk > cur - N_LOCAL_SEL))
    val = jnp.where(forced, FORCED_SCORE, jnp.where(blk <= cur, imp, -1.0))
    cnt = jnp.zeros((nsel, t), f32)
    for j in range(nsel):
        vj = val[j:j + 1, :]
        beats = (vj > val) | ((vj == val) & (blk > j))
        cnt = cnt + jnp.where(beats, 1.0, 0.0)
    mneg = jnp.where(cnt < topn, 0.0, NEG_INF)
    mneg = jnp.concatenate([mneg, jnp.zeros((LANES - nsel, t), f32)], axis=0).T.astype(bf16)
    for g in range(NSA_GROUP):
        qaug_ref[g * t:(g + 1) * t, LANES:2 * LANES] = mneg

    _flash_init(m_ref, acc_ref)

    def sel_logits(kt):
        k = kaug_ref[pl.ds(kt * t, t), :]
        return _dot_nt(qaug_ref[...], k) + tb_ref[0, jnp.minimum(i - kt, 2)]

    def values(v_ref):
        return lambda kt: (v_ref[0, pl.ds(kt * t, t), :], v_ref[1, pl.ds(kt * t, t), :])

    _flash_loop(0, i, sel_logits, values(vsel_ref), s_ref, m_ref, acc_ref)
    os_ref[...] = _flash_finish(acc_ref)

    _flash_init(m_ref, acc_ref)
    nwin = WINDOW // t

    def win_logits(kt):
        d = i - kt
        kind = jnp.where(d == nwin, 3, d)
        return _dot_nt(qs, kw_ref[0, 0, pl.ds(kt * t, t), :]) + tb_ref[0, kind]

    _flash_loop(jnp.maximum(i - nwin, 0), i, win_logits, values(vwin_ref), s_ref, m_ref, acc_ref)
    o_w = _flash_finish(acc_ref)
    o_s = os_ref[...]

    gates = gate_ref[0, 0, tile_rows, :]
    outs = []
    for g in range(NSA_GROUP):
        rb = NSA_ROW_ORDER.index(g)
        sl = slice(rb * t, (rb + 1) * t)
        outs.append(gates[:, g:g + 1] * o_c[sl]
                    + gates[:, NSA_GROUP + g:NSA_GROUP + g + 1] * o_s[sl]
                    + gates[:, 2 * NSA_GROUP + g:2 * NSA_GROUP + g + 1] * o_w[sl])
    for j in range(NSA_GROUP // 2):
        o_ref[0, tile_rows, j * LANES:(j + 1) * LANES] = jnp.where(
            lo_half, outs[2 * j], outs[2 * j + 1]).astype(o_ref.dtype)


def _nsa(qkv, cmpkv, gates, bias_c, tbias, ov, et):
    b, _, s, _ = qkv.shape
    ncmp = cmpkv.shape[3]
    t = T_ATT
    rows = NSA_GROUP * t
    qw = NSA_GROUP * HEAD_DIM
    qtiles = qw // LANES
    kv_spec = lambda col: pl.BlockSpec((1, 1, s, LANES), lambda h, bi, col=col: (bi, col + h, 0, 0))
    cmp_spec = lambda kv: pl.BlockSpec((1, 1, 1, ncmp, LANES), lambda h, bi, kv=kv: (bi, kv, h, 0, 0))
    base = QKV_TILES_KV
    return pl.pallas_call(
        functools.partial(_nsa_kernel, seq=s, ncmp=ncmp),
        grid=(NSA_KV_HEADS, b),
        in_specs=[pl.BlockSpec((1, qtiles, s, LANES), lambda h, bi: (bi, h, 0, 0)),
                  kv_spec(base), kv_spec(base + 2), kv_spec(base + 4), kv_spec(base + 6),
                  cmp_spec(0), cmp_spec(1),
                  pl.BlockSpec((1, 1, s, LANES), lambda h, bi: (bi, h, 0, 0)),
                  pl.BlockSpec((1, (s // t) * ncmp, rows), lambda h, bi: (h, 0, 0)),
                  pl.BlockSpec((1, 4, rows, t), lambda h, bi: (h, 0, 0, 0)),
                  pl.BlockSpec((LANES, ncmp), lambda h, bi: (0, 0)),
                  pl.BlockSpec((s, LANES), lambda h, bi: (0, 0))],
        out_specs=pl.BlockSpec((1, s, qw), lambda h, bi: (bi, 0, h)),
        out_shape=jax.ShapeDtypeStruct((b, s, NSA_Q_W), bf16),
        scratch_shapes=[pltpu.VMEM((rows, 2 * LANES), bf16),
                        pltpu.VMEM((s, 2 * LANES), bf16),
                        pltpu.VMEM((rows, LANES), f32),
                        pltpu.VMEM((rows, LANES), f32),
                        pltpu.VMEM((rows, LANES), f32),
                        pltpu.VMEM((2, rows, t), f32),
                        pltpu.VMEM((2, s, LANES), bf16),
                        pltpu.VMEM((2, s, LANES), bf16)],
        compiler_params=_cparams(("parallel", "parallel")),
        name="nsa",
    )(qkv, qkv, qkv, qkv, qkv, cmpkv, cmpkv, gates, bias_c, tbias, ov, et)


def _fox_kernel(q_ref, k_ref, v_ref, qx_ref, kx_ref, o_ref, qaug_ref, kaug_ref, m_ref, acc_ref, s_ref,
                vaug_ref):
    t = T_ATT
    seq = k_ref.shape[2]
    lane = lax.broadcasted_iota(i32, (t, LANES), 1)
    lo_half = lane < HEAD_DIM
    kaug_ref[:, 0:LANES] = k_ref[0, 0]
    kaug_ref[:, LANES:2 * LANES] = kx_ref[0, 0]
    vaug_ref[0], vaug_ref[1] = _ones_values(v_ref[0, 0])

    def logits(kt):
        return _dot_nt(qaug_ref[...], kaug_ref[pl.ds(kt * t, t), :])

    def values(kt):
        return vaug_ref[0, pl.ds(kt * t, t), :], vaug_ref[1, pl.ds(kt * t, t), :]

    def causal(s):
        row = lax.broadcasted_iota(i32, (2 * t, t), 0) & (t - 1)
        return jnp.where(lax.broadcasted_iota(i32, (2 * t, t), 1) <= row, s, NEG_INF)

    def query_tile(i, carry):
        rows = pl.ds(pl.multiple_of(i * t, t), t)
        q = q_ref[0, 0, rows, :].astype(f32)
        qx = qx_ref[0, 0, rows, :].astype(f32)
        qaug_ref[0:t, 0:LANES] = jnp.where(lo_half, q, 0.0).astype(bf16)
        qaug_ref[t:2 * t, 0:LANES] = jnp.where(lo_half, 0.0, q).astype(bf16)
        qaug_ref[0:t, LANES:2 * LANES] = jnp.where(lane < XCOLS, qx, 0.0).astype(bf16)
        qaug_ref[t:2 * t, LANES:2 * LANES] = jnp.where((lane >= XCOLS) & (lane < 2 * XCOLS), qx, 0.0).astype(bf16)
        _flash_init(m_ref, acc_ref)
        _flash_loop(0, i, logits, values, s_ref, m_ref, acc_ref, last_fix=causal)
        o = _flash_finish(acc_ref)
        o_ref[0, rows, :] = jnp.where(lo_half, o[0:t], o[t:2 * t]).astype(o_ref.dtype)
        return carry

    lax.fori_loop(0, seq // t, query_tile, 0)


def _fox(qkv, qx, kx):
    b, _, s, _ = qkv.shape
    t = T_ATT
    base = QKV_TILES_FOX
    return pl.pallas_call(
        _fox_kernel,
        grid=(b, FOX_PAIRS),
        in_specs=[pl.BlockSpec((1, 1, s, LANES), lambda bi, p: (bi, base + p, 0, 0)),
                  pl.BlockSpec((1, 1, s, LANES), lambda bi, p: (bi, base + FOX_PAIRS + p, 0, 0)),
                  pl.BlockSpec((1, 1, s, LANES), lambda bi, p: (bi, base + 2 * FOX_PAIRS + p, 0, 0)),
                  pl.BlockSpec((1, 1, s, LANES), lambda bi, p: (bi, p, 0, 0)),
                  pl.BlockSpec((1, 1, s, LANES), lambda bi, p: (bi, p, 0, 0))],
        out_specs=pl.BlockSpec((1, s, LANES), lambda bi, p: (bi, 0, p)),
        out_shape=jax.ShapeDtypeStruct((b, s, FOX_W), bf16),
        scratch_shapes=[pltpu.VMEM((2 * t, 2 * LANES), bf16),
                        pltpu.VMEM((s, 2 * LANES), bf16),
                        pltpu.VMEM((2 * t, LANES), f32),
                        pltpu.VMEM((2 * t, LANES), f32),
                        pltpu.VMEM((2, 2 * t, t), f32),
                        pltpu.VMEM((2, s, LANES), bf16)],
        compiler_params=_cparams(("parallel", "parallel")),
        name="fox",
    )(qkv, qkv, qkv, qx, kx)


def _merge_kernel(yn_ref, yf_ref, xb_ref, x_ref, wn_ref, wf_ref, wmg_ref, wo_ref, g_ref, b_ref,
                  xo_ref, xbo_ref):
    mg = _dot(xb_ref[...], wmg_ref[...])
    merged = (_sigmoid(mg[:, 0:D_MODEL]) * _dot(yn_ref[...], wn_ref[...])
              + _sigmoid(mg[:, D_MODEL:2 * D_MODEL]) * _dot(yf_ref[...], wf_ref[...]))
    hmix = _dot(merged.astype(bf16), wo_ref[...])
    xn = _layer_norm(DN_ALPHA * x_ref[...] + hmix, g_ref[...], b_ref[...])
    xo_ref[...] = xn
    xbo_ref[...] = xn.astype(bf16)


def _merge(yn, yf, xb, x, wn, wf, wmg, wo, g, bb):
    m = x.shape[0]
    tm = TM_MERGE
    row = lambda w: pl.BlockSpec((tm, w), lambda i: (i, 0))
    full = lambda a: pl.BlockSpec(a.shape, lambda i: (0, 0))
    return pl.pallas_call(
        _merge_kernel,
        grid=(m // tm,),
        in_specs=[row(NSA_Q_W), row(FOX_W), row(D_MODEL), row(D_MODEL),
                  full(wn), full(wf), full(wmg), full(wo), full(g), full(bb)],
        out_specs=[row(D_MODEL), row(D_MODEL)],
        out_shape=[jax.ShapeDtypeStruct((m, D_MODEL), f32), jax.ShapeDtypeStruct((m, D_MODEL), bf16)],
        compiler_params=_cparams(("parallel",)),
        name="merge",
    )(yn, yf, xb, x, wn, wf, wmg, wo, g, bb)


def _ffn_kernel(xb_ref, x_ref, wg_ref, wu_ref, wd_ref, g_ref, b_ref, xo_ref, xbo_ref, acc_ref, *, nf):
    f = pl.program_id(1)

    @pl.when(f == 0)
    def _():
        acc_ref[...] = jnp.zeros_like(acc_ref)

    xb = xb_ref[...]
    gate = _dot(xb, wg_ref[...])
    up = _dot(xb, wu_ref[...])
    act = (gate * _sigmoid(gate) * up).astype(bf16)
    acc_ref[...] += _dot(act, wd_ref[...])

    @pl.when(f == nf - 1)
    def _():
        xn = _layer_norm(DN_ALPHA * x_ref[...] + acc_ref[...], g_ref[...], b_ref[...])
        xo_ref[...] = xn
        xbo_ref[...] = xn.astype(bf16)


def _ffn(xb, x, wg, wu, wd, g, bb):
    m = x.shape[0]
    dff = wg.shape[1]
    tm, tf = TM_FFN, TF_FFN
    nf = dff // tf
    row = pl.BlockSpec((tm, D_MODEL), lambda i, f: (i, 0))
    vec = pl.BlockSpec((1, D_MODEL), lambda i, f: (0, 0))
    return pl.pallas_call(
        functools.partial(_ffn_kernel, nf=nf),
        grid=(m // tm, nf),
        in_specs=[row, row,
                  pl.BlockSpec((D_MODEL, tf), lambda i, f: (0, f)),
                  pl.BlockSpec((D_MODEL, tf), lambda i, f: (0, f)),
                  pl.BlockSpec((tf, D_MODEL), lambda i, f: (f, 0)),
                  vec, vec],
        out_specs=[row, row],
        out_shape=[jax.ShapeDtypeStruct((m, D_MODEL), f32), jax.ShapeDtypeStruct((m, D_MODEL), bf16)],
        scratch_shapes=[pltpu.VMEM((tm, D_MODEL), f32)],
        compiler_params=_cparams(("parallel", "arbitrary")),
        name="ffn",
    )(xb, x, wg, wu, wd, g, bb)


def _router_kernel(x_ref, r_ref, tri_ref, gate_ref, rank_ref, *, seq):
    tb = tri_ref.shape[0]
    x = x_ref[0]
    xh = x.astype(bf16)
    xl = (x - xh.astype(f32)).astype(bf16)
    r = r_ref[...]
    rh = r.astype(bf16)
    rl = (r - rh.astype(f32)).astype(bf16)
    logits = _dot(xh, rh) + _dot(xh, rl) + _dot(xl, rh)
    lane = lax.broadcasted_iota(i32, (seq, LANES), 1).astype(f32)
    low = -3.0e38
    lg = jnp.where(lane < N_EXPERTS, logits, low)
    m1 = jnp.max(lg, axis=-1, keepdims=True)
    i1 = jnp.min(jnp.where(lg == m1, lane, float(LANES)), axis=-1, keepdims=True)
    lg2 = jnp.where(lane == i1, low, lg)
    m2 = jnp.max(lg2, axis=-1, keepdims=True)
    i2 = jnp.min(jnp.where(lg2 == m2, lane, float(LANES)), axis=-1, keepdims=True)
    e2 = jnp.exp(m2 - m1)
    den = 1.0 + e2
    gate_ref[0] = jnp.where(lane == i1, 1.0 / den, jnp.where(lane == i2, e2 / den, 0.0))
    sel = (lane == i1) | (lane == i2)
    selb = jnp.where(sel, 1.0, 0.0).astype(bf16)
    carry = jnp.zeros((1, LANES), f32)
    for blk in range(seq // tb):
        sl = slice(blk * tb, (blk + 1) * tb)
        c = _dot(tri_ref[...], selb[sl]) + carry
        carry = c[tb - 1:tb, :]
        rank_ref[0, sl, :] = jnp.where(sel[sl], c - 1.0, -1.0)


def _router(x3, router_pad, tri):
    b, s, _ = x3.shape
    return pl.pallas_call(
        functools.partial(_router_kernel, seq=s),
        grid=(b,),
        in_specs=[pl.BlockSpec((1, s, D_MODEL), lambda i: (i, 0, 0)),
                  pl.BlockSpec((D_MODEL, LANES), lambda i: (0, 0)),
                  pl.BlockSpec(tri.shape, lambda i: (0, 0))],
        out_specs=[pl.BlockSpec((1, s, LANES), lambda i: (i, 0, 0)),
                   pl.BlockSpec((1, s, LANES), lambda i: (i, 0, 0))],
        out_shape=[jax.ShapeDtypeStruct((b, s, LANES), f32),
                   jax.ShapeDtypeStruct((b, s, LANES), f32)],
        compiler_params=_cparams(("parallel",)),
        name="router",
    )(x3, router_pad, tri)


def _moe_kernel(tot_ref, tot_al_ref, off_ref, npc_ref, xb_ref, rankt_ref, gatet_ref, ranke_ref, wg_ref, wu_ref, wd_ref,
                y_ref, xg_ref, acc_ref, *, seq, nf):
    tr = TR_MOE
    half = tr // 2
    nsubc = seq // SUB_MOE
    c = pl.program_id(0)
    e = pl.program_id(1)
    f = pl.program_id(2)
    ce = c * N_EXPERTS + e
    tot = tot_ref[ce]
    rem = tot % tr
    nfull = tot // tr + jnp.where(rem > half, 1, 0)
    has_tail = (rem > 0) & (rem <= half)
    tail0 = pl.multiple_of(nfull * tr, half)

    def pieces(fn):
        for q in range(nsubc):
            base = off_ref[ce * nsubc + q]

            def body(p, carry, q=q, base=base):
                fn(q, pl.multiple_of(base + p * PIECE_MOE, ROW_ALIGN))
                return carry
            lax.fori_loop(0, npc_ref[ce * nsubc + q], body, 0)

    def onehot_rows(q, r0):
        rk = rankt_ref[0, pl.ds(e, 1), q * SUB_MOE:(q + 1) * SUB_MOE]
        want = (r0 + lax.broadcasted_iota(i32, (PIECE_MOE, SUB_MOE), 0)).astype(f32)
        return rk == want

    def gather(q, r0):
        p = jnp.where(onehot_rows(q, r0), 1.0, 0.0).astype(bf16)
        xg_ref[pl.ds(r0, PIECE_MOE), :] += _dot(p, xb_ref[0, q * SUB_MOE:(q + 1) * SUB_MOE, :]).astype(bf16)

    def clear(r0, nr):
        xg_ref[pl.ds(r0, nr), :] = jnp.zeros((nr, D_MODEL), bf16)
        acc_ref[pl.ds(r0, nr), :] = jnp.zeros((nr, D_MODEL), f32)

    def hidden(r0, nr):
        xs = xg_ref[pl.ds(r0, nr), :]
        gate = _dot(xs, wg_ref[0])
        up = _dot(xs, wu_ref[0])
        act = (gate * _sigmoid(gate) * up).astype(bf16)
        acc_ref[pl.ds(r0, nr), :] += _dot(act, wd_ref[0])

    def combine(q, r0):
        grow = gatet_ref[0, pl.ds(e, 1), q * SUB_MOE:(q + 1) * SUB_MOE]
        w = jnp.sum(jnp.where(onehot_rows(q, r0), grow, 0.0), axis=-1, keepdims=True)
        z = (acc_ref[pl.ds(r0, PIECE_MOE), :] * w).astype(bf16)
        want = (r0 + lax.broadcasted_iota(i32, (SUB_MOE, PIECE_MOE), 1)).astype(f32)
        rc = ranke_ref[0, 0, q * SUB_MOE:(q + 1) * SUB_MOE, :]
        pt = jnp.where(rc == want, 1.0, 0.0).astype(bf16)
        y_ref[0, q * SUB_MOE:(q + 1) * SUB_MOE, :] += _dot(pt, z)

    def tiles(fn):
        def body(s, carry):
            fn(pl.multiple_of(s * tr, tr), tr)
            return carry
        lax.fori_loop(0, nfull, body, 0)

        @pl.when(has_tail)
        def _():
            fn(tail0, half)

    def zero(j, carry):
        y_ref[0, pl.ds(j * tr, tr), :] = jnp.zeros((tr, D_MODEL), f32)
        return carry

    lax.fori_loop(0, jnp.where((e == 0) & (f == 0), seq // tr, 0), zero, 0)

    @pl.when(f == 0)
    def _():
        tiles(clear)
        clear(pl.multiple_of(tot_al_ref[ce], ROW_ALIGN), tr)
        pieces(gather)

    tiles(hidden)

    @pl.when(f == nf - 1)
    def _():
        pieces(combine)


def _moe(xb3, gate, rank, wg, wu, wd):
    b, s, _ = xb3.shape
    dff = wg.shape[2]
    nf = dff // TF_MOE
    nsubc = s // SUB_MOE
    rk = rank[:, :, :N_EXPERTS]
    cnt = jnp.sum((rk >= 0.0).reshape(b, nsubc, SUB_MOE, N_EXPERTS), axis=2).astype(i32)
    first = jnp.cumsum(cnt, axis=1) - cnt
    off = first // ROW_ALIGN * ROW_ALIGN
    npc = jnp.where(cnt > 0, (first - off + cnt + PIECE_MOE - 1) // PIECE_MOE, 0)
    post = rk.transpose(0, 2, 1)
    gatet = gate[:, :, :N_EXPERTS].transpose(0, 2, 1)
    tot = jnp.sum(cnt, axis=1).reshape(-1)
    tot_al = (tot + ROW_ALIGN - 1) // ROW_ALIGN * ROW_ALIGN
    off_flat = off.transpose(0, 2, 1).reshape(-1)
    npc = npc.transpose(0, 2, 1).reshape(-1)
    buf_rows = s + ROW_ALIGN + PIECE_MOE + TR_MOE
    grid_spec = pltpu.PrefetchScalarGridSpec(
        num_scalar_prefetch=4,
        grid=(b, N_EXPERTS, nf),
        in_specs=[pl.BlockSpec((1, s, D_MODEL), lambda c, e, f, *_: (c, 0, 0)),
                  pl.BlockSpec((1, N_EXPERTS, s), lambda c, e, f, *_: (c, 0, 0)),
                  pl.BlockSpec((1, N_EXPERTS, s), lambda c, e, f, *_: (c, 0, 0)),
                  pl.BlockSpec((1, 1, s, 1), lambda c, e, f, *_: (c, e, 0, 0)),
                  pl.BlockSpec((1, D_MODEL, TF_MOE), lambda c, e, f, *_: (e, 0, f)),
                  pl.BlockSpec((1, D_MODEL, TF_MOE), lambda c, e, f, *_: (e, 0, f)),
                  pl.BlockSpec((1, TF_MOE, D_MODEL), lambda c, e, f, *_: (e, f, 0))],
        out_specs=pl.BlockSpec((1, s, D_MODEL), lambda c, e, f, *_: (c, 0, 0)),
        scratch_shapes=[pltpu.VMEM((buf_rows, D_MODEL), bf16), pltpu.VMEM((buf_rows, D_MODEL), f32)],
    )
    return pl.pallas_call(
        functools.partial(_moe_kernel, seq=s, nf=nf),
        grid_spec=grid_spec,
        out_shape=jax.ShapeDtypeStruct((b, s, D_MODEL), f32),
        compiler_params=_cparams(("parallel", "arbitrary", "arbitrary")),
        name="moe",
    )(tot, tot_al, off_flat, npc, xb3, post, gatet, post[..., None], wg, wu, wd)


def _resln_kernel(x_ref, y_ref, g_ref, b_ref, xo_ref, xbo_ref):
    xn = _layer_norm(DN_ALPHA * x_ref[...] + y_ref[...], g_ref[...], b_ref[...])
    xo_ref[...] = xn
    xbo_ref[...] = xn.astype(bf16)


def _resln(x, y, g, bb):
    m = x.shape[0]
    tm = TM_FFN
    row = pl.BlockSpec((tm, D_MODEL), lambda i: (i, 0))
    vec = pl.BlockSpec((1, D_MODEL), lambda i: (0, 0))
    return pl.pallas_call(
        _resln_kernel,
        grid=(m // tm,),
        in_specs=[row, row, vec, vec],
        out_specs=[row, row],
        out_shape=[jax.ShapeDtypeStruct((m, D_MODEL), f32), jax.ShapeDtypeStruct((m, D_MODEL), bf16)],
        compiler_params=_cparams(("parallel",)),
        name="resln",
    )(x, y, g, bb)


def _t5_bucket(dist):
    n = jnp.maximum(dist, 0)
    max_exact = REL_BUCKETS // 2
    large = max_exact + (jnp.log(jnp.maximum(n, 1).astype(f32) / max_exact)
                         / math.log(REL_MAX_DIST / max_exact) * (REL_BUCKETS - max_exact)).astype(i32)
    large = jnp.minimum(large, REL_BUCKETS - 1)
    return jnp.where(n < max_exact, n, large)


def _bias_of_dist(rel_bias, dist):
    onehot = (_t5_bucket(dist)[None] == jnp.arange(REL_BUCKETS).reshape((-1,) + (1,) * dist.ndim)).astype(f32)
    return LOG2E * jnp.einsum("kh,k...->h...", rel_bias.astype(f32), onehot, precision=lax.Precision.HIGHEST)


def _bias_tables(rel_bias, seq):
    t = T_ATT
    rows = NSA_GROUP * t
    ncmp = seq // CMP_STRIDE
    d0 = jnp.arange(t)[:, None] - jnp.arange(t)[None, :]
    offs = jnp.array([0, t, 2 * t, WINDOW]).reshape(4, 1, 1)
    kinds = _bias_of_dist(rel_bias, offs + d0[None])
    mask = jnp.stack([d0 >= 0, d0 == d0, d0 == d0, d0 < 0])
    kinds = jnp.where(mask[None], kinds, NEG_INF)
    order = np.array(NSA_ROW_ORDER)
    tbias = kinds.reshape(NSA_KV_HEADS, NSA_GROUP, 4, t, t)[:, order].transpose(0, 2, 1, 3, 4)
    tbias = tbias.reshape(NSA_KV_HEADS, 4, rows, t)
    cend = jnp.arange(ncmp) * CMP_STRIDE + CMP_BLOCK - 1
    bc = _bias_of_dist(rel_bias, jnp.arange(seq)[:, None] - cend[None, :])
    bc = bc.reshape(NSA_KV_HEADS, NSA_GROUP, seq // t, t, ncmp)[:, order].transpose(0, 2, 4, 1, 3)
    return tbias, bc.reshape(NSA_KV_HEADS, (seq // t) * ncmp, rows)


def _selection_constants(seq):
    ncmp = seq // CMP_STRIDE
    nsel = seq // SEL_BLOCK
    c0 = np.arange(ncmp)[:, None] * CMP_STRIDE
    s0 = np.arange(LANES)[None, :] * SEL_BLOCK
    ov = np.maximum(np.minimum(c0 + CMP_BLOCK, s0 + SEL_BLOCK) - np.maximum(c0, s0), 0) / CMP_BLOCK
    ov[ncmp - 1, :] = 0.0
    ov[:, nsel:] = 0.0
    et = (np.arange(seq)[:, None] // SEL_BLOCK == np.arange(LANES)[None, :]).astype(np.float32)
    return jnp.asarray(ov.T, bf16), jnp.asarray(et, bf16)


def _fox_placement():
    xw = FOX_PAIRS * LANES
    pq = np.zeros((3, LANES, xw), np.float32)
    pk = np.zeros((3, LANES, xw), np.float32)
    oq = np.zeros((1, xw), np.float32)
    ok = np.zeros((1, xw), np.float32)
    for p in range(FOX_PAIRS):
        for hh in range(2):
            src = FGATE_LANE + 2 * p + hh
            base = p * LANES + hh * XCOLS
            for part in range(3):
                pk[part, src, base + part] = -1.0
                pq[part, src, base + 3 + part] = 1.0
                oq[0, base + part] = 1.0
                ok[0, base + 3 + part] = 1.0
    return jnp.asarray(pq, bf16), jnp.asarray(pk, bf16), jnp.asarray(oq), jnp.asarray(ok)


def _layer_weights(w_in, layer_pe, w1, w2, f_bias):
    offs = np.cumsum((NSA_Q_W, 6 * 2 * HEAD_DIM, 3 * NSA_HEADS, 3 * FOX_W, FOX_HEADS, 2 * D_MODEL))
    kv0, g0, fx0, ff0, mg0 = offs[0], offs[1], offs[2], offs[3], offs[4]
    scale = HEAD_DIM ** -0.5 * LOG2E
    kvw = NSA_KV_HEADS * HEAD_DIM
    w_kv = w_in[:, kv0:g0]
    w_kvdup = jnp.repeat(w_kv[:, 2 * kvw:].reshape(D_MODEL, 4 * NSA_KV_HEADS, 1, HEAD_DIM), 2, axis=2)
    w_kvdup = w_kvdup.reshape(D_MODEL, 8 * kvw)
    w_qkv = jnp.concatenate([w_in[:, :NSA_Q_W] * scale, w_kvdup, w_in[:, fx0:fx0 + FOX_W] * scale,
                             w_in[:, fx0 + FOX_W:ff0]], axis=1).astype(bf16)
    zeros = lambda n: jnp.zeros((D_MODEL, n), w_in.dtype)
    ng = 3 * NSA_GROUP
    w_g = w_in[:, g0:fx0].reshape(D_MODEL, NSA_KV_HEADS, NSA_GROUP, 3).transpose(0, 1, 3, 2)
    gate_cols = lambda h: w_g[:, h].reshape(D_MODEL, ng)
    w_aux = jnp.concatenate([w_kv[:, :2 * kvw],
                             gate_cols(0), zeros(FGATE_LANE - ng), w_in[:, ff0:mg0],
                             zeros(LANES - FGATE_LANE - FOX_HEADS),
                             gate_cols(1), zeros(LANES - ng)], axis=1).astype(bf16)
    fb_row = jnp.zeros((1, LANES), f32).at[0, FGATE_LANE:FGATE_LANE + FOX_HEADS].set(f_bias.astype(f32))
    pe2 = jnp.tile(layer_pe.astype(f32), (1, 1, NSA_KV_HEADS))
    w1r = w1.reshape(2, CMP_BLOCK, HEAD_DIM, CMP_HIDDEN).astype(bf16)
    zero = jnp.zeros_like(w1r)
    w1bd = jnp.concatenate([jnp.concatenate([w1r, zero], axis=-1),
                            jnp.concatenate([zero, w1r], axis=-1)], axis=-2)
    w2d = jnp.concatenate([w2, w2], axis=-1).astype(bf16)
    return (w_qkv, w_aux, w_in[:, mg0:].astype(bf16), fb_row, pe2, w1bd[:, :CMP_STRIDE], w1bd[:, CMP_STRIDE:],
            w2d)


def kernel(x, w_in, nsa_cmp_pe, nsa_cmp_w1, nsa_cmp_w2, fox_f_bias, w_nsa_branch, w_fox_branch, w_out,
           rel_bias, ln1_g, ln1_b, ln2_g, ln2_b, dense_w_gate, dense_w_up, dense_w_down, moe_router,
           moe_w_gate, moe_w_up, moe_w_down):
    b, s, d = x.shape
    m = b * s
    tbias, bias_c = _bias_tables(rel_bias, s)
    ov, et = _selection_constants(s)
    pq, pk, oq, ok = _fox_placement()
    tri128 = jnp.asarray(np.tril(np.ones((LANES, LANES), np.float32)), bf16)
    tri256 = jnp.asarray(np.tril(np.ones((256, 256), np.float32)), bf16)

    xf = x.reshape(m, d).astype(f32)
    xb = xf.astype(bf16)
    for layer in range(DEPTH):
        w_qkv, w_aux, w_mg, fb_row, pe2, w1t, w1b, w2d = _layer_weights(
            w_in[layer], nsa_cmp_pe[layer], nsa_cmp_w1[layer], nsa_cmp_w2[layer], fox_f_bias[layer])
        qkv, aux = _proj(xb, w_qkv, w_aux, b)
        aux = aux.reshape(b, s, AUX_W)
        gates, qx, kx = _gateprep(aux, fb_row, tri128, pq, pk, oq, ok)
        cmpkv = _compress(aux, pe2, w1t, w1b, w2d)
        y_nsa = _nsa(qkv, cmpkv, gates, bias_c, tbias, ov, et).reshape(m, NSA_Q_W)
        y_fox = _fox(qkv, qx, kx).reshape(m, FOX_W)
        xf, xb = _merge(y_nsa, y_fox, xb, xf, w_nsa_branch[layer].astype(bf16),
                        w_fox_branch[layer].astype(bf16), w_mg, w_out[layer].astype(bf16),
                        ln1_g[layer].reshape(1, d), ln1_b[layer].reshape(1, d))
        j = layer // 2
        g2, b2 = ln2_g[layer].reshape(1, d), ln2_b[layer].reshape(1, d)
        if layer % 2 == 0:
            xf, xb = _ffn(xb, xf, dense_w_gate[j].astype(bf16), dense_w_up[j].astype(bf16),
                          dense_w_down[j].astype(bf16), g2, b2)
        else:
            router_pad = jnp.zeros((d, LANES), f32).at[:, :N_EXPERTS].set(moe_router[j].astype(f32))
            gate, rank = _router(xf.reshape(b, s, d), router_pad, tri256)
            y = _moe(xb.reshape(b, s, d), gate, rank,
                     moe_w_gate[j].astype(bf16), moe_w_up[j].astype(bf16), moe_w_down[j].astype(bf16))
            xf, xb = _resln(xf, y.reshape(m, d), g2, b2)
    return xf.reshape(b, s, d).astype(x.dtype)
```

```python
import functools
import math

import numpy as np
import jax
import jax.numpy as jnp
from jax import lax
from jax.experimental import pallas as pl
from jax.experimental.pallas import tpu as pltpu

f32 = jnp.float32
bf16 = jnp.bfloat16
i32 = jnp.int32

D_MODEL = 1024
HEAD_DIM = 64
LANES = 128
NSA_HEADS = 8
NSA_KV_HEADS = 2
NSA_GROUP = NSA_HEADS // NSA_KV_HEADS
NSA_ROW_ORDER = (0, 2, 1, 3)
FOX_HEADS = 8
FOX_PAIRS = FOX_HEADS // 2
CMP_BLOCK = 32
CMP_STRIDE = 16
CMP_HIDDEN = 128
SEL_BLOCK = 64
N_SELECT = 16
N_LOCAL_SEL = 2
WINDOW = 512
REL_BUCKETS = 32
REL_MAX_DIST = 128
N_EXPERTS = 8
DEPTH = 4
DN_ALPHA = (2 * DEPTH) ** 0.25
LN_EPS = 1e-5
FORCED_SCORE = 1e4
NEG_INF = -1e30
LOG2E = math.log2(math.e)

NSA_Q_W = NSA_HEADS * HEAD_DIM
FOX_W = FOX_HEADS * HEAD_DIM
QKV_TILES_KV = NSA_Q_W // LANES
QKV_TILES_FOX = QKV_TILES_KV + 4 * NSA_KV_HEADS
QKV_W = (QKV_TILES_FOX + 3 * FOX_PAIRS) * LANES
AUX_W = 4 * LANES
FGATE_LANE = 24
XCOLS = 6

T_ATT = 256
TM_PROJ = 512
TM_MERGE = 512
TM_FFN = 512
TF_FFN = 1408
TR_MOE = 256
SUB_MOE = 512
PIECE_MOE = 192
ROW_ALIGN = 16
TF_MOE = 896
VMEM_LIMIT = 56 * 1024 * 1024


def _cparams(sem):
    return pltpu.CompilerParams(dimension_semantics=sem, vmem_limit_bytes=VMEM_LIMIT)


def _dot(a, b):
    return jnp.dot(a, b, preferred_element_type=f32)


def _dot_nt(a, b):
    return lax.dot_general(a, b, (((1,), (1,)), ((), ())), preferred_element_type=f32)


def _sigmoid(x):
    return 1.0 / (1.0 + jnp.exp(-x))


def _layer_norm(z, g, b):
    mu = jnp.mean(z, axis=-1, keepdims=True)
    zc = z - mu
    var = jnp.mean(zc * zc, axis=-1, keepdims=True)
    return zc * lax.rsqrt(var + LN_EPS) * g + b


def _split3(x):
    hi = x.astype(bf16)
    r1 = x - hi.astype(f32)
    mid = r1.astype(bf16)
    lo = (r1 - mid.astype(f32)).astype(bf16)
    return hi, mid, lo


def _proj_kernel(x_ref, w_ref, wa_ref, o_ref, oa_ref):
    x = x_ref[...]
    res = _dot(x, w_ref[...])
    for j in range(o_ref.shape[1]):
        o_ref[0, j] = res[:, j * LANES:(j + 1) * LANES].astype(o_ref.dtype)
    oa_ref[...] = _dot(x, wa_ref[...])


def _proj(xb, w, wa, batch):
    m, k = xb.shape
    n, na = w.shape[1], wa.shape[1]
    nb = m // batch // TM_PROJ
    return pl.pallas_call(
        _proj_kernel,
        grid=(m // TM_PROJ,),
        in_specs=[pl.BlockSpec((TM_PROJ, k), lambda i: (i, 0)),
                  pl.BlockSpec((k, n), lambda i: (0, 0)),
                  pl.BlockSpec((k, na), lambda i: (0, 0))],
        out_specs=[pl.BlockSpec((1, n // LANES, TM_PROJ, LANES), lambda i: (i // nb, 0, i % nb, 0)),
                   pl.BlockSpec((TM_PROJ, na), lambda i: (i, 0))],
        out_shape=[jax.ShapeDtypeStruct((batch, n // LANES, m // batch, LANES), bf16),
                   jax.ShapeDtypeStruct((m, na), f32)],
        compiler_params=_cparams(("parallel",)),
        name="proj",
    )(xb, w, wa)


def _gateprep_kernel(a_ref, fb_ref, tri_ref, pq_ref, pk_ref, oq_ref, ok_ref, g_ref, qx_ref, kx_ref, *, seq):
    tb = LANES
    tri = tri_ref[...]
    local = []
    for blk in range(seq // tb):
        sl = slice(blk * tb, (blk + 1) * tb)
        va = a_ref[0, sl, 0:LANES]
        g_ref[0, 0, sl, :] = _sigmoid(va)
        g_ref[0, 1, sl, :] = _sigmoid(a_ref[0, sl, LANES:2 * LANES])
        z = va + fb_ref[...]
        logf = jnp.minimum(z, 0.0) - jnp.log1p(jnp.exp(-jnp.abs(z)))
        hi, mid, lo = _split3(logf)
        local.append(_dot(tri, hi) + _dot(tri, mid) + _dot(tri, lo))
    carry = jnp.zeros((1, LANES), f32)
    for blk in range(seq // tb):
        sl = slice(blk * tb, (blk + 1) * tb)
        c = local[blk] + carry
        carry = c[tb - 1:tb, :]
        chi, cmid, clo = _split3(c * LOG2E)
        qx = _dot(chi, pq_ref[0]) + _dot(cmid, pq_ref[1]) + _dot(clo, pq_ref[2]) + oq_ref[...]
        kx = _dot(chi, pk_ref[0]) + _dot(cmid, pk_ref[1]) + _dot(clo, pk_ref[2]) + ok_ref[...]
        for p in range(FOX_PAIRS):
            qx_ref[0, p, sl, :] = qx[:, p * LANES:(p + 1) * LANES].astype(bf16)
            kx_ref[0, p, sl, :] = kx[:, p * LANES:(p + 1) * LANES].astype(bf16)


def _gateprep(aux, fb_row, tri, pq, pk, oq, ok):
    b, s, _ = aux.shape
    const2 = lambda a: pl.BlockSpec(a.shape, lambda i: (0, 0))
    const3 = lambda a: pl.BlockSpec(a.shape, lambda i: (0, 0, 0))
    return pl.pallas_call(
        functools.partial(_gateprep_kernel, seq=s),
        grid=(b,),
        in_specs=[pl.BlockSpec((1, s, 2 * LANES), lambda i: (i, 0, 1)),
                  const2(fb_row), const2(tri), const3(pq), const3(pk), const2(oq), const2(ok)],
        out_specs=[pl.BlockSpec((1, 2, s, LANES), lambda i: (i, 0, 0, 0)),
                   pl.BlockSpec((1, FOX_PAIRS, s, LANES), lambda i: (i, 0, 0, 0)),
                   pl.BlockSpec((1, FOX_PAIRS, s, LANES), lambda i: (i, 0, 0, 0))],
        out_shape=[jax.ShapeDtypeStruct((b, 2, s, LANES), f32),
                   jax.ShapeDtypeStruct((b, FOX_PAIRS, s, LANES), bf16),
                   jax.ShapeDtypeStruct((b, FOX_PAIRS, s, LANES), bf16)],
        compiler_params=_cparams(("parallel",)),
        name="gateprep",
    )(aux, fb_row, tri, pq, pk, oq, ok)


def _gelu_tanh(x):
    c = math.sqrt(2.0 / math.pi)
    return x * (0.5 * (1.0 + jnp.tanh(c * (x + 0.044715 * (x * x * x)))))


def _compress_kernel(a_ref, pe_ref, w1t_ref, w1b_ref, w2_ref, o_ref, *, nhalf):
    top = jnp.zeros((nhalf, NSA_KV_HEADS * CMP_HIDDEN), f32)
    bot = jnp.zeros((nhalf, NSA_KV_HEADS * CMP_HIDDEN), f32)
    for l in range(CMP_STRIDE):
        rows = a_ref[0, pl.ds(l, nhalf, stride=CMP_STRIDE), :]
        top = top + _dot((rows + pe_ref[0, l:l + 1, :]).astype(bf16), w1t_ref[0, l])
        bot = bot + _dot((rows + pe_ref[0, CMP_STRIDE + l:CMP_STRIDE + l + 1, :]).astype(bf16), w1b_ref[0, l])
    pre = top + pltpu.roll(bot, nhalf - 1, 0)
    act = _gelu_tanh(pre).astype(bf16)
    for h in range(NSA_KV_HEADS):
        o_ref[0, 0, h] = _dot(act[:, h * CMP_HIDDEN:(h + 1) * CMP_HIDDEN], w2_ref[0]).astype(o_ref.dtype)


def _compress(aux, pe2, w1t, w1b, w2d):
    b, s, _ = aux.shape
    nhalf = s // CMP_STRIDE
    w1_spec = pl.BlockSpec((1,) + w1t.shape[1:], lambda i, j: (j, 0, 0, 0))
    return pl.pallas_call(
        functools.partial(_compress_kernel, nhalf=nhalf),
        grid=(b, 2),
        in_specs=[pl.BlockSpec((1, s, LANES), lambda i, j: (i, 0, j)),
                  pl.BlockSpec((1, CMP_BLOCK, LANES), lambda i, j: (j, 0, 0)),
                  w1_spec, w1_spec,
                  pl.BlockSpec((1, CMP_HIDDEN, LANES), lambda i, j: (j, 0, 0))],
        out_specs=pl.BlockSpec((1, 1, NSA_KV_HEADS, nhalf, LANES), lambda i, j: (i, j, 0, 0, 0)),
        out_shape=jax.ShapeDtypeStruct((b, 2, NSA_KV_HEADS, nhalf, LANES), bf16),
        compiler_params=_cparams(("parallel", "parallel")),
        name="compress",
    )(aux, pe2, w1t, w1b, w2d)


def _flash_init(m_ref, acc_ref):
    m_ref[...] = jnp.full(m_ref.shape, NEG_INF, f32)
    acc_ref[...] = jnp.zeros(acc_ref.shape, f32)


def _ones_values(v):
    lo_half = lax.broadcasted_iota(i32, v.shape, 1) < HEAD_DIM
    vf = v.astype(f32)
    return jnp.where(lo_half, vf, 1.0).astype(bf16), jnp.where(lo_half, 1.0, vf).astype(bf16)


def _flash_step(s, v_lo, v_hi, m_ref, acc_ref):
    nk = s.shape[1] // LANES
    half = s.shape[0] // 2
    cols = [s[:, c * LANES:(c + 1) * LANES] for c in range(nk)]
    mx = cols[0]
    for c in cols[1:]:
        mx = jnp.maximum(mx, c)
    m_old = m_ref[...]
    m_new = jnp.maximum(m_old, jnp.broadcast_to(jnp.max(mx, axis=-1, keepdims=True), m_old.shape))
    alpha = jnp.exp2(m_old - m_new)
    p = jnp.concatenate([jnp.exp2(c - m_new).astype(bf16) for c in cols], axis=1)
    pv = jnp.concatenate([_dot(p[0:half], v_lo), _dot(p[half:], v_hi)], axis=0)
    acc_ref[...] = alpha * acc_ref[...] + pv
    m_ref[...] = m_new


def _flash_loop(first, last, logits_fn, v_fn, s_ref, m_ref, acc_ref, last_fix=lambda s: s):
    sa, sb = s_ref.at[0], s_ref.at[1]
    n = last - first + 1
    pairs = (n - 1) // 2

    def step(buf, kt, fix=lambda s: s):
        _flash_step(fix(buf[...]), *v_fn(kt), m_ref, acc_ref)

    sa[...] = logits_fn(first)

    def body(j, carry):
        kt = first + 2 * j
        sb[...] = logits_fn(kt + 1)
        step(sa, kt)
        sa[...] = logits_fn(kt + 2)
        step(sb, kt + 1)
        return carry

    lax.fori_loop(0, pairs, body, 0)
    two_left = n - 2 * pairs == 2

    @pl.when(two_left)
    def _():
        sb[...] = logits_fn(last)
        step(sa, last - 1)
        step(sb, last, last_fix)

    @pl.when(jnp.logical_not(two_left))
    def _():
        step(sa, last, last_fix)


def _flash_finish(acc_ref):
    acc = acc_ref[...]
    return acc * (1.0 / jnp.maximum(pltpu.roll(acc, HEAD_DIM, 1), 1e-30))


def _nsa_kernel(q_ref, ks_ref, vs_ref, kw_ref, vw_ref, kc_ref, vc_ref, gate_ref, bc_ref, tb_ref, ov_ref,
                et_ref, o_ref, qaug_ref, kaug_ref, m_ref, acc_ref, os_ref, s_ref, vsel_ref, vwin_ref, *,
                seq, ncmp):
    kaug_ref[:, 0:LANES] = ks_ref[0, 0]
    kaug_ref[:, LANES:2 * LANES] = et_ref[...]
    vsel_ref[0], vsel_ref[1] = _ones_values(vs_ref[0, 0])
    vwin_ref[0], vwin_ref[1] = _ones_values(vw_ref[0, 0])

    def query_tile(i, carry):
        _nsa_tile(i, q_ref, kw_ref, kc_ref, vc_ref, gate_ref, bc_ref, tb_ref, ov_ref, o_ref, qaug_ref,
                  kaug_ref, m_ref, acc_ref, os_ref, s_ref, vsel_ref, vwin_ref, seq=seq, ncmp=ncmp)
        return carry

    lax.fori_loop(0, seq // T_ATT, query_tile, 0)


def _nsa_tile(i, q_ref, kw_ref, kc_ref, vc_ref, gate_ref, bc_ref, tb_ref, ov_ref, o_ref, qaug_ref,
              kaug_ref, m_ref, acc_ref, os_ref, s_ref, vsel_ref, vwin_ref, *, seq, ncmp):
    t = T_ATT
    rows = NSA_GROUP * t
    nsel = seq // SEL_BLOCK
    topn = min(N_SELECT, nsel)
    t0 = i * t
    tile_rows = pl.ds(pl.multiple_of(t0, t), t)
    lane = lax.broadcasted_iota(i32, (t, LANES), 1)
    lo_half = lane < HEAD_DIM

    for rb, g in enumerate(NSA_ROW_ORDER):
        qg = q_ref[0, g // 2, tile_rows, :].astype(f32)
        keep = lo_half if g % 2 == 0 else jnp.logical_not(lo_half)
        qaug_ref[rb * t:(rb + 1) * t, 0:LANES] = jnp.where(keep, qg, 0.0).astype(bf16)
    qs = qaug_ref[:, 0:LANES]

    s = _dot_nt(kc_ref[0, 0, 0], qs) + bc_ref[0, pl.ds(pl.multiple_of(i * ncmp, ncmp), ncmp), :]
    row_t = t0 + (lax.broadcasted_iota(i32, (ncmp, rows), 1) & (t - 1))
    cmp_end = lax.broadcasted_iota(i32, (ncmp, rows), 0) * CMP_STRIDE + (CMP_BLOCK - 1)
    vis = cmp_end <= row_t
    s = jnp.where(vis, s, NEG_INF)
    s = s - jnp.max(s, axis=0, keepdims=True)
    e = jnp.where(vis, jnp.exp2(s), 0.0)
    p_c = e * (1.0 / jnp.maximum(jnp.sum(e, axis=0, keepdims=True), 1e-30))
    o_c = _dot(p_c.T.astype(bf16), vc_ref[0, 0, 0])

    psum = p_c[:, 0:t]
    for g in range(1, NSA_GROUP):
        psum = psum + p_c[:, g * t:(g + 1) * t]
    hi = psum.astype(bf16)
    lo = (psum - hi.astype(f32)).astype(bf16)
    imp = (_dot(ov_ref[...], hi) + _dot(ov_ref[...], lo))[0:nsel]
    blk = lax.broadcasted_iota(i32, (nsel, t), 0)
    cur = (t0 + lax.broadcasted_iota(i32, (nsel, t), 1)) >> 6
    forced = (blk == 0) | ((blk <= cur) & (blk > cur - N_LOCAL_SEL))
    val = jnp.where(forced, FORCED_SCORE, jnp.where(blk <= cur, imp, -1.0))
    cnt = jnp.zeros((nsel, t), f32)
    for j in range(nsel):
        vj = val[j:j + 1, :]
        beats = (vj > val) | ((vj == val) & (blk > j))
        cnt = cnt + jnp.where(beats, 1.0, 0.0)
    mneg = jnp.where(cnt < topn, 0.0, NEG_INF)
    mneg = jnp.concatenate([mneg, jnp.zeros((LANES - nsel, t), f32)], axis=0).T.astype(bf16)
    for g in range(NSA_GROUP):
        qaug_ref[g * t:(g + 1) * t, LANES:2 * LANES] = mneg

    _flash_init(m_ref, acc_ref)

    def sel_logits(kt):
        k = kaug_ref[pl.ds(kt * t, t), :]
        return _dot_nt(qaug_ref[...], k) + tb_ref[0, jnp.minimum(i - kt, 2)]

    def values(v_ref):
        return lambda kt: (v_ref[0, pl.ds(kt * t, t), :], v_ref[1, pl.ds(kt * t, t), :])

    _flash_loop(0, i, sel_logits, values(vsel_ref), s_ref, m_ref, acc_ref)
    os_ref[...] = _flash_finish(acc_ref)

    _flash_init(m_ref, acc_ref)
    nwin = WINDOW // t

    def win_logits(kt):
        d = i - kt
        kind = jnp.where(d == nwin, 3, d)
        return _dot_nt(qs, kw_ref[0, 0, pl.ds(kt * t, t), :]) + tb_ref[0, kind]

    _flash_loop(jnp.maximum(i - nwin, 0), i, win_logits, values(vwin_ref), s_ref, m_ref, acc_ref)
    o_w = _flash_finish(acc_ref)
    o_s = os_ref[...]

    gates = gate_ref[0, 0, tile_rows, :]
    outs = []
    for g in range(NSA_GROUP):
        rb = NSA_ROW_ORDER.index(g)
        sl = slice(rb * t, (rb + 1) * t)
        outs.append(gates[:, g:g + 1] * o_c[sl]
                    + gates[:, NSA_GROUP + g:NSA_GROUP + g + 1] * o_s[sl]
                    + gates[:, 2 * NSA_GROUP + g:2 * NSA_GROUP + g + 1] * o_w[sl])
    for j in range(NSA_GROUP // 2):
        o_ref[0, tile_rows, j * LANES:(j + 1) * LANES] = jnp.where(
            lo_half, outs[2 * j], outs[2 * j + 1]).astype(o_ref.dtype)


def _nsa(qkv, cmpkv, gates, bias_c, tbias, ov, et):
    b, _, s, _ = qkv.shape
    ncmp = cmpkv.shape[3]
    t = T_ATT
    rows = NSA_GROUP * t
    qw = NSA_GROUP * HEAD_DIM
    qtiles = qw // LANES
    kv_spec = lambda col: pl.BlockSpec((1, 1, s, LANES), lambda h, bi, col=col: (bi, col + h, 0, 0))
    cmp_spec = lambda kv: pl.BlockSpec((1, 1, 1, ncmp, LANES), lambda h, bi, kv=kv: (bi, kv, h, 0, 0))
    base = QKV_TILES_KV
    return pl.pallas_call(
        functools.partial(_nsa_kernel, seq=s, ncmp=ncmp),
        grid=(NSA_KV_HEADS, b),
        in_specs=[pl.BlockSpec((1, qtiles, s, LANES), lambda h, bi: (bi, h, 0, 0)),
                  kv_spec(base), kv_spec(base + 2), kv_spec(base + 4), kv_spec(base + 6),
                  cmp_spec(0), cmp_spec(1),
                  pl.BlockSpec((1, 1, s, LANES), lambda h, bi: (bi, h, 0, 0)),
                  pl.BlockSpec((1, (s // t) * ncmp, rows), lambda h, bi: (h, 0, 0)),
                  pl.BlockSpec((1, 4, rows, t), lambda h, bi: (h, 0, 0, 0)),
                  pl.BlockSpec((LANES, ncmp), lambda h, bi: (0, 0)),
                  pl.BlockSpec((s, LANES), lambda h, bi: (0, 0))],
        out_specs=pl.BlockSpec((1, s, qw), lambda h, bi: (bi, 0, h)),
        out_shape=jax.ShapeDtypeStruct((b, s, NSA_Q_W), bf16),
        scratch_shapes=[pltpu.VMEM((rows, 2 * LANES), bf16),
                        pltpu.VMEM((s, 2 * LANES), bf16),
                        pltpu.VMEM((rows, LANES), f32),
                        pltpu.VMEM((rows, LANES), f32),
                        pltpu.VMEM((rows, LANES), f32),
                        pltpu.VMEM((2, rows, t), f32),
                        pltpu.VMEM((2, s, LANES), bf16),
                        pltpu.VMEM((2, s, LANES), bf16)],
        compiler_params=_cparams(("parallel", "parallel")),
        name="nsa",
    )(qkv, qkv, qkv, qkv, qkv, cmpkv, cmpkv, gates, bias_c, tbias, ov, et)


def _fox_kernel(q_ref, k_ref, v_ref, qx_ref, kx_ref, o_ref, qaug_ref, kaug_ref, m_ref, acc_ref, s_ref,
                vaug_ref):
    t = T_ATT
    seq = k_ref.shape[2]
    lane = lax.broadcasted_iota(i32, (t, LANES), 1)
    lo_half = lane < HEAD_DIM
    kaug_ref[:, 0:LANES] = k_ref[0, 0]
    kaug_ref[:, LANES:2 * LANES] = kx_ref[0, 0]
    vaug_ref[0], vaug_ref[1] = _ones_values(v_ref[0, 0])

    def logits(kt):
        return _dot_nt(qaug_ref[...], kaug_ref[pl.ds(kt * t, t), :])

    def values(kt):
        return vaug_ref[0, pl.ds(kt * t, t), :], vaug_ref[1, pl.ds(kt * t, t), :]

    def causal(s):
        row = lax.broadcasted_iota(i32, (2 * t, t), 0) & (t - 1)
        return jnp.where(lax.broadcasted_iota(i32, (2 * t, t), 1) <= row, s, NEG_INF)

    def query_tile(i, carry):
        rows = pl.ds(pl.multiple_of(i * t, t), t)
        q = q_ref[0, 0, rows, :].astype(f32)
        qx = qx_ref[0, 0, rows, :].astype(f32)
        qaug_ref[0:t, 0:LANES] = jnp.where(lo_half, q, 0.0).astype(bf16)
        qaug_ref[t:2 * t, 0:LANES] = jnp.where(lo_half, 0.0, q).astype(bf16)
        qaug_ref[0:t, LANES:2 * LANES] = jnp.where(lane < XCOLS, qx, 0.0).astype(bf16)
        qaug_ref[t:2 * t, LANES:2 * LANES] = jnp.where((lane >= XCOLS) & (lane < 2 * XCOLS), qx, 0.0).astype(bf16)
        _flash_init(m_ref, acc_ref)
        _flash_loop(0, i, logits, values, s_ref, m_ref, acc_ref, last_fix=causal)
        o = _flash_finish(acc_ref)
        o_ref[0, rows, :] = jnp.where(lo_half, o[0:t], o[t:2 * t]).astype(o_ref.dtype)
        return carry

    lax.fori_loop(0, seq // t, query_tile, 0)


def _fox(qkv, qx, kx):
    b, _, s, _ = qkv.shape
    t = T_ATT
    base = QKV_TILES_FOX
    return pl.pallas_call(
        _fox_kernel,
        grid=(b, FOX_PAIRS),
        in_specs=[pl.BlockSpec((1, 1, s, LANES), lambda bi, p: (bi, base + p, 0, 0)),
                  pl.BlockSpec((1, 1, s, LANES), lambda bi, p: (bi, base + FOX_PAIRS + p, 0, 0)),
                  pl.BlockSpec((1, 1, s, LANES), lambda bi, p: (bi, base + 2 * FOX_PAIRS + p, 0, 0)),
                  pl.BlockSpec((1, 1, s, LANES), lambda bi, p: (bi, p, 0, 0)),
                  pl.BlockSpec((1, 1, s, LANES), lambda bi, p: (bi, p, 0, 0))],
        out_specs=pl.BlockSpec((1, s, LANES), lambda bi, p: (bi, 0, p)),
        out_shape=jax.ShapeDtypeStruct((b, s, FOX_W), bf16),
        scratch_shapes=[pltpu.VMEM((2 * t, 2 * LANES), bf16),
                        pltpu.VMEM((s, 2 * LANES), bf16),
                        pltpu.VMEM((2 * t, LANES), f32),
                        pltpu.VMEM((2 * t, LANES), f32),
                        pltpu.VMEM((2, 2 * t, t), f32),
                        pltpu.VMEM((2, s, LANES), bf16)],
        compiler_params=_cparams(("parallel", "parallel")),
        name="fox",
    )(qkv, qkv, qkv, qx, kx)


def _merge_kernel(yn_ref, yf_ref, xb_ref, x_ref, wn_ref, wf_ref, wmg_ref, wo_ref, g_ref, b_ref,
                  xo_ref, xbo_ref):
    mg = _dot(xb_ref[...], wmg_ref[...])
    merged = (_sigmoid(mg[:, 0:D_MODEL]) * _dot(yn_ref[...], wn_ref[...])
              + _sigmoid(mg[:, D_MODEL:2 * D_MODEL]) * _dot(yf_ref[...], wf_ref[...]))
    hmix = _dot(merged.astype(bf16), wo_ref[...])
    xn = _layer_norm(DN_ALPHA * x_ref[...] + hmix, g_ref[...], b_ref[...])
    xo_ref[...] = xn
    xbo_ref[...] = xn.astype(bf16)


def _merge(yn, yf, xb, x, wn, wf, wmg, wo, g, bb):
    m = x.shape[0]
    tm = TM_MERGE
    row = lambda w: pl.BlockSpec((tm, w), lambda i: (i, 0))
    full = lambda a: pl.BlockSpec(a.shape, lambda i: (0, 0))
    return pl.pallas_call(
        _merge_kernel,
        grid=(m // tm,),
        in_specs=[row(NSA_Q_W), row(FOX_W), row(D_MODEL), row(D_MODEL),
                  full(wn), full(wf), full(wmg), full(wo), full(g), full(bb)],
        out_specs=[row(D_MODEL), row(D_MODEL)],
        out_shape=[jax.ShapeDtypeStruct((m, D_MODEL), f32), jax.ShapeDtypeStruct((m, D_MODEL), bf16)],
        compiler_params=_cparams(("parallel",)),
        name="merge",
    )(yn, yf, xb, x, wn, wf, wmg, wo, g, bb)


def _ffn_kernel(xb_ref, x_ref, wg_ref, wu_ref, wd_ref, g_ref, b_ref, xo_ref, xbo_ref, acc_ref, *, nf):
    f = pl.program_id(1)

    @pl.when(f == 0)
    def _():
        acc_ref[...] = jnp.zeros_like(acc_ref)

    xb = xb_ref[...]
    gate = _dot(xb, wg_ref[...])
    up = _dot(xb, wu_ref[...])
    act = (gate * _sigmoid(gate) * up).astype(bf16)
    acc_ref[...] += _dot(act, wd_ref[...])

    @pl.when(f == nf - 1)
    def _():
        xn = _layer_norm(DN_ALPHA * x_ref[...] + acc_ref[...], g_ref[...], b_ref[...])
        xo_ref[...] = xn
        xbo_ref[...] = xn.astype(bf16)


def _ffn(xb, x, wg, wu, wd, g, bb):
    m = x.shape[0]
    dff = wg.shape[1]
    tm, tf = TM_FFN, TF_FFN
    nf = dff // tf
    row = pl.BlockSpec((tm, D_MODEL), lambda i, f: (i, 0))
    vec = pl.BlockSpec((1, D_MODEL), lambda i, f: (0, 0))
    return pl.pallas_call(
        functools.partial(_ffn_kernel, nf=nf),
        grid=(m // tm, nf),
        in_specs=[row, row,
                  pl.BlockSpec((D_MODEL, tf), lambda i, f: (0, f)),
                  pl.BlockSpec((D_MODEL, tf), lambda i, f: (0, f)),
                  pl.BlockSpec((tf, D_MODEL), lambda i, f: (f, 0)),
                  vec, vec],
        out_specs=[row, row],
        out_shape=[jax.ShapeDtypeStruct((m, D_MODEL), f32), jax.ShapeDtypeStruct((m, D_MODEL), bf16)],
        scratch_shapes=[pltpu.VMEM((tm, D_MODEL), f32)],
        compiler_params=_cparams(("parallel", "arbitrary")),
        name="ffn",
    )(xb, x, wg, wu, wd, g, bb)


def _router_kernel(x_ref, r_ref, tri_ref, gate_ref, rank_ref, *, seq):
    tb = tri_ref.shape[0]
    x = x_ref[0]
    xh = x.astype(bf16)
    xl = (x - xh.astype(f32)).astype(bf16)
    r = r_ref[...]
    rh = r.astype(bf16)
    rl = (r - rh.astype(f32)).astype(bf16)
    logits = _dot(xh, rh) + _dot(xh, rl) + _dot(xl, rh)
    lane = lax.broadcasted_iota(i32, (seq, LANES), 1).astype(f32)
    low = -3.0e38
    lg = jnp.where(lane < N_EXPERTS, logits, low)
    m1 = jnp.max(lg, axis=-1, keepdims=True)
    i1 = jnp.min(jnp.where(lg == m1, lane, float(LANES)), axis=-1, keepdims=True)
    lg2 = jnp.where(lane == i1, low, lg)
    m2 = jnp.max(lg2, axis=-1, keepdims=True)
    i2 = jnp.min(jnp.where(lg2 == m2, lane, float(LANES)), axis=-1, keepdims=True)
    e2 = jnp.exp(m2 - m1)
    den = 1.0 + e2
    gate_ref[0] = jnp.where(lane == i1, 1.0 / den, jnp.where(lane == i2, e2 / den, 0.0))
    sel = (lane == i1) | (lane == i2)
    selb = jnp.where(sel, 1.0, 0.0).astype(bf16)
    carry = jnp.zeros((1, LANES), f32)
    for blk in range(seq // tb):
        sl = slice(blk * tb, (blk + 1) * tb)
        c = _dot(tri_ref[...], selb[sl]) + carry
        carry = c[tb - 1:tb, :]
        rank_ref[0, sl, :] = jnp.where(sel[sl], c - 1.0, -1.0)


def _router(x3, router_pad, tri):
    b, s, _ = x3.shape
    return pl.pallas_call(
        functools.partial(_router_kernel, seq=s),
        grid=(b,),
        in_specs=[pl.BlockSpec((1, s, D_MODEL), lambda i: (i, 0, 0)),
                  pl.BlockSpec((D_MODEL, LANES), lambda i: (0, 0)),
                  pl.BlockSpec(tri.shape, lambda i: (0, 0))],
        out_specs=[pl.BlockSpec((1, s, LANES), lambda i: (i, 0, 0)),
                   pl.BlockSpec((1, s, LANES), lambda i: (i, 0, 0))],
        out_shape=[jax.ShapeDtypeStruct((b, s, LANES), f32),
                   jax.ShapeDtypeStruct((b, s, LANES), f32)],
        compiler_params=_cparams(("parallel",)),
        name="router",
    )(x3, router_pad, tri)


def _moe_kernel(tot_ref, tot_al_ref, off_ref, npc_ref, xb_ref, rankt_ref, gatet_ref, ranke_ref, wg_ref, wu_ref, wd_ref,
                y_ref, xg_ref, acc_ref, *, seq, nf):
    tr = TR_MOE
    half = tr // 2
    nsubc = seq // SUB_MOE
    c = pl.program_id(0)
    e = pl.program_id(1)
    f = pl.program_id(2)
    ce = c * N_EXPERTS + e
    tot = tot_ref[ce]
    rem = tot % tr
    nfull = tot // tr + jnp.where(rem > half, 1, 0)
    has_tail = (rem > 0) & (rem <= half)
    tail0 = pl.multiple_of(nfull * tr, half)

    def pieces(fn):
        for q in range(nsubc):
            base = off_ref[ce * nsubc + q]

            def body(p, carry, q=q, base=base):
                fn(q, pl.multiple_of(base + p * PIECE_MOE, ROW_ALIGN))
                return carry
            lax.fori_loop(0, npc_ref[ce * nsubc + q], body, 0)

    def onehot_rows(q, r0):
        rk = rankt_ref[0, pl.ds(e, 1), q * SUB_MOE:(q + 1) * SUB_MOE]
        want = (r0 + lax.broadcasted_iota(i32, (PIECE_MOE, SUB_MOE), 0)).astype(f32)
        return rk == want

    def gather(q, r0):
        p = jnp.where(onehot_rows(q, r0), 1.0, 0.0).astype(bf16)
        xg_ref[pl.ds(r0, PIECE_MOE), :] += _dot(p, xb_ref[0, q * SUB_MOE:(q + 1) * SUB_MOE, :]).astype(bf16)

    def clear(r0, nr):
        xg_ref[pl.ds(r0, nr), :] = jnp.zeros((nr, D_MODEL), bf16)
        acc_ref[pl.ds(r0, nr), :] = jnp.zeros((nr, D_MODEL), f32)

    def hidden(r0, nr):
        xs = xg_ref[pl.ds(r0, nr), :]
        gate = _dot(xs, wg_ref[0, 0])
        up = _dot(xs, wu_ref[0, 0])
        act = (gate * _sigmoid(gate) * up).astype(bf16)
        acc_ref[pl.ds(r0, nr), :] += _dot(act, wd_ref[0, 0])

    def combine(q, r0):
        grow = gatet_ref[0, pl.ds(e, 1), q * SUB_MOE:(q + 1) * SUB_MOE]
        w = jnp.sum(jnp.where(onehot_rows(q, r0), grow, 0.0), axis=-1, keepdims=True)
        z = (acc_ref[pl.ds(r0, PIECE_MOE), :] * w).astype(bf16)
        want = (r0 + lax.broadcasted_iota(i32, (SUB_MOE, PIECE_MOE), 1)).astype(f32)
        rc = ranke_ref[0, 0, q * SUB_MOE:(q + 1) * SUB_MOE, :]
        pt = jnp.where(rc == want, 1.0, 0.0).astype(bf16)
        y_ref[0, q * SUB_MOE:(q + 1) * SUB_MOE, :] += _dot(pt, z)

    def tiles(fn):
        def body(s, carry):
            fn(pl.multiple_of(s * tr, tr), tr)
            return carry
        lax.fori_loop(0, nfull, body, 0)

        @pl.when(has_tail)
        def _():
            fn(tail0, half)

    def zero(j, carry):
        y_ref[0, pl.ds(j * tr, tr), :] = jnp.zeros((tr, D_MODEL), f32)
        return carry

    lax.fori_loop(0, jnp.where((e == 0) & (f == 0), seq // tr, 0), zero, 0)

    @pl.when(f == 0)
    def _():
        tiles(clear)
        clear(pl.multiple_of(tot_al_ref[ce], ROW_ALIGN), tr)
        pieces(gather)

    tiles(hidden)

    @pl.when(f == nf - 1)
    def _():
        pieces(combine)


def _moe(xb3, gate, rank, wg, wu, wd, layer):
    b, s, _ = xb3.shape
    dff = wg.shape[3]
    nf = dff // TF_MOE
    nsubc = s // SUB_MOE
    rk = rank[:, :, :N_EXPERTS]
    cnt = jnp.sum((rk >= 0.0).reshape(b, nsubc, SUB_MOE, N_EXPERTS), axis=2).astype(i32)
    first = jnp.cumsum(cnt, axis=1) - cnt
    off = first // ROW_ALIGN * ROW_ALIGN
    npc = jnp.where(cnt > 0, (first - off + cnt + PIECE_MOE - 1) // PIECE_MOE, 0)
    post = rk.transpose(0, 2, 1)
    gatet = gate[:, :, :N_EXPERTS].transpose(0, 2, 1)
    tot = jnp.sum(cnt, axis=1).reshape(-1)
    tot_al = (tot + ROW_ALIGN - 1) // ROW_ALIGN * ROW_ALIGN
    off_flat = off.transpose(0, 2, 1).reshape(-1)
    npc = npc.transpose(0, 2, 1).reshape(-1)
    buf_rows = s + ROW_ALIGN + PIECE_MOE + TR_MOE
    grid_spec = pltpu.PrefetchScalarGridSpec(
        num_scalar_prefetch=4,
        grid=(b, N_EXPERTS, nf),
        in_specs=[pl.BlockSpec((1, s, D_MODEL), lambda c, e, f, *_: (c, 0, 0)),
                  pl.BlockSpec((1, N_EXPERTS, s), lambda c, e, f, *_: (c, 0, 0)),
                  pl.BlockSpec((1, N_EXPERTS, s), lambda c, e, f, *_: (c, 0, 0)),
                  pl.BlockSpec((1, 1, s, 1), lambda c, e, f, *_: (c, e, 0, 0)),
                  pl.BlockSpec((1, 1, D_MODEL, TF_MOE), lambda c, e, f, *_: (layer, e, 0, f)),
                  pl.BlockSpec((1, 1, D_MODEL, TF_MOE), lambda c, e, f, *_: (layer, e, 0, f)),
                  pl.BlockSpec((1, 1, TF_MOE, D_MODEL), lambda c, e, f, *_: (layer, e, f, 0))],
        out_specs=pl.BlockSpec((1, s, D_MODEL), lambda c, e, f, *_: (c, 0, 0)),
        scratch_shapes=[pltpu.VMEM((buf_rows, D_MODEL), bf16), pltpu.VMEM((buf_rows, D_MODEL), f32)],
    )
    return pl.pallas_call(
        functools.partial(_moe_kernel, seq=s, nf=nf),
        grid_spec=grid_spec,
        out_shape=jax.ShapeDtypeStruct((b, s, D_MODEL), f32),
        compiler_params=_cparams(("parallel", "arbitrary", "arbitrary")),
        name="moe",
    )(tot, tot_al, off_flat, npc, xb3, post, gatet, post[..., None], wg, wu, wd)


def _resln_kernel(x_ref, y_ref, g_ref, b_ref, xo_ref, xbo_ref):
    xn = _layer_norm(DN_ALPHA * x_ref[...] + y_ref[...], g_ref[...], b_ref[...])
    xo_ref[...] = xn
    xbo_ref[...] = xn.astype(bf16)


def _resln(x, y, g, bb):
    m = x.shape[0]
    tm = TM_FFN
    row = pl.BlockSpec((tm, D_MODEL), lambda i: (i, 0))
    vec = pl.BlockSpec((1, D_MODEL), lambda i: (0, 0))
    return pl.pallas_call(
        _resln_kernel,
        grid=(m // tm,),
        in_specs=[row, row, vec, vec],
        out_specs=[row, row],
        out_shape=[jax.ShapeDtypeStruct((m, D_MODEL), f32), jax.ShapeDtypeStruct((m, D_MODEL), bf16)],
        compiler_params=_cparams(("parallel",)),
        name="resln",
    )(x, y, g, bb)


def _t5_bucket(dist):
    n = jnp.maximum(dist, 0)
    max_exact = REL_BUCKETS // 2
    large = max_exact + (jnp.log(jnp.maximum(n, 1).astype(f32) / max_exact)
                         / math.log(REL_MAX_DIST / max_exact) * (REL_BUCKETS - max_exact)).astype(i32)
    large = jnp.minimum(large, REL_BUCKETS - 1)
    return jnp.where(n < max_exact, n, large)


def _bias_of_dist(rel_bias, dist):
    onehot = (_t5_bucket(dist)[None] == jnp.arange(REL_BUCKETS).reshape((-1,) + (1,) * dist.ndim)).astype(f32)
    return LOG2E * jnp.einsum("kh,k...->h...", rel_bias.astype(f32), onehot, precision=lax.Precision.HIGHEST)


def _bias_tables(rel_bias, seq):
    t = T_ATT
    rows = NSA_GROUP * t
    ncmp = seq // CMP_STRIDE
    d0 = jnp.arange(t)[:, None] - jnp.arange(t)[None, :]
    offs = jnp.array([0, t, 2 * t, WINDOW]).reshape(4, 1, 1)
    kinds = _bias_of_dist(rel_bias, offs + d0[None])
    mask = jnp.stack([d0 >= 0, d0 == d0, d0 == d0, d0 < 0])
    kinds = jnp.where(mask[None], kinds, NEG_INF)
    order = np.array(NSA_ROW_ORDER)
    tbias = kinds.reshape(NSA_KV_HEADS, NSA_GROUP, 4, t, t)[:, order].transpose(0, 2, 1, 3, 4)
    tbias = tbias.reshape(NSA_KV_HEADS, 4, rows, t)
    cend = jnp.arange(ncmp) * CMP_STRIDE + CMP_BLOCK - 1
    bc = _bias_of_dist(rel_bias, jnp.arange(seq)[:, None] - cend[None, :])
    bc = bc.reshape(NSA_KV_HEADS, NSA_GROUP, seq // t, t, ncmp)[:, order].transpose(0, 2, 4, 1, 3)
    return tbias, bc.reshape(NSA_KV_HEADS, (seq // t) * ncmp, rows)


def _selection_constants(seq):
    ncmp = seq // CMP_STRIDE
    nsel = seq // SEL_BLOCK
    c0 = np.arange(ncmp)[:, None] * CMP_STRIDE
    s0 = np.arange(LANES)[None, :] * SEL_BLOCK
    ov = np.maximum(np.minimum(c0 + CMP_BLOCK, s0 + SEL_BLOCK) - np.maximum(c0, s0), 0) / CMP_BLOCK
    ov[ncmp - 1, :] = 0.0
    ov[:, nsel:] = 0.0
    et = (np.arange(seq)[:, None] // SEL_BLOCK == np.arange(LANES)[None, :]).astype(np.float32)
    return jnp.asarray(ov.T, bf16), jnp.asarray(et, bf16)


def _fox_placement():
    xw = FOX_PAIRS * LANES
    pq = np.zeros((3, LANES, xw), np.float32)
    pk = np.zeros((3, LANES, xw), np.float32)
    oq = np.zeros((1, xw), np.float32)
    ok = np.zeros((1, xw), np.float32)
    for p in range(FOX_PAIRS):
        for hh in range(2):
            src = FGATE_LANE + 2 * p + hh
            base = p * LANES + hh * XCOLS
            for part in range(3):
                pk[part, src, base + part] = -1.0
                pq[part, src, base + 3 + part] = 1.0
                oq[0, base + part] = 1.0
                ok[0, base + 3 + part] = 1.0
    return jnp.asarray(pq, bf16), jnp.asarray(pk, bf16), jnp.asarray(oq), jnp.asarray(ok)


def _layer_weights(w_in, layer_pe, w1, w2, f_bias):
    offs = np.cumsum((NSA_Q_W, 6 * 2 * HEAD_DIM, 3 * NSA_HEADS, 3 * FOX_W, FOX_HEADS, 2 * D_MODEL))
    kv0, g0, fx0, ff0, mg0 = offs[0], offs[1], offs[2], offs[3], offs[4]
    scale = HEAD_DIM ** -0.5 * LOG2E
    kvw = NSA_KV_HEADS * HEAD_DIM
    w_kv = w_in[:, kv0:g0]
    w_kvdup = jnp.repeat(w_kv[:, 2 * kvw:].reshape(D_MODEL, 4 * NSA_KV_HEADS, 1, HEAD_DIM), 2, axis=2)
    w_kvdup = w_kvdup.reshape(D_MODEL, 8 * kvw)
    w_qkv = jnp.concatenate([w_in[:, :NSA_Q_W] * scale, w_kvdup, w_in[:, fx0:fx0 + FOX_W] * scale,
                             w_in[:, fx0 + FOX_W:ff0]], axis=1).astype(bf16)
    zeros = lambda n: jnp.zeros((D_MODEL, n), w_in.dtype)
    ng = 3 * NSA_GROUP
    w_g = w_in[:, g0:fx0].reshape(D_MODEL, NSA_KV_HEADS, NSA_GROUP, 3).transpose(0, 1, 3, 2)
    gate_cols = lambda h: w_g[:, h].reshape(D_MODEL, ng)
    w_aux = jnp.concatenate([w_kv[:, :2 * kvw],
                             gate_cols(0), zeros(FGATE_LANE - ng), w_in[:, ff0:mg0],
                             zeros(LANES - FGATE_LANE - FOX_HEADS),
                             gate_cols(1), zeros(LANES - ng)], axis=1).astype(bf16)
    fb_row = jnp.zeros((1, LANES), f32).at[0, FGATE_LANE:FGATE_LANE + FOX_HEADS].set(f_bias.astype(f32))
    pe2 = jnp.tile(layer_pe.astype(f32), (1, 1, NSA_KV_HEADS))
    w1r = w1.reshape(2, CMP_BLOCK, HEAD_DIM, CMP_HIDDEN).astype(bf16)
    zero = jnp.zeros_like(w1r)
    w1bd = jnp.concatenate([jnp.concatenate([w1r, zero], axis=-1),
                            jnp.concatenate([zero, w1r], axis=-1)], axis=-2)
    w2d = jnp.concatenate([w2, w2], axis=-1).astype(bf16)
    return (w_qkv, w_aux, w_in[:, mg0:].astype(bf16), fb_row, pe2, w1bd[:, :CMP_STRIDE], w1bd[:, CMP_STRIDE:],
            w2d)


def kernel(x, w_in, nsa_cmp_pe, nsa_cmp_w1, nsa_cmp_w2, fox_f_bias, w_nsa_branch, w_fox_branch, w_out,
           rel_bias, ln1_g, ln1_b, ln2_g, ln2_b, dense_w_gate, dense_w_up, dense_w_down, moe_router,
           moe_w_gate, moe_w_up, moe_w_down):
    b, s, d = x.shape
    m = b * s
    tbias, bias_c = _bias_tables(rel_bias, s)
    ov, et = _selection_constants(s)
    pq, pk, oq, ok = _fox_placement()
    tri128 = jnp.asarray(np.tril(np.ones((LANES, LANES), np.float32)), bf16)
    tri256 = jnp.asarray(np.tril(np.ones((256, 256), np.float32)), bf16)

    moe_wg, moe_wu, moe_wd = moe_w_gate.astype(bf16), moe_w_up.astype(bf16), moe_w_down.astype(bf16)
    xf = x.reshape(m, d).astype(f32)
    xb = xf.astype(bf16)
    for layer in range(DEPTH):
        w_qkv, w_aux, w_mg, fb_row, pe2, w1t, w1b, w2d = _layer_weights(
            w_in[layer], nsa_cmp_pe[layer], nsa_cmp_w1[layer], nsa_cmp_w2[layer], fox_f_bias[layer])
        qkv, aux = _proj(xb, w_qkv, w_aux, b)
        aux = aux.reshape(b, s, AUX_W)
        gates, qx, kx = _gateprep(aux, fb_row, tri128, pq, pk, oq, ok)
        cmpkv = _compress(aux, pe2, w1t, w1b, w2d)
        y_nsa = _nsa(qkv, cmpkv, gates, bias_c, tbias, ov, et).reshape(m, NSA_Q_W)
        y_fox = _fox(qkv, qx, kx).reshape(m, FOX_W)
        xf, xb = _merge(y_nsa, y_fox, xb, xf, w_nsa_branch[layer].astype(bf16),
                        w_fox_branch[layer].astype(bf16), w_mg, w_out[layer].astype(bf16),
                        ln1_g[layer].reshape(1, d), ln1_b[layer].reshape(1, d))
        j = layer // 2
        g2, b2 = ln2_g[layer].reshape(1, d), ln2_b[layer].reshape(1, d)
        if layer % 2 == 0:
            xf, xb = _ffn(xb, xf, dense_w_gate[j].astype(bf16), dense_w_up[j].astype(bf16),
                          dense_w_down[j].astype(bf16), g2, b2)
        else:
            router_pad = jnp.zeros((d, LANES), f32).at[:, :N_EXPERTS].set(moe_router[j].astype(f32))
            gate, rank = _router(xf.reshape(b, s, d), router_pad, tri256)
            y = _moe(xb.reshape(b, s, d), gate, rank, moe_wg, moe_wu, moe_wd, j)
            xf, xb = _resln(xf, y.reshape(m, d), g2, b2)
    return xf.reshape(b, s, d).astype(x.dtype)
```

```python
import functools
import math

import numpy as np
import jax
import jax.numpy as jnp
from jax import lax
from jax.experimental import pallas as pl
from jax.experimental.pallas import tpu as pltpu

f32 = jnp.float32
bf16 = jnp.bfloat16
i32 = jnp.int32

D_MODEL = 1024
HEAD_DIM = 64
LANES = 128
NSA_HEADS = 8
NSA_KV_HEADS = 2
NSA_GROUP = NSA_HEADS // NSA_KV_HEADS
NSA_ROW_ORDER = (0, 2, 1, 3)
FOX_HEADS = 8
FOX_PAIRS = FOX_HEADS // 2
CMP_BLOCK = 32
CMP_STRIDE = 16
CMP_HIDDEN = 128
SEL_BLOCK = 64
N_SELECT = 16
N_LOCAL_SEL = 2
WINDOW = 512
REL_BUCKETS = 32
REL_MAX_DIST = 128
N_EXPERTS = 8
DEPTH = 4
DN_ALPHA = (2 * DEPTH) ** 0.25
LN_EPS = 1e-5
FORCED_SCORE = 1e4
NEG_INF = -1e30
LOG2E = math.log2(math.e)

NSA_Q_W = NSA_HEADS * HEAD_DIM
FOX_W = FOX_HEADS * HEAD_DIM
QKV_TILES_KV = NSA_Q_W // LANES
QKV_TILES_FOX = QKV_TILES_KV + 4 * NSA_KV_HEADS
QKV_W = (QKV_TILES_FOX + 3 * FOX_PAIRS) * LANES
AUX_W = 4 * LANES
FGATE_LANE = 24
XCOLS = 6

T_ATT = 256
TQ_FOX = 512
TM_PROJ = 512
TM_MERGE = 512
TM_FFN = 512
TF_FFN = 1408
TR_MOE = 256
SUB_MOE = 512
PIECE_MOE = 192
ROW_ALIGN = 16
TF_MOE = 896
VMEM_LIMIT = 56 * 1024 * 1024


def _cparams(sem):
    return pltpu.CompilerParams(dimension_semantics=sem, vmem_limit_bytes=VMEM_LIMIT)


def _dot(a, b):
    return jnp.dot(a, b, preferred_element_type=f32)


def _dot_nt(a, b):
    return lax.dot_general(a, b, (((1,), (1,)), ((), ())), preferred_element_type=f32)


def _sigmoid(x):
    return 1.0 / (1.0 + jnp.exp(-x))


def _layer_norm(z, g, b):
    mu = jnp.mean(z, axis=-1, keepdims=True)
    zc = z - mu
    var = jnp.mean(zc * zc, axis=-1, keepdims=True)
    return zc * lax.rsqrt(var + LN_EPS) * g + b


def _split3(x):
    hi = x.astype(bf16)
    r1 = x - hi.astype(f32)
    mid = r1.astype(bf16)
    lo = (r1 - mid.astype(f32)).astype(bf16)
    return hi, mid, lo


def _proj_kernel(x_ref, w_ref, wa_ref, o_ref, oa_ref):
    x = x_ref[...]
    res = _dot(x, w_ref[...])
    for j in range(o_ref.shape[1]):
        o_ref[0, j] = res[:, j * LANES:(j + 1) * LANES].astype(o_ref.dtype)
    oa_ref[...] = _dot(x, wa_ref[...])


def _proj(xb, w, wa, batch):
    m, k = xb.shape
    n, na = w.shape[1], wa.shape[1]
    nb = m // batch // TM_PROJ
    return pl.pallas_call(
        _proj_kernel,
        grid=(m // TM_PROJ,),
        in_specs=[pl.BlockSpec((TM_PROJ, k), lambda i: (i, 0)),
                  pl.BlockSpec((k, n), lambda i: (0, 0)),
                  pl.BlockSpec((k, na), lambda i: (0, 0))],
        out_specs=[pl.BlockSpec((1, n // LANES, TM_PROJ, LANES), lambda i: (i // nb, 0, i % nb, 0)),
                   pl.BlockSpec((TM_PROJ, na), lambda i: (i, 0))],
        out_shape=[jax.ShapeDtypeStruct((batch, n // LANES, m // batch, LANES), bf16),
                   jax.ShapeDtypeStruct((m, na), f32)],
        compiler_params=_cparams(("parallel",)),
        name="proj",
    )(xb, w, wa)


def _gateprep_kernel(a_ref, fb_ref, tri_ref, pq_ref, pk_ref, oq_ref, ok_ref, g_ref, qx_ref, kx_ref, *, seq):
    tb = LANES
    tri = tri_ref[...]
    local = []
    for blk in range(seq // tb):
        sl = slice(blk * tb, (blk + 1) * tb)
        va = a_ref[0, sl, 0:LANES]
        g_ref[0, 0, sl, :] = _sigmoid(va)
        g_ref[0, 1, sl, :] = _sigmoid(a_ref[0, sl, LANES:2 * LANES])
        z = va + fb_ref[...]
        logf = jnp.minimum(z, 0.0) - jnp.log1p(jnp.exp(-jnp.abs(z)))
        hi, mid, lo = _split3(logf)
        local.append(_dot(tri, hi) + _dot(tri, mid) + _dot(tri, lo))
    carry = jnp.zeros((1, LANES), f32)
    for blk in range(seq // tb):
        sl = slice(blk * tb, (blk + 1) * tb)
        c = local[blk] + carry
        carry = c[tb - 1:tb, :]
        chi, cmid, clo = _split3(c * LOG2E)
        qx = _dot(chi, pq_ref[0]) + _dot(cmid, pq_ref[1]) + _dot(clo, pq_ref[2]) + oq_ref[...]
        kx = _dot(chi, pk_ref[0]) + _dot(cmid, pk_ref[1]) + _dot(clo, pk_ref[2]) + ok_ref[...]
        for p in range(FOX_PAIRS):
            qx_ref[0, p, sl, :] = qx[:, p * LANES:(p + 1) * LANES].astype(bf16)
            kx_ref[0, p, sl, :] = kx[:, p * LANES:(p + 1) * LANES].astype(bf16)


def _gateprep(aux, fb_row, tri, pq, pk, oq, ok):
    b, s, _ = aux.shape
    const2 = lambda a: pl.BlockSpec(a.shape, lambda i: (0, 0))
    const3 = lambda a: pl.BlockSpec(a.shape, lambda i: (0, 0, 0))
    return pl.pallas_call(
        functools.partial(_gateprep_kernel, seq=s),
        grid=(b,),
        in_specs=[pl.BlockSpec((1, s, 2 * LANES), lambda i: (i, 0, 1)),
                  const2(fb_row), const2(tri), const3(pq), const3(pk), const2(oq), const2(ok)],
        out_specs=[pl.BlockSpec((1, 2, s, LANES), lambda i: (i, 0, 0, 0)),
                   pl.BlockSpec((1, FOX_PAIRS, s, LANES), lambda i: (i, 0, 0, 0)),
                   pl.BlockSpec((1, FOX_PAIRS, s, LANES), lambda i: (i, 0, 0, 0))],
        out_shape=[jax.ShapeDtypeStruct((b, 2, s, LANES), f32),
                   jax.ShapeDtypeStruct((b, FOX_PAIRS, s, LANES), bf16),
                   jax.ShapeDtypeStruct((b, FOX_PAIRS, s, LANES), bf16)],
        compiler_params=_cparams(("parallel",)),
        name="gateprep",
    )(aux, fb_row, tri, pq, pk, oq, ok)


def _gelu_tanh(x):
    c = math.sqrt(2.0 / math.pi)
    return x * (0.5 * (1.0 + jnp.tanh(c * (x + 0.044715 * (x * x * x)))))


def _compress_kernel(a_ref, pe_ref, w1t_ref, w1b_ref, w2_ref, o_ref, *, nhalf):
    top = jnp.zeros((nhalf, NSA_KV_HEADS * CMP_HIDDEN), f32)
    bot = jnp.zeros((nhalf, NSA_KV_HEADS * CMP_HIDDEN), f32)
    for l in range(CMP_STRIDE):
        rows = a_ref[0, pl.ds(l, nhalf, stride=CMP_STRIDE), :]
        top = top + _dot((rows + pe_ref[0, l:l + 1, :]).astype(bf16), w1t_ref[0, l])
        bot = bot + _dot((rows + pe_ref[0, CMP_STRIDE + l:CMP_STRIDE + l + 1, :]).astype(bf16), w1b_ref[0, l])
    pre = top + pltpu.roll(bot, nhalf - 1, 0)
    act = _gelu_tanh(pre).astype(bf16)
    for h in range(NSA_KV_HEADS):
        o_ref[0, 0, h] = _dot(act[:, h * CMP_HIDDEN:(h + 1) * CMP_HIDDEN], w2_ref[0]).astype(o_ref.dtype)


def _compress(aux, pe2, w1t, w1b, w2d):
    b, s, _ = aux.shape
    nhalf = s // CMP_STRIDE
    w1_spec = pl.BlockSpec((1,) + w1t.shape[1:], lambda i, j: (j, 0, 0, 0))
    return pl.pallas_call(
        functools.partial(_compress_kernel, nhalf=nhalf),
        grid=(b, 2),
        in_specs=[pl.BlockSpec((1, s, LANES), lambda i, j: (i, 0, j)),
                  pl.BlockSpec((1, CMP_BLOCK, LANES), lambda i, j: (j, 0, 0)),
                  w1_spec, w1_spec,
                  pl.BlockSpec((1, CMP_HIDDEN, LANES), lambda i, j: (j, 0, 0))],
        out_specs=pl.BlockSpec((1, 1, NSA_KV_HEADS, nhalf, LANES), lambda i, j: (i, j, 0, 0, 0)),
        out_shape=jax.ShapeDtypeStruct((b, 2, NSA_KV_HEADS, nhalf, LANES), bf16),
        compiler_params=_cparams(("parallel", "parallel")),
        name="compress",
    )(aux, pe2, w1t, w1b, w2d)


def _flash_init(m_ref, acc_ref):
    m_ref[...] = jnp.full(m_ref.shape, NEG_INF, f32)
    acc_ref[...] = jnp.zeros(acc_ref.shape, f32)


def _ones_values(v):
    lo_half = lax.broadcasted_iota(i32, v.shape, 1) < HEAD_DIM
    vf = v.astype(f32)
    return jnp.where(lo_half, vf, 1.0).astype(bf16), jnp.where(lo_half, 1.0, vf).astype(bf16)


def _flash_step(s, v_lo, v_hi, m_ref, acc_ref):
    nk = s.shape[1] // LANES
    half = s.shape[0] // 2
    cols = [s[:, c * LANES:(c + 1) * LANES] for c in range(nk)]
    mx = cols[0]
    for c in cols[1:]:
        mx = jnp.maximum(mx, c)
    m_old = m_ref[...]
    m_new = jnp.maximum(m_old, jnp.broadcast_to(jnp.max(mx, axis=-1, keepdims=True), m_old.shape))
    alpha = jnp.exp2(m_old - m_new)
    p = jnp.concatenate([jnp.exp2(c - m_new).astype(bf16) for c in cols], axis=1)
    pv = jnp.concatenate([_dot(p[0:half], v_lo), _dot(p[half:], v_hi)], axis=0)
    acc_ref[...] = alpha * acc_ref[...] + pv
    m_ref[...] = m_new


def _flash_loop(first, last, logits_fn, v_fn, s_ref, m_ref, acc_ref, tail_fix=None):
    sa, sb = s_ref.at[0], s_ref.at[1]
    n = last - first + 1
    pairs = (n - 1) // 2

    def step(buf, kt, fix=None):
        s = buf[...] if fix is None else fix(buf[...], kt)
        _flash_step(s, *v_fn(kt), m_ref, acc_ref)

    sa[...] = logits_fn(first)

    def body(j, carry):
        kt = first + 2 * j
        sb[...] = logits_fn(kt + 1)
        step(sa, kt)
        sa[...] = logits_fn(kt + 2)
        step(sb, kt + 1)
        return carry

    lax.fori_loop(0, pairs, body, 0)
    two_left = n - 2 * pairs == 2

    @pl.when(two_left)
    def _():
        sb[...] = logits_fn(last)
        step(sa, last - 1, tail_fix)
        step(sb, last, tail_fix)

    @pl.when(jnp.logical_not(two_left))
    def _():
        step(sa, last, tail_fix)


def _flash_finish(acc_ref):
    acc = acc_ref[...]
    return acc * (1.0 / jnp.maximum(pltpu.roll(acc, HEAD_DIM, 1), 1e-30))


def _nsa_kernel(q_ref, ks_ref, vs_ref, kw_ref, vw_ref, kc_ref, vc_ref, gate_ref, bc_ref, tb_ref, ov_ref,
                et_ref, o_ref, qaug_ref, kaug_ref, m_ref, acc_ref, os_ref, s_ref, vsel_ref, vwin_ref, *,
                seq, ncmp):
    kaug_ref[:, 0:LANES] = ks_ref[0, 0]
    kaug_ref[:, LANES:2 * LANES] = et_ref[...]
    vsel_ref[0], vsel_ref[1] = _ones_values(vs_ref[0, 0])
    vwin_ref[0], vwin_ref[1] = _ones_values(vw_ref[0, 0])

    def query_tile(i, carry):
        _nsa_tile(i, q_ref, kw_ref, kc_ref, vc_ref, gate_ref, bc_ref, tb_ref, ov_ref, o_ref, qaug_ref,
                  kaug_ref, m_ref, acc_ref, os_ref, s_ref, vsel_ref, vwin_ref, seq=seq, ncmp=ncmp)
        return carry

    lax.fori_loop(0, seq // T_ATT, query_tile, 0)


def _nsa_tile(i, q_ref, kw_ref, kc_ref, vc_ref, gate_ref, bc_ref, tb_ref, ov_ref, o_ref, qaug_ref,
              kaug_ref, m_ref, acc_ref, os_ref, s_ref, vsel_ref, vwin_ref, *, seq, ncmp):
    t = T_ATT
    rows = NSA_GROUP * t
    nsel = seq // SEL_BLOCK
    topn = min(N_SELECT, nsel)
    t0 = i * t
    tile_rows = pl.ds(pl.multiple_of(t0, t), t)
    lane = lax.broadcasted_iota(i32, (t, LANES), 1)
    lo_half = lane < HEAD_DIM

    for rb, g in enumerate(NSA_ROW_ORDER):
        qg = q_ref[0, g // 2, tile_rows, :].astype(f32)
        keep = lo_half if g % 2 == 0 else jnp.logical_not(lo_half)
        qaug_ref[rb * t:(rb + 1) * t, 0:LANES] = jnp.where(keep, qg, 0.0).astype(bf16)
    qs = qaug_ref[:, 0:LANES]

    s = _dot_nt(kc_ref[0, 0, 0], qs) + bc_ref[0, pl.ds(pl.multiple_of(i * ncmp, ncmp), ncmp), :]
    row_t = t0 + (lax.broadcasted_iota(i32, (ncmp, rows), 1) & (t - 1))
    cmp_end = lax.broadcasted_iota(i32, (ncmp, rows), 0) * CMP_STRIDE + (CMP_BLOCK - 1)
    vis = cmp_end <= row_t
    s = jnp.where(vis, s, NEG_INF)
    s = s - jnp.max(s, axis=0, keepdims=True)
    e = jnp.where(vis, jnp.exp2(s), 0.0)
    p_c = e * (1.0 / jnp.maximum(jnp.sum(e, axis=0, keepdims=True), 1e-30))
    o_c = _dot(p_c.T.astype(bf16), vc_ref[0, 0, 0])

    psum = p_c[:, 0:t]
    for g in range(1, NSA_GROUP):
        psum = psum + p_c[:, g * t:(g + 1) * t]
    hi = psum.astype(bf16)
    lo = (psum - hi.astype(f32)).astype(bf16)
    imp = (_dot(ov_ref[...], hi) + _dot(ov_ref[...], lo))[0:nsel]
    blk = lax.broadcasted_iota(i32, (nsel, t), 0)
    cur = (t0 + lax.broadcasted_iota(i32, (nsel, t), 1)) >> 6
    forced = (blk == 0) | ((blk <= cur) & (blk > cur - N_LOCAL_SEL))
    val = jnp.where(forced, FORCED_SCORE, jnp.where(blk <= cur, imp, -1.0))
    cnt = jnp.zeros((nsel, t), f32)
    for j in range(nsel):
        vj = val[j:j + 1, :]
        beats = (vj > val) | ((vj == val) & (blk > j))
        cnt = cnt + jnp.where(beats, 1.0, 0.0)
    mneg = jnp.where(cnt < topn, 0.0, NEG_INF)
    mneg = jnp.concatenate([mneg, jnp.zeros((LANES - nsel, t), f32)], axis=0).T.astype(bf16)
    for g in range(NSA_GROUP):
        qaug_ref[g * t:(g + 1) * t, LANES:2 * LANES] = mneg

    _flash_init(m_ref, acc_ref)

    def sel_logits(kt):
        k = kaug_ref[pl.ds(kt * t, t), :]
        return _dot_nt(qaug_ref[...], k) + tb_ref[0, jnp.minimum(i - kt, 2)]

    def values(v_ref):
        return lambda kt: (v_ref[0, pl.ds(kt * t, t), :], v_ref[1, pl.ds(kt * t, t), :])

    _flash_loop(0, i, sel_logits, values(vsel_ref), s_ref, m_ref, acc_ref)
    os_ref[...] = _flash_finish(acc_ref)

    _flash_init(m_ref, acc_ref)
    nwin = WINDOW // t

    def win_logits(kt):
        d = i - kt
        kind = jnp.where(d == nwin, 3, d)
        return _dot_nt(qs, kw_ref[0, 0, pl.ds(kt * t, t), :]) + tb_ref[0, kind]

    _flash_loop(jnp.maximum(i - nwin, 0), i, win_logits, values(vwin_ref), s_ref, m_ref, acc_ref)
    o_w = _flash_finish(acc_ref)
    o_s = os_ref[...]

    gates = gate_ref[0, 0, tile_rows, :]
    outs = []
    for g in range(NSA_GROUP):
        rb = NSA_ROW_ORDER.index(g)
        sl = slice(rb * t, (rb + 1) * t)
        outs.append(gates[:, g:g + 1] * o_c[sl]
                    + gates[:, NSA_GROUP + g:NSA_GROUP + g + 1] * o_s[sl]
                    + gates[:, 2 * NSA_GROUP + g:2 * NSA_GROUP + g + 1] * o_w[sl])
    for j in range(NSA_GROUP // 2):
        o_ref[0, tile_rows, j * LANES:(j + 1) * LANES] = jnp.where(
            lo_half, outs[2 * j], outs[2 * j + 1]).astype(o_ref.dtype)


def _nsa(qkv, cmpkv, gates, bias_c, tbias, ov, et):
    b, _, s, _ = qkv.shape
    ncmp = cmpkv.shape[3]
    t = T_ATT
    rows = NSA_GROUP * t
    qw = NSA_GROUP * HEAD_DIM
    qtiles = qw // LANES
    kv_spec = lambda col: pl.BlockSpec((1, 1, s, LANES), lambda h, bi, col=col: (bi, col + h, 0, 0))
    cmp_spec = lambda kv: pl.BlockSpec((1, 1, 1, ncmp, LANES), lambda h, bi, kv=kv: (bi, kv, h, 0, 0))
    base = QKV_TILES_KV
    return pl.pallas_call(
        functools.partial(_nsa_kernel, seq=s, ncmp=ncmp),
        grid=(NSA_KV_HEADS, b),
        in_specs=[pl.BlockSpec((1, qtiles, s, LANES), lambda h, bi: (bi, h, 0, 0)),
                  kv_spec(base), kv_spec(base + 2), kv_spec(base + 4), kv_spec(base + 6),
                  cmp_spec(0), cmp_spec(1),
                  pl.BlockSpec((1, 1, s, LANES), lambda h, bi: (bi, h, 0, 0)),
                  pl.BlockSpec((1, (s // t) * ncmp, rows), lambda h, bi: (h, 0, 0)),
                  pl.BlockSpec((1, 4, rows, t), lambda h, bi: (h, 0, 0, 0)),
                  pl.BlockSpec((LANES, ncmp), lambda h, bi: (0, 0)),
                  pl.BlockSpec((s, LANES), lambda h, bi: (0, 0))],
        out_specs=pl.BlockSpec((1, s, qw), lambda h, bi: (bi, 0, h)),
        out_shape=jax.ShapeDtypeStruct((b, s, NSA_Q_W), bf16),
        scratch_shapes=[pltpu.VMEM((rows, 2 * LANES), bf16),
                        pltpu.VMEM((s, 2 * LANES), bf16),
                        pltpu.VMEM((rows, LANES), f32),
                        pltpu.VMEM((rows, LANES), f32),
                        pltpu.VMEM((rows, LANES), f32),
                        pltpu.VMEM((2, rows, t), f32),
                        pltpu.VMEM((2, s, LANES), bf16),
                        pltpu.VMEM((2, s, LANES), bf16)],
        compiler_params=_cparams(("parallel", "parallel")),
        name="nsa",
    )(qkv, qkv, qkv, qkv, qkv, cmpkv, cmpkv, gates, bias_c, tbias, ov, et)


def _fox_kernel(q_ref, k_ref, v_ref, qx_ref, kx_ref, o_ref, qaug_ref, kaug_ref, m_ref, acc_ref, s_ref,
                vaug_ref):
    t = T_ATT
    tq = TQ_FOX
    seq = k_ref.shape[2]
    lane = lax.broadcasted_iota(i32, (tq, LANES), 1)
    lo_half = lane < HEAD_DIM
    kaug_ref[:, 0:LANES] = k_ref[0, 0]
    kaug_ref[:, LANES:2 * LANES] = kx_ref[0, 0]
    vaug_ref[0], vaug_ref[1] = _ones_values(v_ref[0, 0])

    def logits(kt):
        return _dot_nt(qaug_ref[...], kaug_ref[pl.ds(kt * t, t), :])

    def values(kt):
        return vaug_ref[0, pl.ds(kt * t, t), :], vaug_ref[1, pl.ds(kt * t, t), :]

    def query_tile(i, carry):
        rows = pl.ds(pl.multiple_of(i * tq, tq), tq)
        q = q_ref[0, 0, rows, :].astype(f32)
        qx = qx_ref[0, 0, rows, :].astype(f32)
        qaug_ref[0:tq, 0:LANES] = jnp.where(lo_half, q, 0.0).astype(bf16)
        qaug_ref[tq:2 * tq, 0:LANES] = jnp.where(lo_half, 0.0, q).astype(bf16)
        qaug_ref[0:tq, LANES:2 * LANES] = jnp.where(lane < XCOLS, qx, 0.0).astype(bf16)
        qaug_ref[tq:2 * tq, LANES:2 * LANES] = jnp.where((lane >= XCOLS) & (lane < 2 * XCOLS), qx, 0.0).astype(bf16)

        def causal(s, kt):
            q_pos = i * tq + (lax.broadcasted_iota(i32, (2 * tq, t), 0) & (tq - 1))
            k_pos = kt * t + lax.broadcasted_iota(i32, (2 * tq, t), 1)
            return jnp.where(k_pos <= q_pos, s, NEG_INF)

        _flash_init(m_ref, acc_ref)
        _flash_loop(0, (i + 1) * (tq // t) - 1, logits, values, s_ref, m_ref, acc_ref, tail_fix=causal)
        o = _flash_finish(acc_ref)
        o_ref[0, rows, :] = jnp.where(lo_half, o[0:tq], o[tq:2 * tq]).astype(o_ref.dtype)
        return carry

    lax.fori_loop(0, seq // tq, query_tile, 0)


def _fox(qkv, qx, kx):
    b, _, s, _ = qkv.shape
    t = T_ATT
    base = QKV_TILES_FOX
    return pl.pallas_call(
        _fox_kernel,
        grid=(b, FOX_PAIRS),
        in_specs=[pl.BlockSpec((1, 1, s, LANES), lambda bi, p: (bi, base + p, 0, 0)),
                  pl.BlockSpec((1, 1, s, LANES), lambda bi, p: (bi, base + FOX_PAIRS + p, 0, 0)),
                  pl.BlockSpec((1, 1, s, LANES), lambda bi, p: (bi, base + 2 * FOX_PAIRS + p, 0, 0)),
                  pl.BlockSpec((1, 1, s, LANES), lambda bi, p: (bi, p, 0, 0)),
                  pl.BlockSpec((1, 1, s, LANES), lambda bi, p: (bi, p, 0, 0))],
        out_specs=pl.BlockSpec((1, s, LANES), lambda bi, p: (bi, 0, p)),
        out_shape=jax.ShapeDtypeStruct((b, s, FOX_W), bf16),
        scratch_shapes=[pltpu.VMEM((2 * TQ_FOX, 2 * LANES), bf16),
                        pltpu.VMEM((s, 2 * LANES), bf16),
                        pltpu.VMEM((2 * TQ_FOX, LANES), f32),
                        pltpu.VMEM((2 * TQ_FOX, LANES), f32),
                        pltpu.VMEM((2, 2 * TQ_FOX, t), f32),
                        pltpu.VMEM((2, s, LANES), bf16)],
        compiler_params=_cparams(("parallel", "parallel")),
        name="fox",
    )(qkv, qkv, qkv, qx, kx)


def _merge_kernel(yn_ref, yf_ref, xb_ref, x_ref, wn_ref, wf_ref, wmg_ref, wo_ref, g_ref, b_ref,
                  xo_ref, xbo_ref):
    mg = _dot(xb_ref[...], wmg_ref[...])
    merged = (_sigmoid(mg[:, 0:D_MODEL]) * _dot(yn_ref[...], wn_ref[...])
              + _sigmoid(mg[:, D_MODEL:2 * D_MODEL]) * _dot(yf_ref[...], wf_ref[...]))
    hmix = _dot(merged.astype(bf16), wo_ref[...])
    xn = _layer_norm(DN_ALPHA * x_ref[...] + hmix, g_ref[...], b_ref[...])
    xo_ref[...] = xn
    xbo_ref[...] = xn.astype(bf16)


def _merge(yn, yf, xb, x, wn, wf, wmg, wo, g, bb):
    m = x.shape[0]
    tm = TM_MERGE
    row = lambda w: pl.BlockSpec((tm, w), lambda i: (i, 0))
    full = lambda a: pl.BlockSpec(a.shape, lambda i: (0, 0))
    return pl.pallas_call(
        _merge_kernel,
        grid=(m // tm,),
        in_specs=[row(NSA_Q_W), row(FOX_W), row(D_MODEL), row(D_MODEL),
                  full(wn), full(wf), full(wmg), full(wo), full(g), full(bb)],
        out_specs=[row(D_MODEL), row(D_MODEL)],
        out_shape=[jax.ShapeDtypeStruct((m, D_MODEL), f32), jax.ShapeDtypeStruct((m, D_MODEL), bf16)],
        compiler_params=_cparams(("parallel",)),
        name="merge",
    )(yn, yf, xb, x, wn, wf, wmg, wo, g, bb)


def _ffn_kernel(xb_ref, x_ref, wg_ref, wu_ref, wd_ref, g_ref, b_ref, xo_ref, xbo_ref, acc_ref, *, nf):
    f = pl.program_id(1)

    @pl.when(f == 0)
    def _():
        acc_ref[...] = jnp.zeros_like(acc_ref)

    xb = xb_ref[...]
    gate = _dot(xb, wg_ref[...])
    up = _dot(xb, wu_ref[...])
    act = (gate * _sigmoid(gate) * up).astype(bf16)
    acc_ref[...] += _dot(act, wd_ref[...])

    @pl.when(f == nf - 1)
    def _():
        xn = _layer_norm(DN_ALPHA * x_ref[...] + acc_ref[...], g_ref[...], b_ref[...])
        xo_ref[...] = xn
        xbo_ref[...] = xn.astype(bf16)


def _ffn(xb, x, wg, wu, wd, g, bb):
    m = x.shape[0]
    dff = wg.shape[1]
    tm, tf = TM_FFN, TF_FFN
    nf = dff // tf
    row = pl.BlockSpec((tm, D_MODEL), lambda i, f: (i, 0))
    vec = pl.BlockSpec((1, D_MODEL), lambda i, f: (0, 0))
    return pl.pallas_call(
        functools.partial(_ffn_kernel, nf=nf),
        grid=(m // tm, nf),
        in_specs=[row, row,
                  pl.BlockSpec((D_MODEL, tf), lambda i, f: (0, f)),
                  pl.BlockSpec((D_MODEL, tf), lambda i, f: (0, f)),
                  pl.BlockSpec((tf, D_MODEL), lambda i, f: (f, 0)),
                  vec, vec],
        out_specs=[row, row],
        out_shape=[jax.ShapeDtypeStruct((m, D_MODEL), f32), jax.ShapeDtypeStruct((m, D_MODEL), bf16)],
        scratch_shapes=[pltpu.VMEM((tm, D_MODEL), f32)],
        compiler_params=_cparams(("parallel", "arbitrary")),
        name="ffn",
    )(xb, x, wg, wu, wd, g, bb)


def _router_kernel(x_ref, r_ref, tri_ref, gate_ref, rank_ref, *, seq):
    tb = tri_ref.shape[0]
    x = x_ref[0]
    xh = x.astype(bf16)
    xl = (x - xh.astype(f32)).astype(bf16)
    r = r_ref[...]
    rh = r.astype(bf16)
    rl = (r - rh.astype(f32)).astype(bf16)
    logits = _dot(xh, rh) + _dot(xh, rl) + _dot(xl, rh)
    lane = lax.broadcasted_iota(i32, (seq, LANES), 1).astype(f32)
    low = -3.0e38
    lg = jnp.where(lane < N_EXPERTS, logits, low)
    m1 = jnp.max(lg, axis=-1, keepdims=True)
    i1 = jnp.min(jnp.where(lg == m1, lane, float(LANES)), axis=-1, keepdims=True)
    lg2 = jnp.where(lane == i1, low, lg)
    m2 = jnp.max(lg2, axis=-1, keepdims=True)
    i2 = jnp.min(jnp.where(lg2 == m2, lane, float(LANES)), axis=-1, keepdims=True)
    e2 = jnp.exp(m2 - m1)
    den = 1.0 + e2
    gate_ref[0] = jnp.where(lane == i1, 1.0 / den, jnp.where(lane == i2, e2 / den, 0.0))
    sel = (lane == i1) | (lane == i2)
    selb = jnp.where(sel, 1.0, 0.0).astype(bf16)
    carry = jnp.zeros((1, LANES), f32)
    for blk in range(seq // tb):
        sl = slice(blk * tb, (blk + 1) * tb)
        c = _dot(tri_ref[...], selb[sl]) + carry
        carry = c[tb - 1:tb, :]
        rank_ref[0, sl, :] = jnp.where(sel[sl], c - 1.0, -1.0)


def _router(x3, router_pad, tri):
    b, s, _ = x3.shape
    return pl.pallas_call(
        functools.partial(_router_kernel, seq=s),
        grid=(b,),
        in_specs=[pl.BlockSpec((1, s, D_MODEL), lambda i: (i, 0, 0)),
                  pl.BlockSpec((D_MODEL, LANES), lambda i: (0, 0)),
                  pl.BlockSpec(tri.shape, lambda i: (0, 0))],
        out_specs=[pl.BlockSpec((1, s, LANES), lambda i: (i, 0, 0)),
                   pl.BlockSpec((1, s, LANES), lambda i: (i, 0, 0))],
        out_shape=[jax.ShapeDtypeStruct((b, s, LANES), f32),
                   jax.ShapeDtypeStruct((b, s, LANES), f32)],
        compiler_params=_cparams(("parallel",)),
        name="router",
    )(x3, router_pad, tri)


def _moe_kernel(tot_ref, tot_al_ref, off_ref, npc_ref, xb_ref, rankt_ref, gatet_ref, ranke_ref, wg_ref, wu_ref, wd_ref,
                y_ref, xg_ref, acc_ref, *, seq, nf):
    tr = TR_MOE
    half = tr // 2
    nsubc = seq // SUB_MOE
    c = pl.program_id(0)
    e = pl.program_id(1)
    f = pl.program_id(2)
    ce = c * N_EXPERTS + e
    tot = tot_ref[ce]
    rem = tot % tr
    nfull = tot // tr + jnp.where(rem > half, 1, 0)
    has_tail = (rem > 0) & (rem <= half)
    tail0 = pl.multiple_of(nfull * tr, half)

    def pieces(fn):
        for q in range(nsubc):
            base = off_ref[ce * nsubc + q]

            def body(p, carry, q=q, base=base):
                fn(q, pl.multiple_of(base + p * PIECE_MOE, ROW_ALIGN))
                return carry
            lax.fori_loop(0, npc_ref[ce * nsubc + q], body, 0)

    def onehot_rows(q, r0):
        rk = rankt_ref[0, pl.ds(e, 1), q * SUB_MOE:(q + 1) * SUB_MOE]
        want = (r0 + lax.broadcasted_iota(i32, (PIECE_MOE, SUB_MOE), 0)).astype(f32)
        return rk == want

    def gather(q, r0):
        p = jnp.where(onehot_rows(q, r0), 1.0, 0.0).astype(bf16)
        xg_ref[pl.ds(r0, PIECE_MOE), :] += _dot(p, xb_ref[0, q * SUB_MOE:(q + 1) * SUB_MOE, :]).astype(bf16)

    def clear(r0, nr):
        xg_ref[pl.ds(r0, nr), :] = jnp.zeros((nr, D_MODEL), bf16)
        acc_ref[pl.ds(r0, nr), :] = jnp.zeros((nr, D_MODEL), f32)

    def hidden(r0, nr):
        xs = xg_ref[pl.ds(r0, nr), :]
        gate = _dot(xs, wg_ref[0, 0])
        up = _dot(xs, wu_ref[0, 0])
        act = (gate * _sigmoid(gate) * up).astype(bf16)
        acc_ref[pl.ds(r0, nr), :] += _dot(act, wd_ref[0, 0])

    def combine(q, r0):
        grow = gatet_ref[0, pl.ds(e, 1), q * SUB_MOE:(q + 1) * SUB_MOE]
        w = jnp.sum(jnp.where(onehot_rows(q, r0), grow, 0.0), axis=-1, keepdims=True)
        z = (acc_ref[pl.ds(r0, PIECE_MOE), :] * w).astype(bf16)
        want = (r0 + lax.broadcasted_iota(i32, (SUB_MOE, PIECE_MOE), 1)).astype(f32)
        rc = ranke_ref[0, 0, q * SUB_MOE:(q + 1) * SUB_MOE, :]
        pt = jnp.where(rc == want, 1.0, 0.0).astype(bf16)
        y_ref[0, q * SUB_MOE:(q + 1) * SUB_MOE, :] += _dot(pt, z)

    def tiles(fn):
        def body(s, carry):
            fn(pl.multiple_of(s * tr, tr), tr)
            return carry
        lax.fori_loop(0, nfull, body, 0)

        @pl.when(has_tail)
        def _():
            fn(tail0, half)

    def zero(j, carry):
        y_ref[0, pl.ds(j * tr, tr), :] = jnp.zeros((tr, D_MODEL), f32)
        return carry

    lax.fori_loop(0, jnp.where((e == 0) & (f == 0), seq // tr, 0), zero, 0)

    @pl.when(f == 0)
    def _():
        tiles(clear)
        clear(pl.multiple_of(tot_al_ref[ce], ROW_ALIGN), tr)
        pieces(gather)

    tiles(hidden)

    @pl.when(f == nf - 1)
    def _():
        pieces(combine)


def _moe(xb3, gate, rank, wg, wu, wd, layer):
    b, s, _ = xb3.shape
    dff = wg.shape[3]
    nf = dff // TF_MOE
    nsubc = s // SUB_MOE
    rk = rank[:, :, :N_EXPERTS]
    cnt = jnp.sum((rk >= 0.0).reshape(b, nsubc, SUB_MOE, N_EXPERTS), axis=2).astype(i32)
    first = jnp.cumsum(cnt, axis=1) - cnt
    off = first // ROW_ALIGN * ROW_ALIGN
    npc = jnp.where(cnt > 0, (first - off + cnt + PIECE_MOE - 1) // PIECE_MOE, 0)
    post = rk.transpose(0, 2, 1)
    gatet = gate[:, :, :N_EXPERTS].transpose(0, 2, 1)
    tot = jnp.sum(cnt, axis=1).reshape(-1)
    tot_al = (tot + ROW_ALIGN - 1) // ROW_ALIGN * ROW_ALIGN
    off_flat = off.transpose(0, 2, 1).reshape(-1)
    npc = npc.transpose(0, 2, 1).reshape(-1)
    buf_rows = s + ROW_ALIGN + PIECE_MOE + TR_MOE
    grid_spec = pltpu.PrefetchScalarGridSpec(
        num_scalar_prefetch=4,
        grid=(b, N_EXPERTS, nf),
        in_specs=[pl.BlockSpec((1, s, D_MODEL), lambda c, e, f, *_: (c, 0, 0)),
                  pl.BlockSpec((1, N_EXPERTS, s), lambda c, e, f, *_: (c, 0, 0)),
                  pl.BlockSpec((1, N_EXPERTS, s), lambda c, e, f, *_: (c, 0, 0)),
                  pl.BlockSpec((1, 1, s, 1), lambda c, e, f, *_: (c, e, 0, 0)),
                  pl.BlockSpec((1, 1, D_MODEL, TF_MOE), lambda c, e, f, *_: (layer, e, 0, f)),
                  pl.BlockSpec((1, 1, D_MODEL, TF_MOE), lambda c, e, f, *_: (layer, e, 0, f)),
                  pl.BlockSpec((1, 1, TF_MOE, D_MODEL), lambda c, e, f, *_: (layer, e, f, 0))],
        out_specs=pl.BlockSpec((1, s, D_MODEL), lambda c, e, f, *_: (c, 0, 0)),
        scratch_shapes=[pltpu.VMEM((buf_rows, D_MODEL), bf16), pltpu.VMEM((buf_rows, D_MODEL), f32)],
    )
    return pl.pallas_call(
        functools.partial(_moe_kernel, seq=s, nf=nf),
        grid_spec=grid_spec,
        out_shape=jax.ShapeDtypeStruct((b, s, D_MODEL), f32),
        compiler_params=_cparams(("parallel", "arbitrary", "arbitrary")),
        name="moe",
    )(tot, tot_al, off_flat, npc, xb3, post, gatet, post[..., None], wg, wu, wd)


def _resln_kernel(x_ref, y_ref, g_ref, b_ref, xo_ref, xbo_ref):
    xn = _layer_norm(DN_ALPHA * x_ref[...] + y_ref[...], g_ref[...], b_ref[...])
    xo_ref[...] = xn
    xbo_ref[...] = xn.astype(bf16)


def _resln(x, y, g, bb):
    m = x.shape[0]
    tm = TM_FFN
    row = pl.BlockSpec((tm, D_MODEL), lambda i: (i, 0))
    vec = pl.BlockSpec((1, D_MODEL), lambda i: (0, 0))
    return pl.pallas_call(
        _resln_kernel,
        grid=(m // tm,),
        in_specs=[row, row, vec, vec],
        out_specs=[row, row],
        out_shape=[jax.ShapeDtypeStruct((m, D_MODEL), f32), jax.ShapeDtypeStruct((m, D_MODEL), bf16)],
        compiler_params=_cparams(("parallel",)),
        name="resln",
    )(x, y, g, bb)


def _t5_bucket(dist):
    n = jnp.maximum(dist, 0)
    max_exact = REL_BUCKETS // 2
    large = max_exact + (jnp.log(jnp.maximum(n, 1).astype(f32) / max_exact)
                         / math.log(REL_MAX_DIST / max_exact) * (REL_BUCKETS - max_exact)).astype(i32)
    large = jnp.minimum(large, REL_BUCKETS - 1)
    return jnp.where(n < max_exact, n, large)


def _bias_of_dist(rel_bias, dist):
    onehot = (_t5_bucket(dist)[None] == jnp.arange(REL_BUCKETS).reshape((-1,) + (1,) * dist.ndim)).astype(f32)
    return LOG2E * jnp.einsum("kh,k...->h...", rel_bias.astype(f32), onehot, precision=lax.Precision.HIGHEST)


def _bias_tables(rel_bias, seq):
    t = T_ATT
    rows = NSA_GROUP * t
    ncmp = seq // CMP_STRIDE
    d0 = jnp.arange(t)[:, None] - jnp.arange(t)[None, :]
    offs = jnp.array([0, t, 2 * t, WINDOW]).reshape(4, 1, 1)
    kinds = _bias_of_dist(rel_bias, offs + d0[None])
    mask = jnp.stack([d0 >= 0, d0 == d0, d0 == d0, d0 < 0])
    kinds = jnp.where(mask[None], kinds, NEG_INF)
    order = np.array(NSA_ROW_ORDER)
    tbias = kinds.reshape(NSA_KV_HEADS, NSA_GROUP, 4, t, t)[:, order].transpose(0, 2, 1, 3, 4)
    tbias = tbias.reshape(NSA_KV_HEADS, 4, rows, t)
    cend = jnp.arange(ncmp) * CMP_STRIDE + CMP_BLOCK - 1
    bc = _bias_of_dist(rel_bias, jnp.arange(seq)[:, None] - cend[None, :])
    bc = bc.reshape(NSA_KV_HEADS, NSA_GROUP, seq // t, t, ncmp)[:, order].transpose(0, 2, 4, 1, 3)
    return tbias, bc.reshape(NSA_KV_HEADS, (seq // t) * ncmp, rows)


def _selection_constants(seq):
    ncmp = seq // CMP_STRIDE
    nsel = seq // SEL_BLOCK
    c0 = np.arange(ncmp)[:, None] * CMP_STRIDE
    s0 = np.arange(LANES)[None, :] * SEL_BLOCK
    ov = np.maximum(np.minimum(c0 + CMP_BLOCK, s0 + SEL_BLOCK) - np.maximum(c0, s0), 0) / CMP_BLOCK
    ov[ncmp - 1, :] = 0.0
    ov[:, nsel:] = 0.0
    et = (np.arange(seq)[:, None] // SEL_BLOCK == np.arange(LANES)[None, :]).astype(np.float32)
    return jnp.asarray(ov.T, bf16), jnp.asarray(et, bf16)


def _fox_placement():
    xw = FOX_PAIRS * LANES
    pq = np.zeros((3, LANES, xw), np.float32)
    pk = np.zeros((3, LANES, xw), np.float32)
    oq = np.zeros((1, xw), np.float32)
    ok = np.zeros((1, xw), np.float32)
    for p in range(FOX_PAIRS):
        for hh in range(2):
            src = FGATE_LANE + 2 * p + hh
            base = p * LANES + hh * XCOLS
            for part in range(3):
                pk[part, src, base + part] = -1.0
                pq[part, src, base + 3 + part] = 1.0
                oq[0, base + part] = 1.0
                ok[0, base + 3 + part] = 1.0
    return jnp.asarray(pq, bf16), jnp.asarray(pk, bf16), jnp.asarray(oq), jnp.asarray(ok)


def _layer_weights(w_in, layer_pe, w1, w2, f_bias):
    offs = np.cumsum((NSA_Q_W, 6 * 2 * HEAD_DIM, 3 * NSA_HEADS, 3 * FOX_W, FOX_HEADS, 2 * D_MODEL))
    kv0, g0, fx0, ff0, mg0 = offs[0], offs[1], offs[2], offs[3], offs[4]
    scale = HEAD_DIM ** -0.5 * LOG2E
    kvw = NSA_KV_HEADS * HEAD_DIM
    w_kv = w_in[:, kv0:g0]
    w_kvdup = jnp.repeat(w_kv[:, 2 * kvw:].reshape(D_MODEL, 4 * NSA_KV_HEADS, 1, HEAD_DIM), 2, axis=2)
    w_kvdup = w_kvdup.reshape(D_MODEL, 8 * kvw)
    w_qkv = jnp.concatenate([w_in[:, :NSA_Q_W] * scale, w_kvdup, w_in[:, fx0:fx0 + FOX_W] * scale,
                             w_in[:, fx0 + FOX_W:ff0]], axis=1).astype(bf16)
    zeros = lambda n: jnp.zeros((D_MODEL, n), w_in.dtype)
    ng = 3 * NSA_GROUP
    w_g = w_in[:, g0:fx0].reshape(D_MODEL, NSA_KV_HEADS, NSA_GROUP, 3).transpose(0, 1, 3, 2)
    gate_cols = lambda h: w_g[:, h].reshape(D_MODEL, ng)
    w_aux = jnp.concatenate([w_kv[:, :2 * kvw],
                             gate_cols(0), zeros(FGATE_LANE - ng), w_in[:, ff0:mg0],
                             zeros(LANES - FGATE_LANE - FOX_HEADS),
                             gate_cols(1), zeros(LANES - ng)], axis=1).astype(bf16)
    fb_row = jnp.zeros((1, LANES), f32).at[0, FGATE_LANE:FGATE_LANE + FOX_HEADS].set(f_bias.astype(f32))
    pe2 = jnp.tile(layer_pe.astype(f32), (1, 1, NSA_KV_HEADS))
    w1r = w1.reshape(2, CMP_BLOCK, HEAD_DIM, CMP_HIDDEN).astype(bf16)
    zero = jnp.zeros_like(w1r)
    w1bd = jnp.concatenate([jnp.concatenate([w1r, zero], axis=-1),
                            jnp.concatenate([zero, w1r], axis=-1)], axis=-2)
    w2d = jnp.concatenate([w2, w2], axis=-1).astype(bf16)
    return (w_qkv, w_aux, w_in[:, mg0:].astype(bf16), fb_row, pe2, w1bd[:, :CMP_STRIDE], w1bd[:, CMP_STRIDE:],
            w2d)


def kernel(x, w_in, nsa_cmp_pe, nsa_cmp_w1, nsa_cmp_w2, fox_f_bias, w_nsa_branch, w_fox_branch, w_out,
           rel_bias, ln1_g, ln1_b, ln2_g, ln2_b, dense_w_gate, dense_w_up, dense_w_down, moe_router,
           moe_w_gate, moe_w_up, moe_w_down):
    b, s, d = x.shape
    m = b * s
    tbias, bias_c = _bias_tables(rel_bias, s)
    ov, et = _selection_constants(s)
    pq, pk, oq, ok = _fox_placement()
    tri128 = jnp.asarray(np.tril(np.ones((LANES, LANES), np.float32)), bf16)
    tri256 = jnp.asarray(np.tril(np.ones((256, 256), np.float32)), bf16)

    moe_wg, moe_wu, moe_wd = moe_w_gate.astype(bf16), moe_w_up.astype(bf16), moe_w_down.astype(bf16)
    xf = x.reshape(m, d).astype(f32)
    xb = xf.astype(bf16)
    for layer in range(DEPTH):
        w_qkv, w_aux, w_mg, fb_row, pe2, w1t, w1b, w2d = _layer_weights(
            w_in[layer], nsa_cmp_pe[layer], nsa_cmp_w1[layer], nsa_cmp_w2[layer], fox_f_bias[layer])
        qkv, aux = _proj(xb, w_qkv, w_aux, b)
        aux = aux.reshape(b, s, AUX_W)
        gates, qx, kx = _gateprep(aux, fb_row, tri128, pq, pk, oq, ok)
        cmpkv = _compress(aux, pe2, w1t, w1b, w2d)
        y_nsa = _nsa(qkv, cmpkv, gates, bias_c, tbias, ov, et).reshape(m, NSA_Q_W)
        y_fox = _fox(qkv, qx, kx).reshape(m, FOX_W)
        xf, xb = _merge(y_nsa, y_fox, xb, xf, w_nsa_branch[layer].astype(bf16),
                        w_fox_branch[layer].astype(bf16), w_mg, w_out[layer].astype(bf16),
                        ln1_g[layer].reshape(1, d), ln1_b[layer].reshape(1, d))
        j = layer // 2
        g2, b2 = ln2_g[layer].reshape(1, d), ln2_b[layer].reshape(1, d)
        if layer % 2 == 0:
            xf, xb = _ffn(xb, xf, dense_w_gate[j].astype(bf16), dense_w_up[j].astype(bf16),
                          dense_w_down[j].astype(bf16), g2, b2)
        else:
            router_pad = jnp.zeros((d, LANES), f32).at[:, :N_EXPERTS].set(moe_router[j].astype(f32))
            gate, rank = _router(xf.reshape(b, s, d), router_pad, tri256)
            y = _moe(xb.reshape(b, s, d), gate, rank, moe_wg, moe_wu, moe_wd, j)
            xf, xb = _resln(xf, y.reshape(m, d), g2, b2)
    return xf.reshape(b, s, d).astype(x.dtype)
```

```python
import functools
import math

import numpy as np
import jax
import jax.numpy as jnp
from jax import lax
from jax.experimental import pallas as pl
from jax.experimental.pallas import tpu as pltpu

f32 = jnp.float32
bf16 = jnp.bfloat16
i32 = jnp.int32

D_MODEL = 1024
HEAD_DIM = 64
LANES = 128
NSA_HEADS = 8
NSA_KV_HEADS = 2
NSA_GROUP = NSA_HEADS // NSA_KV_HEADS
NSA_ROW_ORDER = (0, 2, 1, 3)
FOX_HEADS = 8
FOX_PAIRS = FOX_HEADS // 2
CMP_BLOCK = 32
CMP_STRIDE = 16
CMP_HIDDEN = 128
SEL_BLOCK = 64
N_SELECT = 16
N_LOCAL_SEL = 2
WINDOW = 512
REL_BUCKETS = 32
REL_MAX_DIST = 128
N_EXPERTS = 8
DEPTH = 4
DN_ALPHA = (2 * DEPTH) ** 0.25
LN_EPS = 1e-5
FORCED_SCORE = 1e4
NEG_INF = -1e30
LOG2E = math.log2(math.e)

NSA_Q_W = NSA_HEADS * HEAD_DIM
FOX_W = FOX_HEADS * HEAD_DIM
QKV_TILES_KV = NSA_Q_W // LANES
QKV_TILES_FOX = QKV_TILES_KV + 4 * NSA_KV_HEADS
QKV_W = (QKV_TILES_FOX + 3 * FOX_PAIRS) * LANES
AUX_W = 4 * LANES
FGATE_LANE = 24
XCOLS = 6

T_ATT = 256
TQ_FOX = 512
TM_PROJ = 512
TM_MERGE = 512
TM_FFN = 512
TF_FFN = 1408
TR_MOE = 256
SUB_MOE = 512
PIECE_MOE = 192
ROW_ALIGN = 16
TF_MOE = 896
VMEM_LIMIT = 56 * 1024 * 1024


def _cparams(sem):
    return pltpu.CompilerParams(dimension_semantics=sem, vmem_limit_bytes=VMEM_LIMIT)


def _dot(a, b):
    return jnp.dot(a, b, preferred_element_type=f32)


def _dot_nt(a, b):
    return lax.dot_general(a, b, (((1,), (1,)), ((), ())), preferred_element_type=f32)


def _sigmoid(x):
    return 1.0 / (1.0 + jnp.exp(-x))


def _layer_norm(z, g, b):
    mu = jnp.mean(z, axis=-1, keepdims=True)
    zc = z - mu
    var = jnp.mean(zc * zc, axis=-1, keepdims=True)
    return zc * lax.rsqrt(var + LN_EPS) * g + b


def _split3(x):
    hi = x.astype(bf16)
    r1 = x - hi.astype(f32)
    mid = r1.astype(bf16)
    lo = (r1 - mid.astype(f32)).astype(bf16)
    return hi, mid, lo


def _proj_kernel(x_ref, w_ref, wa_ref, o_ref, oa_ref):
    x = x_ref[...]
    res = _dot(x, w_ref[...])
    for j in range(o_ref.shape[1]):
        o_ref[0, j] = res[:, j * LANES:(j + 1) * LANES].astype(o_ref.dtype)
    oa_ref[...] = _dot(x, wa_ref[...])


def _proj(xb, w, wa, batch):
    m, k = xb.shape
    n, na = w.shape[1], wa.shape[1]
    nb = m // batch // TM_PROJ
    return pl.pallas_call(
        _proj_kernel,
        grid=(m // TM_PROJ,),
        in_specs=[pl.BlockSpec((TM_PROJ, k), lambda i: (i, 0)),
                  pl.BlockSpec((k, n), lambda i: (0, 0)),
                  pl.BlockSpec((k, na), lambda i: (0, 0))],
        out_specs=[pl.BlockSpec((1, n // LANES, TM_PROJ, LANES), lambda i: (i // nb, 0, i % nb, 0)),
                   pl.BlockSpec((TM_PROJ, na), lambda i: (i, 0))],
        out_shape=[jax.ShapeDtypeStruct((batch, n // LANES, m // batch, LANES), bf16),
                   jax.ShapeDtypeStruct((m, na), f32)],
        compiler_params=_cparams(("parallel",)),
        name="proj",
    )(xb, w, wa)


def _gateprep_kernel(a_ref, fb_ref, tri_ref, pq_ref, pk_ref, oq_ref, ok_ref, g_ref, qx_ref, kx_ref, *, seq):
    tb = LANES
    tri = tri_ref[...]
    local = []
    for blk in range(seq // tb):
        sl = slice(blk * tb, (blk + 1) * tb)
        va = a_ref[0, sl, 0:LANES]
        g_ref[0, 0, sl, :] = _sigmoid(va)
        g_ref[0, 1, sl, :] = _sigmoid(a_ref[0, sl, LANES:2 * LANES])
        z = va + fb_ref[...]
        logf = jnp.minimum(z, 0.0) - jnp.log1p(jnp.exp(-jnp.abs(z)))
        hi, mid, lo = _split3(logf)
        local.append(_dot(tri, hi) + _dot(tri, mid) + _dot(tri, lo))
    carry = jnp.zeros((1, LANES), f32)
    for blk in range(seq // tb):
        sl = slice(blk * tb, (blk + 1) * tb)
        c = local[blk] + carry
        carry = c[tb - 1:tb, :]
        chi, cmid, clo = _split3(c * LOG2E)
        qx = _dot(chi, pq_ref[0]) + _dot(cmid, pq_ref[1]) + _dot(clo, pq_ref[2]) + oq_ref[...]
        kx = _dot(chi, pk_ref[0]) + _dot(cmid, pk_ref[1]) + _dot(clo, pk_ref[2]) + ok_ref[...]
        for p in range(FOX_PAIRS):
            qx_ref[0, p, sl, :] = qx[:, p * LANES:(p + 1) * LANES].astype(bf16)
            kx_ref[0, p, sl, :] = kx[:, p * LANES:(p + 1) * LANES].astype(bf16)


def _gateprep(aux, fb_row, tri, pq, pk, oq, ok):
    b, s, _ = aux.shape
    const2 = lambda a: pl.BlockSpec(a.shape, lambda i: (0, 0))
    const3 = lambda a: pl.BlockSpec(a.shape, lambda i: (0, 0, 0))
    return pl.pallas_call(
        functools.partial(_gateprep_kernel, seq=s),
        grid=(b,),
        in_specs=[pl.BlockSpec((1, s, 2 * LANES), lambda i: (i, 0, 1)),
                  const2(fb_row), const2(tri), const3(pq), const3(pk), const2(oq), const2(ok)],
        out_specs=[pl.BlockSpec((1, 2, s, LANES), lambda i: (i, 0, 0, 0)),
                   pl.BlockSpec((1, FOX_PAIRS, s, LANES), lambda i: (i, 0, 0, 0)),
                   pl.BlockSpec((1, FOX_PAIRS, s, LANES), lambda i: (i, 0, 0, 0))],
        out_shape=[jax.ShapeDtypeStruct((b, 2, s, LANES), f32),
                   jax.ShapeDtypeStruct((b, FOX_PAIRS, s, LANES), bf16),
                   jax.ShapeDtypeStruct((b, FOX_PAIRS, s, LANES), bf16)],
        compiler_params=_cparams(("parallel",)),
        name="gateprep",
    )(aux, fb_row, tri, pq, pk, oq, ok)


def _gelu_tanh(x):
    c = math.sqrt(2.0 / math.pi)
    return x * (0.5 * (1.0 + jnp.tanh(c * (x + 0.044715 * (x * x * x)))))


def _compress_kernel(a_ref, pe_ref, w1t_ref, w1b_ref, w2_ref, o_ref, *, nhalf):
    top = jnp.zeros((nhalf, NSA_KV_HEADS * CMP_HIDDEN), f32)
    bot = jnp.zeros((nhalf, NSA_KV_HEADS * CMP_HIDDEN), f32)
    for l in range(CMP_STRIDE):
        rows = a_ref[0, pl.ds(l, nhalf, stride=CMP_STRIDE), :]
        top = top + _dot((rows + pe_ref[0, l:l + 1, :]).astype(bf16), w1t_ref[0, l])
        bot = bot + _dot((rows + pe_ref[0, CMP_STRIDE + l:CMP_STRIDE + l + 1, :]).astype(bf16), w1b_ref[0, l])
    pre = top + pltpu.roll(bot, nhalf - 1, 0)
    act = _gelu_tanh(pre).astype(bf16)
    for h in range(NSA_KV_HEADS):
        o_ref[0, 0, h] = _dot(act[:, h * CMP_HIDDEN:(h + 1) * CMP_HIDDEN], w2_ref[0]).astype(o_ref.dtype)


def _compress(aux, pe2, w1t, w1b, w2d):
    b, s, _ = aux.shape
    nhalf = s // CMP_STRIDE
    w1_spec = pl.BlockSpec((1,) + w1t.shape[1:], lambda i, j: (j, 0, 0, 0))
    return pl.pallas_call(
        functools.partial(_compress_kernel, nhalf=nhalf),
        grid=(b, 2),
        in_specs=[pl.BlockSpec((1, s, LANES), lambda i, j: (i, 0, j)),
                  pl.BlockSpec((1, CMP_BLOCK, LANES), lambda i, j: (j, 0, 0)),
                  w1_spec, w1_spec,
                  pl.BlockSpec((1, CMP_HIDDEN, LANES), lambda i, j: (j, 0, 0))],
        out_specs=pl.BlockSpec((1, 1, NSA_KV_HEADS, nhalf, LANES), lambda i, j: (i, j, 0, 0, 0)),
        out_shape=jax.ShapeDtypeStruct((b, 2, NSA_KV_HEADS, nhalf, LANES), bf16),
        compiler_params=_cparams(("parallel", "parallel")),
        name="compress",
    )(aux, pe2, w1t, w1b, w2d)


def _flash_init(m_ref, acc_ref):
    m_ref[...] = jnp.full(m_ref.shape, NEG_INF, f32)
    acc_ref[...] = jnp.zeros(acc_ref.shape, f32)


def _ones_values(v):
    lo_half = lax.broadcasted_iota(i32, v.shape, 1) < HEAD_DIM
    vf = v.astype(f32)
    return jnp.where(lo_half, vf, 1.0).astype(bf16), jnp.where(lo_half, 1.0, vf).astype(bf16)


def _flash_step(s, v_lo, v_hi, m_ref, acc_ref):
    nk = s.shape[1] // LANES
    half = s.shape[0] // 2
    cols = [s[:, c * LANES:(c + 1) * LANES] for c in range(nk)]
    mx = cols[0]
    for c in cols[1:]:
        mx = jnp.maximum(mx, c)
    m_old = m_ref[...]
    m_new = jnp.maximum(m_old, jnp.broadcast_to(jnp.max(mx, axis=-1, keepdims=True), m_old.shape))
    alpha = jnp.exp2(m_old - m_new)
    p = jnp.concatenate([jnp.exp2(c - m_new).astype(bf16) for c in cols], axis=1)
    pv = jnp.concatenate([_dot(p[0:half], v_lo), _dot(p[half:], v_hi)], axis=0)
    acc_ref[...] = alpha * acc_ref[...] + pv
    m_ref[...] = m_new


def _flash_loop(first, last, logits_fn, v_fn, s_ref, m_ref, acc_ref, tail_fix=None):
    sa, sb = s_ref.at[0], s_ref.at[1]
    n = last - first + 1
    pairs = (n - 1) // 2

    def step(buf, kt, fix=None):
        s = buf[...] if fix is None else fix(buf[...], kt)
        _flash_step(s, *v_fn(kt), m_ref, acc_ref)

    sa[...] = logits_fn(first)

    def body(j, carry):
        kt = first + 2 * j
        sb[...] = logits_fn(kt + 1)
        step(sa, kt)
        sa[...] = logits_fn(kt + 2)
        step(sb, kt + 1)
        return carry

    lax.fori_loop(0, pairs, body, 0)
    two_left = n - 2 * pairs == 2

    @pl.when(two_left)
    def _():
        sb[...] = logits_fn(last)
        step(sa, last - 1, tail_fix)
        step(sb, last, tail_fix)

    @pl.when(jnp.logical_not(two_left))
    def _():
        step(sa, last, tail_fix)


def _flash_finish(acc_ref):
    acc = acc_ref[...]
    return acc * (1.0 / jnp.maximum(pltpu.roll(acc, HEAD_DIM, 1), 1e-30))


def _nsa_kernel(q_ref, ks_ref, vs_ref, kw_ref, vw_ref, kc_ref, vc_ref, gate_ref, bc_ref, tb_ref, ov_ref,
                et_ref, o_ref, qaug_ref, kaug_ref, m_ref, acc_ref, os_ref, s_ref, vsel_ref, vwin_ref, *,
                seq, ncmp):
    kaug_ref[:, 0:LANES] = ks_ref[0, 0]
    kaug_ref[:, LANES:2 * LANES] = et_ref[...]
    vsel_ref[0], vsel_ref[1] = _ones_values(vs_ref[0, 0])
    vwin_ref[0], vwin_ref[1] = _ones_values(vw_ref[0, 0])

    def query_tile(i, carry):
        _nsa_tile(i, q_ref, kw_ref, kc_ref, vc_ref, gate_ref, bc_ref, tb_ref, ov_ref, o_ref, qaug_ref,
                  kaug_ref, m_ref, acc_ref, os_ref, s_ref, vsel_ref, vwin_ref, seq=seq, ncmp=ncmp)
        return carry

    lax.fori_loop(0, seq // T_ATT, query_tile, 0)


def _nsa_tile(i, q_ref, kw_ref, kc_ref, vc_ref, gate_ref, bc_ref, tb_ref, ov_ref, o_ref, qaug_ref,
              kaug_ref, m_ref, acc_ref, os_ref, s_ref, vsel_ref, vwin_ref, *, seq, ncmp):
    t = T_ATT
    rows = NSA_GROUP * t
    nsel = seq // SEL_BLOCK
    topn = min(N_SELECT, nsel)
    t0 = i * t
    tile_rows = pl.ds(pl.multiple_of(t0, t), t)
    lane = lax.broadcasted_iota(i32, (t, LANES), 1)
    lo_half = lane < HEAD_DIM

    for rb, g in enumerate(NSA_ROW_ORDER):
        qg = q_ref[0, g // 2, tile_rows, :].astype(f32)
        keep = lo_half if g % 2 == 0 else jnp.logical_not(lo_half)
        qaug_ref[rb * t:(rb + 1) * t, 0:LANES] = jnp.where(keep, qg, 0.0).astype(bf16)
    qs = qaug_ref[:, 0:LANES]

    s = _dot_nt(kc_ref[0, 0, 0], qs) + bc_ref[0, pl.ds(pl.multiple_of(i * ncmp, ncmp), ncmp), :]
    row_t = t0 + (lax.broadcasted_iota(i32, (ncmp, rows), 1) & (t - 1))
    cmp_end = lax.broadcasted_iota(i32, (ncmp, rows), 0) * CMP_STRIDE + (CMP_BLOCK - 1)
    vis = cmp_end <= row_t
    s = jnp.where(vis, s, NEG_INF)
    s = s - jnp.max(s, axis=0, keepdims=True)
    e = jnp.where(vis, jnp.exp2(s), 0.0)
    p_c = e * (1.0 / jnp.maximum(jnp.sum(e, axis=0, keepdims=True), 1e-30))
    o_c = _dot(p_c.T.astype(bf16), vc_ref[0, 0, 0])

    psum = p_c[:, 0:t]
    for g in range(1, NSA_GROUP):
        psum = psum + p_c[:, g * t:(g + 1) * t]
    hi = psum.astype(bf16)
    lo = (psum - hi.astype(f32)).astype(bf16)
    imp = (_dot(ov_ref[...], hi) + _dot(ov_ref[...], lo))[0:nsel]
    blk = lax.broadcasted_iota(i32, (nsel, t), 0)
    cur = (t0 + lax.broadcasted_iota(i32, (nsel, t), 1)) >> 6
    forced = (blk == 0) | ((blk <= cur) & (blk > cur - N_LOCAL_SEL))
    val = jnp.where(forced, FORCED_SCORE, jnp.where(blk <= cur, imp, -1.0))
    cnt = jnp.zeros((nsel, t), f32)
    for j in range(nsel):
        vj = val[j:j + 1, :]
        beats = (vj > val) | ((vj == val) & (blk > j))
        cnt = cnt + jnp.where(beats, 1.0, 0.0)
    mneg = jnp.where(cnt < topn, 0.0, NEG_INF)
    mneg = jnp.concatenate([mneg, jnp.zeros((LANES - nsel, t), f32)], axis=0).T.astype(bf16)
    for g in range(NSA_GROUP):
        qaug_ref[g * t:(g + 1) * t, LANES:2 * LANES] = mneg

    _flash_init(m_ref, acc_ref)

    def sel_logits(kt):
        k = kaug_ref[pl.ds(kt * t, t), :]
        return _dot_nt(qaug_ref[...], k) + tb_ref[0, jnp.minimum(i - kt, 2)]

    def values(v_ref):
        return lambda kt: (v_ref[0, pl.ds(kt * t, t), :], v_ref[1, pl.ds(kt * t, t), :])

    _flash_loop(0, i, sel_logits, values(vsel_ref), s_ref, m_ref, acc_ref)
    os_ref[...] = _flash_finish(acc_ref)

    _flash_init(m_ref, acc_ref)
    nwin = WINDOW // t

    def win_logits(kt):
        d = i - kt
        kind = jnp.where(d == nwin, 3, d)
        return _dot_nt(qs, kw_ref[0, 0, pl.ds(kt * t, t), :]) + tb_ref[0, kind]

    _flash_loop(jnp.maximum(i - nwin, 0), i, win_logits, values(vwin_ref), s_ref, m_ref, acc_ref)
    o_w = _flash_finish(acc_ref)
    o_s = os_ref[...]

    gates = gate_ref[0, 0, tile_rows, :]
    outs = []
    for g in range(NSA_GROUP):
        rb = NSA_ROW_ORDER.index(g)
        sl = slice(rb * t, (rb + 1) * t)
        outs.append(gates[:, g:g + 1] * o_c[sl]
                    + gates[:, NSA_GROUP + g:NSA_GROUP + g + 1] * o_s[sl]
                    + gates[:, 2 * NSA_GROUP + g:2 * NSA_GROUP + g + 1] * o_w[sl])
    for j in range(NSA_GROUP // 2):
        o_ref[0, tile_rows, j * LANES:(j + 1) * LANES] = jnp.where(
            lo_half, outs[2 * j], outs[2 * j + 1]).astype(o_ref.dtype)


def _nsa(qkv, cmpkv, gates, bias_c, tbias, ov, et):
    b, _, s, _ = qkv.shape
    ncmp = cmpkv.shape[3]
    t = T_ATT
    rows = NSA_GROUP * t
    qw = NSA_GROUP * HEAD_DIM
    qtiles = qw // LANES
    kv_spec = lambda col: pl.BlockSpec((1, 1, s, LANES), lambda h, bi, col=col: (bi, col + h, 0, 0))
    cmp_spec = lambda kv: pl.BlockSpec((1, 1, 1, ncmp, LANES), lambda h, bi, kv=kv: (bi, kv, h, 0, 0))
    base = QKV_TILES_KV
    return pl.pallas_call(
        functools.partial(_nsa_kernel, seq=s, ncmp=ncmp),
        grid=(NSA_KV_HEADS, b),
        in_specs=[pl.BlockSpec((1, qtiles, s, LANES), lambda h, bi: (bi, h, 0, 0)),
                  kv_spec(base), kv_spec(base + 2), kv_spec(base + 4), kv_spec(base + 6),
                  cmp_spec(0), cmp_spec(1),
                  pl.BlockSpec((1, 1, s, LANES), lambda h, bi: (bi, h, 0, 0)),
                  pl.BlockSpec((1, (s // t) * ncmp, rows), lambda h, bi: (h, 0, 0)),
                  pl.BlockSpec((1, 4, rows, t), lambda h, bi: (h, 0, 0, 0)),
                  pl.BlockSpec((LANES, ncmp), lambda h, bi: (0, 0)),
                  pl.BlockSpec((s, LANES), lambda h, bi: (0, 0))],
        out_specs=pl.BlockSpec((1, s, qw), lambda h, bi: (bi, 0, h)),
        out_shape=jax.ShapeDtypeStruct((b, s, NSA_Q_W), bf16),
        scratch_shapes=[pltpu.VMEM((rows, 2 * LANES), bf16),
                        pltpu.VMEM((s, 2 * LANES), bf16),
                        pltpu.VMEM((rows, LANES), f32),
                        pltpu.VMEM((rows, LANES), f32),
                        pltpu.VMEM((rows, LANES), f32),
                        pltpu.VMEM((2, rows, t), f32),
                        pltpu.VMEM((2, s, LANES), bf16),
                        pltpu.VMEM((2, s, LANES), bf16)],
        compiler_params=_cparams(("parallel", "parallel")),
        name="nsa",
    )(qkv, qkv, qkv, qkv, qkv, cmpkv, cmpkv, gates, bias_c, tbias, ov, et)


def _fox_kernel(q_ref, k_ref, v_ref, qx_ref, kx_ref, o_ref, qaug_ref, kaug_ref, m_ref, acc_ref, s_ref,
                vaug_ref):
    t = T_ATT
    tq = TQ_FOX
    seq = k_ref.shape[2]
    lane = lax.broadcasted_iota(i32, (tq, LANES), 1)
    lo_half = lane < HEAD_DIM
    kaug_ref[:, 0:LANES] = k_ref[0, 0]
    kaug_ref[:, LANES:2 * LANES] = kx_ref[0, 0]
    vaug_ref[0], vaug_ref[1] = _ones_values(v_ref[0, 0])

    def logits(kt):
        return _dot_nt(qaug_ref[...], kaug_ref[pl.ds(kt * t, t), :])

    def values(kt):
        return vaug_ref[0, pl.ds(kt * t, t), :], vaug_ref[1, pl.ds(kt * t, t), :]

    def query_tile(i, carry):
        rows = pl.ds(pl.multiple_of(i * tq, tq), tq)
        q = q_ref[0, 0, rows, :].astype(f32)
        qx = qx_ref[0, 0, rows, :].astype(f32)
        qaug_ref[0:tq, 0:LANES] = jnp.where(lo_half, q, 0.0).astype(bf16)
        qaug_ref[tq:2 * tq, 0:LANES] = jnp.where(lo_half, 0.0, q).astype(bf16)
        qaug_ref[0:tq, LANES:2 * LANES] = jnp.where(lane < XCOLS, qx, 0.0).astype(bf16)
        qaug_ref[tq:2 * tq, LANES:2 * LANES] = jnp.where((lane >= XCOLS) & (lane < 2 * XCOLS), qx, 0.0).astype(bf16)

        def causal(s, kt):
            q_pos = i * tq + (lax.broadcasted_iota(i32, (2 * tq, t), 0) & (tq - 1))
            k_pos = kt * t + lax.broadcasted_iota(i32, (2 * tq, t), 1)
            return jnp.where(k_pos <= q_pos, s, NEG_INF)

        _flash_init(m_ref, acc_ref)
        _flash_loop(0, (i + 1) * (tq // t) - 1, logits, values, s_ref, m_ref, acc_ref, tail_fix=causal)
        o = _flash_finish(acc_ref)
        o_ref[0, rows, :] = jnp.where(lo_half, o[0:tq], o[tq:2 * tq]).astype(o_ref.dtype)
        return carry

    lax.fori_loop(0, seq // tq, query_tile, 0)


def _fox(qkv, qx, kx):
    b, _, s, _ = qkv.shape
    t = T_ATT
    base = QKV_TILES_FOX
    return pl.pallas_call(
        _fox_kernel,
        grid=(b, FOX_PAIRS),
        in_specs=[pl.BlockSpec((1, 1, s, LANES), lambda bi, p: (bi, base + p, 0, 0)),
                  pl.BlockSpec((1, 1, s, LANES), lambda bi, p: (bi, base + FOX_PAIRS + p, 0, 0)),
                  pl.BlockSpec((1, 1, s, LANES), lambda bi, p: (bi, base + 2 * FOX_PAIRS + p, 0, 0)),
                  pl.BlockSpec((1, 1, s, LANES), lambda bi, p: (bi, p, 0, 0)),
                  pl.BlockSpec((1, 1, s, LANES), lambda bi, p: (bi, p, 0, 0))],
        out_specs=pl.BlockSpec((1, s, LANES), lambda bi, p: (bi, 0, p)),
        out_shape=jax.ShapeDtypeStruct((b, s, FOX_W), bf16),
        scratch_shapes=[pltpu.VMEM((2 * TQ_FOX, 2 * LANES), bf16),
                        pltpu.VMEM((s, 2 * LANES), bf16),
                        pltpu.VMEM((2 * TQ_FOX, LANES), f32),
                        pltpu.VMEM((2 * TQ_FOX, LANES), f32),
                        pltpu.VMEM((2, 2 * TQ_FOX, t), f32),
                        pltpu.VMEM((2, s, LANES), bf16)],
        compiler_params=_cparams(("parallel", "parallel")),
        name="fox",
    )(qkv, qkv, qkv, qx, kx)


def _merge_kernel(yn_ref, yf_ref, xb_ref, x_ref, wn_ref, wf_ref, wmg_ref, wo_ref, g_ref, b_ref,
                  xo_ref, xbo_ref):
    mg = _dot(xb_ref[...], wmg_ref[...])
    merged = (_sigmoid(mg[:, 0:D_MODEL]) * _dot(yn_ref[...], wn_ref[...])
              + _sigmoid(mg[:, D_MODEL:2 * D_MODEL]) * _dot(yf_ref[...], wf_ref[...]))
    hmix = _dot(merged.astype(bf16), wo_ref[...])
    xn = _layer_norm(DN_ALPHA * x_ref[...] + hmix, g_ref[...], b_ref[...])
    xo_ref[...] = xn
    xbo_ref[...] = xn.astype(bf16)


def _merge(yn, yf, xb, x, wn, wf, wmg, wo, g, bb):
    m = x.shape[0]
    tm = TM_MERGE
    row = lambda w: pl.BlockSpec((tm, w), lambda i: (i, 0))
    full = lambda a: pl.BlockSpec(a.shape, lambda i: (0, 0))
    return pl.pallas_call(
        _merge_kernel,
        grid=(m // tm,),
        in_specs=[row(NSA_Q_W), row(FOX_W), row(D_MODEL), row(D_MODEL),
                  full(wn), full(wf), full(wmg), full(wo), full(g), full(bb)],
        out_specs=[row(D_MODEL), row(D_MODEL)],
        out_shape=[jax.ShapeDtypeStruct((m, D_MODEL), f32), jax.ShapeDtypeStruct((m, D_MODEL), bf16)],
        compiler_params=_cparams(("parallel",)),
        name="merge",
    )(yn, yf, xb, x, wn, wf, wmg, wo, g, bb)


def _ffn_kernel(xb_ref, x_ref, wg_ref, wu_ref, wd_ref, g_ref, b_ref, xo_ref, xbo_ref, acc_ref, *, nf):
    f = pl.program_id(1)

    @pl.when(f == 0)
    def _():
        acc_ref[...] = jnp.zeros_like(acc_ref)

    xb = xb_ref[...]
    gate = _dot(xb, wg_ref[...])
    up = _dot(xb, wu_ref[...])
    act = (gate * _sigmoid(gate) * up).astype(bf16)
    acc_ref[...] += _dot(act, wd_ref[...])

    @pl.when(f == nf - 1)
    def _():
        xn = _layer_norm(DN_ALPHA * x_ref[...] + acc_ref[...], g_ref[...], b_ref[...])
        xo_ref[...] = xn
        xbo_ref[...] = xn.astype(bf16)


def _ffn(xb, x, wg, wu, wd, g, bb):
    m = x.shape[0]
    dff = wg.shape[1]
    tm, tf = TM_FFN, TF_FFN
    nf = dff // tf
    row = pl.BlockSpec((tm, D_MODEL), lambda i, f: (i, 0))
    vec = pl.BlockSpec((1, D_MODEL), lambda i, f: (0, 0))
    return pl.pallas_call(
        functools.partial(_ffn_kernel, nf=nf),
        grid=(m // tm, nf),
        in_specs=[row, row,
                  pl.BlockSpec((D_MODEL, tf), lambda i, f: (0, f)),
                  pl.BlockSpec((D_MODEL, tf), lambda i, f: (0, f)),
                  pl.BlockSpec((tf, D_MODEL), lambda i, f: (f, 0)),
                  vec, vec],
        out_specs=[row, row],
        out_shape=[jax.ShapeDtypeStruct((m, D_MODEL), f32), jax.ShapeDtypeStruct((m, D_MODEL), bf16)],
        scratch_shapes=[pltpu.VMEM((tm, D_MODEL), f32)],
        compiler_params=_cparams(("parallel", "arbitrary")),
        name="ffn",
    )(xb, x, wg, wu, wd, g, bb)


def _router_kernel(x_ref, r_ref, tri_ref, gate_ref, rank_ref, *, seq):
    tb = tri_ref.shape[0]
    x = x_ref[0]
    xh = x.astype(bf16)
    xl = (x - xh.astype(f32)).astype(bf16)
    r = r_ref[...]
    rh = r.astype(bf16)
    rl = (r - rh.astype(f32)).astype(bf16)
    logits = _dot(xh, rh) + _dot(xh, rl) + _dot(xl, rh)
    lane = lax.broadcasted_iota(i32, (seq, LANES), 1).astype(f32)
    low = -3.0e38
    lg = jnp.where(lane < N_EXPERTS, logits, low)
    m1 = jnp.max(lg, axis=-1, keepdims=True)
    i1 = jnp.min(jnp.where(lg == m1, lane, float(LANES)), axis=-1, keepdims=True)
    lg2 = jnp.where(lane == i1, low, lg)
    m2 = jnp.max(lg2, axis=-1, keepdims=True)
    i2 = jnp.min(jnp.where(lg2 == m2, lane, float(LANES)), axis=-1, keepdims=True)
    e2 = jnp.exp(m2 - m1)
    den = 1.0 + e2
    gate_ref[0] = jnp.where(lane == i1, 1.0 / den, jnp.where(lane == i2, e2 / den, 0.0))
    sel = (lane == i1) | (lane == i2)
    selb = jnp.where(sel, 1.0, 0.0).astype(bf16)
    carry = jnp.zeros((1, LANES), f32)
    for blk in range(seq // tb):
        sl = slice(blk * tb, (blk + 1) * tb)
        c = _dot(tri_ref[...], selb[sl]) + carry
        carry = c[tb - 1:tb, :]
        rank_ref[0, sl, :] = jnp.where(sel[sl], c - 1.0, -1.0)


def _router(x3, router_pad, tri):
    b, s, _ = x3.shape
    return pl.pallas_call(
        functools.partial(_router_kernel, seq=s),
        grid=(b,),
        in_specs=[pl.BlockSpec((1, s, D_MODEL), lambda i: (i, 0, 0)),
                  pl.BlockSpec((D_MODEL, LANES), lambda i: (0, 0)),
                  pl.BlockSpec(tri.shape, lambda i: (0, 0))],
        out_specs=[pl.BlockSpec((1, s, LANES), lambda i: (i, 0, 0)),
                   pl.BlockSpec((1, s, LANES), lambda i: (i, 0, 0))],
        out_shape=[jax.ShapeDtypeStruct((b, s, LANES), f32),
                   jax.ShapeDtypeStruct((b, s, LANES), f32)],
        compiler_params=_cparams(("parallel",)),
        name="router",
    )(x3, router_pad, tri)


def _moe_kernel(tot_ref, tot_al_ref, off_ref, npc_ref, xb_ref, rankt_ref, gatet_ref, wg_ref, wu_ref, wd_ref,
                y_ref, xg_ref, acc_ref, *, seq, nf):
    tr = TR_MOE
    half = tr // 2
    nsubc = seq // SUB_MOE
    c = pl.program_id(0)
    e = pl.program_id(1)
    f = pl.program_id(2)
    ce = c * N_EXPERTS + e
    tot = tot_ref[ce]
    rem = tot % tr
    nfull = tot // tr + jnp.where(rem > half, 1, 0)
    has_tail = (rem > 0) & (rem <= half)
    tail0 = pl.multiple_of(nfull * tr, half)

    def pieces(fn):
        for q in range(nsubc):
            base = off_ref[ce * nsubc + q]

            def body(p, carry, q=q, base=base):
                fn(q, pl.multiple_of(base + p * PIECE_MOE, ROW_ALIGN))
                return carry
            lax.fori_loop(0, npc_ref[ce * nsubc + q], body, 0)

    def onehot_rows(q, r0):
        rk = rankt_ref[0, pl.ds(e, 1), q * SUB_MOE:(q + 1) * SUB_MOE]
        want = (r0 + lax.broadcasted_iota(i32, (PIECE_MOE, SUB_MOE), 0)).astype(f32)
        return rk == want

    def gather(q, r0):
        p = jnp.where(onehot_rows(q, r0), 1.0, 0.0).astype(bf16)
        xg_ref[pl.ds(r0, PIECE_MOE), :] += _dot(p, xb_ref[0, q * SUB_MOE:(q + 1) * SUB_MOE, :]).astype(bf16)

    def clear(r0, nr):
        xg_ref[pl.ds(r0, nr), :] = jnp.zeros((nr, D_MODEL), bf16)
        acc_ref[pl.ds(r0, nr), :] = jnp.zeros((nr, D_MODEL), f32)

    def hidden(r0, nr):
        xs = xg_ref[pl.ds(r0, nr), :]
        gate = _dot(xs, wg_ref[0, 0])
        up = _dot(xs, wu_ref[0, 0])
        act = (gate * _sigmoid(gate) * up).astype(bf16)
        acc_ref[pl.ds(r0, nr), :] += _dot(act, wd_ref[0, 0])

    def combine(q, r0):
        grow = gatet_ref[0, pl.ds(e, 1), q * SUB_MOE:(q + 1) * SUB_MOE]
        hit = onehot_rows(q, r0)
        w = jnp.sum(jnp.where(hit, grow, 0.0), axis=-1, keepdims=True)
        z = (acc_ref[pl.ds(r0, PIECE_MOE), :] * w).astype(bf16)
        p = jnp.where(hit, 1.0, 0.0).astype(bf16)
        y_ref[0, q * SUB_MOE:(q + 1) * SUB_MOE, :] += lax.dot_general(
            p, z, (((0,), (0,)), ((), ())), preferred_element_type=f32)

    def tiles(fn):
        def body(s, carry):
            fn(pl.multiple_of(s * tr, tr), tr)
            return carry
        lax.fori_loop(0, nfull, body, 0)

        @pl.when(has_tail)
        def _():
            fn(tail0, half)

    def zero(j, carry):
        y_ref[0, pl.ds(j * tr, tr), :] = jnp.zeros((tr, D_MODEL), f32)
        return carry

    lax.fori_loop(0, jnp.where((e == 0) & (f == 0), seq // tr, 0), zero, 0)

    @pl.when(f == 0)
    def _():
        tiles(clear)
        clear(pl.multiple_of(tot_al_ref[ce], ROW_ALIGN), tr)
        pieces(gather)

    tiles(hidden)

    @pl.when(f == nf - 1)
    def _():
        pieces(combine)


def _moe(xb3, gate, rank, wg, wu, wd, layer):
    b, s, _ = xb3.shape
    dff = wg.shape[3]
    nf = dff // TF_MOE
    nsubc = s // SUB_MOE
    rk = rank[:, :, :N_EXPERTS]
    cnt = jnp.sum((rk >= 0.0).reshape(b, nsubc, SUB_MOE, N_EXPERTS), axis=2).astype(i32)
    first = jnp.cumsum(cnt, axis=1) - cnt
    off = first // ROW_ALIGN * ROW_ALIGN
    npc = jnp.where(cnt > 0, (first - off + cnt + PIECE_MOE - 1) // PIECE_MOE, 0)
    post = rk.transpose(0, 2, 1)
    gatet = gate[:, :, :N_EXPERTS].transpose(0, 2, 1)
    tot = jnp.sum(cnt, axis=1).reshape(-1)
    tot_al = (tot + ROW_ALIGN - 1) // ROW_ALIGN * ROW_ALIGN
    off_flat = off.transpose(0, 2, 1).reshape(-1)
    npc = npc.transpose(0, 2, 1).reshape(-1)
    buf_rows = s + ROW_ALIGN + PIECE_MOE + TR_MOE
    grid_spec = pltpu.PrefetchScalarGridSpec(
        num_scalar_prefetch=4,
        grid=(b, N_EXPERTS, nf),
        in_specs=[pl.BlockSpec((1, s, D_MODEL), lambda c, e, f, *_: (c, 0, 0)),
                  pl.BlockSpec((1, N_EXPERTS, s), lambda c, e, f, *_: (c, 0, 0)),
                  pl.BlockSpec((1, N_EXPERTS, s), lambda c, e, f, *_: (c, 0, 0)),
                  pl.BlockSpec((1, 1, D_MODEL, TF_MOE), lambda c, e, f, *_: (layer, e, 0, f)),
                  pl.BlockSpec((1, 1, D_MODEL, TF_MOE), lambda c, e, f, *_: (layer, e, 0, f)),
                  pl.BlockSpec((1, 1, TF_MOE, D_MODEL), lambda c, e, f, *_: (layer, e, f, 0))],
        out_specs=pl.BlockSpec((1, s, D_MODEL), lambda c, e, f, *_: (c, 0, 0)),
        scratch_shapes=[pltpu.VMEM((buf_rows, D_MODEL), bf16), pltpu.VMEM((buf_rows, D_MODEL), f32)],
    )
    return pl.pallas_call(
        functools.partial(_moe_kernel, seq=s, nf=nf),
        grid_spec=grid_spec,
        out_shape=jax.ShapeDtypeStruct((b, s, D_MODEL), f32),
        compiler_params=_cparams(("parallel", "arbitrary", "arbitrary")),
        name="moe",
    )(tot, tot_al, off_flat, npc, xb3, post, gatet, wg, wu, wd)


def _resln_kernel(x_ref, y_ref, g_ref, b_ref, xo_ref, xbo_ref):
    xn = _layer_norm(DN_ALPHA * x_ref[...] + y_ref[...], g_ref[...], b_ref[...])
    xo_ref[...] = xn
    xbo_ref[...] = xn.astype(bf16)


def _resln(x, y, g, bb):
    m = x.shape[0]
    tm = TM_FFN
    row = pl.BlockSpec((tm, D_MODEL), lambda i: (i, 0))
    vec = pl.BlockSpec((1, D_MODEL), lambda i: (0, 0))
    return pl.pallas_call(
        _resln_kernel,
        grid=(m // tm,),
        in_specs=[row, row, vec, vec],
        out_specs=[row, row],
        out_shape=[jax.ShapeDtypeStruct((m, D_MODEL), f32), jax.ShapeDtypeStruct((m, D_MODEL), bf16)],
        compiler_params=_cparams(("parallel",)),
        name="resln",
    )(x, y, g, bb)


def _t5_bucket(dist):
    n = jnp.maximum(dist, 0)
    max_exact = REL_BUCKETS // 2
    large = max_exact + (jnp.log(jnp.maximum(n, 1).astype(f32) / max_exact)
                         / math.log(REL_MAX_DIST / max_exact) * (REL_BUCKETS - max_exact)).astype(i32)
    large = jnp.minimum(large, REL_BUCKETS - 1)
    return jnp.where(n < max_exact, n, large)


def _bias_of_dist(rel_bias, dist):
    onehot = (_t5_bucket(dist)[None] == jnp.arange(REL_BUCKETS).reshape((-1,) + (1,) * dist.ndim)).astype(f32)
    return LOG2E * jnp.einsum("kh,k...->h...", rel_bias.astype(f32), onehot, precision=lax.Precision.HIGHEST)


def _bias_tables(rel_bias, seq):
    t = T_ATT
    rows = NSA_GROUP * t
    ncmp = seq // CMP_STRIDE
    d0 = jnp.arange(t)[:, None] - jnp.arange(t)[None, :]
    offs = jnp.array([0, t, 2 * t, WINDOW]).reshape(4, 1, 1)
    kinds = _bias_of_dist(rel_bias, offs + d0[None])
    mask = jnp.stack([d0 >= 0, d0 == d0, d0 == d0, d0 < 0])
    kinds = jnp.where(mask[None], kinds, NEG_INF)
    order = np.array(NSA_ROW_ORDER)
    tbias = kinds.reshape(NSA_KV_HEADS, NSA_GROUP, 4, t, t)[:, order].transpose(0, 2, 1, 3, 4)
    tbias = tbias.reshape(NSA_KV_HEADS, 4, rows, t)
    cend = jnp.arange(ncmp) * CMP_STRIDE + CMP_BLOCK - 1
    bc = _bias_of_dist(rel_bias, jnp.arange(seq)[:, None] - cend[None, :])
    bc = bc.reshape(NSA_KV_HEADS, NSA_GROUP, seq // t, t, ncmp)[:, order].transpose(0, 2, 4, 1, 3)
    return tbias, bc.reshape(NSA_KV_HEADS, (seq // t) * ncmp, rows)


def _selection_constants(seq):
    ncmp = seq // CMP_STRIDE
    nsel = seq // SEL_BLOCK
    c0 = np.arange(ncmp)[:, None] * CMP_STRIDE
    s0 = np.arange(LANES)[None, :] * SEL_BLOCK
    ov = np.maximum(np.minimum(c0 + CMP_BLOCK, s0 + SEL_BLOCK) - np.maximum(c0, s0), 0) / CMP_BLOCK
    ov[ncmp - 1, :] = 0.0
    ov[:, nsel:] = 0.0
    et = (np.arange(seq)[:, None] // SEL_BLOCK == np.arange(LANES)[None, :]).astype(np.float32)
    return jnp.asarray(ov.T, bf16), jnp.asarray(et, bf16)


def _fox_placement():
    xw = FOX_PAIRS * LANES
    pq = np.zeros((3, LANES, xw), np.float32)
    pk = np.zeros((3, LANES, xw), np.float32)
    oq = np.zeros((1, xw), np.float32)
    ok = np.zeros((1, xw), np.float32)
    for p in range(FOX_PAIRS):
        for hh in range(2):
            src = FGATE_LANE + 2 * p + hh
            base = p * LANES + hh * XCOLS
            for part in range(3):
                pk[part, src, base + part] = -1.0
                pq[part, src, base + 3 + part] = 1.0
                oq[0, base + part] = 1.0
                ok[0, base + 3 + part] = 1.0
    return jnp.asarray(pq, bf16), jnp.asarray(pk, bf16), jnp.asarray(oq), jnp.asarray(ok)


def _layer_weights(w_in, layer_pe, w1, w2, f_bias):
    offs = np.cumsum((NSA_Q_W, 6 * 2 * HEAD_DIM, 3 * NSA_HEADS, 3 * FOX_W, FOX_HEADS, 2 * D_MODEL))
    kv0, g0, fx0, ff0, mg0 = offs[0], offs[1], offs[2], offs[3], offs[4]
    scale = HEAD_DIM ** -0.5 * LOG2E
    kvw = NSA_KV_HEADS * HEAD_DIM
    w_kv = w_in[:, kv0:g0]
    w_kvdup = jnp.repeat(w_kv[:, 2 * kvw:].reshape(D_MODEL, 4 * NSA_KV_HEADS, 1, HEAD_DIM), 2, axis=2)
    w_kvdup = w_kvdup.reshape(D_MODEL, 8 * kvw)
    w_qkv = jnp.concatenate([w_in[:, :NSA_Q_W] * scale, w_kvdup, w_in[:, fx0:fx0 + FOX_W] * scale,
                             w_in[:, fx0 + FOX_W:ff0]], axis=1).astype(bf16)
    zeros = lambda n: jnp.zeros((D_MODEL, n), w_in.dtype)
    ng = 3 * NSA_GROUP
    w_g = w_in[:, g0:fx0].reshape(D_MODEL, NSA_KV_HEADS, NSA_GROUP, 3).transpose(0, 1, 3, 2)
    gate_cols = lambda h: w_g[:, h].reshape(D_MODEL, ng)
    w_aux = jnp.concatenate([w_kv[:, :2 * kvw],
                             gate_cols(0), zeros(FGATE_LANE - ng), w_in[:, ff0:mg0],
                             zeros(LANES - FGATE_LANE - FOX_HEADS),
                             gate_cols(1), zeros(LANES - ng)], axis=1).astype(bf16)
    fb_row = jnp.zeros((1, LANES), f32).at[0, FGATE_LANE:FGATE_LANE + FOX_HEADS].set(f_bias.astype(f32))
    pe2 = jnp.tile(layer_pe.astype(f32), (1, 1, NSA_KV_HEADS))
    w1r = w1.reshape(2, CMP_BLOCK, HEAD_DIM, CMP_HIDDEN).astype(bf16)
    zero = jnp.zeros_like(w1r)
    w1bd = jnp.concatenate([jnp.concatenate([w1r, zero], axis=-1),
                            jnp.concatenate([zero, w1r], axis=-1)], axis=-2)
    w2d = jnp.concatenate([w2, w2], axis=-1).astype(bf16)
    return (w_qkv, w_aux, w_in[:, mg0:].astype(bf16), fb_row, pe2, w1bd[:, :CMP_STRIDE], w1bd[:, CMP_STRIDE:],
            w2d)


def kernel(x, w_in, nsa_cmp_pe, nsa_cmp_w1, nsa_cmp_w2, fox_f_bias, w_nsa_branch, w_fox_branch, w_out,
           rel_bias, ln1_g, ln1_b, ln2_g, ln2_b, dense_w_gate, dense_w_up, dense_w_down, moe_router,
           moe_w_gate, moe_w_up, moe_w_down):
    b, s, d = x.shape
    m = b * s
    tbias, bias_c = _bias_tables(rel_bias, s)
    ov, et = _selection_constants(s)
    pq, pk, oq, ok = _fox_placement()
    tri128 = jnp.asarray(np.tril(np.ones((LANES, LANES), np.float32)), bf16)
    tri256 = jnp.asarray(np.tril(np.ones((256, 256), np.float32)), bf16)

    moe_wg, moe_wu, moe_wd = moe_w_gate.astype(bf16), moe_w_up.astype(bf16), moe_w_down.astype(bf16)
    xf = x.reshape(m, d).astype(f32)
    xb = xf.astype(bf16)
    for layer in range(DEPTH):
        w_qkv, w_aux, w_mg, fb_row, pe2, w1t, w1b, w2d = _layer_weights(
            w_in[layer], nsa_cmp_pe[layer], nsa_cmp_w1[layer], nsa_cmp_w2[layer], fox_f_bias[layer])
        qkv, aux = _proj(xb, w_qkv, w_aux, b)
        aux = aux.reshape(b, s, AUX_W)
        gates, qx, kx = _gateprep(aux, fb_row, tri128, pq, pk, oq, ok)
        cmpkv = _compress(aux, pe2, w1t, w1b, w2d)
        y_nsa = _nsa(qkv, cmpkv, gates, bias_c, tbias, ov, et).reshape(m, NSA_Q_W)
        y_fox = _fox(qkv, qx, kx).reshape(m, FOX_W)
        xf, xb = _merge(y_nsa, y_fox, xb, xf, w_nsa_branch[layer].astype(bf16),
                        w_fox_branch[layer].astype(bf16), w_mg, w_out[layer].astype(bf16),
                        ln1_g[layer].reshape(1, d), ln1_b[layer].reshape(1, d))
        j = layer // 2
        g2, b2 = ln2_g[layer].reshape(1, d), ln2_b[layer].reshape(1, d)
        if layer % 2 == 0:
            xf, xb = _ffn(xb, xf, dense_w_gate[j].astype(bf16), dense_w_up[j].astype(bf16),
                          dense_w_down[j].astype(bf16), g2, b2)
        else:
            router_pad = jnp.zeros((d, LANES), f32).at[:, :N_EXPERTS].set(moe_router[j].astype(f32))
            gate, rank = _router(xf.reshape(b, s, d), router_pad, tri256)
            y = _moe(xb.reshape(b, s, d), gate, rank, moe_wg, moe_wu, moe_wd, j)
            xf, xb = _resln(xf, y.reshape(m, d), g2, b2)
    return xf.reshape(b, s, d).astype(x.dtype)
```

```python
import functools
import math

import numpy as np
import jax
import jax.numpy as jnp
from jax import lax
from jax.experimental import pallas as pl
from jax.experimental.pallas import tpu as pltpu

f32 = jnp.float32
bf16 = jnp.bfloat16
i32 = jnp.int32

D_MODEL = 1024
HEAD_DIM = 64
LANES = 128
NSA_HEADS = 8
NSA_KV_HEADS = 2
NSA_GROUP = NSA_HEADS // NSA_KV_HEADS
NSA_ROW_ORDER = (0, 2, 1, 3)
FOX_HEADS = 8
FOX_PAIRS = FOX_HEADS // 2
CMP_BLOCK = 32
CMP_STRIDE = 16
CMP_HIDDEN = 128
SEL_BLOCK = 64
N_SELECT = 16
N_LOCAL_SEL = 2
WINDOW = 512
REL_BUCKETS = 32
REL_MAX_DIST = 128
N_EXPERTS = 8
DEPTH = 4
DN_ALPHA = (2 * DEPTH) ** 0.25
LN_EPS = 1e-5
FORCED_SCORE = 1e4
NEG_INF = -1e30
LOG2E = math.log2(math.e)

NSA_Q_W = NSA_HEADS * HEAD_DIM
FOX_W = FOX_HEADS * HEAD_DIM
QKV_TILES_KV = NSA_Q_W // LANES
QKV_TILES_FOX = QKV_TILES_KV + 4 * NSA_KV_HEADS
QKV_W = (QKV_TILES_FOX + 3 * FOX_PAIRS) * LANES
AUX_W = 4 * LANES
FGATE_LANE = 24
XCOLS = 6

T_ATT = 256
TQ_FOX = 512
TK_FOX = 512
TM_PROJ = 512
TM_MERGE = 512
TM_FFN = 512
TF_FFN = 1408
TR_MOE = 256
SUB_MOE = 512
PIECE_MOE = 192
ROW_ALIGN = 16
TF_MOE = 896
VMEM_LIMIT = 56 * 1024 * 1024


def _cparams(sem):
    return pltpu.CompilerParams(dimension_semantics=sem, vmem_limit_bytes=VMEM_LIMIT)


def _dot(a, b):
    return jnp.dot(a, b, preferred_element_type=f32)


def _dot_nt(a, b):
    return lax.dot_general(a, b, (((1,), (1,)), ((), ())), preferred_element_type=f32)


def _sigmoid(x):
    return 1.0 / (1.0 + jnp.exp(-x))


def _layer_norm(z, g, b):
    mu = jnp.mean(z, axis=-1, keepdims=True)
    zc = z - mu
    var = jnp.mean(zc * zc, axis=-1, keepdims=True)
    return zc * lax.rsqrt(var + LN_EPS) * g + b


def _split3(x):
    hi = x.astype(bf16)
    r1 = x - hi.astype(f32)
    mid = r1.astype(bf16)
    lo = (r1 - mid.astype(f32)).astype(bf16)
    return hi, mid, lo


def _proj_kernel(x_ref, w_ref, wa_ref, o_ref, oa_ref):
    x = x_ref[...]
    res = _dot(x, w_ref[...])
    for j in range(o_ref.shape[1]):
        o_ref[0, j] = res[:, j * LANES:(j + 1) * LANES].astype(o_ref.dtype)
    oa_ref[...] = _dot(x, wa_ref[...])


def _proj(xb, w, wa, batch):
    m, k = xb.shape
    n, na = w.shape[1], wa.shape[1]
    nb = m // batch // TM_PROJ
    return pl.pallas_call(
        _proj_kernel,
        grid=(m // TM_PROJ,),
        in_specs=[pl.BlockSpec((TM_PROJ, k), lambda i: (i, 0)),
                  pl.BlockSpec((k, n), lambda i: (0, 0)),
                  pl.BlockSpec((k, na), lambda i: (0, 0))],
        out_specs=[pl.BlockSpec((1, n // LANES, TM_PROJ, LANES), lambda i: (i // nb, 0, i % nb, 0)),
                   pl.BlockSpec((TM_PROJ, na), lambda i: (i, 0))],
        out_shape=[jax.ShapeDtypeStruct((batch, n // LANES, m // batch, LANES), bf16),
                   jax.ShapeDtypeStruct((m, na), f32)],
        compiler_params=_cparams(("parallel",)),
        name="proj",
    )(xb, w, wa)


def _gateprep_kernel(a_ref, fb_ref, tri_ref, pq_ref, pk_ref, oq_ref, ok_ref, g_ref, qx_ref, kx_ref, *, seq):
    tb = LANES
    tri = tri_ref[...]
    local = []
    for blk in range(seq // tb):
        sl = slice(blk * tb, (blk + 1) * tb)
        va = a_ref[0, sl, 0:LANES]
        g_ref[0, 0, sl, :] = _sigmoid(va)
        g_ref[0, 1, sl, :] = _sigmoid(a_ref[0, sl, LANES:2 * LANES])
        z = va + fb_ref[...]
        logf = jnp.minimum(z, 0.0) - jnp.log1p(jnp.exp(-jnp.abs(z)))
        hi, mid, lo = _split3(logf)
        local.append(_dot(tri, hi) + _dot(tri, mid) + _dot(tri, lo))
    carry = jnp.zeros((1, LANES), f32)
    for blk in range(seq // tb):
        sl = slice(blk * tb, (blk + 1) * tb)
        c = local[blk] + carry
        carry = c[tb - 1:tb, :]
        chi, cmid, clo = _split3(c * LOG2E)
        qx = _dot(chi, pq_ref[0]) + _dot(cmid, pq_ref[1]) + _dot(clo, pq_ref[2]) + oq_ref[...]
        kx = _dot(chi, pk_ref[0]) + _dot(cmid, pk_ref[1]) + _dot(clo, pk_ref[2]) + ok_ref[...]
        for p in range(FOX_PAIRS):
            qx_ref[0, p, sl, :] = qx[:, p * LANES:(p + 1) * LANES].astype(bf16)
            kx_ref[0, p, sl, :] = kx[:, p * LANES:(p + 1) * LANES].astype(bf16)


def _gateprep(aux, fb_row, tri, pq, pk, oq, ok):
    b, s, _ = aux.shape
    const2 = lambda a: pl.BlockSpec(a.shape, lambda i: (0, 0))
    const3 = lambda a: pl.BlockSpec(a.shape, lambda i: (0, 0, 0))
    return pl.pallas_call(
        functools.partial(_gateprep_kernel, seq=s),
        grid=(b,),
        in_specs=[pl.BlockSpec((1, s, 2 * LANES), lambda i: (i, 0, 1)),
                  const2(fb_row), const2(tri), const3(pq), const3(pk), const2(oq), const2(ok)],
        out_specs=[pl.BlockSpec((1, 2, s, LANES), lambda i: (i, 0, 0, 0)),
                   pl.BlockSpec((1, FOX_PAIRS, s, LANES), lambda i: (i, 0, 0, 0)),
                   pl.BlockSpec((1, FOX_PAIRS, s, LANES), lambda i: (i, 0, 0, 0))],
        out_shape=[jax.ShapeDtypeStruct((b, 2, s, LANES), f32),
                   jax.ShapeDtypeStruct((b, FOX_PAIRS, s, LANES), bf16),
                   jax.ShapeDtypeStruct((b, FOX_PAIRS, s, LANES), bf16)],
        compiler_params=_cparams(("parallel",)),
        name="gateprep",
    )(aux, fb_row, tri, pq, pk, oq, ok)


def _gelu_tanh(x):
    c = math.sqrt(2.0 / math.pi)
    return x * (0.5 * (1.0 + jnp.tanh(c * (x + 0.044715 * (x * x * x)))))


def _compress_kernel(a_ref, pe_ref, w1t_ref, w1b_ref, w2_ref, o_ref, *, nhalf):
    top = jnp.zeros((nhalf, NSA_KV_HEADS * CMP_HIDDEN), f32)
    bot = jnp.zeros((nhalf, NSA_KV_HEADS * CMP_HIDDEN), f32)
    for l in range(CMP_STRIDE):
        rows = a_ref[0, pl.ds(l, nhalf, stride=CMP_STRIDE), :]
        top = top + _dot((rows + pe_ref[0, l:l + 1, :]).astype(bf16), w1t_ref[0, l])
        bot = bot + _dot((rows + pe_ref[0, CMP_STRIDE + l:CMP_STRIDE + l + 1, :]).astype(bf16), w1b_ref[0, l])
    pre = top + pltpu.roll(bot, nhalf - 1, 0)
    act = _gelu_tanh(pre).astype(bf16)
    for h in range(NSA_KV_HEADS):
        o_ref[0, 0, h] = _dot(act[:, h * CMP_HIDDEN:(h + 1) * CMP_HIDDEN], w2_ref[0]).astype(o_ref.dtype)


def _compress(aux, pe2, w1t, w1b, w2d):
    b, s, _ = aux.shape
    nhalf = s // CMP_STRIDE
    w1_spec = pl.BlockSpec((1,) + w1t.shape[1:], lambda i, j: (j, 0, 0, 0))
    return pl.pallas_call(
        functools.partial(_compress_kernel, nhalf=nhalf),
        grid=(b, 2),
        in_specs=[pl.BlockSpec((1, s, LANES), lambda i, j: (i, 0, j)),
                  pl.BlockSpec((1, CMP_BLOCK, LANES), lambda i, j: (j, 0, 0)),
                  w1_spec, w1_spec,
                  pl.BlockSpec((1, CMP_HIDDEN, LANES), lambda i, j: (j, 0, 0))],
        out_specs=pl.BlockSpec((1, 1, NSA_KV_HEADS, nhalf, LANES), lambda i, j: (i, j, 0, 0, 0)),
        out_shape=jax.ShapeDtypeStruct((b, 2, NSA_KV_HEADS, nhalf, LANES), bf16),
        compiler_params=_cparams(("parallel", "parallel")),
        name="compress",
    )(aux, pe2, w1t, w1b, w2d)


def _flash_init(m_ref, acc_ref):
    m_ref[...] = jnp.full(m_ref.shape, NEG_INF, f32)
    acc_ref[...] = jnp.zeros(acc_ref.shape, f32)


def _ones_values(v):
    lo_half = lax.broadcasted_iota(i32, v.shape, 1) < HEAD_DIM
    vf = v.astype(f32)
    return jnp.where(lo_half, vf, 1.0).astype(bf16), jnp.where(lo_half, 1.0, vf).astype(bf16)


def _flash_step(s, v_lo, v_hi, m_ref, acc_ref):
    nk = s.shape[1] // LANES
    half = s.shape[0] // 2
    cols = [s[:, c * LANES:(c + 1) * LANES] for c in range(nk)]
    mx = cols[0]
    for c in cols[1:]:
        mx = jnp.maximum(mx, c)
    m_old = m_ref[...]
    m_new = jnp.maximum(m_old, jnp.broadcast_to(jnp.max(mx, axis=-1, keepdims=True), m_old.shape))
    alpha = jnp.exp2(m_old - m_new)
    p = jnp.concatenate([jnp.exp2(c - m_new).astype(bf16) for c in cols], axis=1)
    pv = jnp.concatenate([_dot(p[0:half], v_lo), _dot(p[half:], v_hi)], axis=0)
    acc_ref[...] = alpha * acc_ref[...] + pv
    m_ref[...] = m_new


def _flash_loop(first, last, logits_fn, v_fn, s_ref, m_ref, acc_ref, tail_fix=None):
    sa, sb = s_ref.at[0], s_ref.at[1]
    n = last - first + 1
    pairs = (n - 1) // 2

    def step(buf, kt, fix=None):
        s = buf[...] if fix is None else fix(buf[...], kt)
        _flash_step(s, *v_fn(kt), m_ref, acc_ref)

    sa[...] = logits_fn(first)

    def body(j, carry):
        kt = first + 2 * j
        sb[...] = logits_fn(kt + 1)
        step(sa, kt)
        sa[...] = logits_fn(kt + 2)
        step(sb, kt + 1)
        return carry

    lax.fori_loop(0, pairs, body, 0)
    two_left = n - 2 * pairs == 2

    @pl.when(two_left)
    def _():
        sb[...] = logits_fn(last)
        step(sa, last - 1, tail_fix)
        step(sb, last, tail_fix)

    @pl.when(jnp.logical_not(two_left))
    def _():
        step(sa, last, tail_fix)


def _flash_finish(acc_ref):
    acc = acc_ref[...]
    return acc * (1.0 / jnp.maximum(pltpu.roll(acc, HEAD_DIM, 1), 1e-30))


def _nsa_kernel(q_ref, ks_ref, vs_ref, kw_ref, vw_ref, kc_ref, vc_ref, gate_ref, bc_ref, tb_ref, ov_ref,
                et_ref, o_ref, qaug_ref, kaug_ref, m_ref, acc_ref, os_ref, s_ref, vsel_ref, vwin_ref, *,
                seq, ncmp):
    kaug_ref[:, 0:LANES] = ks_ref[0, 0]
    kaug_ref[:, LANES:2 * LANES] = et_ref[...]
    vsel_ref[0], vsel_ref[1] = _ones_values(vs_ref[0, 0])
    vwin_ref[0], vwin_ref[1] = _ones_values(vw_ref[0, 0])

    def query_tile(i, carry):
        _nsa_tile(i, q_ref, kw_ref, kc_ref, vc_ref, gate_ref, bc_ref, tb_ref, ov_ref, o_ref, qaug_ref,
                  kaug_ref, m_ref, acc_ref, os_ref, s_ref, vsel_ref, vwin_ref, seq=seq, ncmp=ncmp)
        return carry

    lax.fori_loop(0, seq // T_ATT, query_tile, 0)


def _nsa_tile(i, q_ref, kw_ref, kc_ref, vc_ref, gate_ref, bc_ref, tb_ref, ov_ref, o_ref, qaug_ref,
              kaug_ref, m_ref, acc_ref, os_ref, s_ref, vsel_ref, vwin_ref, *, seq, ncmp):
    t = T_ATT
    rows = NSA_GROUP * t
    nsel = seq // SEL_BLOCK
    topn = min(N_SELECT, nsel)
    t0 = i * t
    tile_rows = pl.ds(pl.multiple_of(t0, t), t)
    lane = lax.broadcasted_iota(i32, (t, LANES), 1)
    lo_half = lane < HEAD_DIM

    for rb, g in enumerate(NSA_ROW_ORDER):
        qg = q_ref[0, g // 2, tile_rows, :].astype(f32)
        keep = lo_half if g % 2 == 0 else jnp.logical_not(lo_half)
        qaug_ref[rb * t:(rb + 1) * t, 0:LANES] = jnp.where(keep, qg, 0.0).astype(bf16)
    qs = qaug_ref[:, 0:LANES]

    s = _dot_nt(kc_ref[0, 0, 0], qs) + bc_ref[0, pl.ds(pl.multiple_of(i * ncmp, ncmp), ncmp), :]
    row_t = t0 + (lax.broadcasted_iota(i32, (ncmp, rows), 1) & (t - 1))
    cmp_end = lax.broadcasted_iota(i32, (ncmp, rows), 0) * CMP_STRIDE + (CMP_BLOCK - 1)
    vis = cmp_end <= row_t
    s = jnp.where(vis, s, NEG_INF)
    s = s - jnp.max(s, axis=0, keepdims=True)
    e = jnp.where(vis, jnp.exp2(s), 0.0)
    p_c = e * (1.0 / jnp.maximum(jnp.sum(e, axis=0, keepdims=True), 1e-30))
    o_c = _dot(p_c.T.astype(bf16), vc_ref[0, 0, 0])

    psum = p_c[:, 0:t]
    for g in range(1, NSA_GROUP):
        psum = psum + p_c[:, g * t:(g + 1) * t]
    hi = psum.astype(bf16)
    lo = (psum - hi.astype(f32)).astype(bf16)
    imp = (_dot(ov_ref[...], hi) + _dot(ov_ref[...], lo))[0:nsel]
    blk = lax.broadcasted_iota(i32, (nsel, t), 0)
    cur = (t0 + lax.broadcasted_iota(i32, (nsel, t), 1)) >> 6
    forced = (blk == 0) | ((blk <= cur) & (blk > cur - N_LOCAL_SEL))
    val = jnp.where(forced, FORCED_SCORE, jnp.where(blk <= cur, imp, -1.0))
    cnt = jnp.zeros((nsel, t), f32)
    for j in range(nsel):
        vj = val[j:j + 1, :]
        beats = (vj > val) | ((vj == val) & (blk > j))
        cnt = cnt + jnp.where(beats, 1.0, 0.0)
    mneg = jnp.where(cnt < topn, 0.0, NEG_INF)
    mneg = jnp.concatenate([mneg, jnp.zeros((LANES - nsel, t), f32)], axis=0).T.astype(bf16)
    for g in range(NSA_GROUP):
        qaug_ref[g * t:(g + 1) * t, LANES:2 * LANES] = mneg

    _flash_init(m_ref, acc_ref)

    def sel_logits(kt):
        k = kaug_ref[pl.ds(kt * t, t), :]
        return _dot_nt(qaug_ref[...], k) + tb_ref[0, jnp.minimum(i - kt, 2)]

    def values(v_ref):
        return lambda kt: (v_ref[0, pl.ds(kt * t, t), :], v_ref[1, pl.ds(kt * t, t), :])

    _flash_loop(0, i, sel_logits, values(vsel_ref), s_ref, m_ref, acc_ref)
    os_ref[...] = _flash_finish(acc_ref)

    _flash_init(m_ref, acc_ref)
    nwin = WINDOW // t

    def win_logits(kt):
        d = i - kt
        kind = jnp.where(d == nwin, 3, d)
        return _dot_nt(qs, kw_ref[0, 0, pl.ds(kt * t, t), :]) + tb_ref[0, kind]

    _flash_loop(jnp.maximum(i - nwin, 0), i, win_logits, values(vwin_ref), s_ref, m_ref, acc_ref)
    o_w = _flash_finish(acc_ref)
    o_s = os_ref[...]

    gates = gate_ref[0, 0, tile_rows, :]
    outs = []
    for g in range(NSA_GROUP):
        rb = NSA_ROW_ORDER.index(g)
        sl = slice(rb * t, (rb + 1) * t)
        outs.append(gates[:, g:g + 1] * o_c[sl]
                    + gates[:, NSA_GROUP + g:NSA_GROUP + g + 1] * o_s[sl]
                    + gates[:, 2 * NSA_GROUP + g:2 * NSA_GROUP + g + 1] * o_w[sl])
    for j in range(NSA_GROUP // 2):
        o_ref[0, tile_rows, j * LANES:(j + 1) * LANES] = jnp.where(
            lo_half, outs[2 * j], outs[2 * j + 1]).astype(o_ref.dtype)


def _nsa(qkv, cmpkv, gates, bias_c, tbias, ov, et):
    b, _, s, _ = qkv.shape
    ncmp = cmpkv.shape[3]
    t = T_ATT
    rows = NSA_GROUP * t
    qw = NSA_GROUP * HEAD_DIM
    qtiles = qw // LANES
    kv_spec = lambda col: pl.BlockSpec((1, 1, s, LANES), lambda h, bi, col=col: (bi, col + h, 0, 0))
    cmp_spec = lambda kv: pl.BlockSpec((1, 1, 1, ncmp, LANES), lambda h, bi, kv=kv: (bi, kv, h, 0, 0))
    base = QKV_TILES_KV
    return pl.pallas_call(
        functools.partial(_nsa_kernel, seq=s, ncmp=ncmp),
        grid=(NSA_KV_HEADS, b),
        in_specs=[pl.BlockSpec((1, qtiles, s, LANES), lambda h, bi: (bi, h, 0, 0)),
                  kv_spec(base), kv_spec(base + 2), kv_spec(base + 4), kv_spec(base + 6),
                  cmp_spec(0), cmp_spec(1),
                  pl.BlockSpec((1, 1, s, LANES), lambda h, bi: (bi, h, 0, 0)),
                  pl.BlockSpec((1, (s // t) * ncmp, rows), lambda h, bi: (h, 0, 0)),
                  pl.BlockSpec((1, 4, rows, t), lambda h, bi: (h, 0, 0, 0)),
                  pl.BlockSpec((LANES, ncmp), lambda h, bi: (0, 0)),
                  pl.BlockSpec((s, LANES), lambda h, bi: (0, 0))],
        out_specs=pl.BlockSpec((1, s, qw), lambda h, bi: (bi, 0, h)),
        out_shape=jax.ShapeDtypeStruct((b, s, NSA_Q_W), bf16),
        scratch_shapes=[pltpu.VMEM((rows, 2 * LANES), bf16),
                        pltpu.VMEM((s, 2 * LANES), bf16),
                        pltpu.VMEM((rows, LANES), f32),
                        pltpu.VMEM((rows, LANES), f32),
                        pltpu.VMEM((rows, LANES), f32),
                        pltpu.VMEM((2, rows, t), f32),
                        pltpu.VMEM((2, s, LANES), bf16),
                        pltpu.VMEM((2, s, LANES), bf16)],
        compiler_params=_cparams(("parallel", "parallel")),
        name="nsa",
    )(qkv, qkv, qkv, qkv, qkv, cmpkv, cmpkv, gates, bias_c, tbias, ov, et)


def _fox_kernel(q_ref, k_ref, v_ref, qx_ref, kx_ref, o_ref, qaug_ref, kaug_ref, m_ref, acc_ref, s_ref,
                vaug_ref):
    t = TK_FOX
    tq = TQ_FOX
    seq = k_ref.shape[2]
    lane = lax.broadcasted_iota(i32, (tq, LANES), 1)
    lo_half = lane < HEAD_DIM
    kaug_ref[:, 0:LANES] = k_ref[0, 0]
    kaug_ref[:, LANES:2 * LANES] = kx_ref[0, 0]
    vaug_ref[0], vaug_ref[1] = _ones_values(v_ref[0, 0])

    def logits(kt):
        return _dot_nt(qaug_ref[...], kaug_ref[pl.ds(kt * t, t), :])

    def values(kt):
        return vaug_ref[0, pl.ds(kt * t, t), :], vaug_ref[1, pl.ds(kt * t, t), :]

    def query_tile(i, carry):
        rows = pl.ds(pl.multiple_of(i * tq, tq), tq)
        q = q_ref[0, 0, rows, :].astype(f32)
        qx = qx_ref[0, 0, rows, :].astype(f32)
        qaug_ref[0:tq, 0:LANES] = jnp.where(lo_half, q, 0.0).astype(bf16)
        qaug_ref[tq:2 * tq, 0:LANES] = jnp.where(lo_half, 0.0, q).astype(bf16)
        qaug_ref[0:tq, LANES:2 * LANES] = jnp.where(lane < XCOLS, qx, 0.0).astype(bf16)
        qaug_ref[tq:2 * tq, LANES:2 * LANES] = jnp.where((lane >= XCOLS) & (lane < 2 * XCOLS), qx, 0.0).astype(bf16)

        def causal(s, kt):
            q_pos = i * tq + (lax.broadcasted_iota(i32, (2 * tq, t), 0) & (tq - 1))
            k_pos = kt * t + lax.broadcasted_iota(i32, (2 * tq, t), 1)
            return jnp.where(k_pos <= q_pos, s, NEG_INF)

        _flash_init(m_ref, acc_ref)
        _flash_loop(0, (i + 1) * (tq // t) - 1, logits, values, s_ref, m_ref, acc_ref, tail_fix=causal)
        o = _flash_finish(acc_ref)
        o_ref[0, rows, :] = jnp.where(lo_half, o[0:tq], o[tq:2 * tq]).astype(o_ref.dtype)
        return carry

    lax.fori_loop(0, seq // tq, query_tile, 0)


def _fox(qkv, qx, kx):
    b, _, s, _ = qkv.shape
    t = TK_FOX
    base = QKV_TILES_FOX
    return pl.pallas_call(
        _fox_kernel,
        grid=(b, FOX_PAIRS),
        in_specs=[pl.BlockSpec((1, 1, s, LANES), lambda bi, p: (bi, base + p, 0, 0)),
                  pl.BlockSpec((1, 1, s, LANES), lambda bi, p: (bi, base + FOX_PAIRS + p, 0, 0)),
                  pl.BlockSpec((1, 1, s, LANES), lambda bi, p: (bi, base + 2 * FOX_PAIRS + p, 0, 0)),
                  pl.BlockSpec((1, 1, s, LANES), lambda bi, p: (bi, p, 0, 0)),
                  pl.BlockSpec((1, 1, s, LANES), lambda bi, p: (bi, p, 0, 0))],
        out_specs=pl.BlockSpec((1, s, LANES), lambda bi, p: (bi, 0, p)),
        out_shape=jax.ShapeDtypeStruct((b, s, FOX_W), bf16),
        scratch_shapes=[pltpu.VMEM((2 * TQ_FOX, 2 * LANES), bf16),
                        pltpu.VMEM((s, 2 * LANES), bf16),
                        pltpu.VMEM((2 * TQ_FOX, LANES), f32),
                        pltpu.VMEM((2 * TQ_FOX, LANES), f32),
                        pltpu.VMEM((2, 2 * TQ_FOX, t), f32),
                        pltpu.VMEM((2, s, LANES), bf16)],
        compiler_params=_cparams(("parallel", "parallel")),
        name="fox",
    )(qkv, qkv, qkv, qx, kx)


def _merge_kernel(yn_ref, yf_ref, xb_ref, x_ref, wn_ref, wf_ref, wmg_ref, wo_ref, g_ref, b_ref,
                  xo_ref, xbo_ref):
    mg = _dot(xb_ref[...], wmg_ref[...])
    merged = (_sigmoid(mg[:, 0:D_MODEL]) * _dot(yn_ref[...], wn_ref[...])
              + _sigmoid(mg[:, D_MODEL:2 * D_MODEL]) * _dot(yf_ref[...], wf_ref[...]))
    hmix = _dot(merged.astype(bf16), wo_ref[...])
    xn = _layer_norm(DN_ALPHA * x_ref[...] + hmix, g_ref[...], b_ref[...])
    xo_ref[...] = xn
    xbo_ref[...] = xn.astype(bf16)


def _merge(yn, yf, xb, x, wn, wf, wmg, wo, g, bb):
    m = x.shape[0]
    tm = TM_MERGE
    row = lambda w: pl.BlockSpec((tm, w), lambda i: (i, 0))
    full = lambda a: pl.BlockSpec(a.shape, lambda i: (0, 0))
    return pl.pallas_call(
        _merge_kernel,
        grid=(m // tm,),
        in_specs=[row(NSA_Q_W), row(FOX_W), row(D_MODEL), row(D_MODEL),
                  full(wn), full(wf), full(wmg), full(wo), full(g), full(bb)],
        out_specs=[row(D_MODEL), row(D_MODEL)],
        out_shape=[jax.ShapeDtypeStruct((m, D_MODEL), f32), jax.ShapeDtypeStruct((m, D_MODEL), bf16)],
        compiler_params=_cparams(("parallel",)),
        name="merge",
    )(yn, yf, xb, x, wn, wf, wmg, wo, g, bb)


def _ffn_kernel(xb_ref, x_ref, wg_ref, wu_ref, wd_ref, g_ref, b_ref, xo_ref, xbo_ref, acc_ref, *, nf):
    f = pl.program_id(1)

    @pl.when(f == 0)
    def _():
        acc_ref[...] = jnp.zeros_like(acc_ref)

    xb = xb_ref[...]
    gate = _dot(xb, wg_ref[...])
    up = _dot(xb, wu_ref[...])
    act = (gate * _sigmoid(gate) * up).astype(bf16)
    acc_ref[...] += _dot(act, wd_ref[...])

    @pl.when(f == nf - 1)
    def _():
        xn = _layer_norm(DN_ALPHA * x_ref[...] + acc_ref[...], g_ref[...], b_ref[...])
        xo_ref[...] = xn
        xbo_ref[...] = xn.astype(bf16)


def _ffn(xb, x, wg, wu, wd, g, bb):
    m = x.shape[0]
    dff = wg.shape[1]
    tm, tf = TM_FFN, TF_FFN
    nf = dff // tf
    row = pl.BlockSpec((tm, D_MODEL), lambda i, f: (i, 0))
    vec = pl.BlockSpec((1, D_MODEL), lambda i, f: (0, 0))
    return pl.pallas_call(
        functools.partial(_ffn_kernel, nf=nf),
        grid=(m // tm, nf),
        in_specs=[row, row,
                  pl.BlockSpec((D_MODEL, tf), lambda i, f: (0, f)),
                  pl.BlockSpec((D_MODEL, tf), lambda i, f: (0, f)),
                  pl.BlockSpec((tf, D_MODEL), lambda i, f: (f, 0)),
                  vec, vec],
        out_specs=[row, row],
        out_shape=[jax.ShapeDtypeStruct((m, D_MODEL), f32), jax.ShapeDtypeStruct((m, D_MODEL), bf16)],
        scratch_shapes=[pltpu.VMEM((tm, D_MODEL), f32)],
        compiler_params=_cparams(("parallel", "arbitrary")),
        name="ffn",
    )(xb, x, wg, wu, wd, g, bb)


def _router_kernel(x_ref, r_ref, tri_ref, gate_ref, rank_ref, *, seq):
    tb = tri_ref.shape[0]
    x = x_ref[0]
    xh = x.astype(bf16)
    xl = (x - xh.astype(f32)).astype(bf16)
    r = r_ref[...]
    rh = r.astype(bf16)
    rl = (r - rh.astype(f32)).astype(bf16)
    logits = _dot(xh, rh) + _dot(xh, rl) + _dot(xl, rh)
    lane = lax.broadcasted_iota(i32, (seq, LANES), 1).astype(f32)
    low = -3.0e38
    lg = jnp.where(lane < N_EXPERTS, logits, low)
    m1 = jnp.max(lg, axis=-1, keepdims=True)
    i1 = jnp.min(jnp.where(lg == m1, lane, float(LANES)), axis=-1, keepdims=True)
    lg2 = jnp.where(lane == i1, low, lg)
    m2 = jnp.max(lg2, axis=-1, keepdims=True)
    i2 = jnp.min(jnp.where(lg2 == m2, lane, float(LANES)), axis=-1, keepdims=True)
    e2 = jnp.exp(m2 - m1)
    den = 1.0 + e2
    gate_ref[0] = jnp.where(lane == i1, 1.0 / den, jnp.where(lane == i2, e2 / den, 0.0))
    sel = (lane == i1) | (lane == i2)
    selb = jnp.where(sel, 1.0, 0.0).astype(bf16)
    carry = jnp.zeros((1, LANES), f32)
    for blk in range(seq // tb):
        sl = slice(blk * tb, (blk + 1) * tb)
        c = _dot(tri_ref[...], selb[sl]) + carry
        carry = c[tb - 1:tb, :]
        rank_ref[0, sl, :] = jnp.where(sel[sl], c - 1.0, -1.0)


def _router(x3, router_pad, tri):
    b, s, _ = x3.shape
    return pl.pallas_call(
        functools.partial(_router_kernel, seq=s),
        grid=(b,),
        in_specs=[pl.BlockSpec((1, s, D_MODEL), lambda i: (i, 0, 0)),
                  pl.BlockSpec((D_MODEL, LANES), lambda i: (0, 0)),
                  pl.BlockSpec(tri.shape, lambda i: (0, 0))],
        out_specs=[pl.BlockSpec((1, s, LANES), lambda i: (i, 0, 0)),
                   pl.BlockSpec((1, s, LANES), lambda i: (i, 0, 0))],
        out_shape=[jax.ShapeDtypeStruct((b, s, LANES), f32),
                   jax.ShapeDtypeStruct((b, s, LANES), f32)],
        compiler_params=_cparams(("parallel",)),
        name="router",
    )(x3, router_pad, tri)


def _moe_kernel(tot_ref, tot_al_ref, off_ref, npc_ref, xb_ref, rankt_ref, gatet_ref, wg_ref, wu_ref, wd_ref,
                y_ref, xg_ref, acc_ref, *, seq, nf):
    tr = TR_MOE
    half = tr // 2
    nsubc = seq // SUB_MOE
    c = pl.program_id(0)
    e = pl.program_id(1)
    f = pl.program_id(2)
    ce = c * N_EXPERTS + e
    tot = tot_ref[ce]
    rem = tot % tr
    nfull = tot // tr + jnp.where(rem > half, 1, 0)
    has_tail = (rem > 0) & (rem <= half)
    tail0 = pl.multiple_of(nfull * tr, half)

    def pieces(fn):
        for q in range(nsubc):
            base = off_ref[ce * nsubc + q]

            def body(p, carry, q=q, base=base):
                fn(q, pl.multiple_of(base + p * PIECE_MOE, ROW_ALIGN))
                return carry
            lax.fori_loop(0, npc_ref[ce * nsubc + q], body, 0)

    def onehot_rows(q, r0):
        rk = rankt_ref[0, pl.ds(e, 1), q * SUB_MOE:(q + 1) * SUB_MOE]
        want = (r0 + lax.broadcasted_iota(i32, (PIECE_MOE, SUB_MOE), 0)).astype(f32)
        return rk == want

    def gather(q, r0):
        p = jnp.where(onehot_rows(q, r0), 1.0, 0.0).astype(bf16)
        xg_ref[pl.ds(r0, PIECE_MOE), :] += _dot(p, xb_ref[0, q * SUB_MOE:(q + 1) * SUB_MOE, :]).astype(bf16)

    def clear(r0, nr):
        xg_ref[pl.ds(r0, nr), :] = jnp.zeros((nr, D_MODEL), bf16)
        acc_ref[pl.ds(r0, nr), :] = jnp.zeros((nr, D_MODEL), f32)

    def hidden(r0, nr):
        xs = xg_ref[pl.ds(r0, nr), :]
        gate = _dot(xs, wg_ref[0, 0])
        up = _dot(xs, wu_ref[0, 0])
        act = (gate * _sigmoid(gate) * up).astype(bf16)
        acc_ref[pl.ds(r0, nr), :] += _dot(act, wd_ref[0, 0])

    def combine(q, r0):
        grow = gatet_ref[0, pl.ds(e, 1), q * SUB_MOE:(q + 1) * SUB_MOE]
        hit = onehot_rows(q, r0)
        w = jnp.sum(jnp.where(hit, grow, 0.0), axis=-1, keepdims=True)
        z = (acc_ref[pl.ds(r0, PIECE_MOE), :] * w).astype(bf16)
        p = jnp.where(hit, 1.0, 0.0).astype(bf16)
        y_ref[0, q * SUB_MOE:(q + 1) * SUB_MOE, :] += lax.dot_general(
            p, z, (((0,), (0,)), ((), ())), preferred_element_type=f32)

    def tiles(fn):
        def body(s, carry):
            fn(pl.multiple_of(s * tr, tr), tr)
            return carry
        lax.fori_loop(0, nfull, body, 0)

        @pl.when(has_tail)
        def _():
            fn(tail0, half)

    def zero(j, carry):
        y_ref[0, pl.ds(j * tr, tr), :] = jnp.zeros((tr, D_MODEL), f32)
        return carry

    lax.fori_loop(0, jnp.where((e == 0) & (f == 0), seq // tr, 0), zero, 0)

    @pl.when(f == 0)
    def _():
        tiles(clear)
        clear(pl.multiple_of(tot_al_ref[ce], ROW_ALIGN), tr)
        pieces(gather)

    tiles(hidden)

    @pl.when(f == nf - 1)
    def _():
        pieces(combine)


def _moe(xb3, gate, rank, wg, wu, wd, layer):
    b, s, _ = xb3.shape
    dff = wg.shape[3]
    nf = dff // TF_MOE
    nsubc = s // SUB_MOE
    rk = rank[:, :, :N_EXPERTS]
    cnt = jnp.sum((rk >= 0.0).reshape(b, nsubc, SUB_MOE, N_EXPERTS), axis=2).astype(i32)
    first = jnp.cumsum(cnt, axis=1) - cnt
    off = first // ROW_ALIGN * ROW_ALIGN
    npc = jnp.where(cnt > 0, (first - off + cnt + PIECE_MOE - 1) // PIECE_MOE, 0)
    post = rk.transpose(0, 2, 1)
    gatet = gate[:, :, :N_EXPERTS].transpose(0, 2, 1)
    tot = jnp.sum(cnt, axis=1).reshape(-1)
    tot_al = (tot + ROW_ALIGN - 1) // ROW_ALIGN * ROW_ALIGN
    off_flat = off.transpose(0, 2, 1).reshape(-1)
    npc = npc.transpose(0, 2, 1).reshape(-1)
    buf_rows = s + ROW_ALIGN + PIECE_MOE + TR_MOE
    grid_spec = pltpu.PrefetchScalarGridSpec(
        num_scalar_prefetch=4,
        grid=(b, N_EXPERTS, nf),
        in_specs=[pl.BlockSpec((1, s, D_MODEL), lambda c, e, f, *_: (c, 0, 0)),
                  pl.BlockSpec((1, N_EXPERTS, s), lambda c, e, f, *_: (c, 0, 0)),
                  pl.BlockSpec((1, N_EXPERTS, s), lambda c, e, f, *_: (c, 0, 0)),
                  pl.BlockSpec((1, 1, D_MODEL, TF_MOE), lambda c, e, f, *_: (layer, e, 0, f)),
                  pl.BlockSpec((1, 1, D_MODEL, TF_MOE), lambda c, e, f, *_: (layer, e, 0, f)),
                  pl.BlockSpec((1, 1, TF_MOE, D_MODEL), lambda c, e, f, *_: (layer, e, f, 0))],
        out_specs=pl.BlockSpec((1, s, D_MODEL), lambda c, e, f, *_: (c, 0, 0)),
        scratch_shapes=[pltpu.VMEM((buf_rows, D_MODEL), bf16), pltpu.VMEM((buf_rows, D_MODEL), f32)],
    )
    return pl.pallas_call(
        functools.partial(_moe_kernel, seq=s, nf=nf),
        grid_spec=grid_spec,
        out_shape=jax.ShapeDtypeStruct((b, s, D_MODEL), f32),
        compiler_params=_cparams(("parallel", "arbitrary", "arbitrary")),
        name="moe",
    )(tot, tot_al, off_flat, npc, xb3, post, gatet, wg, wu, wd)


def _resln_kernel(x_ref, y_ref, g_ref, b_ref, xo_ref, xbo_ref):
    xn = _layer_norm(DN_ALPHA * x_ref[...] + y_ref[...], g_ref[...], b_ref[...])
    xo_ref[...] = xn
    xbo_ref[...] = xn.astype(bf16)


def _resln(x, y, g, bb):
    m = x.shape[0]
    tm = TM_FFN
    row = pl.BlockSpec((tm, D_MODEL), lambda i: (i, 0))
    vec = pl.BlockSpec((1, D_MODEL), lambda i: (0, 0))
    return pl.pallas_call(
        _resln_kernel,
        grid=(m // tm,),
        in_specs=[row, row, vec, vec],
        out_specs=[row, row],
        out_shape=[jax.ShapeDtypeStruct((m, D_MODEL), f32), jax.ShapeDtypeStruct((m, D_MODEL), bf16)],
        compiler_params=_cparams(("parallel",)),
        name="resln",
    )(x, y, g, bb)


def _t5_bucket(dist):
    n = jnp.maximum(dist, 0)
    max_exact = REL_BUCKETS // 2
    large = max_exact + (jnp.log(jnp.maximum(n, 1).astype(f32) / max_exact)
                         / math.log(REL_MAX_DIST / max_exact) * (REL_BUCKETS - max_exact)).astype(i32)
    large = jnp.minimum(large, REL_BUCKETS - 1)
    return jnp.where(n < max_exact, n, large)


def _bias_of_dist(rel_bias, dist):
    onehot = (_t5_bucket(dist)[None] == jnp.arange(REL_BUCKETS).reshape((-1,) + (1,) * dist.ndim)).astype(f32)
    return LOG2E * jnp.einsum("kh,k...->h...", rel_bias.astype(f32), onehot, precision=lax.Precision.HIGHEST)


def _bias_tables(rel_bias, seq):
    t = T_ATT
    rows = NSA_GROUP * t
    ncmp = seq // CMP_STRIDE
    d0 = jnp.arange(t)[:, None] - jnp.arange(t)[None, :]
    offs = jnp.array([0, t, 2 * t, WINDOW]).reshape(4, 1, 1)
    kinds = _bias_of_dist(rel_bias, offs + d0[None])
    mask = jnp.stack([d0 >= 0, d0 == d0, d0 == d0, d0 < 0])
    kinds = jnp.where(mask[None], kinds, NEG_INF)
    order = np.array(NSA_ROW_ORDER)
    tbias = kinds.reshape(NSA_KV_HEADS, NSA_GROUP, 4, t, t)[:, order].transpose(0, 2, 1, 3, 4)
    tbias = tbias.reshape(NSA_KV_HEADS, 4, rows, t)
    cend = jnp.arange(ncmp) * CMP_STRIDE + CMP_BLOCK - 1
    bc = _bias_of_dist(rel_bias, jnp.arange(seq)[:, None] - cend[None, :])
    bc = bc.reshape(NSA_KV_HEADS, NSA_GROUP, seq // t, t, ncmp)[:, order].transpose(0, 2, 4, 1, 3)
    return tbias, bc.reshape(NSA_KV_HEADS, (seq // t) * ncmp, rows)


def _selection_constants(seq):
    ncmp = seq // CMP_STRIDE
    nsel = seq // SEL_BLOCK
    c0 = np.arange(ncmp)[:, None] * CMP_STRIDE
    s0 = np.arange(LANES)[None, :] * SEL_BLOCK
    ov = np.maximum(np.minimum(c0 + CMP_BLOCK, s0 + SEL_BLOCK) - np.maximum(c0, s0), 0) / CMP_BLOCK
    ov[ncmp - 1, :] = 0.0
    ov[:, nsel:] = 0.0
    et = (np.arange(seq)[:, None] // SEL_BLOCK == np.arange(LANES)[None, :]).astype(np.float32)
    return jnp.asarray(ov.T, bf16), jnp.asarray(et, bf16)


def _fox_placement():
    xw = FOX_PAIRS * LANES
    pq = np.zeros((3, LANES, xw), np.float32)
    pk = np.zeros((3, LANES, xw), np.float32)
    oq = np.zeros((1, xw), np.float32)
    ok = np.zeros((1, xw), np.float32)
    for p in range(FOX_PAIRS):
        for hh in range(2):
            src = FGATE_LANE + 2 * p + hh
            base = p * LANES + hh * XCOLS
            for part in range(3):
                pk[part, src, base + part] = -1.0
                pq[part, src, base + 3 + part] = 1.0
                oq[0, base + part] = 1.0
                ok[0, base + 3 + part] = 1.0
    return jnp.asarray(pq, bf16), jnp.asarray(pk, bf16), jnp.asarray(oq), jnp.asarray(ok)


def _layer_weights(w_in, layer_pe, w1, w2, f_bias):
    offs = np.cumsum((NSA_Q_W, 6 * 2 * HEAD_DIM, 3 * NSA_HEADS, 3 * FOX_W, FOX_HEADS, 2 * D_MODEL))
    kv0, g0, fx0, ff0, mg0 = offs[0], offs[1], offs[2], offs[3], offs[4]
    scale = HEAD_DIM ** -0.5 * LOG2E
    kvw = NSA_KV_HEADS * HEAD_DIM
    w_kv = w_in[:, kv0:g0]
    w_kvdup = jnp.repeat(w_kv[:, 2 * kvw:].reshape(D_MODEL, 4 * NSA_KV_HEADS, 1, HEAD_DIM), 2, axis=2)
    w_kvdup = w_kvdup.reshape(D_MODEL, 8 * kvw)
    w_qkv = jnp.concatenate([w_in[:, :NSA_Q_W] * scale, w_kvdup, w_in[:, fx0:fx0 + FOX_W] * scale,
                             w_in[:, fx0 + FOX_W:ff0]], axis=1).astype(bf16)
    zeros = lambda n: jnp.zeros((D_MODEL, n), w_in.dtype)
    ng = 3 * NSA_GROUP
    w_g = w_in[:, g0:fx0].reshape(D_MODEL, NSA_KV_HEADS, NSA_GROUP, 3).transpose(0, 1, 3, 2)
    gate_cols = lambda h: w_g[:, h].reshape(D_MODEL, ng)
    w_aux = jnp.concatenate([w_kv[:, :2 * kvw],
                             gate_cols(0), zeros(FGATE_LANE - ng), w_in[:, ff0:mg0],
                             zeros(LANES - FGATE_LANE - FOX_HEADS),
                             gate_cols(1), zeros(LANES - ng)], axis=1).astype(bf16)
    fb_row = jnp.zeros((1, LANES), f32).at[0, FGATE_LANE:FGATE_LANE + FOX_HEADS].set(f_bias.astype(f32))
    pe2 = jnp.tile(layer_pe.astype(f32), (1, 1, NSA_KV_HEADS))
    w1r = w1.reshape(2, CMP_BLOCK, HEAD_DIM, CMP_HIDDEN).astype(bf16)
    zero = jnp.zeros_like(w1r)
    w1bd = jnp.concatenate([jnp.concatenate([w1r, zero], axis=-1),
                            jnp.concatenate([zero, w1r], axis=-1)], axis=-2)
    w2d = jnp.concatenate([w2, w2], axis=-1).astype(bf16)
    return (w_qkv, w_aux, w_in[:, mg0:].astype(bf16), fb_row, pe2, w1bd[:, :CMP_STRIDE], w1bd[:, CMP_STRIDE:],
            w2d)


def kernel(x, w_in, nsa_cmp_pe, nsa_cmp_w1, nsa_cmp_w2, fox_f_bias, w_nsa_branch, w_fox_branch, w_out,
           rel_bias, ln1_g, ln1_b, ln2_g, ln2_b, dense_w_gate, dense_w_up, dense_w_down, moe_router,
           moe_w_gate, moe_w_up, moe_w_down):
    b, s, d = x.shape
    m = b * s
    tbias, bias_c = _bias_tables(rel_bias, s)
    ov, et = _selection_constants(s)
    pq, pk, oq, ok = _fox_placement()
    tri128 = jnp.asarray(np.tril(np.ones((LANES, LANES), np.float32)), bf16)
    tri256 = jnp.asarray(np.tril(np.ones((256, 256), np.float32)), bf16)

    moe_wg, moe_wu, moe_wd = moe_w_gate.astype(bf16), moe_w_up.astype(bf16), moe_w_down.astype(bf16)
    xf = x.reshape(m, d).astype(f32)
    xb = xf.astype(bf16)
    for layer in range(DEPTH):
        w_qkv, w_aux, w_mg, fb_row, pe2, w1t, w1b, w2d = _layer_weights(
            w_in[layer], nsa_cmp_pe[layer], nsa_cmp_w1[layer], nsa_cmp_w2[layer], fox_f_bias[layer])
        qkv, aux = _proj(xb, w_qkv, w_aux, b)
        aux = aux.reshape(b, s, AUX_W)
        gates, qx, kx = _gateprep(aux, fb_row, tri128, pq, pk, oq, ok)
        cmpkv = _compress(aux, pe2, w1t, w1b, w2d)
        y_nsa = _nsa(qkv, cmpkv, gates, bias_c, tbias, ov, et).reshape(m, NSA_Q_W)
        y_fox = _fox(qkv, qx, kx).reshape(m, FOX_W)
        xf, xb = _merge(y_nsa, y_fox, xb, xf, w_nsa_branch[layer].astype(bf16),
                        w_fox_branch[layer].astype(bf16), w_mg, w_out[layer].astype(bf16),
                        ln1_g[layer].reshape(1, d), ln1_b[layer].reshape(1, d))
        j = layer // 2
        g2, b2 = ln2_g[layer].reshape(1, d), ln2_b[layer].reshape(1, d)
        if layer % 2 == 0:
            xf, xb = _ffn(xb, xf, dense_w_gate[j].astype(bf16), dense_w_up[j].astype(bf16),
                          dense_w_down[j].astype(bf16), g2, b2)
        else:
            router_pad = jnp.zeros((d, LANES), f32).at[:, :N_EXPERTS].set(moe_router[j].astype(f32))
            gate, rank = _router(xf.reshape(b, s, d), router_pad, tri256)
            y = _moe(xb.reshape(b, s, d), gate, rank, moe_wg, moe_wu, moe_wd, j)
            xf, xb = _resln(xf, y.reshape(m, d), g2, b2)
    return xf.reshape(b, s, d).astype(x.dtype)
```

```python
import functools
import math

import numpy as np
import jax
import jax.numpy as jnp
from jax import lax
from jax.experimental import pallas as pl
from jax.experimental.pallas import tpu as pltpu

f32 = jnp.float32
bf16 = jnp.bfloat16
i32 = jnp.int32

D_MODEL = 1024
HEAD_DIM = 64
LANES = 128
NSA_HEADS = 8
NSA_KV_HEADS = 2
NSA_GROUP = NSA_HEADS // NSA_KV_HEADS
NSA_ROW_ORDER = (0, 2, 1, 3)
FOX_HEADS = 8
FOX_PAIRS = FOX_HEADS // 2
CMP_BLOCK = 32
CMP_STRIDE = 16
CMP_HIDDEN = 128
SEL_BLOCK = 64
N_SELECT = 16
N_LOCAL_SEL = 2
WINDOW = 512
REL_BUCKETS = 32
REL_MAX_DIST = 128
N_EXPERTS = 8
DEPTH = 4
DN_ALPHA = (2 * DEPTH) ** 0.25
LN_EPS = 1e-5
FORCED_SCORE = 1e4
NEG_INF = -1e30
LOG2E = math.log2(math.e)

NSA_Q_W = NSA_HEADS * HEAD_DIM
FOX_W = FOX_HEADS * HEAD_DIM
QKV_TILES_KV = NSA_Q_W // LANES
QKV_TILES_FOX = QKV_TILES_KV + 4 * NSA_KV_HEADS
QKV_W = (QKV_TILES_FOX + 3 * FOX_PAIRS) * LANES
AUX_W = 4 * LANES
FGATE_LANE = 24
XCOLS = 6

T_ATT = 256
TQ_FOX = 512
TK_FOX = 512
TM_PROJ = 512
TM_MERGE = 512
TM_FFN = 512
TF_FFN = 1408
TR_MOE = 256
SUB_MOE = 512
PIECE_MOE = 192
ROW_ALIGN = 16
TF_MOE = 896
VMEM_LIMIT = 56 * 1024 * 1024


def _cparams(sem):
    return pltpu.CompilerParams(dimension_semantics=sem, vmem_limit_bytes=VMEM_LIMIT)


def _dot(a, b):
    return jnp.dot(a, b, preferred_element_type=f32)


def _dot_nt(a, b):
    return lax.dot_general(a, b, (((1,), (1,)), ((), ())), preferred_element_type=f32)


def _sigmoid(x):
    return 1.0 / (1.0 + jnp.exp(-x))


def _layer_norm(z, g, b):
    mu = jnp.mean(z, axis=-1, keepdims=True)
    zc = z - mu
    var = jnp.mean(zc * zc, axis=-1, keepdims=True)
    return zc * lax.rsqrt(var + LN_EPS) * g + b


def _split3(x):
    hi = x.astype(bf16)
    r1 = x - hi.astype(f32)
    mid = r1.astype(bf16)
    lo = (r1 - mid.astype(f32)).astype(bf16)
    return hi, mid, lo


def _proj_kernel(x_ref, w_ref, wa_ref, o_ref, oa_ref):
    x = x_ref[...]
    res = _dot(x, w_ref[...])
    for j in range(o_ref.shape[1]):
        o_ref[0, j] = res[:, j * LANES:(j + 1) * LANES].astype(o_ref.dtype)
    oa_ref[...] = _dot(x, wa_ref[...])


def _proj(xb, w, wa, batch):
    m, k = xb.shape
    n, na = w.shape[1], wa.shape[1]
    nb = m // batch // TM_PROJ
    return pl.pallas_call(
        _proj_kernel,
        grid=(m // TM_PROJ,),
        in_specs=[pl.BlockSpec((TM_PROJ, k), lambda i: (i, 0)),
                  pl.BlockSpec((k, n), lambda i: (0, 0)),
                  pl.BlockSpec((k, na), lambda i: (0, 0))],
        out_specs=[pl.BlockSpec((1, n // LANES, TM_PROJ, LANES), lambda i: (i // nb, 0, i % nb, 0)),
                   pl.BlockSpec((TM_PROJ, na), lambda i: (i, 0))],
        out_shape=[jax.ShapeDtypeStruct((batch, n // LANES, m // batch, LANES), bf16),
                   jax.ShapeDtypeStruct((m, na), f32)],
        compiler_params=_cparams(("parallel",)),
        name="proj",
    )(xb, w, wa)


def _gateprep_kernel(a_ref, fb_ref, tri_ref, pq_ref, pk_ref, oq_ref, ok_ref, g_ref, qx_ref, kx_ref, *, seq):
    tb = LANES
    tri = tri_ref[...]
    local = []
    for blk in range(seq // tb):
        sl = slice(blk * tb, (blk + 1) * tb)
        va = a_ref[0, sl, 0:LANES]
        g_ref[0, 0, sl, :] = _sigmoid(va)
        g_ref[0, 1, sl, :] = _sigmoid(a_ref[0, sl, LANES:2 * LANES])
        z = va + fb_ref[...]
        logf = jnp.minimum(z, 0.0) - jnp.log1p(jnp.exp(-jnp.abs(z)))
        hi, mid, lo = _split3(logf)
        local.append(_dot(tri, hi) + _dot(tri, mid) + _dot(tri, lo))
    carry = jnp.zeros((1, LANES), f32)
    for blk in range(seq // tb):
        sl = slice(blk * tb, (blk + 1) * tb)
        c = local[blk] + carry
        carry = c[tb - 1:tb, :]
        chi, cmid, clo = _split3(c * LOG2E)
        qx = _dot(chi, pq_ref[0]) + _dot(cmid, pq_ref[1]) + _dot(clo, pq_ref[2]) + oq_ref[...]
        kx = _dot(chi, pk_ref[0]) + _dot(cmid, pk_ref[1]) + _dot(clo, pk_ref[2]) + ok_ref[...]
        for p in range(FOX_PAIRS):
            qx_ref[0, p, sl, :] = qx[:, p * LANES:(p + 1) * LANES].astype(bf16)
            kx_ref[0, p, sl, :] = kx[:, p * LANES:(p + 1) * LANES].astype(bf16)


def _gateprep(aux, fb_row, tri, pq, pk, oq, ok):
    b, s, _ = aux.shape
    const2 = lambda a: pl.BlockSpec(a.shape, lambda i: (0, 0))
    const3 = lambda a: pl.BlockSpec(a.shape, lambda i: (0, 0, 0))
    return pl.pallas_call(
        functools.partial(_gateprep_kernel, seq=s),
        grid=(b,),
        in_specs=[pl.BlockSpec((1, s, 2 * LANES), lambda i: (i, 0, 1)),
                  const2(fb_row), const2(tri), const3(pq), const3(pk), const2(oq), const2(ok)],
        out_specs=[pl.BlockSpec((1, 2, s, LANES), lambda i: (i, 0, 0, 0)),
                   pl.BlockSpec((1, FOX_PAIRS, s, LANES), lambda i: (i, 0, 0, 0)),
                   pl.BlockSpec((1, FOX_PAIRS, s, LANES), lambda i: (i, 0, 0, 0))],
        out_shape=[jax.ShapeDtypeStruct((b, 2, s, LANES), f32),
                   jax.ShapeDtypeStruct((b, FOX_PAIRS, s, LANES), bf16),
                   jax.ShapeDtypeStruct((b, FOX_PAIRS, s, LANES), bf16)],
        compiler_params=_cparams(("parallel",)),
        name="gateprep",
    )(aux, fb_row, tri, pq, pk, oq, ok)


def _gelu_tanh(x):
    c = math.sqrt(2.0 / math.pi)
    return x * (0.5 * (1.0 + jnp.tanh(c * (x + 0.044715 * (x * x * x)))))


def _compress_kernel(a_ref, pe_ref, w1t_ref, w1b_ref, w2_ref, o_ref, *, nhalf):
    top = jnp.zeros((nhalf, NSA_KV_HEADS * CMP_HIDDEN), f32)
    bot = jnp.zeros((nhalf, NSA_KV_HEADS * CMP_HIDDEN), f32)
    for l in range(CMP_STRIDE):
        rows = a_ref[0, pl.ds(l, nhalf, stride=CMP_STRIDE), :]
        top = top + _dot((rows + pe_ref[0, l:l + 1, :]).astype(bf16), w1t_ref[0, l])
        bot = bot + _dot((rows + pe_ref[0, CMP_STRIDE + l:CMP_STRIDE + l + 1, :]).astype(bf16), w1b_ref[0, l])
    pre = top + pltpu.roll(bot, nhalf - 1, 0)
    act = _gelu_tanh(pre).astype(bf16)
    for h in range(NSA_KV_HEADS):
        o_ref[0, 0, h] = _dot(act[:, h * CMP_HIDDEN:(h + 1) * CMP_HIDDEN], w2_ref[0]).astype(o_ref.dtype)


def _compress(aux, pe2, w1t, w1b, w2d):
    b, s, _ = aux.shape
    nhalf = s // CMP_STRIDE
    w1_spec = pl.BlockSpec((1,) + w1t.shape[1:], lambda i, j: (j, 0, 0, 0))
    return pl.pallas_call(
        functools.partial(_compress_kernel, nhalf=nhalf),
        grid=(b, 2),
        in_specs=[pl.BlockSpec((1, s, LANES), lambda i, j: (i, 0, j)),
                  pl.BlockSpec((1, CMP_BLOCK, LANES), lambda i, j: (j, 0, 0)),
                  w1_spec, w1_spec,
                  pl.BlockSpec((1, CMP_HIDDEN, LANES), lambda i, j: (j, 0, 0))],
        out_specs=pl.BlockSpec((1, 1, NSA_KV_HEADS, nhalf, LANES), lambda i, j: (i, j, 0, 0, 0)),
        out_shape=jax.ShapeDtypeStruct((b, 2, NSA_KV_HEADS, nhalf, LANES), bf16),
        compiler_params=_cparams(("parallel", "parallel")),
        name="compress",
    )(aux, pe2, w1t, w1b, w2d)


def _flash_init(m_ref, acc_ref):
    m_ref[...] = jnp.full(m_ref.shape, NEG_INF, f32)
    acc_ref[...] = jnp.zeros(acc_ref.shape, f32)


def _ones_values(v):
    lo_half = lax.broadcasted_iota(i32, v.shape, 1) < HEAD_DIM
    vf = v.astype(f32)
    return jnp.where(lo_half, vf, 1.0).astype(bf16), jnp.where(lo_half, 1.0, vf).astype(bf16)


def _flash_step(s, v_lo, v_hi, m_ref, acc_ref):
    nk = s.shape[1] // LANES
    half = s.shape[0] // 2
    cols = [s[:, c * LANES:(c + 1) * LANES] for c in range(nk)]
    mx = cols[0]
    for c in cols[1:]:
        mx = jnp.maximum(mx, c)
    m_old = m_ref[...]
    m_new = jnp.maximum(m_old, jnp.broadcast_to(jnp.max(mx, axis=-1, keepdims=True), m_old.shape))
    alpha = jnp.exp2(m_old - m_new)
    p = jnp.concatenate([jnp.exp2(c - m_new).astype(bf16) for c in cols], axis=1)
    pv = jnp.concatenate([_dot(p[0:half], v_lo), _dot(p[half:], v_hi)], axis=0)
    acc_ref[...] = alpha * acc_ref[...] + pv
    m_ref[...] = m_new


def _flash_loop(first, last, logits_fn, v_fn, s_ref, m_ref, acc_ref, last_fix=None):
    sa, sb = s_ref.at[0], s_ref.at[1]
    n = last - first + 1
    pairs = (n - 1) // 2

    def step(buf, kt, fix=None):
        s = buf[...] if fix is None else fix(buf[...])
        _flash_step(s, *v_fn(kt), m_ref, acc_ref)

    sa[...] = logits_fn(first)

    def body(j, carry):
        kt = first + 2 * j
        sb[...] = logits_fn(kt + 1)
        step(sa, kt)
        sa[...] = logits_fn(kt + 2)
        step(sb, kt + 1)
        return carry

    lax.fori_loop(0, pairs, body, 0)
    two_left = n - 2 * pairs == 2

    @pl.when(two_left)
    def _():
        sb[...] = logits_fn(last)
        step(sa, last - 1)
        step(sb, last, last_fix)

    @pl.when(jnp.logical_not(two_left))
    def _():
        step(sa, last, last_fix)


def _flash_finish(acc_ref):
    acc = acc_ref[...]
    return acc * (1.0 / jnp.maximum(pltpu.roll(acc, HEAD_DIM, 1), 1e-30))


def _nsa_kernel(q_ref, ks_ref, vs_ref, kw_ref, vw_ref, kc_ref, vc_ref, gate_ref, bc_ref, tb_ref, ov_ref,
                et_ref, o_ref, qaug_ref, kaug_ref, m_ref, acc_ref, os_ref, s_ref, vsel_ref, vwin_ref, *,
                seq, ncmp):
    kaug_ref[:, 0:LANES] = ks_ref[0, 0]
    kaug_ref[:, LANES:2 * LANES] = et_ref[...]
    vsel_ref[0], vsel_ref[1] = _ones_values(vs_ref[0, 0])
    vwin_ref[0], vwin_ref[1] = _ones_values(vw_ref[0, 0])

    def query_tile(i, carry):
        _nsa_tile(i, q_ref, kw_ref, kc_ref, vc_ref, gate_ref, bc_ref, tb_ref, ov_ref, o_ref, qaug_ref,
                  kaug_ref, m_ref, acc_ref, os_ref, s_ref, vsel_ref, vwin_ref, seq=seq, ncmp=ncmp)
        return carry

    lax.fori_loop(0, seq // T_ATT, query_tile, 0)


def _nsa_tile(i, q_ref, kw_ref, kc_ref, vc_ref, gate_ref, bc_ref, tb_ref, ov_ref, o_ref, qaug_ref,
              kaug_ref, m_ref, acc_ref, os_ref, s_ref, vsel_ref, vwin_ref, *, seq, ncmp):
    t = T_ATT
    rows = NSA_GROUP * t
    nsel = seq // SEL_BLOCK
    topn = min(N_SELECT, nsel)
    t0 = i * t
    tile_rows = pl.ds(pl.multiple_of(t0, t), t)
    lane = lax.broadcasted_iota(i32, (t, LANES), 1)
    lo_half = lane < HEAD_DIM

    for rb, g in enumerate(NSA_ROW_ORDER):
        qg = q_ref[0, g // 2, tile_rows, :].astype(f32)
        keep = lo_half if g % 2 == 0 else jnp.logical_not(lo_half)
        qaug_ref[rb * t:(rb + 1) * t, 0:LANES] = jnp.where(keep, qg, 0.0).astype(bf16)
    qs = qaug_ref[:, 0:LANES]

    s = _dot_nt(kc_ref[0, 0, 0], qs) + bc_ref[0, pl.ds(pl.multiple_of(i * ncmp, ncmp), ncmp), :]
    sees_any = t0 + (lax.broadcasted_iota(i32, (1, rows), 1) & (t - 1)) >= CMP_BLOCK - 1
    s = s - jnp.max(s, axis=0, keepdims=True)
    e = jnp.where(sees_any, jnp.exp2(s), 0.0)
    p_c = e * (1.0 / jnp.maximum(jnp.sum(e, axis=0, keepdims=True), 1e-30))
    o_c = _dot(p_c.T.astype(bf16), vc_ref[0, 0, 0])

    psum = p_c[:, 0:t]
    for g in range(1, NSA_GROUP):
        psum = psum + p_c[:, g * t:(g + 1) * t]
    hi = psum.astype(bf16)
    lo = (psum - hi.astype(f32)).astype(bf16)
    imp = (_dot(ov_ref[...], hi) + _dot(ov_ref[...], lo))[0:nsel]
    blk = lax.broadcasted_iota(i32, (nsel, t), 0)
    cur = (t0 + lax.broadcasted_iota(i32, (nsel, t), 1)) >> 6
    forced = (blk == 0) | ((blk <= cur) & (blk > cur - N_LOCAL_SEL))
    val = jnp.where(forced, FORCED_SCORE, jnp.where(blk <= cur, imp, -1.0))
    cnt = jnp.zeros((nsel, t), f32)
    for j in range(nsel):
        vj = val[j:j + 1, :]
        beats = (vj > val) | ((vj == val) & (blk > j))
        cnt = cnt + jnp.where(beats, 1.0, 0.0)
    mneg = jnp.where(cnt < topn, 0.0, NEG_INF)
    mneg = jnp.concatenate([mneg, jnp.zeros((LANES - nsel, t), f32)], axis=0).T.astype(bf16)
    for g in range(NSA_GROUP):
        qaug_ref[g * t:(g + 1) * t, LANES:2 * LANES] = mneg

    _flash_init(m_ref, acc_ref)

    def sel_logits(kt):
        k = kaug_ref[pl.ds(kt * t, t), :]
        return _dot_nt(qaug_ref[...], k) + tb_ref[0, jnp.minimum(i - kt, 2)]

    def values(v_ref):
        return lambda kt: (v_ref[0, pl.ds(kt * t, t), :], v_ref[1, pl.ds(kt * t, t), :])

    _flash_loop(0, i, sel_logits, values(vsel_ref), s_ref, m_ref, acc_ref)
    os_ref[...] = _flash_finish(acc_ref)

    _flash_init(m_ref, acc_ref)
    nwin = WINDOW // t

    def win_logits(kt):
        d = i - kt
        kind = jnp.where(d == nwin, 3, d)
        return _dot_nt(qs, kw_ref[0, 0, pl.ds(kt * t, t), :]) + tb_ref[0, kind]

    _flash_loop(jnp.maximum(i - nwin, 0), i, win_logits, values(vwin_ref), s_ref, m_ref, acc_ref)
    o_w = _flash_finish(acc_ref)
    o_s = os_ref[...]

    gates = gate_ref[0, 0, tile_rows, :]
    outs = []
    for g in range(NSA_GROUP):
        rb = NSA_ROW_ORDER.index(g)
        sl = slice(rb * t, (rb + 1) * t)
        outs.append(gates[:, g:g + 1] * o_c[sl]
                    + gates[:, NSA_GROUP + g:NSA_GROUP + g + 1] * o_s[sl]
                    + gates[:, 2 * NSA_GROUP + g:2 * NSA_GROUP + g + 1] * o_w[sl])
    for j in range(NSA_GROUP // 2):
        o_ref[0, tile_rows, j * LANES:(j + 1) * LANES] = jnp.where(
            lo_half, outs[2 * j], outs[2 * j + 1]).astype(o_ref.dtype)


def _nsa(qkv, cmpkv, gates, bias_c, tbias, ov, et):
    b, _, s, _ = qkv.shape
    ncmp = cmpkv.shape[3]
    t = T_ATT
    rows = NSA_GROUP * t
    qw = NSA_GROUP * HEAD_DIM
    qtiles = qw // LANES
    kv_spec = lambda col: pl.BlockSpec((1, 1, s, LANES), lambda h, bi, col=col: (bi, col + h, 0, 0))
    cmp_spec = lambda kv: pl.BlockSpec((1, 1, 1, ncmp, LANES), lambda h, bi, kv=kv: (bi, kv, h, 0, 0))
    base = QKV_TILES_KV
    return pl.pallas_call(
        functools.partial(_nsa_kernel, seq=s, ncmp=ncmp),
        grid=(NSA_KV_HEADS, b),
        in_specs=[pl.BlockSpec((1, qtiles, s, LANES), lambda h, bi: (bi, h, 0, 0)),
                  kv_spec(base), kv_spec(base + 2), kv_spec(base + 4), kv_spec(base + 6),
                  cmp_spec(0), cmp_spec(1),
                  pl.BlockSpec((1, 1, s, LANES), lambda h, bi: (bi, h, 0, 0)),
                  pl.BlockSpec((1, (s // t) * ncmp, rows), lambda h, bi: (h, 0, 0)),
                  pl.BlockSpec((1, 4, rows, t), lambda h, bi: (h, 0, 0, 0)),
                  pl.BlockSpec((LANES, ncmp), lambda h, bi: (0, 0)),
                  pl.BlockSpec((s, LANES), lambda h, bi: (0, 0))],
        out_specs=pl.BlockSpec((1, s, qw), lambda h, bi: (bi, 0, h)),
        out_shape=jax.ShapeDtypeStruct((b, s, NSA_Q_W), bf16),
        scratch_shapes=[pltpu.VMEM((rows, 2 * LANES), bf16),
                        pltpu.VMEM((s, 2 * LANES), bf16),
                        pltpu.VMEM((rows, LANES), f32),
                        pltpu.VMEM((rows, LANES), f32),
                        pltpu.VMEM((rows, LANES), f32),
                        pltpu.VMEM((2, rows, t), f32),
                        pltpu.VMEM((2, s, LANES), bf16),
                        pltpu.VMEM((2, s, LANES), bf16)],
        compiler_params=_cparams(("parallel", "parallel")),
        name="nsa",
    )(qkv, qkv, qkv, qkv, qkv, cmpkv, cmpkv, gates, bias_c, tbias, ov, et)


def _fox_kernel(q_ref, k_ref, v_ref, qx_ref, kx_ref, tri_ref, o_ref, qaug_ref, kaug_ref, m_ref, acc_ref, s_ref,
                vaug_ref):
    t = TK_FOX
    tq = TQ_FOX
    seq = k_ref.shape[2]
    lane = lax.broadcasted_iota(i32, (tq, LANES), 1)
    lo_half = lane < HEAD_DIM
    kaug_ref[:, 0:LANES] = k_ref[0, 0]
    kaug_ref[:, LANES:2 * LANES] = kx_ref[0, 0]
    vaug_ref[0], vaug_ref[1] = _ones_values(v_ref[0, 0])

    def logits(kt):
        return _dot_nt(qaug_ref[...], kaug_ref[pl.ds(kt * t, t), :])

    def values(kt):
        return vaug_ref[0, pl.ds(kt * t, t), :], vaug_ref[1, pl.ds(kt * t, t), :]

    def query_tile(i, carry):
        rows = pl.ds(pl.multiple_of(i * tq, tq), tq)
        q = q_ref[0, 0, rows, :].astype(f32)
        qx = qx_ref[0, 0, rows, :].astype(f32)
        qaug_ref[0:tq, 0:LANES] = jnp.where(lo_half, q, 0.0).astype(bf16)
        qaug_ref[tq:2 * tq, 0:LANES] = jnp.where(lo_half, 0.0, q).astype(bf16)
        qaug_ref[0:tq, LANES:2 * LANES] = jnp.where(lane < XCOLS, qx, 0.0).astype(bf16)
        qaug_ref[tq:2 * tq, LANES:2 * LANES] = jnp.where((lane >= XCOLS) & (lane < 2 * XCOLS), qx, 0.0).astype(bf16)

        _flash_init(m_ref, acc_ref)
        _flash_loop(0, i, logits, values, s_ref, m_ref, acc_ref, last_fix=lambda s: s + tri_ref[...])
        o = _flash_finish(acc_ref)
        o_ref[0, rows, :] = jnp.where(lo_half, o[0:tq], o[tq:2 * tq]).astype(o_ref.dtype)
        return carry

    lax.fori_loop(0, seq // tq, query_tile, 0)


def _fox(qkv, qx, kx):
    b, _, s, _ = qkv.shape
    t = TK_FOX
    assert t == TQ_FOX
    r = np.arange(2 * TQ_FOX)[:, None] % TQ_FOX
    tri = jnp.asarray(np.where(np.arange(t)[None, :] <= r, 0.0, NEG_INF), f32)
    base = QKV_TILES_FOX
    return pl.pallas_call(
        _fox_kernel,
        grid=(b, FOX_PAIRS),
        in_specs=[pl.BlockSpec((1, 1, s, LANES), lambda bi, p: (bi, base + p, 0, 0)),
                  pl.BlockSpec((1, 1, s, LANES), lambda bi, p: (bi, base + FOX_PAIRS + p, 0, 0)),
                  pl.BlockSpec((1, 1, s, LANES), lambda bi, p: (bi, base + 2 * FOX_PAIRS + p, 0, 0)),
                  pl.BlockSpec((1, 1, s, LANES), lambda bi, p: (bi, p, 0, 0)),
                  pl.BlockSpec((1, 1, s, LANES), lambda bi, p: (bi, p, 0, 0)),
                  pl.BlockSpec((2 * TQ_FOX, t), lambda bi, p: (0, 0))],
        out_specs=pl.BlockSpec((1, s, LANES), lambda bi, p: (bi, 0, p)),
        out_shape=jax.ShapeDtypeStruct((b, s, FOX_W), bf16),
        scratch_shapes=[pltpu.VMEM((2 * TQ_FOX, 2 * LANES), bf16),
                        pltpu.VMEM((s, 2 * LANES), bf16),
                        pltpu.VMEM((2 * TQ_FOX, LANES), f32),
                        pltpu.VMEM((2 * TQ_FOX, LANES), f32),
                        pltpu.VMEM((2, 2 * TQ_FOX, t), f32),
                        pltpu.VMEM((2, s, LANES), bf16)],
        compiler_params=_cparams(("parallel", "parallel")),
        name="fox",
    )(qkv, qkv, qkv, qx, kx, tri)


def _merge_kernel(yn_ref, yf_ref, xb_ref, x_ref, wn_ref, wf_ref, wmg_ref, wo_ref, g_ref, b_ref,
                  xo_ref, xbo_ref):
    mg = _dot(xb_ref[...], wmg_ref[...])
    merged = (_sigmoid(mg[:, 0:D_MODEL]) * _dot(yn_ref[...], wn_ref[...])
              + _sigmoid(mg[:, D_MODEL:2 * D_MODEL]) * _dot(yf_ref[...], wf_ref[...]))
    hmix = _dot(merged.astype(bf16), wo_ref[...])
    xn = _layer_norm(DN_ALPHA * x_ref[...] + hmix, g_ref[...], b_ref[...])
    xo_ref[...] = xn
    xbo_ref[...] = xn.astype(bf16)


def _merge(yn, yf, xb, x, wn, wf, wmg, wo, g, bb):
    m = x.shape[0]
    tm = TM_MERGE
    row = lambda w: pl.BlockSpec((tm, w), lambda i: (i, 0))
    full = lambda a: pl.BlockSpec(a.shape, lambda i: (0, 0))
    return pl.pallas_call(
        _merge_kernel,
        grid=(m // tm,),
        in_specs=[row(NSA_Q_W), row(FOX_W), row(D_MODEL), row(D_MODEL),
                  full(wn), full(wf), full(wmg), full(wo), full(g), full(bb)],
        out_specs=[row(D_MODEL), row(D_MODEL)],
        out_shape=[jax.ShapeDtypeStruct((m, D_MODEL), f32), jax.ShapeDtypeStruct((m, D_MODEL), bf16)],
        compiler_params=_cparams(("parallel",)),
        name="merge",
    )(yn, yf, xb, x, wn, wf, wmg, wo, g, bb)


def _ffn_kernel(xb_ref, x_ref, wg_ref, wu_ref, wd_ref, g_ref, b_ref, xo_ref, xbo_ref, acc_ref, *, nf):
    f = pl.program_id(1)

    @pl.when(f == 0)
    def _():
        acc_ref[...] = jnp.zeros_like(acc_ref)

    xb = xb_ref[...]
    gate = _dot(xb, wg_ref[...])
    up = _dot(xb, wu_ref[...])
    act = (gate * _sigmoid(gate) * up).astype(bf16)
    acc_ref[...] += _dot(act, wd_ref[...])

    @pl.when(f == nf - 1)
    def _():
        xn = _layer_norm(DN_ALPHA * x_ref[...] + acc_ref[...], g_ref[...], b_ref[...])
        xo_ref[...] = xn
        xbo_ref[...] = xn.astype(bf16)


def _ffn(xb, x, wg, wu, wd, g, bb):
    m = x.shape[0]
    dff = wg.shape[1]
    tm, tf = TM_FFN, TF_FFN
    nf = dff // tf
    row = pl.BlockSpec((tm, D_MODEL), lambda i, f: (i, 0))
    vec = pl.BlockSpec((1, D_MODEL), lambda i, f: (0, 0))
    return pl.pallas_call(
        functools.partial(_ffn_kernel, nf=nf),
        grid=(m // tm, nf),
        in_specs=[row, row,
                  pl.BlockSpec((D_MODEL, tf), lambda i, f: (0, f)),
                  pl.BlockSpec((D_MODEL, tf), lambda i, f: (0, f)),
                  pl.BlockSpec((tf, D_MODEL), lambda i, f: (f, 0)),
                  vec, vec],
        out_specs=[row, row],
        out_shape=[jax.ShapeDtypeStruct((m, D_MODEL), f32), jax.ShapeDtypeStruct((m, D_MODEL), bf16)],
        scratch_shapes=[pltpu.VMEM((tm, D_MODEL), f32)],
        compiler_params=_cparams(("parallel", "arbitrary")),
        name="ffn",
    )(xb, x, wg, wu, wd, g, bb)


def _router_kernel(x_ref, r_ref, tri_ref, gate_ref, rank_ref, *, seq):
    tb = tri_ref.shape[0]
    x = x_ref[0]
    xh = x.astype(bf16)
    xl = (x - xh.astype(f32)).astype(bf16)
    r = r_ref[...]
    rh = r.astype(bf16)
    rl = (r - rh.astype(f32)).astype(bf16)
    logits = _dot(xh, rh) + _dot(xh, rl) + _dot(xl, rh)
    lane = lax.broadcasted_iota(i32, (seq, LANES), 1).astype(f32)
    low = -3.0e38
    lg = jnp.where(lane < N_EXPERTS, logits, low)
    m1 = jnp.max(lg, axis=-1, keepdims=True)
    i1 = jnp.min(jnp.where(lg == m1, lane, float(LANES)), axis=-1, keepdims=True)
    lg2 = jnp.where(lane == i1, low, lg)
    m2 = jnp.max(lg2, axis=-1, keepdims=True)
    i2 = jnp.min(jnp.where(lg2 == m2, lane, float(LANES)), axis=-1, keepdims=True)
    e2 = jnp.exp(m2 - m1)
    den = 1.0 + e2
    gate_ref[0] = jnp.where(lane == i1, 1.0 / den, jnp.where(lane == i2, e2 / den, 0.0))
    sel = (lane == i1) | (lane == i2)
    selb = jnp.where(sel, 1.0, 0.0).astype(bf16)
    carry = jnp.zeros((1, LANES), f32)
    for blk in range(seq // tb):
        sl = slice(blk * tb, (blk + 1) * tb)
        c = _dot(tri_ref[...], selb[sl]) + carry
        carry = c[tb - 1:tb, :]
        rank_ref[0, sl, :] = jnp.where(sel[sl], c - 1.0, -1.0)


def _router(x3, router_pad, tri):
    b, s, _ = x3.shape
    return pl.pallas_call(
        functools.partial(_router_kernel, seq=s),
        grid=(b,),
        in_specs=[pl.BlockSpec((1, s, D_MODEL), lambda i: (i, 0, 0)),
                  pl.BlockSpec((D_MODEL, LANES), lambda i: (0, 0)),
                  pl.BlockSpec(tri.shape, lambda i: (0, 0))],
        out_specs=[pl.BlockSpec((1, s, LANES), lambda i: (i, 0, 0)),
                   pl.BlockSpec((1, s, LANES), lambda i: (i, 0, 0))],
        out_shape=[jax.ShapeDtypeStruct((b, s, LANES), f32),
                   jax.ShapeDtypeStruct((b, s, LANES), f32)],
        compiler_params=_cparams(("parallel",)),
        name="router",
    )(x3, router_pad, tri)


def _moe_kernel(tot_ref, tot_al_ref, off_ref, npc_ref, xb_ref, rankt_ref, gatet_ref, wg_ref, wu_ref, wd_ref,
                y_ref, xg_ref, acc_ref, *, seq, nf):
    tr = TR_MOE
    half = tr // 2
    nsubc = seq // SUB_MOE
    c = pl.program_id(0)
    e = pl.program_id(1)
    f = pl.program_id(2)
    ce = c * N_EXPERTS + e
    tot = tot_ref[ce]
    rem = tot % tr
    nfull = tot // tr + jnp.where(rem > half, 1, 0)
    has_tail = (rem > 0) & (rem <= half)
    tail0 = pl.multiple_of(nfull * tr, half)

    def pieces(fn):
        for q in range(nsubc):
            base = off_ref[ce * nsubc + q]

            def body(p, carry, q=q, base=base):
                fn(q, pl.multiple_of(base + p * PIECE_MOE, ROW_ALIGN))
                return carry
            lax.fori_loop(0, npc_ref[ce * nsubc + q], body, 0)

    def onehot_rows(q, r0):
        rk = rankt_ref[0, pl.ds(e, 1), q * SUB_MOE:(q + 1) * SUB_MOE]
        want = (r0 + lax.broadcasted_iota(i32, (PIECE_MOE, SUB_MOE), 0)).astype(f32)
        return rk == want

    def gather(q, r0):
        p = jnp.where(onehot_rows(q, r0), 1.0, 0.0).astype(bf16)
        xg_ref[pl.ds(r0, PIECE_MOE), :] += _dot(p, xb_ref[0, q * SUB_MOE:(q + 1) * SUB_MOE, :]).astype(bf16)

    def clear(r0, nr):
        xg_ref[pl.ds(r0, nr), :] = jnp.zeros((nr, D_MODEL), bf16)
        acc_ref[pl.ds(r0, nr), :] = jnp.zeros((nr, D_MODEL), f32)

    def hidden(r0, nr):
        xs = xg_ref[pl.ds(r0, nr), :]
        gate = _dot(xs, wg_ref[0, 0])
        up = _dot(xs, wu_ref[0, 0])
        act = (gate * _sigmoid(gate) * up).astype(bf16)
        acc_ref[pl.ds(r0, nr), :] += _dot(act, wd_ref[0, 0])

    def combine(q, r0):
        grow = gatet_ref[0, pl.ds(e, 1), q * SUB_MOE:(q + 1) * SUB_MOE]
        hit = onehot_rows(q, r0)
        w = jnp.sum(jnp.where(hit, grow, 0.0), axis=-1, keepdims=True)
        z = (acc_ref[pl.ds(r0, PIECE_MOE), :] * w).astype(bf16)
        p = jnp.where(hit, 1.0, 0.0).astype(bf16)
        y_ref[0, q * SUB_MOE:(q + 1) * SUB_MOE, :] += lax.dot_general(
            p, z, (((0,), (0,)), ((), ())), preferred_element_type=f32)

    def tiles(fn):
        def body(s, carry):
            fn(pl.multiple_of(s * tr, tr), tr)
            return carry
        lax.fori_loop(0, nfull, body, 0)

        @pl.when(has_tail)
        def _():
            fn(tail0, half)

    def zero(j, carry):
        y_ref[0, pl.ds(j * tr, tr), :] = jnp.zeros((tr, D_MODEL), f32)
        return carry

    lax.fori_loop(0, jnp.where((e == 0) & (f == 0), seq // tr, 0), zero, 0)

    @pl.when(f == 0)
    def _():
        tiles(clear)
        clear(pl.multiple_of(tot_al_ref[ce], ROW_ALIGN), tr)
        pieces(gather)

    tiles(hidden)

    @pl.when(f == nf - 1)
    def _():
        pieces(combine)


def _moe(xb3, gate, rank, wg, wu, wd, layer):
    b, s, _ = xb3.shape
    dff = wg.shape[3]
    nf = dff // TF_MOE
    nsubc = s // SUB_MOE
    rk = rank[:, :, :N_EXPERTS]
    cnt = jnp.sum((rk >= 0.0).reshape(b, nsubc, SUB_MOE, N_EXPERTS), axis=2).astype(i32)
    first = jnp.cumsum(cnt, axis=1) - cnt
    off = first // ROW_ALIGN * ROW_ALIGN
    npc = jnp.where(cnt > 0, (first - off + cnt + PIECE_MOE - 1) // PIECE_MOE, 0)
    post = rk.transpose(0, 2, 1)
    gatet = gate[:, :, :N_EXPERTS].transpose(0, 2, 1)
    tot = jnp.sum(cnt, axis=1).reshape(-1)
    tot_al = (tot + ROW_ALIGN - 1) // ROW_ALIGN * ROW_ALIGN
    off_flat = off.transpose(0, 2, 1).reshape(-1)
    npc = npc.transpose(0, 2, 1).reshape(-1)
    buf_rows = s + ROW_ALIGN + PIECE_MOE + TR_MOE
    grid_spec = pltpu.PrefetchScalarGridSpec(
        num_scalar_prefetch=4,
        grid=(b, N_EXPERTS, nf),
        in_specs=[pl.BlockSpec((1, s, D_MODEL), lambda c, e, f, *_: (c, 0, 0)),
                  pl.BlockSpec((1, N_EXPERTS, s), lambda c, e, f, *_: (c, 0, 0)),
                  pl.BlockSpec((1, N_EXPERTS, s), lambda c, e, f, *_: (c, 0, 0)),
                  pl.BlockSpec((1, 1, D_MODEL, TF_MOE), lambda c, e, f, *_: (layer, e, 0, f)),
                  pl.BlockSpec((1, 1, D_MODEL, TF_MOE), lambda c, e, f, *_: (layer, e, 0, f)),
                  pl.BlockSpec((1, 1, TF_MOE, D_MODEL), lambda c, e, f, *_: (layer, e, f, 0))],
        out_specs=pl.BlockSpec((1, s, D_MODEL), lambda c, e, f, *_: (c, 0, 0)),
        scratch_shapes=[pltpu.VMEM((buf_rows, D_MODEL), bf16), pltpu.VMEM((buf_rows, D_MODEL), f32)],
    )
    return pl.pallas_call(
        functools.partial(_moe_kernel, seq=s, nf=nf),
        grid_spec=grid_spec,
        out_shape=jax.ShapeDtypeStruct((b, s, D_MODEL), f32),
        compiler_params=_cparams(("parallel", "arbitrary", "arbitrary")),
        name="moe",
    )(tot, tot_al, off_flat, npc, xb3, post, gatet, wg, wu, wd)


def _resln_kernel(x_ref, y_ref, g_ref, b_ref, xo_ref, xbo_ref):
    xn = _layer_norm(DN_ALPHA * x_ref[...] + y_ref[...], g_ref[...], b_ref[...])
    xo_ref[...] = xn
    xbo_ref[...] = xn.astype(bf16)


def _resln(x, y, g, bb):
    m = x.shape[0]
    tm = TM_FFN
    row = pl.BlockSpec((tm, D_MODEL), lambda i: (i, 0))
    vec = pl.BlockSpec((1, D_MODEL), lambda i: (0, 0))
    return pl.pallas_call(
        _resln_kernel,
        grid=(m // tm,),
        in_specs=[row, row, vec, vec],
        out_specs=[row, row],
        out_shape=[jax.ShapeDtypeStruct((m, D_MODEL), f32), jax.ShapeDtypeStruct((m, D_MODEL), bf16)],
        compiler_params=_cparams(("parallel",)),
        name="resln",
    )(x, y, g, bb)


def _t5_bucket(dist):
    n = jnp.maximum(dist, 0)
    max_exact = REL_BUCKETS // 2
    large = max_exact + (jnp.log(jnp.maximum(n, 1).astype(f32) / max_exact)
                         / math.log(REL_MAX_DIST / max_exact) * (REL_BUCKETS - max_exact)).astype(i32)
    large = jnp.minimum(large, REL_BUCKETS - 1)
    return jnp.where(n < max_exact, n, large)


def _bias_of_dist(rel_bias, dist):
    onehot = (_t5_bucket(dist)[None] == jnp.arange(REL_BUCKETS).reshape((-1,) + (1,) * dist.ndim)).astype(f32)
    return LOG2E * jnp.einsum("kh,k...->h...", rel_bias.astype(f32), onehot, precision=lax.Precision.HIGHEST)


def _bias_tables(rel_bias, seq):
    t = T_ATT
    rows = NSA_GROUP * t
    ncmp = seq // CMP_STRIDE
    d0 = jnp.arange(t)[:, None] - jnp.arange(t)[None, :]
    offs = jnp.array([0, t, 2 * t, WINDOW]).reshape(4, 1, 1)
    kinds = _bias_of_dist(rel_bias, offs + d0[None])
    mask = jnp.stack([d0 >= 0, d0 == d0, d0 == d0, d0 < 0])
    kinds = jnp.where(mask[None], kinds, NEG_INF)
    order = np.array(NSA_ROW_ORDER)
    tbias = kinds.reshape(NSA_KV_HEADS, NSA_GROUP, 4, t, t)[:, order].transpose(0, 2, 1, 3, 4)
    tbias = tbias.reshape(NSA_KV_HEADS, 4, rows, t)
    cend = jnp.arange(ncmp) * CMP_STRIDE + CMP_BLOCK - 1
    dist_c = jnp.arange(seq)[:, None] - cend[None, :]
    bc = jnp.where(dist_c >= 0, _bias_of_dist(rel_bias, dist_c), NEG_INF)
    bc = bc.reshape(NSA_KV_HEADS, NSA_GROUP, seq // t, t, ncmp)[:, order].transpose(0, 2, 4, 1, 3)
    return tbias, bc.reshape(NSA_KV_HEADS, (seq // t) * ncmp, rows)


def _selection_constants(seq):
    ncmp = seq // CMP_STRIDE
    nsel = seq // SEL_BLOCK
    c0 = np.arange(ncmp)[:, None] * CMP_STRIDE
    s0 = np.arange(LANES)[None, :] * SEL_BLOCK
    ov = np.maximum(np.minimum(c0 + CMP_BLOCK, s0 + SEL_BLOCK) - np.maximum(c0, s0), 0) / CMP_BLOCK
    ov[ncmp - 1, :] = 0.0
    ov[:, nsel:] = 0.0
    et = (np.arange(seq)[:, None] // SEL_BLOCK == np.arange(LANES)[None, :]).astype(np.float32)
    return jnp.asarray(ov.T, bf16), jnp.asarray(et, bf16)


def _fox_placement():
    xw = FOX_PAIRS * LANES
    pq = np.zeros((3, LANES, xw), np.float32)
    pk = np.zeros((3, LANES, xw), np.float32)
    oq = np.zeros((1, xw), np.float32)
    ok = np.zeros((1, xw), np.float32)
    for p in range(FOX_PAIRS):
        for hh in range(2):
            src = FGATE_LANE + 2 * p + hh
            base = p * LANES + hh * XCOLS
            for part in range(3):
                pk[part, src, base + part] = -1.0
                pq[part, src, base + 3 + part] = 1.0
                oq[0, base + part] = 1.0
                ok[0, base + 3 + part] = 1.0
    return jnp.asarray(pq, bf16), jnp.asarray(pk, bf16), jnp.asarray(oq), jnp.asarray(ok)


def _layer_weights(w_in, layer_pe, w1, w2, f_bias):
    offs = np.cumsum((NSA_Q_W, 6 * 2 * HEAD_DIM, 3 * NSA_HEADS, 3 * FOX_W, FOX_HEADS, 2 * D_MODEL))
    kv0, g0, fx0, ff0, mg0 = offs[0], offs[1], offs[2], offs[3], offs[4]
    scale = HEAD_DIM ** -0.5 * LOG2E
    kvw = NSA_KV_HEADS * HEAD_DIM
    w_kv = w_in[:, kv0:g0]
    w_kvdup = jnp.repeat(w_kv[:, 2 * kvw:].reshape(D_MODEL, 4 * NSA_KV_HEADS, 1, HEAD_DIM), 2, axis=2)
    w_kvdup = w_kvdup.reshape(D_MODEL, 8 * kvw)
    w_qkv = jnp.concatenate([w_in[:, :NSA_Q_W] * scale, w_kvdup, w_in[:, fx0:fx0 + FOX_W] * scale,
                             w_in[:, fx0 + FOX_W:ff0]], axis=1).astype(bf16)
    zeros = lambda n: jnp.zeros((D_MODEL, n), w_in.dtype)
    ng = 3 * NSA_GROUP
    w_g = w_in[:, g0:fx0].reshape(D_MODEL, NSA_KV_HEADS, NSA_GROUP, 3).transpose(0, 1, 3, 2)
    gate_cols = lambda h: w_g[:, h].reshape(D_MODEL, ng)
    w_aux = jnp.concatenate([w_kv[:, :2 * kvw],
                             gate_cols(0), zeros(FGATE_LANE - ng), w_in[:, ff0:mg0],
                             zeros(LANES - FGATE_LANE - FOX_HEADS),
                             gate_cols(1), zeros(LANES - ng)], axis=1).astype(bf16)
    fb_row = jnp.zeros((1, LANES), f32).at[0, FGATE_LANE:FGATE_LANE + FOX_HEADS].set(f_bias.astype(f32))
    pe2 = jnp.tile(layer_pe.astype(f32), (1, 1, NSA_KV_HEADS))
    w1r = w1.reshape(2, CMP_BLOCK, HEAD_DIM, CMP_HIDDEN).astype(bf16)
    zero = jnp.zeros_like(w1r)
    w1bd = jnp.concatenate([jnp.concatenate([w1r, zero], axis=-1),
                            jnp.concatenate([zero, w1r], axis=-1)], axis=-2)
    w2d = jnp.concatenate([w2, w2], axis=-1).astype(bf16)
    return (w_qkv, w_aux, w_in[:, mg0:].astype(bf16), fb_row, pe2, w1bd[:, :CMP_STRIDE], w1bd[:, CMP_STRIDE:],
            w2d)


def kernel(x, w_in, nsa_cmp_pe, nsa_cmp_w1, nsa_cmp_w2, fox_f_bias, w_nsa_branch, w_fox_branch, w_out,
           rel_bias, ln1_g, ln1_b, ln2_g, ln2_b, dense_w_gate, dense_w_up, dense_w_down, moe_router,
           moe_w_gate, moe_w_up, moe_w_down):
    b, s, d = x.shape
    m = b * s
    tbias, bias_c = _bias_tables(rel_bias, s)
    ov, et = _selection_constants(s)
    pq, pk, oq, ok = _fox_placement()
    tri128 = jnp.asarray(np.tril(np.ones((LANES, LANES), np.float32)), bf16)
    tri256 = jnp.asarray(np.tril(np.ones((256, 256), np.float32)), bf16)

    moe_wg, moe_wu, moe_wd = moe_w_gate.astype(bf16), moe_w_up.astype(bf16), moe_w_down.astype(bf16)
    xf = x.reshape(m, d).astype(f32)
    xb = xf.astype(bf16)
    for layer in range(DEPTH):
        w_qkv, w_aux, w_mg, fb_row, pe2, w1t, w1b, w2d = _layer_weights(
            w_in[layer], nsa_cmp_pe[layer], nsa_cmp_w1[layer], nsa_cmp_w2[layer], fox_f_bias[layer])
        qkv, aux = _proj(xb, w_qkv, w_aux, b)
        aux = aux.reshape(b, s, AUX_W)
        gates, qx, kx = _gateprep(aux, fb_row, tri128, pq, pk, oq, ok)
        cmpkv = _compress(aux, pe2, w1t, w1b, w2d)
        y_nsa = _nsa(qkv, cmpkv, gates, bias_c, tbias, ov, et).reshape(m, NSA_Q_W)
        y_fox = _fox(qkv, qx, kx).reshape(m, FOX_W)
        xf, xb = _merge(y_nsa, y_fox, xb, xf, w_nsa_branch[layer].astype(bf16),
                        w_fox_branch[layer].astype(bf16), w_mg, w_out[layer].astype(bf16),
                        ln1_g[layer].reshape(1, d), ln1_b[layer].reshape(1, d))
        j = layer // 2
        g2, b2 = ln2_g[layer].reshape(1, d), ln2_b[layer].reshape(1, d)
        if layer % 2 == 0:
            xf, xb = _ffn(xb, xf, dense_w_gate[j].astype(bf16), dense_w_up[j].astype(bf16),
                          dense_w_down[j].astype(bf16), g2, b2)
        else:
            router_pad = jnp.zeros((d, LANES), f32).at[:, :N_EXPERTS].set(moe_router[j].astype(f32))
            gate, rank = _router(xf.reshape(b, s, d), router_pad, tri256)
            y = _moe(xb.reshape(b, s, d), gate, rank, moe_wg, moe_wu, moe_wd, j)
            xf, xb = _resln(xf, y.reshape(m, d), g2, b2)
    return xf.reshape(b, s, d).astype(x.dtype)
```

```python
import functools
import math

import numpy as np
import jax
import jax.numpy as jnp
from jax import lax
from jax.experimental import pallas as pl
from jax.experimental.pallas import tpu as pltpu

f32 = jnp.float32
bf16 = jnp.bfloat16
i32 = jnp.int32

D_MODEL = 1024
HEAD_DIM = 64
LANES = 128
NSA_HEADS = 8
NSA_KV_HEADS = 2
NSA_GROUP = NSA_HEADS // NSA_KV_HEADS
NSA_ROW_ORDER = (0, 2, 1, 3)
FOX_HEADS = 8
FOX_PAIRS = FOX_HEADS // 2
CMP_BLOCK = 32
CMP_STRIDE = 16
CMP_HIDDEN = 128
SEL_BLOCK = 64
N_SELECT = 16
N_LOCAL_SEL = 2
WINDOW = 512
REL_BUCKETS = 32
REL_MAX_DIST = 128
N_EXPERTS = 8
DEPTH = 4
DN_ALPHA = (2 * DEPTH) ** 0.25
LN_EPS = 1e-5
FORCED_SCORE = 1e4
NEG_INF = -1e30
LOG2E = math.log2(math.e)

NSA_Q_W = NSA_HEADS * HEAD_DIM
FOX_W = FOX_HEADS * HEAD_DIM
QKV_TILES_KV = NSA_Q_W // LANES
QKV_TILES_FOX = QKV_TILES_KV + 4 * NSA_KV_HEADS
QKV_W = (QKV_TILES_FOX + 3 * FOX_PAIRS) * LANES
AUX_W = 4 * LANES
FGATE_LANE = 24
XCOLS = 6

T_ATT = 256
TQ_FOX = 512
TK_FOX = 512
TM_PROJ = 512
TM_MERGE = 512
TM_FFN = 512
TF_FFN = 2816
TR_MOE = 256
SUB_MOE = 512
PIECE_MOE = 192
ROW_ALIGN = 16
TF_MOE = 896
VMEM_LIMIT = 56 * 1024 * 1024


def _cparams(sem):
    return pltpu.CompilerParams(dimension_semantics=sem, vmem_limit_bytes=VMEM_LIMIT)


def _dot(a, b):
    return jnp.dot(a, b, preferred_element_type=f32)


def _dot_nt(a, b):
    return lax.dot_general(a, b, (((1,), (1,)), ((), ())), preferred_element_type=f32)


def _sigmoid(x):
    return 1.0 / (1.0 + jnp.exp(-x))


def _layer_norm(z, g, b):
    mu = jnp.mean(z, axis=-1, keepdims=True)
    zc = z - mu
    var = jnp.mean(zc * zc, axis=-1, keepdims=True)
    return zc * lax.rsqrt(var + LN_EPS) * g + b


def _split3(x):
    hi = x.astype(bf16)
    r1 = x - hi.astype(f32)
    mid = r1.astype(bf16)
    lo = (r1 - mid.astype(f32)).astype(bf16)
    return hi, mid, lo


def _proj_kernel(x_ref, w_ref, wa_ref, o_ref, oa_ref):
    x = x_ref[...]
    res = _dot(x, w_ref[...])
    for j in range(o_ref.shape[1]):
        o_ref[0, j] = res[:, j * LANES:(j + 1) * LANES].astype(o_ref.dtype)
    oa_ref[...] = _dot(x, wa_ref[...])


def _proj(xb, w, wa, batch):
    m, k = xb.shape
    n, na = w.shape[1], wa.shape[1]
    nb = m // batch // TM_PROJ
    return pl.pallas_call(
        _proj_kernel,
        grid=(m // TM_PROJ,),
        in_specs=[pl.BlockSpec((TM_PROJ, k), lambda i: (i, 0)),
                  pl.BlockSpec((k, n), lambda i: (0, 0)),
                  pl.BlockSpec((k, na), lambda i: (0, 0))],
        out_specs=[pl.BlockSpec((1, n // LANES, TM_PROJ, LANES), lambda i: (i // nb, 0, i % nb, 0)),
                   pl.BlockSpec((TM_PROJ, na), lambda i: (i, 0))],
        out_shape=[jax.ShapeDtypeStruct((batch, n // LANES, m // batch, LANES), bf16),
                   jax.ShapeDtypeStruct((m, na), f32)],
        compiler_params=_cparams(("parallel",)),
        name="proj",
    )(xb, w, wa)


def _gateprep_kernel(a_ref, fb_ref, tri_ref, pq_ref, pk_ref, oq_ref, ok_ref, g_ref, qx_ref, kx_ref, *, seq):
    tb = LANES
    tri = tri_ref[...]
    local = []
    for blk in range(seq // tb):
        sl = slice(blk * tb, (blk + 1) * tb)
        va = a_ref[0, sl, 0:LANES]
        g_ref[0, 0, sl, :] = _sigmoid(va)
        g_ref[0, 1, sl, :] = _sigmoid(a_ref[0, sl, LANES:2 * LANES])
        z = va + fb_ref[...]
        logf = jnp.minimum(z, 0.0) - jnp.log1p(jnp.exp(-jnp.abs(z)))
        hi, mid, lo = _split3(logf)
        local.append(_dot(tri, hi) + _dot(tri, mid) + _dot(tri, lo))
    carry = jnp.zeros((1, LANES), f32)
    for blk in range(seq // tb):
        sl = slice(blk * tb, (blk + 1) * tb)
        c = local[blk] + carry
        carry = c[tb - 1:tb, :]
        chi, cmid, clo = _split3(c * LOG2E)
        qx = _dot(chi, pq_ref[0]) + _dot(cmid, pq_ref[1]) + _dot(clo, pq_ref[2]) + oq_ref[...]
        kx = _dot(chi, pk_ref[0]) + _dot(cmid, pk_ref[1]) + _dot(clo, pk_ref[2]) + ok_ref[...]
        for p in range(FOX_PAIRS):
            qx_ref[0, p, sl, :] = qx[:, p * LANES:(p + 1) * LANES].astype(bf16)
            kx_ref[0, p, sl, :] = kx[:, p * LANES:(p + 1) * LANES].astype(bf16)


def _gateprep(aux, fb_row, tri, pq, pk, oq, ok):
    b, s, _ = aux.shape
    const2 = lambda a: pl.BlockSpec(a.shape, lambda i: (0, 0))
    const3 = lambda a: pl.BlockSpec(a.shape, lambda i: (0, 0, 0))
    return pl.pallas_call(
        functools.partial(_gateprep_kernel, seq=s),
        grid=(b,),
        in_specs=[pl.BlockSpec((1, s, 2 * LANES), lambda i: (i, 0, 1)),
                  const2(fb_row), const2(tri), const3(pq), const3(pk), const2(oq), const2(ok)],
        out_specs=[pl.BlockSpec((1, 2, s, LANES), lambda i: (i, 0, 0, 0)),
                   pl.BlockSpec((1, FOX_PAIRS, s, LANES), lambda i: (i, 0, 0, 0)),
                   pl.BlockSpec((1, FOX_PAIRS, s, LANES), lambda i: (i, 0, 0, 0))],
        out_shape=[jax.ShapeDtypeStruct((b, 2, s, LANES), f32),
                   jax.ShapeDtypeStruct((b, FOX_PAIRS, s, LANES), bf16),
                   jax.ShapeDtypeStruct((b, FOX_PAIRS, s, LANES), bf16)],
        compiler_params=_cparams(("parallel",)),
        name="gateprep",
    )(aux, fb_row, tri, pq, pk, oq, ok)


def _gelu_tanh(x):
    c = math.sqrt(2.0 / math.pi)
    return x * (0.5 * (1.0 + jnp.tanh(c * (x + 0.044715 * (x * x * x)))))


def _compress_kernel(a_ref, pe_ref, w1t_ref, w1b_ref, w2_ref, o_ref, *, nhalf):
    top = jnp.zeros((nhalf, NSA_KV_HEADS * CMP_HIDDEN), f32)
    bot = jnp.zeros((nhalf, NSA_KV_HEADS * CMP_HIDDEN), f32)
    for l in range(CMP_STRIDE):
        rows = a_ref[0, pl.ds(l, nhalf, stride=CMP_STRIDE), :]
        top = top + _dot((rows + pe_ref[0, l:l + 1, :]).astype(bf16), w1t_ref[0, l])
        bot = bot + _dot((rows + pe_ref[0, CMP_STRIDE + l:CMP_STRIDE + l + 1, :]).astype(bf16), w1b_ref[0, l])
    pre = top + pltpu.roll(bot, nhalf - 1, 0)
    act = _gelu_tanh(pre).astype(bf16)
    for h in range(NSA_KV_HEADS):
        o_ref[0, 0, h] = _dot(act[:, h * CMP_HIDDEN:(h + 1) * CMP_HIDDEN], w2_ref[0]).astype(o_ref.dtype)


def _compress(aux, pe2, w1t, w1b, w2d):
    b, s, _ = aux.shape
    nhalf = s // CMP_STRIDE
    w1_spec = pl.BlockSpec((1,) + w1t.shape[1:], lambda i, j: (j, 0, 0, 0))
    return pl.pallas_call(
        functools.partial(_compress_kernel, nhalf=nhalf),
        grid=(b, 2),
        in_specs=[pl.BlockSpec((1, s, LANES), lambda i, j: (i, 0, j)),
                  pl.BlockSpec((1, CMP_BLOCK, LANES), lambda i, j: (j, 0, 0)),
                  w1_spec, w1_spec,
                  pl.BlockSpec((1, CMP_HIDDEN, LANES), lambda i, j: (j, 0, 0))],
        out_specs=pl.BlockSpec((1, 1, NSA_KV_HEADS, nhalf, LANES), lambda i, j: (i, j, 0, 0, 0)),
        out_shape=jax.ShapeDtypeStruct((b, 2, NSA_KV_HEADS, nhalf, LANES), bf16),
        compiler_params=_cparams(("parallel", "parallel")),
        name="compress",
    )(aux, pe2, w1t, w1b, w2d)


def _flash_init(m_ref, acc_ref):
    m_ref[...] = jnp.full(m_ref.shape, NEG_INF, f32)
    acc_ref[...] = jnp.zeros(acc_ref.shape, f32)


def _ones_values(v):
    lo_half = lax.broadcasted_iota(i32, v.shape, 1) < HEAD_DIM
    vf = v.astype(f32)
    return jnp.where(lo_half, vf, 1.0).astype(bf16), jnp.where(lo_half, 1.0, vf).astype(bf16)


def _flash_step(s, v_lo, v_hi, m_ref, acc_ref):
    nk = s.shape[1] // LANES
    half = s.shape[0] // 2
    cols = [s[:, c * LANES:(c + 1) * LANES] for c in range(nk)]
    mx = cols[0]
    for c in cols[1:]:
        mx = jnp.maximum(mx, c)
    m_old = m_ref[...]
    m_new = jnp.maximum(m_old, jnp.broadcast_to(jnp.max(mx, axis=-1, keepdims=True), m_old.shape))
    alpha = jnp.exp2(m_old - m_new)
    p = jnp.concatenate([jnp.exp2(c - m_new).astype(bf16) for c in cols], axis=1)
    pv = jnp.concatenate([_dot(p[0:half], v_lo), _dot(p[half:], v_hi)], axis=0)
    acc_ref[...] = alpha * acc_ref[...] + pv
    m_ref[...] = m_new


def _flash_loop(first, last, logits_fn, v_fn, s_ref, m_ref, acc_ref, last_fix=None):
    sa, sb = s_ref.at[0], s_ref.at[1]
    n = last - first + 1
    pairs = (n - 1) // 2

    def step(buf, kt, fix=None):
        s = buf[...] if fix is None else fix(buf[...])
        _flash_step(s, *v_fn(kt), m_ref, acc_ref)

    sa[...] = logits_fn(first)

    def body(j, carry):
        kt = first + 2 * j
        sb[...] = logits_fn(kt + 1)
        step(sa, kt)
        sa[...] = logits_fn(kt + 2)
        step(sb, kt + 1)
        return carry

    lax.fori_loop(0, pairs, body, 0)
    two_left = n - 2 * pairs == 2

    @pl.when(two_left)
    def _():
        sb[...] = logits_fn(last)
        step(sa, last - 1)
        step(sb, last, last_fix)

    @pl.when(jnp.logical_not(two_left))
    def _():
        step(sa, last, last_fix)


def _flash_finish(acc_ref):
    acc = acc_ref[...]
    return acc * (1.0 / jnp.maximum(pltpu.roll(acc, HEAD_DIM, 1), 1e-30))


def _nsa_kernel(q_ref, ks_ref, vs_ref, kw_ref, vw_ref, kc_ref, vc_ref, gate_ref, bc_ref, tb_ref, ov_ref,
                et_ref, o_ref, qaug_ref, kaug_ref, m_ref, acc_ref, os_ref, s_ref, vsel_ref, vwin_ref, *,
                seq, ncmp):
    kaug_ref[:, 0:LANES] = ks_ref[0, 0]
    kaug_ref[:, LANES:2 * LANES] = et_ref[...]
    vsel_ref[0], vsel_ref[1] = _ones_values(vs_ref[0, 0])
    vwin_ref[0], vwin_ref[1] = _ones_values(vw_ref[0, 0])

    def query_tile(i, carry):
        _nsa_tile(i, q_ref, kw_ref, kc_ref, vc_ref, gate_ref, bc_ref, tb_ref, ov_ref, o_ref, qaug_ref,
                  kaug_ref, m_ref, acc_ref, os_ref, s_ref, vsel_ref, vwin_ref, seq=seq, ncmp=ncmp)
        return carry

    lax.fori_loop(0, seq // T_ATT, query_tile, 0)


def _nsa_tile(i, q_ref, kw_ref, kc_ref, vc_ref, gate_ref, bc_ref, tb_ref, ov_ref, o_ref, qaug_ref,
              kaug_ref, m_ref, acc_ref, os_ref, s_ref, vsel_ref, vwin_ref, *, seq, ncmp):
    t = T_ATT
    rows = NSA_GROUP * t
    nsel = seq // SEL_BLOCK
    topn = min(N_SELECT, nsel)
    t0 = i * t
    tile_rows = pl.ds(pl.multiple_of(t0, t), t)
    lane = lax.broadcasted_iota(i32, (t, LANES), 1)
    lo_half = lane < HEAD_DIM

    for rb, g in enumerate(NSA_ROW_ORDER):
        qg = q_ref[0, g // 2, tile_rows, :].astype(f32)
        keep = lo_half if g % 2 == 0 else jnp.logical_not(lo_half)
        qaug_ref[rb * t:(rb + 1) * t, 0:LANES] = jnp.where(keep, qg, 0.0).astype(bf16)
    qs = qaug_ref[:, 0:LANES]

    s = _dot_nt(kc_ref[0, 0, 0], qs) + bc_ref[0, pl.ds(pl.multiple_of(i * ncmp, ncmp), ncmp), :]
    sees_any = t0 + (lax.broadcasted_iota(i32, (1, rows), 1) & (t - 1)) >= CMP_BLOCK - 1
    s = s - jnp.max(s, axis=0, keepdims=True)
    e = jnp.where(sees_any, jnp.exp2(s), 0.0)
    p_c = e * (1.0 / jnp.maximum(jnp.sum(e, axis=0, keepdims=True), 1e-30))
    o_c = lax.dot_general(p_c.astype(bf16), vc_ref[0, 0, 0], (((0,), (0,)), ((), ())),
                          preferred_element_type=f32)

    psum = p_c[:, 0:t]
    for g in range(1, NSA_GROUP):
        psum = psum + p_c[:, g * t:(g + 1) * t]
    hi = psum.astype(bf16)
    lo = (psum - hi.astype(f32)).astype(bf16)
    imp = (_dot(ov_ref[...], hi) + _dot(ov_ref[...], lo))[0:nsel]
    blk = lax.broadcasted_iota(i32, (nsel, t), 0)
    cur = (t0 + lax.broadcasted_iota(i32, (nsel, t), 1)) >> 6
    forced = (blk == 0) | ((blk <= cur) & (blk > cur - N_LOCAL_SEL))
    val = jnp.where(forced, FORCED_SCORE, jnp.where(blk <= cur, imp, -1.0))
    cnt = jnp.zeros((nsel, t), f32)
    for j in range(nsel):
        vj = val[j:j + 1, :]
        beats = (vj > val) | ((vj == val) & (blk > j))
        cnt = cnt + jnp.where(beats, 1.0, 0.0)
    mneg = jnp.where(cnt < topn, 0.0, NEG_INF)
    mneg = jnp.concatenate([mneg, jnp.zeros((LANES - nsel, t), f32)], axis=0).T.astype(bf16)
    for g in range(NSA_GROUP):
        qaug_ref[g * t:(g + 1) * t, LANES:2 * LANES] = mneg

    _flash_init(m_ref, acc_ref)

    def sel_logits(kt):
        k = kaug_ref[pl.ds(kt * t, t), :]
        return _dot_nt(qaug_ref[...], k) + tb_ref[0, jnp.minimum(i - kt, 2)]

    def values(v_ref):
        return lambda kt: (v_ref[0, pl.ds(kt * t, t), :], v_ref[1, pl.ds(kt * t, t), :])

    _flash_loop(0, i, sel_logits, values(vsel_ref), s_ref, m_ref, acc_ref)
    os_ref[...] = _flash_finish(acc_ref)

    _flash_init(m_ref, acc_ref)
    nwin = WINDOW // t

    def win_logits(kt):
        d = i - kt
        kind = jnp.where(d == nwin, 3, d)
        return _dot_nt(qs, kw_ref[0, 0, pl.ds(kt * t, t), :]) + tb_ref[0, kind]

    _flash_loop(jnp.maximum(i - nwin, 0), i, win_logits, values(vwin_ref), s_ref, m_ref, acc_ref)
    o_w = _flash_finish(acc_ref)
    o_s = os_ref[...]

    gates = gate_ref[0, 0, tile_rows, :]
    outs = []
    for g in range(NSA_GROUP):
        rb = NSA_ROW_ORDER.index(g)
        sl = slice(rb * t, (rb + 1) * t)
        outs.append(gates[:, g:g + 1] * o_c[sl]
                    + gates[:, NSA_GROUP + g:NSA_GROUP + g + 1] * o_s[sl]
                    + gates[:, 2 * NSA_GROUP + g:2 * NSA_GROUP + g + 1] * o_w[sl])
    for j in range(NSA_GROUP // 2):
        o_ref[0, tile_rows, j * LANES:(j + 1) * LANES] = jnp.where(
            lo_half, outs[2 * j], outs[2 * j + 1]).astype(o_ref.dtype)


def _nsa(qkv, cmpkv, gates, bias_c, tbias, ov, et):
    b, _, s, _ = qkv.shape
    ncmp = cmpkv.shape[3]
    t = T_ATT
    rows = NSA_GROUP * t
    qw = NSA_GROUP * HEAD_DIM
    qtiles = qw // LANES
    kv_spec = lambda col: pl.BlockSpec((1, 1, s, LANES), lambda h, bi, col=col: (bi, col + h, 0, 0))
    cmp_spec = lambda kv: pl.BlockSpec((1, 1, 1, ncmp, LANES), lambda h, bi, kv=kv: (bi, kv, h, 0, 0))
    base = QKV_TILES_KV
    return pl.pallas_call(
        functools.partial(_nsa_kernel, seq=s, ncmp=ncmp),
        grid=(NSA_KV_HEADS, b),
        in_specs=[pl.BlockSpec((1, qtiles, s, LANES), lambda h, bi: (bi, h, 0, 0)),
                  kv_spec(base), kv_spec(base + 2), kv_spec(base + 4), kv_spec(base + 6),
                  cmp_spec(0), cmp_spec(1),
                  pl.BlockSpec((1, 1, s, LANES), lambda h, bi: (bi, h, 0, 0)),
                  pl.BlockSpec((1, (s // t) * ncmp, rows), lambda h, bi: (h, 0, 0)),
                  pl.BlockSpec((1, 4, rows, t), lambda h, bi: (h, 0, 0, 0)),
                  pl.BlockSpec((LANES, ncmp), lambda h, bi: (0, 0)),
                  pl.BlockSpec((s, LANES), lambda h, bi: (0, 0))],
        out_specs=pl.BlockSpec((1, s, qw), lambda h, bi: (bi, 0, h)),
        out_shape=jax.ShapeDtypeStruct((b, s, NSA_Q_W), bf16),
        scratch_shapes=[pltpu.VMEM((rows, 2 * LANES), bf16),
                        pltpu.VMEM((s, 2 * LANES), bf16),
                        pltpu.VMEM((rows, LANES), f32),
                        pltpu.VMEM((rows, LANES), f32),
                        pltpu.VMEM((rows, LANES), f32),
                        pltpu.VMEM((2, rows, t), f32),
                        pltpu.VMEM((2, s, LANES), bf16),
                        pltpu.VMEM((2, s, LANES), bf16)],
        compiler_params=_cparams(("parallel", "parallel")),
        name="nsa",
    )(qkv, qkv, qkv, qkv, qkv, cmpkv, cmpkv, gates, bias_c, tbias, ov, et)


def _fox_kernel(q_ref, k_ref, v_ref, qx_ref, kx_ref, tri_ref, o_ref, qaug_ref, kaug_ref, m_ref, acc_ref, s_ref,
                vaug_ref):
    t = TK_FOX
    tq = TQ_FOX
    seq = k_ref.shape[2]
    lane = lax.broadcasted_iota(i32, (tq, LANES), 1)
    lo_half = lane < HEAD_DIM
    kaug_ref[:, 0:LANES] = k_ref[0, 0]
    kaug_ref[:, LANES:2 * LANES] = kx_ref[0, 0]
    vaug_ref[0], vaug_ref[1] = _ones_values(v_ref[0, 0])

    def logits(kt):
        return _dot_nt(qaug_ref[...], kaug_ref[pl.ds(kt * t, t), :])

    def values(kt):
        return vaug_ref[0, pl.ds(kt * t, t), :], vaug_ref[1, pl.ds(kt * t, t), :]

    def query_tile(i, carry):
        rows = pl.ds(pl.multiple_of(i * tq, tq), tq)
        q = q_ref[0, 0, rows, :].astype(f32)
        qx = qx_ref[0, 0, rows, :].astype(f32)
        qaug_ref[0:tq, 0:LANES] = jnp.where(lo_half, q, 0.0).astype(bf16)
        qaug_ref[tq:2 * tq, 0:LANES] = jnp.where(lo_half, 0.0, q).astype(bf16)
        qaug_ref[0:tq, LANES:2 * LANES] = jnp.where(lane < XCOLS, qx, 0.0).astype(bf16)
        qaug_ref[tq:2 * tq, LANES:2 * LANES] = jnp.where((lane >= XCOLS) & (lane < 2 * XCOLS), qx, 0.0).astype(bf16)

        _flash_init(m_ref, acc_ref)
        _flash_loop(0, i, logits, values, s_ref, m_ref, acc_ref, last_fix=lambda s: s + tri_ref[...])
        o = _flash_finish(acc_ref)
        o_ref[0, rows, :] = jnp.where(lo_half, o[0:tq], o[tq:2 * tq]).astype(o_ref.dtype)
        return carry

    lax.fori_loop(0, seq // tq, query_tile, 0)


def _fox(qkv, qx, kx):
    b, _, s, _ = qkv.shape
    t = TK_FOX
    assert t == TQ_FOX
    r = np.arange(2 * TQ_FOX)[:, None] % TQ_FOX
    tri = jnp.asarray(np.where(np.arange(t)[None, :] <= r, 0.0, NEG_INF), f32)
    base = QKV_TILES_FOX
    return pl.pallas_call(
        _fox_kernel,
        grid=(b, FOX_PAIRS),
        in_specs=[pl.BlockSpec((1, 1, s, LANES), lambda bi, p: (bi, base + p, 0, 0)),
                  pl.BlockSpec((1, 1, s, LANES), lambda bi, p: (bi, base + FOX_PAIRS + p, 0, 0)),
                  pl.BlockSpec((1, 1, s, LANES), lambda bi, p: (bi, base + 2 * FOX_PAIRS + p, 0, 0)),
                  pl.BlockSpec((1, 1, s, LANES), lambda bi, p: (bi, p, 0, 0)),
                  pl.BlockSpec((1, 1, s, LANES), lambda bi, p: (bi, p, 0, 0)),
                  pl.BlockSpec((2 * TQ_FOX, t), lambda bi, p: (0, 0))],
        out_specs=pl.BlockSpec((1, s, LANES), lambda bi, p: (bi, 0, p)),
        out_shape=jax.ShapeDtypeStruct((b, s, FOX_W), bf16),
        scratch_shapes=[pltpu.VMEM((2 * TQ_FOX, 2 * LANES), bf16),
                        pltpu.VMEM((s, 2 * LANES), bf16),
                        pltpu.VMEM((2 * TQ_FOX, LANES), f32),
                        pltpu.VMEM((2 * TQ_FOX, LANES), f32),
                        pltpu.VMEM((2, 2 * TQ_FOX, t), f32),
                        pltpu.VMEM((2, s, LANES), bf16)],
        compiler_params=_cparams(("parallel", "parallel")),
        name="fox",
    )(qkv, qkv, qkv, qx, kx, tri)


def _merge_kernel(yn_ref, yf_ref, xb_ref, x_ref, wn_ref, wf_ref, wmg_ref, wo_ref, g_ref, b_ref,
                  xo_ref, xbo_ref):
    mg = _dot(xb_ref[...], wmg_ref[...])
    merged = (_sigmoid(mg[:, 0:D_MODEL]) * _dot(yn_ref[...], wn_ref[...])
              + _sigmoid(mg[:, D_MODEL:2 * D_MODEL]) * _dot(yf_ref[...], wf_ref[...]))
    hmix = _dot(merged.astype(bf16), wo_ref[...])
    xn = _layer_norm(DN_ALPHA * x_ref[...] + hmix, g_ref[...], b_ref[...])
    xo_ref[...] = xn
    xbo_ref[...] = xn.astype(bf16)


def _merge(yn, yf, xb, x, wn, wf, wmg, wo, g, bb):
    m = x.shape[0]
    tm = TM_MERGE
    row = lambda w: pl.BlockSpec((tm, w), lambda i: (i, 0))
    full = lambda a: pl.BlockSpec(a.shape, lambda i: (0, 0))
    return pl.pallas_call(
        _merge_kernel,
        grid=(m // tm,),
        in_specs=[row(NSA_Q_W), row(FOX_W), row(D_MODEL), row(D_MODEL),
                  full(wn), full(wf), full(wmg), full(wo), full(g), full(bb)],
        out_specs=[row(D_MODEL), row(D_MODEL)],
        out_shape=[jax.ShapeDtypeStruct((m, D_MODEL), f32), jax.ShapeDtypeStruct((m, D_MODEL), bf16)],
        compiler_params=_cparams(("parallel",)),
        name="merge",
    )(yn, yf, xb, x, wn, wf, wmg, wo, g, bb)


def _ffn_kernel(xb_ref, x_ref, wg_ref, wu_ref, wd_ref, g_ref, b_ref, xo_ref, xbo_ref, acc_ref, *, nf):
    f = pl.program_id(1)

    @pl.when(f == 0)
    def _():
        acc_ref[...] = jnp.zeros_like(acc_ref)

    xb = xb_ref[...]
    gate = _dot(xb, wg_ref[...])
    up = _dot(xb, wu_ref[...])
    act = (gate * _sigmoid(gate) * up).astype(bf16)
    acc_ref[...] += _dot(act, wd_ref[...])

    @pl.when(f == nf - 1)
    def _():
        xn = _layer_norm(DN_ALPHA * x_ref[...] + acc_ref[...], g_ref[...], b_ref[...])
        xo_ref[...] = xn
        xbo_ref[...] = xn.astype(bf16)


def _ffn(xb, x, wg, wu, wd, g, bb):
    m = x.shape[0]
    dff = wg.shape[1]
    tm, tf = TM_FFN, TF_FFN
    nf = dff // tf
    row = pl.BlockSpec((tm, D_MODEL), lambda i, f: (i, 0))
    vec = pl.BlockSpec((1, D_MODEL), lambda i, f: (0, 0))
    return pl.pallas_call(
        functools.partial(_ffn_kernel, nf=nf),
        grid=(m // tm, nf),
        in_specs=[row, row,
                  pl.BlockSpec((D_MODEL, tf), lambda i, f: (0, f)),
                  pl.BlockSpec((D_MODEL, tf), lambda i, f: (0, f)),
                  pl.BlockSpec((tf, D_MODEL), lambda i, f: (f, 0)),
                  vec, vec],
        out_specs=[row, row],
        out_shape=[jax.ShapeDtypeStruct((m, D_MODEL), f32), jax.ShapeDtypeStruct((m, D_MODEL), bf16)],
        scratch_shapes=[pltpu.VMEM((tm, D_MODEL), f32)],
        compiler_params=_cparams(("parallel", "arbitrary")),
        name="ffn",
    )(xb, x, wg, wu, wd, g, bb)


def _router_kernel(x_ref, r_ref, tri_ref, gate_ref, rank_ref, *, seq):
    tb = tri_ref.shape[0]
    x = x_ref[0]
    xh = x.astype(bf16)
    xl = (x - xh.astype(f32)).astype(bf16)
    r = r_ref[...]
    rh = r.astype(bf16)
    rl = (r - rh.astype(f32)).astype(bf16)
    logits = _dot(xh, rh) + _dot(xh, rl) + _dot(xl, rh)
    lane = lax.broadcasted_iota(i32, (seq, LANES), 1).astype(f32)
    low = -3.0e38
    lg = jnp.where(lane < N_EXPERTS, logits, low)
    m1 = jnp.max(lg, axis=-1, keepdims=True)
    i1 = jnp.min(jnp.where(lg == m1, lane, float(LANES)), axis=-1, keepdims=True)
    lg2 = jnp.where(lane == i1, low, lg)
    m2 = jnp.max(lg2, axis=-1, keepdims=True)
    i2 = jnp.min(jnp.where(lg2 == m2, lane, float(LANES)), axis=-1, keepdims=True)
    e2 = jnp.exp(m2 - m1)
    den = 1.0 + e2
    gate_ref[0] = jnp.where(lane == i1, 1.0 / den, jnp.where(lane == i2, e2 / den, 0.0))
    sel = (lane == i1) | (lane == i2)
    selb = jnp.where(sel, 1.0, 0.0).astype(bf16)
    carry = jnp.zeros((1, LANES), f32)
    for blk in range(seq // tb):
        sl = slice(blk * tb, (blk + 1) * tb)
        c = _dot(tri_ref[...], selb[sl]) + carry
        carry = c[tb - 1:tb, :]
        rank_ref[0, sl, :] = jnp.where(sel[sl], c - 1.0, -1.0)


def _router(x3, router_pad, tri):
    b, s, _ = x3.shape
    return pl.pallas_call(
        functools.partial(_router_kernel, seq=s),
        grid=(b,),
        in_specs=[pl.BlockSpec((1, s, D_MODEL), lambda i: (i, 0, 0)),
                  pl.BlockSpec((D_MODEL, LANES), lambda i: (0, 0)),
                  pl.BlockSpec(tri.shape, lambda i: (0, 0))],
        out_specs=[pl.BlockSpec((1, s, LANES), lambda i: (i, 0, 0)),
                   pl.BlockSpec((1, s, LANES), lambda i: (i, 0, 0))],
        out_shape=[jax.ShapeDtypeStruct((b, s, LANES), f32),
                   jax.ShapeDtypeStruct((b, s, LANES), f32)],
        compiler_params=_cparams(("parallel",)),
        name="router",
    )(x3, router_pad, tri)


def _moe_kernel(tot_ref, tot_al_ref, off_ref, npc_ref, xb_ref, rankt_ref, gatet_ref, wg_ref, wu_ref, wd_ref,
                y_ref, xg_ref, acc_ref, *, seq, nf):
    tr = TR_MOE
    half = tr // 2
    nsubc = seq // SUB_MOE
    c = pl.program_id(0)
    e = pl.program_id(1)
    f = pl.program_id(2)
    ce = c * N_EXPERTS + e
    tot = tot_ref[ce]
    rem = tot % tr
    nfull = tot // tr + jnp.where(rem > half, 1, 0)
    has_tail = (rem > 0) & (rem <= half)
    tail0 = pl.multiple_of(nfull * tr, half)

    def pieces(fn):
        for q in range(nsubc):
            base = off_ref[ce * nsubc + q]

            def body(p, carry, q=q, base=base):
                fn(q, pl.multiple_of(base + p * PIECE_MOE, ROW_ALIGN))
                return carry
            lax.fori_loop(0, npc_ref[ce * nsubc + q], body, 0)

    def onehot_rows(q, r0):
        rk = rankt_ref[0, pl.ds(e, 1), q * SUB_MOE:(q + 1) * SUB_MOE]
        want = (r0 + lax.broadcasted_iota(i32, (PIECE_MOE, SUB_MOE), 0)).astype(f32)
        return rk == want

    def gather(q, r0):
        p = jnp.where(onehot_rows(q, r0), 1.0, 0.0).astype(bf16)
        xg_ref[pl.ds(r0, PIECE_MOE), :] += _dot(p, xb_ref[0, q * SUB_MOE:(q + 1) * SUB_MOE, :]).astype(bf16)

    def clear(r0, nr):
        xg_ref[pl.ds(r0, nr), :] = jnp.zeros((nr, D_MODEL), bf16)
        acc_ref[pl.ds(r0, nr), :] = jnp.zeros((nr, D_MODEL), f32)

    def hidden(r0, nr):
        xs = xg_ref[pl.ds(r0, nr), :]
        gate = _dot(xs, wg_ref[0, 0])
        up = _dot(xs, wu_ref[0, 0])
        act = (gate * _sigmoid(gate) * up).astype(bf16)
        acc_ref[pl.ds(r0, nr), :] += _dot(act, wd_ref[0, 0])

    def combine(q, r0):
        grow = gatet_ref[0, pl.ds(e, 1), q * SUB_MOE:(q + 1) * SUB_MOE]
        hit = onehot_rows(q, r0)
        w = jnp.sum(jnp.where(hit, grow, 0.0), axis=-1, keepdims=True)
        z = (acc_ref[pl.ds(r0, PIECE_MOE), :] * w).astype(bf16)
        p = jnp.where(hit, 1.0, 0.0).astype(bf16)
        y_ref[0, q * SUB_MOE:(q + 1) * SUB_MOE, :] += lax.dot_general(
            p, z, (((0,), (0,)), ((), ())), preferred_element_type=f32)

    def tiles(fn):
        def body(s, carry):
            fn(pl.multiple_of(s * tr, tr), tr)
            return carry
        lax.fori_loop(0, nfull, body, 0)

        @pl.when(has_tail)
        def _():
            fn(tail0, half)

    def zero(j, carry):
        y_ref[0, pl.ds(j * tr, tr), :] = jnp.zeros((tr, D_MODEL), f32)
        return carry

    lax.fori_loop(0, jnp.where((e == 0) & (f == 0), seq // tr, 0), zero, 0)

    @pl.when(f == 0)
    def _():
        tiles(clear)
        clear(pl.multiple_of(tot_al_ref[ce], ROW_ALIGN), tr)
        pieces(gather)

    tiles(hidden)

    @pl.when(f == nf - 1)
    def _():
        pieces(combine)


def _moe(xb3, gate, rank, wg, wu, wd, layer):
    b, s, _ = xb3.shape
    dff = wg.shape[3]
    nf = dff // TF_MOE
    nsubc = s // SUB_MOE
    rk = rank[:, :, :N_EXPERTS]
    cnt = jnp.sum((rk >= 0.0).reshape(b, nsubc, SUB_MOE, N_EXPERTS), axis=2).astype(i32)
    first = jnp.cumsum(cnt, axis=1) - cnt
    off = first // ROW_ALIGN * ROW_ALIGN
    npc = jnp.where(cnt > 0, (first - off + cnt + PIECE_MOE - 1) // PIECE_MOE, 0)
    post = rk.transpose(0, 2, 1)
    gatet = gate[:, :, :N_EXPERTS].transpose(0, 2, 1)
    tot = jnp.sum(cnt, axis=1).reshape(-1)
    tot_al = (tot + ROW_ALIGN - 1) // ROW_ALIGN * ROW_ALIGN
    off_flat = off.transpose(0, 2, 1).reshape(-1)
    npc = npc.transpose(0, 2, 1).reshape(-1)
    buf_rows = s + ROW_ALIGN + PIECE_MOE + TR_MOE
    grid_spec = pltpu.PrefetchScalarGridSpec(
        num_scalar_prefetch=4,
        grid=(b, N_EXPERTS, nf),
        in_specs=[pl.BlockSpec((1, s, D_MODEL), lambda c, e, f, *_: (c, 0, 0)),
                  pl.BlockSpec((1, N_EXPERTS, s), lambda c, e, f, *_: (c, 0, 0)),
                  pl.BlockSpec((1, N_EXPERTS, s), lambda c, e, f, *_: (c, 0, 0)),
                  pl.BlockSpec((1, 1, D_MODEL, TF_MOE), lambda c, e, f, *_: (layer, e, 0, f)),
                  pl.BlockSpec((1, 1, D_MODEL, TF_MOE), lambda c, e, f, *_: (layer, e, 0, f)),
                  pl.BlockSpec((1, 1, TF_MOE, D_MODEL), lambda c, e, f, *_: (layer, e, f, 0))],
        out_specs=pl.BlockSpec((1, s, D_MODEL), lambda c, e, f, *_: (c, 0, 0)),
        scratch_shapes=[pltpu.VMEM((buf_rows, D_MODEL), bf16), pltpu.VMEM((buf_rows, D_MODEL), f32)],
    )
    return pl.pallas_call(
        functools.partial(_moe_kernel, seq=s, nf=nf),
        grid_spec=grid_spec,
        out_shape=jax.ShapeDtypeStruct((b, s, D_MODEL), f32),
        compiler_params=_cparams(("parallel", "arbitrary", "arbitrary")),
        name="moe",
    )(tot, tot_al, off_flat, npc, xb3, post, gatet, wg, wu, wd)


def _resln_kernel(x_ref, y_ref, g_ref, b_ref, xo_ref, xbo_ref):
    xn = _layer_norm(DN_ALPHA * x_ref[...] + y_ref[...], g_ref[...], b_ref[...])
    xo_ref[...] = xn
    xbo_ref[...] = xn.astype(bf16)


def _resln(x, y, g, bb):
    m = x.shape[0]
    tm = TM_FFN
    row = pl.BlockSpec((tm, D_MODEL), lambda i: (i, 0))
    vec = pl.BlockSpec((1, D_MODEL), lambda i: (0, 0))
    return pl.pallas_call(
        _resln_kernel,
        grid=(m // tm,),
        in_specs=[row, row, vec, vec],
        out_specs=[row, row],
        out_shape=[jax.ShapeDtypeStruct((m, D_MODEL), f32), jax.ShapeDtypeStruct((m, D_MODEL), bf16)],
        compiler_params=_cparams(("parallel",)),
        name="resln",
    )(x, y, g, bb)


def _t5_bucket(dist):
    n = jnp.maximum(dist, 0)
    max_exact = REL_BUCKETS // 2
    large = max_exact + (jnp.log(jnp.maximum(n, 1).astype(f32) / max_exact)
                         / math.log(REL_MAX_DIST / max_exact) * (REL_BUCKETS - max_exact)).astype(i32)
    large = jnp.minimum(large, REL_BUCKETS - 1)
    return jnp.where(n < max_exact, n, large)


def _bias_of_dist(rel_bias, dist):
    onehot = (_t5_bucket(dist)[None] == jnp.arange(REL_BUCKETS).reshape((-1,) + (1,) * dist.ndim)).astype(f32)
    return LOG2E * jnp.einsum("kh,k...->h...", rel_bias.astype(f32), onehot, precision=lax.Precision.HIGHEST)


def _bias_tables(rel_bias, seq):
    t = T_ATT
    rows = NSA_GROUP * t
    ncmp = seq // CMP_STRIDE
    d0 = jnp.arange(t)[:, None] - jnp.arange(t)[None, :]
    offs = jnp.array([0, t, 2 * t, WINDOW]).reshape(4, 1, 1)
    kinds = _bias_of_dist(rel_bias, offs + d0[None])
    mask = jnp.stack([d0 >= 0, d0 == d0, d0 == d0, d0 < 0])
    kinds = jnp.where(mask[None], kinds, NEG_INF)
    order = np.array(NSA_ROW_ORDER)
    tbias = kinds.reshape(NSA_KV_HEADS, NSA_GROUP, 4, t, t)[:, order].transpose(0, 2, 1, 3, 4)
    tbias = tbias.reshape(NSA_KV_HEADS, 4, rows, t)
    cend = jnp.arange(ncmp) * CMP_STRIDE + CMP_BLOCK - 1
    dist_c = jnp.arange(seq)[:, None] - cend[None, :]
    bc = jnp.where(dist_c >= 0, _bias_of_dist(rel_bias, dist_c), NEG_INF)
    bc = bc.reshape(NSA_KV_HEADS, NSA_GROUP, seq // t, t, ncmp)[:, order].transpose(0, 2, 4, 1, 3)
    return tbias, bc.reshape(NSA_KV_HEADS, (seq // t) * ncmp, rows)


def _selection_constants(seq):
    ncmp = seq // CMP_STRIDE
    nsel = seq // SEL_BLOCK
    c0 = np.arange(ncmp)[:, None] * CMP_STRIDE
    s0 = np.arange(LANES)[None, :] * SEL_BLOCK
    ov = np.maximum(np.minimum(c0 + CMP_BLOCK, s0 + SEL_BLOCK) - np.maximum(c0, s0), 0) / CMP_BLOCK
    ov[ncmp - 1, :] = 0.0
    ov[:, nsel:] = 0.0
    et = (np.arange(seq)[:, None] // SEL_BLOCK == np.arange(LANES)[None, :]).astype(np.float32)
    return jnp.asarray(ov.T, bf16), jnp.asarray(et, bf16)


def _fox_placement():
    xw = FOX_PAIRS * LANES
    pq = np.zeros((3, LANES, xw), np.float32)
    pk = np.zeros((3, LANES, xw), np.float32)
    oq = np.zeros((1, xw), np.float32)
    ok = np.zeros((1, xw), np.float32)
    for p in range(FOX_PAIRS):
        for hh in range(2):
            src = FGATE_LANE + 2 * p + hh
            base = p * LANES + hh * XCOLS
            for part in range(3):
                pk[part, src, base + part] = -1.0
                pq[part, src, base + 3 + part] = 1.0
                oq[0, base + part] = 1.0
                ok[0, base + 3 + part] = 1.0
    return jnp.asarray(pq, bf16), jnp.asarray(pk, bf16), jnp.asarray(oq), jnp.asarray(ok)


def _layer_weights(w_in, layer_pe, w1, w2, f_bias):
    offs = np.cumsum((NSA_Q_W, 6 * 2 * HEAD_DIM, 3 * NSA_HEADS, 3 * FOX_W, FOX_HEADS, 2 * D_MODEL))
    kv0, g0, fx0, ff0, mg0 = offs[0], offs[1], offs[2], offs[3], offs[4]
    scale = HEAD_DIM ** -0.5 * LOG2E
    kvw = NSA_KV_HEADS * HEAD_DIM
    w_kv = w_in[:, kv0:g0]
    w_kvdup = jnp.repeat(w_kv[:, 2 * kvw:].reshape(D_MODEL, 4 * NSA_KV_HEADS, 1, HEAD_DIM), 2, axis=2)
    w_kvdup = w_kvdup.reshape(D_MODEL, 8 * kvw)
    w_qkv = jnp.concatenate([w_in[:, :NSA_Q_W] * scale, w_kvdup, w_in[:, fx0:fx0 + FOX_W] * scale,
                             w_in[:, fx0 + FOX_W:ff0]], axis=1).astype(bf16)
    zeros = lambda n: jnp.zeros((D_MODEL, n), w_in.dtype)
    ng = 3 * NSA_GROUP
    w_g = w_in[:, g0:fx0].reshape(D_MODEL, NSA_KV_HEADS, NSA_GROUP, 3).transpose(0, 1, 3, 2)
    gate_cols = lambda h: w_g[:, h].reshape(D_MODEL, ng)
    w_aux = jnp.concatenate([w_kv[:, :2 * kvw],
                             gate_cols(0), zeros(FGATE_LANE - ng), w_in[:, ff0:mg0],
                             zeros(LANES - FGATE_LANE - FOX_HEADS),
                             gate_cols(1), zeros(LANES - ng)], axis=1).astype(bf16)
    fb_row = jnp.zeros((1, LANES), f32).at[0, FGATE_LANE:FGATE_LANE + FOX_HEADS].set(f_bias.astype(f32))
    pe2 = jnp.tile(layer_pe.astype(f32), (1, 1, NSA_KV_HEADS))
    w1r = w1.reshape(2, CMP_BLOCK, HEAD_DIM, CMP_HIDDEN).astype(bf16)
    zero = jnp.zeros_like(w1r)
    w1bd = jnp.concatenate([jnp.concatenate([w1r, zero], axis=-1),
                            jnp.concatenate([zero, w1r], axis=-1)], axis=-2)
    w2d = jnp.concatenate([w2, w2], axis=-1).astype(bf16)
    return (w_qkv, w_aux, w_in[:, mg0:].astype(bf16), fb_row, pe2, w1bd[:, :CMP_STRIDE], w1bd[:, CMP_STRIDE:],
            w2d)


def kernel(x, w_in, nsa_cmp_pe, nsa_cmp_w1, nsa_cmp_w2, fox_f_bias, w_nsa_branch, w_fox_branch, w_out,
           rel_bias, ln1_g, ln1_b, ln2_g, ln2_b, dense_w_gate, dense_w_up, dense_w_down, moe_router,
           moe_w_gate, moe_w_up, moe_w_down):
    b, s, d = x.shape
    m = b * s
    tbias, bias_c = _bias_tables(rel_bias, s)
    ov, et = _selection_constants(s)
    pq, pk, oq, ok = _fox_placement()
    tri128 = jnp.asarray(np.tril(np.ones((LANES, LANES), np.float32)), bf16)
    tri256 = jnp.asarray(np.tril(np.ones((256, 256), np.float32)), bf16)

    moe_wg, moe_wu, moe_wd = moe_w_gate.astype(bf16), moe_w_up.astype(bf16), moe_w_down.astype(bf16)
    xf = x.reshape(m, d).astype(f32)
    xb = xf.astype(bf16)
    for layer in range(DEPTH):
        w_qkv, w_aux, w_mg, fb_row, pe2, w1t, w1b, w2d = _layer_weights(
            w_in[layer], nsa_cmp_pe[layer], nsa_cmp_w1[layer], nsa_cmp_w2[layer], fox_f_bias[layer])
        qkv, aux = _proj(xb, w_qkv, w_aux, b)
        aux = aux.reshape(b, s, AUX_W)
        gates, qx, kx = _gateprep(aux, fb_row, tri128, pq, pk, oq, ok)
        cmpkv = _compress(aux, pe2, w1t, w1b, w2d)
        y_nsa = _nsa(qkv, cmpkv, gates, bias_c, tbias, ov, et).reshape(m, NSA_Q_W)
        y_fox = _fox(qkv, qx, kx).reshape(m, FOX_W)
        xf, xb = _merge(y_nsa, y_fox, xb, xf, w_nsa_branch[layer].astype(bf16),
                        w_fox_branch[layer].astype(bf16), w_mg, w_out[layer].astype(bf16),
                        ln1_g[layer].reshape(1, d), ln1_b[layer].reshape(1, d))
        j = layer // 2
        g2, b2 = ln2_g[layer].reshape(1, d), ln2_b[layer].reshape(1, d)
        if layer % 2 == 0:
            xf, xb = _ffn(xb, xf, dense_w_gate[j].astype(bf16), dense_w_up[j].astype(bf16),
                          dense_w_down[j].astype(bf16), g2, b2)
        else:
            router_pad = jnp.zeros((d, LANES), f32).at[:, :N_EXPERTS].set(moe_router[j].astype(f32))
            gate, rank = _router(xf.reshape(b, s, d), router_pad, tri256)
            y = _moe(xb.reshape(b, s, d), gate, rank, moe_wg, moe_wu, moe_wd, j)
            xf, xb = _resln(xf, y.reshape(m, d), g2, b2)
    return xf.reshape(b, s, d).astype(x.dtype)
```

```python
import functools
import math

import numpy as np
import jax
import jax.numpy as jnp
from jax import lax
from jax.experimental import pallas as pl
from jax.experimental.pallas import tpu as pltpu

f32 = jnp.float32
bf16 = jnp.bfloat16
i32 = jnp.int32

D_MODEL = 1024
HEAD_DIM = 64
LANES = 128
NSA_HEADS = 8
NSA_KV_HEADS = 2
NSA_GROUP = NSA_HEADS // NSA_KV_HEADS
NSA_ROW_ORDER = (0, 2, 1, 3)
FOX_HEADS = 8
FOX_PAIRS = FOX_HEADS // 2
CMP_BLOCK = 32
CMP_STRIDE = 16
CMP_HIDDEN = 128
SEL_BLOCK = 64
N_SELECT = 16
N_LOCAL_SEL = 2
WINDOW = 512
REL_BUCKETS = 32
REL_MAX_DIST = 128
N_EXPERTS = 8
DEPTH = 4
DN_ALPHA = (2 * DEPTH) ** 0.25
LN_EPS = 1e-5
FORCED_SCORE = 1e4
NEG_INF = -1e30
LOG2E = math.log2(math.e)

NSA_Q_W = NSA_HEADS * HEAD_DIM
FOX_W = FOX_HEADS * HEAD_DIM
QKV_TILES_KV = NSA_Q_W // LANES
QKV_TILES_FOX = QKV_TILES_KV + 4 * NSA_KV_HEADS
QKV_W = (QKV_TILES_FOX + 3 * FOX_PAIRS) * LANES
AUX_W = 4 * LANES
FGATE_LANE = 24
XCOLS = 6

T_ATT = 256
TQ_FOX = 512
TK_FOX = 512
TM_PROJ = 512
TM_MERGE = 512
TM_FFN = 512
TF_FFN = 2816
TR_MOE = 256
SUB_MOE = 512
PIECE_MOE = 192
ROW_ALIGN = 16
TF_MOE = 896
VMEM_LIMIT = 56 * 1024 * 1024


def _cparams(sem):
    return pltpu.CompilerParams(dimension_semantics=sem, vmem_limit_bytes=VMEM_LIMIT)


def _dot(a, b):
    return jnp.dot(a, b, preferred_element_type=f32)


def _dot_nt(a, b):
    return lax.dot_general(a, b, (((1,), (1,)), ((), ())), preferred_element_type=f32)


def _sigmoid(x):
    return 1.0 / (1.0 + jnp.exp(-x))


def _layer_norm(z, g, b):
    mu = jnp.mean(z, axis=-1, keepdims=True)
    zc = z - mu
    var = jnp.mean(zc * zc, axis=-1, keepdims=True)
    return zc * lax.rsqrt(var + LN_EPS) * g + b


def _split3(x):
    hi = x.astype(bf16)
    r1 = x - hi.astype(f32)
    mid = r1.astype(bf16)
    lo = (r1 - mid.astype(f32)).astype(bf16)
    return hi, mid, lo


def _proj_kernel(x_ref, w_ref, wa_ref, o_ref, oa_ref):
    x = x_ref[...]
    res = _dot(x, w_ref[...])
    for j in range(o_ref.shape[1]):
        o_ref[0, j] = res[:, j * LANES:(j + 1) * LANES].astype(o_ref.dtype)
    oa_ref[...] = _dot(x, wa_ref[...])


def _proj(xb, w, wa, batch):
    m, k = xb.shape
    n, na = w.shape[1], wa.shape[1]
    nb = m // batch // TM_PROJ
    return pl.pallas_call(
        _proj_kernel,
        grid=(m // TM_PROJ,),
        in_specs=[pl.BlockSpec((TM_PROJ, k), lambda i: (i, 0)),
                  pl.BlockSpec((k, n), lambda i: (0, 0)),
                  pl.BlockSpec((k, na), lambda i: (0, 0))],
        out_specs=[pl.BlockSpec((1, n // LANES, TM_PROJ, LANES), lambda i: (i // nb, 0, i % nb, 0)),
                   pl.BlockSpec((TM_PROJ, na), lambda i: (i, 0))],
        out_shape=[jax.ShapeDtypeStruct((batch, n // LANES, m // batch, LANES), bf16),
                   jax.ShapeDtypeStruct((m, na), f32)],
        compiler_params=_cparams(("parallel",)),
        name="proj",
    )(xb, w, wa)


def _gateprep_kernel(a_ref, fb_ref, tri_ref, pq_ref, pk_ref, oq_ref, ok_ref, g_ref, qx_ref, kx_ref, *, seq):
    tb = LANES
    tri = tri_ref[...]
    local = []
    for blk in range(seq // tb):
        sl = slice(blk * tb, (blk + 1) * tb)
        va = a_ref[0, sl, 0:LANES]
        g_ref[0, 0, sl, :] = _sigmoid(va)
        g_ref[0, 1, sl, :] = _sigmoid(a_ref[0, sl, LANES:2 * LANES])
        z = va + fb_ref[...]
        logf = jnp.minimum(z, 0.0) - jnp.log1p(jnp.exp(-jnp.abs(z)))
        hi, mid, lo = _split3(logf)
        local.append(_dot(tri, hi) + _dot(tri, mid) + _dot(tri, lo))
    carry = jnp.zeros((1, LANES), f32)
    for blk in range(seq // tb):
        sl = slice(blk * tb, (blk + 1) * tb)
        c = local[blk] + carry
        carry = c[tb - 1:tb, :]
        chi, cmid, clo = _split3(c * LOG2E)
        qx = _dot(chi, pq_ref[0]) + _dot(cmid, pq_ref[1]) + _dot(clo, pq_ref[2]) + oq_ref[...]
        kx = _dot(chi, pk_ref[0]) + _dot(cmid, pk_ref[1]) + _dot(clo, pk_ref[2]) + ok_ref[...]
        for p in range(FOX_PAIRS):
            qx_ref[0, p, sl, :] = qx[:, p * LANES:(p + 1) * LANES].astype(bf16)
            kx_ref[0, p, sl, :] = kx[:, p * LANES:(p + 1) * LANES].astype(bf16)


def _gateprep(aux, fb_row, tri, pq, pk, oq, ok):
    b, s, _ = aux.shape
    const2 = lambda a: pl.BlockSpec(a.shape, lambda i: (0, 0))
    const3 = lambda a: pl.BlockSpec(a.shape, lambda i: (0, 0, 0))
    return pl.pallas_call(
        functools.partial(_gateprep_kernel, seq=s),
        grid=(b,),
        in_specs=[pl.BlockSpec((1, s, 2 * LANES), lambda i: (i, 0, 1)),
                  const2(fb_row), const2(tri), const3(pq), const3(pk), const2(oq), const2(ok)],
        out_specs=[pl.BlockSpec((1, 2, s, LANES), lambda i: (i, 0, 0, 0)),
                   pl.BlockSpec((1, FOX_PAIRS, s, LANES), lambda i: (i, 0, 0, 0)),
                   pl.BlockSpec((1, FOX_PAIRS, s, LANES), lambda i: (i, 0, 0, 0))],
        out_shape=[jax.ShapeDtypeStruct((b, 2, s, LANES), f32),
                   jax.ShapeDtypeStruct((b, FOX_PAIRS, s, LANES), bf16),
                   jax.ShapeDtypeStruct((b, FOX_PAIRS, s, LANES), bf16)],
        compiler_params=_cparams(("parallel",)),
        name="gateprep",
    )(aux, fb_row, tri, pq, pk, oq, ok)


def _gelu_tanh(x):
    c = math.sqrt(2.0 / math.pi)
    return x * (0.5 * (1.0 + jnp.tanh(c * (x + 0.044715 * (x * x * x)))))


def _compress_kernel(a_ref, pe_ref, w1t_ref, w1b_ref, w2_ref, o_ref, *, nhalf):
    top = jnp.zeros((nhalf, NSA_KV_HEADS * CMP_HIDDEN), f32)
    bot = jnp.zeros((nhalf, NSA_KV_HEADS * CMP_HIDDEN), f32)
    for l in range(CMP_STRIDE):
        rows = a_ref[0, pl.ds(l, nhalf, stride=CMP_STRIDE), :]
        top = top + _dot((rows + pe_ref[0, l:l + 1, :]).astype(bf16), w1t_ref[0, l])
        bot = bot + _dot((rows + pe_ref[0, CMP_STRIDE + l:CMP_STRIDE + l + 1, :]).astype(bf16), w1b_ref[0, l])
    pre = top + pltpu.roll(bot, nhalf - 1, 0)
    act = _gelu_tanh(pre).astype(bf16)
    for h in range(NSA_KV_HEADS):
        o_ref[0, 0, h] = _dot(act[:, h * CMP_HIDDEN:(h + 1) * CMP_HIDDEN], w2_ref[0]).astype(o_ref.dtype)


def _compress(aux, pe2, w1t, w1b, w2d):
    b, s, _ = aux.shape
    nhalf = s // CMP_STRIDE
    w1_spec = pl.BlockSpec((1,) + w1t.shape[1:], lambda i, j: (j, 0, 0, 0))
    return pl.pallas_call(
        functools.partial(_compress_kernel, nhalf=nhalf),
        grid=(b, 2),
        in_specs=[pl.BlockSpec((1, s, LANES), lambda i, j: (i, 0, j)),
                  pl.BlockSpec((1, CMP_BLOCK, LANES), lambda i, j: (j, 0, 0)),
                  w1_spec, w1_spec,
                  pl.BlockSpec((1, CMP_HIDDEN, LANES), lambda i, j: (j, 0, 0))],
        out_specs=pl.BlockSpec((1, 1, NSA_KV_HEADS, nhalf, LANES), lambda i, j: (i, j, 0, 0, 0)),
        out_shape=jax.ShapeDtypeStruct((b, 2, NSA_KV_HEADS, nhalf, LANES), bf16),
        compiler_params=_cparams(("parallel", "parallel")),
        name="compress",
    )(aux, pe2, w1t, w1b, w2d)


def _flash_init(m_ref, acc_ref):
    m_ref[...] = jnp.full(m_ref.shape, NEG_INF, f32)
    acc_ref[...] = jnp.zeros(acc_ref.shape, f32)


def _ones_values(v):
    lo_half = lax.broadcasted_iota(i32, v.shape, 1) < HEAD_DIM
    vf = v.astype(f32)
    return jnp.where(lo_half, vf, 1.0).astype(bf16), jnp.where(lo_half, 1.0, vf).astype(bf16)


def _flash_step(s, v_lo, v_hi, m_ref, acc_ref):
    nk = s.shape[1] // LANES
    half = s.shape[0] // 2
    cols = [s[:, c * LANES:(c + 1) * LANES] for c in range(nk)]
    mx = cols[0]
    for c in cols[1:]:
        mx = jnp.maximum(mx, c)
    m_old = m_ref[...]
    m_new = jnp.maximum(m_old, jnp.broadcast_to(jnp.max(mx, axis=-1, keepdims=True), m_old.shape))
    alpha = jnp.exp2(m_old - m_new)
    p = jnp.concatenate([jnp.exp2(c - m_new).astype(bf16) for c in cols], axis=1)
    pv = jnp.concatenate([_dot(p[0:half], v_lo), _dot(p[half:], v_hi)], axis=0)
    acc_ref[...] = alpha * acc_ref[...] + pv
    m_ref[...] = m_new


def _flash_loop(first, last, logits_fn, v_fn, s_ref, m_ref, acc_ref, last_fix=None):
    sa, sb = s_ref.at[0], s_ref.at[1]
    n = last - first + 1
    pairs = (n - 1) // 2

    def step(buf, kt, fix=None):
        s = buf[...] if fix is None else fix(buf[...])
        _flash_step(s, *v_fn(kt), m_ref, acc_ref)

    sa[...] = logits_fn(first)

    def body(j, carry):
        kt = first + 2 * j
        sb[...] = logits_fn(kt + 1)
        step(sa, kt)
        sa[...] = logits_fn(kt + 2)
        step(sb, kt + 1)
        return carry

    lax.fori_loop(0, pairs, body, 0)
    two_left = n - 2 * pairs == 2

    @pl.when(two_left)
    def _():
        sb[...] = logits_fn(last)
        step(sa, last - 1)
        step(sb, last, last_fix)

    @pl.when(jnp.logical_not(two_left))
    def _():
        step(sa, last, last_fix)


def _flash_finish(acc_ref):
    acc = acc_ref[...]
    return acc * (1.0 / jnp.maximum(pltpu.roll(acc, HEAD_DIM, 1), 1e-30))


def _nsa_kernel(q_ref, ks_ref, vs_ref, kw_ref, vw_ref, kc_ref, vc_ref, gate_ref, bc_ref, tb_ref, ov_ref,
                et_ref, eg_ref, o_ref, qaug_ref, kaug_ref, m_ref, acc_ref, os_ref, s_ref, vsel_ref,
                vwin_ref, *, seq, ncmp):
    kaug_ref[:, 0:LANES] = ks_ref[0, 0]
    kaug_ref[:, LANES:2 * LANES] = et_ref[...]
    vsel_ref[0], vsel_ref[1] = _ones_values(vs_ref[0, 0])
    vwin_ref[0], vwin_ref[1] = _ones_values(vw_ref[0, 0])

    def query_tile(i, carry):
        _nsa_tile(i, q_ref, kw_ref, kc_ref, vc_ref, gate_ref, bc_ref, tb_ref, ov_ref, eg_ref, o_ref, qaug_ref,
                  kaug_ref, m_ref, acc_ref, os_ref, s_ref, vsel_ref, vwin_ref, seq=seq, ncmp=ncmp)
        return carry

    lax.fori_loop(0, seq // T_ATT, query_tile, 0)


def _nsa_tile(i, q_ref, kw_ref, kc_ref, vc_ref, gate_ref, bc_ref, tb_ref, ov_ref, eg_ref, o_ref, qaug_ref,
              kaug_ref, m_ref, acc_ref, os_ref, s_ref, vsel_ref, vwin_ref, *, seq, ncmp):
    t = T_ATT
    rows = NSA_GROUP * t
    nsel = seq // SEL_BLOCK
    topn = min(N_SELECT, nsel)
    t0 = i * t
    tile_rows = pl.ds(pl.multiple_of(t0, t), t)
    lane = lax.broadcasted_iota(i32, (t, LANES), 1)
    lo_half = lane < HEAD_DIM

    for rb, g in enumerate(NSA_ROW_ORDER):
        qg = q_ref[0, g // 2, tile_rows, :].astype(f32)
        keep = lo_half if g % 2 == 0 else jnp.logical_not(lo_half)
        qaug_ref[rb * t:(rb + 1) * t, 0:LANES] = jnp.where(keep, qg, 0.0).astype(bf16)
    qs = qaug_ref[:, 0:LANES]

    s = _dot_nt(kc_ref[0, 0, 0], qs) + bc_ref[0, pl.ds(pl.multiple_of(i * ncmp, ncmp), ncmp), :]
    sees_any = t0 + (lax.broadcasted_iota(i32, (1, rows), 1) & (t - 1)) >= CMP_BLOCK - 1
    s = s - jnp.max(s, axis=0, keepdims=True)
    e = jnp.where(sees_any, jnp.exp2(s), 0.0)
    p_c = e * (1.0 / jnp.maximum(jnp.sum(e, axis=0, keepdims=True), 1e-30))
    o_c = lax.dot_general(p_c.astype(bf16), vc_ref[0, 0, 0], (((0,), (0,)), ((), ())),
                          preferred_element_type=f32)

    psum = p_c[:, 0:t]
    for g in range(1, NSA_GROUP):
        psum = psum + p_c[:, g * t:(g + 1) * t]
    hi = psum.astype(bf16)
    lo = (psum - hi.astype(f32)).astype(bf16)
    imp = (_dot(ov_ref[...], hi) + _dot(ov_ref[...], lo))[0:nsel]
    blk = lax.broadcasted_iota(i32, (nsel, t), 0)
    cur = (t0 + lax.broadcasted_iota(i32, (nsel, t), 1)) >> 6
    forced = (blk == 0) | ((blk <= cur) & (blk > cur - N_LOCAL_SEL))
    val = jnp.where(forced, FORCED_SCORE, jnp.where(blk <= cur, imp, -1.0))
    ngrp = nsel // 8
    sub = lax.broadcasted_iota(i32, (8, t), 0)
    vals = [val[8 * r:8 * r + 8, :] for r in range(ngrp)]
    cnts = [jnp.zeros((8, t), f32) for _ in range(ngrp)]
    for j in range(nsel):
        vj = val[j:j + 1, :]
        for r in range(ngrp):
            ahead = jnp.where(vj > vals[r], 1.0, 0.0)
            ahead_or_tied = jnp.where(vj >= vals[r], 1.0, 0.0)
            if r < j // 8:
                cnts[r] = cnts[r] + ahead
            elif r > j // 8:
                cnts[r] = cnts[r] + ahead_or_tied
            else:
                cnts[r] = cnts[r] + jnp.where(sub > j % 8, ahead_or_tied, ahead)
    cnt = jnp.concatenate(cnts, axis=0)
    mneg = jnp.where(cnt < topn, 0.0, NEG_INF)
    mneg = jnp.concatenate([mneg, jnp.zeros((LANES - nsel, t), f32)], axis=0).T.astype(bf16)
    for g in range(NSA_GROUP):
        qaug_ref[g * t:(g + 1) * t, LANES:2 * LANES] = mneg

    _flash_init(m_ref, acc_ref)

    def sel_logits(kt):
        k = kaug_ref[pl.ds(kt * t, t), :]
        return _dot_nt(qaug_ref[...], k) + tb_ref[0, jnp.minimum(i - kt, 2)]

    def values(v_ref):
        return lambda kt: (v_ref[0, pl.ds(kt * t, t), :], v_ref[1, pl.ds(kt * t, t), :])

    _flash_loop(0, i, sel_logits, values(vsel_ref), s_ref, m_ref, acc_ref)
    os_ref[...] = _flash_finish(acc_ref)

    _flash_init(m_ref, acc_ref)
    nwin = WINDOW // t

    def win_logits(kt):
        d = i - kt
        kind = jnp.where(d == nwin, 3, d)
        return _dot_nt(qs, kw_ref[0, 0, pl.ds(kt * t, t), :]) + tb_ref[0, kind]

    _flash_loop(jnp.maximum(i - nwin, 0), i, win_logits, values(vwin_ref), s_ref, m_ref, acc_ref)
    o_w = _flash_finish(acc_ref)
    o_s = os_ref[...]

    ghi, gmid, glo = _split3(gate_ref[0, 0, tile_rows, :])
    gexp = _dot(ghi, eg_ref[...]) + _dot(gmid, eg_ref[...]) + _dot(glo, eg_ref[...])
    branches = (o_c, o_s, o_w)
    for j in range(NSA_GROUP // 2):
        ra, rb = NSA_ROW_ORDER.index(2 * j), NSA_ROW_ORDER.index(2 * j + 1)
        out = jnp.zeros((t, LANES), f32)
        for br, o_br in enumerate(branches):
            o_pair = jnp.where(lo_half, o_br[ra * t:(ra + 1) * t], o_br[rb * t:(rb + 1) * t])
            col = (br * (NSA_GROUP // 2) + j) * LANES
            out = out + gexp[:, col:col + LANES] * o_pair
        o_ref[0, tile_rows, j * LANES:(j + 1) * LANES] = out.astype(o_ref.dtype)


def _nsa(qkv, cmpkv, gates, bias_c, tbias, ov, et, eg):
    b, _, s, _ = qkv.shape
    ncmp = cmpkv.shape[3]
    t = T_ATT
    rows = NSA_GROUP * t
    qw = NSA_GROUP * HEAD_DIM
    qtiles = qw // LANES
    kv_spec = lambda col: pl.BlockSpec((1, 1, s, LANES), lambda h, bi, col=col: (bi, col + h, 0, 0))
    cmp_spec = lambda kv: pl.BlockSpec((1, 1, 1, ncmp, LANES), lambda h, bi, kv=kv: (bi, kv, h, 0, 0))
    base = QKV_TILES_KV
    return pl.pallas_call(
        functools.partial(_nsa_kernel, seq=s, ncmp=ncmp),
        grid=(NSA_KV_HEADS, b),
        in_specs=[pl.BlockSpec((1, qtiles, s, LANES), lambda h, bi: (bi, h, 0, 0)),
                  kv_spec(base), kv_spec(base + 2), kv_spec(base + 4), kv_spec(base + 6),
                  cmp_spec(0), cmp_spec(1),
                  pl.BlockSpec((1, 1, s, LANES), lambda h, bi: (bi, h, 0, 0)),
                  pl.BlockSpec((1, (s // t) * ncmp, rows), lambda h, bi: (h, 0, 0)),
                  pl.BlockSpec((1, 4, rows, t), lambda h, bi: (h, 0, 0, 0)),
                  pl.BlockSpec((LANES, ncmp), lambda h, bi: (0, 0)),
                  pl.BlockSpec((s, LANES), lambda h, bi: (0, 0)),
                  pl.BlockSpec(eg.shape, lambda h, bi: (0, 0))],
        out_specs=pl.BlockSpec((1, s, qw), lambda h, bi: (bi, 0, h)),
        out_shape=jax.ShapeDtypeStruct((b, s, NSA_Q_W), bf16),
        scratch_shapes=[pltpu.VMEM((rows, 2 * LANES), bf16),
                        pltpu.VMEM((s, 2 * LANES), bf16),
                        pltpu.VMEM((rows, LANES), f32),
                        pltpu.VMEM((rows, LANES), f32),
                        pltpu.VMEM((rows, LANES), f32),
                        pltpu.VMEM((2, rows, t), f32),
                        pltpu.VMEM((2, s, LANES), bf16),
                        pltpu.VMEM((2, s, LANES), bf16)],
        compiler_params=_cparams(("parallel", "parallel")),
        name="nsa",
    )(qkv, qkv, qkv, qkv, qkv, cmpkv, cmpkv, gates, bias_c, tbias, ov, et, eg)


def _fox_kernel(q_ref, k_ref, v_ref, qx_ref, kx_ref, tri_ref, o_ref, qaug_ref, kaug_ref, m_ref, acc_ref, s_ref,
                vaug_ref):
    t = TK_FOX
    tq = TQ_FOX
    seq = k_ref.shape[2]
    lane = lax.broadcasted_iota(i32, (tq, LANES), 1)
    lo_half = lane < HEAD_DIM
    kaug_ref[:, 0:LANES] = k_ref[0, 0]
    kaug_ref[:, LANES:2 * LANES] = kx_ref[0, 0]
    vaug_ref[0], vaug_ref[1] = _ones_values(v_ref[0, 0])

    def logits(kt):
        return _dot_nt(qaug_ref[...], kaug_ref[pl.ds(kt * t, t), :])

    def values(kt):
        return vaug_ref[0, pl.ds(kt * t, t), :], vaug_ref[1, pl.ds(kt * t, t), :]

    def query_tile(i, carry):
        rows = pl.ds(pl.multiple_of(i * tq, tq), tq)
        q = q_ref[0, 0, rows, :].astype(f32)
        qx = qx_ref[0, 0, rows, :].astype(f32)
        qaug_ref[0:tq, 0:LANES] = jnp.where(lo_half, q, 0.0).astype(bf16)
        qaug_ref[tq:2 * tq, 0:LANES] = jnp.where(lo_half, 0.0, q).astype(bf16)
        qaug_ref[0:tq, LANES:2 * LANES] = jnp.where(lane < XCOLS, qx, 0.0).astype(bf16)
        qaug_ref[tq:2 * tq, LANES:2 * LANES] = jnp.where((lane >= XCOLS) & (lane < 2 * XCOLS), qx, 0.0).astype(bf16)

        _flash_init(m_ref, acc_ref)
        _flash_loop(0, i, logits, values, s_ref, m_ref, acc_ref, last_fix=lambda s: s + tri_ref[...])
        o = _flash_finish(acc_ref)
        o_ref[0, rows, :] = jnp.where(lo_half, o[0:tq], o[tq:2 * tq]).astype(o_ref.dtype)
        return carry

    lax.fori_loop(0, seq // tq, query_tile, 0)


def _fox(qkv, qx, kx):
    b, _, s, _ = qkv.shape
    t = TK_FOX
    assert t == TQ_FOX
    r = np.arange(2 * TQ_FOX)[:, None] % TQ_FOX
    tri = jnp.asarray(np.where(np.arange(t)[None, :] <= r, 0.0, NEG_INF), f32)
    base = QKV_TILES_FOX
    return pl.pallas_call(
        _fox_kernel,
        grid=(b, FOX_PAIRS),
        in_specs=[pl.BlockSpec((1, 1, s, LANES), lambda bi, p: (bi, base + p, 0, 0)),
                  pl.BlockSpec((1, 1, s, LANES), lambda bi, p: (bi, base + FOX_PAIRS + p, 0, 0)),
                  pl.BlockSpec((1, 1, s, LANES), lambda bi, p: (bi, base + 2 * FOX_PAIRS + p, 0, 0)),
                  pl.BlockSpec((1, 1, s, LANES), lambda bi, p: (bi, p, 0, 0)),
                  pl.BlockSpec((1, 1, s, LANES), lambda bi, p: (bi, p, 0, 0)),
                  pl.BlockSpec((2 * TQ_FOX, t), lambda bi, p: (0, 0))],
        out_specs=pl.BlockSpec((1, s, LANES), lambda bi, p: (bi, 0, p)),
        out_shape=jax.ShapeDtypeStruct((b, s, FOX_W), bf16),
        scratch_shapes=[pltpu.VMEM((2 * TQ_FOX, 2 * LANES), bf16),
                        pltpu.VMEM((s, 2 * LANES), bf16),
                        pltpu.VMEM((2 * TQ_FOX, LANES), f32),
                        pltpu.VMEM((2 * TQ_FOX, LANES), f32),
                        pltpu.VMEM((2, 2 * TQ_FOX, t), f32),
                        pltpu.VMEM((2, s, LANES), bf16)],
        compiler_params=_cparams(("parallel", "parallel")),
        name="fox",
    )(qkv, qkv, qkv, qx, kx, tri)


def _merge_kernel(yn_ref, yf_ref, xb_ref, x_ref, wn_ref, wf_ref, wmg_ref, wo_ref, g_ref, b_ref,
                  xo_ref, xbo_ref):
    mg = _dot(xb_ref[...], wmg_ref[...])
    merged = (_sigmoid(mg[:, 0:D_MODEL]) * _dot(yn_ref[...], wn_ref[...])
              + _sigmoid(mg[:, D_MODEL:2 * D_MODEL]) * _dot(yf_ref[...], wf_ref[...]))
    hmix = _dot(merged.astype(bf16), wo_ref[...])
    xn = _layer_norm(DN_ALPHA * x_ref[...] + hmix, g_ref[...], b_ref[...])
    xo_ref[...] = xn
    xbo_ref[...] = xn.astype(bf16)


def _merge(yn, yf, xb, x, wn, wf, wmg, wo, g, bb):
    m = x.shape[0]
    tm = TM_MERGE
    row = lambda w: pl.BlockSpec((tm, w), lambda i: (i, 0))
    full = lambda a: pl.BlockSpec(a.shape, lambda i: (0, 0))
    return pl.pallas_call(
        _merge_kernel,
        grid=(m // tm,),
        in_specs=[row(NSA_Q_W), row(FOX_W), row(D_MODEL), row(D_MODEL),
                  full(wn), full(wf), full(wmg), full(wo), full(g), full(bb)],
        out_specs=[row(D_MODEL), row(D_MODEL)],
        out_shape=[jax.ShapeDtypeStruct((m, D_MODEL), f32), jax.ShapeDtypeStruct((m, D_MODEL), bf16)],
        compiler_params=_cparams(("parallel",)),
        name="merge",
    )(yn, yf, xb, x, wn, wf, wmg, wo, g, bb)


def _ffn_kernel(xb_ref, x_ref, wg_ref, wu_ref, wd_ref, g_ref, b_ref, xo_ref, xbo_ref, acc_ref, *, nf):
    f = pl.program_id(1)

    @pl.when(f == 0)
    def _():
        acc_ref[...] = jnp.zeros_like(acc_ref)

    xb = xb_ref[...]
    gate = _dot(xb, wg_ref[...])
    up = _dot(xb, wu_ref[...])
    act = (gate * _sigmoid(gate) * up).astype(bf16)
    acc_ref[...] += _dot(act, wd_ref[...])

    @pl.when(f == nf - 1)
    def _():
        xn = _layer_norm(DN_ALPHA * x_ref[...] + acc_ref[...], g_ref[...], b_ref[...])
        xo_ref[...] = xn
        xbo_ref[...] = xn.astype(bf16)


def _ffn(xb, x, wg, wu, wd, g, bb):
    m = x.shape[0]
    dff = wg.shape[1]
    tm, tf = TM_FFN, TF_FFN
    nf = dff // tf
    row = pl.BlockSpec((tm, D_MODEL), lambda i, f: (i, 0))
    vec = pl.BlockSpec((1, D_MODEL), lambda i, f: (0, 0))
    return pl.pallas_call(
        functools.partial(_ffn_kernel, nf=nf),
        grid=(m // tm, nf),
        in_specs=[row, row,
                  pl.BlockSpec((D_MODEL, tf), lambda i, f: (0, f)),
                  pl.BlockSpec((D_MODEL, tf), lambda i, f: (0, f)),
                  pl.BlockSpec((tf, D_MODEL), lambda i, f: (f, 0)),
                  vec, vec],
        out_specs=[row, row],
        out_shape=[jax.ShapeDtypeStruct((m, D_MODEL), f32), jax.ShapeDtypeStruct((m, D_MODEL), bf16)],
        scratch_shapes=[pltpu.VMEM((tm, D_MODEL), f32)],
        compiler_params=_cparams(("parallel", "arbitrary")),
        name="ffn",
    )(xb, x, wg, wu, wd, g, bb)


def _router_kernel(x_ref, r_ref, tri_ref, gate_ref, rank_ref, *, seq):
    tb = tri_ref.shape[0]
    x = x_ref[0]
    xh = x.astype(bf16)
    xl = (x - xh.astype(f32)).astype(bf16)
    r = r_ref[...]
    rh = r.astype(bf16)
    rl = (r - rh.astype(f32)).astype(bf16)
    logits = _dot(xh, rh) + _dot(xh, rl) + _dot(xl, rh)
    lane = lax.broadcasted_iota(i32, (seq, LANES), 1).astype(f32)
    low = -3.0e38
    lg = jnp.where(lane < N_EXPERTS, logits, low)
    m1 = jnp.max(lg, axis=-1, keepdims=True)
    i1 = jnp.min(jnp.where(lg == m1, lane, float(LANES)), axis=-1, keepdims=True)
    lg2 = jnp.where(lane == i1, low, lg)
    m2 = jnp.max(lg2, axis=-1, keepdims=True)
    i2 = jnp.min(jnp.where(lg2 == m2, lane, float(LANES)), axis=-1, keepdims=True)
    e2 = jnp.exp(m2 - m1)
    den = 1.0 + e2
    gate_ref[0] = jnp.where(lane == i1, 1.0 / den, jnp.where(lane == i2, e2 / den, 0.0))
    sel = (lane == i1) | (lane == i2)
    selb = jnp.where(sel, 1.0, 0.0).astype(bf16)
    carry = jnp.zeros((1, LANES), f32)
    for blk in range(seq // tb):
        sl = slice(blk * tb, (blk + 1) * tb)
        c = _dot(tri_ref[...], selb[sl]) + carry
        carry = c[tb - 1:tb, :]
        rank_ref[0, sl, :] = jnp.where(sel[sl], c - 1.0, -1.0)


def _router(x3, router_pad, tri):
    b, s, _ = x3.shape
    return pl.pallas_call(
        functools.partial(_router_kernel, seq=s),
        grid=(b,),
        in_specs=[pl.BlockSpec((1, s, D_MODEL), lambda i: (i, 0, 0)),
                  pl.BlockSpec((D_MODEL, LANES), lambda i: (0, 0)),
                  pl.BlockSpec(tri.shape, lambda i: (0, 0))],
        out_specs=[pl.BlockSpec((1, s, LANES), lambda i: (i, 0, 0)),
                   pl.BlockSpec((1, s, LANES), lambda i: (i, 0, 0))],
        out_shape=[jax.ShapeDtypeStruct((b, s, LANES), f32),
                   jax.ShapeDtypeStruct((b, s, LANES), f32)],
        compiler_params=_cparams(("parallel",)),
        name="router",
    )(x3, router_pad, tri)


def _moe_kernel(tot_ref, tot_al_ref, off_ref, npc_ref, xb_ref, rankt_ref, gatet_ref, wg_ref, wu_ref, wd_ref,
                y_ref, xg_ref, acc_ref, *, seq, nf):
    tr = TR_MOE
    half = tr // 2
    nsubc = seq // SUB_MOE
    c = pl.program_id(0)
    e = pl.program_id(1)
    f = pl.program_id(2)
    ce = c * N_EXPERTS + e
    tot = tot_ref[ce]
    rem = tot % tr
    nfull = tot // tr + jnp.where(rem > half, 1, 0)
    has_tail = (rem > 0) & (rem <= half)
    tail0 = pl.multiple_of(nfull * tr, half)

    def pieces(fn):
        for q in range(nsubc):
            base = off_ref[ce * nsubc + q]

            def body(p, carry, q=q, base=base):
                fn(q, pl.multiple_of(base + p * PIECE_MOE, ROW_ALIGN))
                return carry
            lax.fori_loop(0, npc_ref[ce * nsubc + q], body, 0)

    def onehot_rows(q, r0):
        rk = rankt_ref[0, pl.ds(e, 1), q * SUB_MOE:(q + 1) * SUB_MOE]
        want = (r0 + lax.broadcasted_iota(i32, (PIECE_MOE, SUB_MOE), 0)).astype(f32)
        return rk == want

    def gather(q, r0):
        p = jnp.where(onehot_rows(q, r0), 1.0, 0.0).astype(bf16)
        xg_ref[pl.ds(r0, PIECE_MOE), :] += _dot(p, xb_ref[0, q * SUB_MOE:(q + 1) * SUB_MOE, :]).astype(bf16)

    def clear(r0, nr):
        xg_ref[pl.ds(r0, nr), :] = jnp.zeros((nr, D_MODEL), bf16)
        acc_ref[pl.ds(r0, nr), :] = jnp.zeros((nr, D_MODEL), f32)

    def hidden(r0, nr):
        xs = xg_ref[pl.ds(r0, nr), :]
        gate = _dot(xs, wg_ref[0, 0])
        up = _dot(xs, wu_ref[0, 0])
        act = (gate * _sigmoid(gate) * up).astype(bf16)
        acc_ref[pl.ds(r0, nr), :] += _dot(act, wd_ref[0, 0])

    def combine(q, r0):
        grow = gatet_ref[0, pl.ds(e, 1), q * SUB_MOE:(q + 1) * SUB_MOE]
        hit = onehot_rows(q, r0)
        w = jnp.sum(jnp.where(hit, grow, 0.0), axis=-1, keepdims=True)
        z = (acc_ref[pl.ds(r0, PIECE_MOE), :] * w).astype(bf16)
        p = jnp.where(hit, 1.0, 0.0).astype(bf16)
        y_ref[0, q * SUB_MOE:(q + 1) * SUB_MOE, :] += lax.dot_general(
            p, z, (((0,), (0,)), ((), ())), preferred_element_type=f32)

    def tiles(fn):
        def body(s, carry):
            fn(pl.multiple_of(s * tr, tr), tr)
            return carry
        lax.fori_loop(0, nfull, body, 0)

        @pl.when(has_tail)
        def _():
            fn(tail0, half)

    def zero(j, carry):
        y_ref[0, pl.ds(j * tr, tr), :] = jnp.zeros((tr, D_MODEL), f32)
        return carry

    lax.fori_loop(0, jnp.where((e == 0) & (f == 0), seq // tr, 0), zero, 0)

    @pl.when(f == 0)
    def _():
        tiles(clear)
        clear(pl.multiple_of(tot_al_ref[ce], ROW_ALIGN), tr)
        pieces(gather)

    tiles(hidden)

    @pl.when(f == nf - 1)
    def _():
        pieces(combine)


def _moe(xb3, gate, rank, wg, wu, wd, layer):
    b, s, _ = xb3.shape
    dff = wg.shape[3]
    nf = dff // TF_MOE
    nsubc = s // SUB_MOE
    rk = rank[:, :, :N_EXPERTS]
    cnt = jnp.sum((rk >= 0.0).reshape(b, nsubc, SUB_MOE, N_EXPERTS), axis=2).astype(i32)
    first = jnp.cumsum(cnt, axis=1) - cnt
    off = first // ROW_ALIGN * ROW_ALIGN
    npc = jnp.where(cnt > 0, (first - off + cnt + PIECE_MOE - 1) // PIECE_MOE, 0)
    post = rk.transpose(0, 2, 1)
    gatet = gate[:, :, :N_EXPERTS].transpose(0, 2, 1)
    tot = jnp.sum(cnt, axis=1).reshape(-1)
    tot_al = (tot + ROW_ALIGN - 1) // ROW_ALIGN * ROW_ALIGN
    off_flat = off.transpose(0, 2, 1).reshape(-1)
    npc = npc.transpose(0, 2, 1).reshape(-1)
    buf_rows = s + ROW_ALIGN + PIECE_MOE + TR_MOE
    grid_spec = pltpu.PrefetchScalarGridSpec(
        num_scalar_prefetch=4,
        grid=(b, N_EXPERTS, nf),
        in_specs=[pl.BlockSpec((1, s, D_MODEL), lambda c, e, f, *_: (c, 0, 0)),
                  pl.BlockSpec((1, N_EXPERTS, s), lambda c, e, f, *_: (c, 0, 0)),
                  pl.BlockSpec((1, N_EXPERTS, s), lambda c, e, f, *_: (c, 0, 0)),
                  pl.BlockSpec((1, 1, D_MODEL, TF_MOE), lambda c, e, f, *_: (layer, e, 0, f)),
                  pl.BlockSpec((1, 1, D_MODEL, TF_MOE), lambda c, e, f, *_: (layer, e, 0, f)),
                  pl.BlockSpec((1, 1, TF_MOE, D_MODEL), lambda c, e, f, *_: (layer, e, f, 0))],
        out_specs=pl.BlockSpec((1, s, D_MODEL), lambda c, e, f, *_: (c, 0, 0)),
        scratch_shapes=[pltpu.VMEM((buf_rows, D_MODEL), bf16), pltpu.VMEM((buf_rows, D_MODEL), f32)],
    )
    return pl.pallas_call(
        functools.partial(_moe_kernel, seq=s, nf=nf),
        grid_spec=grid_spec,
        out_shape=jax.ShapeDtypeStruct((b, s, D_MODEL), f32),
        compiler_params=_cparams(("parallel", "arbitrary", "arbitrary")),
        name="moe",
    )(tot, tot_al, off_flat, npc, xb3, post, gatet, wg, wu, wd)


def _resln_kernel(x_ref, y_ref, g_ref, b_ref, xo_ref, xbo_ref):
    xn = _layer_norm(DN_ALPHA * x_ref[...] + y_ref[...], g_ref[...], b_ref[...])
    xo_ref[...] = xn
    xbo_ref[...] = xn.astype(bf16)


def _resln(x, y, g, bb):
    m = x.shape[0]
    tm = TM_FFN
    row = pl.BlockSpec((tm, D_MODEL), lambda i: (i, 0))
    vec = pl.BlockSpec((1, D_MODEL), lambda i: (0, 0))
    return pl.pallas_call(
        _resln_kernel,
        grid=(m // tm,),
        in_specs=[row, row, vec, vec],
        out_specs=[row, row],
        out_shape=[jax.ShapeDtypeStruct((m, D_MODEL), f32), jax.ShapeDtypeStruct((m, D_MODEL), bf16)],
        compiler_params=_cparams(("parallel",)),
        name="resln",
    )(x, y, g, bb)


def _t5_bucket(dist):
    n = jnp.maximum(dist, 0)
    max_exact = REL_BUCKETS // 2
    large = max_exact + (jnp.log(jnp.maximum(n, 1).astype(f32) / max_exact)
                         / math.log(REL_MAX_DIST / max_exact) * (REL_BUCKETS - max_exact)).astype(i32)
    large = jnp.minimum(large, REL_BUCKETS - 1)
    return jnp.where(n < max_exact, n, large)


def _bias_of_dist(rel_bias, dist):
    onehot = (_t5_bucket(dist)[None] == jnp.arange(REL_BUCKETS).reshape((-1,) + (1,) * dist.ndim)).astype(f32)
    return LOG2E * jnp.einsum("kh,k...->h...", rel_bias.astype(f32), onehot, precision=lax.Precision.HIGHEST)


def _bias_tables(rel_bias, seq):
    t = T_ATT
    rows = NSA_GROUP * t
    ncmp = seq // CMP_STRIDE
    d0 = jnp.arange(t)[:, None] - jnp.arange(t)[None, :]
    offs = jnp.array([0, t, 2 * t, WINDOW]).reshape(4, 1, 1)
    kinds = _bias_of_dist(rel_bias, offs + d0[None])
    mask = jnp.stack([d0 >= 0, d0 == d0, d0 == d0, d0 < 0])
    kinds = jnp.where(mask[None], kinds, NEG_INF)
    order = np.array(NSA_ROW_ORDER)
    tbias = kinds.reshape(NSA_KV_HEADS, NSA_GROUP, 4, t, t)[:, order].transpose(0, 2, 1, 3, 4)
    tbias = tbias.reshape(NSA_KV_HEADS, 4, rows, t)
    cend = jnp.arange(ncmp) * CMP_STRIDE + CMP_BLOCK - 1
    dist_c = jnp.arange(seq)[:, None] - cend[None, :]
    bc = jnp.where(dist_c >= 0, _bias_of_dist(rel_bias, dist_c), NEG_INF)
    bc = bc.reshape(NSA_KV_HEADS, NSA_GROUP, seq // t, t, ncmp)[:, order].transpose(0, 2, 4, 1, 3)
    return tbias, bc.reshape(NSA_KV_HEADS, (seq // t) * ncmp, rows)


def _selection_constants(seq):
    ncmp = seq // CMP_STRIDE
    nsel = seq // SEL_BLOCK
    c0 = np.arange(ncmp)[:, None] * CMP_STRIDE
    s0 = np.arange(LANES)[None, :] * SEL_BLOCK
    ov = np.maximum(np.minimum(c0 + CMP_BLOCK, s0 + SEL_BLOCK) - np.maximum(c0, s0), 0) / CMP_BLOCK
    ov[ncmp - 1, :] = 0.0
    ov[:, nsel:] = 0.0
    et = (np.arange(seq)[:, None] // SEL_BLOCK == np.arange(LANES)[None, :]).astype(np.float32)
    eg = np.zeros((LANES, 3 * (NSA_GROUP // 2) * LANES), np.float32)
    for br in range(3):
        for g in range(NSA_GROUP):
            c0 = (br * (NSA_GROUP // 2) + g // 2) * LANES + (g % 2) * HEAD_DIM
            eg[br * NSA_GROUP + g, c0:c0 + HEAD_DIM] = 1.0
    return jnp.asarray(ov.T, bf16), jnp.asarray(et, bf16), jnp.asarray(eg, bf16)


def _fox_placement():
    xw = FOX_PAIRS * LANES
    pq = np.zeros((3, LANES, xw), np.float32)
    pk = np.zeros((3, LANES, xw), np.float32)
    oq = np.zeros((1, xw), np.float32)
    ok = np.zeros((1, xw), np.float32)
    for p in range(FOX_PAIRS):
        for hh in range(2):
            src = FGATE_LANE + 2 * p + hh
            base = p * LANES + hh * XCOLS
            for part in range(3):
                pk[part, src, base + part] = -1.0
                pq[part, src, base + 3 + part] = 1.0
                oq[0, base + part] = 1.0
                ok[0, base + 3 + part] = 1.0
    return jnp.asarray(pq, bf16), jnp.asarray(pk, bf16), jnp.asarray(oq), jnp.asarray(ok)


def _layer_weights(w_in, layer_pe, w1, w2, f_bias):
    offs = np.cumsum((NSA_Q_W, 6 * 2 * HEAD_DIM, 3 * NSA_HEADS, 3 * FOX_W, FOX_HEADS, 2 * D_MODEL))
    kv0, g0, fx0, ff0, mg0 = offs[0], offs[1], offs[2], offs[3], offs[4]
    scale = HEAD_DIM ** -0.5 * LOG2E
    kvw = NSA_KV_HEADS * HEAD_DIM
    w_kv = w_in[:, kv0:g0]
    w_kvdup = jnp.repeat(w_kv[:, 2 * kvw:].reshape(D_MODEL, 4 * NSA_KV_HEADS, 1, HEAD_DIM), 2, axis=2)
    w_kvdup = w_kvdup.reshape(D_MODEL, 8 * kvw)
    w_qkv = jnp.concatenate([w_in[:, :NSA_Q_W] * scale, w_kvdup, w_in[:, fx0:fx0 + FOX_W] * scale,
                             w_in[:, fx0 + FOX_W:ff0]], axis=1).astype(bf16)
    zeros = lambda n: jnp.zeros((D_MODEL, n), w_in.dtype)
    ng = 3 * NSA_GROUP
    w_g = w_in[:, g0:fx0].reshape(D_MODEL, NSA_KV_HEADS, NSA_GROUP, 3).transpose(0, 1, 3, 2)
    gate_cols = lambda h: w_g[:, h].reshape(D_MODEL, ng)
    w_aux = jnp.concatenate([w_kv[:, :2 * kvw],
                             gate_cols(0), zeros(FGATE_LANE - ng), w_in[:, ff0:mg0],
                             zeros(LANES - FGATE_LANE - FOX_HEADS),
                             gate_cols(1), zeros(LANES - ng)], axis=1).astype(bf16)
    fb_row = jnp.zeros((1, LANES), f32).at[0, FGATE_LANE:FGATE_LANE + FOX_HEADS].set(f_bias.astype(f32))
    pe2 = jnp.tile(layer_pe.astype(f32), (1, 1, NSA_KV_HEADS))
    w1r = w1.reshape(2, CMP_BLOCK, HEAD_DIM, CMP_HIDDEN).astype(bf16)
    zero = jnp.zeros_like(w1r)
    w1bd = jnp.concatenate([jnp.concatenate([w1r, zero], axis=-1),
                            jnp.concatenate([zero, w1r], axis=-1)], axis=-2)
    w2d = jnp.concatenate([w2, w2], axis=-1).astype(bf16)
    return (w_qkv, w_aux, w_in[:, mg0:].astype(bf16), fb_row, pe2, w1bd[:, :CMP_STRIDE], w1bd[:, CMP_STRIDE:],
            w2d)


def kernel(x, w_in, nsa_cmp_pe, nsa_cmp_w1, nsa_cmp_w2, fox_f_bias, w_nsa_branch, w_fox_branch, w_out,
           rel_bias, ln1_g, ln1_b, ln2_g, ln2_b, dense_w_gate, dense_w_up, dense_w_down, moe_router,
           moe_w_gate, moe_w_up, moe_w_down):
    b, s, d = x.shape
    m = b * s
    tbias, bias_c = _bias_tables(rel_bias, s)
    ov, et, eg = _selection_constants(s)
    pq, pk, oq, ok = _fox_placement()
    tri128 = jnp.asarray(np.tril(np.ones((LANES, LANES), np.float32)), bf16)
    tri256 = jnp.asarray(np.tril(np.ones((256, 256), np.float32)), bf16)

    moe_wg, moe_wu, moe_wd = moe_w_gate.astype(bf16), moe_w_up.astype(bf16), moe_w_down.astype(bf16)
    xf = x.reshape(m, d).astype(f32)
    xb = xf.astype(bf16)
    for layer in range(DEPTH):
        w_qkv, w_aux, w_mg, fb_row, pe2, w1t, w1b, w2d = _layer_weights(
            w_in[layer], nsa_cmp_pe[layer], nsa_cmp_w1[layer], nsa_cmp_w2[layer], fox_f_bias[layer])
        qkv, aux = _proj(xb, w_qkv, w_aux, b)
        aux = aux.reshape(b, s, AUX_W)
        gates, qx, kx = _gateprep(aux, fb_row, tri128, pq, pk, oq, ok)
        cmpkv = _compress(aux, pe2, w1t, w1b, w2d)
        y_nsa = _nsa(qkv, cmpkv, gates, bias_c, tbias, ov, et, eg).reshape(m, NSA_Q_W)
        y_fox = _fox(qkv, qx, kx).reshape(m, FOX_W)
        xf, xb = _merge(y_nsa, y_fox, xb, xf, w_nsa_branch[layer].astype(bf16),
                        w_fox_branch[layer].astype(bf16), w_mg, w_out[layer].astype(bf16),
                        ln1_g[layer].reshape(1, d), ln1_b[layer].reshape(1, d))
        j = layer // 2
        g2, b2 = ln2_g[layer].reshape(1, d), ln2_b[layer].reshape(1, d)
        if layer % 2 == 0:
            xf, xb = _ffn(xb, xf, dense_w_gate[j].astype(bf16), dense_w_up[j].astype(bf16),
                          dense_w_down[j].astype(bf16), g2, b2)
        else:
            router_pad = jnp.zeros((d, LANES), f32).at[:, :N_EXPERTS].set(moe_router[j].astype(f32))
            gate, rank = _router(xf.reshape(b, s, d), router_pad, tri256)
            y = _moe(xb.reshape(b, s, d), gate, rank, moe_wg, moe_wu, moe_wd, j)
            xf, xb = _resln(xf, y.reshape(m, d), g2, b2)
    return xf.reshape(b, s, d).astype(x.dtype)
```

```python
import functools
import math

import numpy as np
import jax
import jax.numpy as jnp
from jax import lax
from jax.experimental import pallas as pl
from jax.experimental.pallas import tpu as pltpu

f32 = jnp.float32
bf16 = jnp.bfloat16
i32 = jnp.int32

D_MODEL = 1024
HEAD_DIM = 64
LANES = 128
NSA_HEADS = 8
NSA_KV_HEADS = 2
NSA_GROUP = NSA_HEADS // NSA_KV_HEADS
NSA_ROW_ORDER = (0, 2, 1, 3)
FOX_HEADS = 8
FOX_PAIRS = FOX_HEADS // 2
CMP_BLOCK = 32
CMP_STRIDE = 16
CMP_HIDDEN = 128
SEL_BLOCK = 64
N_SELECT = 16
N_LOCAL_SEL = 2
WINDOW = 512
REL_BUCKETS = 32
REL_MAX_DIST = 128
N_EXPERTS = 8
DEPTH = 4
DN_ALPHA = (2 * DEPTH) ** 0.25
LN_EPS = 1e-5
FORCED_SCORE = 1e4
NEG_INF = -1e30
LOG2E = math.log2(math.e)

NSA_Q_W = NSA_HEADS * HEAD_DIM
FOX_W = FOX_HEADS * HEAD_DIM
QKV_TILES_KV = NSA_Q_W // LANES
QKV_TILES_FOX = QKV_TILES_KV + 4 * NSA_KV_HEADS
QKV_W = (QKV_TILES_FOX + 3 * FOX_PAIRS) * LANES
AUX_W = 4 * LANES
FGATE_LANE = 24
XCOLS = 6

T_ATT = 256
TQ_FOX = 512
TK_FOX = 512
TM_PROJ = 512
TM_MERGE = 512
TM_FFN = 512
TF_FFN = 2816
TR_MOE = 256
SUB_MOE = 512
PIECE_MOE = 192
ROW_ALIGN = 16
TF_MOE = 896
VMEM_LIMIT = 56 * 1024 * 1024


def _cparams(sem):
    return pltpu.CompilerParams(dimension_semantics=sem, vmem_limit_bytes=VMEM_LIMIT)


def _dot(a, b):
    return jnp.dot(a, b, preferred_element_type=f32)


def _dot_nt(a, b):
    return lax.dot_general(a, b, (((1,), (1,)), ((), ())), preferred_element_type=f32)


def _sigmoid(x):
    return 1.0 / (1.0 + jnp.exp(-x))


def _layer_norm(z, g, b):
    mu = jnp.mean(z, axis=-1, keepdims=True)
    zc = z - mu
    var = jnp.mean(zc * zc, axis=-1, keepdims=True)
    return zc * lax.rsqrt(var + LN_EPS) * g + b


def _split3(x):
    hi = x.astype(bf16)
    r1 = x - hi.astype(f32)
    mid = r1.astype(bf16)
    lo = (r1 - mid.astype(f32)).astype(bf16)
    return hi, mid, lo


def _proj_kernel(x_ref, w_ref, wa_ref, o_ref, oa_ref):
    x = x_ref[...]
    res = _dot(x, w_ref[...])
    for j in range(o_ref.shape[1]):
        o_ref[0, j] = res[:, j * LANES:(j + 1) * LANES].astype(o_ref.dtype)
    oa_ref[...] = _dot(x, wa_ref[...])


def _proj(xb, w, wa, batch):
    m, k = xb.shape
    n, na = w.shape[1], wa.shape[1]
    nb = m // batch // TM_PROJ
    return pl.pallas_call(
        _proj_kernel,
        grid=(m // TM_PROJ,),
        in_specs=[pl.BlockSpec((TM_PROJ, k), lambda i: (i, 0)),
                  pl.BlockSpec((k, n), lambda i: (0, 0)),
                  pl.BlockSpec((k, na), lambda i: (0, 0))],
        out_specs=[pl.BlockSpec((1, n // LANES, TM_PROJ, LANES), lambda i: (i // nb, 0, i % nb, 0)),
                   pl.BlockSpec((TM_PROJ, na), lambda i: (i, 0))],
        out_shape=[jax.ShapeDtypeStruct((batch, n // LANES, m // batch, LANES), bf16),
                   jax.ShapeDtypeStruct((m, na), f32)],
        compiler_params=_cparams(("parallel",)),
        name="proj",
    )(xb, w, wa)


def _gateprep_kernel(a_ref, fb_ref, tri_ref, pq_ref, pk_ref, oq_ref, ok_ref, g_ref, qx_ref, kx_ref, *, seq):
    tb = LANES
    tri = tri_ref[...]
    local = []
    for blk in range(seq // tb):
        sl = slice(blk * tb, (blk + 1) * tb)
        va = a_ref[0, sl, 0:LANES]
        g_ref[0, 0, sl, :] = _sigmoid(va)
        g_ref[0, 1, sl, :] = _sigmoid(a_ref[0, sl, LANES:2 * LANES])
        z = va + fb_ref[...]
        logf = jnp.minimum(z, 0.0) - jnp.log1p(jnp.exp(-jnp.abs(z)))
        hi, mid, lo = _split3(logf)
        local.append(_dot(tri, hi) + _dot(tri, mid) + _dot(tri, lo))
    carry = jnp.zeros((1, LANES), f32)
    for blk in range(seq // tb):
        sl = slice(blk * tb, (blk + 1) * tb)
        c = local[blk] + carry
        carry = c[tb - 1:tb, :]
        chi, cmid, clo = _split3(c * LOG2E)
        qx = _dot(chi, pq_ref[0]) + _dot(cmid, pq_ref[1]) + _dot(clo, pq_ref[2]) + oq_ref[...]
        kx = _dot(chi, pk_ref[0]) + _dot(cmid, pk_ref[1]) + _dot(clo, pk_ref[2]) + ok_ref[...]
        for p in range(FOX_PAIRS):
            qx_ref[0, p, sl, :] = qx[:, p * LANES:(p + 1) * LANES].astype(bf16)
            kx_ref[0, p, sl, :] = kx[:, p * LANES:(p + 1) * LANES].astype(bf16)


def _gateprep(aux, fb_row, tri, pq, pk, oq, ok):
    b, s, _ = aux.shape
    const2 = lambda a: pl.BlockSpec(a.shape, lambda i: (0, 0))
    const3 = lambda a: pl.BlockSpec(a.shape, lambda i: (0, 0, 0))
    return pl.pallas_call(
        functools.partial(_gateprep_kernel, seq=s),
        grid=(b,),
        in_specs=[pl.BlockSpec((1, s, 2 * LANES), lambda i: (i, 0, 1)),
                  const2(fb_row), const2(tri), const3(pq), const3(pk), const2(oq), const2(ok)],
        out_specs=[pl.BlockSpec((1, 2, s, LANES), lambda i: (i, 0, 0, 0)),
                   pl.BlockSpec((1, FOX_PAIRS, s, LANES), lambda i: (i, 0, 0, 0)),
                   pl.BlockSpec((1, FOX_PAIRS, s, LANES), lambda i: (i, 0, 0, 0))],
        out_shape=[jax.ShapeDtypeStruct((b, 2, s, LANES), f32),
                   jax.ShapeDtypeStruct((b, FOX_PAIRS, s, LANES), bf16),
                   jax.ShapeDtypeStruct((b, FOX_PAIRS, s, LANES), bf16)],
        compiler_params=_cparams(("parallel",)),
        name="gateprep",
    )(aux, fb_row, tri, pq, pk, oq, ok)


def _gelu_tanh(x):
    c = math.sqrt(2.0 / math.pi)
    return x * (0.5 * (1.0 + jnp.tanh(c * (x + 0.044715 * (x * x * x)))))


def _compress_kernel(a_ref, pe_ref, w1t_ref, w1b_ref, w2_ref, o_ref, *, nhalf):
    top = jnp.zeros((nhalf, NSA_KV_HEADS * CMP_HIDDEN), f32)
    bot = jnp.zeros((nhalf, NSA_KV_HEADS * CMP_HIDDEN), f32)
    for l in range(CMP_STRIDE):
        rows = a_ref[0, pl.ds(l, nhalf, stride=CMP_STRIDE), :]
        top = top + _dot((rows + pe_ref[0, l:l + 1, :]).astype(bf16), w1t_ref[0, l])
        bot = bot + _dot((rows + pe_ref[0, CMP_STRIDE + l:CMP_STRIDE + l + 1, :]).astype(bf16), w1b_ref[0, l])
    pre = top + pltpu.roll(bot, nhalf - 1, 0)
    act = _gelu_tanh(pre).astype(bf16)
    for h in range(NSA_KV_HEADS):
        o_ref[0, 0, h] = _dot(act[:, h * CMP_HIDDEN:(h + 1) * CMP_HIDDEN], w2_ref[0]).astype(o_ref.dtype)


def _compress(aux, pe2, w1t, w1b, w2d):
    b, s, _ = aux.shape
    nhalf = s // CMP_STRIDE
    w1_spec = pl.BlockSpec((1,) + w1t.shape[1:], lambda i, j: (j, 0, 0, 0))
    return pl.pallas_call(
        functools.partial(_compress_kernel, nhalf=nhalf),
        grid=(b, 2),
        in_specs=[pl.BlockSpec((1, s, LANES), lambda i, j: (i, 0, j)),
                  pl.BlockSpec((1, CMP_BLOCK, LANES), lambda i, j: (j, 0, 0)),
                  w1_spec, w1_spec,
                  pl.BlockSpec((1, CMP_HIDDEN, LANES), lambda i, j: (j, 0, 0))],
        out_specs=pl.BlockSpec((1, 1, NSA_KV_HEADS, nhalf, LANES), lambda i, j: (i, j, 0, 0, 0)),
        out_shape=jax.ShapeDtypeStruct((b, 2, NSA_KV_HEADS, nhalf, LANES), bf16),
        compiler_params=_cparams(("parallel", "parallel")),
        name="compress",
    )(aux, pe2, w1t, w1b, w2d)


def _flash_init(m_ref, acc_ref):
    m_ref[...] = jnp.full(m_ref.shape, NEG_INF, f32)
    acc_ref[...] = jnp.zeros(acc_ref.shape, f32)


def _ones_values(v):
    lo_half = lax.broadcasted_iota(i32, v.shape, 1) < HEAD_DIM
    vf = v.astype(f32)
    return jnp.where(lo_half, vf, 1.0).astype(bf16), jnp.where(lo_half, 1.0, vf).astype(bf16)


def _flash_step(s, v_lo, v_hi, m_ref, acc_ref):
    nk = s.shape[1] // LANES
    half = s.shape[0] // 2
    cols = [s[:, c * LANES:(c + 1) * LANES] for c in range(nk)]
    mx = cols[0]
    for c in cols[1:]:
        mx = jnp.maximum(mx, c)
    m_old = m_ref[...]
    m_new = jnp.maximum(m_old, jnp.broadcast_to(jnp.max(mx, axis=-1, keepdims=True), m_old.shape))
    alpha = jnp.exp2(m_old - m_new)
    p = jnp.concatenate([jnp.exp2(c - m_new).astype(bf16) for c in cols], axis=1)
    pv = jnp.concatenate([_dot(p[0:half], v_lo), _dot(p[half:], v_hi)], axis=0)
    acc_ref[...] = alpha * acc_ref[...] + pv
    m_ref[...] = m_new


def _flash_loop(first, last, logits_fn, v_fn, s_ref, m_ref, acc_ref, last_fix=None):
    sa, sb = s_ref.at[0], s_ref.at[1]
    n = last - first + 1
    pairs = (n - 1) // 2

    def step(buf, kt, fix=None):
        s = buf[...] if fix is None else fix(buf[...])
        _flash_step(s, *v_fn(kt), m_ref, acc_ref)

    sa[...] = logits_fn(first)

    def body(j, carry):
        kt = first + 2 * j
        sb[...] = logits_fn(kt + 1)
        step(sa, kt)
        sa[...] = logits_fn(kt + 2)
        step(sb, kt + 1)
        return carry

    lax.fori_loop(0, pairs, body, 0)
    two_left = n - 2 * pairs == 2

    @pl.when(two_left)
    def _():
        sb[...] = logits_fn(last)
        step(sa, last - 1)
        step(sb, last, last_fix)

    @pl.when(jnp.logical_not(two_left))
    def _():
        step(sa, last, last_fix)


def _flash_pair(acc_lo, acc_hi, lo_half):
    num = jnp.where(lo_half, acc_lo, acc_hi)
    den = pltpu.roll(jnp.where(lo_half, acc_hi, acc_lo), HEAD_DIM, 1)
    return num * (1.0 / jnp.maximum(den, 1e-30))


def _nsa_kernel(q_ref, ks_ref, vs_ref, kw_ref, vw_ref, kc_ref, vc_ref, gate_ref, bc_ref, tb_ref, ov_ref,
                et_ref, eg_ref, o_ref, qaug_ref, kaug_ref, m_ref, acc_ref, os_ref, s_ref, vsel_ref,
                vwin_ref, *, seq, ncmp):
    kaug_ref[:, 0:LANES] = ks_ref[0, 0]
    kaug_ref[:, LANES:2 * LANES] = et_ref[...]
    vsel_ref[0], vsel_ref[1] = _ones_values(vs_ref[0, 0])
    vwin_ref[0], vwin_ref[1] = _ones_values(vw_ref[0, 0])

    def query_tile(i, carry):
        _nsa_tile(i, q_ref, kw_ref, kc_ref, vc_ref, gate_ref, bc_ref, tb_ref, ov_ref, eg_ref, o_ref, qaug_ref,
                  kaug_ref, m_ref, acc_ref, os_ref, s_ref, vsel_ref, vwin_ref, seq=seq, ncmp=ncmp)
        return carry

    lax.fori_loop(0, seq // T_ATT, query_tile, 0)


def _nsa_tile(i, q_ref, kw_ref, kc_ref, vc_ref, gate_ref, bc_ref, tb_ref, ov_ref, eg_ref, o_ref, qaug_ref,
              kaug_ref, m_ref, acc_ref, os_ref, s_ref, vsel_ref, vwin_ref, *, seq, ncmp):
    t = T_ATT
    rows = NSA_GROUP * t
    nsel = seq // SEL_BLOCK
    topn = min(N_SELECT, nsel)
    t0 = i * t
    tile_rows = pl.ds(pl.multiple_of(t0, t), t)
    lane = lax.broadcasted_iota(i32, (t, LANES), 1)
    lo_half = lane < HEAD_DIM

    for rb, g in enumerate(NSA_ROW_ORDER):
        qg = q_ref[0, g // 2, tile_rows, :].astype(f32)
        keep = lo_half if g % 2 == 0 else jnp.logical_not(lo_half)
        qaug_ref[rb * t:(rb + 1) * t, 0:LANES] = jnp.where(keep, qg, 0.0).astype(bf16)
    qs = qaug_ref[:, 0:LANES]

    s = _dot_nt(kc_ref[0, 0, 0], qs) + bc_ref[0, pl.ds(pl.multiple_of(i * ncmp, ncmp), ncmp), :]
    sees_any = t0 + (lax.broadcasted_iota(i32, (1, rows), 1) & (t - 1)) >= CMP_BLOCK - 1
    s = s - jnp.max(s, axis=0, keepdims=True)
    e = jnp.where(sees_any, jnp.exp2(s), 0.0)
    p_c = e * (1.0 / jnp.maximum(jnp.sum(e, axis=0, keepdims=True), 1e-30))
    o_c = lax.dot_general(p_c.astype(bf16), vc_ref[0, 0, 0], (((0,), (0,)), ((), ())),
                          preferred_element_type=f32)

    psum = p_c[:, 0:t]
    for g in range(1, NSA_GROUP):
        psum = psum + p_c[:, g * t:(g + 1) * t]
    hi = psum.astype(bf16)
    lo = (psum - hi.astype(f32)).astype(bf16)
    imp = (_dot(ov_ref[...], hi) + _dot(ov_ref[...], lo))[0:nsel]
    blk = lax.broadcasted_iota(i32, (nsel, t), 0)
    cur = (t0 + lax.broadcasted_iota(i32, (nsel, t), 1)) >> 6
    forced = (blk == 0) | ((blk <= cur) & (blk > cur - N_LOCAL_SEL))
    val = jnp.where(forced, FORCED_SCORE, jnp.where(blk <= cur, imp, -1.0))
    ngrp = nsel // 8
    sub = lax.broadcasted_iota(i32, (8, t), 0)
    vals = [val[8 * r:8 * r + 8, :] for r in range(ngrp)]
    cnts = [jnp.zeros((8, t), f32) for _ in range(ngrp)]
    for j in range(nsel):
        vj = val[j:j + 1, :]
        for r in range(ngrp):
            ahead = jnp.where(vj > vals[r], 1.0, 0.0)
            ahead_or_tied = jnp.where(vj >= vals[r], 1.0, 0.0)
            if r < j // 8:
                cnts[r] = cnts[r] + ahead
            elif r > j // 8:
                cnts[r] = cnts[r] + ahead_or_tied
            else:
                cnts[r] = cnts[r] + jnp.where(sub > j % 8, ahead_or_tied, ahead)
    cnt = jnp.concatenate(cnts, axis=0)
    mneg = jnp.where(cnt < topn, 0.0, NEG_INF)
    mneg = jnp.concatenate([mneg, jnp.zeros((LANES - nsel, t), f32)], axis=0).T.astype(bf16)
    for g in range(NSA_GROUP):
        qaug_ref[g * t:(g + 1) * t, LANES:2 * LANES] = mneg

    _flash_init(m_ref, os_ref)

    def sel_logits(kt):
        k = kaug_ref[pl.ds(kt * t, t), :]
        return _dot_nt(qaug_ref[...], k) + tb_ref[0, jnp.minimum(i - kt, 2)]

    def values(v_ref):
        return lambda kt: (v_ref[0, pl.ds(kt * t, t), :], v_ref[1, pl.ds(kt * t, t), :])

    _flash_loop(0, i, sel_logits, values(vsel_ref), s_ref, m_ref, os_ref)

    _flash_init(m_ref, acc_ref)
    nwin = WINDOW // t

    def win_logits(kt):
        d = i - kt
        kind = jnp.where(d == nwin, 3, d)
        return _dot_nt(qs, kw_ref[0, 0, pl.ds(kt * t, t), :]) + tb_ref[0, kind]

    _flash_loop(jnp.maximum(i - nwin, 0), i, win_logits, values(vwin_ref), s_ref, m_ref, acc_ref)

    ghi, gmid, glo = _split3(gate_ref[0, 0, tile_rows, :])
    gexp = _dot(ghi, eg_ref[...]) + _dot(gmid, eg_ref[...]) + _dot(glo, eg_ref[...])
    for j in range(NSA_GROUP // 2):
        ra, rb = NSA_ROW_ORDER.index(2 * j), NSA_ROW_ORDER.index(2 * j + 1)
        out = jnp.zeros((t, LANES), f32)
        sa, sb = slice(ra * t, (ra + 1) * t), slice(rb * t, (rb + 1) * t)
        pairs = (jnp.where(lo_half, o_c[sa], o_c[sb]),
                 _flash_pair(os_ref[sa, :], os_ref[sb, :], lo_half),
                 _flash_pair(acc_ref[sa, :], acc_ref[sb, :], lo_half))
        for br, o_pair in enumerate(pairs):
            col = (br * (NSA_GROUP // 2) + j) * LANES
            out = out + gexp[:, col:col + LANES] * o_pair
        o_ref[0, tile_rows, j * LANES:(j + 1) * LANES] = out.astype(o_ref.dtype)


def _nsa(qkv, cmpkv, gates, bias_c, tbias, ov, et, eg):
    b, _, s, _ = qkv.shape
    ncmp = cmpkv.shape[3]
    t = T_ATT
    rows = NSA_GROUP * t
    qw = NSA_GROUP * HEAD_DIM
    qtiles = qw // LANES
    kv_spec = lambda col: pl.BlockSpec((1, 1, s, LANES), lambda h, bi, col=col: (bi, col + h, 0, 0))
    cmp_spec = lambda kv: pl.BlockSpec((1, 1, 1, ncmp, LANES), lambda h, bi, kv=kv: (bi, kv, h, 0, 0))
    base = QKV_TILES_KV
    return pl.pallas_call(
        functools.partial(_nsa_kernel, seq=s, ncmp=ncmp),
        grid=(NSA_KV_HEADS, b),
        in_specs=[pl.BlockSpec((1, qtiles, s, LANES), lambda h, bi: (bi, h, 0, 0)),
                  kv_spec(base), kv_spec(base + 2), kv_spec(base + 4), kv_spec(base + 6),
                  cmp_spec(0), cmp_spec(1),
                  pl.BlockSpec((1, 1, s, LANES), lambda h, bi: (bi, h, 0, 0)),
                  pl.BlockSpec((1, (s // t) * ncmp, rows), lambda h, bi: (h, 0, 0)),
                  pl.BlockSpec((1, 4, rows, t), lambda h, bi: (h, 0, 0, 0)),
                  pl.BlockSpec((LANES, ncmp), lambda h, bi: (0, 0)),
                  pl.BlockSpec((s, LANES), lambda h, bi: (0, 0)),
                  pl.BlockSpec(eg.shape, lambda h, bi: (0, 0))],
        out_specs=pl.BlockSpec((1, s, qw), lambda h, bi: (bi, 0, h)),
        out_shape=jax.ShapeDtypeStruct((b, s, NSA_Q_W), bf16),
        scratch_shapes=[pltpu.VMEM((rows, 2 * LANES), bf16),
                        pltpu.VMEM((s, 2 * LANES), bf16),
                        pltpu.VMEM((rows, LANES), f32),
                        pltpu.VMEM((rows, LANES), f32),
                        pltpu.VMEM((rows, LANES), f32),
                        pltpu.VMEM((2, rows, t), f32),
                        pltpu.VMEM((2, s, LANES), bf16),
                        pltpu.VMEM((2, s, LANES), bf16)],
        compiler_params=_cparams(("parallel", "parallel")),
        name="nsa",
    )(qkv, qkv, qkv, qkv, qkv, cmpkv, cmpkv, gates, bias_c, tbias, ov, et, eg)


def _fox_kernel(q_ref, k_ref, v_ref, qx_ref, kx_ref, tri_ref, o_ref, qaug_ref, kaug_ref, m_ref, acc_ref, s_ref,
                vaug_ref):
    t = TK_FOX
    tq = TQ_FOX
    seq = k_ref.shape[2]
    lane = lax.broadcasted_iota(i32, (tq, LANES), 1)
    lo_half = lane < HEAD_DIM
    kaug_ref[:, 0:LANES] = k_ref[0, 0]
    kaug_ref[:, LANES:2 * LANES] = kx_ref[0, 0]
    vaug_ref[0], vaug_ref[1] = _ones_values(v_ref[0, 0])

    def logits(kt):
        return _dot_nt(qaug_ref[...], kaug_ref[pl.ds(kt * t, t), :])

    def values(kt):
        return vaug_ref[0, pl.ds(kt * t, t), :], vaug_ref[1, pl.ds(kt * t, t), :]

    def query_tile(i, carry):
        rows = pl.ds(pl.multiple_of(i * tq, tq), tq)
        q = q_ref[0, 0, rows, :].astype(f32)
        qx = qx_ref[0, 0, rows, :].astype(f32)
        qaug_ref[0:tq, 0:LANES] = jnp.where(lo_half, q, 0.0).astype(bf16)
        qaug_ref[tq:2 * tq, 0:LANES] = jnp.where(lo_half, 0.0, q).astype(bf16)
        qaug_ref[0:tq, LANES:2 * LANES] = jnp.where(lane < XCOLS, qx, 0.0).astype(bf16)
        qaug_ref[tq:2 * tq, LANES:2 * LANES] = jnp.where((lane >= XCOLS) & (lane < 2 * XCOLS), qx, 0.0).astype(bf16)

        _flash_init(m_ref, acc_ref)
        _flash_loop(0, i, logits, values, s_ref, m_ref, acc_ref, last_fix=lambda s: s + tri_ref[...])
        o_ref[0, rows, :] = _flash_pair(acc_ref[0:tq, :], acc_ref[tq:2 * tq, :], lo_half).astype(o_ref.dtype)
        return carry

    lax.fori_loop(0, seq // tq, query_tile, 0)


def _fox(qkv, qx, kx):
    b, _, s, _ = qkv.shape
    t = TK_FOX
    assert t == TQ_FOX
    r = np.arange(2 * TQ_FOX)[:, None] % TQ_FOX
    tri = jnp.asarray(np.where(np.arange(t)[None, :] <= r, 0.0, NEG_INF), f32)
    base = QKV_TILES_FOX
    return pl.pallas_call(
        _fox_kernel,
        grid=(b, FOX_PAIRS),
        in_specs=[pl.BlockSpec((1, 1, s, LANES), lambda bi, p: (bi, base + p, 0, 0)),
                  pl.BlockSpec((1, 1, s, LANES), lambda bi, p: (bi, base + FOX_PAIRS + p, 0, 0)),
                  pl.BlockSpec((1, 1, s, LANES), lambda bi, p: (bi, base + 2 * FOX_PAIRS + p, 0, 0)),
                  pl.BlockSpec((1, 1, s, LANES), lambda bi, p: (bi, p, 0, 0)),
                  pl.BlockSpec((1, 1, s, LANES), lambda bi, p: (bi, p, 0, 0)),
                  pl.BlockSpec((2 * TQ_FOX, t), lambda bi, p: (0, 0))],
        out_specs=pl.BlockSpec((1, s, LANES), lambda bi, p: (bi, 0, p)),
        out_shape=jax.ShapeDtypeStruct((b, s, FOX_W), bf16),
        scratch_shapes=[pltpu.VMEM((2 * TQ_FOX, 2 * LANES), bf16),
                        pltpu.VMEM((s, 2 * LANES), bf16),
                        pltpu.VMEM((2 * TQ_FOX, LANES), f32),
                        pltpu.VMEM((2 * TQ_FOX, LANES), f32),
                        pltpu.VMEM((2, 2 * TQ_FOX, t), f32),
                        pltpu.VMEM((2, s, LANES), bf16)],
        compiler_params=_cparams(("parallel", "parallel")),
        name="fox",
    )(qkv, qkv, qkv, qx, kx, tri)


def _merge_kernel(yn_ref, yf_ref, xb_ref, x_ref, wn_ref, wf_ref, wmg_ref, wo_ref, g_ref, b_ref,
                  xo_ref, xbo_ref):
    mg = _dot(xb_ref[...], wmg_ref[...])
    merged = (_sigmoid(mg[:, 0:D_MODEL]) * _dot(yn_ref[...], wn_ref[...])
              + _sigmoid(mg[:, D_MODEL:2 * D_MODEL]) * _dot(yf_ref[...], wf_ref[...]))
    hmix = _dot(merged.astype(bf16), wo_ref[...])
    xn = _layer_norm(DN_ALPHA * x_ref[...] + hmix, g_ref[...], b_ref[...])
    xo_ref[...] = xn
    xbo_ref[...] = xn.astype(bf16)


def _merge(yn, yf, xb, x, wn, wf, wmg, wo, g, bb):
    m = x.shape[0]
    tm = TM_MERGE
    row = lambda w: pl.BlockSpec((tm, w), lambda i: (i, 0))
    full = lambda a: pl.BlockSpec(a.shape, lambda i: (0, 0))
    return pl.pallas_call(
        _merge_kernel,
        grid=(m // tm,),
        in_specs=[row(NSA_Q_W), row(FOX_W), row(D_MODEL), row(D_MODEL),
                  full(wn), full(wf), full(wmg), full(wo), full(g), full(bb)],
        out_specs=[row(D_MODEL), row(D_MODEL)],
        out_shape=[jax.ShapeDtypeStruct((m, D_MODEL), f32), jax.ShapeDtypeStruct((m, D_MODEL), bf16)],
        compiler_params=_cparams(("parallel",)),
        name="merge",
    )(yn, yf, xb, x, wn, wf, wmg, wo, g, bb)


def _ffn_kernel(xb_ref, x_ref, wg_ref, wu_ref, wd_ref, g_ref, b_ref, xo_ref, xbo_ref, acc_ref, *, nf):
    f = pl.program_id(1)

    @pl.when(f == 0)
    def _():
        acc_ref[...] = jnp.zeros_like(acc_ref)

    xb = xb_ref[...]
    gate = _dot(xb, wg_ref[...])
    up = _dot(xb, wu_ref[...])
    act = (gate * _sigmoid(gate) * up).astype(bf16)
    acc_ref[...] += _dot(act, wd_ref[...])

    @pl.when(f == nf - 1)
    def _():
        xn = _layer_norm(DN_ALPHA * x_ref[...] + acc_ref[...], g_ref[...], b_ref[...])
        xo_ref[...] = xn
        xbo_ref[...] = xn.astype(bf16)


def _ffn(xb, x, wg, wu, wd, g, bb):
    m = x.shape[0]
    dff = wg.shape[1]
    tm, tf = TM_FFN, TF_FFN
    nf = dff // tf
    row = pl.BlockSpec((tm, D_MODEL), lambda i, f: (i, 0))
    vec = pl.BlockSpec((1, D_MODEL), lambda i, f: (0, 0))
    return pl.pallas_call(
        functools.partial(_ffn_kernel, nf=nf),
        grid=(m // tm, nf),
        in_specs=[row, row,
                  pl.BlockSpec((D_MODEL, tf), lambda i, f: (0, f)),
                  pl.BlockSpec((D_MODEL, tf), lambda i, f: (0, f)),
                  pl.BlockSpec((tf, D_MODEL), lambda i, f: (f, 0)),
                  vec, vec],
        out_specs=[row, row],
        out_shape=[jax.ShapeDtypeStruct((m, D_MODEL), f32), jax.ShapeDtypeStruct((m, D_MODEL), bf16)],
        scratch_shapes=[pltpu.VMEM((tm, D_MODEL), f32)],
        compiler_params=_cparams(("parallel", "arbitrary")),
        name="ffn",
    )(xb, x, wg, wu, wd, g, bb)


def _router_kernel(x_ref, r_ref, tri_ref, gate_ref, rank_ref, *, seq):
    tb = tri_ref.shape[0]
    x = x_ref[0]
    xh = x.astype(bf16)
    xl = (x - xh.astype(f32)).astype(bf16)
    r = r_ref[...]
    rh = r.astype(bf16)
    rl = (r - rh.astype(f32)).astype(bf16)
    logits = _dot(xh, rh) + _dot(xh, rl) + _dot(xl, rh)
    lane = lax.broadcasted_iota(i32, (seq, LANES), 1).astype(f32)
    low = -3.0e38
    lg = jnp.where(lane < N_EXPERTS, logits, low)
    m1 = jnp.max(lg, axis=-1, keepdims=True)
    i1 = jnp.min(jnp.where(lg == m1, lane, float(LANES)), axis=-1, keepdims=True)
    lg2 = jnp.where(lane == i1, low, lg)
    m2 = jnp.max(lg2, axis=-1, keepdims=True)
    i2 = jnp.min(jnp.where(lg2 == m2, lane, float(LANES)), axis=-1, keepdims=True)
    e2 = jnp.exp(m2 - m1)
    den = 1.0 + e2
    gate_ref[0] = jnp.where(lane == i1, 1.0 / den, jnp.where(lane == i2, e2 / den, 0.0))
    sel = (lane == i1) | (lane == i2)
    selb = jnp.where(sel, 1.0, 0.0).astype(bf16)
    carry = jnp.zeros((1, LANES), f32)
    for blk in range(seq // tb):
        sl = slice(blk * tb, (blk + 1) * tb)
        c = _dot(tri_ref[...], selb[sl]) + carry
        carry = c[tb - 1:tb, :]
        rank_ref[0, sl, :] = jnp.where(sel[sl], c - 1.0, -1.0)


def _router(x3, router_pad, tri):
    b, s, _ = x3.shape
    return pl.pallas_call(
        functools.partial(_router_kernel, seq=s),
        grid=(b,),
        in_specs=[pl.BlockSpec((1, s, D_MODEL), lambda i: (i, 0, 0)),
                  pl.BlockSpec((D_MODEL, LANES), lambda i: (0, 0)),
                  pl.BlockSpec(tri.shape, lambda i: (0, 0))],
        out_specs=[pl.BlockSpec((1, s, LANES), lambda i: (i, 0, 0)),
                   pl.BlockSpec((1, s, LANES), lambda i: (i, 0, 0))],
        out_shape=[jax.ShapeDtypeStruct((b, s, LANES), f32),
                   jax.ShapeDtypeStruct((b, s, LANES), f32)],
        compiler_params=_cparams(("parallel",)),
        name="router",
    )(x3, router_pad, tri)


def _moe_kernel(tot_ref, tot_al_ref, off_ref, npc_ref, xb_ref, rankt_ref, gatet_ref, wg_ref, wu_ref, wd_ref,
                y_ref, xg_ref, acc_ref, *, seq, nf):
    tr = TR_MOE
    half = tr // 2
    nsubc = seq // SUB_MOE
    c = pl.program_id(0)
    e = pl.program_id(1)
    f = pl.program_id(2)
    ce = c * N_EXPERTS + e
    tot = tot_ref[ce]
    rem = tot % tr
    nfull = tot // tr + jnp.where(rem > half, 1, 0)
    has_tail = (rem > 0) & (rem <= half)
    tail0 = pl.multiple_of(nfull * tr, half)

    def pieces(fn):
        for q in range(nsubc):
            base = off_ref[ce * nsubc + q]

            def body(p, carry, q=q, base=base):
                fn(q, pl.multiple_of(base + p * PIECE_MOE, ROW_ALIGN))
                return carry
            lax.fori_loop(0, npc_ref[ce * nsubc + q], body, 0)

    def onehot_rows(q, r0):
        rk = rankt_ref[0, pl.ds(e, 1), q * SUB_MOE:(q + 1) * SUB_MOE]
        want = (r0 + lax.broadcasted_iota(i32, (PIECE_MOE, SUB_MOE), 0)).astype(f32)
        return rk == want

    def gather(q, r0):
        p = jnp.where(onehot_rows(q, r0), 1.0, 0.0).astype(bf16)
        xg_ref[pl.ds(r0, PIECE_MOE), :] += _dot(p, xb_ref[0, q * SUB_MOE:(q + 1) * SUB_MOE, :]).astype(bf16)

    def clear(r0, nr):
        xg_ref[pl.ds(r0, nr), :] = jnp.zeros((nr, D_MODEL), bf16)
        acc_ref[pl.ds(r0, nr), :] = jnp.zeros((nr, D_MODEL), f32)

    def hidden(r0, nr):
        xs = xg_ref[pl.ds(r0, nr), :]
        gate = _dot(xs, wg_ref[0, 0])
        up = _dot(xs, wu_ref[0, 0])
        act = (gate * _sigmoid(gate) * up).astype(bf16)
        acc_ref[pl.ds(r0, nr), :] += _dot(act, wd_ref[0, 0])

    def combine(q, r0):
        grow = gatet_ref[0, pl.ds(e, 1), q * SUB_MOE:(q + 1) * SUB_MOE]
        hit = onehot_rows(q, r0)
        w = jnp.sum(jnp.where(hit, grow, 0.0), axis=-1, keepdims=True)
        z = (acc_ref[pl.ds(r0, PIECE_MOE), :] * w).astype(bf16)
        p = jnp.where(hit, 1.0, 0.0).astype(bf16)
        y_ref[0, q * SUB_MOE:(q + 1) * SUB_MOE, :] += lax.dot_general(
            p, z, (((0,), (0,)), ((), ())), preferred_element_type=f32)

    def tiles(fn):
        def body(s, carry):
            fn(pl.multiple_of(s * tr, tr), tr)
            return carry
        lax.fori_loop(0, nfull, body, 0)

        @pl.when(has_tail)
        def _():
            fn(tail0, half)

    def zero(j, carry):
        y_ref[0, pl.ds(j * tr, tr), :] = jnp.zeros((tr, D_MODEL), f32)
        return carry

    lax.fori_loop(0, jnp.where((e == 0) & (f == 0), seq // tr, 0), zero, 0)

    @pl.when(f == 0)
    def _():
        tiles(clear)
        clear(pl.multiple_of(tot_al_ref[ce], ROW_ALIGN), tr)
        pieces(gather)

    tiles(hidden)

    @pl.when(f == nf - 1)
    def _():
        pieces(combine)


def _moe(xb3, gate, rank, wg, wu, wd, layer):
    b, s, _ = xb3.shape
    dff = wg.shape[3]
    nf = dff // TF_MOE
    nsubc = s // SUB_MOE
    rk = rank[:, :, :N_EXPERTS]
    cnt = jnp.sum((rk >= 0.0).reshape(b, nsubc, SUB_MOE, N_EXPERTS), axis=2).astype(i32)
    first = jnp.cumsum(cnt, axis=1) - cnt
    off = first // ROW_ALIGN * ROW_ALIGN
    npc = jnp.where(cnt > 0, (first - off + cnt + PIECE_MOE - 1) // PIECE_MOE, 0)
    post = rk.transpose(0, 2, 1)
    gatet = gate[:, :, :N_EXPERTS].transpose(0, 2, 1)
    tot = jnp.sum(cnt, axis=1).reshape(-1)
    tot_al = (tot + ROW_ALIGN - 1) // ROW_ALIGN * ROW_ALIGN
    off_flat = off.transpose(0, 2, 1).reshape(-1)
    npc = npc.transpose(0, 2, 1).reshape(-1)
    buf_rows = s + ROW_ALIGN + PIECE_MOE + TR_MOE
    grid_spec = pltpu.PrefetchScalarGridSpec(
        num_scalar_prefetch=4,
        grid=(b, N_EXPERTS, nf),
        in_specs=[pl.BlockSpec((1, s, D_MODEL), lambda c, e, f, *_: (c, 0, 0)),
                  pl.BlockSpec((1, N_EXPERTS, s), lambda c, e, f, *_: (c, 0, 0)),
                  pl.BlockSpec((1, N_EXPERTS, s), lambda c, e, f, *_: (c, 0, 0)),
                  pl.BlockSpec((1, 1, D_MODEL, TF_MOE), lambda c, e, f, *_: (layer, e, 0, f)),
                  pl.BlockSpec((1, 1, D_MODEL, TF_MOE), lambda c, e, f, *_: (layer, e, 0, f)),
                  pl.BlockSpec((1, 1, TF_MOE, D_MODEL), lambda c, e, f, *_: (layer, e, f, 0))],
        out_specs=pl.BlockSpec((1, s, D_MODEL), lambda c, e, f, *_: (c, 0, 0)),
        scratch_shapes=[pltpu.VMEM((buf_rows, D_MODEL), bf16), pltpu.VMEM((buf_rows, D_MODEL), f32)],
    )
    return pl.pallas_call(
        functools.partial(_moe_kernel, seq=s, nf=nf),
        grid_spec=grid_spec,
        out_shape=jax.ShapeDtypeStruct((b, s, D_MODEL), f32),
        compiler_params=_cparams(("parallel", "arbitrary", "arbitrary")),
        name="moe",
    )(tot, tot_al, off_flat, npc, xb3, post, gatet, wg, wu, wd)


def _resln_kernel(x_ref, y_ref, g_ref, b_ref, xo_ref, xbo_ref):
    xn = _layer_norm(DN_ALPHA * x_ref[...] + y_ref[...], g_ref[...], b_ref[...])
    xo_ref[...] = xn
    xbo_ref[...] = xn.astype(bf16)


def _resln(x, y, g, bb):
    m = x.shape[0]
    tm = TM_FFN
    row = pl.BlockSpec((tm, D_MODEL), lambda i: (i, 0))
    vec = pl.BlockSpec((1, D_MODEL), lambda i: (0, 0))
    return pl.pallas_call(
        _resln_kernel,
        grid=(m // tm,),
        in_specs=[row, row, vec, vec],
        out_specs=[row, row],
        out_shape=[jax.ShapeDtypeStruct((m, D_MODEL), f32), jax.ShapeDtypeStruct((m, D_MODEL), bf16)],
        compiler_params=_cparams(("parallel",)),
        name="resln",
    )(x, y, g, bb)


def _t5_bucket(dist):
    n = jnp.maximum(dist, 0)
    max_exact = REL_BUCKETS // 2
    large = max_exact + (jnp.log(jnp.maximum(n, 1).astype(f32) / max_exact)
                         / math.log(REL_MAX_DIST / max_exact) * (REL_BUCKETS - max_exact)).astype(i32)
    large = jnp.minimum(large, REL_BUCKETS - 1)
    return jnp.where(n < max_exact, n, large)


def _bias_of_dist(rel_bias, dist):
    onehot = (_t5_bucket(dist)[None] == jnp.arange(REL_BUCKETS).reshape((-1,) + (1,) * dist.ndim)).astype(f32)
    return LOG2E * jnp.einsum("kh,k...->h...", rel_bias.astype(f32), onehot, precision=lax.Precision.HIGHEST)


def _bias_tables(rel_bias, seq):
    t = T_ATT
    rows = NSA_GROUP * t
    ncmp = seq // CMP_STRIDE
    d0 = jnp.arange(t)[:, None] - jnp.arange(t)[None, :]
    offs = jnp.array([0, t, 2 * t, WINDOW]).reshape(4, 1, 1)
    kinds = _bias_of_dist(rel_bias, offs + d0[None])
    mask = jnp.stack([d0 >= 0, d0 == d0, d0 == d0, d0 < 0])
    kinds = jnp.where(mask[None], kinds, NEG_INF)
    order = np.array(NSA_ROW_ORDER)
    tbias = kinds.reshape(NSA_KV_HEADS, NSA_GROUP, 4, t, t)[:, order].transpose(0, 2, 1, 3, 4)
    tbias = tbias.reshape(NSA_KV_HEADS, 4, rows, t)
    cend = jnp.arange(ncmp) * CMP_STRIDE + CMP_BLOCK - 1
    dist_c = jnp.arange(seq)[:, None] - cend[None, :]
    bc = jnp.where(dist_c >= 0, _bias_of_dist(rel_bias, dist_c), NEG_INF)
    bc = bc.reshape(NSA_KV_HEADS, NSA_GROUP, seq // t, t, ncmp)[:, order].transpose(0, 2, 4, 1, 3)
    return tbias, bc.reshape(NSA_KV_HEADS, (seq // t) * ncmp, rows)


def _selection_constants(seq):
    ncmp = seq // CMP_STRIDE
    nsel = seq // SEL_BLOCK
    c0 = np.arange(ncmp)[:, None] * CMP_STRIDE
    s0 = np.arange(LANES)[None, :] * SEL_BLOCK
    ov = np.maximum(np.minimum(c0 + CMP_BLOCK, s0 + SEL_BLOCK) - np.maximum(c0, s0), 0) / CMP_BLOCK
    ov[ncmp - 1, :] = 0.0
    ov[:, nsel:] = 0.0
    et = (np.arange(seq)[:, None] // SEL_BLOCK == np.arange(LANES)[None, :]).astype(np.float32)
    eg = np.zeros((LANES, 3 * (NSA_GROUP // 2) * LANES), np.float32)
    for br in range(3):
        for g in range(NSA_GROUP):
            c0 = (br * (NSA_GROUP // 2) + g // 2) * LANES + (g % 2) * HEAD_DIM
            eg[br * NSA_GROUP + g, c0:c0 + HEAD_DIM] = 1.0
    return jnp.asarray(ov.T, bf16), jnp.asarray(et, bf16), jnp.asarray(eg, bf16)


def _fox_placement():
    xw = FOX_PAIRS * LANES
    pq = np.zeros((3, LANES, xw), np.float32)
    pk = np.zeros((3, LANES, xw), np.float32)
    oq = np.zeros((1, xw), np.float32)
    ok = np.zeros((1, xw), np.float32)
    for p in range(FOX_PAIRS):
        for hh in range(2):
            src = FGATE_LANE + 2 * p + hh
            base = p * LANES + hh * XCOLS
            for part in range(3):
                pk[part, src, base + part] = -1.0
                pq[part, src, base + 3 + part] = 1.0
                oq[0, base + part] = 1.0
                ok[0, base + 3 + part] = 1.0
    return jnp.asarray(pq, bf16), jnp.asarray(pk, bf16), jnp.asarray(oq), jnp.asarray(ok)


def _layer_weights(w_in, layer_pe, w1, w2, f_bias):
    offs = np.cumsum((NSA_Q_W, 6 * 2 * HEAD_DIM, 3 * NSA_HEADS, 3 * FOX_W, FOX_HEADS, 2 * D_MODEL))
    kv0, g0, fx0, ff0, mg0 = offs[0], offs[1], offs[2], offs[3], offs[4]
    scale = HEAD_DIM ** -0.5 * LOG2E
    kvw = NSA_KV_HEADS * HEAD_DIM
    w_kv = w_in[:, kv0:g0]
    w_kvdup = jnp.repeat(w_kv[:, 2 * kvw:].reshape(D_MODEL, 4 * NSA_KV_HEADS, 1, HEAD_DIM), 2, axis=2)
    w_kvdup = w_kvdup.reshape(D_MODEL, 8 * kvw)
    w_qkv = jnp.concatenate([w_in[:, :NSA_Q_W] * scale, w_kvdup, w_in[:, fx0:fx0 + FOX_W] * scale,
                             w_in[:, fx0 + FOX_W:ff0]], axis=1).astype(bf16)
    zeros = lambda n: jnp.zeros((D_MODEL, n), w_in.dtype)
    ng = 3 * NSA_GROUP
    w_g = w_in[:, g0:fx0].reshape(D_MODEL, NSA_KV_HEADS, NSA_GROUP, 3).transpose(0, 1, 3, 2)
    gate_cols = lambda h: w_g[:, h].reshape(D_MODEL, ng)
    w_aux = jnp.concatenate([w_kv[:, :2 * kvw],
                             gate_cols(0), zeros(FGATE_LANE - ng), w_in[:, ff0:mg0],
                             zeros(LANES - FGATE_LANE - FOX_HEADS),
                             gate_cols(1), zeros(LANES - ng)], axis=1).astype(bf16)
    fb_row = jnp.zeros((1, LANES), f32).at[0, FGATE_LANE:FGATE_LANE + FOX_HEADS].set(f_bias.astype(f32))
    pe2 = jnp.tile(layer_pe.astype(f32), (1, 1, NSA_KV_HEADS))
    w1r = w1.reshape(2, CMP_BLOCK, HEAD_DIM, CMP_HIDDEN).astype(bf16)
    zero = jnp.zeros_like(w1r)
    w1bd = jnp.concatenate([jnp.concatenate([w1r, zero], axis=-1),
                            jnp.concatenate([zero, w1r], axis=-1)], axis=-2)
    w2d = jnp.concatenate([w2, w2], axis=-1).astype(bf16)
    return (w_qkv, w_aux, w_in[:, mg0:].astype(bf16), fb_row, pe2, w1bd[:, :CMP_STRIDE], w1bd[:, CMP_STRIDE:],
            w2d)


def kernel(x, w_in, nsa_cmp_pe, nsa_cmp_w1, nsa_cmp_w2, fox_f_bias, w_nsa_branch, w_fox_branch, w_out,
           rel_bias, ln1_g, ln1_b, ln2_g, ln2_b, dense_w_gate, dense_w_up, dense_w_down, moe_router,
           moe_w_gate, moe_w_up, moe_w_down):
    b, s, d = x.shape
    m = b * s
    tbias, bias_c = _bias_tables(rel_bias, s)
    ov, et, eg = _selection_constants(s)
    pq, pk, oq, ok = _fox_placement()
    tri128 = jnp.asarray(np.tril(np.ones((LANES, LANES), np.float32)), bf16)
    tri256 = jnp.asarray(np.tril(np.ones((256, 256), np.float32)), bf16)

    moe_wg, moe_wu, moe_wd = moe_w_gate.astype(bf16), moe_w_up.astype(bf16), moe_w_down.astype(bf16)
    xf = x.reshape(m, d).astype(f32)
    xb = xf.astype(bf16)
    for layer in range(DEPTH):
        w_qkv, w_aux, w_mg, fb_row, pe2, w1t, w1b, w2d = _layer_weights(
            w_in[layer], nsa_cmp_pe[layer], nsa_cmp_w1[layer], nsa_cmp_w2[layer], fox_f_bias[layer])
        qkv, aux = _proj(xb, w_qkv, w_aux, b)
        aux = aux.reshape(b, s, AUX_W)
        gates, qx, kx = _gateprep(aux, fb_row, tri128, pq, pk, oq, ok)
        cmpkv = _compress(aux, pe2, w1t, w1b, w2d)
        y_nsa = _nsa(qkv, cmpkv, gates, bias_c, tbias, ov, et, eg).reshape(m, NSA_Q_W)
        y_fox = _fox(qkv, qx, kx).reshape(m, FOX_W)
        xf, xb = _merge(y_nsa, y_fox, xb, xf, w_nsa_branch[layer].astype(bf16),
                        w_fox_branch[layer].astype(bf16), w_mg, w_out[layer].astype(bf16),
                        ln1_g[layer].reshape(1, d), ln1_b[layer].reshape(1, d))
        j = layer // 2
        g2, b2 = ln2_g[layer].reshape(1, d), ln2_b[layer].reshape(1, d)
        if layer % 2 == 0:
            xf, xb = _ffn(xb, xf, dense_w_gate[j].astype(bf16), dense_w_up[j].astype(bf16),
                          dense_w_down[j].astype(bf16), g2, b2)
        else:
            router_pad = jnp.zeros((d, LANES), f32).at[:, :N_EXPERTS].set(moe_router[j].astype(f32))
            gate, rank = _router(xf.reshape(b, s, d), router_pad, tri256)
            y = _moe(xb.reshape(b, s, d), gate, rank, moe_wg, moe_wu, moe_wd, j)
            xf, xb = _resln(xf, y.reshape(m, d), g2, b2)
    return xf.reshape(b, s, d).astype(x.dtype)
```

```python
import functools
import math

import numpy as np
import jax
import jax.numpy as jnp
from jax import lax
from jax.experimental import pallas as pl
from jax.experimental.pallas import tpu as pltpu

f32 = jnp.float32
bf16 = jnp.bfloat16
i32 = jnp.int32

D_MODEL = 1024
HEAD_DIM = 64
LANES = 128
NSA_HEADS = 8
NSA_KV_HEADS = 2
NSA_GROUP = NSA_HEADS // NSA_KV_HEADS
NSA_ROW_ORDER = (0, 2, 1, 3)
FOX_HEADS = 8
FOX_PAIRS = FOX_HEADS // 2
CMP_BLOCK = 32
CMP_STRIDE = 16
CMP_HIDDEN = 128
SEL_BLOCK = 64
N_SELECT = 16
N_LOCAL_SEL = 2
WINDOW = 512
REL_BUCKETS = 32
REL_MAX_DIST = 128
N_EXPERTS = 8
DEPTH = 4
DN_ALPHA = (2 * DEPTH) ** 0.25
LN_EPS = 1e-5
FORCED_SCORE = 1e4
NEG_INF = -1e30
LOG2E = math.log2(math.e)

NSA_Q_W = NSA_HEADS * HEAD_DIM
FOX_W = FOX_HEADS * HEAD_DIM
QKV_TILES_KV = NSA_Q_W // LANES
QKV_TILES_FOX = QKV_TILES_KV + 4 * NSA_KV_HEADS
QKV_W = (QKV_TILES_FOX + 3 * FOX_PAIRS) * LANES
AUX_W = 4 * LANES
FGATE_LANE = 24
XCOLS = 6

T_ATT = 256
TQ_FOX = 512
TK_FOX = 512
TM_PROJ = 1024
TM_MERGE = 1024
TM_FFN = 512
TF_FFN = 2816
TR_MOE = 256
SUB_MOE = 512
PIECE_MOE = 192
ROW_ALIGN = 16
TF_MOE = 896
VMEM_LIMIT = 56 * 1024 * 1024


def _cparams(sem):
    return pltpu.CompilerParams(dimension_semantics=sem, vmem_limit_bytes=VMEM_LIMIT)


def _dot(a, b):
    return jnp.dot(a, b, preferred_element_type=f32)


def _dot_nt(a, b):
    return lax.dot_general(a, b, (((1,), (1,)), ((), ())), preferred_element_type=f32)


def _sigmoid(x):
    return 1.0 / (1.0 + jnp.exp(-x))


def _layer_norm(z, g, b):
    mu = jnp.mean(z, axis=-1, keepdims=True)
    zc = z - mu
    var = jnp.mean(zc * zc, axis=-1, keepdims=True)
    return zc * lax.rsqrt(var + LN_EPS) * g + b


def _split3(x):
    hi = x.astype(bf16)
    r1 = x - hi.astype(f32)
    mid = r1.astype(bf16)
    lo = (r1 - mid.astype(f32)).astype(bf16)
    return hi, mid, lo


def _proj_kernel(x_ref, w_ref, wa_ref, o_ref, oa_ref):
    x = x_ref[...]
    res = _dot(x, w_ref[...])
    for j in range(o_ref.shape[1]):
        o_ref[0, j] = res[:, j * LANES:(j + 1) * LANES].astype(o_ref.dtype)
    oa_ref[...] = _dot(x, wa_ref[...])


def _proj(xb, w, wa, batch):
    m, k = xb.shape
    n, na = w.shape[1], wa.shape[1]
    nb = m // batch // TM_PROJ
    return pl.pallas_call(
        _proj_kernel,
        grid=(m // TM_PROJ,),
        in_specs=[pl.BlockSpec((TM_PROJ, k), lambda i: (i, 0)),
                  pl.BlockSpec((k, n), lambda i: (0, 0)),
                  pl.BlockSpec((k, na), lambda i: (0, 0))],
        out_specs=[pl.BlockSpec((1, n // LANES, TM_PROJ, LANES), lambda i: (i // nb, 0, i % nb, 0)),
                   pl.BlockSpec((TM_PROJ, na), lambda i: (i, 0))],
        out_shape=[jax.ShapeDtypeStruct((batch, n // LANES, m // batch, LANES), bf16),
                   jax.ShapeDtypeStruct((m, na), f32)],
        compiler_params=_cparams(("parallel",)),
        name="proj",
    )(xb, w, wa)


def _gateprep_kernel(a_ref, fb_ref, tri_ref, pq_ref, pk_ref, oq_ref, ok_ref, g_ref, qx_ref, kx_ref, *, seq):
    tb = LANES
    tri = tri_ref[...]
    local = []
    for blk in range(seq // tb):
        sl = slice(blk * tb, (blk + 1) * tb)
        va = a_ref[0, sl, 0:LANES]
        g_ref[0, 0, sl, :] = _sigmoid(va)
        g_ref[0, 1, sl, :] = _sigmoid(a_ref[0, sl, LANES:2 * LANES])
        z = va + fb_ref[...]
        logf = jnp.minimum(z, 0.0) - jnp.log1p(jnp.exp(-jnp.abs(z)))
        hi, mid, lo = _split3(logf)
        local.append(_dot(tri, hi) + _dot(tri, mid) + _dot(tri, lo))
    carry = jnp.zeros((1, LANES), f32)
    for blk in range(seq // tb):
        sl = slice(blk * tb, (blk + 1) * tb)
        c = local[blk] + carry
        carry = c[tb - 1:tb, :]
        chi, cmid, clo = _split3(c * LOG2E)
        qx = _dot(chi, pq_ref[0]) + _dot(cmid, pq_ref[1]) + _dot(clo, pq_ref[2]) + oq_ref[...]
        kx = _dot(chi, pk_ref[0]) + _dot(cmid, pk_ref[1]) + _dot(clo, pk_ref[2]) + ok_ref[...]
        for p in range(FOX_PAIRS):
            qx_ref[0, p, sl, :] = qx[:, p * LANES:(p + 1) * LANES].astype(bf16)
            kx_ref[0, p, sl, :] = kx[:, p * LANES:(p + 1) * LANES].astype(bf16)


def _gateprep(aux, fb_row, tri, pq, pk, oq, ok):
    b, s, _ = aux.shape
    const2 = lambda a: pl.BlockSpec(a.shape, lambda i: (0, 0))
    const3 = lambda a: pl.BlockSpec(a.shape, lambda i: (0, 0, 0))
    return pl.pallas_call(
        functools.partial(_gateprep_kernel, seq=s),
        grid=(b,),
        in_specs=[pl.BlockSpec((1, s, 2 * LANES), lambda i: (i, 0, 1)),
                  const2(fb_row), const2(tri), const3(pq), const3(pk), const2(oq), const2(ok)],
        out_specs=[pl.BlockSpec((1, 2, s, LANES), lambda i: (i, 0, 0, 0)),
                   pl.BlockSpec((1, FOX_PAIRS, s, LANES), lambda i: (i, 0, 0, 0)),
                   pl.BlockSpec((1, FOX_PAIRS, s, LANES), lambda i: (i, 0, 0, 0))],
        out_shape=[jax.ShapeDtypeStruct((b, 2, s, LANES), f32),
                   jax.ShapeDtypeStruct((b, FOX_PAIRS, s, LANES), bf16),
                   jax.ShapeDtypeStruct((b, FOX_PAIRS, s, LANES), bf16)],
        compiler_params=_cparams(("parallel",)),
        name="gateprep",
    )(aux, fb_row, tri, pq, pk, oq, ok)


def _gelu_tanh(x):
    c = math.sqrt(2.0 / math.pi)
    return x * (0.5 * (1.0 + jnp.tanh(c * (x + 0.044715 * (x * x * x)))))


def _compress_kernel(a_ref, pe_ref, w1t_ref, w1b_ref, w2_ref, o_ref, *, nhalf):
    top = jnp.zeros((nhalf, NSA_KV_HEADS * CMP_HIDDEN), f32)
    bot = jnp.zeros((nhalf, NSA_KV_HEADS * CMP_HIDDEN), f32)
    for l in range(CMP_STRIDE):
        rows = a_ref[0, pl.ds(l, nhalf, stride=CMP_STRIDE), :]
        top = top + _dot((rows + pe_ref[0, l:l + 1, :]).astype(bf16), w1t_ref[0, l])
        bot = bot + _dot((rows + pe_ref[0, CMP_STRIDE + l:CMP_STRIDE + l + 1, :]).astype(bf16), w1b_ref[0, l])
    pre = top + pltpu.roll(bot, nhalf - 1, 0)
    act = _gelu_tanh(pre).astype(bf16)
    for h in range(NSA_KV_HEADS):
        o_ref[0, 0, h] = _dot(act[:, h * CMP_HIDDEN:(h + 1) * CMP_HIDDEN], w2_ref[0]).astype(o_ref.dtype)


def _compress(aux, pe2, w1t, w1b, w2d):
    b, s, _ = aux.shape
    nhalf = s // CMP_STRIDE
    w1_spec = pl.BlockSpec((1,) + w1t.shape[1:], lambda i, j: (j, 0, 0, 0))
    return pl.pallas_call(
        functools.partial(_compress_kernel, nhalf=nhalf),
        grid=(b, 2),
        in_specs=[pl.BlockSpec((1, s, LANES), lambda i, j: (i, 0, j)),
                  pl.BlockSpec((1, CMP_BLOCK, LANES), lambda i, j: (j, 0, 0)),
                  w1_spec, w1_spec,
                  pl.BlockSpec((1, CMP_HIDDEN, LANES), lambda i, j: (j, 0, 0))],
        out_specs=pl.BlockSpec((1, 1, NSA_KV_HEADS, nhalf, LANES), lambda i, j: (i, j, 0, 0, 0)),
        out_shape=jax.ShapeDtypeStruct((b, 2, NSA_KV_HEADS, nhalf, LANES), bf16),
        compiler_params=_cparams(("parallel", "parallel")),
        name="compress",
    )(aux, pe2, w1t, w1b, w2d)


def _flash_init(m_ref, acc_ref):
    m_ref[...] = jnp.full(m_ref.shape, NEG_INF, f32)
    acc_ref[...] = jnp.zeros(acc_ref.shape, f32)


def _ones_values(v):
    lo_half = lax.broadcasted_iota(i32, v.shape, 1) < HEAD_DIM
    vf = v.astype(f32)
    return jnp.where(lo_half, vf, 1.0).astype(bf16), jnp.where(lo_half, 1.0, vf).astype(bf16)


def _flash_step(s, v_lo, v_hi, m_ref, acc_ref):
    nk = s.shape[1] // LANES
    half = s.shape[0] // 2
    cols = [s[:, c * LANES:(c + 1) * LANES] for c in range(nk)]
    mx = cols[0]
    for c in cols[1:]:
        mx = jnp.maximum(mx, c)
    m_old = m_ref[...]
    m_new = jnp.maximum(m_old, jnp.broadcast_to(jnp.max(mx, axis=-1, keepdims=True), m_old.shape))
    alpha = jnp.exp2(m_old - m_new)
    p = jnp.concatenate([jnp.exp2(c - m_new).astype(bf16) for c in cols], axis=1)
    pv = jnp.concatenate([_dot(p[0:half], v_lo), _dot(p[half:], v_hi)], axis=0)
    acc_ref[...] = alpha * acc_ref[...] + pv
    m_ref[...] = m_new


def _flash_loop(first, last, logits_fn, v_fn, s_ref, m_ref, acc_ref, last_fix=None):
    sa, sb = s_ref.at[0], s_ref.at[1]
    n = last - first + 1
    pairs = (n - 1) // 2

    def step(buf, kt, fix=None):
        s = buf[...] if fix is None else fix(buf[...])
        _flash_step(s, *v_fn(kt), m_ref, acc_ref)

    sa[...] = logits_fn(first)

    def body(j, carry):
        kt = first + 2 * j
        sb[...] = logits_fn(kt + 1)
        step(sa, kt)
        sa[...] = logits_fn(kt + 2)
        step(sb, kt + 1)
        return carry

    lax.fori_loop(0, pairs, body, 0)
    two_left = n - 2 * pairs == 2

    @pl.when(two_left)
    def _():
        sb[...] = logits_fn(last)
        step(sa, last - 1)
        step(sb, last, last_fix)

    @pl.when(jnp.logical_not(two_left))
    def _():
        step(sa, last, last_fix)


def _flash_pair(acc_lo, acc_hi, lo_half):
    num = jnp.where(lo_half, acc_lo, acc_hi)
    den = pltpu.roll(jnp.where(lo_half, acc_hi, acc_lo), HEAD_DIM, 1)
    return num * (1.0 / jnp.maximum(den, 1e-30))


def _nsa_kernel(q_ref, ks_ref, vs_ref, kw_ref, vw_ref, kc_ref, vc_ref, gate_ref, bc_ref, tb_ref, ov_ref,
                et_ref, eg_ref, o_ref, qaug_ref, kaug_ref, m_ref, acc_ref, os_ref, s_ref, vsel_ref,
                vwin_ref, *, seq, ncmp):
    kaug_ref[:, 0:LANES] = ks_ref[0, 0]
    kaug_ref[:, LANES:2 * LANES] = et_ref[...]
    vsel_ref[0], vsel_ref[1] = _ones_values(vs_ref[0, 0])
    vwin_ref[0], vwin_ref[1] = _ones_values(vw_ref[0, 0])

    def query_tile(i, carry):
        _nsa_tile(i, q_ref, kw_ref, kc_ref, vc_ref, gate_ref, bc_ref, tb_ref, ov_ref, eg_ref, o_ref, qaug_ref,
                  kaug_ref, m_ref, acc_ref, os_ref, s_ref, vsel_ref, vwin_ref, seq=seq, ncmp=ncmp)
        return carry

    lax.fori_loop(0, seq // T_ATT, query_tile, 0)


def _nsa_tile(i, q_ref, kw_ref, kc_ref, vc_ref, gate_ref, bc_ref, tb_ref, ov_ref, eg_ref, o_ref, qaug_ref,
              kaug_ref, m_ref, acc_ref, os_ref, s_ref, vsel_ref, vwin_ref, *, seq, ncmp):
    t = T_ATT
    rows = NSA_GROUP * t
    nsel = seq // SEL_BLOCK
    topn = min(N_SELECT, nsel)
    t0 = i * t
    tile_rows = pl.ds(pl.multiple_of(t0, t), t)
    lane = lax.broadcasted_iota(i32, (t, LANES), 1)
    lo_half = lane < HEAD_DIM

    for rb, g in enumerate(NSA_ROW_ORDER):
        qg = q_ref[0, g // 2, tile_rows, :].astype(f32)
        keep = lo_half if g % 2 == 0 else jnp.logical_not(lo_half)
        qaug_ref[rb * t:(rb + 1) * t, 0:LANES] = jnp.where(keep, qg, 0.0).astype(bf16)
    qs = qaug_ref[:, 0:LANES]

    s = _dot_nt(kc_ref[0, 0, 0], qs) + bc_ref[0, pl.ds(pl.multiple_of(i * ncmp, ncmp), ncmp), :]
    sees_any = t0 + (lax.broadcasted_iota(i32, (1, rows), 1) & (t - 1)) >= CMP_BLOCK - 1
    s = s - jnp.max(s, axis=0, keepdims=True)
    e = jnp.where(sees_any, jnp.exp2(s), 0.0)
    p_c = e * (1.0 / jnp.maximum(jnp.sum(e, axis=0, keepdims=True), 1e-30))
    o_c = lax.dot_general(p_c.astype(bf16), vc_ref[0, 0, 0], (((0,), (0,)), ((), ())),
                          preferred_element_type=f32)

    psum = p_c[:, 0:t]
    for g in range(1, NSA_GROUP):
        psum = psum + p_c[:, g * t:(g + 1) * t]
    hi = psum.astype(bf16)
    lo = (psum - hi.astype(f32)).astype(bf16)
    imp = (_dot(ov_ref[...], hi) + _dot(ov_ref[...], lo))[0:nsel]
    blk = lax.broadcasted_iota(i32, (nsel, t), 0)
    cur = (t0 + lax.broadcasted_iota(i32, (nsel, t), 1)) >> 6
    forced = (blk == 0) | ((blk <= cur) & (blk > cur - N_LOCAL_SEL))
    val = jnp.where(forced, FORCED_SCORE, jnp.where(blk <= cur, imp, -1.0))
    ngrp = nsel // 8
    sub = lax.broadcasted_iota(i32, (8, t), 0)
    vals = [val[8 * r:8 * r + 8, :] for r in range(ngrp)]
    cnts = [jnp.zeros((8, t), f32) for _ in range(ngrp)]
    for j in range(nsel):
        vj = val[j:j + 1, :]
        for r in range(ngrp):
            ahead = jnp.where(vj > vals[r], 1.0, 0.0)
            ahead_or_tied = jnp.where(vj >= vals[r], 1.0, 0.0)
            if r < j // 8:
                cnts[r] = cnts[r] + ahead
            elif r > j // 8:
                cnts[r] = cnts[r] + ahead_or_tied
            else:
                cnts[r] = cnts[r] + jnp.where(sub > j % 8, ahead_or_tied, ahead)
    cnt = jnp.concatenate(cnts, axis=0)
    mneg = jnp.where(cnt < topn, 0.0, NEG_INF)
    mneg = jnp.concatenate([mneg, jnp.zeros((LANES - nsel, t), f32)], axis=0).T.astype(bf16)
    for g in range(NSA_GROUP):
        qaug_ref[g * t:(g + 1) * t, LANES:2 * LANES] = mneg

    _flash_init(m_ref, os_ref)

    def sel_logits(kt):
        k = kaug_ref[pl.ds(kt * t, t), :]
        return _dot_nt(qaug_ref[...], k) + tb_ref[0, jnp.minimum(i - kt, 2)]

    def values(v_ref):
        return lambda kt: (v_ref[0, pl.ds(kt * t, t), :], v_ref[1, pl.ds(kt * t, t), :])

    _flash_loop(0, i, sel_logits, values(vsel_ref), s_ref, m_ref, os_ref)

    _flash_init(m_ref, acc_ref)
    nwin = WINDOW // t

    def win_logits(kt):
        d = i - kt
        kind = jnp.where(d == nwin, 3, d)
        return _dot_nt(qs, kw_ref[0, 0, pl.ds(kt * t, t), :]) + tb_ref[0, kind]

    _flash_loop(jnp.maximum(i - nwin, 0), i, win_logits, values(vwin_ref), s_ref, m_ref, acc_ref)

    ghi, gmid, glo = _split3(gate_ref[0, 0, tile_rows, :])
    gexp = _dot(ghi, eg_ref[...]) + _dot(gmid, eg_ref[...]) + _dot(glo, eg_ref[...])
    for j in range(NSA_GROUP // 2):
        ra, rb = NSA_ROW_ORDER.index(2 * j), NSA_ROW_ORDER.index(2 * j + 1)
        out = jnp.zeros((t, LANES), f32)
        sa, sb = slice(ra * t, (ra + 1) * t), slice(rb * t, (rb + 1) * t)
        pairs = (jnp.where(lo_half, o_c[sa], o_c[sb]),
                 _flash_pair(os_ref[sa, :], os_ref[sb, :], lo_half),
                 _flash_pair(acc_ref[sa, :], acc_ref[sb, :], lo_half))
        for br, o_pair in enumerate(pairs):
            col = (br * (NSA_GROUP // 2) + j) * LANES
            out = out + gexp[:, col:col + LANES] * o_pair
        o_ref[0, tile_rows, j * LANES:(j + 1) * LANES] = out.astype(o_ref.dtype)


def _nsa(qkv, cmpkv, gates, bias_c, tbias, ov, et, eg):
    b, _, s, _ = qkv.shape
    ncmp = cmpkv.shape[3]
    t = T_ATT
    rows = NSA_GROUP * t
    qw = NSA_GROUP * HEAD_DIM
    qtiles = qw // LANES
    kv_spec = lambda col: pl.BlockSpec((1, 1, s, LANES), lambda h, bi, col=col: (bi, col + h, 0, 0))
    cmp_spec = lambda kv: pl.BlockSpec((1, 1, 1, ncmp, LANES), lambda h, bi, kv=kv: (bi, kv, h, 0, 0))
    base = QKV_TILES_KV
    return pl.pallas_call(
        functools.partial(_nsa_kernel, seq=s, ncmp=ncmp),
        grid=(NSA_KV_HEADS, b),
        in_specs=[pl.BlockSpec((1, qtiles, s, LANES), lambda h, bi: (bi, h, 0, 0)),
                  kv_spec(base), kv_spec(base + 2), kv_spec(base + 4), kv_spec(base + 6),
                  cmp_spec(0), cmp_spec(1),
                  pl.BlockSpec((1, 1, s, LANES), lambda h, bi: (bi, h, 0, 0)),
                  pl.BlockSpec((1, (s // t) * ncmp, rows), lambda h, bi: (h, 0, 0)),
                  pl.BlockSpec((1, 4, rows, t), lambda h, bi: (h, 0, 0, 0)),
                  pl.BlockSpec((LANES, ncmp), lambda h, bi: (0, 0)),
                  pl.BlockSpec((s, LANES), lambda h, bi: (0, 0)),
                  pl.BlockSpec(eg.shape, lambda h, bi: (0, 0))],
        out_specs=pl.BlockSpec((1, s, qw), lambda h, bi: (bi, 0, h)),
        out_shape=jax.ShapeDtypeStruct((b, s, NSA_Q_W), bf16),
        scratch_shapes=[pltpu.VMEM((rows, 2 * LANES), bf16),
                        pltpu.VMEM((s, 2 * LANES), bf16),
                        pltpu.VMEM((rows, LANES), f32),
                        pltpu.VMEM((rows, LANES), f32),
                        pltpu.VMEM((rows, LANES), f32),
                        pltpu.VMEM((2, rows, t), f32),
                        pltpu.VMEM((2, s, LANES), bf16),
                        pltpu.VMEM((2, s, LANES), bf16)],
        compiler_params=_cparams(("parallel", "parallel")),
        name="nsa",
    )(qkv, qkv, qkv, qkv, qkv, cmpkv, cmpkv, gates, bias_c, tbias, ov, et, eg)


def _fox_kernel(q_ref, k_ref, v_ref, qx_ref, kx_ref, tri_ref, o_ref, qaug_ref, kaug_ref, m_ref, acc_ref, s_ref,
                vaug_ref):
    t = TK_FOX
    tq = TQ_FOX
    seq = k_ref.shape[2]
    lane = lax.broadcasted_iota(i32, (tq, LANES), 1)
    lo_half = lane < HEAD_DIM
    kaug_ref[:, 0:LANES] = k_ref[0, 0]
    kaug_ref[:, LANES:2 * LANES] = kx_ref[0, 0]
    vaug_ref[0], vaug_ref[1] = _ones_values(v_ref[0, 0])

    def logits(kt):
        return _dot_nt(qaug_ref[...], kaug_ref[pl.ds(kt * t, t), :])

    def values(kt):
        return vaug_ref[0, pl.ds(kt * t, t), :], vaug_ref[1, pl.ds(kt * t, t), :]

    def query_tile(i, carry):
        rows = pl.ds(pl.multiple_of(i * tq, tq), tq)
        q = q_ref[0, 0, rows, :].astype(f32)
        qx = qx_ref[0, 0, rows, :].astype(f32)
        qaug_ref[0:tq, 0:LANES] = jnp.where(lo_half, q, 0.0).astype(bf16)
        qaug_ref[tq:2 * tq, 0:LANES] = jnp.where(lo_half, 0.0, q).astype(bf16)
        qaug_ref[0:tq, LANES:2 * LANES] = jnp.where(lane < XCOLS, qx, 0.0).astype(bf16)
        qaug_ref[tq:2 * tq, LANES:2 * LANES] = jnp.where((lane >= XCOLS) & (lane < 2 * XCOLS), qx, 0.0).astype(bf16)

        _flash_init(m_ref, acc_ref)
        _flash_loop(0, i, logits, values, s_ref, m_ref, acc_ref, last_fix=lambda s: s + tri_ref[...])
        o_ref[0, rows, :] = _flash_pair(acc_ref[0:tq, :], acc_ref[tq:2 * tq, :], lo_half).astype(o_ref.dtype)
        return carry

    lax.fori_loop(0, seq // tq, query_tile, 0)


def _fox(qkv, qx, kx):
    b, _, s, _ = qkv.shape
    t = TK_FOX
    assert t == TQ_FOX
    r = np.arange(2 * TQ_FOX)[:, None] % TQ_FOX
    tri = jnp.asarray(np.where(np.arange(t)[None, :] <= r, 0.0, NEG_INF), f32)
    base = QKV_TILES_FOX
    return pl.pallas_call(
        _fox_kernel,
        grid=(b, FOX_PAIRS),
        in_specs=[pl.BlockSpec((1, 1, s, LANES), lambda bi, p: (bi, base + p, 0, 0)),
                  pl.BlockSpec((1, 1, s, LANES), lambda bi, p: (bi, base + FOX_PAIRS + p, 0, 0)),
                  pl.BlockSpec((1, 1, s, LANES), lambda bi, p: (bi, base + 2 * FOX_PAIRS + p, 0, 0)),
                  pl.BlockSpec((1, 1, s, LANES), lambda bi, p: (bi, p, 0, 0)),
                  pl.BlockSpec((1, 1, s, LANES), lambda bi, p: (bi, p, 0, 0)),
                  pl.BlockSpec((2 * TQ_FOX, t), lambda bi, p: (0, 0))],
        out_specs=pl.BlockSpec((1, s, LANES), lambda bi, p: (bi, 0, p)),
        out_shape=jax.ShapeDtypeStruct((b, s, FOX_W), bf16),
        scratch_shapes=[pltpu.VMEM((2 * TQ_FOX, 2 * LANES), bf16),
                        pltpu.VMEM((s, 2 * LANES), bf16),
                        pltpu.VMEM((2 * TQ_FOX, LANES), f32),
                        pltpu.VMEM((2 * TQ_FOX, LANES), f32),
                        pltpu.VMEM((2, 2 * TQ_FOX, t), f32),
                        pltpu.VMEM((2, s, LANES), bf16)],
        compiler_params=_cparams(("parallel", "parallel")),
        name="fox",
    )(qkv, qkv, qkv, qx, kx, tri)


def _merge_kernel(yn_ref, yf_ref, xb_ref, x_ref, wn_ref, wf_ref, wmg_ref, wo_ref, g_ref, b_ref,
                  xo_ref, xbo_ref):
    mg = _dot(xb_ref[...], wmg_ref[...])
    merged = (_sigmoid(mg[:, 0:D_MODEL]) * _dot(yn_ref[...], wn_ref[...])
              + _sigmoid(mg[:, D_MODEL:2 * D_MODEL]) * _dot(yf_ref[...], wf_ref[...]))
    hmix = _dot(merged.astype(bf16), wo_ref[...])
    xn = _layer_norm(DN_ALPHA * x_ref[...] + hmix, g_ref[...], b_ref[...])
    xo_ref[...] = xn
    xbo_ref[...] = xn.astype(bf16)


def _merge(yn, yf, xb, x, wn, wf, wmg, wo, g, bb):
    m = x.shape[0]
    tm = TM_MERGE
    row = lambda w: pl.BlockSpec((tm, w), lambda i: (i, 0))
    full = lambda a: pl.BlockSpec(a.shape, lambda i: (0, 0))
    return pl.pallas_call(
        _merge_kernel,
        grid=(m // tm,),
        in_specs=[row(NSA_Q_W), row(FOX_W), row(D_MODEL), row(D_MODEL),
                  full(wn), full(wf), full(wmg), full(wo), full(g), full(bb)],
        out_specs=[row(D_MODEL), row(D_MODEL)],
        out_shape=[jax.ShapeDtypeStruct((m, D_MODEL), f32), jax.ShapeDtypeStruct((m, D_MODEL), bf16)],
        compiler_params=_cparams(("parallel",)),
        name="merge",
    )(yn, yf, xb, x, wn, wf, wmg, wo, g, bb)


def _ffn_kernel(xb_ref, x_ref, wg_ref, wu_ref, wd_ref, g_ref, b_ref, xo_ref, xbo_ref, acc_ref, *, nf):
    f = pl.program_id(1)

    @pl.when(f == 0)
    def _():
        acc_ref[...] = jnp.zeros_like(acc_ref)

    xb = xb_ref[...]
    gate = _dot(xb, wg_ref[...])
    up = _dot(xb, wu_ref[...])
    act = (gate * _sigmoid(gate) * up).astype(bf16)
    acc_ref[...] += _dot(act, wd_ref[...])

    @pl.when(f == nf - 1)
    def _():
        xn = _layer_norm(DN_ALPHA * x_ref[...] + acc_ref[...], g_ref[...], b_ref[...])
        xo_ref[...] = xn
        xbo_ref[...] = xn.astype(bf16)


def _ffn(xb, x, wg, wu, wd, g, bb):
    m = x.shape[0]
    dff = wg.shape[1]
    tm, tf = TM_FFN, TF_FFN
    nf = dff // tf
    row = pl.BlockSpec((tm, D_MODEL), lambda i, f: (i, 0))
    vec = pl.BlockSpec((1, D_MODEL), lambda i, f: (0, 0))
    return pl.pallas_call(
        functools.partial(_ffn_kernel, nf=nf),
        grid=(m // tm, nf),
        in_specs=[row, row,
                  pl.BlockSpec((D_MODEL, tf), lambda i, f: (0, f)),
                  pl.BlockSpec((D_MODEL, tf), lambda i, f: (0, f)),
                  pl.BlockSpec((tf, D_MODEL), lambda i, f: (f, 0)),
                  vec, vec],
        out_specs=[row, row],
        out_shape=[jax.ShapeDtypeStruct((m, D_MODEL), f32), jax.ShapeDtypeStruct((m, D_MODEL), bf16)],
        scratch_shapes=[pltpu.VMEM((tm, D_MODEL), f32)],
        compiler_params=_cparams(("parallel", "arbitrary")),
        name="ffn",
    )(xb, x, wg, wu, wd, g, bb)


def _router_kernel(x_ref, r_ref, tri_ref, gate_ref, rank_ref, *, seq):
    tb = tri_ref.shape[0]
    x = x_ref[0]
    xh = x.astype(bf16)
    xl = (x - xh.astype(f32)).astype(bf16)
    r = r_ref[...]
    rh = r.astype(bf16)
    rl = (r - rh.astype(f32)).astype(bf16)
    logits = _dot(xh, rh) + _dot(xh, rl) + _dot(xl, rh)
    lane = lax.broadcasted_iota(i32, (seq, LANES), 1).astype(f32)
    low = -3.0e38
    lg = jnp.where(lane < N_EXPERTS, logits, low)
    m1 = jnp.max(lg, axis=-1, keepdims=True)
    i1 = jnp.min(jnp.where(lg == m1, lane, float(LANES)), axis=-1, keepdims=True)
    lg2 = jnp.where(lane == i1, low, lg)
    m2 = jnp.max(lg2, axis=-1, keepdims=True)
    i2 = jnp.min(jnp.where(lg2 == m2, lane, float(LANES)), axis=-1, keepdims=True)
    e2 = jnp.exp(m2 - m1)
    den = 1.0 + e2
    gate_ref[0] = jnp.where(lane == i1, 1.0 / den, jnp.where(lane == i2, e2 / den, 0.0))
    sel = (lane == i1) | (lane == i2)
    selb = jnp.where(sel, 1.0, 0.0).astype(bf16)
    carry = jnp.zeros((1, LANES), f32)
    for blk in range(seq // tb):
        sl = slice(blk * tb, (blk + 1) * tb)
        c = _dot(tri_ref[...], selb[sl]) + carry
        carry = c[tb - 1:tb, :]
        rank_ref[0, sl, :] = jnp.where(sel[sl], c - 1.0, -1.0)


def _router(x3, router_pad, tri):
    b, s, _ = x3.shape
    return pl.pallas_call(
        functools.partial(_router_kernel, seq=s),
        grid=(b,),
        in_specs=[pl.BlockSpec((1, s, D_MODEL), lambda i: (i, 0, 0)),
                  pl.BlockSpec((D_MODEL, LANES), lambda i: (0, 0)),
                  pl.BlockSpec(tri.shape, lambda i: (0, 0))],
        out_specs=[pl.BlockSpec((1, s, LANES), lambda i: (i, 0, 0)),
                   pl.BlockSpec((1, s, LANES), lambda i: (i, 0, 0))],
        out_shape=[jax.ShapeDtypeStruct((b, s, LANES), f32),
                   jax.ShapeDtypeStruct((b, s, LANES), f32)],
        compiler_params=_cparams(("parallel",)),
        name="router",
    )(x3, router_pad, tri)


def _moe_kernel(tot_ref, tot_al_ref, off_ref, npc_ref, xb_ref, rankt_ref, gatet_ref, wg_ref, wu_ref, wd_ref,
                y_ref, xg_ref, acc_ref, *, seq, nf):
    tr = TR_MOE
    half = tr // 2
    nsubc = seq // SUB_MOE
    c = pl.program_id(0)
    e = pl.program_id(1)
    f = pl.program_id(2)
    ce = c * N_EXPERTS + e
    tot = tot_ref[ce]
    rem = tot % tr
    nfull = tot // tr + jnp.where(rem > half, 1, 0)
    has_tail = (rem > 0) & (rem <= half)
    tail0 = pl.multiple_of(nfull * tr, half)

    def pieces(fn):
        for q in range(nsubc):
            base = off_ref[ce * nsubc + q]

            def body(p, carry, q=q, base=base):
                fn(q, pl.multiple_of(base + p * PIECE_MOE, ROW_ALIGN))
                return carry
            lax.fori_loop(0, npc_ref[ce * nsubc + q], body, 0)

    def onehot_rows(q, r0):
        rk = rankt_ref[0, pl.ds(e, 1), q * SUB_MOE:(q + 1) * SUB_MOE]
        want = (r0 + lax.broadcasted_iota(i32, (PIECE_MOE, SUB_MOE), 0)).astype(f32)
        return rk == want

    def gather(q, r0):
        p = jnp.where(onehot_rows(q, r0), 1.0, 0.0).astype(bf16)
        xg_ref[pl.ds(r0, PIECE_MOE), :] += _dot(p, xb_ref[0, q * SUB_MOE:(q + 1) * SUB_MOE, :]).astype(bf16)

    def clear(r0, nr):
        xg_ref[pl.ds(r0, nr), :] = jnp.zeros((nr, D_MODEL), bf16)
        acc_ref[pl.ds(r0, nr), :] = jnp.zeros((nr, D_MODEL), f32)

    def hidden(r0, nr):
        xs = xg_ref[pl.ds(r0, nr), :]
        gate = _dot(xs, wg_ref[0, 0])
        up = _dot(xs, wu_ref[0, 0])
        act = (gate * _sigmoid(gate) * up).astype(bf16)
        acc_ref[pl.ds(r0, nr), :] += _dot(act, wd_ref[0, 0])

    def combine(q, r0):
        grow = gatet_ref[0, pl.ds(e, 1), q * SUB_MOE:(q + 1) * SUB_MOE]
        hit = onehot_rows(q, r0)
        w = jnp.sum(jnp.where(hit, grow, 0.0), axis=-1, keepdims=True)
        z = (acc_ref[pl.ds(r0, PIECE_MOE), :] * w).astype(bf16)
        p = jnp.where(hit, 1.0, 0.0).astype(bf16)
        y_ref[0, q * SUB_MOE:(q + 1) * SUB_MOE, :] += lax.dot_general(
            p, z, (((0,), (0,)), ((), ())), preferred_element_type=f32)

    def tiles(fn):
        def body(s, carry):
            fn(pl.multiple_of(s * tr, tr), tr)
            return carry
        lax.fori_loop(0, nfull, body, 0)

        @pl.when(has_tail)
        def _():
            fn(tail0, half)

    def zero(j, carry):
        y_ref[0, pl.ds(j * tr, tr), :] = jnp.zeros((tr, D_MODEL), f32)
        return carry

    lax.fori_loop(0, jnp.where((e == 0) & (f == 0), seq // tr, 0), zero, 0)

    @pl.when(f == 0)
    def _():
        tiles(clear)
        clear(pl.multiple_of(tot_al_ref[ce], ROW_ALIGN), tr)
        pieces(gather)

    tiles(hidden)

    @pl.when(f == nf - 1)
    def _():
        pieces(combine)


def _moe(xb3, gate, rank, wg, wu, wd, layer):
    b, s, _ = xb3.shape
    dff = wg.shape[3]
    nf = dff // TF_MOE
    nsubc = s // SUB_MOE
    rk = rank[:, :, :N_EXPERTS]
    cnt = jnp.sum((rk >= 0.0).reshape(b, nsubc, SUB_MOE, N_EXPERTS), axis=2).astype(i32)
    first = jnp.cumsum(cnt, axis=1) - cnt
    off = first // ROW_ALIGN * ROW_ALIGN
    npc = jnp.where(cnt > 0, (first - off + cnt + PIECE_MOE - 1) // PIECE_MOE, 0)
    post = rk.transpose(0, 2, 1)
    gatet = gate[:, :, :N_EXPERTS].transpose(0, 2, 1)
    tot = jnp.sum(cnt, axis=1).reshape(-1)
    tot_al = (tot + ROW_ALIGN - 1) // ROW_ALIGN * ROW_ALIGN
    off_flat = off.transpose(0, 2, 1).reshape(-1)
    npc = npc.transpose(0, 2, 1).reshape(-1)
    buf_rows = s + ROW_ALIGN + PIECE_MOE + TR_MOE
    grid_spec = pltpu.PrefetchScalarGridSpec(
        num_scalar_prefetch=4,
        grid=(b, N_EXPERTS, nf),
        in_specs=[pl.BlockSpec((1, s, D_MODEL), lambda c, e, f, *_: (c, 0, 0)),
                  pl.BlockSpec((1, N_EXPERTS, s), lambda c, e, f, *_: (c, 0, 0)),
                  pl.BlockSpec((1, N_EXPERTS, s), lambda c, e, f, *_: (c, 0, 0)),
                  pl.BlockSpec((1, 1, D_MODEL, TF_MOE), lambda c, e, f, *_: (layer, e, 0, f)),
                  pl.BlockSpec((1, 1, D_MODEL, TF_MOE), lambda c, e, f, *_: (layer, e, 0, f)),
                  pl.BlockSpec((1, 1, TF_MOE, D_MODEL), lambda c, e, f, *_: (layer, e, f, 0))],
        out_specs=pl.BlockSpec((1, s, D_MODEL), lambda c, e, f, *_: (c, 0, 0)),
        scratch_shapes=[pltpu.VMEM((buf_rows, D_MODEL), bf16), pltpu.VMEM((buf_rows, D_MODEL), f32)],
    )
    return pl.pallas_call(
        functools.partial(_moe_kernel, seq=s, nf=nf),
        grid_spec=grid_spec,
        out_shape=jax.ShapeDtypeStruct((b, s, D_MODEL), f32),
        compiler_params=_cparams(("parallel", "arbitrary", "arbitrary")),
        name="moe",
    )(tot, tot_al, off_flat, npc, xb3, post, gatet, wg, wu, wd)


def _resln_kernel(x_ref, y_ref, g_ref, b_ref, xo_ref, xbo_ref):
    xn = _layer_norm(DN_ALPHA * x_ref[...] + y_ref[...], g_ref[...], b_ref[...])
    xo_ref[...] = xn
    xbo_ref[...] = xn.astype(bf16)


def _resln(x, y, g, bb):
    m = x.shape[0]
    tm = TM_FFN
    row = pl.BlockSpec((tm, D_MODEL), lambda i: (i, 0))
    vec = pl.BlockSpec((1, D_MODEL), lambda i: (0, 0))
    return pl.pallas_call(
        _resln_kernel,
        grid=(m // tm,),
        in_specs=[row, row, vec, vec],
        out_specs=[row, row],
        out_shape=[jax.ShapeDtypeStruct((m, D_MODEL), f32), jax.ShapeDtypeStruct((m, D_MODEL), bf16)],
        compiler_params=_cparams(("parallel",)),
        name="resln",
    )(x, y, g, bb)


def _t5_bucket(dist):
    n = jnp.maximum(dist, 0)
    max_exact = REL_BUCKETS // 2
    large = max_exact + (jnp.log(jnp.maximum(n, 1).astype(f32) / max_exact)
                         / math.log(REL_MAX_DIST / max_exact) * (REL_BUCKETS - max_exact)).astype(i32)
    large = jnp.minimum(large, REL_BUCKETS - 1)
    return jnp.where(n < max_exact, n, large)


def _bias_of_dist(rel_bias, dist):
    onehot = (_t5_bucket(dist)[None] == jnp.arange(REL_BUCKETS).reshape((-1,) + (1,) * dist.ndim)).astype(f32)
    return LOG2E * jnp.einsum("kh,k...->h...", rel_bias.astype(f32), onehot, precision=lax.Precision.HIGHEST)


def _bias_tables(rel_bias, seq):
    t = T_ATT
    rows = NSA_GROUP * t
    ncmp = seq // CMP_STRIDE
    d0 = jnp.arange(t)[:, None] - jnp.arange(t)[None, :]
    offs = jnp.array([0, t, 2 * t, WINDOW]).reshape(4, 1, 1)
    kinds = _bias_of_dist(rel_bias, offs + d0[None])
    mask = jnp.stack([d0 >= 0, d0 == d0, d0 == d0, d0 < 0])
    kinds = jnp.where(mask[None], kinds, NEG_INF)
    order = np.array(NSA_ROW_ORDER)
    tbias = kinds.reshape(NSA_KV_HEADS, NSA_GROUP, 4, t, t)[:, order].transpose(0, 2, 1, 3, 4)
    tbias = tbias.reshape(NSA_KV_HEADS, 4, rows, t)
    cend = jnp.arange(ncmp) * CMP_STRIDE + CMP_BLOCK - 1
    dist_c = jnp.arange(seq)[:, None] - cend[None, :]
    bc = jnp.where(dist_c >= 0, _bias_of_dist(rel_bias, dist_c), NEG_INF)
    bc = bc.reshape(NSA_KV_HEADS, NSA_GROUP, seq // t, t, ncmp)[:, order].transpose(0, 2, 4, 1, 3)
    return tbias, bc.reshape(NSA_KV_HEADS, (seq // t) * ncmp, rows)


def _selection_constants(seq):
    ncmp = seq // CMP_STRIDE
    nsel = seq // SEL_BLOCK
    c0 = np.arange(ncmp)[:, None] * CMP_STRIDE
    s0 = np.arange(LANES)[None, :] * SEL_BLOCK
    ov = np.maximum(np.minimum(c0 + CMP_BLOCK, s0 + SEL_BLOCK) - np.maximum(c0, s0), 0) / CMP_BLOCK
    ov[ncmp - 1, :] = 0.0
    ov[:, nsel:] = 0.0
    et = (np.arange(seq)[:, None] // SEL_BLOCK == np.arange(LANES)[None, :]).astype(np.float32)
    eg = np.zeros((LANES, 3 * (NSA_GROUP // 2) * LANES), np.float32)
    for br in range(3):
        for g in range(NSA_GROUP):
            c0 = (br * (NSA_GROUP // 2) + g // 2) * LANES + (g % 2) * HEAD_DIM
            eg[br * NSA_GROUP + g, c0:c0 + HEAD_DIM] = 1.0
    return jnp.asarray(ov.T, bf16), jnp.asarray(et, bf16), jnp.asarray(eg, bf16)


def _fox_placement():
    xw = FOX_PAIRS * LANES
    pq = np.zeros((3, LANES, xw), np.float32)
    pk = np.zeros((3, LANES, xw), np.float32)
    oq = np.zeros((1, xw), np.float32)
    ok = np.zeros((1, xw), np.float32)
    for p in range(FOX_PAIRS):
        for hh in range(2):
            src = FGATE_LANE + 2 * p + hh
            base = p * LANES + hh * XCOLS
            for part in range(3):
                pk[part, src, base + part] = -1.0
                pq[part, src, base + 3 + part] = 1.0
                oq[0, base + part] = 1.0
                ok[0, base + 3 + part] = 1.0
    return jnp.asarray(pq, bf16), jnp.asarray(pk, bf16), jnp.asarray(oq), jnp.asarray(ok)


def _layer_weights(w_in, layer_pe, w1, w2, f_bias):
    offs = np.cumsum((NSA_Q_W, 6 * 2 * HEAD_DIM, 3 * NSA_HEADS, 3 * FOX_W, FOX_HEADS, 2 * D_MODEL))
    kv0, g0, fx0, ff0, mg0 = offs[0], offs[1], offs[2], offs[3], offs[4]
    scale = HEAD_DIM ** -0.5 * LOG2E
    kvw = NSA_KV_HEADS * HEAD_DIM
    w_kv = w_in[:, kv0:g0]
    w_kvdup = jnp.repeat(w_kv[:, 2 * kvw:].reshape(D_MODEL, 4 * NSA_KV_HEADS, 1, HEAD_DIM), 2, axis=2)
    w_kvdup = w_kvdup.reshape(D_MODEL, 8 * kvw)
    w_qkv = jnp.concatenate([w_in[:, :NSA_Q_W] * scale, w_kvdup, w_in[:, fx0:fx0 + FOX_W] * scale,
                             w_in[:, fx0 + FOX_W:ff0]], axis=1).astype(bf16)
    zeros = lambda n: jnp.zeros((D_MODEL, n), w_in.dtype)
    ng = 3 * NSA_GROUP
    w_g = w_in[:, g0:fx0].reshape(D_MODEL, NSA_KV_HEADS, NSA_GROUP, 3).transpose(0, 1, 3, 2)
    gate_cols = lambda h: w_g[:, h].reshape(D_MODEL, ng)
    w_aux = jnp.concatenate([w_kv[:, :2 * kvw],
                             gate_cols(0), zeros(FGATE_LANE - ng), w_in[:, ff0:mg0],
                             zeros(LANES - FGATE_LANE - FOX_HEADS),
                             gate_cols(1), zeros(LANES - ng)], axis=1).astype(bf16)
    fb_row = jnp.zeros((1, LANES), f32).at[0, FGATE_LANE:FGATE_LANE + FOX_HEADS].set(f_bias.astype(f32))
    pe2 = jnp.tile(layer_pe.astype(f32), (1, 1, NSA_KV_HEADS))
    w1r = w1.reshape(2, CMP_BLOCK, HEAD_DIM, CMP_HIDDEN).astype(bf16)
    zero = jnp.zeros_like(w1r)
    w1bd = jnp.concatenate([jnp.concatenate([w1r, zero], axis=-1),
                            jnp.concatenate([zero, w1r], axis=-1)], axis=-2)
    w2d = jnp.concatenate([w2, w2], axis=-1).astype(bf16)
    return (w_qkv, w_aux, w_in[:, mg0:].astype(bf16), fb_row, pe2, w1bd[:, :CMP_STRIDE], w1bd[:, CMP_STRIDE:],
            w2d)


def kernel(x, w_in, nsa_cmp_pe, nsa_cmp_w1, nsa_cmp_w2, fox_f_bias, w_nsa_branch, w_fox_branch, w_out,
           rel_bias, ln1_g, ln1_b, ln2_g, ln2_b, dense_w_gate, dense_w_up, dense_w_down, moe_router,
           moe_w_gate, moe_w_up, moe_w_down):
    b, s, d = x.shape
    m = b * s
    tbias, bias_c = _bias_tables(rel_bias, s)
    ov, et, eg = _selection_constants(s)
    pq, pk, oq, ok = _fox_placement()
    tri128 = jnp.asarray(np.tril(np.ones((LANES, LANES), np.float32)), bf16)
    tri256 = jnp.asarray(np.tril(np.ones((256, 256), np.float32)), bf16)

    moe_wg, moe_wu, moe_wd = moe_w_gate.astype(bf16), moe_w_up.astype(bf16), moe_w_down.astype(bf16)
    xf = x.reshape(m, d).astype(f32)
    xb = xf.astype(bf16)
    for layer in range(DEPTH):
        w_qkv, w_aux, w_mg, fb_row, pe2, w1t, w1b, w2d = _layer_weights(
            w_in[layer], nsa_cmp_pe[layer], nsa_cmp_w1[layer], nsa_cmp_w2[layer], fox_f_bias[layer])
        qkv, aux = _proj(xb, w_qkv, w_aux, b)
        aux = aux.reshape(b, s, AUX_W)
        gates, qx, kx = _gateprep(aux, fb_row, tri128, pq, pk, oq, ok)
        cmpkv = _compress(aux, pe2, w1t, w1b, w2d)
        y_nsa = _nsa(qkv, cmpkv, gates, bias_c, tbias, ov, et, eg).reshape(m, NSA_Q_W)
        y_fox = _fox(qkv, qx, kx).reshape(m, FOX_W)
        xf, xb = _merge(y_nsa, y_fox, xb, xf, w_nsa_branch[layer].astype(bf16),
                        w_fox_branch[layer].astype(bf16), w_mg, w_out[layer].astype(bf16),
                        ln1_g[layer].reshape(1, d), ln1_b[layer].reshape(1, d))
        j = layer // 2
        g2, b2 = ln2_g[layer].reshape(1, d), ln2_b[layer].reshape(1, d)
        if layer % 2 == 0:
            xf, xb = _ffn(xb, xf, dense_w_gate[j].astype(bf16), dense_w_up[j].astype(bf16),
                          dense_w_down[j].astype(bf16), g2, b2)
        else:
            router_pad = jnp.zeros((d, LANES), f32).at[:, :N_EXPERTS].set(moe_router[j].astype(f32))
            gate, rank = _router(xf.reshape(b, s, d), router_pad, tri256)
            y = _moe(xb.reshape(b, s, d), gate, rank, moe_wg, moe_wu, moe_wd, j)
            xf, xb = _resln(xf, y.reshape(m, d), g2, b2)
    return xf.reshape(b, s, d).astype(x.dtype)
```

```python
import functools
import math

import numpy as np
import jax
import jax.numpy as jnp
from jax import lax
from jax.experimental import pallas as pl
from jax.experimental.pallas import tpu as pltpu

f32 = jnp.float32
bf16 = jnp.bfloat16
i32 = jnp.int32

D_MODEL = 1024
HEAD_DIM = 64
LANES = 128
NSA_HEADS = 8
NSA_KV_HEADS = 2
NSA_GROUP = NSA_HEADS // NSA_KV_HEADS
NSA_ROW_ORDER = (0, 2, 1, 3)
FOX_HEADS = 8
FOX_PAIRS = FOX_HEADS // 2
CMP_BLOCK = 32
CMP_STRIDE = 16
CMP_HIDDEN = 128
SEL_BLOCK = 64
N_SELECT = 16
N_LOCAL_SEL = 2
WINDOW = 512
REL_BUCKETS = 32
REL_MAX_DIST = 128
N_EXPERTS = 8
DEPTH = 4
DN_ALPHA = (2 * DEPTH) ** 0.25
LN_EPS = 1e-5
FORCED_SCORE = 1e4
NEG_INF = -1e30
LOG2E = math.log2(math.e)

NSA_Q_W = NSA_HEADS * HEAD_DIM
FOX_W = FOX_HEADS * HEAD_DIM
QKV_TILES_KV = NSA_Q_W // LANES
QKV_TILES_FOX = QKV_TILES_KV + 4 * NSA_KV_HEADS
QKV_W = (QKV_TILES_FOX + 3 * FOX_PAIRS) * LANES
AUX_W = 4 * LANES
FGATE_LANE = 24
XCOLS = 6

T_ATT = 256
TQ_FOX = 512
TK_FOX = 512
TM_PROJ = 1024
TM_MERGE = 1024
TM_FFN = 512
TF_FFN = 2816
TR_MOE = 256
SUB_MOE = 512
PIECE_MOE = 192
ROW_ALIGN = 16
TF_MOE = 896
VMEM_LIMIT = 56 * 1024 * 1024


def _cparams(sem):
    return pltpu.CompilerParams(dimension_semantics=sem, vmem_limit_bytes=VMEM_LIMIT)


def _dot(a, b):
    return jnp.dot(a, b, preferred_element_type=f32)


def _dot_nt(a, b):
    return lax.dot_general(a, b, (((1,), (1,)), ((), ())), preferred_element_type=f32)


def _sigmoid(x):
    return 1.0 / (1.0 + jnp.exp(-x))


def _layer_norm(z, g, b):
    mu = jnp.mean(z, axis=-1, keepdims=True)
    zc = z - mu
    var = jnp.mean(zc * zc, axis=-1, keepdims=True)
    return zc * lax.rsqrt(var + LN_EPS) * g + b


def _split3(x):
    hi = x.astype(bf16)
    r1 = x - hi.astype(f32)
    mid = r1.astype(bf16)
    lo = (r1 - mid.astype(f32)).astype(bf16)
    return hi, mid, lo


def _proj_kernel(x_ref, w_ref, wa_ref, o_ref, oa_ref):
    x = x_ref[...]
    res = _dot(x, w_ref[...])
    for j in range(o_ref.shape[1]):
        o_ref[0, j] = res[:, j * LANES:(j + 1) * LANES].astype(o_ref.dtype)
    oa_ref[...] = _dot(x, wa_ref[...])


def _proj(xb, w, wa, batch):
    m, k = xb.shape
    n, na = w.shape[1], wa.shape[1]
    nb = m // batch // TM_PROJ
    return pl.pallas_call(
        _proj_kernel,
        grid=(m // TM_PROJ,),
        in_specs=[pl.BlockSpec((TM_PROJ, k), lambda i: (i, 0)),
                  pl.BlockSpec((k, n), lambda i: (0, 0)),
                  pl.BlockSpec((k, na), lambda i: (0, 0))],
        out_specs=[pl.BlockSpec((1, n // LANES, TM_PROJ, LANES), lambda i: (i // nb, 0, i % nb, 0)),
                   pl.BlockSpec((TM_PROJ, na), lambda i: (i, 0))],
        out_shape=[jax.ShapeDtypeStruct((batch, n // LANES, m // batch, LANES), bf16),
                   jax.ShapeDtypeStruct((m, na), f32)],
        compiler_params=_cparams(("parallel",)),
        name="proj",
    )(xb, w, wa)


def _gateprep_kernel(a_ref, fb_ref, tri_ref, pq_ref, pk_ref, oq_ref, ok_ref, g_ref, qx_ref, kx_ref, *, seq):
    tb = LANES
    tri = tri_ref[...]
    local = []
    for blk in range(seq // tb):
        sl = slice(blk * tb, (blk + 1) * tb)
        va = a_ref[0, sl, 0:LANES]
        g_ref[0, 0, sl, :] = _sigmoid(va)
        g_ref[0, 1, sl, :] = _sigmoid(a_ref[0, sl, LANES:2 * LANES])
        z = va + fb_ref[...]
        logf = jnp.minimum(z, 0.0) - jnp.log1p(jnp.exp(-jnp.abs(z)))
        hi, mid, lo = _split3(logf)
        local.append(_dot(tri, hi) + _dot(tri, mid) + _dot(tri, lo))
    carry = jnp.zeros((1, LANES), f32)
    for blk in range(seq // tb):
        sl = slice(blk * tb, (blk + 1) * tb)
        c = local[blk] + carry
        carry = c[tb - 1:tb, :]
        chi, cmid, clo = _split3(c * LOG2E)
        qx = _dot(chi, pq_ref[0]) + _dot(cmid, pq_ref[1]) + _dot(clo, pq_ref[2]) + oq_ref[...]
        kx = _dot(chi, pk_ref[0]) + _dot(cmid, pk_ref[1]) + _dot(clo, pk_ref[2]) + ok_ref[...]
        for p in range(FOX_PAIRS):
            qx_ref[0, p, sl, :] = qx[:, p * LANES:(p + 1) * LANES].astype(bf16)
            kx_ref[0, p, sl, :] = kx[:, p * LANES:(p + 1) * LANES].astype(bf16)


def _gateprep(aux, fb_row, tri, pq, pk, oq, ok):
    b, s, _ = aux.shape
    const2 = lambda a: pl.BlockSpec(a.shape, lambda i: (0, 0))
    const3 = lambda a: pl.BlockSpec(a.shape, lambda i: (0, 0, 0))
    return pl.pallas_call(
        functools.partial(_gateprep_kernel, seq=s),
        grid=(b,),
        in_specs=[pl.BlockSpec((1, s, 2 * LANES), lambda i: (i, 0, 1)),
                  const2(fb_row), const2(tri), const3(pq), const3(pk), const2(oq), const2(ok)],
        out_specs=[pl.BlockSpec((1, 2, s, LANES), lambda i: (i, 0, 0, 0)),
                   pl.BlockSpec((1, FOX_PAIRS, s, LANES), lambda i: (i, 0, 0, 0)),
                   pl.BlockSpec((1, FOX_PAIRS, s, LANES), lambda i: (i, 0, 0, 0))],
        out_shape=[jax.ShapeDtypeStruct((b, 2, s, LANES), f32),
                   jax.ShapeDtypeStruct((b, FOX_PAIRS, s, LANES), bf16),
                   jax.ShapeDtypeStruct((b, FOX_PAIRS, s, LANES), bf16)],
        compiler_params=_cparams(("parallel",)),
        name="gateprep",
    )(aux, fb_row, tri, pq, pk, oq, ok)


def _gelu_tanh(x):
    c = math.sqrt(2.0 / math.pi)
    return x * (0.5 * (1.0 + jnp.tanh(c * (x + 0.044715 * (x * x * x)))))


def _compress_kernel(a_ref, pe_ref, w1t_ref, w1b_ref, w2_ref, o_ref, *, nhalf):
    top = jnp.zeros((nhalf, NSA_KV_HEADS * CMP_HIDDEN), f32)
    bot = jnp.zeros((nhalf, NSA_KV_HEADS * CMP_HIDDEN), f32)
    for l in range(CMP_STRIDE):
        rows = a_ref[0, pl.ds(l, nhalf, stride=CMP_STRIDE), :]
        top = top + _dot((rows + pe_ref[0, l:l + 1, :]).astype(bf16), w1t_ref[0, l])
        bot = bot + _dot((rows + pe_ref[0, CMP_STRIDE + l:CMP_STRIDE + l + 1, :]).astype(bf16), w1b_ref[0, l])
    pre = top + pltpu.roll(bot, nhalf - 1, 0)
    act = _gelu_tanh(pre).astype(bf16)
    for h in range(NSA_KV_HEADS):
        o_ref[0, 0, h] = _dot(act[:, h * CMP_HIDDEN:(h + 1) * CMP_HIDDEN], w2_ref[0]).astype(o_ref.dtype)


def _compress(aux, pe2, w1t, w1b, w2d):
    b, s, _ = aux.shape
    nhalf = s // CMP_STRIDE
    w1_spec = pl.BlockSpec((1,) + w1t.shape[1:], lambda i, j: (j, 0, 0, 0))
    return pl.pallas_call(
        functools.partial(_compress_kernel, nhalf=nhalf),
        grid=(b, 2),
        in_specs=[pl.BlockSpec((1, s, LANES), lambda i, j: (i, 0, j)),
                  pl.BlockSpec((1, CMP_BLOCK, LANES), lambda i, j: (j, 0, 0)),
                  w1_spec, w1_spec,
                  pl.BlockSpec((1, CMP_HIDDEN, LANES), lambda i, j: (j, 0, 0))],
        out_specs=pl.BlockSpec((1, 1, NSA_KV_HEADS, nhalf, LANES), lambda i, j: (i, j, 0, 0, 0)),
        out_shape=jax.ShapeDtypeStruct((b, 2, NSA_KV_HEADS, nhalf, LANES), bf16),
        compiler_params=_cparams(("parallel", "parallel")),
        name="compress",
    )(aux, pe2, w1t, w1b, w2d)


def _flash_init(m_ref, acc_ref):
    m_ref[...] = jnp.full(m_ref.shape, NEG_INF, f32)
    acc_ref[...] = jnp.zeros(acc_ref.shape, f32)


def _ones_values(v):
    lo_half = lax.broadcasted_iota(i32, v.shape, 1) < HEAD_DIM
    vf = v.astype(f32)
    return jnp.where(lo_half, vf, 1.0).astype(bf16), jnp.where(lo_half, 1.0, vf).astype(bf16)


def _flash_step(s, v_lo, v_hi, m_ref, acc_ref):
    nk = s.shape[1] // LANES
    half = s.shape[0] // 2
    cols = [s[:, c * LANES:(c + 1) * LANES] for c in range(nk)]
    mx = cols[0]
    for c in cols[1:]:
        mx = jnp.maximum(mx, c)
    m_old = m_ref[...]
    m_new = jnp.maximum(m_old, jnp.broadcast_to(jnp.max(mx, axis=-1, keepdims=True), m_old.shape))
    alpha = jnp.exp2(m_old - m_new)
    p = jnp.concatenate([jnp.exp2(c - m_new).astype(bf16) for c in cols], axis=1)
    pv = jnp.concatenate([_dot(p[0:half], v_lo), _dot(p[half:], v_hi)], axis=0)
    acc_ref[...] = alpha * acc_ref[...] + pv
    m_ref[...] = m_new


def _flash_loop(first, last, logits_fn, v_fn, s_ref, m_ref, acc_ref, last_fix=None):
    sa, sb = s_ref.at[0], s_ref.at[1]
    n = last - first + 1
    pairs = (n - 1) // 2

    def step(buf, kt, fix=None):
        s = buf[...] if fix is None else fix(buf[...])
        _flash_step(s, *v_fn(kt), m_ref, acc_ref)

    sa[...] = logits_fn(first)

    def body(j, carry):
        kt = first + 2 * j
        sb[...] = logits_fn(kt + 1)
        step(sa, kt)
        sa[...] = logits_fn(kt + 2)
        step(sb, kt + 1)
        return carry

    lax.fori_loop(0, pairs, body, 0)
    two_left = n - 2 * pairs == 2

    @pl.when(two_left)
    def _():
        sb[...] = logits_fn(last)
        step(sa, last - 1)
        step(sb, last, last_fix)

    @pl.when(jnp.logical_not(two_left))
    def _():
        step(sa, last, last_fix)


def _flash_pair(acc_lo, acc_hi, lo_half):
    num = jnp.where(lo_half, acc_lo, acc_hi)
    den = pltpu.roll(jnp.where(lo_half, acc_hi, acc_lo), HEAD_DIM, 1)
    return num * (1.0 / jnp.maximum(den, 1e-30))


def _nsa_kernel(q_ref, ks_ref, vs_ref, kw_ref, vw_ref, kc_ref, vc_ref, gate_ref, bc_ref, tb_ref, ov_ref,
                et_ref, eg_ref, o_ref, qaug_ref, kaug_ref, m_ref, acc_ref, os_ref, s_ref, vsel_ref,
                vwin_ref, *, seq, ncmp):
    kaug_ref[:, 0:LANES] = ks_ref[0, 0]
    kaug_ref[:, LANES:2 * LANES] = et_ref[...]
    vsel_ref[0], vsel_ref[1] = _ones_values(vs_ref[0, 0])
    vwin_ref[0], vwin_ref[1] = _ones_values(vw_ref[0, 0])

    def query_tile(i, carry):
        _nsa_tile(i, q_ref, kw_ref, kc_ref, vc_ref, gate_ref, bc_ref, tb_ref, ov_ref, eg_ref, o_ref, qaug_ref,
                  kaug_ref, m_ref, acc_ref, os_ref, s_ref, vsel_ref, vwin_ref, seq=seq, ncmp=ncmp)
        return carry

    lax.fori_loop(0, seq // T_ATT, query_tile, 0)


def _nsa_tile(i, q_ref, kw_ref, kc_ref, vc_ref, gate_ref, bc_ref, tb_ref, ov_ref, eg_ref, o_ref, qaug_ref,
              kaug_ref, m_ref, acc_ref, os_ref, s_ref, vsel_ref, vwin_ref, *, seq, ncmp):
    t = T_ATT
    rows = NSA_GROUP * t
    nsel = seq // SEL_BLOCK
    topn = min(N_SELECT, nsel)
    t0 = i * t
    tile_rows = pl.ds(pl.multiple_of(t0, t), t)
    lane = lax.broadcasted_iota(i32, (t, LANES), 1)
    lo_half = lane < HEAD_DIM

    for rb, g in enumerate(NSA_ROW_ORDER):
        qg = q_ref[0, g // 2, tile_rows, :].astype(f32)
        keep = lo_half if g % 2 == 0 else jnp.logical_not(lo_half)
        qaug_ref[rb * t:(rb + 1) * t, 0:LANES] = jnp.where(keep, qg, 0.0).astype(bf16)
    qs = qaug_ref[:, 0:LANES]

    s = _dot_nt(kc_ref[0, 0, 0], qs) + bc_ref[0, pl.ds(pl.multiple_of(i * ncmp, ncmp), ncmp), :]
    sees_any = t0 + (lax.broadcasted_iota(i32, (1, rows), 1) & (t - 1)) >= CMP_BLOCK - 1
    s = s - jnp.max(s, axis=0, keepdims=True)
    e = jnp.where(sees_any, jnp.exp2(s), 0.0)
    p_c = e * (1.0 / jnp.maximum(jnp.sum(e, axis=0, keepdims=True), 1e-30))
    o_c = lax.dot_general(p_c.astype(bf16), vc_ref[0, 0, 0], (((0,), (0,)), ((), ())),
                          preferred_element_type=f32)

    psum = p_c[:, 0:t]
    for g in range(1, NSA_GROUP):
        psum = psum + p_c[:, g * t:(g + 1) * t]
    hi = psum.astype(bf16)
    lo = (psum - hi.astype(f32)).astype(bf16)
    imp = (_dot(ov_ref[...], hi) + _dot(ov_ref[...], lo))[0:nsel]
    blk = lax.broadcasted_iota(i32, (nsel, t), 0)
    cur = (t0 + lax.broadcasted_iota(i32, (nsel, t), 1)) >> 6
    forced = (blk == 0) | ((blk <= cur) & (blk > cur - N_LOCAL_SEL))
    val = jnp.where(forced, FORCED_SCORE, jnp.where(blk <= cur, imp, -1.0))
    ngrp = nsel // 8
    sub = lax.broadcasted_iota(i32, (8, t), 0)
    vals = [val[8 * r:8 * r + 8, :] for r in range(ngrp)]
    cnts = [jnp.zeros((8, t), f32) for _ in range(ngrp)]
    for j in range(nsel):
        vj = val[j:j + 1, :]
        for r in range(ngrp):
            ahead = jnp.where(vj > vals[r], 1.0, 0.0)
            ahead_or_tied = jnp.where(vj >= vals[r], 1.0, 0.0)
            if r < j // 8:
                cnts[r] = cnts[r] + ahead
            elif r > j // 8:
                cnts[r] = cnts[r] + ahead_or_tied
            else:
                cnts[r] = cnts[r] + jnp.where(sub > j % 8, ahead_or_tied, ahead)
    cnt = jnp.concatenate(cnts, axis=0)
    mneg = jnp.where(cnt < topn, 0.0, NEG_INF)
    mneg = jnp.concatenate([mneg, jnp.zeros((LANES - nsel, t), f32)], axis=0).T.astype(bf16)
    for g in range(NSA_GROUP):
        qaug_ref[g * t:(g + 1) * t, LANES:2 * LANES] = mneg

    _flash_init(m_ref, os_ref)

    def sel_logits(kt):
        k = kaug_ref[pl.ds(kt * t, t), :]
        return _dot_nt(qaug_ref[...], k) + tb_ref[0, jnp.minimum(i - kt, 2)]

    def values(v_ref):
        return lambda kt: (v_ref[0, pl.ds(kt * t, t), :], v_ref[1, pl.ds(kt * t, t), :])

    _flash_loop(0, i, sel_logits, values(vsel_ref), s_ref, m_ref, os_ref)

    _flash_init(m_ref, acc_ref)
    nwin = WINDOW // t

    def win_logits(kt):
        d = i - kt
        kind = jnp.where(d == nwin, 3, d)
        return _dot_nt(qs, kw_ref[0, 0, pl.ds(kt * t, t), :]) + tb_ref[0, kind]

    _flash_loop(jnp.maximum(i - nwin, 0), i, win_logits, values(vwin_ref), s_ref, m_ref, acc_ref)

    ghi, gmid, glo = _split3(gate_ref[0, 0, tile_rows, :])
    gexp = _dot(ghi, eg_ref[...]) + _dot(gmid, eg_ref[...]) + _dot(glo, eg_ref[...])
    for j in range(NSA_GROUP // 2):
        ra, rb = NSA_ROW_ORDER.index(2 * j), NSA_ROW_ORDER.index(2 * j + 1)
        out = jnp.zeros((t, LANES), f32)
        sa, sb = slice(ra * t, (ra + 1) * t), slice(rb * t, (rb + 1) * t)
        pairs = (jnp.where(lo_half, o_c[sa], o_c[sb]),
                 _flash_pair(os_ref[sa, :], os_ref[sb, :], lo_half),
                 _flash_pair(acc_ref[sa, :], acc_ref[sb, :], lo_half))
        for br, o_pair in enumerate(pairs):
            col = (br * (NSA_GROUP // 2) + j) * LANES
            out = out + gexp[:, col:col + LANES] * o_pair
        o_ref[0, tile_rows, j * LANES:(j + 1) * LANES] = out.astype(o_ref.dtype)


def _nsa(qkv, cmpkv, gates, bias_c, tbias, ov, et, eg):
    b, _, s, _ = qkv.shape
    ncmp = cmpkv.shape[3]
    t = T_ATT
    rows = NSA_GROUP * t
    qw = NSA_GROUP * HEAD_DIM
    qtiles = qw // LANES
    kv_spec = lambda col: pl.BlockSpec((1, 1, s, LANES), lambda h, bi, col=col: (bi, col + h, 0, 0))
    cmp_spec = lambda kv: pl.BlockSpec((1, 1, 1, ncmp, LANES), lambda h, bi, kv=kv: (bi, kv, h, 0, 0))
    base = QKV_TILES_KV
    return pl.pallas_call(
        functools.partial(_nsa_kernel, seq=s, ncmp=ncmp),
        grid=(NSA_KV_HEADS, b),
        in_specs=[pl.BlockSpec((1, qtiles, s, LANES), lambda h, bi: (bi, h, 0, 0)),
                  kv_spec(base), kv_spec(base + 2), kv_spec(base + 4), kv_spec(base + 6),
                  cmp_spec(0), cmp_spec(1),
                  pl.BlockSpec((1, 1, s, LANES), lambda h, bi: (bi, h, 0, 0)),
                  pl.BlockSpec((1, (s // t) * ncmp, rows), lambda h, bi: (h, 0, 0)),
                  pl.BlockSpec((1, 4, rows, t), lambda h, bi: (h, 0, 0, 0)),
                  pl.BlockSpec((LANES, ncmp), lambda h, bi: (0, 0)),
                  pl.BlockSpec((s, LANES), lambda h, bi: (0, 0)),
                  pl.BlockSpec(eg.shape, lambda h, bi: (0, 0))],
        out_specs=pl.BlockSpec((1, s, qw), lambda h, bi: (bi, 0, h)),
        out_shape=jax.ShapeDtypeStruct((b, s, NSA_Q_W), bf16),
        scratch_shapes=[pltpu.VMEM((rows, 2 * LANES), bf16),
                        pltpu.VMEM((s, 2 * LANES), bf16),
                        pltpu.VMEM((rows, LANES), f32),
                        pltpu.VMEM((rows, LANES), f32),
                        pltpu.VMEM((rows, LANES), f32),
                        pltpu.VMEM((2, rows, t), f32),
                        pltpu.VMEM((2, s, LANES), bf16),
                        pltpu.VMEM((2, s, LANES), bf16)],
        compiler_params=_cparams(("parallel", "parallel")),
        name="nsa",
    )(qkv, qkv, qkv, qkv, qkv, cmpkv, cmpkv, gates, bias_c, tbias, ov, et, eg)


def _fox_kernel(q_ref, k_ref, v_ref, qx_ref, kx_ref, tri_ref, o_ref, qaug_ref, kaug_ref, m_ref, acc_ref, s_ref,
                vaug_ref):
    t = TK_FOX
    tq = TQ_FOX
    seq = k_ref.shape[2]
    lane = lax.broadcasted_iota(i32, (tq, LANES), 1)
    lo_half = lane < HEAD_DIM
    kaug_ref[:, 0:LANES] = k_ref[0, 0]
    kaug_ref[:, LANES:2 * LANES] = kx_ref[0, 0]
    vaug_ref[0], vaug_ref[1] = _ones_values(v_ref[0, 0])

    def logits(kt):
        return _dot_nt(qaug_ref[...], kaug_ref[pl.ds(kt * t, t), :])

    def values(kt):
        return vaug_ref[0, pl.ds(kt * t, t), :], vaug_ref[1, pl.ds(kt * t, t), :]

    def query_tile(i, carry):
        rows = pl.ds(pl.multiple_of(i * tq, tq), tq)
        q = q_ref[0, 0, rows, :].astype(f32)
        qx = qx_ref[0, 0, rows, :].astype(f32)
        qaug_ref[0:tq, 0:LANES] = jnp.where(lo_half, q, 0.0).astype(bf16)
        qaug_ref[tq:2 * tq, 0:LANES] = jnp.where(lo_half, 0.0, q).astype(bf16)
        qaug_ref[0:tq, LANES:2 * LANES] = jnp.where(lane < XCOLS, qx, 0.0).astype(bf16)
        qaug_ref[tq:2 * tq, LANES:2 * LANES] = jnp.where((lane >= XCOLS) & (lane < 2 * XCOLS), qx, 0.0).astype(bf16)

        _flash_init(m_ref, acc_ref)
        _flash_loop(0, i, logits, values, s_ref, m_ref, acc_ref, last_fix=lambda s: s + tri_ref[...])
        o_ref[0, rows, :] = _flash_pair(acc_ref[0:tq, :], acc_ref[tq:2 * tq, :], lo_half).astype(o_ref.dtype)
        return carry

    lax.fori_loop(0, seq // tq, query_tile, 0)


def _fox(qkv, qx, kx):
    b, _, s, _ = qkv.shape
    t = TK_FOX
    assert t == TQ_FOX
    r = np.arange(2 * TQ_FOX)[:, None] % TQ_FOX
    tri = jnp.asarray(np.where(np.arange(t)[None, :] <= r, 0.0, NEG_INF), f32)
    base = QKV_TILES_FOX
    return pl.pallas_call(
        _fox_kernel,
        grid=(b, FOX_PAIRS),
        in_specs=[pl.BlockSpec((1, 1, s, LANES), lambda bi, p: (bi, base + p, 0, 0)),
                  pl.BlockSpec((1, 1, s, LANES), lambda bi, p: (bi, base + FOX_PAIRS + p, 0, 0)),
                  pl.BlockSpec((1, 1, s, LANES), lambda bi, p: (bi, base + 2 * FOX_PAIRS + p, 0, 0)),
                  pl.BlockSpec((1, 1, s, LANES), lambda bi, p: (bi, p, 0, 0)),
                  pl.BlockSpec((1, 1, s, LANES), lambda bi, p: (bi, p, 0, 0)),
                  pl.BlockSpec((2 * TQ_FOX, t), lambda bi, p: (0, 0))],
        out_specs=pl.BlockSpec((1, s, LANES), lambda bi, p: (bi, 0, p)),
        out_shape=jax.ShapeDtypeStruct((b, s, FOX_W), bf16),
        scratch_shapes=[pltpu.VMEM((2 * TQ_FOX, 2 * LANES), bf16),
                        pltpu.VMEM((s, 2 * LANES), bf16),
                        pltpu.VMEM((2 * TQ_FOX, LANES), f32),
                        pltpu.VMEM((2 * TQ_FOX, LANES), f32),
                        pltpu.VMEM((2, 2 * TQ_FOX, t), f32),
                        pltpu.VMEM((2, s, LANES), bf16)],
        compiler_params=_cparams(("parallel", "parallel")),
        name="fox",
    )(qkv, qkv, qkv, qx, kx, tri)


def _merge_kernel(yn_ref, yf_ref, xb_ref, x_ref, wn_ref, wf_ref, wmg_ref, wo_ref, g_ref, b_ref,
                  xo_ref, xbo_ref):
    mg = _dot(xb_ref[...], wmg_ref[...])
    merged = (_sigmoid(mg[:, 0:D_MODEL]) * _dot(yn_ref[...], wn_ref[...])
              + _sigmoid(mg[:, D_MODEL:2 * D_MODEL]) * _dot(yf_ref[...], wf_ref[...]))
    hmix = _dot(merged.astype(bf16), wo_ref[...])
    xn = _layer_norm(DN_ALPHA * x_ref[...] + hmix, g_ref[...], b_ref[...])
    xo_ref[...] = xn
    xbo_ref[...] = xn.astype(bf16)


def _merge(yn, yf, xb, x, wn, wf, wmg, wo, g, bb):
    m = x.shape[0]
    tm = TM_MERGE
    row = lambda w: pl.BlockSpec((tm, w), lambda i: (i, 0))
    full = lambda a: pl.BlockSpec(a.shape, lambda i: (0, 0))
    return pl.pallas_call(
        _merge_kernel,
        grid=(m // tm,),
        in_specs=[row(NSA_Q_W), row(FOX_W), row(D_MODEL), row(D_MODEL),
                  full(wn), full(wf), full(wmg), full(wo), full(g), full(bb)],
        out_specs=[row(D_MODEL), row(D_MODEL)],
        out_shape=[jax.ShapeDtypeStruct((m, D_MODEL), f32), jax.ShapeDtypeStruct((m, D_MODEL), bf16)],
        compiler_params=_cparams(("parallel",)),
        name="merge",
    )(yn, yf, xb, x, wn, wf, wmg, wo, g, bb)


def _ffn_kernel(xb_ref, x_ref, wg_ref, wu_ref, wd_ref, g_ref, b_ref, xo_ref, xbo_ref, acc_ref, *, nf):
    f = pl.program_id(1)

    @pl.when(f == 0)
    def _():
        acc_ref[...] = jnp.zeros_like(acc_ref)

    xb = xb_ref[...]
    gate = _dot(xb, wg_ref[...])
    up = _dot(xb, wu_ref[...])
    act = (gate * _sigmoid(gate) * up).astype(bf16)
    acc_ref[...] += _dot(act, wd_ref[...])

    @pl.when(f == nf - 1)
    def _():
        xn = _layer_norm(DN_ALPHA * x_ref[...] + acc_ref[...], g_ref[...], b_ref[...])
        xo_ref[...] = xn
        xbo_ref[...] = xn.astype(bf16)


def _ffn(xb, x, wg, wu, wd, g, bb):
    m = x.shape[0]
    dff = wg.shape[1]
    tm, tf = TM_FFN, TF_FFN
    nf = dff // tf
    row = pl.BlockSpec((tm, D_MODEL), lambda i, f: (i, 0))
    vec = pl.BlockSpec((1, D_MODEL), lambda i, f: (0, 0))
    return pl.pallas_call(
        functools.partial(_ffn_kernel, nf=nf),
        grid=(m // tm, nf),
        in_specs=[row, row,
                  pl.BlockSpec((D_MODEL, tf), lambda i, f: (0, f)),
                  pl.BlockSpec((D_MODEL, tf), lambda i, f: (0, f)),
                  pl.BlockSpec((tf, D_MODEL), lambda i, f: (f, 0)),
                  vec, vec],
        out_specs=[row, row],
        out_shape=[jax.ShapeDtypeStruct((m, D_MODEL), f32), jax.ShapeDtypeStruct((m, D_MODEL), bf16)],
        scratch_shapes=[pltpu.VMEM((tm, D_MODEL), f32)],
        compiler_params=_cparams(("parallel", "arbitrary")),
        name="ffn",
    )(xb, x, wg, wu, wd, g, bb)


def _router_kernel(x_ref, r_ref, tri_ref, gate_ref, rank_ref, *, seq):
    tb = tri_ref.shape[0]
    x = x_ref[0]
    xh = x.astype(bf16)
    xl = (x - xh.astype(f32)).astype(bf16)
    r = r_ref[...]
    rh = r.astype(bf16)
    rl = (r - rh.astype(f32)).astype(bf16)
    logits = _dot(xh, rh) + _dot(xh, rl) + _dot(xl, rh)
    lane = lax.broadcasted_iota(i32, (seq, LANES), 1).astype(f32)
    low = -3.0e38
    lg = jnp.where(lane < N_EXPERTS, logits, low)
    m1 = jnp.max(lg, axis=-1, keepdims=True)
    i1 = jnp.min(jnp.where(lg == m1, lane, float(LANES)), axis=-1, keepdims=True)
    lg2 = jnp.where(lane == i1, low, lg)
    m2 = jnp.max(lg2, axis=-1, keepdims=True)
    i2 = jnp.min(jnp.where(lg2 == m2, lane, float(LANES)), axis=-1, keepdims=True)
    e2 = jnp.exp(m2 - m1)
    den = 1.0 + e2
    gate_ref[0] = jnp.where(lane == i1, 1.0 / den, jnp.where(lane == i2, e2 / den, 0.0))
    sel = (lane == i1) | (lane == i2)
    selb = jnp.where(sel, 1.0, 0.0).astype(bf16)
    carry = jnp.zeros((1, LANES), f32)
    for blk in range(seq // tb):
        sl = slice(blk * tb, (blk + 1) * tb)
        c = _dot(tri_ref[...], selb[sl]) + carry
        carry = c[tb - 1:tb, :]
        rank_ref[0, sl, :] = jnp.where(sel[sl], c - 1.0, -1.0)


def _router(x3, router_pad, tri):
    b, s, _ = x3.shape
    return pl.pallas_call(
        functools.partial(_router_kernel, seq=s),
        grid=(b,),
        in_specs=[pl.BlockSpec((1, s, D_MODEL), lambda i: (i, 0, 0)),
                  pl.BlockSpec((D_MODEL, LANES), lambda i: (0, 0)),
                  pl.BlockSpec(tri.shape, lambda i: (0, 0))],
        out_specs=[pl.BlockSpec((1, s, LANES), lambda i: (i, 0, 0)),
                   pl.BlockSpec((1, s, LANES), lambda i: (i, 0, 0))],
        out_shape=[jax.ShapeDtypeStruct((b, s, LANES), f32),
                   jax.ShapeDtypeStruct((b, s, LANES), f32)],
        compiler_params=_cparams(("parallel",)),
        name="router",
    )(x3, router_pad, tri)


def _moe_kernel(tot_ref, tot_al_ref, off_ref, npc_ref, xb_ref, rankt_ref, gatet_ref, wg_hbm, wu_hbm, wd_hbm,
                y_ref, xg_ref, acc_ref, wg_buf, wu_buf, wd_buf, sem, *, seq, nf, layer):
    tr = TR_MOE
    half = tr // 2
    nsubc = seq // SUB_MOE
    c = pl.program_id(0)
    e = pl.program_id(1)
    ce = c * N_EXPERTS + e
    last_step = pl.num_programs(0) * N_EXPERTS - 1

    def weight_copies(expert, f, slot):
        cols = pl.ds(f * TF_MOE, TF_MOE)
        return (pltpu.make_async_copy(wg_hbm.at[layer, expert, :, cols], wg_buf.at[slot], sem.at[0, slot]),
                pltpu.make_async_copy(wu_hbm.at[layer, expert, :, cols], wu_buf.at[slot], sem.at[1, slot]),
                pltpu.make_async_copy(wd_hbm.at[layer, expert, cols, :], wd_buf.at[slot], sem.at[2, slot]))

    def start_tile(expert, f, slot):
        for cp in weight_copies(expert, f, slot):
            cp.start()

    def wait_tile(expert, f, slot):
        for cp in weight_copies(expert, f, slot):
            cp.wait()
    tot = tot_ref[ce]
    rem = tot % tr
    nfull = tot // tr + jnp.where(rem > half, 1, 0)
    has_tail = (rem > 0) & (rem <= half)
    tail0 = pl.multiple_of(nfull * tr, half)

    def pieces(fn):
        for q in range(nsubc):
            base = off_ref[ce * nsubc + q]

            def body(p, carry, q=q, base=base):
                fn(q, pl.multiple_of(base + p * PIECE_MOE, ROW_ALIGN))
                return carry
            lax.fori_loop(0, npc_ref[ce * nsubc + q], body, 0)

    def onehot_rows(q, r0):
        rk = rankt_ref[0, pl.ds(e, 1), q * SUB_MOE:(q + 1) * SUB_MOE]
        want = (r0 + lax.broadcasted_iota(i32, (PIECE_MOE, SUB_MOE), 0)).astype(f32)
        return rk == want

    def gather(q, r0):
        p = jnp.where(onehot_rows(q, r0), 1.0, 0.0).astype(bf16)
        xg_ref[pl.ds(r0, PIECE_MOE), :] += _dot(p, xb_ref[0, q * SUB_MOE:(q + 1) * SUB_MOE, :]).astype(bf16)

    def clear(r0, nr):
        xg_ref[pl.ds(r0, nr), :] = jnp.zeros((nr, D_MODEL), bf16)
        acc_ref[pl.ds(r0, nr), :] = jnp.zeros((nr, D_MODEL), f32)

    def hidden(slot):
        def fn(r0, nr):
            xs = xg_ref[pl.ds(r0, nr), :]
            gate = _dot(xs, wg_buf[slot])
            up = _dot(xs, wu_buf[slot])
            act = (gate * _sigmoid(gate) * up).astype(bf16)
            acc_ref[pl.ds(r0, nr), :] += _dot(act, wd_buf[slot])
        return fn

    def combine(q, r0):
        grow = gatet_ref[0, pl.ds(e, 1), q * SUB_MOE:(q + 1) * SUB_MOE]
        hit = onehot_rows(q, r0)
        w = jnp.sum(jnp.where(hit, grow, 0.0), axis=-1, keepdims=True)
        z = (acc_ref[pl.ds(r0, PIECE_MOE), :] * w).astype(bf16)
        p = jnp.where(hit, 1.0, 0.0).astype(bf16)
        y_ref[0, q * SUB_MOE:(q + 1) * SUB_MOE, :] += lax.dot_general(
            p, z, (((0,), (0,)), ((), ())), preferred_element_type=f32)

    def tiles(fn):
        def body(s, carry):
            fn(pl.multiple_of(s * tr, tr), tr)
            return carry
        lax.fori_loop(0, nfull, body, 0)

        @pl.when(has_tail)
        def _():
            fn(tail0, half)

    def zero(j, carry):
        y_ref[0, pl.ds(j * tr, tr), :] = jnp.zeros((tr, D_MODEL), f32)
        return carry

    @pl.when(ce == 0)
    def _():
        start_tile(e, 0, 0)

    lax.fori_loop(0, jnp.where(e == 0, seq // tr, 0), zero, 0)
    tiles(clear)
    clear(pl.multiple_of(tot_al_ref[ce], ROW_ALIGN), tr)
    pieces(gather)

    next_e = (e + 1) % N_EXPERTS
    for f in range(nf):
        slot = f % 2
        wait_tile(e, f, slot)
        if f + 1 < nf:
            start_tile(e, f + 1, 1 - slot)
        else:
            @pl.when(ce < last_step)
            def _():
                start_tile(next_e, 0, 0)
        tiles(hidden(slot))

    pieces(combine)


def _moe(xb3, gate, rank, wg, wu, wd, layer):
    b, s, _ = xb3.shape
    dff = wg.shape[3]
    nf = dff // TF_MOE
    nsubc = s // SUB_MOE
    rk = rank[:, :, :N_EXPERTS]
    cnt = jnp.sum((rk >= 0.0).reshape(b, nsubc, SUB_MOE, N_EXPERTS), axis=2).astype(i32)
    first = jnp.cumsum(cnt, axis=1) - cnt
    off = first // ROW_ALIGN * ROW_ALIGN
    npc = jnp.where(cnt > 0, (first - off + cnt + PIECE_MOE - 1) // PIECE_MOE, 0)
    post = rk.transpose(0, 2, 1)
    gatet = gate[:, :, :N_EXPERTS].transpose(0, 2, 1)
    tot = jnp.sum(cnt, axis=1).reshape(-1)
    tot_al = (tot + ROW_ALIGN - 1) // ROW_ALIGN * ROW_ALIGN
    off_flat = off.transpose(0, 2, 1).reshape(-1)
    npc = npc.transpose(0, 2, 1).reshape(-1)
    buf_rows = s + ROW_ALIGN + PIECE_MOE + TR_MOE
    grid_spec = pltpu.PrefetchScalarGridSpec(
        num_scalar_prefetch=4,
        grid=(b, N_EXPERTS),
        in_specs=[pl.BlockSpec((1, s, D_MODEL), lambda c, e, *_: (c, 0, 0)),
                  pl.BlockSpec((1, N_EXPERTS, s), lambda c, e, *_: (c, 0, 0)),
                  pl.BlockSpec((1, N_EXPERTS, s), lambda c, e, *_: (c, 0, 0)),
                  pl.BlockSpec(memory_space=pl.ANY),
                  pl.BlockSpec(memory_space=pl.ANY),
                  pl.BlockSpec(memory_space=pl.ANY)],
        out_specs=pl.BlockSpec((1, s, D_MODEL), lambda c, e, *_: (c, 0, 0)),
        scratch_shapes=[pltpu.VMEM((buf_rows, D_MODEL), bf16), pltpu.VMEM((buf_rows, D_MODEL), f32),
                        pltpu.VMEM((2, D_MODEL, TF_MOE), bf16), pltpu.VMEM((2, D_MODEL, TF_MOE), bf16),
                        pltpu.VMEM((2, TF_MOE, D_MODEL), bf16), pltpu.SemaphoreType.DMA((3, 2))],
    )
    assert nf % 2 == 0
    return pl.pallas_call(
        functools.partial(_moe_kernel, seq=s, nf=nf, layer=layer),
        grid_spec=grid_spec,
        out_shape=jax.ShapeDtypeStruct((b, s, D_MODEL), f32),
        compiler_params=_cparams(("arbitrary", "arbitrary")),
        name="moe",
    )(tot, tot_al, off_flat, npc, xb3, post, gatet, wg, wu, wd)


def _resln_kernel(x_ref, y_ref, g_ref, b_ref, xo_ref, xbo_ref):
    xn = _layer_norm(DN_ALPHA * x_ref[...] + y_ref[...], g_ref[...], b_ref[...])
    xo_ref[...] = xn
    xbo_ref[...] = xn.astype(bf16)


def _resln(x, y, g, bb):
    m = x.shape[0]
    tm = TM_FFN
    row = pl.BlockSpec((tm, D_MODEL), lambda i: (i, 0))
    vec = pl.BlockSpec((1, D_MODEL), lambda i: (0, 0))
    return pl.pallas_call(
        _resln_kernel,
        grid=(m // tm,),
        in_specs=[row, row, vec, vec],
        out_specs=[row, row],
        out_shape=[jax.ShapeDtypeStruct((m, D_MODEL), f32), jax.ShapeDtypeStruct((m, D_MODEL), bf16)],
        compiler_params=_cparams(("parallel",)),
        name="resln",
    )(x, y, g, bb)


def _t5_bucket(dist):
    n = jnp.maximum(dist, 0)
    max_exact = REL_BUCKETS // 2
    large = max_exact + (jnp.log(jnp.maximum(n, 1).astype(f32) / max_exact)
                         / math.log(REL_MAX_DIST / max_exact) * (REL_BUCKETS - max_exact)).astype(i32)
    large = jnp.minimum(large, REL_BUCKETS - 1)
    return jnp.where(n < max_exact, n, large)


def _bias_of_dist(rel_bias, dist):
    onehot = (_t5_bucket(dist)[None] == jnp.arange(REL_BUCKETS).reshape((-1,) + (1,) * dist.ndim)).astype(f32)
    return LOG2E * jnp.einsum("kh,k...->h...", rel_bias.astype(f32), onehot, precision=lax.Precision.HIGHEST)


def _bias_tables(rel_bias, seq):
    t = T_ATT
    rows = NSA_GROUP * t
    ncmp = seq // CMP_STRIDE
    d0 = jnp.arange(t)[:, None] - jnp.arange(t)[None, :]
    offs = jnp.array([0, t, 2 * t, WINDOW]).reshape(4, 1, 1)
    kinds = _bias_of_dist(rel_bias, offs + d0[None])
    mask = jnp.stack([d0 >= 0, d0 == d0, d0 == d0, d0 < 0])
    kinds = jnp.where(mask[None], kinds, NEG_INF)
    order = np.array(NSA_ROW_ORDER)
    tbias = kinds.reshape(NSA_KV_HEADS, NSA_GROUP, 4, t, t)[:, order].transpose(0, 2, 1, 3, 4)
    tbias = tbias.reshape(NSA_KV_HEADS, 4, rows, t)
    cend = jnp.arange(ncmp) * CMP_STRIDE + CMP_BLOCK - 1
    dist_c = jnp.arange(seq)[:, None] - cend[None, :]
    bc = jnp.where(dist_c >= 0, _bias_of_dist(rel_bias, dist_c), NEG_INF)
    bc = bc.reshape(NSA_KV_HEADS, NSA_GROUP, seq // t, t, ncmp)[:, order].transpose(0, 2, 4, 1, 3)
    return tbias, bc.reshape(NSA_KV_HEADS, (seq // t) * ncmp, rows)


def _selection_constants(seq):
    ncmp = seq // CMP_STRIDE
    nsel = seq // SEL_BLOCK
    c0 = np.arange(ncmp)[:, None] * CMP_STRIDE
    s0 = np.arange(LANES)[None, :] * SEL_BLOCK
    ov = np.maximum(np.minimum(c0 + CMP_BLOCK, s0 + SEL_BLOCK) - np.maximum(c0, s0), 0) / CMP_BLOCK
    ov[ncmp - 1, :] = 0.0
    ov[:, nsel:] = 0.0
    et = (np.arange(seq)[:, None] // SEL_BLOCK == np.arange(LANES)[None, :]).astype(np.float32)
    eg = np.zeros((LANES, 3 * (NSA_GROUP // 2) * LANES), np.float32)
    for br in range(3):
        for g in range(NSA_GROUP):
            c0 = (br * (NSA_GROUP // 2) + g // 2) * LANES + (g % 2) * HEAD_DIM
            eg[br * NSA_GROUP + g, c0:c0 + HEAD_DIM] = 1.0
    return jnp.asarray(ov.T, bf16), jnp.asarray(et, bf16), jnp.asarray(eg, bf16)


def _fox_placement():
    xw = FOX_PAIRS * LANES
    pq = np.zeros((3, LANES, xw), np.float32)
    pk = np.zeros((3, LANES, xw), np.float32)
    oq = np.zeros((1, xw), np.float32)
    ok = np.zeros((1, xw), np.float32)
    for p in range(FOX_PAIRS):
        for hh in range(2):
            src = FGATE_LANE + 2 * p + hh
            base = p * LANES + hh * XCOLS
            for part in range(3):
                pk[part, src, base + part] = -1.0
                pq[part, src, base + 3 + part] = 1.0
                oq[0, base + part] = 1.0
                ok[0, base + 3 + part] = 1.0
    return jnp.asarray(pq, bf16), jnp.asarray(pk, bf16), jnp.asarray(oq), jnp.asarray(ok)


def _layer_weights(w_in, layer_pe, w1, w2, f_bias):
    offs = np.cumsum((NSA_Q_W, 6 * 2 * HEAD_DIM, 3 * NSA_HEADS, 3 * FOX_W, FOX_HEADS, 2 * D_MODEL))
    kv0, g0, fx0, ff0, mg0 = offs[0], offs[1], offs[2], offs[3], offs[4]
    scale = HEAD_DIM ** -0.5 * LOG2E
    kvw = NSA_KV_HEADS * HEAD_DIM
    w_kv = w_in[:, kv0:g0]
    w_kvdup = jnp.repeat(w_kv[:, 2 * kvw:].reshape(D_MODEL, 4 * NSA_KV_HEADS, 1, HEAD_DIM), 2, axis=2)
    w_kvdup = w_kvdup.reshape(D_MODEL, 8 * kvw)
    w_qkv = jnp.concatenate([w_in[:, :NSA_Q_W] * scale, w_kvdup, w_in[:, fx0:fx0 + FOX_W] * scale,
                             w_in[:, fx0 + FOX_W:ff0]], axis=1).astype(bf16)
    zeros = lambda n: jnp.zeros((D_MODEL, n), w_in.dtype)
    ng = 3 * NSA_GROUP
    w_g = w_in[:, g0:fx0].reshape(D_MODEL, NSA_KV_HEADS, NSA_GROUP, 3).transpose(0, 1, 3, 2)
    gate_cols = lambda h: w_g[:, h].reshape(D_MODEL, ng)
    w_aux = jnp.concatenate([w_kv[:, :2 * kvw],
                             gate_cols(0), zeros(FGATE_LANE - ng), w_in[:, ff0:mg0],
                             zeros(LANES - FGATE_LANE - FOX_HEADS),
                             gate_cols(1), zeros(LANES - ng)], axis=1).astype(bf16)
    fb_row = jnp.zeros((1, LANES), f32).at[0, FGATE_LANE:FGATE_LANE + FOX_HEADS].set(f_bias.astype(f32))
    pe2 = jnp.tile(layer_pe.astype(f32), (1, 1, NSA_KV_HEADS))
    w1r = w1.reshape(2, CMP_BLOCK, HEAD_DIM, CMP_HIDDEN).astype(bf16)
    zero = jnp.zeros_like(w1r)
    w1bd = jnp.concatenate([jnp.concatenate([w1r, zero], axis=-1),
                            jnp.concatenate([zero, w1r], axis=-1)], axis=-2)
    w2d = jnp.concatenate([w2, w2], axis=-1).astype(bf16)
    return (w_qkv, w_aux, w_in[:, mg0:].astype(bf16), fb_row, pe2, w1bd[:, :CMP_STRIDE], w1bd[:, CMP_STRIDE:],
            w2d)


def kernel(x, w_in, nsa_cmp_pe, nsa_cmp_w1, nsa_cmp_w2, fox_f_bias, w_nsa_branch, w_fox_branch, w_out,
           rel_bias, ln1_g, ln1_b, ln2_g, ln2_b, dense_w_gate, dense_w_up, dense_w_down, moe_router,
           moe_w_gate, moe_w_up, moe_w_down):
    b, s, d = x.shape
    m = b * s
    tbias, bias_c = _bias_tables(rel_bias, s)
    ov, et, eg = _selection_constants(s)
    pq, pk, oq, ok = _fox_placement()
    tri128 = jnp.asarray(np.tril(np.ones((LANES, LANES), np.float32)), bf16)
    tri256 = jnp.asarray(np.tril(np.ones((256, 256), np.float32)), bf16)

    moe_wg, moe_wu, moe_wd = moe_w_gate.astype(bf16), moe_w_up.astype(bf16), moe_w_down.astype(bf16)
    xf = x.reshape(m, d).astype(f32)
    xb = xf.astype(bf16)
    for layer in range(DEPTH):
        w_qkv, w_aux, w_mg, fb_row, pe2, w1t, w1b, w2d = _layer_weights(
            w_in[layer], nsa_cmp_pe[layer], nsa_cmp_w1[layer], nsa_cmp_w2[layer], fox_f_bias[layer])
        qkv, aux = _proj(xb, w_qkv, w_aux, b)
        aux = aux.reshape(b, s, AUX_W)
        gates, qx, kx = _gateprep(aux, fb_row, tri128, pq, pk, oq, ok)
        cmpkv = _compress(aux, pe2, w1t, w1b, w2d)
        y_nsa = _nsa(qkv, cmpkv, gates, bias_c, tbias, ov, et, eg).reshape(m, NSA_Q_W)
        y_fox = _fox(qkv, qx, kx).reshape(m, FOX_W)
        xf, xb = _merge(y_nsa, y_fox, xb, xf, w_nsa_branch[layer].astype(bf16),
                        w_fox_branch[layer].astype(bf16), w_mg, w_out[layer].astype(bf16),
                        ln1_g[layer].reshape(1, d), ln1_b[layer].reshape(1, d))
        j = layer // 2
        g2, b2 = ln2_g[layer].reshape(1, d), ln2_b[layer].reshape(1, d)
        if layer % 2 == 0:
            xf, xb = _ffn(xb, xf, dense_w_gate[j].astype(bf16), dense_w_up[j].astype(bf16),
                          dense_w_down[j].astype(bf16), g2, b2)
        else:
            router_pad = jnp.zeros((d, LANES), f32).at[:, :N_EXPERTS].set(moe_router[j].astype(f32))
            gate, rank = _router(xf.reshape(b, s, d), router_pad, tri256)
            y = _moe(xb.reshape(b, s, d), gate, rank, moe_wg, moe_wu, moe_wd, j)
            xf, xb = _resln(xf, y.reshape(m, d), g2, b2)
    return xf.reshape(b, s, d).astype(x.dtype)
```

```python
import functools
import math

import numpy as np
import jax
import jax.numpy as jnp
from jax import lax
from jax.experimental import pallas as pl
from jax.experimental.pallas import tpu as pltpu

f32 = jnp.float32
bf16 = jnp.bfloat16
i32 = jnp.int32

D_MODEL = 1024
HEAD_DIM = 64
LANES = 128
NSA_HEADS = 8
NSA_KV_HEADS = 2
NSA_GROUP = NSA_HEADS // NSA_KV_HEADS
NSA_ROW_ORDER = (0, 2, 1, 3)
FOX_HEADS = 8
FOX_PAIRS = FOX_HEADS // 2
CMP_BLOCK = 32
CMP_STRIDE = 16
CMP_HIDDEN = 128
SEL_BLOCK = 64
N_SELECT = 16
N_LOCAL_SEL = 2
WINDOW = 512
REL_BUCKETS = 32
REL_MAX_DIST = 128
N_EXPERTS = 8
DEPTH = 4
DN_ALPHA = (2 * DEPTH) ** 0.25
LN_EPS = 1e-5
FORCED_SCORE = 1e4
NEG_INF = -1e30
LOG2E = math.log2(math.e)

NSA_Q_W = NSA_HEADS * HEAD_DIM
FOX_W = FOX_HEADS * HEAD_DIM
QKV_TILES_KV = NSA_Q_W // LANES
QKV_TILES_FOX = QKV_TILES_KV + 4 * NSA_KV_HEADS
QKV_W = (QKV_TILES_FOX + 3 * FOX_PAIRS) * LANES
AUX_W = 4 * LANES
FGATE_LANE = 24
XCOLS = 6

T_ATT = 256
TQ_FOX = 512
TK_FOX = 512
TM_PROJ = 1024
TM_MERGE = 1024
TM_FFN = 512
TF_FFN = 2816
TR_MOE = 256
SUB_MOE = 512
PIECE_MOE = 192
ROW_ALIGN = 16
TF_MOE = 896
VMEM_LIMIT = 56 * 1024 * 1024


def _cparams(sem):
    return pltpu.CompilerParams(dimension_semantics=sem, vmem_limit_bytes=VMEM_LIMIT)


def _dot(a, b):
    return jnp.dot(a, b, preferred_element_type=f32)


def _dot_nt(a, b):
    return lax.dot_general(a, b, (((1,), (1,)), ((), ())), preferred_element_type=f32)


def _sigmoid(x):
    return 1.0 / (1.0 + jnp.exp(-x))


def _layer_norm(z, g, b):
    mu = jnp.mean(z, axis=-1, keepdims=True)
    zc = z - mu
    var = jnp.mean(zc * zc, axis=-1, keepdims=True)
    return zc * lax.rsqrt(var + LN_EPS) * g + b


def _split3(x):
    hi = x.astype(bf16)
    r1 = x - hi.astype(f32)
    mid = r1.astype(bf16)
    lo = (r1 - mid.astype(f32)).astype(bf16)
    return hi, mid, lo


def _proj_kernel(x_ref, w_ref, wa_ref, o_ref, oa_ref):
    x = x_ref[...]
    res = _dot(x, w_ref[...])
    for j in range(o_ref.shape[1]):
        o_ref[0, j] = res[:, j * LANES:(j + 1) * LANES].astype(o_ref.dtype)
    oa_ref[...] = _dot(x, wa_ref[...])


def _proj(xb, w, wa, batch):
    m, k = xb.shape
    n, na = w.shape[1], wa.shape[1]
    nb = m // batch // TM_PROJ
    return pl.pallas_call(
        _proj_kernel,
        grid=(m // TM_PROJ,),
        in_specs=[pl.BlockSpec((TM_PROJ, k), lambda i: (i, 0)),
                  pl.BlockSpec((k, n), lambda i: (0, 0)),
                  pl.BlockSpec((k, na), lambda i: (0, 0))],
        out_specs=[pl.BlockSpec((1, n // LANES, TM_PROJ, LANES), lambda i: (i // nb, 0, i % nb, 0)),
                   pl.BlockSpec((TM_PROJ, na), lambda i: (i, 0))],
        out_shape=[jax.ShapeDtypeStruct((batch, n // LANES, m // batch, LANES), bf16),
                   jax.ShapeDtypeStruct((m, na), f32)],
        compiler_params=_cparams(("parallel",)),
        name="proj",
    )(xb, w, wa)


def _gateprep_kernel(a_ref, fb_ref, tri_ref, pq_ref, pk_ref, oq_ref, ok_ref, g_ref, qx_ref, kx_ref, *, seq):
    tb = LANES
    tri = tri_ref[...]
    local = []
    for blk in range(seq // tb):
        sl = slice(blk * tb, (blk + 1) * tb)
        va = a_ref[0, sl, 0:LANES]
        g_ref[0, 0, sl, :] = _sigmoid(va)
        g_ref[0, 1, sl, :] = _sigmoid(a_ref[0, sl, LANES:2 * LANES])
        z = va + fb_ref[...]
        logf = jnp.minimum(z, 0.0) - jnp.log1p(jnp.exp(-jnp.abs(z)))
        hi, mid, lo = _split3(logf)
        local.append(_dot(tri, hi) + _dot(tri, mid) + _dot(tri, lo))
    carry = jnp.zeros((1, LANES), f32)
    for blk in range(seq // tb):
        sl = slice(blk * tb, (blk + 1) * tb)
        c = local[blk] + carry
        carry = c[tb - 1:tb, :]
        chi, cmid, clo = _split3(c * LOG2E)
        qx = _dot(chi, pq_ref[0]) + _dot(cmid, pq_ref[1]) + _dot(clo, pq_ref[2]) + oq_ref[...]
        kx = _dot(chi, pk_ref[0]) + _dot(cmid, pk_ref[1]) + _dot(clo, pk_ref[2]) + ok_ref[...]
        for p in range(FOX_PAIRS):
            qx_ref[0, p, sl, :] = qx[:, p * LANES:(p + 1) * LANES].astype(bf16)
            kx_ref[0, p, sl, :] = kx[:, p * LANES:(p + 1) * LANES].astype(bf16)


def _gateprep(aux, fb_row, tri, pq, pk, oq, ok):
    b, s, _ = aux.shape
    const2 = lambda a: pl.BlockSpec(a.shape, lambda i: (0, 0))
    const3 = lambda a: pl.BlockSpec(a.shape, lambda i: (0, 0, 0))
    return pl.pallas_call(
        functools.partial(_gateprep_kernel, seq=s),
        grid=(b,),
        in_specs=[pl.BlockSpec((1, s, 2 * LANES), lambda i: (i, 0, 1)),
                  const2(fb_row), const2(tri), const3(pq), const3(pk), const2(oq), const2(ok)],
        out_specs=[pl.BlockSpec((1, 2, s, LANES), lambda i: (i, 0, 0, 0)),
                   pl.BlockSpec((1, FOX_PAIRS, s, LANES), lambda i: (i, 0, 0, 0)),
                   pl.BlockSpec((1, FOX_PAIRS, s, LANES), lambda i: (i, 0, 0, 0))],
        out_shape=[jax.ShapeDtypeStruct((b, 2, s, LANES), f32),
                   jax.ShapeDtypeStruct((b, FOX_PAIRS, s, LANES), bf16),
                   jax.ShapeDtypeStruct((b, FOX_PAIRS, s, LANES), bf16)],
        compiler_params=_cparams(("parallel",)),
        name="gateprep",
    )(aux, fb_row, tri, pq, pk, oq, ok)


def _gelu_tanh(x):
    c = math.sqrt(2.0 / math.pi)
    return x * (0.5 * (1.0 + jnp.tanh(c * (x + 0.044715 * (x * x * x)))))


def _compress_kernel(a_ref, pe_ref, w1t_ref, w1b_ref, w2_ref, o_ref, *, nhalf):
    top = jnp.zeros((nhalf, NSA_KV_HEADS * CMP_HIDDEN), f32)
    bot = jnp.zeros((nhalf, NSA_KV_HEADS * CMP_HIDDEN), f32)
    for l in range(CMP_STRIDE):
        rows = a_ref[0, pl.ds(l, nhalf, stride=CMP_STRIDE), :]
        top = top + _dot((rows + pe_ref[0, l:l + 1, :]).astype(bf16), w1t_ref[0, l])
        bot = bot + _dot((rows + pe_ref[0, CMP_STRIDE + l:CMP_STRIDE + l + 1, :]).astype(bf16), w1b_ref[0, l])
    pre = top + pltpu.roll(bot, nhalf - 1, 0)
    act = _gelu_tanh(pre).astype(bf16)
    for h in range(NSA_KV_HEADS):
        o_ref[0, 0, h] = _dot(act[:, h * CMP_HIDDEN:(h + 1) * CMP_HIDDEN], w2_ref[0]).astype(o_ref.dtype)


def _compress(aux, pe2, w1t, w1b, w2d):
    b, s, _ = aux.shape
    nhalf = s // CMP_STRIDE
    w1_spec = pl.BlockSpec((1,) + w1t.shape[1:], lambda i, j: (j, 0, 0, 0))
    return pl.pallas_call(
        functools.partial(_compress_kernel, nhalf=nhalf),
        grid=(b, 2),
        in_specs=[pl.BlockSpec((1, s, LANES), lambda i, j: (i, 0, j)),
                  pl.BlockSpec((1, CMP_BLOCK, LANES), lambda i, j: (j, 0, 0)),
                  w1_spec, w1_spec,
                  pl.BlockSpec((1, CMP_HIDDEN, LANES), lambda i, j: (j, 0, 0))],
        out_specs=pl.BlockSpec((1, 1, NSA_KV_HEADS, nhalf, LANES), lambda i, j: (i, j, 0, 0, 0)),
        out_shape=jax.ShapeDtypeStruct((b, 2, NSA_KV_HEADS, nhalf, LANES), bf16),
        compiler_params=_cparams(("parallel", "parallel")),
        name="compress",
    )(aux, pe2, w1t, w1b, w2d)


def _flash_init(m_ref, acc_ref):
    m_ref[...] = jnp.full(m_ref.shape, NEG_INF, f32)
    acc_ref[...] = jnp.zeros(acc_ref.shape, f32)


def _ones_values(v):
    lo_half = lax.broadcasted_iota(i32, v.shape, 1) < HEAD_DIM
    vf = v.astype(f32)
    return jnp.where(lo_half, vf, 1.0).astype(bf16), jnp.where(lo_half, 1.0, vf).astype(bf16)


def _flash_step(s, v_lo, v_hi, m_ref, acc_ref):
    nk = s.shape[1] // LANES
    half = s.shape[0] // 2
    cols = [s[:, c * LANES:(c + 1) * LANES] for c in range(nk)]
    mx = cols[0]
    for c in cols[1:]:
        mx = jnp.maximum(mx, c)
    m_old = m_ref[...]
    m_new = jnp.maximum(m_old, jnp.broadcast_to(jnp.max(mx, axis=-1, keepdims=True), m_old.shape))
    alpha = jnp.exp2(m_old - m_new)
    p = jnp.concatenate([jnp.exp2(c - m_new).astype(bf16) for c in cols], axis=1)
    pv = jnp.concatenate([_dot(p[0:half], v_lo), _dot(p[half:], v_hi)], axis=0)
    acc_ref[...] = alpha * acc_ref[...] + pv
    m_ref[...] = m_new


def _flash_loop(first, last, logits_fn, v_fn, s_ref, m_ref, acc_ref, last_fix=None):
    sa, sb = s_ref.at[0], s_ref.at[1]
    n = last - first + 1
    pairs = (n - 1) // 2

    def step(buf, kt, fix=None):
        s = buf[...] if fix is None else fix(buf[...])
        _flash_step(s, *v_fn(kt), m_ref, acc_ref)

    sa[...] = logits_fn(first)

    def body(j, carry):
        kt = first + 2 * j
        sb[...] = logits_fn(kt + 1)
        step(sa, kt)
        sa[...] = logits_fn(kt + 2)
        step(sb, kt + 1)
        return carry

    lax.fori_loop(0, pairs, body, 0)
    two_left = n - 2 * pairs == 2

    @pl.when(two_left)
    def _():
        sb[...] = logits_fn(last)
        step(sa, last - 1)
        step(sb, last, last_fix)

    @pl.when(jnp.logical_not(two_left))
    def _():
        step(sa, last, last_fix)


def _flash_pair(acc_lo, acc_hi, lo_half):
    num = jnp.where(lo_half, acc_lo, acc_hi)
    den = pltpu.roll(jnp.where(lo_half, acc_hi, acc_lo), HEAD_DIM, 1)
    return num * (1.0 / jnp.maximum(den, 1e-30))


def _nsa_kernel(q_ref, ks_ref, vs_ref, kw_ref, vw_ref, kc_ref, vc_ref, gate_ref, bc_ref, tb_ref, ov_ref,
                et_ref, eg_ref, o_ref, qaug_ref, kaug_ref, m_ref, acc_ref, os_ref, s_ref, vsel_ref,
                vwin_ref, *, seq, ncmp):
    kaug_ref[:, 0:LANES] = ks_ref[0, 0]
    kaug_ref[:, LANES:2 * LANES] = et_ref[...]
    vsel_ref[0], vsel_ref[1] = _ones_values(vs_ref[0, 0])
    vwin_ref[0], vwin_ref[1] = _ones_values(vw_ref[0, 0])

    def query_tile(i, carry):
        _nsa_tile(i, q_ref, kw_ref, kc_ref, vc_ref, gate_ref, bc_ref, tb_ref, ov_ref, eg_ref, o_ref, qaug_ref,
                  kaug_ref, m_ref, acc_ref, os_ref, s_ref, vsel_ref, vwin_ref, seq=seq, ncmp=ncmp)
        return carry

    lax.fori_loop(0, seq // T_ATT, query_tile, 0)


def _nsa_tile(i, q_ref, kw_ref, kc_ref, vc_ref, gate_ref, bc_ref, tb_ref, ov_ref, eg_ref, o_ref, qaug_ref,
              kaug_ref, m_ref, acc_ref, os_ref, s_ref, vsel_ref, vwin_ref, *, seq, ncmp):
    t = T_ATT
    rows = NSA_GROUP * t
    nsel = seq // SEL_BLOCK
    topn = min(N_SELECT, nsel)
    t0 = i * t
    tile_rows = pl.ds(pl.multiple_of(t0, t), t)
    lane = lax.broadcasted_iota(i32, (t, LANES), 1)
    lo_half = lane < HEAD_DIM

    for rb, g in enumerate(NSA_ROW_ORDER):
        qg = q_ref[0, g // 2, tile_rows, :].astype(f32)
        keep = lo_half if g % 2 == 0 else jnp.logical_not(lo_half)
        qaug_ref[rb * t:(rb + 1) * t, 0:LANES] = jnp.where(keep, qg, 0.0).astype(bf16)
    qs = qaug_ref[:, 0:LANES]

    s = _dot_nt(kc_ref[0, 0, 0], qs) + bc_ref[0, pl.ds(pl.multiple_of(i * ncmp, ncmp), ncmp), :]
    sees_any = t0 + (lax.broadcasted_iota(i32, (1, rows), 1) & (t - 1)) >= CMP_BLOCK - 1
    s = s - jnp.max(s, axis=0, keepdims=True)
    e = jnp.where(sees_any, jnp.exp2(s), 0.0)
    p_c = e * (1.0 / jnp.maximum(jnp.sum(e, axis=0, keepdims=True), 1e-30))
    o_c = lax.dot_general(p_c.astype(bf16), vc_ref[0, 0, 0], (((0,), (0,)), ((), ())),
                          preferred_element_type=f32)

    psum = p_c[:, 0:t]
    for g in range(1, NSA_GROUP):
        psum = psum + p_c[:, g * t:(g + 1) * t]
    hi = psum.astype(bf16)
    lo = (psum - hi.astype(f32)).astype(bf16)
    imp = (_dot(ov_ref[...], hi) + _dot(ov_ref[...], lo))[0:nsel]
    blk = lax.broadcasted_iota(i32, (nsel, t), 0)
    cur = (t0 + lax.broadcasted_iota(i32, (nsel, t), 1)) >> 6
    forced = (blk == 0) | ((blk <= cur) & (blk > cur - N_LOCAL_SEL))
    val = jnp.where(forced, FORCED_SCORE, jnp.where(blk <= cur, imp, -1.0))
    ngrp = nsel // 8
    sub = lax.broadcasted_iota(i32, (8, t), 0)
    vals = [val[8 * r:8 * r + 8, :] for r in range(ngrp)]
    cnts = [jnp.zeros((8, t), f32) for _ in range(ngrp)]
    for j in range(nsel):
        vj = val[j:j + 1, :]
        for r in range(ngrp):
            ahead = jnp.where(vj > vals[r], 1.0, 0.0)
            ahead_or_tied = jnp.where(vj >= vals[r], 1.0, 0.0)
            if r < j // 8:
                cnts[r] = cnts[r] + ahead
            elif r > j // 8:
                cnts[r] = cnts[r] + ahead_or_tied
            else:
                cnts[r] = cnts[r] + jnp.where(sub > j % 8, ahead_or_tied, ahead)
    cnt = jnp.concatenate(cnts, axis=0)
    mneg = jnp.where(cnt < topn, 0.0, NEG_INF)
    mneg = jnp.concatenate([mneg, jnp.zeros((LANES - nsel, t), f32)], axis=0).T.astype(bf16)
    for g in range(NSA_GROUP):
        qaug_ref[g * t:(g + 1) * t, LANES:2 * LANES] = mneg

    _flash_init(m_ref, os_ref)

    def sel_logits(kt):
        k = kaug_ref[pl.ds(kt * t, t), :]
        return _dot_nt(qaug_ref[...], k) + tb_ref[0, jnp.minimum(i - kt, 2)]

    def values(v_ref):
        return lambda kt: (v_ref[0, pl.ds(kt * t, t), :], v_ref[1, pl.ds(kt * t, t), :])

    _flash_loop(0, i, sel_logits, values(vsel_ref), s_ref, m_ref, os_ref)

    _flash_init(m_ref, acc_ref)
    nwin = WINDOW // t

    def win_logits(kt):
        d = i - kt
        kind = jnp.where(d == nwin, 3, d)
        return _dot_nt(qs, kw_ref[0, 0, pl.ds(kt * t, t), :]) + tb_ref[0, kind]

    _flash_loop(jnp.maximum(i - nwin, 0), i, win_logits, values(vwin_ref), s_ref, m_ref, acc_ref)

    ghi, gmid, glo = _split3(gate_ref[0, 0, tile_rows, :])
    gexp = _dot(ghi, eg_ref[...]) + _dot(gmid, eg_ref[...]) + _dot(glo, eg_ref[...])
    for j in range(NSA_GROUP // 2):
        ra, rb = NSA_ROW_ORDER.index(2 * j), NSA_ROW_ORDER.index(2 * j + 1)
        out = jnp.zeros((t, LANES), f32)
        sa, sb = slice(ra * t, (ra + 1) * t), slice(rb * t, (rb + 1) * t)
        pairs = (jnp.where(lo_half, o_c[sa], o_c[sb]),
                 _flash_pair(os_ref[sa, :], os_ref[sb, :], lo_half),
                 _flash_pair(acc_ref[sa, :], acc_ref[sb, :], lo_half))
        for br, o_pair in enumerate(pairs):
            col = (br * (NSA_GROUP // 2) + j) * LANES
            out = out + gexp[:, col:col + LANES] * o_pair
        o_ref[0, tile_rows, j * LANES:(j + 1) * LANES] = out.astype(o_ref.dtype)


def _nsa(qkv, cmpkv, gates, bias_c, tbias, ov, et, eg):
    b, _, s, _ = qkv.shape
    ncmp = cmpkv.shape[3]
    t = T_ATT
    rows = NSA_GROUP * t
    qw = NSA_GROUP * HEAD_DIM
    qtiles = qw // LANES
    kv_spec = lambda col: pl.BlockSpec((1, 1, s, LANES), lambda h, bi, col=col: (bi, col + h, 0, 0))
    cmp_spec = lambda kv: pl.BlockSpec((1, 1, 1, ncmp, LANES), lambda h, bi, kv=kv: (bi, kv, h, 0, 0))
    base = QKV_TILES_KV
    return pl.pallas_call(
        functools.partial(_nsa_kernel, seq=s, ncmp=ncmp),
        grid=(NSA_KV_HEADS, b),
        in_specs=[pl.BlockSpec((1, qtiles, s, LANES), lambda h, bi: (bi, h, 0, 0)),
                  kv_spec(base), kv_spec(base + 2), kv_spec(base + 4), kv_spec(base + 6),
                  cmp_spec(0), cmp_spec(1),
                  pl.BlockSpec((1, 1, s, LANES), lambda h, bi: (bi, h, 0, 0)),
                  pl.BlockSpec((1, (s // t) * ncmp, rows), lambda h, bi: (h, 0, 0)),
                  pl.BlockSpec((1, 4, rows, t), lambda h, bi: (h, 0, 0, 0)),
                  pl.BlockSpec((LANES, ncmp), lambda h, bi: (0, 0)),
                  pl.BlockSpec((s, LANES), lambda h, bi: (0, 0)),
                  pl.BlockSpec(eg.shape, lambda h, bi: (0, 0))],
        out_specs=pl.BlockSpec((1, s, qw), lambda h, bi: (bi, 0, h)),
        out_shape=jax.ShapeDtypeStruct((b, s, NSA_Q_W), bf16),
        scratch_shapes=[pltpu.VMEM((rows, 2 * LANES), bf16),
                        pltpu.VMEM((s, 2 * LANES), bf16),
                        pltpu.VMEM((rows, LANES), f32),
                        pltpu.VMEM((rows, LANES), f32),
                        pltpu.VMEM((rows, LANES), f32),
                        pltpu.VMEM((2, rows, t), f32),
                        pltpu.VMEM((2, s, LANES), bf16),
                        pltpu.VMEM((2, s, LANES), bf16)],
        compiler_params=_cparams(("parallel", "parallel")),
        name="nsa",
    )(qkv, qkv, qkv, qkv, qkv, cmpkv, cmpkv, gates, bias_c, tbias, ov, et, eg)


def _fox_kernel(q_ref, k_ref, v_ref, qx_ref, kx_ref, tri_ref, o_ref, qaug_ref, kaug_ref, m_ref, acc_ref, s_ref,
                vaug_ref):
    t = TK_FOX
    tq = TQ_FOX
    seq = k_ref.shape[2]
    lane = lax.broadcasted_iota(i32, (tq, LANES), 1)
    lo_half = lane < HEAD_DIM
    kaug_ref[:, 0:LANES] = k_ref[0, 0]
    kaug_ref[:, LANES:2 * LANES] = kx_ref[0, 0]
    vaug_ref[0], vaug_ref[1] = _ones_values(v_ref[0, 0])

    def logits(kt):
        return _dot_nt(qaug_ref[...], kaug_ref[pl.ds(kt * t, t), :])

    def values(kt):
        return vaug_ref[0, pl.ds(kt * t, t), :], vaug_ref[1, pl.ds(kt * t, t), :]

    def query_tile(i, carry):
        rows = pl.ds(pl.multiple_of(i * tq, tq), tq)
        q = q_ref[0, 0, rows, :].astype(f32)
        qx = qx_ref[0, 0, rows, :].astype(f32)
        qaug_ref[0:tq, 0:LANES] = jnp.where(lo_half, q, 0.0).astype(bf16)
        qaug_ref[tq:2 * tq, 0:LANES] = jnp.where(lo_half, 0.0, q).astype(bf16)
        qaug_ref[0:tq, LANES:2 * LANES] = jnp.where(lane < XCOLS, qx, 0.0).astype(bf16)
        qaug_ref[tq:2 * tq, LANES:2 * LANES] = jnp.where((lane >= XCOLS) & (lane < 2 * XCOLS), qx, 0.0).astype(bf16)

        _flash_init(m_ref, acc_ref)
        _flash_loop(0, i, logits, values, s_ref, m_ref, acc_ref, last_fix=lambda s: s + tri_ref[...])
        o_ref[0, rows, :] = _flash_pair(acc_ref[0:tq, :], acc_ref[tq:2 * tq, :], lo_half).astype(o_ref.dtype)
        return carry

    lax.fori_loop(0, seq // tq, query_tile, 0)


def _fox(qkv, qx, kx):
    b, _, s, _ = qkv.shape
    t = TK_FOX
    assert t == TQ_FOX
    r = np.arange(2 * TQ_FOX)[:, None] % TQ_FOX
    tri = jnp.asarray(np.where(np.arange(t)[None, :] <= r, 0.0, NEG_INF), f32)
    base = QKV_TILES_FOX
    return pl.pallas_call(
        _fox_kernel,
        grid=(b, FOX_PAIRS),
        in_specs=[pl.BlockSpec((1, 1, s, LANES), lambda bi, p: (bi, base + p, 0, 0)),
                  pl.BlockSpec((1, 1, s, LANES), lambda bi, p: (bi, base + FOX_PAIRS + p, 0, 0)),
                  pl.BlockSpec((1, 1, s, LANES), lambda bi, p: (bi, base + 2 * FOX_PAIRS + p, 0, 0)),
                  pl.BlockSpec((1, 1, s, LANES), lambda bi, p: (bi, p, 0, 0)),
                  pl.BlockSpec((1, 1, s, LANES), lambda bi, p: (bi, p, 0, 0)),
                  pl.BlockSpec((2 * TQ_FOX, t), lambda bi, p: (0, 0))],
        out_specs=pl.BlockSpec((1, s, LANES), lambda bi, p: (bi, 0, p)),
        out_shape=jax.ShapeDtypeStruct((b, s, FOX_W), bf16),
        scratch_shapes=[pltpu.VMEM((2 * TQ_FOX, 2 * LANES), bf16),
                        pltpu.VMEM((s, 2 * LANES), bf16),
                        pltpu.VMEM((2 * TQ_FOX, LANES), f32),
                        pltpu.VMEM((2 * TQ_FOX, LANES), f32),
                        pltpu.VMEM((2, 2 * TQ_FOX, t), f32),
                        pltpu.VMEM((2, s, LANES), bf16)],
        compiler_params=_cparams(("parallel", "parallel")),
        name="fox",
    )(qkv, qkv, qkv, qx, kx, tri)


def _merge_kernel(yn_ref, yf_ref, xb_ref, x_ref, wn_ref, wf_ref, wmg_ref, wo_ref, g_ref, b_ref,
                  xo_ref, xbo_ref):
    mg = _dot(xb_ref[...], wmg_ref[...])
    merged = (_sigmoid(mg[:, 0:D_MODEL]) * _dot(yn_ref[...], wn_ref[...])
              + _sigmoid(mg[:, D_MODEL:2 * D_MODEL]) * _dot(yf_ref[...], wf_ref[...]))
    hmix = _dot(merged.astype(bf16), wo_ref[...])
    xn = _layer_norm(DN_ALPHA * x_ref[...] + hmix, g_ref[...], b_ref[...])
    xo_ref[...] = xn
    xbo_ref[...] = xn.astype(bf16)


def _merge(yn, yf, xb, x, wn, wf, wmg, wo, g, bb):
    m = x.shape[0]
    tm = TM_MERGE
    row = lambda w: pl.BlockSpec((tm, w), lambda i: (i, 0))
    full = lambda a: pl.BlockSpec(a.shape, lambda i: (0, 0))
    return pl.pallas_call(
        _merge_kernel,
        grid=(m // tm,),
        in_specs=[row(NSA_Q_W), row(FOX_W), row(D_MODEL), row(D_MODEL),
                  full(wn), full(wf), full(wmg), full(wo), full(g), full(bb)],
        out_specs=[row(D_MODEL), row(D_MODEL)],
        out_shape=[jax.ShapeDtypeStruct((m, D_MODEL), f32), jax.ShapeDtypeStruct((m, D_MODEL), bf16)],
        compiler_params=_cparams(("parallel",)),
        name="merge",
    )(yn, yf, xb, x, wn, wf, wmg, wo, g, bb)


def _ffn_kernel(xb_ref, x_ref, wg_ref, wu_ref, wd_ref, g_ref, b_ref, xo_ref, xbo_ref, acc_ref, *, nf):
    f = pl.program_id(1)

    @pl.when(f == 0)
    def _():
        acc_ref[...] = jnp.zeros_like(acc_ref)

    xb = xb_ref[...]
    gate = _dot(xb, wg_ref[...])
    up = _dot(xb, wu_ref[...])
    act = (gate * _sigmoid(gate) * up).astype(bf16)
    acc_ref[...] += _dot(act, wd_ref[...])

    @pl.when(f == nf - 1)
    def _():
        xn = _layer_norm(DN_ALPHA * x_ref[...] + acc_ref[...], g_ref[...], b_ref[...])
        xo_ref[...] = xn
        xbo_ref[...] = xn.astype(bf16)


def _ffn(xb, x, wg, wu, wd, g, bb):
    m = x.shape[0]
    dff = wg.shape[1]
    tm, tf = TM_FFN, TF_FFN
    nf = dff // tf
    row = pl.BlockSpec((tm, D_MODEL), lambda i, f: (i, 0))
    vec = pl.BlockSpec((1, D_MODEL), lambda i, f: (0, 0))
    return pl.pallas_call(
        functools.partial(_ffn_kernel, nf=nf),
        grid=(m // tm, nf),
        in_specs=[row, row,
                  pl.BlockSpec((D_MODEL, tf), lambda i, f: (0, f)),
                  pl.BlockSpec((D_MODEL, tf), lambda i, f: (0, f)),
                  pl.BlockSpec((tf, D_MODEL), lambda i, f: (f, 0)),
                  vec, vec],
        out_specs=[row, row],
        out_shape=[jax.ShapeDtypeStruct((m, D_MODEL), f32), jax.ShapeDtypeStruct((m, D_MODEL), bf16)],
        scratch_shapes=[pltpu.VMEM((tm, D_MODEL), f32)],
        compiler_params=_cparams(("parallel", "arbitrary")),
        name="ffn",
    )(xb, x, wg, wu, wd, g, bb)


def _router_kernel(x_ref, r_ref, tri_ref, gate_ref, rank_ref, *, seq):
    tb = tri_ref.shape[0]
    x = x_ref[0]
    xh = x.astype(bf16)
    xl = (x - xh.astype(f32)).astype(bf16)
    r = r_ref[...]
    rh = r.astype(bf16)
    rl = (r - rh.astype(f32)).astype(bf16)
    logits = _dot(xh, rh) + _dot(xh, rl) + _dot(xl, rh)
    lane = lax.broadcasted_iota(i32, (seq, LANES), 1).astype(f32)
    low = -3.0e38
    lg = jnp.where(lane < N_EXPERTS, logits, low)
    m1 = jnp.max(lg, axis=-1, keepdims=True)
    i1 = jnp.min(jnp.where(lg == m1, lane, float(LANES)), axis=-1, keepdims=True)
    lg2 = jnp.where(lane == i1, low, lg)
    m2 = jnp.max(lg2, axis=-1, keepdims=True)
    i2 = jnp.min(jnp.where(lg2 == m2, lane, float(LANES)), axis=-1, keepdims=True)
    e2 = jnp.exp(m2 - m1)
    den = 1.0 + e2
    gate_ref[0] = jnp.where(lane == i1, 1.0 / den, jnp.where(lane == i2, e2 / den, 0.0))
    sel = (lane == i1) | (lane == i2)
    selb = jnp.where(sel, 1.0, 0.0).astype(bf16)
    carry = jnp.zeros((1, LANES), f32)
    for blk in range(seq // tb):
        sl = slice(blk * tb, (blk + 1) * tb)
        c = _dot(tri_ref[...], selb[sl]) + carry
        carry = c[tb - 1:tb, :]
        rank_ref[0, sl, :] = jnp.where(sel[sl], c - 1.0, -1.0)


def _router(x3, router_pad, tri):
    b, s, _ = x3.shape
    return pl.pallas_call(
        functools.partial(_router_kernel, seq=s),
        grid=(b,),
        in_specs=[pl.BlockSpec((1, s, D_MODEL), lambda i: (i, 0, 0)),
                  pl.BlockSpec((D_MODEL, LANES), lambda i: (0, 0)),
                  pl.BlockSpec(tri.shape, lambda i: (0, 0))],
        out_specs=[pl.BlockSpec((1, s, LANES), lambda i: (i, 0, 0)),
                   pl.BlockSpec((1, s, LANES), lambda i: (i, 0, 0))],
        out_shape=[jax.ShapeDtypeStruct((b, s, LANES), f32),
                   jax.ShapeDtypeStruct((b, s, LANES), f32)],
        compiler_params=_cparams(("parallel",)),
        name="router",
    )(x3, router_pad, tri)


def _moe_kernel(tot_ref, tot_al_ref, off_ref, npc_ref, xb_ref, rankt_ref, gatet_ref, wg_hbm, wu_hbm, wd_hbm,
                y_ref, xg_ref, acc_ref, wg_buf, wu_buf, wd_buf, sem, *, seq, nf, layer):
    tr = TR_MOE
    half = tr // 2
    nsubc = seq // SUB_MOE
    c = pl.program_id(0)
    e = pl.program_id(1)
    ce = c * N_EXPERTS + e
    last_step = pl.num_programs(0) * N_EXPERTS - 1

    def weight_copies(expert, f, slot):
        cols = pl.ds(f * TF_MOE, TF_MOE)
        return (pltpu.make_async_copy(wg_hbm.at[layer, expert, :, cols], wg_buf.at[slot], sem.at[0, slot]),
                pltpu.make_async_copy(wu_hbm.at[layer, expert, :, cols], wu_buf.at[slot], sem.at[1, slot]),
                pltpu.make_async_copy(wd_hbm.at[layer, expert, cols, :], wd_buf.at[slot], sem.at[2, slot]))

    def start_tile(expert, f, slot):
        for n, cp in enumerate(weight_copies(expert, f, slot)):
            cp.start(priority=n % 2)

    def wait_tile(expert, f, slot):
        for cp in weight_copies(expert, f, slot):
            cp.wait()
    tot = tot_ref[ce]
    rem = tot % tr
    nfull = tot // tr + jnp.where(rem > half, 1, 0)
    has_tail = (rem > 0) & (rem <= half)
    tail0 = pl.multiple_of(nfull * tr, half)

    def pieces(fn):
        for q in range(nsubc):
            base = off_ref[ce * nsubc + q]

            def body(p, carry, q=q, base=base):
                fn(q, pl.multiple_of(base + p * PIECE_MOE, ROW_ALIGN))
                return carry
            lax.fori_loop(0, npc_ref[ce * nsubc + q], body, 0)

    def onehot_rows(q, r0):
        rk = rankt_ref[0, pl.ds(e, 1), q * SUB_MOE:(q + 1) * SUB_MOE]
        want = (r0 + lax.broadcasted_iota(i32, (PIECE_MOE, SUB_MOE), 0)).astype(f32)
        return rk == want

    def gather(q, r0):
        p = jnp.where(onehot_rows(q, r0), 1.0, 0.0).astype(bf16)
        xg_ref[pl.ds(r0, PIECE_MOE), :] += _dot(p, xb_ref[0, q * SUB_MOE:(q + 1) * SUB_MOE, :]).astype(bf16)

    def clear(r0, nr):
        xg_ref[pl.ds(r0, nr), :] = jnp.zeros((nr, D_MODEL), bf16)
        acc_ref[pl.ds(r0, nr), :] = jnp.zeros((nr, D_MODEL), f32)

    def hidden(slot):
        def fn(r0, nr):
            xs = xg_ref[pl.ds(r0, nr), :]
            gate = _dot(xs, wg_buf[slot])
            up = _dot(xs, wu_buf[slot])
            act = (gate * _sigmoid(gate) * up).astype(bf16)
            acc_ref[pl.ds(r0, nr), :] += _dot(act, wd_buf[slot])
        return fn

    def combine(q, r0):
        grow = gatet_ref[0, pl.ds(e, 1), q * SUB_MOE:(q + 1) * SUB_MOE]
        hit = onehot_rows(q, r0)
        w = jnp.sum(jnp.where(hit, grow, 0.0), axis=-1, keepdims=True)
        z = (acc_ref[pl.ds(r0, PIECE_MOE), :] * w).astype(bf16)
        p = jnp.where(hit, 1.0, 0.0).astype(bf16)
        y_ref[0, q * SUB_MOE:(q + 1) * SUB_MOE, :] += lax.dot_general(
            p, z, (((0,), (0,)), ((), ())), preferred_element_type=f32)

    def tiles(fn):
        def body(s, carry):
            fn(pl.multiple_of(s * tr, tr), tr)
            return carry
        lax.fori_loop(0, nfull, body, 0)

        @pl.when(has_tail)
        def _():
            fn(tail0, half)

    def zero(j, carry):
        y_ref[0, pl.ds(j * tr, tr), :] = jnp.zeros((tr, D_MODEL), f32)
        return carry

    @pl.when(ce == 0)
    def _():
        start_tile(e, 0, 0)

    lax.fori_loop(0, jnp.where(e == 0, seq // tr, 0), zero, 0)
    tiles(clear)
    clear(pl.multiple_of(tot_al_ref[ce], ROW_ALIGN), tr)
    pieces(gather)

    next_e = (e + 1) % N_EXPERTS
    for f in range(nf):
        slot = f % 2
        wait_tile(e, f, slot)
        if f + 1 < nf:
            start_tile(e, f + 1, 1 - slot)
        else:
            @pl.when(ce < last_step)
            def _():
                start_tile(next_e, 0, 0)
        tiles(hidden(slot))

    pieces(combine)


def _moe(xb3, gate, rank, wg, wu, wd, layer):
    b, s, _ = xb3.shape
    dff = wg.shape[3]
    nf = dff // TF_MOE
    nsubc = s // SUB_MOE
    rk = rank[:, :, :N_EXPERTS]
    cnt = jnp.sum((rk >= 0.0).reshape(b, nsubc, SUB_MOE, N_EXPERTS), axis=2).astype(i32)
    first = jnp.cumsum(cnt, axis=1) - cnt
    off = first // ROW_ALIGN * ROW_ALIGN
    npc = jnp.where(cnt > 0, (first - off + cnt + PIECE_MOE - 1) // PIECE_MOE, 0)
    post = rk.transpose(0, 2, 1)
    gatet = gate[:, :, :N_EXPERTS].transpose(0, 2, 1)
    tot = jnp.sum(cnt, axis=1).reshape(-1)
    tot_al = (tot + ROW_ALIGN - 1) // ROW_ALIGN * ROW_ALIGN
    off_flat = off.transpose(0, 2, 1).reshape(-1)
    npc = npc.transpose(0, 2, 1).reshape(-1)
    buf_rows = s + ROW_ALIGN + PIECE_MOE + TR_MOE
    grid_spec = pltpu.PrefetchScalarGridSpec(
        num_scalar_prefetch=4,
        grid=(b, N_EXPERTS),
        in_specs=[pl.BlockSpec((1, s, D_MODEL), lambda c, e, *_: (c, 0, 0)),
                  pl.BlockSpec((1, N_EXPERTS, s), lambda c, e, *_: (c, 0, 0)),
                  pl.BlockSpec((1, N_EXPERTS, s), lambda c, e, *_: (c, 0, 0)),
                  pl.BlockSpec(memory_space=pl.ANY),
                  pl.BlockSpec(memory_space=pl.ANY),
                  pl.BlockSpec(memory_space=pl.ANY)],
        out_specs=pl.BlockSpec((1, s, D_MODEL), lambda c, e, *_: (c, 0, 0)),
        scratch_shapes=[pltpu.VMEM((buf_rows, D_MODEL), bf16), pltpu.VMEM((buf_rows, D_MODEL), f32),
                        pltpu.VMEM((2, D_MODEL, TF_MOE), bf16), pltpu.VMEM((2, D_MODEL, TF_MOE), bf16),
                        pltpu.VMEM((2, TF_MOE, D_MODEL), bf16), pltpu.SemaphoreType.DMA((3, 2))],
    )
    assert nf % 2 == 0
    return pl.pallas_call(
        functools.partial(_moe_kernel, seq=s, nf=nf, layer=layer),
        grid_spec=grid_spec,
        out_shape=jax.ShapeDtypeStruct((b, s, D_MODEL), f32),
        compiler_params=_cparams(("arbitrary", "arbitrary")),
        name="moe",
    )(tot, tot_al, off_flat, npc, xb3, post, gatet, wg, wu, wd)


def _resln_kernel(x_ref, y_ref, g_ref, b_ref, xo_ref, xbo_ref):
    xn = _layer_norm(DN_ALPHA * x_ref[...] + y_ref[...], g_ref[...], b_ref[...])
    xo_ref[...] = xn
    xbo_ref[...] = xn.astype(bf16)


def _resln(x, y, g, bb):
    m = x.shape[0]
    tm = TM_FFN
    row = pl.BlockSpec((tm, D_MODEL), lambda i: (i, 0))
    vec = pl.BlockSpec((1, D_MODEL), lambda i: (0, 0))
    return pl.pallas_call(
        _resln_kernel,
        grid=(m // tm,),
        in_specs=[row, row, vec, vec],
        out_specs=[row, row],
        out_shape=[jax.ShapeDtypeStruct((m, D_MODEL), f32), jax.ShapeDtypeStruct((m, D_MODEL), bf16)],
        compiler_params=_cparams(("parallel",)),
        name="resln",
    )(x, y, g, bb)


def _t5_bucket(dist):
    n = jnp.maximum(dist, 0)
    max_exact = REL_BUCKETS // 2
    large = max_exact + (jnp.log(jnp.maximum(n, 1).astype(f32) / max_exact)
                         / math.log(REL_MAX_DIST / max_exact) * (REL_BUCKETS - max_exact)).astype(i32)
    large = jnp.minimum(large, REL_BUCKETS - 1)
    return jnp.where(n < max_exact, n, large)


def _bias_of_dist(rel_bias, dist):
    onehot = (_t5_bucket(dist)[None] == jnp.arange(REL_BUCKETS).reshape((-1,) + (1,) * dist.ndim)).astype(f32)
    return LOG2E * jnp.einsum("kh,k...->h...", rel_bias.astype(f32), onehot, precision=lax.Precision.HIGHEST)


def _bias_tables(rel_bias, seq):
    t = T_ATT
    rows = NSA_GROUP * t
    ncmp = seq // CMP_STRIDE
    d0 = jnp.arange(t)[:, None] - jnp.arange(t)[None, :]
    offs = jnp.array([0, t, 2 * t, WINDOW]).reshape(4, 1, 1)
    kinds = _bias_of_dist(rel_bias, offs + d0[None])
    mask = jnp.stack([d0 >= 0, d0 == d0, d0 == d0, d0 < 0])
    kinds = jnp.where(mask[None], kinds, NEG_INF)
    order = np.array(NSA_ROW_ORDER)
    tbias = kinds.reshape(NSA_KV_HEADS, NSA_GROUP, 4, t, t)[:, order].transpose(0, 2, 1, 3, 4)
    tbias = tbias.reshape(NSA_KV_HEADS, 4, rows, t)
    cend = jnp.arange(ncmp) * CMP_STRIDE + CMP_BLOCK - 1
    dist_c = jnp.arange(seq)[:, None] - cend[None, :]
    bc = jnp.where(dist_c >= 0, _bias_of_dist(rel_bias, dist_c), NEG_INF)
    bc = bc.reshape(NSA_KV_HEADS, NSA_GROUP, seq // t, t, ncmp)[:, order].transpose(0, 2, 4, 1, 3)
    return tbias, bc.reshape(NSA_KV_HEADS, (seq // t) * ncmp, rows)


def _selection_constants(seq):
    ncmp = seq // CMP_STRIDE
    nsel = seq // SEL_BLOCK
    c0 = np.arange(ncmp)[:, None] * CMP_STRIDE
    s0 = np.arange(LANES)[None, :] * SEL_BLOCK
    ov = np.maximum(np.minimum(c0 + CMP_BLOCK, s0 + SEL_BLOCK) - np.maximum(c0, s0), 0) / CMP_BLOCK
    ov[ncmp - 1, :] = 0.0
    ov[:, nsel:] = 0.0
    et = (np.arange(seq)[:, None] // SEL_BLOCK == np.arange(LANES)[None, :]).astype(np.float32)
    eg = np.zeros((LANES, 3 * (NSA_GROUP // 2) * LANES), np.float32)
    for br in range(3):
        for g in range(NSA_GROUP):
            c0 = (br * (NSA_GROUP // 2) + g // 2) * LANES + (g % 2) * HEAD_DIM
            eg[br * NSA_GROUP + g, c0:c0 + HEAD_DIM] = 1.0
    return jnp.asarray(ov.T, bf16), jnp.asarray(et, bf16), jnp.asarray(eg, bf16)


def _fox_placement():
    xw = FOX_PAIRS * LANES
    pq = np.zeros((3, LANES, xw), np.float32)
    pk = np.zeros((3, LANES, xw), np.float32)
    oq = np.zeros((1, xw), np.float32)
    ok = np.zeros((1, xw), np.float32)
    for p in range(FOX_PAIRS):
        for hh in range(2):
            src = FGATE_LANE + 2 * p + hh
            base = p * LANES + hh * XCOLS
            for part in range(3):
                pk[part, src, base + part] = -1.0
                pq[part, src, base + 3 + part] = 1.0
                oq[0, base + part] = 1.0
                ok[0, base + 3 + part] = 1.0
    return jnp.asarray(pq, bf16), jnp.asarray(pk, bf16), jnp.asarray(oq), jnp.asarray(ok)


def _layer_weights(w_in, layer_pe, w1, w2, f_bias):
    offs = np.cumsum((NSA_Q_W, 6 * 2 * HEAD_DIM, 3 * NSA_HEADS, 3 * FOX_W, FOX_HEADS, 2 * D_MODEL))
    kv0, g0, fx0, ff0, mg0 = offs[0], offs[1], offs[2], offs[3], offs[4]
    scale = HEAD_DIM ** -0.5 * LOG2E
    kvw = NSA_KV_HEADS * HEAD_DIM
    w_kv = w_in[:, kv0:g0]
    w_kvdup = jnp.repeat(w_kv[:, 2 * kvw:].reshape(D_MODEL, 4 * NSA_KV_HEADS, 1, HEAD_DIM), 2, axis=2)
    w_kvdup = w_kvdup.reshape(D_MODEL, 8 * kvw)
    w_qkv = jnp.concatenate([w_in[:, :NSA_Q_W] * scale, w_kvdup, w_in[:, fx0:fx0 + FOX_W] * scale,
                             w_in[:, fx0 + FOX_W:ff0]], axis=1).astype(bf16)
    zeros = lambda n: jnp.zeros((D_MODEL, n), w_in.dtype)
    ng = 3 * NSA_GROUP
    w_g = w_in[:, g0:fx0].reshape(D_MODEL, NSA_KV_HEADS, NSA_GROUP, 3).transpose(0, 1, 3, 2)
    gate_cols = lambda h: w_g[:, h].reshape(D_MODEL, ng)
    w_aux = jnp.concatenate([w_kv[:, :2 * kvw],
                             gate_cols(0), zeros(FGATE_LANE - ng), w_in[:, ff0:mg0],
                             zeros(LANES - FGATE_LANE - FOX_HEADS),
                             gate_cols(1), zeros(LANES - ng)], axis=1).astype(bf16)
    fb_row = jnp.zeros((1, LANES), f32).at[0, FGATE_LANE:FGATE_LANE + FOX_HEADS].set(f_bias.astype(f32))
    pe2 = jnp.tile(layer_pe.astype(f32), (1, 1, NSA_KV_HEADS))
    w1r = w1.reshape(2, CMP_BLOCK, HEAD_DIM, CMP_HIDDEN).astype(bf16)
    zero = jnp.zeros_like(w1r)
    w1bd = jnp.concatenate([jnp.concatenate([w1r, zero], axis=-1),
                            jnp.concatenate([zero, w1r], axis=-1)], axis=-2)
    w2d = jnp.concatenate([w2, w2], axis=-1).astype(bf16)
    return (w_qkv, w_aux, w_in[:, mg0:].astype(bf16), fb_row, pe2, w1bd[:, :CMP_STRIDE], w1bd[:, CMP_STRIDE:],
            w2d)


def kernel(x, w_in, nsa_cmp_pe, nsa_cmp_w1, nsa_cmp_w2, fox_f_bias, w_nsa_branch, w_fox_branch, w_out,
           rel_bias, ln1_g, ln1_b, ln2_g, ln2_b, dense_w_gate, dense_w_up, dense_w_down, moe_router,
           moe_w_gate, moe_w_up, moe_w_down):
    b, s, d = x.shape
    m = b * s
    tbias, bias_c = _bias_tables(rel_bias, s)
    ov, et, eg = _selection_constants(s)
    pq, pk, oq, ok = _fox_placement()
    tri128 = jnp.asarray(np.tril(np.ones((LANES, LANES), np.float32)), bf16)
    tri256 = jnp.asarray(np.tril(np.ones((256, 256), np.float32)), bf16)

    moe_wg, moe_wu, moe_wd = moe_w_gate.astype(bf16), moe_w_up.astype(bf16), moe_w_down.astype(bf16)
    xf = x.reshape(m, d).astype(f32)
    xb = xf.astype(bf16)
    for layer in range(DEPTH):
        w_qkv, w_aux, w_mg, fb_row, pe2, w1t, w1b, w2d = _layer_weights(
            w_in[layer], nsa_cmp_pe[layer], nsa_cmp_w1[layer], nsa_cmp_w2[layer], fox_f_bias[layer])
        qkv, aux = _proj(xb, w_qkv, w_aux, b)
        aux = aux.reshape(b, s, AUX_W)
        gates, qx, kx = _gateprep(aux, fb_row, tri128, pq, pk, oq, ok)
        cmpkv = _compress(aux, pe2, w1t, w1b, w2d)
        y_nsa = _nsa(qkv, cmpkv, gates, bias_c, tbias, ov, et, eg).reshape(m, NSA_Q_W)
        y_fox = _fox(qkv, qx, kx).reshape(m, FOX_W)
        xf, xb = _merge(y_nsa, y_fox, xb, xf, w_nsa_branch[layer].astype(bf16),
                        w_fox_branch[layer].astype(bf16), w_mg, w_out[layer].astype(bf16),
                        ln1_g[layer].reshape(1, d), ln1_b[layer].reshape(1, d))
        j = layer // 2
        g2, b2 = ln2_g[layer].reshape(1, d), ln2_b[layer].reshape(1, d)
        if layer % 2 == 0:
            xf, xb = _ffn(xb, xf, dense_w_gate[j].astype(bf16), dense_w_up[j].astype(bf16),
                          dense_w_down[j].astype(bf16), g2, b2)
        else:
            router_pad = jnp.zeros((d, LANES), f32).at[:, :N_EXPERTS].set(moe_router[j].astype(f32))
            gate, rank = _router(xf.reshape(b, s, d), router_pad, tri256)
            y = _moe(xb.reshape(b, s, d), gate, rank, moe_wg, moe_wu, moe_wd, j)
            xf, xb = _resln(xf, y.reshape(m, d), g2, b2)
    return xf.reshape(b, s, d).astype(x.dtype)
```
